```python
import math
import jax, jax.numpy as jnp
from jax import lax
import numpy as np

D_MODEL = 1024
BATCH = 4
SEQ = 4096
DEPTH = 1

MEM_LEN = 256
POOL_WINDOWS = (2, 4, 8, 16)
POOL_GROUPS = 4
POOL_GROUP_DIM = D_MODEL // 8
POOL_DIM = POOL_GROUPS * POOL_GROUP_DIM
MLA_HEADS = 8
QK_NOPE_DIM = 128
QK_ROPE_DIM = 64
V_HEAD_DIM = 128
Q_LORA_RANK = 384
KV_LORA_RANK = 256
MLA_V_DIM = MLA_HEADS * V_HEAD_DIM
ROPE_THETA = 10000.0
Q_BLOCK = 128
XATTN_HEADS = 4
XATTN_HEAD_DIM = 128
XATTN_DIM = XATTN_HEADS * XATTN_HEAD_DIM
N_BRANCHES = 3
IN_SPLIT_SIZES = (POOL_DIM, Q_LORA_RANK, KV_LORA_RANK, QK_ROPE_DIM, XATTN_DIM, N_BRANCHES * D_MODEL)
D_IN = sum(IN_SPLIT_SIZES)
N_GROUPS = 4
EXPERTS_PER_GROUP = 8
N_EXPERTS = N_GROUPS * EXPERTS_PER_GROUP
TOP_K = 2
D_EXPERT = D_MODEL // 4
MOE_BLOCK = 128
RMS_EPS = 1e-6
NEG_INF = -1e30

kernel_name = "hybrid_pool_mla_memxattn_hmoe"


def rms_norm(x, g):
    xf = x.astype(jnp.float32)
    xf = xf * lax.rsqrt(jnp.mean(xf * xf, axis=-1, keepdims=True) + RMS_EPS)
    return xf.astype(x.dtype) * g


def rope_tables(positions):
    inv_freq = 1.0 / (ROPE_THETA ** (jnp.arange(0, QK_ROPE_DIM, 2, dtype=jnp.float32) / QK_ROPE_DIM))
    ang = positions.astype(jnp.float32)[..., None] * inv_freq
    return jnp.cos(ang), jnp.sin(ang)


def apply_rope(x, cos, sin):
    xf = x.astype(jnp.float32)
    half = xf.shape[-1] // 2
    x1, x2 = xf[..., :half], xf[..., half:]
    return jnp.concatenate([x1 * cos - x2 * sin, x2 * cos + x1 * sin], axis=-1).astype(x.dtype)


def pool_mixer(u, pool_w, pool_scale):
    B, S, _ = u.shape
    uf = u.astype(jnp.float32)
    cs = jnp.pad(jnp.cumsum(uf, axis=1), ((0, 0), (1, 0), (0, 0)))
    cnt_base = jnp.arange(S) + 1
    outs = []
    for g, w in enumerate(POOL_WINDOWS):
        sl = slice(g * POOL_GROUP_DIM, (g + 1) * POOL_GROUP_DIM)
        cg = cs[..., sl]
        lag = jnp.pad(cg[:, :S + 1 - w], ((0, 0), (w - 1, 0), (0, 0)))
        cnt = jnp.minimum(cnt_base, w).astype(jnp.float32)[None, :, None]
        outs.append((cg[:, 1:] - lag) / cnt - uf[..., sl])
    p = jnp.stack(outs, axis=2).astype(u.dtype)
    y = jnp.einsum('bsgc,gcd->bsgd', p, pool_w).reshape(B, S, POOL_DIM)
    return y * pool_scale


def causal_block_attention(q_nope, q_rope, k_nope, k_rope, v):
    B, S, H, _ = q_nope.shape
    nb = S // Q_BLOCK
    scale = (QK_NOPE_DIM + QK_ROPE_DIM) ** -0.5
    k_idx = jnp.arange(S)

    def one_block(args):
        qn, qr, i = args
        s = (jnp.einsum('bqhd,bkhd->bhqk', qn, k_nope)
             + jnp.einsum('bqhr,bkr->bhqk', qr, k_rope)).astype(jnp.float32) * scale
        q_idx = i * Q_BLOCK + jnp.arange(Q_BLOCK)
        s = jnp.where(k_idx[None, :] <= q_idx[:, None], s, NEG_INF)
        p = jax.nn.softmax(s, axis=-1).astype(v.dtype)
        return jnp.einsum('bhqk,bkhd->bqhd', p, v)

    qn_b = q_nope.reshape(B, nb, Q_BLOCK, H, -1).transpose(1, 0, 2, 3, 4)
    qr_b = q_rope.reshape(B, nb, Q_BLOCK, H, -1).transpose(1, 0, 2, 3, 4)
    out = lax.map(one_block, (qn_b, qr_b, jnp.arange(nb)))
    return out.transpose(1, 0, 2, 3, 4).reshape(B, S, H, V_HEAD_DIM)


def mla_branch(q_down, kv_down, k_rope_in, cos, sin, q_norm_g, w_uq, kv_norm_g, w_uk, w_uv):
    B, S, _ = q_down.shape
    c_q = rms_norm(q_down, q_norm_g)
    q = (c_q @ w_uq).reshape(B, S, MLA_HEADS, QK_NOPE_DIM + QK_ROPE_DIM)
    q_nope = q[..., :QK_NOPE_DIM]
    q_rope = apply_rope(q[..., QK_NOPE_DIM:], cos[:, :, None, :], sin[:, :, None, :])
    c_kv = rms_norm(kv_down, kv_norm_g)
    k_nope = (c_kv @ w_uk).reshape(B, S, MLA_HEADS, QK_NOPE_DIM)
    v = (c_kv @ w_uv).reshape(B, S, MLA_HEADS, V_HEAD_DIM)
    k_rope = apply_rope(k_rope_in, cos, sin)
    return causal_block_attention(q_nope, q_rope, k_nope, k_rope, v).reshape(B, S, MLA_V_DIM)


def memory_cross_attention(xq, mem_n, w_mem_kv):
    B, S, _ = xq.shape
    kv = mem_n @ w_mem_kv
    k = kv[..., :XATTN_DIM].reshape(B, -1, XATTN_HEADS, XATTN_HEAD_DIM)
    v = kv[..., XATTN_DIM:].reshape(B, -1, XATTN_HEADS, XATTN_HEAD_DIM)
    q = xq.reshape(B, S, XATTN_HEADS, XATTN_HEAD_DIM)
    s = jnp.einsum('bshd,bmhd->bhsm', q, k).astype(jnp.float32) * (XATTN_HEAD_DIM ** -0.5)
    p = jax.nn.softmax(s, axis=-1).astype(v.dtype)
    return jnp.einsum('bhsm,bmhd->bshd', p, v).reshape(B, S, XATTN_DIM)


def hierarchical_moe(h, w_rg, b_rg, w_re, b_re, w_gate_e, w_up_e, w_down_e):
    B, S, D = h.shape
    T = B * S
    xt = h.reshape(T, D)
    g_logits = (xt @ w_rg).astype(jnp.float32) + b_rg.astype(jnp.float32)
    g_probs = jax.nn.softmax(g_logits, axis=-1)
    g_idx = jnp.argmax(g_logits, axis=-1)
    p_g = jnp.take_along_axis(g_probs, g_idx[:, None], axis=1)
    e_logits = ((xt @ w_re).astype(jnp.float32) + b_re.astype(jnp.float32)).reshape(T, N_GROUPS, EXPERTS_PER_GROUP)
    e_sel = jnp.take_along_axis(e_logits, g_idx[:, None, None], axis=1)[:, 0]
    p_e = jax.nn.softmax(e_sel, axis=-1)
    top_p, top_i = lax.top_k(p_e, TOP_K)
    weights = p_g * top_p / jnp.sum(top_p, axis=-1, keepdims=True)
    expert_idx = g_idx[:, None] * EXPERTS_PER_GROUP + top_i

    A = T * TOP_K
    e_flat = expert_idx.reshape(A).astype(jnp.int32)
    tok_flat = jnp.repeat(jnp.arange(T, dtype=jnp.int32), TOP_K)
    w_flat = weights.reshape(A)
    order = jnp.argsort(e_flat)
    e_sorted = e_flat[order]
    counts = jnp.bincount(e_flat, length=N_EXPERTS)
    padded = ((counts + MOE_BLOCK - 1) // MOE_BLOCK) * MOE_BLOCK
    pad_end = jnp.cumsum(padded)
    pad_start = pad_end - padded
    start = jnp.cumsum(counts) - counts
    dest = pad_start[e_sorted] + (jnp.arange(A) - start[e_sorted])
    R = ((A + MOE_BLOCK - 1) // MOE_BLOCK) * MOE_BLOCK + N_EXPERTS * MOE_BLOCK
    n_blk = R // MOE_BLOCK
    row_tok = jnp.full((R,), T, dtype=jnp.int32).at[dest].set(tok_flat[order])
    row_w = jnp.zeros((R,), dtype=jnp.float32).at[dest].set(w_flat[order])
    blk_e = jnp.minimum(jnp.searchsorted(pad_end, jnp.arange(n_blk) * MOE_BLOCK, side='right'), N_EXPERTS - 1)
    x_pad = jnp.concatenate([xt, jnp.zeros((1, D), xt.dtype)], axis=0)

    def run_block(args):
        toks, e = args
        xb = x_pad[toks]
        a = jax.nn.silu(xb @ w_gate_e[e]) * (xb @ w_up_e[e])
        return a @ w_down_e[e]

    yb = lax.map(run_block, (row_tok.reshape(n_blk, MOE_BLOCK), blk_e))
    y = yb.reshape(R, D) * row_w[:, None].astype(yb.dtype)
    out = jax.ops.segment_sum(y, row_tok, num_segments=T + 1)[:T]
    return out.reshape(B, S, D)


def setup_inputs(seed: int = 0) -> dict:
    key = jax.random.key(seed)
    ks = jax.random.split(key, 32)
    f32 = jnp.float32

    def dense(k, shape, fan_in):
        return jax.random.normal(k, shape, f32) * (fan_in ** -0.5)

    def gain(k, shape):
        return 1.0 + 0.05 * jax.random.normal(k, shape, f32)

    L = DEPTH
    return {
        "x": jax.random.normal(ks[0], (BATCH, SEQ, D_MODEL), f32),
        "mem": jax.random.normal(ks[1], (BATCH, MEM_LEN, D_MODEL), f32),
        "positions": (jnp.arange(SEQ, dtype=jnp.int32)[None, :]
                      + jax.random.randint(ks[2], (BATCH, 1), 0, 1024, dtype=jnp.int32)),
        "mix_norm_g": gain(ks[3], (L, D_MODEL)),
        "w_in": dense(ks[4], (L, D_MODEL, D_IN), D_MODEL),
        "gate_b": 0.02 * jax.random.normal(ks[5], (L, N_BRANCHES, D_MODEL), f32),
        "q_norm_g": gain(ks[6], (L, Q_LORA_RANK)),
        "w_uq": dense(ks[7], (L, Q_LORA_RANK, MLA_HEADS * (QK_NOPE_DIM + QK_ROPE_DIM)), Q_LORA_RANK),
        "kv_norm_g": gain(ks[8], (L, KV_LORA_RANK)),
        "w_uk": dense(ks[9], (L, KV_LORA_RANK, MLA_HEADS * QK_NOPE_DIM), KV_LORA_RANK),
        "w_uv": dense(ks[10], (L, KV_LORA_RANK, MLA_V_DIM), KV_LORA_RANK),
        "pool_w": dense(ks[11], (L, POOL_GROUPS, POOL_GROUP_DIM, POOL_GROUP_DIM), POOL_GROUP_DIM),
        "pool_scale": gain(ks[12], (L, POOL_DIM)),
        "mem_norm_g": gain(ks[13], (L, D_MODEL)),
        "w_mem_kv": dense(ks[14], (L, D_MODEL, 2 * XATTN_DIM), D_MODEL),
        "w_br_pool": dense(ks[15], (L, POOL_DIM, D_MODEL), POOL_DIM),
        "w_br_mla": dense(ks[16], (L, MLA_V_DIM, D_MODEL), MLA_V_DIM),
        "w_br_mem": dense(ks[17], (L, XATTN_DIM, D_MODEL), XATTN_DIM),
        "w_out": dense(ks[18], (L, D_MODEL, D_MODEL), D_MODEL),
        "ffn_norm_g": gain(ks[19], (L, D_MODEL)),
        "w_router_group": dense(ks[20], (L, D_MODEL, N_GROUPS), D_MODEL),
        "b_router_group": 0.01 * jax.random.normal(ks[21], (L, N_GROUPS), f32),
        "w_router_expert": dense(ks[22], (L, D_MODEL, N_EXPERTS), D_MODEL),
        "b_router_expert": 0.01 * jax.random.normal(ks[23], (L, N_EXPERTS), f32),
        "w_gate_e": dense(ks[24], (L, N_EXPERTS, D_MODEL, D_EXPERT), D_MODEL),
        "w_up_e": dense(ks[25], (L, N_EXPERTS, D_MODEL, D_EXPERT), D_MODEL),
        "w_down_e": dense(ks[26], (L, N_EXPERTS, D_EXPERT, D_MODEL), D_EXPERT),
        "final_norm_g": gain(ks[27], (D_MODEL,)),
    }


def reference(x, mem, positions, mix_norm_g, w_in, gate_b, q_norm_g, w_uq, kv_norm_g, w_uk, w_uv,
              pool_w, pool_scale, mem_norm_g, w_mem_kv, w_br_pool, w_br_mla, w_br_mem, w_out,
              ffn_norm_g, w_router_group, b_router_group, w_router_expert, b_router_expert,
              w_gate_e, w_up_e, w_down_e, final_norm_g):
    B, S, D = x.shape
    cos, sin = rope_tables(positions)
    split_points = list(np.cumsum(IN_SPLIT_SIZES)[:-1])
    for l in range(DEPTH):
        h = rms_norm(x, mix_norm_g[l])
        proj = h @ w_in[l]
        u_pool, q_down, kv_down, k_rope_in, xq, gate_logits = jnp.split(proj, split_points, axis=-1)
        y_pool = pool_mixer(u_pool, pool_w[l], pool_scale[l])
        y_mla = mla_branch(q_down, kv_down, k_rope_in, cos, sin,
                           q_norm_g[l], w_uq[l], kv_norm_g[l], w_uk[l], w_uv[l])
        y_mem = memory_cross_attention(xq, rms_norm(mem, mem_norm_g[l]), w_mem_kv[l])
        gates = jax.nn.sigmoid(gate_logits.reshape(B, S, N_BRANCHES, D) + gate_b[l])
        merged = (gates[:, :, 0] * (y_pool @ w_br_pool[l])
                  + gates[:, :, 1] * (y_mla @ w_br_mla[l])
                  + gates[:, :, 2] * (y_mem @ w_br_mem[l]))
        x = x + merged @ w_out[l]
        h2 = rms_norm(x, ffn_norm_g[l])
        x = x + hierarchical_moe(h2, w_router_group[l], b_router_group[l], w_router_expert[l],
                                 b_router_expert[l], w_gate_e[l], w_up_e[l], w_down_e[l])
    return rms_norm(x, final_norm_g)
```

```python
import functools
import math

import jax
import jax.numpy as jnp
from jax import lax
from jax.experimental import pallas as pl
from jax.experimental.pallas import tpu as pltpu

D_MODEL = 1024
POOL_WINDOWS = (2, 4, 8, 16)
POOL_GROUP_DIM = 128
POOL_DIM = 512
MLA_HEADS = 8
QK_NOPE_DIM = 128
QK_ROPE_DIM = 64
V_HEAD_DIM = 128
Q_LORA_RANK = 384
KV_LORA_RANK = 256
ROPE_THETA = 10000.0
XATTN_HEADS = 4
XATTN_HEAD_DIM = 128
XATTN_DIM = 512
N_BRANCHES = 3
N_GROUPS = 4
EXPERTS_PER_GROUP = 8
N_EXPERTS = 32
D_EXPERT = 256
RMS_EPS = 1e-6
NEG_INF = -1e30

LANES = 128
QK_PAD_DIM = 2 * LANES
POOL_HALO = 16
MOE_ROWS = 256
ROUTER_ROWS = 40
META_LANES = 256
META_PAD_END = 192
META_NACT = 255
VMEM_LIMIT_BYTES = 56 * 1024 * 1024

IN_POOL, IN_QD, IN_KV, IN_XQ, IN_GATE, IN_KR, IN_END = 0, 512, 896, 1152, 1664, 4736, 4864

F32 = jnp.float32
BF16 = jnp.bfloat16


def _rms(x, g):
    ms = jnp.mean(x * x, axis=-1, keepdims=True)
    return (x * lax.rsqrt(ms + RMS_EPS)) * g


def _dot(a, b):
    return jnp.dot(a, b, preferred_element_type=F32)


def _dot_nt(a, b):
    return lax.dot_general(a, b, (((1,), (1,)), ((), ())), preferred_element_type=F32)


def _const_spec(shape):
    nd = len(shape)
    return pl.BlockSpec(shape, lambda *_: (0,) * nd, pipeline_mode=pl.Buffered(1))


def _mixer_in_body(x_ref, pos_ref, invf_ref, mixg_ref, win_ref, gateb_ref, qg_ref, wuq_ref,
                   kvg_ref, wuk_ref, wuv_ref, poolw_ref, pools_ref,
                   ypool_ref, xq_ref, gates_ref, q_ref, k_ref, v_ref, ext_ref,
                   *, tm, tiles_per_seq, q_scale):
    si = lax.rem(pl.program_id(0), tiles_per_seq)
    hb = _rms(x_ref[...], mixg_ref[...]).astype(BF16)

    u = _dot(hb, win_ref[:, IN_POOL:IN_QD])

    @pl.when(si == 0)
    def _():
        ext_ref[0:POOL_HALO, :] = jnp.zeros((POOL_HALO, POOL_DIM), F32)

    ext_ref[POOL_HALO:POOL_HALO + tm, :] = u
    t_seq = lax.broadcasted_iota(jnp.int32, (tm, 1), 0) + si * tm
    for g, w in enumerate(POOL_WINDOWS):
        lo = g * POOL_GROUP_DIM
        hi = lo + POOL_GROUP_DIM
        acc = u[:, lo:hi]
        for j in range(1, w):
            acc = acc + ext_ref[POOL_HALO - j:POOL_HALO - j + tm, lo:hi]
        cnt = jnp.minimum(t_seq + 1, w).astype(F32)
        p = acc / cnt - u[:, lo:hi]
        y = _dot(p.astype(BF16), poolw_ref[g]) * pools_ref[:, lo:hi]
        ypool_ref[:, lo:hi] = y.astype(BF16)
    ext_ref[0:POOL_HALO, :] = ext_ref[tm:tm + POOL_HALO, :]

    ang = pos_ref[...].astype(F32) * invf_ref[...]
    cos = jnp.cos(ang)
    sin = jnp.sin(ang)
    first_half = lax.broadcasted_iota(jnp.int32, (tm, LANES), 1) < (QK_ROPE_DIM // 2)
    sin_signed = jnp.where(first_half, -sin, sin)

    def rope(r):
        swapped = jnp.where(first_half, pltpu.roll(r, LANES - QK_ROPE_DIM // 2, 1),
                            pltpu.roll(r, QK_ROPE_DIM // 2, 1))
        return r * cos + swapped * sin_signed

    cq = _rms(_dot(hb, win_ref[:, IN_QD:IN_KV]), qg_ref[...]).astype(BF16)
    for h in range(MLA_HEADS):
        qh = _dot(cq, wuq_ref[:, h * QK_PAD_DIM:(h + 1) * QK_PAD_DIM])
        q_ref[0, h, :, 0:LANES] = (qh[:, 0:LANES] * q_scale).astype(BF16)
        q_ref[0, h, :, LANES:QK_PAD_DIM] = (rope(qh[:, LANES:QK_PAD_DIM]) * q_scale).astype(BF16)

    ckv = _rms(_dot(hb, win_ref[:, IN_KV:IN_XQ]), kvg_ref[...]).astype(BF16)
    kr = rope(_dot(hb, win_ref[:, IN_KR:IN_END])).astype(BF16)
    for hp in range(MLA_HEADS // 2):
        cols = slice(hp * 2 * LANES, (hp + 1) * 2 * LANES)
        kn = _dot(ckv, wuk_ref[:, cols]).astype(BF16)
        vv = _dot(ckv, wuv_ref[:, cols]).astype(BF16)
        for j in range(2):
            h = 2 * hp + j
            k_ref[0, h, :, 0:LANES] = kn[:, j * LANES:(j + 1) * LANES]
            k_ref[0, h, :, LANES:QK_PAD_DIM] = kr
            v_ref[0, h] = vv[:, j * LANES:(j + 1) * LANES]

    xq_ref[...] = _dot(hb, win_ref[:, IN_XQ:IN_GATE]).astype(BF16)
    for c in range(N_BRANCHES):
        gl = _dot(hb, win_ref[:, IN_GATE + c * D_MODEL:IN_GATE + (c + 1) * D_MODEL])
        gates_ref[:, c * D_MODEL:(c + 1) * D_MODEL] = jax.nn.sigmoid(gl + gateb_ref[c:c + 1, :]).astype(BF16)


def _mixer_in(x2, pos2, invf, mixg, win_p, gate_b, qg, wuq_p, kvg, wuk, wuv, pool_w, pool_s, *, B, S, tm):
    T = B * S
    tps = S // tm
    q_scale = (QK_NOPE_DIM + QK_ROPE_DIM) ** -0.5 * math.log2(math.e)
    body = functools.partial(_mixer_in_body, tm=tm, tiles_per_seq=tps, q_scale=q_scale)
    row = lambda i: (i, 0)
    head = lambda i: (i // tps, 0, i % tps, 0)
    return pl.pallas_call(
        body,
        grid=(T // tm,),
        in_specs=[
            pl.BlockSpec((tm, D_MODEL), row),
            pl.BlockSpec((tm, 1), row),
            _const_spec((1, LANES)),
            _const_spec((1, D_MODEL)),
            _const_spec((D_MODEL, IN_END)),
            _const_spec((N_BRANCHES, D_MODEL)),
            _const_spec((1, Q_LORA_RANK)),
            _const_spec((Q_LORA_RANK, MLA_HEADS * QK_PAD_DIM)),
            _const_spec((1, KV_LORA_RANK)),
            _const_spec((KV_LORA_RANK, MLA_HEADS * QK_NOPE_DIM)),
            _const_spec((KV_LORA_RANK, MLA_HEADS * V_HEAD_DIM)),
            _const_spec((len(POOL_WINDOWS), POOL_GROUP_DIM, POOL_GROUP_DIM)),
            _const_spec((1, POOL_DIM)),
        ],
        out_specs=[
            pl.BlockSpec((tm, POOL_DIM), row),
            pl.BlockSpec((tm, XATTN_DIM), row),
            pl.BlockSpec((tm, N_BRANCHES * D_MODEL), row),
            pl.BlockSpec((1, MLA_HEADS, tm, QK_PAD_DIM), head),
            pl.BlockSpec((1, MLA_HEADS, tm, QK_PAD_DIM), head),
            pl.BlockSpec((1, MLA_HEADS, tm, V_HEAD_DIM), head),
        ],
        out_shape=[
            jax.ShapeDtypeStruct((T, POOL_DIM), BF16),
            jax.ShapeDtypeStruct((T, XATTN_DIM), BF16),
            jax.ShapeDtypeStruct((T, N_BRANCHES * D_MODEL), BF16),
            jax.ShapeDtypeStruct((B, MLA_HEADS, S, QK_PAD_DIM), BF16),
            jax.ShapeDtypeStruct((B, MLA_HEADS, S, QK_PAD_DIM), BF16),
            jax.ShapeDtypeStruct((B, MLA_HEADS, S, V_HEAD_DIM), BF16),
        ],
        scratch_shapes=[pltpu.VMEM((tm + POOL_HALO, POOL_DIM), F32)],
        compiler_params=pltpu.CompilerParams(dimension_semantics=("arbitrary",),
                                             vmem_limit_bytes=VMEM_LIMIT_BYTES),
        name="mixer_in",
    )(x2, pos2, invf, mixg, win_p, gate_b, qg, wuq_p, kvg, wuk, wuv, pool_w, pool_s)


def _mem_kv_body(mem_ref, g_ref, w_ref, k_ref, v_ref):
    mb = _rms(mem_ref[...], g_ref[...]).astype(BF16)
    kv = _dot(mb, w_ref[...])
    k_ref[...] = kv[:, 0:XATTN_DIM].astype(BF16)
    v_ref[...] = kv[:, XATTN_DIM:2 * XATTN_DIM].astype(BF16)


def _mem_kv(mem2, g, w):
    rows = mem2.shape[0]
    tr = min(rows, 512)
    return pl.pallas_call(
        _mem_kv_body,
        grid=(rows // tr,),
        in_specs=[pl.BlockSpec((tr, D_MODEL), lambda i: (i, 0)),
                  _const_spec((1, D_MODEL)),
                  _const_spec((D_MODEL, 2 * XATTN_DIM))],
        out_specs=[pl.BlockSpec((tr, XATTN_DIM), lambda i: (i, 0)),
                   pl.BlockSpec((tr, XATTN_DIM), lambda i: (i, 0))],
        out_shape=[jax.ShapeDtypeStruct((rows, XATTN_DIM), BF16),
                   jax.ShapeDtypeStruct((rows, XATTN_DIM), BF16)],
        compiler_params=pltpu.CompilerParams(dimension_semantics=("arbitrary",)),
        name="mem_kv",
    )(mem2, g, w)


def _attn_body(q_ref, k_ref, v_ref, o_ref, m_ref, l_ref, acc_ref, *, tq):
    qi = pl.program_id(2)
    q = q_ref[0, 0]
    m_ref[...] = jnp.full((tq, 1), NEG_INF, F32)
    l_ref[...] = jnp.zeros((tq, 1), F32)
    acc_ref[...] = jnp.zeros((tq, V_HEAD_DIM), F32)

    def step(kb, diagonal):
        start = pl.multiple_of(kb * tq, tq)
        k = k_ref[0, 0, pl.ds(start, tq), :]
        v = v_ref[0, 0, pl.ds(start, tq), :]
        s = _dot_nt(q, k)
        if diagonal:
            r = lax.broadcasted_iota(jnp.int32, (tq, tq), 0)
            c = lax.broadcasted_iota(jnp.int32, (tq, tq), 1)
            s = jnp.where(c <= r, s, NEG_INF)
        m_prev = m_ref[...]
        m_new = jnp.maximum(m_prev, jnp.max(s, axis=1, keepdims=True))
        alpha = jnp.exp2(m_prev - m_new)
        p = jnp.exp2(s - m_new)
        l_ref[...] = alpha * l_ref[...] + jnp.sum(p, axis=1, keepdims=True)
        acc_ref[...] = alpha * acc_ref[...] + _dot(p.astype(BF16), v)
        m_ref[...] = m_new

    def loop_body(kb, carry):
        step(kb, False)
        return carry

    lax.fori_loop(0, qi, loop_body, 0)
    step(qi, True)
    o_ref[0] = (acc_ref[...] / l_ref[...]).astype(BF16)


def _mla_attn(q, k, v, *, tq):
    B, H, S, _ = q.shape
    return pl.pallas_call(
        functools.partial(_attn_body, tq=tq),
        grid=(B, H, S // tq),
        in_specs=[pl.BlockSpec((1, 1, tq, QK_PAD_DIM), lambda b, h, i: (b, h, i, 0)),
                  pl.BlockSpec((1, 1, S, QK_PAD_DIM), lambda b, h, i: (b, h, 0, 0)),
                  pl.BlockSpec((1, 1, S, V_HEAD_DIM), lambda b, h, i: (b, h, 0, 0))],
        out_specs=pl.BlockSpec((1, tq, V_HEAD_DIM), lambda b, h, i: (b, i, h)),
        out_shape=jax.ShapeDtypeStruct((B, S, H * V_HEAD_DIM), BF16),
        scratch_shapes=[pltpu.VMEM((tq, 1), F32), pltpu.VMEM((tq, 1), F32),
                        pltpu.VMEM((tq, V_HEAD_DIM), F32)],
        compiler_params=pltpu.CompilerParams(
            dimension_semantics=("arbitrary", "arbitrary", "arbitrary"),
            vmem_limit_bytes=VMEM_LIMIT_BYTES),
        name="mla_attn",
    )(q, k, v)


def _merge_body(x_ref, ypool_ref, ymla_ref, xq_ref, gates_ref, kmem_ref, vmem_ref,
                wbp_ref, wbm_ref, wbx_ref, wout_ref, ffng_ref, wr_ref, br_ref,
                x1_ref, h2_ref, eidx_ref, wts_ref, rank_ref, counts_ref, carry_ref, *, tm):
    @pl.when(pl.program_id(0) == 0)
    def _():
        carry_ref[...] = jnp.zeros((N_EXPERTS, LANES), F32)

    xq = xq_ref[...]
    parts = []
    for h in range(XATTN_HEADS):
        cols = slice(h * XATTN_HEAD_DIM, (h + 1) * XATTN_HEAD_DIM)
        s = _dot_nt(xq[:, cols], kmem_ref[:, cols]) * (XATTN_HEAD_DIM ** -0.5)
        e = jnp.exp(s - jnp.max(s, axis=1, keepdims=True))
        p = e / jnp.sum(e, axis=1, keepdims=True)
        parts.append(_dot(p.astype(BF16), vmem_ref[:, cols]))
    ymem = jnp.concatenate(parts, axis=1).astype(BF16)

    gates = gates_ref[...].astype(F32)
    merged = (gates[:, 0:D_MODEL] * _dot(ypool_ref[...], wbp_ref[...])
              + gates[:, D_MODEL:2 * D_MODEL] * _dot(ymla_ref[...], wbm_ref[...])
              + gates[:, 2 * D_MODEL:3 * D_MODEL] * _dot(ymem, wbx_ref[...]))
    x1 = x_ref[...] + _dot(merged.astype(BF16), wout_ref[...])
    x1_ref[...] = x1
    h2 = _rms(x1, ffng_ref[...])
    h2_ref[...] = h2

    lt = _dot_nt(wr_ref[...], h2.astype(BF16)) + br_ref[...]
    gl = lt[N_EXPERTS:N_EXPERTS + N_GROUPS, :]
    gmax = jnp.max(gl, axis=0, keepdims=True)
    r4 = lax.broadcasted_iota(jnp.int32, (N_GROUPS, tm), 0).astype(F32)
    gidx = jnp.min(jnp.where(gl == gmax, r4, float(N_GROUPS)), axis=0, keepdims=True)
    pg = 1.0 / jnp.sum(jnp.exp(gl - gmax), axis=0, keepdims=True)
    esel = lt[0:EXPERTS_PER_GROUP, :]
    for g in range(1, N_GROUPS):
        esel = jnp.where(gidx == float(g), lt[g * EXPERTS_PER_GROUP:(g + 1) * EXPERTS_PER_GROUP, :], esel)
    r8 = lax.broadcasted_iota(jnp.int32, (EXPERTS_PER_GROUP, tm), 0).astype(F32)
    m1 = jnp.max(esel, axis=0, keepdims=True)
    i1 = jnp.min(jnp.where(esel == m1, r8, float(EXPERTS_PER_GROUP)), axis=0, keepdims=True)
    rest = jnp.where(r8 == i1, -jnp.inf, esel)
    m2 = jnp.max(rest, axis=0, keepdims=True)
    i2 = jnp.min(jnp.where(rest == m2, r8, float(EXPERTS_PER_GROUP)), axis=0, keepdims=True)
    e2 = jnp.exp(m2 - m1)
    den = 1.0 + e2
    wts_ref[0:1, :] = pg / den
    wts_ref[1:2, :] = pg * e2 / den
    ex1 = gidx * float(EXPERTS_PER_GROUP) + i1
    ex2 = gidx * float(EXPERTS_PER_GROUP) + i2
    eidx_ref[0:1, :] = ex1.astype(jnp.int32)
    eidx_ref[1:2, :] = ex2.astype(jnp.int32)

    r32 = lax.broadcasted_iota(jnp.int32, (N_EXPERTS, tm), 0).astype(F32)
    is1 = r32 == ex1
    is2 = r32 == ex2
    member = jnp.where(is1 | is2, 1.0, 0.0)
    upper = jnp.where(lax.broadcasted_iota(jnp.int32, (tm, tm), 0)
                      <= lax.broadcasted_iota(jnp.int32, (tm, tm), 1), 1.0, 0.0).astype(BF16)
    incl = _dot(member.astype(BF16), upper)
    carry = carry_ref[:, 0:1]
    excl = incl - member + carry
    rank_ref[0:1, :] = jnp.sum(jnp.where(is1, excl, 0.0), axis=0, keepdims=True).astype(jnp.int32)
    rank_ref[1:2, :] = jnp.sum(jnp.where(is2, excl, 0.0), axis=0, keepdims=True).astype(jnp.int32)
    total = carry + jnp.sum(member, axis=1, keepdims=True)
    carry_ref[...] = jnp.broadcast_to(total, (N_EXPERTS, LANES))
    counts_ref[...] = jnp.broadcast_to(total, (N_EXPERTS, LANES)).astype(jnp.int32)


def _merge(x2, ypool, ymla, xq, gates, kmem, vmem, wbp, wbm, wbx, wout, ffng, wr, br, *, B, S, tm, mem_len):
    T = B * S
    tps = S // tm
    row = lambda i: (i, 0)
    lane = lambda i: (0, i)
    memb = lambda i: (i // tps, 0)
    return pl.pallas_call(
        functools.partial(_merge_body, tm=tm),
        grid=(T // tm,),
        in_specs=[
            pl.BlockSpec((tm, D_MODEL), row),
            pl.BlockSpec((tm, POOL_DIM), row),
            pl.BlockSpec((tm, MLA_HEADS * V_HEAD_DIM), row),
            pl.BlockSpec((tm, XATTN_DIM), row),
            pl.BlockSpec((tm, N_BRANCHES * D_MODEL), row),
            pl.BlockSpec((mem_len, XATTN_DIM), memb),
            pl.BlockSpec((mem_len, XATTN_DIM), memb),
            _const_spec((POOL_DIM, D_MODEL)),
            _const_spec((MLA_HEADS * V_HEAD_DIM, D_MODEL)),
            _const_spec((XATTN_DIM, D_MODEL)),
            _const_spec((D_MODEL, D_MODEL)),
            _const_spec((1, D_MODEL)),
            _const_spec((ROUTER_ROWS, D_MODEL)),
            _const_spec((ROUTER_ROWS, 1)),
        ],
        out_specs=[
            pl.BlockSpec((tm, D_MODEL), row),
            pl.BlockSpec((tm, D_MODEL), row),
            pl.BlockSpec((2, tm), lane),
            pl.BlockSpec((2, tm), lane),
            pl.BlockSpec((2, tm), lane),
            pl.BlockSpec((N_EXPERTS, LANES), lambda i: (0, 0)),
        ],
        out_shape=[
            jax.ShapeDtypeStruct((T, D_MODEL), F32),
            jax.ShapeDtypeStruct((T, D_MODEL), F32),
            jax.ShapeDtypeStruct((2, T), jnp.int32),
            jax.ShapeDtypeStruct((2, T), F32),
            jax.ShapeDtypeStruct((2, T), jnp.int32),
            jax.ShapeDtypeStruct((N_EXPERTS, LANES), jnp.int32),
        ],
        scratch_shapes=[pltpu.VMEM((N_EXPERTS, LANES), F32)],
        compiler_params=pltpu.CompilerParams(dimension_semantics=("arbitrary",),
                                             vmem_limit_bytes=VMEM_LIMIT_BYTES),
        name="merge",
    )(x2, ypool, ymla, xq, gates, kmem, vmem, wbp, wbm, wbx, wout, ffng, wr, br)


def _moe_pos_body(counts_ref, eidx_ref, rank_ref, dest_ref, meta_ref):
    shift = int(math.log2(MOE_ROWS))
    cnt = counts_ref[...]
    padded = lax.shift_left(lax.shift_right_logical(cnt + (MOE_ROWS - 1), shift), shift)
    r32 = lax.broadcasted_iota(jnp.int32, (N_EXPERTS, LANES), 0)
    pad_start = jnp.zeros((N_EXPERTS, LANES), jnp.int32)
    for e in range(N_EXPERTS - 1):
        pad_start = pad_start + jnp.where(r32 > e, padded[e:e + 1, :], 0)
    pad_end = pad_start + padded

    eidx = eidx_ref[...]
    dest = rank_ref[...]
    for e in range(N_EXPERTS):
        dest = dest + jnp.where(eidx == e, pad_start[e:e + 1, 0:1], 0)
    dest_ref[...] = dest

    lane = lax.broadcasted_iota(jnp.int32, (1, META_LANES), 1)
    block_row = lane * MOE_ROWS
    blk_e = jnp.zeros((1, META_LANES), jnp.int32)
    pe_row = jnp.zeros((1, META_LANES), jnp.int32)
    for e in range(N_EXPERTS):
        pe = pad_end[e:e + 1, 0:1]
        blk_e = blk_e + jnp.where(pe <= block_row, 1, 0)
        pe_row = pe_row + jnp.where(lane == META_PAD_END + e, pe, 0)
    blk_e = jnp.minimum(blk_e, N_EXPERTS - 1)
    nact = lax.shift_right_logical(pad_end[N_EXPERTS - 1:N_EXPERTS, 0:1], shift)
    meta = jnp.where(lane < META_PAD_END, blk_e, pe_row)
    meta_ref[...] = jnp.where(lane == META_NACT, nact, meta)


def _moe_pos(counts, eidx, rank):
    T = eidx.shape[1]
    full = lambda shape: pl.BlockSpec(shape, lambda i: (0,) * len(shape))
    return pl.pallas_call(
        _moe_pos_body,
        grid=(1,),
        in_specs=[full((N_EXPERTS, LANES)), full((2, T)), full((2, T))],
        out_specs=[full((2, T)), full((1, META_LANES))],
        out_shape=[jax.ShapeDtypeStruct((2, T), jnp.int32),
                   jax.ShapeDtypeStruct((1, META_LANES), jnp.int32)],
        compiler_params=pltpu.CompilerParams(dimension_semantics=("arbitrary",)),
        name="moe_pos",
    )(counts, eidx, rank)


def _dispatch_body(meta_ref, dest_ref, h2_ref, xs_ref, zero_ref, sem, zsem, *, td, n_blocks):
    def pad_copy(e):
        end = pl.multiple_of(meta_ref[META_PAD_END + e], MOE_ROWS)
        return pltpu.make_async_copy(zero_ref, xs_ref.at[pl.ds(end - MOE_ROWS, MOE_ROWS)], zsem)

    def has_rows(e):
        prev = jnp.where(e == 0, 0, meta_ref[META_PAD_END + jnp.maximum(e - 1, 0)])
        return meta_ref[META_PAD_END + e] > prev

    @pl.when(pl.program_id(0) == 0)
    def _():
        zero_ref[...] = jnp.zeros((MOE_ROWS, D_MODEL), F32)

        def start(e, c):
            @pl.when(has_rows(e))
            def _():
                pad_copy(e).start()
            return c

        def wait(e, c):
            @pl.when(has_rows(e))
            def _():
                pad_copy(e).wait()
            return c

        def tail_copy(b):
            return pltpu.make_async_copy(
                zero_ref, xs_ref.at[pl.ds(pl.multiple_of(b * MOE_ROWS, MOE_ROWS), MOE_ROWS)], zsem)

        def tail_start(b, c):
            tail_copy(b).start()
            return c

        def tail_wait(b, c):
            tail_copy(b).wait()
            return c

        nact = meta_ref[META_NACT]
        lax.fori_loop(0, N_EXPERTS, start, 0)
        lax.fori_loop(nact, n_blocks, tail_start, 0)
        lax.fori_loop(0, N_EXPERTS, wait, 0)
        lax.fori_loop(nact, n_blocks, tail_wait, 0)

    def row_copy(i, slot):
        return pltpu.make_async_copy(h2_ref.at[pl.ds(i, 1)], xs_ref.at[pl.ds(dest_ref[slot, i], 1)], sem)

    def issue(i, c):
        row_copy(i, 0).start()
        row_copy(i, 1).start()
        return c

    lax.fori_loop(0, td, issue, 0, unroll=8)
    for _ in range(2):
        pltpu.make_async_copy(h2_ref, xs_ref.at[pl.ds(0, td)], sem).wait()


def _dispatch(meta1, dest, h2, *, R, td):
    T = h2.shape[0]
    return pl.pallas_call(
        functools.partial(_dispatch_body, td=td, n_blocks=R // MOE_ROWS),
        grid_spec=pltpu.PrefetchScalarGridSpec(
            num_scalar_prefetch=1,
            grid=(T // td,),
            in_specs=[pl.BlockSpec((2, td), lambda i, m: (0, i), memory_space=pltpu.SMEM),
                      pl.BlockSpec((td, D_MODEL), lambda i, m: (i, 0))],
            out_specs=pl.BlockSpec(memory_space=pl.ANY),
            scratch_shapes=[pltpu.VMEM((MOE_ROWS, D_MODEL), F32),
                            pltpu.SemaphoreType.DMA, pltpu.SemaphoreType.DMA],
        ),
        out_shape=jax.ShapeDtypeStruct((R, D_MODEL), F32),
        compiler_params=pltpu.CompilerParams(dimension_semantics=("arbitrary",)),
        name="dispatch",
    )(meta1, dest, h2)


def _moe_ffn_body(meta_ref, xs_ref, wgu_ref, wd_ref, ys_ref):
    active = pl.program_id(0) < meta_ref[META_NACT]

    @pl.when(active)
    def _():
        gu = _dot(xs_ref[...].astype(BF16), wgu_ref[0])
        g = gu[:, 0:D_EXPERT]
        a = (g * jax.nn.sigmoid(g)) * gu[:, D_EXPERT:2 * D_EXPERT]
        ys_ref[...] = _dot(a.astype(BF16), wd_ref[0])

    @pl.when(jnp.logical_not(active))
    def _():
        ys_ref[...] = jnp.zeros((MOE_ROWS, D_MODEL), F32)


def _moe_ffn(meta1, xs, wgu, wd):
    R = xs.shape[0]
    blk = lambda b, m: (jnp.minimum(b, m[META_NACT] - 1), 0)
    wsel = lambda b, m: (m[jnp.minimum(b, m[META_NACT] - 1)], 0, 0)
    return pl.pallas_call(
        _moe_ffn_body,
        grid_spec=pltpu.PrefetchScalarGridSpec(
            num_scalar_prefetch=1,
            grid=(R // MOE_ROWS,),
            in_specs=[pl.BlockSpec((MOE_ROWS, D_MODEL), blk),
                      pl.BlockSpec((1, D_MODEL, 2 * D_EXPERT), wsel),
                      pl.BlockSpec((1, D_EXPERT, D_MODEL), wsel)],
            out_specs=pl.BlockSpec((MOE_ROWS, D_MODEL), lambda b, m: (b, 0)),
        ),
        out_shape=jax.ShapeDtypeStruct((R, D_MODEL), F32),
        compiler_params=pltpu.CompilerParams(dimension_semantics=("arbitrary",)),
        name="moe_ffn",
    )(meta1, xs, wgu, wd)


def _combine_body(dest_ref, x1_ref, w_ref, fg_ref, ys_ref, out_ref, ya_ref, yb_ref, sem, *, tc):
    def row_copy(i, slot, buf):
        return pltpu.make_async_copy(ys_ref.at[pl.ds(dest_ref[slot, i], 1)], buf.at[pl.ds(i, 1)], sem)

    def issue(i, c):
        row_copy(i, 0, ya_ref).start()
        row_copy(i, 1, yb_ref).start()
        return c

    lax.fori_loop(0, tc, issue, 0, unroll=8)
    pltpu.make_async_copy(ys_ref.at[pl.ds(0, tc)], ya_ref, sem).wait()
    pltpu.make_async_copy(ys_ref.at[pl.ds(0, tc)], yb_ref, sem).wait()
    w = w_ref[...]
    y = x1_ref[...] + w[:, 0:1] * ya_ref[...] + w[:, 1:2] * yb_ref[...]
    out_ref[...] = _rms(y, fg_ref[...])


def _combine(dest, x1, wts_t, fg, ys, *, tc):
    T = x1.shape[0]
    return pl.pallas_call(
        functools.partial(_combine_body, tc=tc),
        grid=(T // tc,),
        in_specs=[pl.BlockSpec((2, tc), lambda i: (0, i), memory_space=pltpu.SMEM),
                  pl.BlockSpec((tc, D_MODEL), lambda i: (i, 0)),
                  pl.BlockSpec((tc, 2), lambda i: (i, 0)),
                  _const_spec((1, D_MODEL)),
                  pl.BlockSpec(memory_space=pl.ANY)],
        out_specs=pl.BlockSpec((tc, D_MODEL), lambda i: (i, 0)),
        out_shape=jax.ShapeDtypeStruct((T, D_MODEL), F32),
        scratch_shapes=[pltpu.VMEM((tc, D_MODEL), F32), pltpu.VMEM((tc, D_MODEL), F32),
                        pltpu.SemaphoreType.DMA],
        compiler_params=pltpu.CompilerParams(dimension_semantics=("arbitrary",)),
        name="combine",
    )(dest, x1, wts_t, fg, ys)


def _tile(n, t):
    t = min(n, t)
    assert n % t == 0, (n, t)
    return t


def kernel(x, mem, positions, mix_norm_g, w_in, gate_b, q_norm_g, w_uq, kv_norm_g, w_uk, w_uv, pool_w, pool_scale, mem_norm_g, w_mem_kv, w_br_pool, w_br_mla, w_br_mem, w_out, ffn_norm_g, w_router_group, b_router_group, w_router_expert, b_router_expert, w_gate_e, w_up_e, w_down_e, final_norm_g):
    B, S, D = x.shape
    assert D == D_MODEL and mix_norm_g.shape[0] == 1
    T = B * S
    mem_len = mem.shape[1]
    tm = _tile(S, 512)
    l = 0

    wi = w_in[l]
    o_qd, o_kv, o_kr, o_xq, o_gate = 512, 896, 1152, 1216, 1728
    win_p = jnp.concatenate(
        [wi[:, 0:o_qd], wi[:, o_qd:o_kv], wi[:, o_kv:o_kr], wi[:, o_xq:o_gate], wi[:, o_gate:],
         wi[:, o_kr:o_xq], jnp.zeros((D_MODEL, LANES - QK_ROPE_DIM), wi.dtype)], axis=1).astype(BF16)
    wuq_p = jnp.pad(w_uq[l].reshape(Q_LORA_RANK, MLA_HEADS, QK_NOPE_DIM + QK_ROPE_DIM),
                    ((0, 0), (0, 0), (0, QK_PAD_DIM - QK_NOPE_DIM - QK_ROPE_DIM))
                    ).reshape(Q_LORA_RANK, MLA_HEADS * QK_PAD_DIM).astype(BF16)
    inv_freq = 1.0 / (ROPE_THETA ** (jnp.arange(0, QK_ROPE_DIM, 2, dtype=F32) / QK_ROPE_DIM))
    invf = jnp.concatenate([inv_freq, inv_freq, jnp.zeros((LANES - QK_ROPE_DIM,), F32)])[None, :]
    wr = jnp.concatenate([w_router_expert[l], w_router_group[l],
                          jnp.zeros((D_MODEL, ROUTER_ROWS - N_EXPERTS - N_GROUPS), F32)], axis=1).T.astype(BF16)
    br = jnp.concatenate([b_router_expert[l], b_router_group[l],
                          jnp.zeros((ROUTER_ROWS - N_EXPERTS - N_GROUPS,), F32)])[:, None].astype(F32)
    wgu = jnp.concatenate([w_gate_e[l], w_up_e[l]], axis=2).astype(BF16)
    wd = w_down_e[l].astype(BF16)

    x2 = x.reshape(T, D_MODEL)
    pos2 = positions.reshape(T, 1)

    ypool, xq, gates, q, k, v = _mixer_in(
        x2, pos2, invf, mix_norm_g[l][None, :], win_p, gate_b[l], q_norm_g[l][None, :], wuq_p,
        kv_norm_g[l][None, :], w_uk[l].astype(BF16), w_uv[l].astype(BF16), pool_w[l].astype(BF16),
        pool_scale[l][None, :], B=B, S=S, tm=tm)
    kmem, vmem = _mem_kv(mem.reshape(B * mem_len, D_MODEL), mem_norm_g[l][None, :], w_mem_kv[l].astype(BF16))
    ymla = _mla_attn(q, k, v, tq=tm).reshape(T, MLA_HEADS * V_HEAD_DIM)
    x1, h2, eidx, wts, rank, counts = _merge(
        x2, ypool, ymla, xq, gates, kmem, vmem, w_br_pool[l].astype(BF16), w_br_mla[l].astype(BF16),
        w_br_mem[l].astype(BF16), w_out[l].astype(BF16), ffn_norm_g[l][None, :], wr, br,
        B=B, S=S, tm=tm, mem_len=mem_len)

    R = 2 * T + N_EXPERTS * MOE_ROWS
    assert R // MOE_ROWS <= META_PAD_END
    dest, meta = _moe_pos(counts, eidx, rank)
    meta1 = meta.reshape(META_LANES)
    xs = _dispatch(meta1, dest, h2, R=R, td=tm)
    ys = _moe_ffn(meta1, xs, wgu, wd)
    out = _combine(dest, x1, wts.T, final_norm_g[None, :], ys, tc=_tile(T, 256))
    return out.reshape(B, S, D_MODEL)
```

```python
import functools
import math

import jax
import jax.numpy as jnp
from jax import lax
from jax.experimental import pallas as pl
from jax.experimental.pallas import tpu as pltpu

D_MODEL = 1024
POOL_WINDOWS = (2, 4, 8, 16)
POOL_GROUP_DIM = 128
POOL_DIM = 512
MLA_HEADS = 8
QK_NOPE_DIM = 128
QK_ROPE_DIM = 64
V_HEAD_DIM = 128
Q_LORA_RANK = 384
KV_LORA_RANK = 256
ROPE_THETA = 10000.0
XATTN_HEADS = 4
XATTN_HEAD_DIM = 128
XATTN_DIM = 512
N_BRANCHES = 3
N_GROUPS = 4
EXPERTS_PER_GROUP = 8
N_EXPERTS = 32
D_EXPERT = 256
RMS_EPS = 1e-6
NEG_INF = -1e30

LANES = 128
QK_PAD_DIM = 2 * LANES
POOL_HALO = 16
MOE_ROWS = 256
ROUTER_ROWS = 40
META_LANES = 256
META_PAD_END = 192
META_NACT = 255
VMEM_LIMIT_BYTES = 56 * 1024 * 1024

IN_POOL, IN_QD, IN_KV, IN_XQ, IN_GATE, IN_KR, IN_END = 0, 512, 896, 1152, 1664, 4736, 4864

F32 = jnp.float32
BF16 = jnp.bfloat16


def _rms(x, g):
    ms = jnp.mean(x * x, axis=-1, keepdims=True)
    return (x * lax.rsqrt(ms + RMS_EPS)) * g


def _dot(a, b):
    return jnp.dot(a, b, preferred_element_type=F32)


def _dot_nt(a, b):
    return lax.dot_general(a, b, (((1,), (1,)), ((), ())), preferred_element_type=F32)


def _const_spec(shape):
    nd = len(shape)
    return pl.BlockSpec(shape, lambda *_: (0,) * nd, pipeline_mode=pl.Buffered(1))


def _mixer_in_body(x_ref, pos_ref, invf_ref, mixg_ref, win_ref, gateb_ref, qg_ref, wuq_ref,
                   kvg_ref, wuk_ref, wuv_ref, poolw_ref, pools_ref,
                   ypool_ref, xq_ref, gates_ref, q_ref, k_ref, v_ref, ext_ref,
                   *, tm, tiles_per_seq, q_scale):
    si = lax.rem(pl.program_id(0), tiles_per_seq)
    hb = _rms(x_ref[...], mixg_ref[...]).astype(BF16)

    u = _dot(hb, win_ref[:, IN_POOL:IN_QD])

    @pl.when(si == 0)
    def _():
        ext_ref[0:POOL_HALO, :] = jnp.zeros((POOL_HALO, POOL_DIM), F32)

    ext_ref[POOL_HALO:POOL_HALO + tm, :] = u
    t_seq = lax.broadcasted_iota(jnp.int32, (tm, 1), 0) + si * tm
    for g, w in enumerate(POOL_WINDOWS):
        lo = g * POOL_GROUP_DIM
        hi = lo + POOL_GROUP_DIM
        acc = u[:, lo:hi]
        for j in range(1, w):
            acc = acc + ext_ref[POOL_HALO - j:POOL_HALO - j + tm, lo:hi]
        cnt = jnp.minimum(t_seq + 1, w).astype(F32)
        p = acc / cnt - u[:, lo:hi]
        y = _dot(p.astype(BF16), poolw_ref[g]) * pools_ref[:, lo:hi]
        ypool_ref[:, lo:hi] = y.astype(BF16)
    ext_ref[0:POOL_HALO, :] = ext_ref[tm:tm + POOL_HALO, :]

    ang = pos_ref[...].astype(F32) * invf_ref[...]
    cos = jnp.cos(ang)
    sin = jnp.sin(ang)
    first_half = lax.broadcasted_iota(jnp.int32, (tm, LANES), 1) < (QK_ROPE_DIM // 2)
    sin_signed = jnp.where(first_half, -sin, sin)

    def rope(r):
        swapped = jnp.where(first_half, pltpu.roll(r, LANES - QK_ROPE_DIM // 2, 1),
                            pltpu.roll(r, QK_ROPE_DIM // 2, 1))
        return r * cos + swapped * sin_signed

    cq = _rms(_dot(hb, win_ref[:, IN_QD:IN_KV]), qg_ref[...]).astype(BF16)
    for h in range(MLA_HEADS):
        qh = _dot(cq, wuq_ref[:, h * QK_PAD_DIM:(h + 1) * QK_PAD_DIM])
        q_ref[0, h, :, 0:LANES] = (qh[:, 0:LANES] * q_scale).astype(BF16)
        q_ref[0, h, :, LANES:QK_PAD_DIM] = (rope(qh[:, LANES:QK_PAD_DIM]) * q_scale).astype(BF16)

    ckv = _rms(_dot(hb, win_ref[:, IN_KV:IN_XQ]), kvg_ref[...]).astype(BF16)
    kr = rope(_dot(hb, win_ref[:, IN_KR:IN_END])).astype(BF16)
    for hp in range(MLA_HEADS // 2):
        cols = slice(hp * 2 * LANES, (hp + 1) * 2 * LANES)
        kn = _dot(ckv, wuk_ref[:, cols]).astype(BF16)
        vv = _dot(ckv, wuv_ref[:, cols]).astype(BF16)
        for j in range(2):
            h = 2 * hp + j
            k_ref[0, h, :, 0:LANES] = kn[:, j * LANES:(j + 1) * LANES]
            k_ref[0, h, :, LANES:QK_PAD_DIM] = kr
            v_ref[0, h] = vv[:, j * LANES:(j + 1) * LANES]

    xq_ref[...] = _dot(hb, win_ref[:, IN_XQ:IN_GATE]).astype(BF16)
    for c in range(N_BRANCHES):
        gl = _dot(hb, win_ref[:, IN_GATE + c * D_MODEL:IN_GATE + (c + 1) * D_MODEL])
        gates_ref[:, c * D_MODEL:(c + 1) * D_MODEL] = jax.nn.sigmoid(gl + gateb_ref[c:c + 1, :]).astype(BF16)


def _mixer_in(x2, pos2, invf, mixg, win_p, gate_b, qg, wuq_p, kvg, wuk, wuv, pool_w, pool_s, *, B, S, tm):
    T = B * S
    tps = S // tm
    q_scale = (QK_NOPE_DIM + QK_ROPE_DIM) ** -0.5 * math.log2(math.e)
    body = functools.partial(_mixer_in_body, tm=tm, tiles_per_seq=tps, q_scale=q_scale)
    row = lambda i: (i, 0)
    head = lambda i: (i // tps, 0, i % tps, 0)
    return pl.pallas_call(
        body,
        grid=(T // tm,),
        in_specs=[
            pl.BlockSpec((tm, D_MODEL), row),
            pl.BlockSpec((tm, 1), row),
            _const_spec((1, LANES)),
            _const_spec((1, D_MODEL)),
            _const_spec((D_MODEL, IN_END)),
            _const_spec((N_BRANCHES, D_MODEL)),
            _const_spec((1, Q_LORA_RANK)),
            _const_spec((Q_LORA_RANK, MLA_HEADS * QK_PAD_DIM)),
            _const_spec((1, KV_LORA_RANK)),
            _const_spec((KV_LORA_RANK, MLA_HEADS * QK_NOPE_DIM)),
            _const_spec((KV_LORA_RANK, MLA_HEADS * V_HEAD_DIM)),
            _const_spec((len(POOL_WINDOWS), POOL_GROUP_DIM, POOL_GROUP_DIM)),
            _const_spec((1, POOL_DIM)),
        ],
        out_specs=[
            pl.BlockSpec((tm, POOL_DIM), row),
            pl.BlockSpec((tm, XATTN_DIM), row),
            pl.BlockSpec((tm, N_BRANCHES * D_MODEL), row),
            pl.BlockSpec((1, MLA_HEADS, tm, QK_PAD_DIM), head),
            pl.BlockSpec((1, MLA_HEADS, tm, QK_PAD_DIM), head),
            pl.BlockSpec((1, MLA_HEADS, tm, V_HEAD_DIM), head),
        ],
        out_shape=[
            jax.ShapeDtypeStruct((T, POOL_DIM), BF16),
            jax.ShapeDtypeStruct((T, XATTN_DIM), BF16),
            jax.ShapeDtypeStruct((T, N_BRANCHES * D_MODEL), BF16),
            jax.ShapeDtypeStruct((B, MLA_HEADS, S, QK_PAD_DIM), BF16),
            jax.ShapeDtypeStruct((B, MLA_HEADS, S, QK_PAD_DIM), BF16),
            jax.ShapeDtypeStruct((B, MLA_HEADS, S, V_HEAD_DIM), BF16),
        ],
        scratch_shapes=[pltpu.VMEM((tm + POOL_HALO, POOL_DIM), F32)],
        compiler_params=pltpu.CompilerParams(dimension_semantics=("arbitrary",),
                                             vmem_limit_bytes=VMEM_LIMIT_BYTES),
        name="mixer_in",
    )(x2, pos2, invf, mixg, win_p, gate_b, qg, wuq_p, kvg, wuk, wuv, pool_w, pool_s)


def _mem_kv_body(mem_ref, g_ref, w_ref, k_ref, v_ref):
    mb = _rms(mem_ref[...], g_ref[...]).astype(BF16)
    kv = _dot(mb, w_ref[...])
    k_ref[...] = kv[:, 0:XATTN_DIM].astype(BF16)
    v_ref[...] = kv[:, XATTN_DIM:2 * XATTN_DIM].astype(BF16)


def _mem_kv(mem2, g, w):
    rows = mem2.shape[0]
    tr = min(rows, 512)
    return pl.pallas_call(
        _mem_kv_body,
        grid=(rows // tr,),
        in_specs=[pl.BlockSpec((tr, D_MODEL), lambda i: (i, 0)),
                  _const_spec((1, D_MODEL)),
                  _const_spec((D_MODEL, 2 * XATTN_DIM))],
        out_specs=[pl.BlockSpec((tr, XATTN_DIM), lambda i: (i, 0)),
                   pl.BlockSpec((tr, XATTN_DIM), lambda i: (i, 0))],
        out_shape=[jax.ShapeDtypeStruct((rows, XATTN_DIM), BF16),
                   jax.ShapeDtypeStruct((rows, XATTN_DIM), BF16)],
        compiler_params=pltpu.CompilerParams(dimension_semantics=("arbitrary",)),
        name="mem_kv",
    )(mem2, g, w)


def _attn_body(q_ref, k_ref, v_ref, o_ref, s_a, s_b, mc_a, mc_b, m_ref, l_ref, acc_ref, *, S, tq):
    s_bufs = (s_a, s_b)
    mc_bufs = (mc_a, mc_b)
    mxu_row_split = 2

    def q_tile(qi, carry):
        q0 = pl.multiple_of(qi * tq, tq)
        m_ref[...] = jnp.full((tq, LANES), NEG_INF, F32)
        l_ref[...] = jnp.zeros((tq, LANES), F32)
        acc_ref[...] = jnp.zeros((tq, V_HEAD_DIM), F32)

        def scores(kb, slot, masked):
            c0 = pl.multiple_of(kb * tq, tq)
            s = _dot_nt(q_ref[0, 0, pl.ds(q0, tq), :], k_ref[0, 0, pl.ds(c0, tq), :])
            if masked:
                ri = lax.broadcasted_iota(jnp.int32, (tq, tq), 0)
                ci = lax.broadcasted_iota(jnp.int32, (tq, tq), 1)
                s = jnp.where(ci - ri <= q0 - c0, s, NEG_INF)
            s_bufs[slot][...] = s
            mc_bufs[slot][...] = jnp.broadcast_to(jnp.max(s, axis=1, keepdims=True), (tq, LANES))

        def accumulate(kb, slot):
            c0 = pl.multiple_of(kb * tq, tq)
            m_prev = m_ref[...]
            m_new = jnp.maximum(m_prev, mc_bufs[slot][...])
            alpha = jnp.exp2(m_prev - m_new)
            p = jnp.exp2(s_bufs[slot][...] - jnp.concatenate([m_new] * (tq // LANES), axis=1))
            psum = p[:, 0:LANES]
            for c in range(1, tq // LANES):
                psum = psum + p[:, c * LANES:(c + 1) * LANES]
            l_ref[...] = alpha * l_ref[...] + psum
            m_ref[...] = m_new
            pb = p.astype(BF16)
            v = v_ref[0, 0, pl.ds(c0, tq), :]
            h = tq // mxu_row_split
            for i in range(mxu_row_split):
                rows = slice(i * h, (i + 1) * h)
                acc_ref[rows, :] = alpha[rows, :] * acc_ref[rows, :] + _dot(pb[rows, :], v)

        scores(0, 0, True)

        def pair(i, c):
            scores(2 * i + 1, 1, False)
            accumulate(2 * i, 0)
            scores(2 * i + 2, 0, False)
            accumulate(2 * i + 1, 1)
            return c

        lax.fori_loop(0, jnp.maximum(qi - 1, 0) // 2, pair, 0)

        @pl.when(qi == 0)
        def _():
            accumulate(0, 0)

        @pl.when(lax.rem(qi, 2) == 1)
        def _():
            scores(qi, 1, True)
            accumulate(qi - 1, 0)
            accumulate(qi, 1)

        @pl.when(jnp.logical_and(qi >= 2, lax.rem(qi, 2) == 0))
        def _():
            scores(qi - 1, 1, False)
            accumulate(qi - 2, 0)
            scores(qi, 0, True)
            accumulate(qi - 1, 1)
            accumulate(qi, 0)

        l = jnp.sum(l_ref[...], axis=1, keepdims=True)
        o_ref[0, pl.ds(q0, tq), :] = (acc_ref[...] / l).astype(BF16)
        return carry

    lax.fori_loop(0, S // tq, q_tile, 0)


def _mla_attn(q, k, v, *, tq):
    B, H, S, _ = q.shape
    per_head = lambda b, h: (b, h, 0, 0)
    return pl.pallas_call(
        functools.partial(_attn_body, S=S, tq=tq),
        grid=(B, H),
        in_specs=[pl.BlockSpec((1, 1, S, QK_PAD_DIM), per_head),
                  pl.BlockSpec((1, 1, S, QK_PAD_DIM), per_head),
                  pl.BlockSpec((1, 1, S, V_HEAD_DIM), per_head)],
        out_specs=pl.BlockSpec((1, S, V_HEAD_DIM), lambda b, h: (b, 0, h)),
        out_shape=jax.ShapeDtypeStruct((B, S, H * V_HEAD_DIM), BF16),
        scratch_shapes=[pltpu.VMEM((tq, tq), F32), pltpu.VMEM((tq, tq), F32),
                        pltpu.VMEM((tq, LANES), F32), pltpu.VMEM((tq, LANES), F32),
                        pltpu.VMEM((tq, LANES), F32), pltpu.VMEM((tq, LANES), F32),
                        pltpu.VMEM((tq, V_HEAD_DIM), F32)],
        compiler_params=pltpu.CompilerParams(dimension_semantics=("arbitrary", "arbitrary"),
                                             vmem_limit_bytes=VMEM_LIMIT_BYTES),
        name="mla_attn",
    )(q, k, v)


def _merge_body(x_ref, ypool_ref, ymla_ref, xq_ref, gates_ref, kmem_ref, vmem_ref,
                wbp_ref, wbm_ref, wbx_ref, wout_ref, ffng_ref, wr_ref, br_ref,
                x1_ref, h2_ref, eidx_ref, wts_ref, rank_ref, counts_ref, carry_ref, *, tm):
    @pl.when(pl.program_id(0) == 0)
    def _():
        carry_ref[...] = jnp.zeros((N_EXPERTS, LANES), F32)

    xq = xq_ref[...]
    parts = []
    for h in range(XATTN_HEADS):
        cols = slice(h * XATTN_HEAD_DIM, (h + 1) * XATTN_HEAD_DIM)
        s = _dot_nt(xq[:, cols], kmem_ref[:, cols]) * (XATTN_HEAD_DIM ** -0.5)
        e = jnp.exp(s - jnp.max(s, axis=1, keepdims=True))
        p = e / jnp.sum(e, axis=1, keepdims=True)
        parts.append(_dot(p.astype(BF16), vmem_ref[:, cols]))
    ymem = jnp.concatenate(parts, axis=1).astype(BF16)

    gates = gates_ref[...].astype(F32)
    merged = (gates[:, 0:D_MODEL] * _dot(ypool_ref[...], wbp_ref[...])
              + gates[:, D_MODEL:2 * D_MODEL] * _dot(ymla_ref[...], wbm_ref[...])
              + gates[:, 2 * D_MODEL:3 * D_MODEL] * _dot(ymem, wbx_ref[...]))
    x1 = x_ref[...] + _dot(merged.astype(BF16), wout_ref[...])
    x1_ref[...] = x1
    h2 = _rms(x1, ffng_ref[...])
    h2_ref[...] = h2

    lt = _dot_nt(wr_ref[...], h2.astype(BF16)) + br_ref[...]
    gl = lt[N_EXPERTS:N_EXPERTS + N_GROUPS, :]
    gmax = jnp.max(gl, axis=0, keepdims=True)
    r4 = lax.broadcasted_iota(jnp.int32, (N_GROUPS, tm), 0).astype(F32)
    gidx = jnp.min(jnp.where(gl == gmax, r4, float(N_GROUPS)), axis=0, keepdims=True)
    pg = 1.0 / jnp.sum(jnp.exp(gl - gmax), axis=0, keepdims=True)
    esel = lt[0:EXPERTS_PER_GROUP, :]
    for g in range(1, N_GROUPS):
        esel = jnp.where(gidx == float(g), lt[g * EXPERTS_PER_GROUP:(g + 1) * EXPERTS_PER_GROUP, :], esel)
    r8 = lax.broadcasted_iota(jnp.int32, (EXPERTS_PER_GROUP, tm), 0).astype(F32)
    m1 = jnp.max(esel, axis=0, keepdims=True)
    i1 = jnp.min(jnp.where(esel == m1, r8, float(EXPERTS_PER_GROUP)), axis=0, keepdims=True)
    rest = jnp.where(r8 == i1, -jnp.inf, esel)
    m2 = jnp.max(rest, axis=0, keepdims=True)
    i2 = jnp.min(jnp.where(rest == m2, r8, float(EXPERTS_PER_GROUP)), axis=0, keepdims=True)
    e2 = jnp.exp(m2 - m1)
    den = 1.0 + e2
    wts_ref[0:1, :] = pg / den
    wts_ref[1:2, :] = pg * e2 / den
    ex1 = gidx * float(EXPERTS_PER_GROUP) + i1
    ex2 = gidx * float(EXPERTS_PER_GROUP) + i2
    eidx_ref[0:1, :] = ex1.astype(jnp.int32)
    eidx_ref[1:2, :] = ex2.astype(jnp.int32)

    r32 = lax.broadcasted_iota(jnp.int32, (N_EXPERTS, tm), 0).astype(F32)
    is1 = r32 == ex1
    is2 = r32 == ex2
    member = jnp.where(is1 | is2, 1.0, 0.0)
    upper = jnp.where(lax.broadcasted_iota(jnp.int32, (tm, tm), 0)
                      <= lax.broadcasted_iota(jnp.int32, (tm, tm), 1), 1.0, 0.0).astype(BF16)
    incl = _dot(member.astype(BF16), upper)
    carry = carry_ref[:, 0:1]
    excl = incl - member + carry
    rank_ref[0:1, :] = jnp.sum(jnp.where(is1, excl, 0.0), axis=0, keepdims=True).astype(jnp.int32)
    rank_ref[1:2, :] = jnp.sum(jnp.where(is2, excl, 0.0), axis=0, keepdims=True).astype(jnp.int32)
    total = carry + jnp.sum(member, axis=1, keepdims=True)
    carry_ref[...] = jnp.broadcast_to(total, (N_EXPERTS, LANES))
    counts_ref[...] = jnp.broadcast_to(total, (N_EXPERTS, LANES)).astype(jnp.int32)


def _merge(x2, ypool, ymla, xq, gates, kmem, vmem, wbp, wbm, wbx, wout, ffng, wr, br, *, B, S, tm, mem_len):
    T = B * S
    tps = S // tm
    row = lambda i: (i, 0)
    lane = lambda i: (0, i)
    memb = lambda i: (i // tps, 0)
    return pl.pallas_call(
        functools.partial(_merge_body, tm=tm),
        grid=(T // tm,),
        in_specs=[
            pl.BlockSpec((tm, D_MODEL), row),
            pl.BlockSpec((tm, POOL_DIM), row),
            pl.BlockSpec((tm, MLA_HEADS * V_HEAD_DIM), row),
            pl.BlockSpec((tm, XATTN_DIM), row),
            pl.BlockSpec((tm, N_BRANCHES * D_MODEL), row),
            pl.BlockSpec((mem_len, XATTN_DIM), memb),
            pl.BlockSpec((mem_len, XATTN_DIM), memb),
            _const_spec((POOL_DIM, D_MODEL)),
            _const_spec((MLA_HEADS * V_HEAD_DIM, D_MODEL)),
            _const_spec((XATTN_DIM, D_MODEL)),
            _const_spec((D_MODEL, D_MODEL)),
            _const_spec((1, D_MODEL)),
            _const_spec((ROUTER_ROWS, D_MODEL)),
            _const_spec((ROUTER_ROWS, 1)),
        ],
        out_specs=[
            pl.BlockSpec((tm, D_MODEL), row),
            pl.BlockSpec((tm, D_MODEL), row),
            pl.BlockSpec((2, tm), lane),
            pl.BlockSpec((2, tm), lane),
            pl.BlockSpec((2, tm), lane),
            pl.BlockSpec((N_EXPERTS, LANES), lambda i: (0, 0)),
        ],
        out_shape=[
            jax.ShapeDtypeStruct((T, D_MODEL), F32),
            jax.ShapeDtypeStruct((T, D_MODEL), F32),
            jax.ShapeDtypeStruct((2, T), jnp.int32),
            jax.ShapeDtypeStruct((2, T), F32),
            jax.ShapeDtypeStruct((2, T), jnp.int32),
            jax.ShapeDtypeStruct((N_EXPERTS, LANES), jnp.int32),
        ],
        scratch_shapes=[pltpu.VMEM((N_EXPERTS, LANES), F32)],
        compiler_params=pltpu.CompilerParams(dimension_semantics=("arbitrary",),
                                             vmem_limit_bytes=VMEM_LIMIT_BYTES),
        name="merge",
    )(x2, ypool, ymla, xq, gates, kmem, vmem, wbp, wbm, wbx, wout, ffng, wr, br)


def _moe_pos_body(counts_ref, eidx_ref, rank_ref, dest_ref, meta_ref):
    shift = int(math.log2(MOE_ROWS))
    cnt = counts_ref[...]
    padded = lax.shift_left(lax.shift_right_logical(cnt + (MOE_ROWS - 1), shift), shift)
    r32 = lax.broadcasted_iota(jnp.int32, (N_EXPERTS, LANES), 0)
    pad_start = jnp.zeros((N_EXPERTS, LANES), jnp.int32)
    for e in range(N_EXPERTS - 1):
        pad_start = pad_start + jnp.where(r32 > e, padded[e:e + 1, :], 0)
    pad_end = pad_start + padded

    eidx = eidx_ref[...]
    dest = rank_ref[...]
    for e in range(N_EXPERTS):
        dest = dest + jnp.where(eidx == e, pad_start[e:e + 1, 0:1], 0)
    dest_ref[...] = dest

    lane = lax.broadcasted_iota(jnp.int32, (1, META_LANES), 1)
    block_row = lane * MOE_ROWS
    blk_e = jnp.zeros((1, META_LANES), jnp.int32)
    pe_row = jnp.zeros((1, META_LANES), jnp.int32)
    for e in range(N_EXPERTS):
        pe = pad_end[e:e + 1, 0:1]
        blk_e = blk_e + jnp.where(pe <= block_row, 1, 0)
        pe_row = pe_row + jnp.where(lane == META_PAD_END + e, pe, 0)
    blk_e = jnp.minimum(blk_e, N_EXPERTS - 1)
    nact = lax.shift_right_logical(pad_end[N_EXPERTS - 1:N_EXPERTS, 0:1], shift)
    meta = jnp.where(lane < META_PAD_END, blk_e, pe_row)
    meta_ref[...] = jnp.where(lane == META_NACT, nact, meta)


def _moe_pos(counts, eidx, rank):
    T = eidx.shape[1]
    full = lambda shape: pl.BlockSpec(shape, lambda i: (0,) * len(shape))
    return pl.pallas_call(
        _moe_pos_body,
        grid=(1,),
        in_specs=[full((N_EXPERTS, LANES)), full((2, T)), full((2, T))],
        out_specs=[full((2, T)), full((1, META_LANES))],
        out_shape=[jax.ShapeDtypeStruct((2, T), jnp.int32),
                   jax.ShapeDtypeStruct((1, META_LANES), jnp.int32)],
        compiler_params=pltpu.CompilerParams(dimension_semantics=("arbitrary",)),
        name="moe_pos",
    )(counts, eidx, rank)


def _dispatch_body(meta_ref, dest_ref, h2_ref, xs_ref, zero_ref, sem, zsem, *, td, n_blocks):
    def pad_copy(e):
        end = pl.multiple_of(meta_ref[META_PAD_END + e], MOE_ROWS)
        return pltpu.make_async_copy(zero_ref, xs_ref.at[pl.ds(end - MOE_ROWS, MOE_ROWS)], zsem)

    def has_rows(e):
        prev = jnp.where(e == 0, 0, meta_ref[META_PAD_END + jnp.maximum(e - 1, 0)])
        return meta_ref[META_PAD_END + e] > prev

    @pl.when(pl.program_id(0) == 0)
    def _():
        zero_ref[...] = jnp.zeros((MOE_ROWS, D_MODEL), F32)

        def start(e, c):
            @pl.when(has_rows(e))
            def _():
                pad_copy(e).start()
            return c

        def wait(e, c):
            @pl.when(has_rows(e))
            def _():
                pad_copy(e).wait()
            return c

        def tail_copy(b):
            return pltpu.make_async_copy(
                zero_ref, xs_ref.at[pl.ds(pl.multiple_of(b * MOE_ROWS, MOE_ROWS), MOE_ROWS)], zsem)

        def tail_start(b, c):
            tail_copy(b).start()
            return c

        def tail_wait(b, c):
            tail_copy(b).wait()
            return c

        nact = meta_ref[META_NACT]
        lax.fori_loop(0, N_EXPERTS, start, 0)
        lax.fori_loop(nact, n_blocks, tail_start, 0)
        lax.fori_loop(0, N_EXPERTS, wait, 0)
        lax.fori_loop(nact, n_blocks, tail_wait, 0)

    def row_copy(i, slot):
        return pltpu.make_async_copy(h2_ref.at[pl.ds(i, 1)], xs_ref.at[pl.ds(dest_ref[slot, i], 1)], sem)

    def issue(i, c):
        row_copy(i, 0).start()
        row_copy(i, 1).start()
        return c

    lax.fori_loop(0, td, issue, 0, unroll=8)
    for _ in range(2):
        pltpu.make_async_copy(h2_ref, xs_ref.at[pl.ds(0, td)], sem).wait()


def _dispatch(meta1, dest, h2, *, R, td):
    T = h2.shape[0]
    return pl.pallas_call(
        functools.partial(_dispatch_body, td=td, n_blocks=R // MOE_ROWS),
        grid_spec=pltpu.PrefetchScalarGridSpec(
            num_scalar_prefetch=1,
            grid=(T // td,),
            in_specs=[pl.BlockSpec((2, td), lambda i, m: (0, i), memory_space=pltpu.SMEM),
                      pl.BlockSpec((td, D_MODEL), lambda i, m: (i, 0))],
            out_specs=pl.BlockSpec(memory_space=pl.ANY),
            scratch_shapes=[pltpu.VMEM((MOE_ROWS, D_MODEL), F32),
                            pltpu.SemaphoreType.DMA, pltpu.SemaphoreType.DMA],
        ),
        out_shape=jax.ShapeDtypeStruct((R, D_MODEL), F32),
        compiler_params=pltpu.CompilerParams(dimension_semantics=("arbitrary",)),
        name="dispatch",
    )(meta1, dest, h2)


def _moe_ffn_body(meta_ref, xs_ref, wgu_ref, wd_ref, ys_ref):
    active = pl.program_id(0) < meta_ref[META_NACT]

    @pl.when(active)
    def _():
        gu = _dot(xs_ref[...].astype(BF16), wgu_ref[0])
        g = gu[:, 0:D_EXPERT]
        a = (g * jax.nn.sigmoid(g)) * gu[:, D_EXPERT:2 * D_EXPERT]
        ys_ref[...] = _dot(a.astype(BF16), wd_ref[0])

    @pl.when(jnp.logical_not(active))
    def _():
        ys_ref[...] = jnp.zeros((MOE_ROWS, D_MODEL), F32)


def _moe_ffn(meta1, xs, wgu, wd):
    R = xs.shape[0]
    blk = lambda b, m: (jnp.minimum(b, m[META_NACT] - 1), 0)
    wsel = lambda b, m: (m[jnp.minimum(b, m[META_NACT] - 1)], 0, 0)
    return pl.pallas_call(
        _moe_ffn_body,
        grid_spec=pltpu.PrefetchScalarGridSpec(
            num_scalar_prefetch=1,
            grid=(R // MOE_ROWS,),
            in_specs=[pl.BlockSpec((MOE_ROWS, D_MODEL), blk),
                      pl.BlockSpec((1, D_MODEL, 2 * D_EXPERT), wsel),
                      pl.BlockSpec((1, D_EXPERT, D_MODEL), wsel)],
            out_specs=pl.BlockSpec((MOE_ROWS, D_MODEL), lambda b, m: (b, 0)),
        ),
        out_shape=jax.ShapeDtypeStruct((R, D_MODEL), F32),
        compiler_params=pltpu.CompilerParams(dimension_semantics=("arbitrary",)),
        name="moe_ffn",
    )(meta1, xs, wgu, wd)


def _combine_body(dest_ref, x1_ref, w_ref, fg_ref, ys_ref, out_ref, ya_ref, yb_ref, sem, *, tc):
    def row_copy(i, slot, buf):
        return pltpu.make_async_copy(ys_ref.at[pl.ds(dest_ref[slot, i], 1)], buf.at[pl.ds(i, 1)], sem)

    def issue(i, c):
        row_copy(i, 0, ya_ref).start()
        row_copy(i, 1, yb_ref).start()
        return c

    lax.fori_loop(0, tc, issue, 0, unroll=8)
    pltpu.make_async_copy(ys_ref.at[pl.ds(0, tc)], ya_ref, sem).wait()
    pltpu.make_async_copy(ys_ref.at[pl.ds(0, tc)], yb_ref, sem).wait()
    w = w_ref[...]
    y = x1_ref[...] + w[:, 0:1] * ya_ref[...] + w[:, 1:2] * yb_ref[...]
    out_ref[...] = _rms(y, fg_ref[...])


def _combine(dest, x1, wts_t, fg, ys, *, tc):
    T = x1.shape[0]
    return pl.pallas_call(
        functools.partial(_combine_body, tc=tc),
        grid=(T // tc,),
        in_specs=[pl.BlockSpec((2, tc), lambda i: (0, i), memory_space=pltpu.SMEM),
                  pl.BlockSpec((tc, D_MODEL), lambda i: (i, 0)),
                  pl.BlockSpec((tc, 2), lambda i: (i, 0)),
                  _const_spec((1, D_MODEL)),
                  pl.BlockSpec(memory_space=pl.ANY)],
        out_specs=pl.BlockSpec((tc, D_MODEL), lambda i: (i, 0)),
        out_shape=jax.ShapeDtypeStruct((T, D_MODEL), F32),
        scratch_shapes=[pltpu.VMEM((tc, D_MODEL), F32), pltpu.VMEM((tc, D_MODEL), F32),
                        pltpu.SemaphoreType.DMA],
        compiler_params=pltpu.CompilerParams(dimension_semantics=("arbitrary",)),
        name="combine",
    )(dest, x1, wts_t, fg, ys)


def _tile(n, t):
    t = min(n, t)
    assert n % t == 0, (n, t)
    return t


def kernel(x, mem, positions, mix_norm_g, w_in, gate_b, q_norm_g, w_uq, kv_norm_g, w_uk, w_uv, pool_w, pool_scale, mem_norm_g, w_mem_kv, w_br_pool, w_br_mla, w_br_mem, w_out, ffn_norm_g, w_router_group, b_router_group, w_router_expert, b_router_expert, w_gate_e, w_up_e, w_down_e, final_norm_g):
    B, S, D = x.shape
    assert D == D_MODEL and mix_norm_g.shape[0] == 1
    T = B * S
    mem_len = mem.shape[1]
    tm = _tile(S, 512)
    l = 0

    wi = w_in[l]
    o_qd, o_kv, o_kr, o_xq, o_gate = 512, 896, 1152, 1216, 1728
    win_p = jnp.concatenate(
        [wi[:, 0:o_qd], wi[:, o_qd:o_kv], wi[:, o_kv:o_kr], wi[:, o_xq:o_gate], wi[:, o_gate:],
         wi[:, o_kr:o_xq], jnp.zeros((D_MODEL, LANES - QK_ROPE_DIM), wi.dtype)], axis=1).astype(BF16)
    wuq_p = jnp.pad(w_uq[l].reshape(Q_LORA_RANK, MLA_HEADS, QK_NOPE_DIM + QK_ROPE_DIM),
                    ((0, 0), (0, 0), (0, QK_PAD_DIM - QK_NOPE_DIM - QK_ROPE_DIM))
                    ).reshape(Q_LORA_RANK, MLA_HEADS * QK_PAD_DIM).astype(BF16)
    inv_freq = 1.0 / (ROPE_THETA ** (jnp.arange(0, QK_ROPE_DIM, 2, dtype=F32) / QK_ROPE_DIM))
    invf = jnp.concatenate([inv_freq, inv_freq, jnp.zeros((LANES - QK_ROPE_DIM,), F32)])[None, :]
    wr = jnp.concatenate([w_router_expert[l], w_router_group[l],
                          jnp.zeros((D_MODEL, ROUTER_ROWS - N_EXPERTS - N_GROUPS), F32)], axis=1).T.astype(BF16)
    br = jnp.concatenate([b_router_expert[l], b_router_group[l],
                          jnp.zeros((ROUTER_ROWS - N_EXPERTS - N_GROUPS,), F32)])[:, None].astype(F32)
    wgu = jnp.concatenate([w_gate_e[l], w_up_e[l]], axis=2).astype(BF16)
    wd = w_down_e[l].astype(BF16)

    x2 = x.reshape(T, D_MODEL)
    pos2 = positions.reshape(T, 1)

    ypool, xq, gates, q, k, v = _mixer_in(
        x2, pos2, invf, mix_norm_g[l][None, :], win_p, gate_b[l], q_norm_g[l][None, :], wuq_p,
        kv_norm_g[l][None, :], w_uk[l].astype(BF16), w_uv[l].astype(BF16), pool_w[l].astype(BF16),
        pool_scale[l][None, :], B=B, S=S, tm=tm)
    kmem, vmem = _mem_kv(mem.reshape(B * mem_len, D_MODEL), mem_norm_g[l][None, :], w_mem_kv[l].astype(BF16))
    ymla = _mla_attn(q, k, v, tq=tm).reshape(T, MLA_HEADS * V_HEAD_DIM)
    x1, h2, eidx, wts, rank, counts = _merge(
        x2, ypool, ymla, xq, gates, kmem, vmem, w_br_pool[l].astype(BF16), w_br_mla[l].astype(BF16),
        w_br_mem[l].astype(BF16), w_out[l].astype(BF16), ffn_norm_g[l][None, :], wr, br,
        B=B, S=S, tm=tm, mem_len=mem_len)

    R = 2 * T + N_EXPERTS * MOE_ROWS
    assert R // MOE_ROWS <= META_PAD_END
    dest, meta = _moe_pos(counts, eidx, rank)
    meta1 = meta.reshape(META_LANES)
    xs = _dispatch(meta1, dest, h2, R=R, td=tm)
    ys = _moe_ffn(meta1, xs, wgu, wd)
    out = _combine(dest, x1, wts.T, final_norm_g[None, :], ys, tc=_tile(T, 256))
    return out.reshape(B, S, D_MODEL)
```

```python
import functools
import math

import jax
import jax.numpy as jnp
from jax import lax
from jax.experimental import pallas as pl
from jax.experimental.pallas import tpu as pltpu

D_MODEL = 1024
POOL_WINDOWS = (2, 4, 8, 16)
POOL_GROUP_DIM = 128
POOL_DIM = 512
MLA_HEADS = 8
QK_NOPE_DIM = 128
QK_ROPE_DIM = 64
V_HEAD_DIM = 128
Q_LORA_RANK = 384
KV_LORA_RANK = 256
ROPE_THETA = 10000.0
XATTN_HEADS = 4
XATTN_HEAD_DIM = 128
XATTN_DIM = 512
N_BRANCHES = 3
N_GROUPS = 4
EXPERTS_PER_GROUP = 8
N_EXPERTS = 32
D_EXPERT = 256
RMS_EPS = 1e-6
NEG_INF = -1e30

LANES = 128
QK_PAD_DIM = 2 * LANES
POOL_HALO = 16
MOE_ROWS = 256
RUN_ALIGN = 8
PACKED_DIM = D_MODEL // 2
ROUTER_ROWS = 40
META_LANES = 256
META_PAD_END = 192
META_NACT = 255
VMEM_LIMIT_BYTES = 56 * 1024 * 1024

IN_POOL, IN_QD, IN_KV, IN_XQ, IN_GATE, IN_KR, IN_END = 0, 512, 896, 1152, 1664, 4736, 4864

F32 = jnp.float32
BF16 = jnp.bfloat16
U32 = jnp.uint32


def _rms(x, g):
    ms = jnp.mean(x * x, axis=-1, keepdims=True)
    return (x * lax.rsqrt(ms + RMS_EPS)) * g


def _dot(a, b):
    return jnp.dot(a, b, preferred_element_type=F32)


def _dot_nt(a, b):
    return lax.dot_general(a, b, (((1,), (1,)), ((), ())), preferred_element_type=F32)


def _const_spec(shape):
    nd = len(shape)
    return pl.BlockSpec(shape, lambda *_: (0,) * nd, pipeline_mode=pl.Buffered(1))


def _mixer_in_body(x_ref, pos_ref, invf_ref, mixg_ref, win_ref, gateb_ref, qg_ref, wuq_ref,
                   kvg_ref, wuk_ref, wuv_ref, poolw_ref, pools_ref,
                   ypool_ref, xq_ref, gates_ref, q_ref, k_ref, v_ref, ext_ref,
                   *, tm, tiles_per_seq, q_scale):
    si = lax.rem(pl.program_id(0), tiles_per_seq)
    hb = _rms(x_ref[...], mixg_ref[...]).astype(BF16)

    u = _dot(hb, win_ref[:, IN_POOL:IN_QD])

    @pl.when(si == 0)
    def _():
        ext_ref[0:POOL_HALO, :] = jnp.zeros((POOL_HALO, POOL_DIM), F32)

    ext_ref[POOL_HALO:POOL_HALO + tm, :] = u
    t_seq = lax.broadcasted_iota(jnp.int32, (tm, 1), 0) + si * tm
    for g, w in enumerate(POOL_WINDOWS):
        lo = g * POOL_GROUP_DIM
        hi = lo + POOL_GROUP_DIM
        acc = u[:, lo:hi]
        for j in range(1, w):
            acc = acc + ext_ref[POOL_HALO - j:POOL_HALO - j + tm, lo:hi]
        cnt = jnp.minimum(t_seq + 1, w).astype(F32)
        p = acc / cnt - u[:, lo:hi]
        y = _dot(p.astype(BF16), poolw_ref[g]) * pools_ref[:, lo:hi]
        ypool_ref[:, lo:hi] = y.astype(BF16)
    ext_ref[0:POOL_HALO, :] = ext_ref[tm:tm + POOL_HALO, :]

    ang = pos_ref[...].astype(F32) * invf_ref[...]
    cos = jnp.cos(ang)
    sin = jnp.sin(ang)
    first_half = lax.broadcasted_iota(jnp.int32, (tm, LANES), 1) < (QK_ROPE_DIM // 2)
    sin_signed = jnp.where(first_half, -sin, sin)

    def rope(r):
        swapped = jnp.where(first_half, pltpu.roll(r, LANES - QK_ROPE_DIM // 2, 1),
                            pltpu.roll(r, QK_ROPE_DIM // 2, 1))
        return r * cos + swapped * sin_signed

    cq = _rms(_dot(hb, win_ref[:, IN_QD:IN_KV]), qg_ref[...]).astype(BF16)
    for h in range(MLA_HEADS):
        qh = _dot(cq, wuq_ref[:, h * QK_PAD_DIM:(h + 1) * QK_PAD_DIM])
        q_ref[0, h, :, 0:LANES] = (qh[:, 0:LANES] * q_scale).astype(BF16)
        q_ref[0, h, :, LANES:QK_PAD_DIM] = (rope(qh[:, LANES:QK_PAD_DIM]) * q_scale).astype(BF16)

    ckv = _rms(_dot(hb, win_ref[:, IN_KV:IN_XQ]), kvg_ref[...]).astype(BF16)
    kr = rope(_dot(hb, win_ref[:, IN_KR:IN_END])).astype(BF16)
    for hp in range(MLA_HEADS // 2):
        cols = slice(hp * 2 * LANES, (hp + 1) * 2 * LANES)
        kn = _dot(ckv, wuk_ref[:, cols]).astype(BF16)
        vv = _dot(ckv, wuv_ref[:, cols]).astype(BF16)
        for j in range(2):
            h = 2 * hp + j
            k_ref[0, h, :, 0:LANES] = kn[:, j * LANES:(j + 1) * LANES]
            k_ref[0, h, :, LANES:QK_PAD_DIM] = kr
            v_ref[0, h] = vv[:, j * LANES:(j + 1) * LANES]

    xq_ref[...] = _dot(hb, win_ref[:, IN_XQ:IN_GATE]).astype(BF16)
    for c in range(N_BRANCHES):
        gl = _dot(hb, win_ref[:, IN_GATE + c * D_MODEL:IN_GATE + (c + 1) * D_MODEL])
        gates_ref[:, c * D_MODEL:(c + 1) * D_MODEL] = jax.nn.sigmoid(gl + gateb_ref[c:c + 1, :]).astype(BF16)


def _mixer_in(x2, pos2, invf, mixg, win_p, gate_b, qg, wuq_p, kvg, wuk, wuv, pool_w, pool_s, *, B, S, tm):
    T = B * S
    tps = S // tm
    q_scale = (QK_NOPE_DIM + QK_ROPE_DIM) ** -0.5 * math.log2(math.e)
    body = functools.partial(_mixer_in_body, tm=tm, tiles_per_seq=tps, q_scale=q_scale)
    row = lambda i: (i, 0)
    head = lambda i: (i // tps, 0, i % tps, 0)
    return pl.pallas_call(
        body,
        grid=(T // tm,),
        in_specs=[
            pl.BlockSpec((tm, D_MODEL), row),
            pl.BlockSpec((tm, 1), row),
            _const_spec((1, LANES)),
            _const_spec((1, D_MODEL)),
            _const_spec((D_MODEL, IN_END)),
            _const_spec((N_BRANCHES, D_MODEL)),
            _const_spec((1, Q_LORA_RANK)),
            _const_spec((Q_LORA_RANK, MLA_HEADS * QK_PAD_DIM)),
            _const_spec((1, KV_LORA_RANK)),
            _const_spec((KV_LORA_RANK, MLA_HEADS * QK_NOPE_DIM)),
            _const_spec((KV_LORA_RANK, MLA_HEADS * V_HEAD_DIM)),
            _const_spec((len(POOL_WINDOWS), POOL_GROUP_DIM, POOL_GROUP_DIM)),
            _const_spec((1, POOL_DIM)),
        ],
        out_specs=[
            pl.BlockSpec((tm, POOL_DIM), row),
            pl.BlockSpec((tm, XATTN_DIM), row),
            pl.BlockSpec((tm, N_BRANCHES * D_MODEL), row),
            pl.BlockSpec((1, MLA_HEADS, tm, QK_PAD_DIM), head),
            pl.BlockSpec((1, MLA_HEADS, tm, QK_PAD_DIM), head),
            pl.BlockSpec((1, MLA_HEADS, tm, V_HEAD_DIM), head),
        ],
        out_shape=[
            jax.ShapeDtypeStruct((T, POOL_DIM), BF16),
            jax.ShapeDtypeStruct((T, XATTN_DIM), BF16),
            jax.ShapeDtypeStruct((T, N_BRANCHES * D_MODEL), BF16),
            jax.ShapeDtypeStruct((B, MLA_HEADS, S, QK_PAD_DIM), BF16),
            jax.ShapeDtypeStruct((B, MLA_HEADS, S, QK_PAD_DIM), BF16),
            jax.ShapeDtypeStruct((B, MLA_HEADS, S, V_HEAD_DIM), BF16),
        ],
        scratch_shapes=[pltpu.VMEM((tm + POOL_HALO, POOL_DIM), F32)],
        compiler_params=pltpu.CompilerParams(dimension_semantics=("arbitrary",),
                                             vmem_limit_bytes=VMEM_LIMIT_BYTES),
        name="mixer_in",
    )(x2, pos2, invf, mixg, win_p, gate_b, qg, wuq_p, kvg, wuk, wuv, pool_w, pool_s)


def _mem_kv_body(mem_ref, g_ref, w_ref, k_ref, v_ref):
    mb = _rms(mem_ref[...], g_ref[...]).astype(BF16)
    kv = _dot(mb, w_ref[...])
    k_ref[...] = kv[:, 0:XATTN_DIM].astype(BF16)
    v_ref[...] = kv[:, XATTN_DIM:2 * XATTN_DIM].astype(BF16)


def _mem_kv(mem2, g, w):
    rows = mem2.shape[0]
    tr = min(rows, 512)
    return pl.pallas_call(
        _mem_kv_body,
        grid=(rows // tr,),
        in_specs=[pl.BlockSpec((tr, D_MODEL), lambda i: (i, 0)),
                  _const_spec((1, D_MODEL)),
                  _const_spec((D_MODEL, 2 * XATTN_DIM))],
        out_specs=[pl.BlockSpec((tr, XATTN_DIM), lambda i: (i, 0)),
                   pl.BlockSpec((tr, XATTN_DIM), lambda i: (i, 0))],
        out_shape=[jax.ShapeDtypeStruct((rows, XATTN_DIM), BF16),
                   jax.ShapeDtypeStruct((rows, XATTN_DIM), BF16)],
        compiler_params=pltpu.CompilerParams(dimension_semantics=("arbitrary",)),
        name="mem_kv",
    )(mem2, g, w)


def _attn_body(q_ref, k_ref, v_ref, o_ref, s_a, s_b, mc_a, mc_b, m_ref, l_ref, acc_ref, *, S, tq):
    s_bufs = (s_a, s_b)
    mc_bufs = (mc_a, mc_b)
    mxu_row_split = 2

    def q_tile(qi, carry):
        q0 = pl.multiple_of(qi * tq, tq)
        m_ref[...] = jnp.full((tq, LANES), NEG_INF, F32)
        l_ref[...] = jnp.zeros((tq, LANES), F32)
        acc_ref[...] = jnp.zeros((tq, V_HEAD_DIM), F32)

        def scores(kb, slot, masked):
            c0 = pl.multiple_of(kb * tq, tq)
            s = _dot_nt(q_ref[0, 0, pl.ds(q0, tq), :], k_ref[0, 0, pl.ds(c0, tq), :])
            if masked:
                ri = lax.broadcasted_iota(jnp.int32, (tq, tq), 0)
                ci = lax.broadcasted_iota(jnp.int32, (tq, tq), 1)
                s = jnp.where(ci - ri <= q0 - c0, s, NEG_INF)
            s_bufs[slot][...] = s
            mc_bufs[slot][...] = jnp.broadcast_to(jnp.max(s, axis=1, keepdims=True), (tq, LANES))

        def accumulate(kb, slot):
            c0 = pl.multiple_of(kb * tq, tq)
            m_prev = m_ref[...]
            m_new = jnp.maximum(m_prev, mc_bufs[slot][...])
            alpha = jnp.exp2(m_prev - m_new)
            p = jnp.exp2(s_bufs[slot][...] - jnp.concatenate([m_new] * (tq // LANES), axis=1))
            psum = p[:, 0:LANES]
            for c in range(1, tq // LANES):
                psum = psum + p[:, c * LANES:(c + 1) * LANES]
            l_ref[...] = alpha * l_ref[...] + psum
            m_ref[...] = m_new
            pb = p.astype(BF16)
            v = v_ref[0, 0, pl.ds(c0, tq), :]
            h = tq // mxu_row_split
            for i in range(mxu_row_split):
                rows = slice(i * h, (i + 1) * h)
                acc_ref[rows, :] = alpha[rows, :] * acc_ref[rows, :] + _dot(pb[rows, :], v)

        scores(0, 0, True)

        def pair(i, c):
            scores(2 * i + 1, 1, False)
            accumulate(2 * i, 0)
            scores(2 * i + 2, 0, False)
            accumulate(2 * i + 1, 1)
            return c

        lax.fori_loop(0, jnp.maximum(qi - 1, 0) // 2, pair, 0)

        @pl.when(qi == 0)
        def _():
            accumulate(0, 0)

        @pl.when(lax.rem(qi, 2) == 1)
        def _():
            scores(qi, 1, True)
            accumulate(qi - 1, 0)
            accumulate(qi, 1)

        @pl.when(jnp.logical_and(qi >= 2, lax.rem(qi, 2) == 0))
        def _():
            scores(qi - 1, 1, False)
            accumulate(qi - 2, 0)
            scores(qi, 0, True)
            accumulate(qi - 1, 1)
            accumulate(qi, 0)

        l = jnp.sum(l_ref[...], axis=1, keepdims=True)
        o_ref[0, pl.ds(q0, tq), :] = (acc_ref[...] / l).astype(BF16)
        return carry

    lax.fori_loop(0, S // tq, q_tile, 0)


def _mla_attn(q, k, v, *, tq):
    B, H, S, _ = q.shape
    per_head = lambda b, h: (b, h, 0, 0)
    return pl.pallas_call(
        functools.partial(_attn_body, S=S, tq=tq),
        grid=(B, H),
        in_specs=[pl.BlockSpec((1, 1, S, QK_PAD_DIM), per_head),
                  pl.BlockSpec((1, 1, S, QK_PAD_DIM), per_head),
                  pl.BlockSpec((1, 1, S, V_HEAD_DIM), per_head)],
        out_specs=pl.BlockSpec((1, S, V_HEAD_DIM), lambda b, h: (b, 0, h)),
        out_shape=jax.ShapeDtypeStruct((B, S, H * V_HEAD_DIM), BF16),
        scratch_shapes=[pltpu.VMEM((tq, tq), F32), pltpu.VMEM((tq, tq), F32),
                        pltpu.VMEM((tq, LANES), F32), pltpu.VMEM((tq, LANES), F32),
                        pltpu.VMEM((tq, LANES), F32), pltpu.VMEM((tq, LANES), F32),
                        pltpu.VMEM((tq, V_HEAD_DIM), F32)],
        compiler_params=pltpu.CompilerParams(dimension_semantics=("arbitrary", "arbitrary"),
                                             vmem_limit_bytes=VMEM_LIMIT_BYTES),
        name="mla_attn",
    )(q, k, v)


def _merge_body(x_ref, ypool_ref, ymla_ref, xq_ref, gates_ref, kmem_ref, vmem_ref,
                wbp_ref, wbm_ref, wbx_ref, wout_ref, ffng_ref, wr_ref, br_ref,
                x1_ref, h2_ref, wts_ref, loc_ref, cnt_tab_ref, carry_tab_ref, counts_ref, carry_ref, *, tm):
    @pl.when(pl.program_id(0) == 0)
    def _():
        carry_ref[...] = jnp.zeros((N_EXPERTS, LANES), F32)

    xq = xq_ref[...]
    parts = []
    for h in range(XATTN_HEADS):
        cols = slice(h * XATTN_HEAD_DIM, (h + 1) * XATTN_HEAD_DIM)
        s = _dot_nt(xq[:, cols], kmem_ref[:, cols]) * (XATTN_HEAD_DIM ** -0.5)
        e = jnp.exp(s - jnp.max(s, axis=1, keepdims=True))
        p = e / jnp.sum(e, axis=1, keepdims=True)
        parts.append(_dot(p.astype(BF16), vmem_ref[:, cols]))
    ymem = jnp.concatenate(parts, axis=1).astype(BF16)

    gates = gates_ref[...].astype(F32)
    merged = (gates[:, 0:D_MODEL] * _dot(ypool_ref[...], wbp_ref[...])
              + gates[:, D_MODEL:2 * D_MODEL] * _dot(ymla_ref[...], wbm_ref[...])
              + gates[:, 2 * D_MODEL:3 * D_MODEL] * _dot(ymem, wbx_ref[...]))
    x1 = x_ref[...] + _dot(merged.astype(BF16), wout_ref[...])
    x1_ref[...] = x1
    h2 = _rms(x1, ffng_ref[...]).astype(BF16)
    h2_ref[...] = h2

    lt = _dot_nt(wr_ref[...], h2) + br_ref[...]
    gl = lt[N_EXPERTS:N_EXPERTS + N_GROUPS, :]
    gmax = jnp.max(gl, axis=0, keepdims=True)
    r4 = lax.broadcasted_iota(jnp.int32, (N_GROUPS, tm), 0).astype(F32)
    gidx = jnp.min(jnp.where(gl == gmax, r4, float(N_GROUPS)), axis=0, keepdims=True)
    pg = 1.0 / jnp.sum(jnp.exp(gl - gmax), axis=0, keepdims=True)
    esel = lt[0:EXPERTS_PER_GROUP, :]
    for g in range(1, N_GROUPS):
        esel = jnp.where(gidx == float(g), lt[g * EXPERTS_PER_GROUP:(g + 1) * EXPERTS_PER_GROUP, :], esel)
    r8 = lax.broadcasted_iota(jnp.int32, (EXPERTS_PER_GROUP, tm), 0).astype(F32)
    m1 = jnp.max(esel, axis=0, keepdims=True)
    i1 = jnp.min(jnp.where(esel == m1, r8, float(EXPERTS_PER_GROUP)), axis=0, keepdims=True)
    rest = jnp.where(r8 == i1, -jnp.inf, esel)
    m2 = jnp.max(rest, axis=0, keepdims=True)
    i2 = jnp.min(jnp.where(rest == m2, r8, float(EXPERTS_PER_GROUP)), axis=0, keepdims=True)
    e2 = jnp.exp(m2 - m1)
    den = 1.0 + e2
    wts_ref[0:1, :] = pg / den
    wts_ref[1:2, :] = pg * e2 / den
    ex1 = gidx * float(EXPERTS_PER_GROUP) + i1
    ex2 = gidx * float(EXPERTS_PER_GROUP) + i2

    r32 = lax.broadcasted_iota(jnp.int32, (N_EXPERTS, tm), 0).astype(F32)
    is1 = r32 == ex1
    is2 = r32 == ex2
    member = jnp.where(is1 | is2, 1.0, 0.0)
    upper = jnp.where(lax.broadcasted_iota(jnp.int32, (tm, tm), 0)
                      <= lax.broadcasted_iota(jnp.int32, (tm, tm), 1), 1.0, 0.0).astype(BF16)
    incl = _dot(member.astype(BF16), upper)
    run = jnp.floor((jnp.sum(member, axis=1, keepdims=True) + (RUN_ALIGN - 1)) / RUN_ALIGN) * RUN_ALIGN
    rcol = lax.broadcasted_iota(jnp.int32, (N_EXPERTS, 1), 0)
    run_start = jnp.zeros((N_EXPERTS, 1), F32)
    for e in range(N_EXPERTS - 1):
        run_start = run_start + jnp.where(rcol > e, run[e:e + 1, :], 0.0)
    pos = incl - 1.0 + run_start
    loc_ref[0:1, :] = jnp.sum(jnp.where(is1, pos, 0.0), axis=0, keepdims=True).astype(jnp.int32)
    loc_ref[1:2, :] = jnp.sum(jnp.where(is2, pos, 0.0), axis=0, keepdims=True).astype(jnp.int32)
    carry = carry_ref[...]
    total = carry + run
    cnt_tab_ref[...] = jnp.broadcast_to(run, (N_EXPERTS, LANES)).astype(jnp.int32)
    carry_tab_ref[...] = carry.astype(jnp.int32)
    carry_ref[...] = total
    counts_ref[...] = total.astype(jnp.int32)


def _merge(x2, ypool, ymla, xq, gates, kmem, vmem, wbp, wbm, wbx, wout, ffng, wr, br, *, B, S, tm, mem_len):
    T = B * S
    tps = S // tm
    row = lambda i: (i, 0)
    lane = lambda i: (0, i)
    memb = lambda i: (i // tps, 0)
    return pl.pallas_call(
        functools.partial(_merge_body, tm=tm),
        grid=(T // tm,),
        in_specs=[
            pl.BlockSpec((tm, D_MODEL), row),
            pl.BlockSpec((tm, POOL_DIM), row),
            pl.BlockSpec((tm, MLA_HEADS * V_HEAD_DIM), row),
            pl.BlockSpec((tm, XATTN_DIM), row),
            pl.BlockSpec((tm, N_BRANCHES * D_MODEL), row),
            pl.BlockSpec((mem_len, XATTN_DIM), memb),
            pl.BlockSpec((mem_len, XATTN_DIM), memb),
            _const_spec((POOL_DIM, D_MODEL)),
            _const_spec((MLA_HEADS * V_HEAD_DIM, D_MODEL)),
            _const_spec((XATTN_DIM, D_MODEL)),
            _const_spec((D_MODEL, D_MODEL)),
            _const_spec((1, D_MODEL)),
            _const_spec((ROUTER_ROWS, D_MODEL)),
            _const_spec((ROUTER_ROWS, 1)),
        ],
        out_specs=[
            pl.BlockSpec((tm, D_MODEL), row),
            pl.BlockSpec((tm, D_MODEL), row),
            pl.BlockSpec((2, tm), lane),
            pl.BlockSpec((2, tm), lane),
            pl.BlockSpec((N_EXPERTS, LANES), row),
            pl.BlockSpec((N_EXPERTS, LANES), row),
            pl.BlockSpec((N_EXPERTS, LANES), lambda i: (0, 0)),
        ],
        out_shape=[
            jax.ShapeDtypeStruct((T, D_MODEL), F32),
            jax.ShapeDtypeStruct((T, D_MODEL), BF16),
            jax.ShapeDtypeStruct((2, T), F32),
            jax.ShapeDtypeStruct((2, T), jnp.int32),
            jax.ShapeDtypeStruct((T // tm * N_EXPERTS, LANES), jnp.int32),
            jax.ShapeDtypeStruct((T // tm * N_EXPERTS, LANES), jnp.int32),
            jax.ShapeDtypeStruct((N_EXPERTS, LANES), jnp.int32),
        ],
        scratch_shapes=[pltpu.VMEM((N_EXPERTS, LANES), F32)],
        compiler_params=pltpu.CompilerParams(dimension_semantics=("arbitrary",),
                                             vmem_limit_bytes=VMEM_LIMIT_BYTES),
        name="merge",
    )(x2, ypool, ymla, xq, gates, kmem, vmem, wbp, wbm, wbx, wout, ffng, wr, br)


def _moe_pos_body(counts_ref, meta_ref):
    shift = int(math.log2(MOE_ROWS))
    cnt = counts_ref[...]
    padded = lax.shift_left(lax.shift_right_logical(cnt + (MOE_ROWS - 1), shift), shift)
    r32 = lax.broadcasted_iota(jnp.int32, (N_EXPERTS, LANES), 0)
    pad_start = jnp.zeros((N_EXPERTS, LANES), jnp.int32)
    for e in range(N_EXPERTS - 1):
        pad_start = pad_start + jnp.where(r32 > e, padded[e:e + 1, :], 0)
    pad_end = pad_start + padded

    lane = lax.broadcasted_iota(jnp.int32, (1, META_LANES), 1)
    block_row = lane * MOE_ROWS
    blk_e = jnp.zeros((1, META_LANES), jnp.int32)
    pe_row = jnp.zeros((1, META_LANES), jnp.int32)
    for e in range(N_EXPERTS):
        pe = pad_end[e:e + 1, 0:1]
        blk_e = blk_e + jnp.where(pe <= block_row, 1, 0)
        pe_row = pe_row + jnp.where(lane == META_PAD_END + e, pe, 0)
    blk_e = jnp.minimum(blk_e, N_EXPERTS - 1)
    nact = lax.shift_right_logical(pad_end[N_EXPERTS - 1:N_EXPERTS, 0:1], shift)
    meta = jnp.where(lane < META_PAD_END, blk_e, pe_row)
    meta_ref[...] = jnp.where(lane == META_NACT, nact, meta)


def _moe_pos(counts):
    full = lambda shape: pl.BlockSpec(shape, lambda i: (0,) * len(shape))
    return pl.pallas_call(
        _moe_pos_body,
        grid=(1,),
        in_specs=[full((N_EXPERTS, LANES))],
        out_specs=full((1, META_LANES)),
        out_shape=jax.ShapeDtypeStruct((1, META_LANES), jnp.int32),
        compiler_params=pltpu.CompilerParams(dimension_semantics=("arbitrary",)),
        name="moe_pos",
    )(counts)


def _pack_bf16_pairs(x):
    lo = pltpu.bitcast(x[:, 0:PACKED_DIM], U32)
    hi = pltpu.bitcast(x[:, PACKED_DIM:D_MODEL], U32)
    return hi | lax.shift_right_logical(lo, jnp.uint32(16))


def _unpack_bf16_pairs(w):
    lo = pltpu.bitcast(lax.shift_left(w, jnp.uint32(16)), F32)
    hi = pltpu.bitcast(w & jnp.uint32(0xFFFF0000), F32)
    return jnp.concatenate([lo, hi], axis=1)


def _loc_rows(tm):
    return 2 * tm + N_EXPERTS * RUN_ALIGN


def _run_copies(tile, cnt_ref, carry_ref, meta_ref, make_copy, max_run):
    def per_expert(e, local):
        n = cnt_ref[tile * N_EXPERTS + e]
        start = jnp.where(e == 0, 0, meta_ref[META_PAD_END + jnp.maximum(e - 1, 0)])
        glob = start + carry_ref[tile * N_EXPERTS + e]
        bit = max_run
        while bit >= RUN_ALIGN:
            take = n & bit

            @pl.when(take != 0)
            def _(local=local, glob=glob, bit=bit):
                make_copy(pl.multiple_of(local, RUN_ALIGN), pl.multiple_of(glob, RUN_ALIGN), bit).start()

            local = local + take
            glob = glob + take
            bit //= 2
        return local

    return pl.multiple_of(lax.fori_loop(0, N_EXPERTS, per_expert, 0), RUN_ALIGN)


def _dispatch_body(meta_ref, cnt_ref, carry_ref, loc_ref, h2_ref, xs_ref, xloc_ref, zero_ref, sem, zsem,
                   *, tm, n_blocks):
    tile = pl.program_id(0)

    def pad_copy(e):
        end = pl.multiple_of(meta_ref[META_PAD_END + e], MOE_ROWS)
        return pltpu.make_async_copy(zero_ref, xs_ref.at[pl.ds(end - MOE_ROWS, MOE_ROWS)], zsem)

    def has_rows(e):
        prev = jnp.where(e == 0, 0, meta_ref[META_PAD_END + jnp.maximum(e - 1, 0)])
        return meta_ref[META_PAD_END + e] > prev

    @pl.when(tile == 0)
    def _():
        zero_ref[...] = jnp.zeros((MOE_ROWS, PACKED_DIM), U32)

        def start(e, c):
            @pl.when(has_rows(e))
            def _():
                pad_copy(e).start()
            return c

        def wait(e, c):
            @pl.when(has_rows(e))
            def _():
                pad_copy(e).wait()
            return c

        def tail_copy(b):
            return pltpu.make_async_copy(
                zero_ref, xs_ref.at[pl.ds(pl.multiple_of(b * MOE_ROWS, MOE_ROWS), MOE_ROWS)], zsem)

        def tail_start(b, c):
            tail_copy(b).start()
            return c

        def tail_wait(b, c):
            tail_copy(b).wait()
            return c

        nact = meta_ref[META_NACT]
        lax.fori_loop(0, N_EXPERTS, start, 0)
        lax.fori_loop(nact, n_blocks, tail_start, 0)
        lax.fori_loop(0, N_EXPERTS, wait, 0)
        lax.fori_loop(nact, n_blocks, tail_wait, 0)

    r = lax.broadcasted_iota(jnp.int32, (_loc_rows(tm), tm), 0)
    onehot = jnp.where((r == loc_ref[0:1, :]) | (r == loc_ref[1:2, :]), 1.0, 0.0).astype(BF16)
    xloc_ref[...] = _pack_bf16_pairs(_dot(onehot, h2_ref[...]))

    def make_copy(local, glob, n):
        return pltpu.make_async_copy(xloc_ref.at[pl.ds(local, n)], xs_ref.at[pl.ds(glob, n)], sem)

    total = _run_copies(tile, cnt_ref, carry_ref, meta_ref, make_copy, tm)
    pltpu.make_async_copy(xloc_ref.at[pl.ds(0, total)], xs_ref.at[pl.ds(0, total)], sem).wait()


def _dispatch(meta1, cnt_tab, carry_tab, loc, h2, *, R, tm):
    T = h2.shape[0]
    return pl.pallas_call(
        functools.partial(_dispatch_body, tm=tm, n_blocks=R // MOE_ROWS),
        grid_spec=pltpu.PrefetchScalarGridSpec(
            num_scalar_prefetch=3,
            grid=(T // tm,),
            in_specs=[pl.BlockSpec((2, tm), lambda i, *_: (0, i)),
                      pl.BlockSpec((tm, D_MODEL), lambda i, *_: (i, 0))],
            out_specs=pl.BlockSpec(memory_space=pl.ANY),
            scratch_shapes=[pltpu.VMEM((_loc_rows(tm), PACKED_DIM), U32),
                            pltpu.VMEM((MOE_ROWS, PACKED_DIM), U32),
                            pltpu.SemaphoreType.DMA, pltpu.SemaphoreType.DMA],
        ),
        out_shape=jax.ShapeDtypeStruct((R, PACKED_DIM), U32),
        compiler_params=pltpu.CompilerParams(dimension_semantics=("arbitrary",),
                                             vmem_limit_bytes=VMEM_LIMIT_BYTES),
        name="dispatch",
    )(meta1, cnt_tab, carry_tab, loc, h2)


def _moe_ffn_body(meta_ref, xs_ref, wg_ref, wu_ref, wd_ref, ys_ref):
    active = pl.program_id(0) < meta_ref[META_NACT]

    @pl.when(active)
    def _():
        x = _unpack_bf16_pairs(xs_ref[...]).astype(BF16)
        g = _dot(x, wg_ref[0].astype(BF16))
        u = _dot(x, wu_ref[0].astype(BF16))
        a = (g * jax.nn.sigmoid(g)) * u
        y = _dot(a.astype(BF16), wd_ref[0].astype(BF16))
        ys_ref[...] = _pack_bf16_pairs(y.astype(BF16).astype(F32))

    @pl.when(jnp.logical_not(active))
    def _():
        ys_ref[...] = jnp.zeros((MOE_ROWS, PACKED_DIM), U32)


def _moe_ffn(meta1, xs, wg, wu, wd):
    R = xs.shape[0]
    blk = lambda b, m: (jnp.minimum(b, m[META_NACT] - 1), 0)
    wsel = lambda b, m: (m[jnp.minimum(b, m[META_NACT] - 1)], 0, 0)
    return pl.pallas_call(
        _moe_ffn_body,
        grid_spec=pltpu.PrefetchScalarGridSpec(
            num_scalar_prefetch=1,
            grid=(R // MOE_ROWS,),
            in_specs=[pl.BlockSpec((MOE_ROWS, PACKED_DIM), blk),
                      pl.BlockSpec((1, D_MODEL, D_EXPERT), wsel),
                      pl.BlockSpec((1, D_MODEL, D_EXPERT), wsel),
                      pl.BlockSpec((1, D_EXPERT, D_MODEL), wsel)],
            out_specs=pl.BlockSpec((MOE_ROWS, PACKED_DIM), lambda b, m: (b, 0)),
        ),
        out_shape=jax.ShapeDtypeStruct((R, PACKED_DIM), U32),
        compiler_params=pltpu.CompilerParams(dimension_semantics=("arbitrary",),
                                             vmem_limit_bytes=VMEM_LIMIT_BYTES),
        name="moe_ffn",
    )(meta1, xs, wg, wu, wd)


def _combine_body(meta_ref, cnt_ref, carry_ref, x1_ref, loct_ref, wt_ref, fg_ref, ys_ref, out_ref,
                  yloc_ref, sem, *, tm):
    tile = pl.program_id(0)

    @pl.when(tile == 0)
    def _():
        yloc_ref[...] = jnp.zeros(yloc_ref.shape, U32)

    def make_copy(local, glob, n):
        return pltpu.make_async_copy(ys_ref.at[pl.ds(glob, n)], yloc_ref.at[pl.ds(local, n)], sem)

    total = _run_copies(tile, cnt_ref, carry_ref, meta_ref, make_copy, tm)
    pltpu.make_async_copy(ys_ref.at[pl.ds(0, total)], yloc_ref.at[pl.ds(0, total)], sem).wait()
    y = _unpack_bf16_pairs(yloc_ref[...]).astype(BF16)
    c = lax.broadcasted_iota(jnp.int32, (tm, _loc_rows(tm)), 1)
    loct = loct_ref[...]
    w = wt_ref[...]
    ya = _dot(jnp.where(c == loct[:, 0:1], 1.0, 0.0).astype(BF16), y)
    yb = _dot(jnp.where(c == loct[:, 1:2], 1.0, 0.0).astype(BF16), y)
    out_ref[...] = _rms(x1_ref[...] + w[:, 0:1] * ya + w[:, 1:2] * yb, fg_ref[...])


def _combine(meta1, cnt_tab, carry_tab, x1, loc_t, wts_t, fg, ys, *, tm):
    T = x1.shape[0]
    row = lambda i, *_: (i, 0)
    return pl.pallas_call(
        functools.partial(_combine_body, tm=tm),
        grid_spec=pltpu.PrefetchScalarGridSpec(
            num_scalar_prefetch=3,
            grid=(T // tm,),
            in_specs=[pl.BlockSpec((tm, D_MODEL), row),
                      pl.BlockSpec((tm, 2), row),
                      pl.BlockSpec((tm, 2), row),
                      pl.BlockSpec((1, D_MODEL), lambda i, *_: (0, 0)),
                      pl.BlockSpec(memory_space=pl.ANY)],
            out_specs=pl.BlockSpec((tm, D_MODEL), row),
            scratch_shapes=[pltpu.VMEM((_loc_rows(tm), PACKED_DIM), U32), pltpu.SemaphoreType.DMA],
        ),
        out_shape=jax.ShapeDtypeStruct((T, D_MODEL), F32),
        compiler_params=pltpu.CompilerParams(dimension_semantics=("arbitrary",),
                                             vmem_limit_bytes=VMEM_LIMIT_BYTES),
        name="combine",
    )(meta1, cnt_tab, carry_tab, x1, loc_t, wts_t, fg, ys)


def _tile(n, t):
    t = min(n, t)
    assert n % t == 0, (n, t)
    return t


def kernel(x, mem, positions, mix_norm_g, w_in, gate_b, q_norm_g, w_uq, kv_norm_g, w_uk, w_uv, pool_w, pool_scale, mem_norm_g, w_mem_kv, w_br_pool, w_br_mla, w_br_mem, w_out, ffn_norm_g, w_router_group, b_router_group, w_router_expert, b_router_expert, w_gate_e, w_up_e, w_down_e, final_norm_g):
    B, S, D = x.shape
    assert D == D_MODEL and mix_norm_g.shape[0] == 1
    T = B * S
    mem_len = mem.shape[1]
    tm = _tile(S, 512)
    l = 0

    wi = w_in[l]
    o_qd, o_kv, o_kr, o_xq, o_gate = 512, 896, 1152, 1216, 1728
    win_p = jnp.concatenate(
        [wi[:, 0:o_qd], wi[:, o_qd:o_kv], wi[:, o_kv:o_kr], wi[:, o_xq:o_gate], wi[:, o_gate:],
         wi[:, o_kr:o_xq], jnp.zeros((D_MODEL, LANES - QK_ROPE_DIM), wi.dtype)], axis=1).astype(BF16)
    wuq_p = jnp.pad(w_uq[l].reshape(Q_LORA_RANK, MLA_HEADS, QK_NOPE_DIM + QK_ROPE_DIM),
                    ((0, 0), (0, 0), (0, QK_PAD_DIM - QK_NOPE_DIM - QK_ROPE_DIM))
                    ).reshape(Q_LORA_RANK, MLA_HEADS * QK_PAD_DIM).astype(BF16)
    inv_freq = 1.0 / (ROPE_THETA ** (jnp.arange(0, QK_ROPE_DIM, 2, dtype=F32) / QK_ROPE_DIM))
    invf = jnp.concatenate([inv_freq, inv_freq, jnp.zeros((LANES - QK_ROPE_DIM,), F32)])[None, :]
    wr = jnp.concatenate([w_router_expert[l], w_router_group[l],
                          jnp.zeros((D_MODEL, ROUTER_ROWS - N_EXPERTS - N_GROUPS), F32)], axis=1).T.astype(BF16)
    br = jnp.concatenate([b_router_expert[l], b_router_group[l],
                          jnp.zeros((ROUTER_ROWS - N_EXPERTS - N_GROUPS,), F32)])[:, None].astype(F32)
    x2 = x.reshape(T, D_MODEL)
    pos2 = positions.reshape(T, 1)

    ypool, xq, gates, q, k, v = _mixer_in(
        x2, pos2, invf, mix_norm_g[l][None, :], win_p, gate_b[l], q_norm_g[l][None, :], wuq_p,
        kv_norm_g[l][None, :], w_uk[l].astype(BF16), w_uv[l].astype(BF16), pool_w[l].astype(BF16),
        pool_scale[l][None, :], B=B, S=S, tm=tm)
    kmem, vmem = _mem_kv(mem.reshape(B * mem_len, D_MODEL), mem_norm_g[l][None, :], w_mem_kv[l].astype(BF16))
    ymla = _mla_attn(q, k, v, tq=tm).reshape(T, MLA_HEADS * V_HEAD_DIM)
    x1, h2, wts, loc, cnt_tab, carry_tab, counts = _merge(
        x2, ypool, ymla, xq, gates, kmem, vmem, w_br_pool[l].astype(BF16), w_br_mla[l].astype(BF16),
        w_br_mem[l].astype(BF16), w_out[l].astype(BF16), ffn_norm_g[l][None, :], wr, br,
        B=B, S=S, tm=tm, mem_len=mem_len)

    R = 2 * T + (T // tm) * N_EXPERTS * RUN_ALIGN + N_EXPERTS * MOE_ROWS
    assert R % MOE_ROWS == 0 and R // MOE_ROWS <= META_PAD_END
    meta1 = _moe_pos(counts).reshape(META_LANES)
    cnt1 = cnt_tab[:, 0]
    carry1 = carry_tab[:, 0]
    xs = _dispatch(meta1, cnt1, carry1, loc, h2, R=R, tm=tm)
    ys = _moe_ffn(meta1, xs, w_gate_e[l], w_up_e[l], w_down_e[l])
    out = _combine(meta1, cnt1, carry1, x1, loc.T, wts.T, final_norm_g[None, :], ys, tm=tm)
    return out.reshape(B, S, D_MODEL)
```

```python
import functools
import math

import jax
import jax.numpy as jnp
from jax import lax
from jax.experimental import pallas as pl
from jax.experimental.pallas import tpu as pltpu

D_MODEL = 1024
POOL_WINDOWS = (2, 4, 8, 16)
POOL_GROUP_DIM = 128
POOL_DIM = 512
MLA_HEADS = 8
QK_NOPE_DIM = 128
QK_ROPE_DIM = 64
V_HEAD_DIM = 128
Q_LORA_RANK = 384
KV_LORA_RANK = 256
ROPE_THETA = 10000.0
XATTN_HEADS = 4
XATTN_HEAD_DIM = 128
XATTN_DIM = 512
N_BRANCHES = 3
N_GROUPS = 4
EXPERTS_PER_GROUP = 8
N_EXPERTS = 32
D_EXPERT = 256
RMS_EPS = 1e-6
NEG_INF = -1e30

LANES = 128
QK_PAD_DIM = 2 * LANES
POOL_HALO = 16
MOE_ROWS = 512
MOE_CHUNK = 256
RUN_ALIGN = 8
PACKED_DIM = D_MODEL // 2
ROUTER_ROWS = 40
META_LANES = 256
META_PAD_END = 192
META_NACT = 255
VMEM_LIMIT_BYTES = 56 * 1024 * 1024

IN_POOL, IN_QD, IN_KV, IN_XQ, IN_GATE, IN_KR, IN_END = 0, 512, 896, 1152, 1664, 4736, 4864

F32 = jnp.float32
BF16 = jnp.bfloat16
U32 = jnp.uint32


def _rms(x, g):
    ms = jnp.mean(x * x, axis=-1, keepdims=True)
    return (x * lax.rsqrt(ms + RMS_EPS)) * g


def _dot(a, b):
    return jnp.dot(a, b, preferred_element_type=F32)


def _dot_nt(a, b):
    return lax.dot_general(a, b, (((1,), (1,)), ((), ())), preferred_element_type=F32)


def _const_spec(shape):
    nd = len(shape)
    return pl.BlockSpec(shape, lambda *_: (0,) * nd, pipeline_mode=pl.Buffered(1))


def _mixer_in_body(x_ref, pos_ref, invf_ref, mixg_ref, win_ref, gateb_ref, qg_ref, wuq_ref,
                   kvg_ref, wuk_ref, wuv_ref, poolw_ref, pools_ref,
                   ypool_ref, xq_ref, gates_ref, q_ref, k_ref, v_ref, ext_ref,
                   *, tm, tiles_per_seq, q_scale):
    si = lax.rem(pl.program_id(0), tiles_per_seq)
    hb = _rms(x_ref[...], mixg_ref[...]).astype(BF16)

    u = _dot(hb, win_ref[:, IN_POOL:IN_QD])

    @pl.when(si == 0)
    def _():
        ext_ref[0:POOL_HALO, :] = jnp.zeros((POOL_HALO, POOL_DIM), F32)

    ext_ref[POOL_HALO:POOL_HALO + tm, :] = u
    t_seq = lax.broadcasted_iota(jnp.int32, (tm, 1), 0) + si * tm
    for g, w in enumerate(POOL_WINDOWS):
        lo = g * POOL_GROUP_DIM
        hi = lo + POOL_GROUP_DIM
        acc = u[:, lo:hi]
        for j in range(1, w):
            acc = acc + ext_ref[POOL_HALO - j:POOL_HALO - j + tm, lo:hi]
        cnt = jnp.minimum(t_seq + 1, w).astype(F32)
        p = acc / cnt - u[:, lo:hi]
        y = _dot(p.astype(BF16), poolw_ref[g]) * pools_ref[:, lo:hi]
        ypool_ref[:, lo:hi] = y.astype(BF16)
    ext_ref[0:POOL_HALO, :] = ext_ref[tm:tm + POOL_HALO, :]

    ang = pos_ref[...].astype(F32) * invf_ref[...]
    cos = jnp.cos(ang)
    sin = jnp.sin(ang)
    first_half = lax.broadcasted_iota(jnp.int32, (tm, LANES), 1) < (QK_ROPE_DIM // 2)
    sin_signed = jnp.where(first_half, -sin, sin)

    def rope(r):
        swapped = jnp.where(first_half, pltpu.roll(r, LANES - QK_ROPE_DIM // 2, 1),
                            pltpu.roll(r, QK_ROPE_DIM // 2, 1))
        return r * cos + swapped * sin_signed

    cq = _rms(_dot(hb, win_ref[:, IN_QD:IN_KV]), qg_ref[...]).astype(BF16)
    for h in range(MLA_HEADS):
        qh = _dot(cq, wuq_ref[:, h * QK_PAD_DIM:(h + 1) * QK_PAD_DIM])
        q_ref[0, h, :, 0:LANES] = (qh[:, 0:LANES] * q_scale).astype(BF16)
        q_ref[0, h, :, LANES:QK_PAD_DIM] = (rope(qh[:, LANES:QK_PAD_DIM]) * q_scale).astype(BF16)

    ckv = _rms(_dot(hb, win_ref[:, IN_KV:IN_XQ]), kvg_ref[...]).astype(BF16)
    kr = rope(_dot(hb, win_ref[:, IN_KR:IN_END])).astype(BF16)
    for hp in range(MLA_HEADS // 2):
        cols = slice(hp * 2 * LANES, (hp + 1) * 2 * LANES)
        kn = _dot(ckv, wuk_ref[:, cols]).astype(BF16)
        vv = _dot(ckv, wuv_ref[:, cols]).astype(BF16)
        for j in range(2):
            h = 2 * hp + j
            k_ref[0, h, :, 0:LANES] = kn[:, j * LANES:(j + 1) * LANES]
            k_ref[0, h, :, LANES:QK_PAD_DIM] = kr
            v_ref[0, h] = vv[:, j * LANES:(j + 1) * LANES]

    xq_ref[...] = _dot(hb, win_ref[:, IN_XQ:IN_GATE]).astype(BF16)
    for c in range(N_BRANCHES):
        gl = _dot(hb, win_ref[:, IN_GATE + c * D_MODEL:IN_GATE + (c + 1) * D_MODEL])
        gates_ref[:, c * D_MODEL:(c + 1) * D_MODEL] = jax.nn.sigmoid(gl + gateb_ref[c:c + 1, :]).astype(BF16)


def _mixer_in(x2, pos2, invf, mixg, win_p, gate_b, qg, wuq_p, kvg, wuk, wuv, pool_w, pool_s, *, B, S, tm):
    T = B * S
    tps = S // tm
    q_scale = (QK_NOPE_DIM + QK_ROPE_DIM) ** -0.5 * math.log2(math.e)
    body = functools.partial(_mixer_in_body, tm=tm, tiles_per_seq=tps, q_scale=q_scale)
    row = lambda i: (i, 0)
    head = lambda i: (i // tps, 0, i % tps, 0)
    return pl.pallas_call(
        body,
        grid=(T // tm,),
        in_specs=[
            pl.BlockSpec((tm, D_MODEL), row),
            pl.BlockSpec((tm, 1), row),
            _const_spec((1, LANES)),
            _const_spec((1, D_MODEL)),
            _const_spec((D_MODEL, IN_END)),
            _const_spec((N_BRANCHES, D_MODEL)),
            _const_spec((1, Q_LORA_RANK)),
            _const_spec((Q_LORA_RANK, MLA_HEADS * QK_PAD_DIM)),
            _const_spec((1, KV_LORA_RANK)),
            _const_spec((KV_LORA_RANK, MLA_HEADS * QK_NOPE_DIM)),
            _const_spec((KV_LORA_RANK, MLA_HEADS * V_HEAD_DIM)),
            _const_spec((len(POOL_WINDOWS), POOL_GROUP_DIM, POOL_GROUP_DIM)),
            _const_spec((1, POOL_DIM)),
        ],
        out_specs=[
            pl.BlockSpec((tm, POOL_DIM), row),
            pl.BlockSpec((tm, XATTN_DIM), row),
            pl.BlockSpec((tm, N_BRANCHES * D_MODEL), row),
            pl.BlockSpec((1, MLA_HEADS, tm, QK_PAD_DIM), head),
            pl.BlockSpec((1, MLA_HEADS, tm, QK_PAD_DIM), head),
            pl.BlockSpec((1, MLA_HEADS, tm, V_HEAD_DIM), head),
        ],
        out_shape=[
            jax.ShapeDtypeStruct((T, POOL_DIM), BF16),
            jax.ShapeDtypeStruct((T, XATTN_DIM), BF16),
            jax.ShapeDtypeStruct((T, N_BRANCHES * D_MODEL), BF16),
            jax.ShapeDtypeStruct((B, MLA_HEADS, S, QK_PAD_DIM), BF16),
            jax.ShapeDtypeStruct((B, MLA_HEADS, S, QK_PAD_DIM), BF16),
            jax.ShapeDtypeStruct((B, MLA_HEADS, S, V_HEAD_DIM), BF16),
        ],
        scratch_shapes=[pltpu.VMEM((tm + POOL_HALO, POOL_DIM), F32)],
        compiler_params=pltpu.CompilerParams(dimension_semantics=("arbitrary",),
                                             vmem_limit_bytes=VMEM_LIMIT_BYTES),
        name="mixer_in",
    )(x2, pos2, invf, mixg, win_p, gate_b, qg, wuq_p, kvg, wuk, wuv, pool_w, pool_s)


def _mem_kv_body(mem_ref, g_ref, w_ref, k_ref, v_ref):
    mb = _rms(mem_ref[...], g_ref[...]).astype(BF16)
    kv = _dot(mb, w_ref[...])
    k_ref[...] = kv[:, 0:XATTN_DIM].astype(BF16)
    v_ref[...] = kv[:, XATTN_DIM:2 * XATTN_DIM].astype(BF16)


def _mem_kv(mem2, g, w):
    rows = mem2.shape[0]
    tr = min(rows, 512)
    return pl.pallas_call(
        _mem_kv_body,
        grid=(rows // tr,),
        in_specs=[pl.BlockSpec((tr, D_MODEL), lambda i: (i, 0)),
                  _const_spec((1, D_MODEL)),
                  _const_spec((D_MODEL, 2 * XATTN_DIM))],
        out_specs=[pl.BlockSpec((tr, XATTN_DIM), lambda i: (i, 0)),
                   pl.BlockSpec((tr, XATTN_DIM), lambda i: (i, 0))],
        out_shape=[jax.ShapeDtypeStruct((rows, XATTN_DIM), BF16),
                   jax.ShapeDtypeStruct((rows, XATTN_DIM), BF16)],
        compiler_params=pltpu.CompilerParams(dimension_semantics=("arbitrary",)),
        name="mem_kv",
    )(mem2, g, w)


def _attn_body(q_ref, k_ref, v_ref, o_ref, s_a, s_b, mc_a, mc_b, m_ref, l_ref, acc_ref, *, S, tq):
    s_bufs = (s_a, s_b)
    mc_bufs = (mc_a, mc_b)
    mxu_row_split = 2

    def q_tile(qi, carry):
        q0 = pl.multiple_of(qi * tq, tq)
        m_ref[...] = jnp.full((tq, LANES), NEG_INF, F32)
        l_ref[...] = jnp.zeros((tq, LANES), F32)
        acc_ref[...] = jnp.zeros((tq, V_HEAD_DIM), F32)

        def scores(kb, slot, masked):
            c0 = pl.multiple_of(kb * tq, tq)
            s = _dot_nt(q_ref[0, 0, pl.ds(q0, tq), :], k_ref[0, 0, pl.ds(c0, tq), :])
            if masked:
                ri = lax.broadcasted_iota(jnp.int32, (tq, tq), 0)
                ci = lax.broadcasted_iota(jnp.int32, (tq, tq), 1)
                s = jnp.where(ci - ri <= q0 - c0, s, NEG_INF)
            s_bufs[slot][...] = s
            mc_bufs[slot][...] = jnp.broadcast_to(jnp.max(s, axis=1, keepdims=True), (tq, LANES))

        def accumulate(kb, slot):
            c0 = pl.multiple_of(kb * tq, tq)
            m_prev = m_ref[...]
            m_new = jnp.maximum(m_prev, mc_bufs[slot][...])
            alpha = jnp.exp2(m_prev - m_new)
            p = jnp.exp2(s_bufs[slot][...] - jnp.concatenate([m_new] * (tq // LANES), axis=1))
            psum = p[:, 0:LANES]
            for c in range(1, tq // LANES):
                psum = psum + p[:, c * LANES:(c + 1) * LANES]
            l_ref[...] = alpha * l_ref[...] + psum
            m_ref[...] = m_new
            pb = p.astype(BF16)
            v = v_ref[0, 0, pl.ds(c0, tq), :]
            h = tq // mxu_row_split
            for i in range(mxu_row_split):
                rows = slice(i * h, (i + 1) * h)
                acc_ref[rows, :] = alpha[rows, :] * acc_ref[rows, :] + _dot(pb[rows, :], v)

        scores(0, 0, True)

        def pair(i, c):
            scores(2 * i + 1, 1, False)
            accumulate(2 * i, 0)
            scores(2 * i + 2, 0, False)
            accumulate(2 * i + 1, 1)
            return c

        lax.fori_loop(0, jnp.maximum(qi - 1, 0) // 2, pair, 0)

        @pl.when(qi == 0)
        def _():
            accumulate(0, 0)

        @pl.when(lax.rem(qi, 2) == 1)
        def _():
            scores(qi, 1, True)
            accumulate(qi - 1, 0)
            accumulate(qi, 1)

        @pl.when(jnp.logical_and(qi >= 2, lax.rem(qi, 2) == 0))
        def _():
            scores(qi - 1, 1, False)
            accumulate(qi - 2, 0)
            scores(qi, 0, True)
            accumulate(qi - 1, 1)
            accumulate(qi, 0)

        l = jnp.sum(l_ref[...], axis=1, keepdims=True)
        o_ref[0, pl.ds(q0, tq), :] = (acc_ref[...] / l).astype(BF16)
        return carry

    lax.fori_loop(0, S // tq, q_tile, 0)


def _mla_attn(q, k, v, *, tq):
    B, H, S, _ = q.shape
    per_head = lambda b, h: (b, h, 0, 0)
    return pl.pallas_call(
        functools.partial(_attn_body, S=S, tq=tq),
        grid=(B, H),
        in_specs=[pl.BlockSpec((1, 1, S, QK_PAD_DIM), per_head),
                  pl.BlockSpec((1, 1, S, QK_PAD_DIM), per_head),
                  pl.BlockSpec((1, 1, S, V_HEAD_DIM), per_head)],
        out_specs=pl.BlockSpec((1, S, V_HEAD_DIM), lambda b, h: (b, 0, h)),
        out_shape=jax.ShapeDtypeStruct((B, S, H * V_HEAD_DIM), BF16),
        scratch_shapes=[pltpu.VMEM((tq, tq), F32), pltpu.VMEM((tq, tq), F32),
                        pltpu.VMEM((tq, LANES), F32), pltpu.VMEM((tq, LANES), F32),
                        pltpu.VMEM((tq, LANES), F32), pltpu.VMEM((tq, LANES), F32),
                        pltpu.VMEM((tq, V_HEAD_DIM), F32)],
        compiler_params=pltpu.CompilerParams(dimension_semantics=("arbitrary", "arbitrary"),
                                             vmem_limit_bytes=VMEM_LIMIT_BYTES),
        name="mla_attn",
    )(q, k, v)


def _merge_body(x_ref, ypool_ref, ymla_ref, xq_ref, gates_ref, kmem_ref, vmem_ref,
                wbp_ref, wbm_ref, wbx_ref, wout_ref, ffng_ref, wr_ref, br_ref,
                x1_ref, h2_ref, wts_ref, loc_ref, cnt_tab_ref, carry_tab_ref, counts_ref, carry_ref, *, tm):
    @pl.when(pl.program_id(0) == 0)
    def _():
        carry_ref[...] = jnp.zeros((N_EXPERTS, LANES), F32)

    xq = xq_ref[...]
    parts = []
    for h in range(XATTN_HEADS):
        cols = slice(h * XATTN_HEAD_DIM, (h + 1) * XATTN_HEAD_DIM)
        s = _dot_nt(xq[:, cols], kmem_ref[:, cols]) * (XATTN_HEAD_DIM ** -0.5)
        e = jnp.exp(s - jnp.max(s, axis=1, keepdims=True))
        p = e / jnp.sum(e, axis=1, keepdims=True)
        parts.append(_dot(p.astype(BF16), vmem_ref[:, cols]))
    ymem = jnp.concatenate(parts, axis=1).astype(BF16)

    gates = gates_ref[...].astype(F32)
    merged = (gates[:, 0:D_MODEL] * _dot(ypool_ref[...], wbp_ref[...])
              + gates[:, D_MODEL:2 * D_MODEL] * _dot(ymla_ref[...], wbm_ref[...])
              + gates[:, 2 * D_MODEL:3 * D_MODEL] * _dot(ymem, wbx_ref[...]))
    x1 = x_ref[...] + _dot(merged.astype(BF16), wout_ref[...])
    x1_ref[...] = x1
    h2 = _rms(x1, ffng_ref[...]).astype(BF16)
    h2_ref[...] = h2

    lt = _dot_nt(wr_ref[...], h2) + br_ref[...]
    gl = lt[N_EXPERTS:N_EXPERTS + N_GROUPS, :]
    gmax = jnp.max(gl, axis=0, keepdims=True)
    r4 = lax.broadcasted_iota(jnp.int32, (N_GROUPS, tm), 0).astype(F32)
    gidx = jnp.min(jnp.where(gl == gmax, r4, float(N_GROUPS)), axis=0, keepdims=True)
    pg = 1.0 / jnp.sum(jnp.exp(gl - gmax), axis=0, keepdims=True)
    esel = lt[0:EXPERTS_PER_GROUP, :]
    for g in range(1, N_GROUPS):
        esel = jnp.where(gidx == float(g), lt[g * EXPERTS_PER_GROUP:(g + 1) * EXPERTS_PER_GROUP, :], esel)
    r8 = lax.broadcasted_iota(jnp.int32, (EXPERTS_PER_GROUP, tm), 0).astype(F32)
    m1 = jnp.max(esel, axis=0, keepdims=True)
    i1 = jnp.min(jnp.where(esel == m1, r8, float(EXPERTS_PER_GROUP)), axis=0, keepdims=True)
    rest = jnp.where(r8 == i1, -jnp.inf, esel)
    m2 = jnp.max(rest, axis=0, keepdims=True)
    i2 = jnp.min(jnp.where(rest == m2, r8, float(EXPERTS_PER_GROUP)), axis=0, keepdims=True)
    e2 = jnp.exp(m2 - m1)
    den = 1.0 + e2
    wts_ref[0:1, :] = pg / den
    wts_ref[1:2, :] = pg * e2 / den
    ex1 = gidx * float(EXPERTS_PER_GROUP) + i1
    ex2 = gidx * float(EXPERTS_PER_GROUP) + i2

    r32 = lax.broadcasted_iota(jnp.int32, (N_EXPERTS, tm), 0).astype(F32)
    is1 = r32 == ex1
    is2 = r32 == ex2
    member = jnp.where(is1 | is2, 1.0, 0.0)
    upper = jnp.where(lax.broadcasted_iota(jnp.int32, (tm, tm), 0)
                      <= lax.broadcasted_iota(jnp.int32, (tm, tm), 1), 1.0, 0.0).astype(BF16)
    incl = _dot(member.astype(BF16), upper)
    run = jnp.floor((jnp.sum(member, axis=1, keepdims=True) + (RUN_ALIGN - 1)) / RUN_ALIGN) * RUN_ALIGN
    rcol = lax.broadcasted_iota(jnp.int32, (N_EXPERTS, 1), 0)
    run_start = jnp.zeros((N_EXPERTS, 1), F32)
    for e in range(N_EXPERTS - 1):
        run_start = run_start + jnp.where(rcol > e, run[e:e + 1, :], 0.0)
    pos = incl - 1.0 + run_start
    loc_ref[0:1, :] = jnp.sum(jnp.where(is1, pos, 0.0), axis=0, keepdims=True).astype(jnp.int32)
    loc_ref[1:2, :] = jnp.sum(jnp.where(is2, pos, 0.0), axis=0, keepdims=True).astype(jnp.int32)
    carry = carry_ref[...]
    total = carry + run
    cnt_tab_ref[...] = jnp.broadcast_to(run, (N_EXPERTS, LANES)).astype(jnp.int32)
    carry_tab_ref[...] = carry.astype(jnp.int32)
    carry_ref[...] = total
    counts_ref[...] = total.astype(jnp.int32)


def _merge(x2, ypool, ymla, xq, gates, kmem, vmem, wbp, wbm, wbx, wout, ffng, wr, br, *, B, S, tm, mem_len):
    T = B * S
    tps = S // tm
    row = lambda i: (i, 0)
    lane = lambda i: (0, i)
    memb = lambda i: (i // tps, 0)
    return pl.pallas_call(
        functools.partial(_merge_body, tm=tm),
        grid=(T // tm,),
        in_specs=[
            pl.BlockSpec((tm, D_MODEL), row),
            pl.BlockSpec((tm, POOL_DIM), row),
            pl.BlockSpec((tm, MLA_HEADS * V_HEAD_DIM), row),
            pl.BlockSpec((tm, XATTN_DIM), row),
            pl.BlockSpec((tm, N_BRANCHES * D_MODEL), row),
            pl.BlockSpec((mem_len, XATTN_DIM), memb),
            pl.BlockSpec((mem_len, XATTN_DIM), memb),
            _const_spec((POOL_DIM, D_MODEL)),
            _const_spec((MLA_HEADS * V_HEAD_DIM, D_MODEL)),
            _const_spec((XATTN_DIM, D_MODEL)),
            _const_spec((D_MODEL, D_MODEL)),
            _const_spec((1, D_MODEL)),
            _const_spec((ROUTER_ROWS, D_MODEL)),
            _const_spec((ROUTER_ROWS, 1)),
        ],
        out_specs=[
            pl.BlockSpec((tm, D_MODEL), row),
            pl.BlockSpec((tm, D_MODEL), row),
            pl.BlockSpec((2, tm), lane),
            pl.BlockSpec((2, tm), lane),
            pl.BlockSpec((N_EXPERTS, LANES), row),
            pl.BlockSpec((N_EXPERTS, LANES), row),
            pl.BlockSpec((N_EXPERTS, LANES), lambda i: (0, 0)),
        ],
        out_shape=[
            jax.ShapeDtypeStruct((T, D_MODEL), F32),
            jax.ShapeDtypeStruct((T, D_MODEL), BF16),
            jax.ShapeDtypeStruct((2, T), F32),
            jax.ShapeDtypeStruct((2, T), jnp.int32),
            jax.ShapeDtypeStruct((T // tm * N_EXPERTS, LANES), jnp.int32),
            jax.ShapeDtypeStruct((T // tm * N_EXPERTS, LANES), jnp.int32),
            jax.ShapeDtypeStruct((N_EXPERTS, LANES), jnp.int32),
        ],
        scratch_shapes=[pltpu.VMEM((N_EXPERTS, LANES), F32)],
        compiler_params=pltpu.CompilerParams(dimension_semantics=("arbitrary",),
                                             vmem_limit_bytes=VMEM_LIMIT_BYTES),
        name="merge",
    )(x2, ypool, ymla, xq, gates, kmem, vmem, wbp, wbm, wbx, wout, ffng, wr, br)


def _moe_pos_body(counts_ref, meta_ref):
    shift = int(math.log2(MOE_ROWS))
    cnt = counts_ref[...]
    padded = lax.shift_left(lax.shift_right_logical(cnt + (MOE_ROWS - 1), shift), shift)
    r32 = lax.broadcasted_iota(jnp.int32, (N_EXPERTS, LANES), 0)
    pad_start = jnp.zeros((N_EXPERTS, LANES), jnp.int32)
    for e in range(N_EXPERTS - 1):
        pad_start = pad_start + jnp.where(r32 > e, padded[e:e + 1, :], 0)
    pad_end = pad_start + padded

    lane = lax.broadcasted_iota(jnp.int32, (1, META_LANES), 1)
    block_row = lane * MOE_ROWS
    blk_e = jnp.zeros((1, META_LANES), jnp.int32)
    pe_row = jnp.zeros((1, META_LANES), jnp.int32)
    for e in range(N_EXPERTS):
        pe = pad_end[e:e + 1, 0:1]
        blk_e = blk_e + jnp.where(pe <= block_row, 1, 0)
        pe_row = pe_row + jnp.where(lane == META_PAD_END + e, pe, 0)
    blk_e = jnp.minimum(blk_e, N_EXPERTS - 1)
    nact = lax.shift_right_logical(pad_end[N_EXPERTS - 1:N_EXPERTS, 0:1], shift)
    meta = jnp.where(lane < META_PAD_END, blk_e, pe_row)
    meta_ref[...] = jnp.where(lane == META_NACT, nact, meta)


def _moe_pos(counts):
    full = lambda shape: pl.BlockSpec(shape, lambda i: (0,) * len(shape))
    return pl.pallas_call(
        _moe_pos_body,
        grid=(1,),
        in_specs=[full((N_EXPERTS, LANES))],
        out_specs=full((1, META_LANES)),
        out_shape=jax.ShapeDtypeStruct((1, META_LANES), jnp.int32),
        compiler_params=pltpu.CompilerParams(dimension_semantics=("arbitrary",)),
        name="moe_pos",
    )(counts)


def _pack_bf16_pairs(x):
    lo = pltpu.bitcast(x[:, 0:PACKED_DIM], U32)
    hi = pltpu.bitcast(x[:, PACKED_DIM:D_MODEL], U32)
    return hi | lax.shift_right_logical(lo, jnp.uint32(16))


def _unpack_bf16_pairs(w):
    lo = pltpu.bitcast(lax.shift_left(w, jnp.uint32(16)), F32)
    hi = pltpu.bitcast(w & jnp.uint32(0xFFFF0000), F32)
    return jnp.concatenate([lo, hi], axis=1)


def _loc_rows(tm):
    return 2 * tm + N_EXPERTS * RUN_ALIGN


def _run_copies(tile, cnt_ref, carry_ref, meta_ref, make_copy, max_run):
    def per_expert(e, local):
        n = cnt_ref[tile * N_EXPERTS + e]
        start = jnp.where(e == 0, 0, meta_ref[META_PAD_END + jnp.maximum(e - 1, 0)])
        glob = start + carry_ref[tile * N_EXPERTS + e]
        bit = max_run
        while bit >= RUN_ALIGN:
            take = n & bit

            @pl.when(take != 0)
            def _(local=local, glob=glob, bit=bit):
                make_copy(pl.multiple_of(local, RUN_ALIGN), pl.multiple_of(glob, RUN_ALIGN), bit).start()

            local = local + take
            glob = glob + take
            bit //= 2
        return local

    return pl.multiple_of(lax.fori_loop(0, N_EXPERTS, per_expert, 0), RUN_ALIGN)


def _tile_rows(tile, cnt_ref):
    total = lax.fori_loop(0, N_EXPERTS, lambda e, t: t + cnt_ref[tile * N_EXPERTS + e], 0)
    return pl.multiple_of(total, RUN_ALIGN)


def _dispatch_body(meta_ref, cnt_ref, carry_ref, loc_ref, h2_ref, xs_ref, xloc_ref, zero_ref, sems, zsem,
                   *, tm, n_blocks):
    tile = pl.program_id(0)
    slot = lax.rem(tile, 2)

    def wait_rows(t, s):
        n = _tile_rows(t, cnt_ref)
        pltpu.make_async_copy(xloc_ref.at[s, pl.ds(0, n)], xs_ref.at[pl.ds(0, n)], sems.at[s]).wait()

    def pad_copy(e):
        end = pl.multiple_of(meta_ref[META_PAD_END + e], MOE_ROWS)
        return pltpu.make_async_copy(zero_ref, xs_ref.at[pl.ds(end - MOE_ROWS, MOE_ROWS)], zsem)

    def has_rows(e):
        prev = jnp.where(e == 0, 0, meta_ref[META_PAD_END + jnp.maximum(e - 1, 0)])
        return meta_ref[META_PAD_END + e] > prev

    @pl.when(tile == 0)
    def _():
        zero_ref[...] = jnp.zeros((MOE_ROWS, PACKED_DIM), U32)

        def start(e, c):
            @pl.when(has_rows(e))
            def _():
                pad_copy(e).start()
            return c

        def wait(e, c):
            @pl.when(has_rows(e))
            def _():
                pad_copy(e).wait()
            return c

        def tail_copy(b):
            return pltpu.make_async_copy(
                zero_ref, xs_ref.at[pl.ds(pl.multiple_of(b * MOE_ROWS, MOE_ROWS), MOE_ROWS)], zsem)

        def tail_start(b, c):
            tail_copy(b).start()
            return c

        def tail_wait(b, c):
            tail_copy(b).wait()
            return c

        nact = meta_ref[META_NACT]
        lax.fori_loop(0, N_EXPERTS, start, 0)
        lax.fori_loop(nact, n_blocks, tail_start, 0)
        lax.fori_loop(0, N_EXPERTS, wait, 0)
        lax.fori_loop(nact, n_blocks, tail_wait, 0)

    @pl.when(tile >= 2)
    def _():
        wait_rows(tile - 2, slot)

    r = lax.broadcasted_iota(jnp.int32, (_loc_rows(tm), tm), 0)
    onehot = jnp.where((r == loc_ref[0:1, :]) | (r == loc_ref[1:2, :]), 1.0, 0.0).astype(BF16)
    xloc_ref[slot] = _pack_bf16_pairs(_dot(onehot, h2_ref[...]))

    def make_copy(local, glob, n):
        return pltpu.make_async_copy(xloc_ref.at[slot, pl.ds(local, n)], xs_ref.at[pl.ds(glob, n)],
                                     sems.at[slot])

    _run_copies(tile, cnt_ref, carry_ref, meta_ref, make_copy, tm)

    @pl.when(tile == pl.num_programs(0) - 1)
    def _():
        @pl.when(tile >= 1)
        def _():
            wait_rows(tile - 1, 1 - slot)

        wait_rows(tile, slot)


def _dispatch(meta1, cnt_tab, carry_tab, loc, h2, *, R, tm):
    T = h2.shape[0]
    return pl.pallas_call(
        functools.partial(_dispatch_body, tm=tm, n_blocks=R // MOE_ROWS),
        grid_spec=pltpu.PrefetchScalarGridSpec(
            num_scalar_prefetch=3,
            grid=(T // tm,),
            in_specs=[pl.BlockSpec((2, tm), lambda i, *_: (0, i)),
                      pl.BlockSpec((tm, D_MODEL), lambda i, *_: (i, 0))],
            out_specs=pl.BlockSpec(memory_space=pl.ANY),
            scratch_shapes=[pltpu.VMEM((2, _loc_rows(tm), PACKED_DIM), U32),
                            pltpu.VMEM((MOE_ROWS, PACKED_DIM), U32),
                            pltpu.SemaphoreType.DMA((2,)), pltpu.SemaphoreType.DMA],
        ),
        out_shape=jax.ShapeDtypeStruct((R, PACKED_DIM), U32),
        compiler_params=pltpu.CompilerParams(dimension_semantics=("arbitrary",),
                                             vmem_limit_bytes=VMEM_LIMIT_BYTES),
        name="dispatch",
    )(meta1, cnt_tab, carry_tab, loc, h2)


def _moe_ffn_body(meta_ref, xs_ref, wg_ref, wu_ref, wd_ref, ys_ref):
    active = pl.program_id(0) < meta_ref[META_NACT]

    @pl.when(active)
    def _():
        wg = wg_ref[0].astype(BF16)
        wu = wu_ref[0].astype(BF16)
        wd = wd_ref[0].astype(BF16)
        for c in range(MOE_ROWS // MOE_CHUNK):
            rows = slice(c * MOE_CHUNK, (c + 1) * MOE_CHUNK)
            x = _unpack_bf16_pairs(xs_ref[rows, :]).astype(BF16)
            g = _dot(x, wg)
            a = (g * jax.nn.sigmoid(g)) * _dot(x, wu)
            y = _dot(a.astype(BF16), wd)
            ys_ref[rows, :] = _pack_bf16_pairs(y.astype(BF16).astype(F32))

    @pl.when(jnp.logical_not(active))
    def _():
        ys_ref[...] = jnp.zeros((MOE_ROWS, PACKED_DIM), U32)


def _moe_ffn(meta1, xs, wg, wu, wd):
    R = xs.shape[0]
    blk = lambda b, m: (jnp.minimum(b, m[META_NACT] - 1), 0)
    wsel = lambda b, m: (m[jnp.minimum(b, m[META_NACT] - 1)], 0, 0)
    return pl.pallas_call(
        _moe_ffn_body,
        grid_spec=pltpu.PrefetchScalarGridSpec(
            num_scalar_prefetch=1,
            grid=(R // MOE_ROWS,),
            in_specs=[pl.BlockSpec((MOE_ROWS, PACKED_DIM), blk),
                      pl.BlockSpec((1, D_MODEL, D_EXPERT), wsel),
                      pl.BlockSpec((1, D_MODEL, D_EXPERT), wsel),
                      pl.BlockSpec((1, D_EXPERT, D_MODEL), wsel)],
            out_specs=pl.BlockSpec((MOE_ROWS, PACKED_DIM), lambda b, m: (b, 0)),
        ),
        out_shape=jax.ShapeDtypeStruct((R, PACKED_DIM), U32),
        compiler_params=pltpu.CompilerParams(dimension_semantics=("arbitrary",),
                                             vmem_limit_bytes=VMEM_LIMIT_BYTES),
        name="moe_ffn",
    )(meta1, xs, wg, wu, wd)


def _combine_body(meta_ref, cnt_ref, carry_ref, x1_ref, loc_ref, wts_ref, fg_ref, ys_ref, out_ref,
                  yloc_ref, sems, *, tm):
    tile = pl.program_id(0)
    slot = lax.rem(tile, 2)

    def fetch(t, s):
        def make_copy(local, glob, n):
            return pltpu.make_async_copy(ys_ref.at[pl.ds(glob, n)], yloc_ref.at[s, pl.ds(local, n)],
                                         sems.at[s])
        _run_copies(t, cnt_ref, carry_ref, meta_ref, make_copy, tm)

    @pl.when(tile == 0)
    def _():
        yloc_ref[...] = jnp.zeros(yloc_ref.shape, U32)
        fetch(tile, slot)

    @pl.when(tile + 1 < pl.num_programs(0))
    def _():
        fetch(tile + 1, 1 - slot)

    n = _tile_rows(tile, cnt_ref)
    pltpu.make_async_copy(ys_ref.at[pl.ds(0, n)], yloc_ref.at[slot, pl.ds(0, n)], sems.at[slot]).wait()
    r = lax.broadcasted_iota(jnp.int32, (_loc_rows(tm), tm), 0)
    is0 = r == loc_ref[0:1, :]
    is1 = r == loc_ref[1:2, :]
    row_w = jnp.sum(jnp.where(is0, wts_ref[0:1, :], 0.0) + jnp.where(is1, wts_ref[1:2, :], 0.0),
                    axis=1, keepdims=True)
    yw = (row_w * _unpack_bf16_pairs(yloc_ref[slot])).astype(BF16)
    twohot = jnp.where(is0 | is1, 1.0, 0.0).astype(BF16)
    moe = lax.dot_general(twohot, yw, (((0,), (0,)), ((), ())), preferred_element_type=F32)
    out_ref[...] = _rms(x1_ref[...] + moe, fg_ref[...])


def _combine(meta1, cnt_tab, carry_tab, x1, loc, wts, fg, ys, *, tm):
    T = x1.shape[0]
    row = lambda i, *_: (i, 0)
    lane = lambda i, *_: (0, i)
    return pl.pallas_call(
        functools.partial(_combine_body, tm=tm),
        grid_spec=pltpu.PrefetchScalarGridSpec(
            num_scalar_prefetch=3,
            grid=(T // tm,),
            in_specs=[pl.BlockSpec((tm, D_MODEL), row),
                      pl.BlockSpec((2, tm), lane),
                      pl.BlockSpec((2, tm), lane),
                      pl.BlockSpec((1, D_MODEL), lambda i, *_: (0, 0)),
                      pl.BlockSpec(memory_space=pl.ANY)],
            out_specs=pl.BlockSpec((tm, D_MODEL), row),
            scratch_shapes=[pltpu.VMEM((2, _loc_rows(tm), PACKED_DIM), U32),
                            pltpu.SemaphoreType.DMA((2,))],
        ),
        out_shape=jax.ShapeDtypeStruct((T, D_MODEL), F32),
        compiler_params=pltpu.CompilerParams(dimension_semantics=("arbitrary",),
                                             vmem_limit_bytes=VMEM_LIMIT_BYTES),
        name="combine",
    )(meta1, cnt_tab, carry_tab, x1, loc, wts, fg, ys)


def _tile(n, t):
    t = min(n, t)
    assert n % t == 0, (n, t)
    return t


def kernel(x, mem, positions, mix_norm_g, w_in, gate_b, q_norm_g, w_uq, kv_norm_g, w_uk, w_uv, pool_w, pool_scale, mem_norm_g, w_mem_kv, w_br_pool, w_br_mla, w_br_mem, w_out, ffn_norm_g, w_router_group, b_router_group, w_router_expert, b_router_expert, w_gate_e, w_up_e, w_down_e, final_norm_g):
    B, S, D = x.shape
    assert D == D_MODEL and mix_norm_g.shape[0] == 1
    T = B * S
    mem_len = mem.shape[1]
    tm = _tile(S, 512)
    l = 0

    wi = w_in[l]
    o_qd, o_kv, o_kr, o_xq, o_gate = 512, 896, 1152, 1216, 1728
    win_p = jnp.concatenate(
        [wi[:, 0:o_qd], wi[:, o_qd:o_kv], wi[:, o_kv:o_kr], wi[:, o_xq:o_gate], wi[:, o_gate:],
         wi[:, o_kr:o_xq], jnp.zeros((D_MODEL, LANES - QK_ROPE_DIM), wi.dtype)], axis=1).astype(BF16)
    wuq_p = jnp.pad(w_uq[l].reshape(Q_LORA_RANK, MLA_HEADS, QK_NOPE_DIM + QK_ROPE_DIM),
                    ((0, 0), (0, 0), (0, QK_PAD_DIM - QK_NOPE_DIM - QK_ROPE_DIM))
                    ).reshape(Q_LORA_RANK, MLA_HEADS * QK_PAD_DIM).astype(BF16)
    inv_freq = 1.0 / (ROPE_THETA ** (jnp.arange(0, QK_ROPE_DIM, 2, dtype=F32) / QK_ROPE_DIM))
    invf = jnp.concatenate([inv_freq, inv_freq, jnp.zeros((LANES - QK_ROPE_DIM,), F32)])[None, :]
    wr = jnp.concatenate([w_router_expert[l], w_router_group[l],
                          jnp.zeros((D_MODEL, ROUTER_ROWS - N_EXPERTS - N_GROUPS), F32)], axis=1).T.astype(BF16)
    br = jnp.concatenate([b_router_expert[l], b_router_group[l],
                          jnp.zeros((ROUTER_ROWS - N_EXPERTS - N_GROUPS,), F32)])[:, None].astype(F32)
    x2 = x.reshape(T, D_MODEL)
    pos2 = positions.reshape(T, 1)

    ypool, xq, gates, q, k, v = _mixer_in(
        x2, pos2, invf, mix_norm_g[l][None, :], win_p, gate_b[l], q_norm_g[l][None, :], wuq_p,
        kv_norm_g[l][None, :], w_uk[l].astype(BF16), w_uv[l].astype(BF16), pool_w[l].astype(BF16),
        pool_scale[l][None, :], B=B, S=S, tm=tm)
    kmem, vmem = _mem_kv(mem.reshape(B * mem_len, D_MODEL), mem_norm_g[l][None, :], w_mem_kv[l].astype(BF16))
    ymla = _mla_attn(q, k, v, tq=tm).reshape(T, MLA_HEADS * V_HEAD_DIM)
    x1, h2, wts, loc, cnt_tab, carry_tab, counts = _merge(
        x2, ypool, ymla, xq, gates, kmem, vmem, w_br_pool[l].astype(BF16), w_br_mla[l].astype(BF16),
        w_br_mem[l].astype(BF16), w_out[l].astype(BF16), ffn_norm_g[l][None, :], wr, br,
        B=B, S=S, tm=tm, mem_len=mem_len)

    R = 2 * T + (T // tm) * N_EXPERTS * RUN_ALIGN + N_EXPERTS * MOE_ROWS
    assert R % MOE_ROWS == 0 and R // MOE_ROWS <= META_PAD_END
    meta1 = _moe_pos(counts).reshape(META_LANES)
    cnt1 = cnt_tab[:, 0]
    carry1 = carry_tab[:, 0]
    xs = _dispatch(meta1, cnt1, carry1, loc, h2, R=R, tm=tm)
    ys = _moe_ffn(meta1, xs, w_gate_e[l], w_up_e[l], w_down_e[l])
    out = _combine(meta1, cnt1, carry1, x1, loc, wts, final_norm_g[None, :], ys, tm=tm)
    return out.reshape(B, S, D_MODEL)
```

```python
import functools
import math

import jax
import jax.numpy as jnp
import numpy as np
from jax import lax
from jax.experimental import pallas as pl
from jax.experimental.pallas import tpu as pltpu

D_MODEL = 1024
POOL_WINDOWS = (2, 4, 8, 16)
POOL_GROUP_DIM = 128
POOL_DIM = 512
MLA_HEADS = 8
QK_NOPE_DIM = 128
QK_ROPE_DIM = 64
V_HEAD_DIM = 128
Q_LORA_RANK = 384
KV_LORA_RANK = 256
ROPE_THETA = 10000.0
XATTN_HEADS = 4
XATTN_HEAD_DIM = 128
XATTN_DIM = 512
N_BRANCHES = 3
N_GROUPS = 4
EXPERTS_PER_GROUP = 8
N_EXPERTS = 32
D_EXPERT = 256
RMS_EPS = 1e-6
NEG_INF = -1e30

LANES = 128
QK_PAD_DIM = 2 * LANES
POOL_HALO = 16
MOE_ROWS = 512
MOE_CHUNK = 256
RUN_ALIGN = 8
PACKED_DIM = D_MODEL // 2
ROUTER_ROWS = 40
META_LANES = 256
META_PAD_END = 192
META_NACT = 255
VMEM_LIMIT_BYTES = 56 * 1024 * 1024

IN_POOL, IN_QD, IN_KV, IN_XQ, IN_GATE, IN_KR, IN_END = 0, 512, 896, 1152, 1664, 4736, 4864

F32 = jnp.float32
BF16 = jnp.bfloat16
U32 = jnp.uint32


def _rms(x, g):
    ms = jnp.mean(x * x, axis=-1, keepdims=True)
    return (x * lax.rsqrt(ms + RMS_EPS)) * g


def _dot(a, b):
    return jnp.dot(a, b, preferred_element_type=F32)


def _dot_nt(a, b):
    return lax.dot_general(a, b, (((1,), (1,)), ((), ())), preferred_element_type=F32)


def _const_spec(shape):
    nd = len(shape)
    return pl.BlockSpec(shape, lambda *_: (0,) * nd, pipeline_mode=pl.Buffered(1))


def _mixer_in_body(x_ref, pos_ref, invf_ref, mixg_ref, win_ref, gateb_ref, qg_ref, wuq_ref,
                   kvg_ref, wuk_ref, wuv_ref, poolw_ref, pools_ref,
                   ypool_ref, xq_ref, gates_ref, q_ref, k_ref, v_ref, ext_ref,
                   *, tm, tiles_per_seq, q_scale):
    si = lax.rem(pl.program_id(0), tiles_per_seq)
    hb = _rms(x_ref[...], mixg_ref[...]).astype(BF16)

    u = _dot(hb, win_ref[:, IN_POOL:IN_QD])

    @pl.when(si == 0)
    def _():
        ext_ref[0:POOL_HALO, :] = jnp.zeros((POOL_HALO, POOL_DIM), F32)

    ext_ref[POOL_HALO:POOL_HALO + tm, :] = u
    t_seq = lax.broadcasted_iota(jnp.int32, (tm, 1), 0) + si * tm
    for g, w in enumerate(POOL_WINDOWS):
        lo = g * POOL_GROUP_DIM
        hi = lo + POOL_GROUP_DIM
        acc = u[:, lo:hi]
        for j in range(1, w):
            acc = acc + ext_ref[POOL_HALO - j:POOL_HALO - j + tm, lo:hi]
        cnt = jnp.minimum(t_seq + 1, w).astype(F32)
        p = acc / cnt - u[:, lo:hi]
        y = _dot(p.astype(BF16), poolw_ref[g]) * pools_ref[:, lo:hi]
        ypool_ref[:, lo:hi] = y.astype(BF16)
    ext_ref[0:POOL_HALO, :] = ext_ref[tm:tm + POOL_HALO, :]

    ang = pos_ref[...].astype(F32) * invf_ref[...]
    cos = jnp.cos(ang)
    sin = jnp.sin(ang)
    first_half = lax.broadcasted_iota(jnp.int32, (tm, LANES), 1) < (QK_ROPE_DIM // 2)
    sin_signed = jnp.where(first_half, -sin, sin)

    def rope(r):
        swapped = jnp.where(first_half, pltpu.roll(r, LANES - QK_ROPE_DIM // 2, 1),
                            pltpu.roll(r, QK_ROPE_DIM // 2, 1))
        return r * cos + swapped * sin_signed

    cq = _rms(_dot(hb, win_ref[:, IN_QD:IN_KV]), qg_ref[...]).astype(BF16)
    for h in range(MLA_HEADS):
        qh = _dot(cq, wuq_ref[:, h * QK_PAD_DIM:(h + 1) * QK_PAD_DIM])
        q_ref[0, h, :, 0:LANES] = (qh[:, 0:LANES] * q_scale).astype(BF16)
        q_ref[0, h, :, LANES:QK_PAD_DIM] = (rope(qh[:, LANES:QK_PAD_DIM]) * q_scale).astype(BF16)

    ckv = _rms(_dot(hb, win_ref[:, IN_KV:IN_XQ]), kvg_ref[...]).astype(BF16)
    kr = rope(_dot(hb, win_ref[:, IN_KR:IN_END])).astype(BF16)
    for hp in range(MLA_HEADS // 2):
        cols = slice(hp * 2 * LANES, (hp + 1) * 2 * LANES)
        kn = _dot(ckv, wuk_ref[:, cols]).astype(BF16)
        vv = _dot(ckv, wuv_ref[:, cols]).astype(BF16)
        for j in range(2):
            h = 2 * hp + j
            k_ref[0, h, :, 0:LANES] = kn[:, j * LANES:(j + 1) * LANES]
            k_ref[0, h, :, LANES:QK_PAD_DIM] = kr
            v_ref[0, h] = vv[:, j * LANES:(j + 1) * LANES]

    xq_ref[...] = _dot(hb, win_ref[:, IN_XQ:IN_GATE]).astype(BF16)
    for c in range(N_BRANCHES):
        gl = _dot(hb, win_ref[:, IN_GATE + c * D_MODEL:IN_GATE + (c + 1) * D_MODEL])
        gates_ref[:, c * D_MODEL:(c + 1) * D_MODEL] = jax.nn.sigmoid(gl + gateb_ref[c:c + 1, :]).astype(BF16)


def _mixer_in(x2, pos2, invf, mixg, win_p, gate_b, qg, wuq_p, kvg, wuk, wuv, pool_w, pool_s, *, B, S, tm):
    T = B * S
    tps = S // tm
    q_scale = (QK_NOPE_DIM + QK_ROPE_DIM) ** -0.5 * math.log2(math.e)
    body = functools.partial(_mixer_in_body, tm=tm, tiles_per_seq=tps, q_scale=q_scale)
    row = lambda i: (i, 0)
    head = lambda i: (i // tps, 0, i % tps, 0)
    return pl.pallas_call(
        body,
        grid=(T // tm,),
        in_specs=[
            pl.BlockSpec((tm, D_MODEL), row),
            pl.BlockSpec((tm, 1), row),
            _const_spec((1, LANES)),
            _const_spec((1, D_MODEL)),
            _const_spec((D_MODEL, IN_END)),
            _const_spec((N_BRANCHES, D_MODEL)),
            _const_spec((1, Q_LORA_RANK)),
            _const_spec((Q_LORA_RANK, MLA_HEADS * QK_PAD_DIM)),
            _const_spec((1, KV_LORA_RANK)),
            _const_spec((KV_LORA_RANK, MLA_HEADS * QK_NOPE_DIM)),
            _const_spec((KV_LORA_RANK, MLA_HEADS * V_HEAD_DIM)),
            _const_spec((len(POOL_WINDOWS), POOL_GROUP_DIM, POOL_GROUP_DIM)),
            _const_spec((1, POOL_DIM)),
        ],
        out_specs=[
            pl.BlockSpec((tm, POOL_DIM), row),
            pl.BlockSpec((tm, XATTN_DIM), row),
            pl.BlockSpec((tm, N_BRANCHES * D_MODEL), row),
            pl.BlockSpec((1, MLA_HEADS, tm, QK_PAD_DIM), head),
            pl.BlockSpec((1, MLA_HEADS, tm, QK_PAD_DIM), head),
            pl.BlockSpec((1, MLA_HEADS, tm, V_HEAD_DIM), head),
        ],
        out_shape=[
            jax.ShapeDtypeStruct((T, POOL_DIM), BF16),
            jax.ShapeDtypeStruct((T, XATTN_DIM), BF16),
            jax.ShapeDtypeStruct((T, N_BRANCHES * D_MODEL), BF16),
            jax.ShapeDtypeStruct((B, MLA_HEADS, S, QK_PAD_DIM), BF16),
            jax.ShapeDtypeStruct((B, MLA_HEADS, S, QK_PAD_DIM), BF16),
            jax.ShapeDtypeStruct((B, MLA_HEADS, S, V_HEAD_DIM), BF16),
        ],
        scratch_shapes=[pltpu.VMEM((tm + POOL_HALO, POOL_DIM), F32)],
        compiler_params=pltpu.CompilerParams(dimension_semantics=("arbitrary",),
                                             vmem_limit_bytes=VMEM_LIMIT_BYTES),
        name="mixer_in",
    )(x2, pos2, invf, mixg, win_p, gate_b, qg, wuq_p, kvg, wuk, wuv, pool_w, pool_s)


def _mem_kv_body(mem_ref, g_ref, w_ref, k_ref, v_ref):
    mb = _rms(mem_ref[...], g_ref[...]).astype(BF16)
    kv = _dot(mb, w_ref[...])
    k_ref[...] = kv[:, 0:XATTN_DIM].astype(BF16)
    v_ref[...] = kv[:, XATTN_DIM:2 * XATTN_DIM].astype(BF16)


def _mem_kv(mem2, g, w):
    rows = mem2.shape[0]
    tr = min(rows, 512)
    return pl.pallas_call(
        _mem_kv_body,
        grid=(rows // tr,),
        in_specs=[pl.BlockSpec((tr, D_MODEL), lambda i: (i, 0)),
                  _const_spec((1, D_MODEL)),
                  _const_spec((D_MODEL, 2 * XATTN_DIM))],
        out_specs=[pl.BlockSpec((tr, XATTN_DIM), lambda i: (i, 0)),
                   pl.BlockSpec((tr, XATTN_DIM), lambda i: (i, 0))],
        out_shape=[jax.ShapeDtypeStruct((rows, XATTN_DIM), BF16),
                   jax.ShapeDtypeStruct((rows, XATTN_DIM), BF16)],
        compiler_params=pltpu.CompilerParams(dimension_semantics=("arbitrary",)),
        name="mem_kv",
    )(mem2, g, w)


def _attn_block_tables(nq):
    below = [(qi, kb) for qi in range(nq) for kb in range(qi)]
    diag = [(qi, qi) for qi in range(nq)]
    tab = np.asarray(below + diag, np.int32)
    return tab[:, 0], tab[:, 1], len(below)


def _attn_body(qi_tab, kb_tab, q_ref, k_ref, v_ref, o_ref, s_a, s_b, mc_a, mc_b, m_ref, l_ref, acc_ref,
               *, nq, tq, n_below):
    s_bufs = (s_a, s_b)
    mc_bufs = (mc_a, mc_b)
    mxu_row_split = 2

    m_ref[...] = jnp.full(m_ref.shape, NEG_INF, F32)
    l_ref[...] = jnp.zeros(l_ref.shape, F32)
    acc_ref[...] = jnp.zeros(acc_ref.shape, F32)

    def scores(t, slot, diagonal):
        q0 = pl.multiple_of(qi_tab[t] * tq, tq)
        c0 = pl.multiple_of(kb_tab[t] * tq, tq)
        s = _dot_nt(q_ref[0, 0, pl.ds(q0, tq), :], k_ref[0, 0, pl.ds(c0, tq), :])
        if diagonal:
            ri = lax.broadcasted_iota(jnp.int32, (tq, tq), 0)
            ci = lax.broadcasted_iota(jnp.int32, (tq, tq), 1)
            s = jnp.where(ci <= ri, s, NEG_INF)
        s_bufs[slot][...] = s
        mc_bufs[slot][...] = jnp.broadcast_to(jnp.max(s, axis=1, keepdims=True), (tq, LANES))

    def accumulate(t, slot, diagonal):
        qi = qi_tab[t]
        c0 = pl.multiple_of(kb_tab[t] * tq, tq)
        m_prev = m_ref[qi]
        m_new = jnp.maximum(m_prev, mc_bufs[slot][...])
        alpha = jnp.exp2(m_prev - m_new)
        p = jnp.exp2(s_bufs[slot][...] - jnp.concatenate([m_new] * (tq // LANES), axis=1))
        psum = p[:, 0:LANES]
        for c in range(1, tq // LANES):
            psum = psum + p[:, c * LANES:(c + 1) * LANES]
        l_new = alpha * l_ref[qi] + psum
        pb = p.astype(BF16)
        v = v_ref[0, 0, pl.ds(c0, tq), :]
        h = tq // mxu_row_split
        if diagonal:
            inv = 1.0 / jnp.sum(l_new, axis=1, keepdims=True)
            for i in range(mxu_row_split):
                rows = slice(i * h, (i + 1) * h)
                acc = alpha[rows, :] * acc_ref[qi, rows, :] + _dot(pb[rows, :], v)
                o_ref[0, pl.ds(pl.multiple_of(qi * tq, tq) + i * h, h), :] = (acc * inv[rows, :]).astype(BF16)
        else:
            l_ref[qi] = l_new
            m_ref[qi] = m_new
            for i in range(mxu_row_split):
                rows = slice(i * h, (i + 1) * h)
                acc_ref[qi, rows, :] = alpha[rows, :] * acc_ref[qi, rows, :] + _dot(pb[rows, :], v)

    def run(start, n, diagonal):
        if n == 0:
            return
        scores(start, 0, diagonal)

        def pair(i, c):
            t = start + 2 * i
            scores(t + 1, 1, diagonal)
            accumulate(t, 0, diagonal)
            scores(t + 2, 0, diagonal)
            accumulate(t + 1, 1, diagonal)
            return c

        lax.fori_loop(0, (n - 1) // 2, pair, 0)
        if (n - 1) % 2 == 1:
            scores(start + n - 1, 1, diagonal)
            accumulate(start + n - 2, 0, diagonal)
            accumulate(start + n - 1, 1, diagonal)
        else:
            accumulate(start + n - 1, 0, diagonal)

    run(0, n_below, False)
    run(n_below, nq, True)


def _mla_attn(q, k, v, *, tq):
    B, H, S, _ = q.shape
    nq = S // tq
    qi_tab, kb_tab, n_below = _attn_block_tables(nq)
    per_head = lambda b, h, *_: (b, h, 0, 0)
    return pl.pallas_call(
        functools.partial(_attn_body, nq=nq, tq=tq, n_below=n_below),
        grid_spec=pltpu.PrefetchScalarGridSpec(
            num_scalar_prefetch=2,
            grid=(B, H),
            in_specs=[pl.BlockSpec((1, 1, S, QK_PAD_DIM), per_head),
                      pl.BlockSpec((1, 1, S, QK_PAD_DIM), per_head),
                      pl.BlockSpec((1, 1, S, V_HEAD_DIM), per_head)],
            out_specs=pl.BlockSpec((1, S, V_HEAD_DIM), lambda b, h, *_: (b, 0, h)),
            scratch_shapes=[pltpu.VMEM((tq, tq), F32), pltpu.VMEM((tq, tq), F32),
                            pltpu.VMEM((tq, LANES), F32), pltpu.VMEM((tq, LANES), F32),
                            pltpu.VMEM((nq, tq, LANES), F32), pltpu.VMEM((nq, tq, LANES), F32),
                            pltpu.VMEM((nq, tq, V_HEAD_DIM), F32)],
        ),
        out_shape=jax.ShapeDtypeStruct((B, S, H * V_HEAD_DIM), BF16),
        compiler_params=pltpu.CompilerParams(dimension_semantics=("arbitrary", "arbitrary"),
                                             vmem_limit_bytes=VMEM_LIMIT_BYTES),
        name="mla_attn",
    )(jnp.asarray(qi_tab), jnp.asarray(kb_tab), q, k, v)


def _merge_body(x_ref, ypool_ref, ymla_ref, xq_ref, gates_ref, kmem_ref, vmem_ref,
                wbp_ref, wbm_ref, wbx_ref, wout_ref, ffng_ref, wr_ref, br_ref,
                x1_ref, h2_ref, wts_ref, loc_ref, cnt_tab_ref, carry_tab_ref, counts_ref, carry_ref, *, tm):
    @pl.when(pl.program_id(0) == 0)
    def _():
        carry_ref[...] = jnp.zeros((N_EXPERTS, LANES), F32)

    xq = xq_ref[...]
    parts = []
    for h in range(XATTN_HEADS):
        cols = slice(h * XATTN_HEAD_DIM, (h + 1) * XATTN_HEAD_DIM)
        s = _dot_nt(xq[:, cols], kmem_ref[:, cols]) * (XATTN_HEAD_DIM ** -0.5)
        e = jnp.exp(s - jnp.max(s, axis=1, keepdims=True))
        p = e / jnp.sum(e, axis=1, keepdims=True)
        parts.append(_dot(p.astype(BF16), vmem_ref[:, cols]))
    ymem = jnp.concatenate(parts, axis=1).astype(BF16)

    gates = gates_ref[...].astype(F32)
    merged = (gates[:, 0:D_MODEL] * _dot(ypool_ref[...], wbp_ref[...])
              + gates[:, D_MODEL:2 * D_MODEL] * _dot(ymla_ref[...], wbm_ref[...])
              + gates[:, 2 * D_MODEL:3 * D_MODEL] * _dot(ymem, wbx_ref[...]))
    x1 = x_ref[...] + _dot(merged.astype(BF16), wout_ref[...])
    x1_ref[...] = x1
    h2 = _rms(x1, ffng_ref[...]).astype(BF16)
    h2_ref[...] = h2

    lt = _dot_nt(wr_ref[...], h2) + br_ref[...]
    gl = lt[N_EXPERTS:N_EXPERTS + N_GROUPS, :]
    gmax = jnp.max(gl, axis=0, keepdims=True)
    r4 = lax.broadcasted_iota(jnp.int32, (N_GROUPS, tm), 0).astype(F32)
    gidx = jnp.min(jnp.where(gl == gmax, r4, float(N_GROUPS)), axis=0, keepdims=True)
    pg = 1.0 / jnp.sum(jnp.exp(gl - gmax), axis=0, keepdims=True)
    esel = lt[0:EXPERTS_PER_GROUP, :]
    for g in range(1, N_GROUPS):
        esel = jnp.where(gidx == float(g), lt[g * EXPERTS_PER_GROUP:(g + 1) * EXPERTS_PER_GROUP, :], esel)
    r8 = lax.broadcasted_iota(jnp.int32, (EXPERTS_PER_GROUP, tm), 0).astype(F32)
    m1 = jnp.max(esel, axis=0, keepdims=True)
    i1 = jnp.min(jnp.where(esel == m1, r8, float(EXPERTS_PER_GROUP)), axis=0, keepdims=True)
    rest = jnp.where(r8 == i1, -jnp.inf, esel)
    m2 = jnp.max(rest, axis=0, keepdims=True)
    i2 = jnp.min(jnp.where(rest == m2, r8, float(EXPERTS_PER_GROUP)), axis=0, keepdims=True)
    e2 = jnp.exp(m2 - m1)
    den = 1.0 + e2
    wts_ref[0:1, :] = pg / den
    wts_ref[1:2, :] = pg * e2 / den
    ex1 = gidx * float(EXPERTS_PER_GROUP) + i1
    ex2 = gidx * float(EXPERTS_PER_GROUP) + i2

    r32 = lax.broadcasted_iota(jnp.int32, (N_EXPERTS, tm), 0).astype(F32)
    is1 = r32 == ex1
    is2 = r32 == ex2
    member = jnp.where(is1 | is2, 1.0, 0.0)
    upper = jnp.where(lax.broadcasted_iota(jnp.int32, (tm, tm), 0)
                      <= lax.broadcasted_iota(jnp.int32, (tm, tm), 1), 1.0, 0.0).astype(BF16)
    incl = _dot(member.astype(BF16), upper)
    run = jnp.floor((jnp.sum(member, axis=1, keepdims=True) + (RUN_ALIGN - 1)) / RUN_ALIGN) * RUN_ALIGN
    rcol = lax.broadcasted_iota(jnp.int32, (N_EXPERTS, 1), 0)
    run_start = jnp.zeros((N_EXPERTS, 1), F32)
    for e in range(N_EXPERTS - 1):
        run_start = run_start + jnp.where(rcol > e, run[e:e + 1, :], 0.0)
    pos = incl - 1.0 + run_start
    loc_ref[0:1, :] = jnp.sum(jnp.where(is1, pos, 0.0), axis=0, keepdims=True).astype(jnp.int32)
    loc_ref[1:2, :] = jnp.sum(jnp.where(is2, pos, 0.0), axis=0, keepdims=True).astype(jnp.int32)
    carry = carry_ref[...]
    total = carry + run
    cnt_tab_ref[...] = jnp.broadcast_to(run, (N_EXPERTS, LANES)).astype(jnp.int32)
    carry_tab_ref[...] = carry.astype(jnp.int32)
    carry_ref[...] = total
    counts_ref[...] = total.astype(jnp.int32)


def _merge(x2, ypool, ymla, xq, gates, kmem, vmem, wbp, wbm, wbx, wout, ffng, wr, br, *, B, S, tm, mem_len):
    T = B * S
    tps = S // tm
    row = lambda i: (i, 0)
    lane = lambda i: (0, i)
    memb = lambda i: (i // tps, 0)
    return pl.pallas_call(
        functools.partial(_merge_body, tm=tm),
        grid=(T // tm,),
        in_specs=[
            pl.BlockSpec((tm, D_MODEL), row),
            pl.BlockSpec((tm, POOL_DIM), row),
            pl.BlockSpec((tm, MLA_HEADS * V_HEAD_DIM), row),
            pl.BlockSpec((tm, XATTN_DIM), row),
            pl.BlockSpec((tm, N_BRANCHES * D_MODEL), row),
            pl.BlockSpec((mem_len, XATTN_DIM), memb),
            pl.BlockSpec((mem_len, XATTN_DIM), memb),
            _const_spec((POOL_DIM, D_MODEL)),
            _const_spec((MLA_HEADS * V_HEAD_DIM, D_MODEL)),
            _const_spec((XATTN_DIM, D_MODEL)),
            _const_spec((D_MODEL, D_MODEL)),
            _const_spec((1, D_MODEL)),
            _const_spec((ROUTER_ROWS, D_MODEL)),
            _const_spec((ROUTER_ROWS, 1)),
        ],
        out_specs=[
            pl.BlockSpec((tm, D_MODEL), row),
            pl.BlockSpec((tm, D_MODEL), row),
            pl.BlockSpec((2, tm), lane),
            pl.BlockSpec((2, tm), lane),
            pl.BlockSpec((N_EXPERTS, LANES), row),
            pl.BlockSpec((N_EXPERTS, LANES), row),
            pl.BlockSpec((N_EXPERTS, LANES), lambda i: (0, 0)),
        ],
        out_shape=[
            jax.ShapeDtypeStruct((T, D_MODEL), F32),
            jax.ShapeDtypeStruct((T, D_MODEL), BF16),
            jax.ShapeDtypeStruct((2, T), F32),
            jax.ShapeDtypeStruct((2, T), jnp.int32),
            jax.ShapeDtypeStruct((T // tm * N_EXPERTS, LANES), jnp.int32),
            jax.ShapeDtypeStruct((T // tm * N_EXPERTS, LANES), jnp.int32),
            jax.ShapeDtypeStruct((N_EXPERTS, LANES), jnp.int32),
        ],
        scratch_shapes=[pltpu.VMEM((N_EXPERTS, LANES), F32)],
        compiler_params=pltpu.CompilerParams(dimension_semantics=("arbitrary",),
                                             vmem_limit_bytes=VMEM_LIMIT_BYTES),
        name="merge",
    )(x2, ypool, ymla, xq, gates, kmem, vmem, wbp, wbm, wbx, wout, ffng, wr, br)


def _moe_pos_body(counts_ref, meta_ref):
    shift = int(math.log2(MOE_ROWS))
    cnt = counts_ref[...]
    padded = lax.shift_left(lax.shift_right_logical(cnt + (MOE_ROWS - 1), shift), shift)
    r32 = lax.broadcasted_iota(jnp.int32, (N_EXPERTS, LANES), 0)
    pad_start = jnp.zeros((N_EXPERTS, LANES), jnp.int32)
    for e in range(N_EXPERTS - 1):
        pad_start = pad_start + jnp.where(r32 > e, padded[e:e + 1, :], 0)
    pad_end = pad_start + padded

    lane = lax.broadcasted_iota(jnp.int32, (1, META_LANES), 1)
    block_row = lane * MOE_ROWS
    blk_e = jnp.zeros((1, META_LANES), jnp.int32)
    pe_row = jnp.zeros((1, META_LANES), jnp.int32)
    for e in range(N_EXPERTS):
        pe = pad_end[e:e + 1, 0:1]
        blk_e = blk_e + jnp.where(pe <= block_row, 1, 0)
        pe_row = pe_row + jnp.where(lane == META_PAD_END + e, pe, 0)
    blk_e = jnp.minimum(blk_e, N_EXPERTS - 1)
    nact = lax.shift_right_logical(pad_end[N_EXPERTS - 1:N_EXPERTS, 0:1], shift)
    meta = jnp.where(lane < META_PAD_END, blk_e, pe_row)
    meta_ref[...] = jnp.where(lane == META_NACT, nact, meta)


def _moe_pos(counts):
    full = lambda shape: pl.BlockSpec(shape, lambda i: (0,) * len(shape))
    return pl.pallas_call(
        _moe_pos_body,
        grid=(1,),
        in_specs=[full((N_EXPERTS, LANES))],
        out_specs=full((1, META_LANES)),
        out_shape=jax.ShapeDtypeStruct((1, META_LANES), jnp.int32),
        compiler_params=pltpu.CompilerParams(dimension_semantics=("arbitrary",)),
        name="moe_pos",
    )(counts)


def _pack_bf16_pairs(x):
    lo = pltpu.bitcast(x[:, 0:PACKED_DIM], U32)
    hi = pltpu.bitcast(x[:, PACKED_DIM:D_MODEL], U32)
    return hi | lax.shift_right_logical(lo, jnp.uint32(16))


def _unpack_bf16_pairs(w):
    lo = pltpu.bitcast(lax.shift_left(w, jnp.uint32(16)), F32)
    hi = pltpu.bitcast(w & jnp.uint32(0xFFFF0000), F32)
    return jnp.concatenate([lo, hi], axis=1)


def _loc_rows(tm):
    return 2 * tm + N_EXPERTS * RUN_ALIGN


def _run_copies(tile, cnt_ref, carry_ref, meta_ref, make_copy, max_run):
    def per_expert(e, local):
        n = cnt_ref[tile * N_EXPERTS + e]
        start = jnp.where(e == 0, 0, meta_ref[META_PAD_END + jnp.maximum(e - 1, 0)])
        glob = start + carry_ref[tile * N_EXPERTS + e]
        bit = max_run
        while bit >= RUN_ALIGN:
            take = n & bit

            @pl.when(take != 0)
            def _(local=local, glob=glob, bit=bit):
                make_copy(pl.multiple_of(local, RUN_ALIGN), pl.multiple_of(glob, RUN_ALIGN), bit).start()

            local = local + take
            glob = glob + take
            bit //= 2
        return local

    return pl.multiple_of(lax.fori_loop(0, N_EXPERTS, per_expert, 0), RUN_ALIGN)


def _tile_rows(tile, cnt_ref):
    total = lax.fori_loop(0, N_EXPERTS, lambda e, t: t + cnt_ref[tile * N_EXPERTS + e], 0)
    return pl.multiple_of(total, RUN_ALIGN)


def _dispatch_body(meta_ref, cnt_ref, carry_ref, loc_ref, h2_ref, xs_ref, xloc_ref, zero_ref, sems, zsem,
                   *, tm, n_blocks):
    tile = pl.program_id(0)
    slot = lax.rem(tile, 2)

    def wait_rows(t, s):
        n = _tile_rows(t, cnt_ref)
        pltpu.make_async_copy(xloc_ref.at[s, pl.ds(0, n)], xs_ref.at[pl.ds(0, n)], sems.at[s]).wait()

    def pad_copy(e):
        end = pl.multiple_of(meta_ref[META_PAD_END + e], MOE_ROWS)
        return pltpu.make_async_copy(zero_ref, xs_ref.at[pl.ds(end - MOE_ROWS, MOE_ROWS)], zsem)

    def has_rows(e):
        prev = jnp.where(e == 0, 0, meta_ref[META_PAD_END + jnp.maximum(e - 1, 0)])
        return meta_ref[META_PAD_END + e] > prev

    @pl.when(tile == 0)
    def _():
        zero_ref[...] = jnp.zeros((MOE_ROWS, PACKED_DIM), U32)

        def start(e, c):
            @pl.when(has_rows(e))
            def _():
                pad_copy(e).start()
            return c

        def wait(e, c):
            @pl.when(has_rows(e))
            def _():
                pad_copy(e).wait()
            return c

        def tail_copy(b):
            return pltpu.make_async_copy(
                zero_ref, xs_ref.at[pl.ds(pl.multiple_of(b * MOE_ROWS, MOE_ROWS), MOE_ROWS)], zsem)

        def tail_start(b, c):
            tail_copy(b).start()
            return c

        def tail_wait(b, c):
            tail_copy(b).wait()
            return c

        nact = meta_ref[META_NACT]
        lax.fori_loop(0, N_EXPERTS, start, 0)
        lax.fori_loop(nact, n_blocks, tail_start, 0)
        lax.fori_loop(0, N_EXPERTS, wait, 0)
        lax.fori_loop(nact, n_blocks, tail_wait, 0)

    @pl.when(tile >= 2)
    def _():
        wait_rows(tile - 2, slot)

    r = lax.broadcasted_iota(jnp.int32, (_loc_rows(tm), tm), 0)
    onehot = jnp.where((r == loc_ref[0:1, :]) | (r == loc_ref[1:2, :]), 1.0, 0.0).astype(BF16)
    xloc_ref[slot] = _pack_bf16_pairs(_dot(onehot, h2_ref[...]))

    def make_copy(local, glob, n):
        return pltpu.make_async_copy(xloc_ref.at[slot, pl.ds(local, n)], xs_ref.at[pl.ds(glob, n)],
                                     sems.at[slot])

    _run_copies(tile, cnt_ref, carry_ref, meta_ref, make_copy, tm)

    @pl.when(tile == pl.num_programs(0) - 1)
    def _():
        @pl.when(tile >= 1)
        def _():
            wait_rows(tile - 1, 1 - slot)

        wait_rows(tile, slot)


def _dispatch(meta1, cnt_tab, carry_tab, loc, h2, *, R, tm):
    T = h2.shape[0]
    return pl.pallas_call(
        functools.partial(_dispatch_body, tm=tm, n_blocks=R // MOE_ROWS),
        grid_spec=pltpu.PrefetchScalarGridSpec(
            num_scalar_prefetch=3,
            grid=(T // tm,),
            in_specs=[pl.BlockSpec((2, tm), lambda i, *_: (0, i)),
                      pl.BlockSpec((tm, D_MODEL), lambda i, *_: (i, 0))],
            out_specs=pl.BlockSpec(memory_space=pl.ANY),
            scratch_shapes=[pltpu.VMEM((2, _loc_rows(tm), PACKED_DIM), U32),
                            pltpu.VMEM((MOE_ROWS, PACKED_DIM), U32),
                            pltpu.SemaphoreType.DMA((2,)), pltpu.SemaphoreType.DMA],
        ),
        out_shape=jax.ShapeDtypeStruct((R, PACKED_DIM), U32),
        compiler_params=pltpu.CompilerParams(dimension_semantics=("arbitrary",),
                                             vmem_limit_bytes=VMEM_LIMIT_BYTES),
        name="dispatch",
    )(meta1, cnt_tab, carry_tab, loc, h2)


def _moe_ffn_body(meta_ref, xs_ref, wg_ref, wu_ref, wd_ref, ys_ref):
    active = pl.program_id(0) < meta_ref[META_NACT]

    @pl.when(active)
    def _():
        wg = wg_ref[0].astype(BF16)
        wu = wu_ref[0].astype(BF16)
        wd = wd_ref[0].astype(BF16)
        for c in range(MOE_ROWS // MOE_CHUNK):
            rows = slice(c * MOE_CHUNK, (c + 1) * MOE_CHUNK)
            x = _unpack_bf16_pairs(xs_ref[rows, :]).astype(BF16)
            g = _dot(x, wg)
            a = (g * jax.nn.sigmoid(g)) * _dot(x, wu)
            y = _dot(a.astype(BF16), wd)
            ys_ref[rows, :] = _pack_bf16_pairs(y.astype(BF16).astype(F32))

    @pl.when(jnp.logical_not(active))
    def _():
        ys_ref[...] = jnp.zeros((MOE_ROWS, PACKED_DIM), U32)


def _moe_ffn(meta1, xs, wg, wu, wd):
    R = xs.shape[0]
    blk = lambda b, m: (jnp.minimum(b, m[META_NACT] - 1), 0)
    wsel = lambda b, m: (m[jnp.minimum(b, m[META_NACT] - 1)], 0, 0)
    return pl.pallas_call(
        _moe_ffn_body,
        grid_spec=pltpu.PrefetchScalarGridSpec(
            num_scalar_prefetch=1,
            grid=(R // MOE_ROWS,),
            in_specs=[pl.BlockSpec((MOE_ROWS, PACKED_DIM), blk),
                      pl.BlockSpec((1, D_MODEL, D_EXPERT), wsel),
                      pl.BlockSpec((1, D_MODEL, D_EXPERT), wsel),
                      pl.BlockSpec((1, D_EXPERT, D_MODEL), wsel)],
            out_specs=pl.BlockSpec((MOE_ROWS, PACKED_DIM), lambda b, m: (b, 0)),
        ),
        out_shape=jax.ShapeDtypeStruct((R, PACKED_DIM), U32),
        compiler_params=pltpu.CompilerParams(dimension_semantics=("arbitrary",),
                                             vmem_limit_bytes=VMEM_LIMIT_BYTES),
        name="moe_ffn",
    )(meta1, xs, wg, wu, wd)


def _combine_body(meta_ref, cnt_ref, carry_ref, x1_ref, loc_ref, wts_ref, fg_ref, ys_ref, out_ref,
                  yloc_ref, sems, *, tm):
    tile = pl.program_id(0)
    slot = lax.rem(tile, 2)

    def fetch(t, s):
        def make_copy(local, glob, n):
            return pltpu.make_async_copy(ys_ref.at[pl.ds(glob, n)], yloc_ref.at[s, pl.ds(local, n)],
                                         sems.at[s])
        _run_copies(t, cnt_ref, carry_ref, meta_ref, make_copy, tm)

    @pl.when(tile == 0)
    def _():
        yloc_ref[...] = jnp.zeros(yloc_ref.shape, U32)
        fetch(tile, slot)

    @pl.when(tile + 1 < pl.num_programs(0))
    def _():
        fetch(tile + 1, 1 - slot)

    n = _tile_rows(tile, cnt_ref)
    pltpu.make_async_copy(ys_ref.at[pl.ds(0, n)], yloc_ref.at[slot, pl.ds(0, n)], sems.at[slot]).wait()
    r = lax.broadcasted_iota(jnp.int32, (_loc_rows(tm), tm), 0)
    is0 = r == loc_ref[0:1, :]
    is1 = r == loc_ref[1:2, :]
    row_w = jnp.sum(jnp.where(is0, wts_ref[0:1, :], 0.0) + jnp.where(is1, wts_ref[1:2, :], 0.0),
                    axis=1, keepdims=True)
    yw = (row_w * _unpack_bf16_pairs(yloc_ref[slot])).astype(BF16)
    twohot = jnp.where(is0 | is1, 1.0, 0.0).astype(BF16)
    moe = lax.dot_general(twohot, yw, (((0,), (0,)), ((), ())), preferred_element_type=F32)
    out_ref[...] = _rms(x1_ref[...] + moe, fg_ref[...])


def _combine(meta1, cnt_tab, carry_tab, x1, loc, wts, fg, ys, *, tm):
    T = x1.shape[0]
    row = lambda i, *_: (i, 0)
    lane = lambda i, *_: (0, i)
    return pl.pallas_call(
        functools.partial(_combine_body, tm=tm),
        grid_spec=pltpu.PrefetchScalarGridSpec(
            num_scalar_prefetch=3,
            grid=(T // tm,),
            in_specs=[pl.BlockSpec((tm, D_MODEL), row),
                      pl.BlockSpec((2, tm), lane),
                      pl.BlockSpec((2, tm), lane),
                      pl.BlockSpec((1, D_MODEL), lambda i, *_: (0, 0)),
                      pl.BlockSpec(memory_space=pl.ANY)],
            out_specs=pl.BlockSpec((tm, D_MODEL), row),
            scratch_shapes=[pltpu.VMEM((2, _loc_rows(tm), PACKED_DIM), U32),
                            pltpu.SemaphoreType.DMA((2,))],
        ),
        out_shape=jax.ShapeDtypeStruct((T, D_MODEL), F32),
        compiler_params=pltpu.CompilerParams(dimension_semantics=("arbitrary",),
                                             vmem_limit_bytes=VMEM_LIMIT_BYTES),
        name="combine",
    )(meta1, cnt_tab, carry_tab, x1, loc, wts, fg, ys)


def _tile(n, t):
    t = min(n, t)
    assert n % t == 0, (n, t)
    return t


def kernel(x, mem, positions, mix_norm_g, w_in, gate_b, q_norm_g, w_uq, kv_norm_g, w_uk, w_uv, pool_w, pool_scale, mem_norm_g, w_mem_kv, w_br_pool, w_br_mla, w_br_mem, w_out, ffn_norm_g, w_router_group, b_router_group, w_router_expert, b_router_expert, w_gate_e, w_up_e, w_down_e, final_norm_g):
    B, S, D = x.shape
    assert D == D_MODEL and mix_norm_g.shape[0] == 1
    T = B * S
    mem_len = mem.shape[1]
    tm = _tile(S, 512)
    l = 0

    wi = w_in[l]
    o_qd, o_kv, o_kr, o_xq, o_gate = 512, 896, 1152, 1216, 1728
    win_p = jnp.concatenate(
        [wi[:, 0:o_qd], wi[:, o_qd:o_kv], wi[:, o_kv:o_kr], wi[:, o_xq:o_gate], wi[:, o_gate:],
         wi[:, o_kr:o_xq], jnp.zeros((D_MODEL, LANES - QK_ROPE_DIM), wi.dtype)], axis=1).astype(BF16)
    wuq_p = jnp.pad(w_uq[l].reshape(Q_LORA_RANK, MLA_HEADS, QK_NOPE_DIM + QK_ROPE_DIM),
                    ((0, 0), (0, 0), (0, QK_PAD_DIM - QK_NOPE_DIM - QK_ROPE_DIM))
                    ).reshape(Q_LORA_RANK, MLA_HEADS * QK_PAD_DIM).astype(BF16)
    inv_freq = 1.0 / (ROPE_THETA ** (jnp.arange(0, QK_ROPE_DIM, 2, dtype=F32) / QK_ROPE_DIM))
    invf = jnp.concatenate([inv_freq, inv_freq, jnp.zeros((LANES - QK_ROPE_DIM,), F32)])[None, :]
    wr = jnp.concatenate([w_router_expert[l], w_router_group[l],
                          jnp.zeros((D_MODEL, ROUTER_ROWS - N_EXPERTS - N_GROUPS), F32)], axis=1).T.astype(BF16)
    br = jnp.concatenate([b_router_expert[l], b_router_group[l],
                          jnp.zeros((ROUTER_ROWS - N_EXPERTS - N_GROUPS,), F32)])[:, None].astype(F32)
    x2 = x.reshape(T, D_MODEL)
    pos2 = positions.reshape(T, 1)

    ypool, xq, gates, q, k, v = _mixer_in(
        x2, pos2, invf, mix_norm_g[l][None, :], win_p, gate_b[l], q_norm_g[l][None, :], wuq_p,
        kv_norm_g[l][None, :], w_uk[l].astype(BF16), w_uv[l].astype(BF16), pool_w[l].astype(BF16),
        pool_scale[l][None, :], B=B, S=S, tm=tm)
    kmem, vmem = _mem_kv(mem.reshape(B * mem_len, D_MODEL), mem_norm_g[l][None, :], w_mem_kv[l].astype(BF16))
    ymla = _mla_attn(q, k, v, tq=tm).reshape(T, MLA_HEADS * V_HEAD_DIM)
    x1, h2, wts, loc, cnt_tab, carry_tab, counts = _merge(
        x2, ypool, ymla, xq, gates, kmem, vmem, w_br_pool[l].astype(BF16), w_br_mla[l].astype(BF16),
        w_br_mem[l].astype(BF16), w_out[l].astype(BF16), ffn_norm_g[l][None, :], wr, br,
        B=B, S=S, tm=tm, mem_len=mem_len)

    R = 2 * T + (T // tm) * N_EXPERTS * RUN_ALIGN + N_EXPERTS * MOE_ROWS
    assert R % MOE_ROWS == 0 and R // MOE_ROWS <= META_PAD_END
    meta1 = _moe_pos(counts).reshape(META_LANES)
    cnt1 = cnt_tab[:, 0]
    carry1 = carry_tab[:, 0]
    xs = _dispatch(meta1, cnt1, carry1, loc, h2, R=R, tm=tm)
    ys = _moe_ffn(meta1, xs, w_gate_e[l], w_up_e[l], w_down_e[l])
    out = _combine(meta1, cnt1, carry1, x1, loc, wts, final_norm_g[None, :], ys, tm=tm)
    return out.reshape(B, S, D_MODEL)
```

```python
import functools
import math

import jax
import jax.numpy as jnp
from jax import lax
from jax.experimental import pallas as pl
from jax.experimental.pallas import tpu as pltpu

D_MODEL = 1024
POOL_WINDOWS = (2, 4, 8, 16)
POOL_GROUP_DIM = 128
POOL_DIM = 512
MLA_HEADS = 8
QK_NOPE_DIM = 128
QK_ROPE_DIM = 64
V_HEAD_DIM = 128
Q_LORA_RANK = 384
KV_LORA_RANK = 256
ROPE_THETA = 10000.0
XATTN_HEADS = 4
XATTN_HEAD_DIM = 128
XATTN_DIM = 512
N_BRANCHES = 3
N_GROUPS = 4
EXPERTS_PER_GROUP = 8
N_EXPERTS = 32
D_EXPERT = 256
RMS_EPS = 1e-6
NEG_INF = -1e30

LANES = 128
QK_PAD_DIM = 2 * LANES
POOL_HALO = 16
MOE_ROWS = 512
MOE_CHUNK = 256
RUN_ALIGN = 8
PACKED_DIM = D_MODEL // 2
RARE_RUN = 64
ROUTER_ROWS = 40
META_LANES = 256
META_PAD_END = 192
META_NACT = 255
VMEM_LIMIT_BYTES = 56 * 1024 * 1024

IN_POOL, IN_QD, IN_KV, IN_XQ, IN_GATE, IN_KR, IN_END = 0, 512, 896, 1152, 1664, 4736, 4864
W_IN_KR, W_IN_XQ, W_IN_END = 1152, 1216, 4800

F32 = jnp.float32
BF16 = jnp.bfloat16
U32 = jnp.uint32


def _rms(x, g):
    ms = jnp.mean(x * x, axis=-1, keepdims=True)
    return (x * lax.rsqrt(ms + RMS_EPS)) * g


def _dot(a, b):
    return jnp.dot(a, b, preferred_element_type=F32)


def _dot_nt(a, b):
    return lax.dot_general(a, b, (((1,), (1,)), ((), ())), preferred_element_type=F32)


def _const_spec(shape):
    nd = len(shape)
    return pl.BlockSpec(shape, lambda *_: (0,) * nd, pipeline_mode=pl.Buffered(1))


def _pack_w_in_body(w_ref, o_ref):
    w = w_ref[...]
    rows = w.shape[0]
    o_ref[...] = jnp.concatenate(
        [w[:, 0:W_IN_KR], w[:, W_IN_XQ:W_IN_END], w[:, W_IN_KR:W_IN_XQ],
         jnp.zeros((rows, LANES - QK_ROPE_DIM), w.dtype)], axis=1).astype(BF16)


def _pack_w_in(w):
    tr = 128
    return pl.pallas_call(
        _pack_w_in_body,
        grid=(D_MODEL // tr,),
        in_specs=[pl.BlockSpec((tr, W_IN_END), lambda i: (i, 0))],
        out_specs=pl.BlockSpec((tr, IN_END), lambda i: (i, 0)),
        out_shape=jax.ShapeDtypeStruct((D_MODEL, IN_END), BF16),
        compiler_params=pltpu.CompilerParams(dimension_semantics=("arbitrary",)),
        name="pack_w_in",
    )(w)


def _mixer_in_body(x_ref, pos_ref, invf_ref, mixg_ref, win_ref, gateb_ref, qg_ref, wuq_ref,
                   kvg_ref, wuk_ref, wuv_ref, poolw_ref, pools_ref,
                   ypool_ref, xq_ref, gates_ref, q_ref, k_ref, v_ref, ext_ref,
                   *, tm, tiles_per_seq, q_scale):
    si = lax.rem(pl.program_id(0), tiles_per_seq)

    @pl.when(pl.program_id(0) == 0)
    def _():
        ext_ref[0:POOL_HALO, :] = jnp.zeros((POOL_HALO, POOL_DIM), F32)

    hb = _rms(x_ref[...], mixg_ref[...]).astype(BF16)

    u = _dot(hb, win_ref[:, IN_POOL:IN_QD])
    ext_ref[0:POOL_HALO, :] = jnp.where(si == 0, 0.0, ext_ref[0:POOL_HALO, :])
    ext_ref[POOL_HALO:POOL_HALO + tm, :] = u
    t_seq = lax.broadcasted_iota(jnp.int32, (tm, 1), 0) + si * tm
    for g, w in enumerate(POOL_WINDOWS):
        lo = g * POOL_GROUP_DIM
        hi = lo + POOL_GROUP_DIM
        acc = u[:, lo:hi]
        for j in range(1, w):
            acc = acc + ext_ref[POOL_HALO - j:POOL_HALO - j + tm, lo:hi]
        cnt = jnp.minimum(t_seq + 1, w).astype(F32)
        p = acc / cnt - u[:, lo:hi]
        y = _dot(p.astype(BF16), poolw_ref[g]) * pools_ref[:, lo:hi]
        ypool_ref[:, lo:hi] = y.astype(BF16)
    ext_ref[0:POOL_HALO, :] = ext_ref[tm:tm + POOL_HALO, :]

    ang = pos_ref[...].astype(F32) * invf_ref[...]
    cos = jnp.cos(ang)
    sin = jnp.sin(ang)
    first_half = lax.broadcasted_iota(jnp.int32, (tm, LANES), 1) < (QK_ROPE_DIM // 2)
    sin_signed = jnp.where(first_half, -sin, sin)

    def rope(r):
        swapped = jnp.where(first_half, pltpu.roll(r, LANES - QK_ROPE_DIM // 2, 1),
                            pltpu.roll(r, QK_ROPE_DIM // 2, 1))
        return r * cos + swapped * sin_signed

    cq = _rms(_dot(hb, win_ref[:, IN_QD:IN_KV]), qg_ref[...]).astype(BF16)
    for h in range(MLA_HEADS):
        qh = _dot(cq, wuq_ref[:, h * QK_PAD_DIM:(h + 1) * QK_PAD_DIM])
        q_ref[0, h, :, 0:LANES] = (qh[:, 0:LANES] * q_scale).astype(BF16)
        q_ref[0, h, :, LANES:QK_PAD_DIM] = (rope(qh[:, LANES:QK_PAD_DIM]) * q_scale).astype(BF16)

    ckv = _rms(_dot(hb, win_ref[:, IN_KV:IN_XQ]), kvg_ref[...]).astype(BF16)
    kr = rope(_dot(hb, win_ref[:, IN_KR:IN_END])).astype(BF16)
    for hp in range(MLA_HEADS // 2):
        cols = slice(hp * 2 * LANES, (hp + 1) * 2 * LANES)
        kn = _dot(ckv, wuk_ref[:, cols]).astype(BF16)
        vv = _dot(ckv, wuv_ref[:, cols]).astype(BF16)
        for j in range(2):
            h = 2 * hp + j
            k_ref[0, h, :, 0:LANES] = kn[:, j * LANES:(j + 1) * LANES]
            k_ref[0, h, :, LANES:QK_PAD_DIM] = kr
            v_ref[0, h] = vv[:, j * LANES:(j + 1) * LANES]

    xq_ref[...] = _dot(hb, win_ref[:, IN_XQ:IN_GATE]).astype(BF16)
    for c in range(N_BRANCHES):
        gl = _dot(hb, win_ref[:, IN_GATE + c * D_MODEL:IN_GATE + (c + 1) * D_MODEL])
        gates_ref[:, c * D_MODEL:(c + 1) * D_MODEL] = jax.nn.sigmoid(gl + gateb_ref[c:c + 1, :]).astype(BF16)


def _mixer_in(x2, pos2, invf, mixg, win_p, gate_b, qg, wuq_p, kvg, wuk, wuv, pool_w, pool_s, *, B, S, tm):
    T = B * S
    tps = S // tm
    q_scale = (QK_NOPE_DIM + QK_ROPE_DIM) ** -0.5 * math.log2(math.e)
    body = functools.partial(_mixer_in_body, tm=tm, tiles_per_seq=tps, q_scale=q_scale)
    row = lambda i: (i, 0)
    head = lambda i: (i // tps, 0, i % tps, 0)
    return pl.pallas_call(
        body,
        grid=(T // tm,),
        in_specs=[
            pl.BlockSpec((tm, D_MODEL), row),
            pl.BlockSpec((tm, 1), row),
            _const_spec((1, LANES)),
            _const_spec((1, D_MODEL)),
            _const_spec((D_MODEL, IN_END)),
            _const_spec((N_BRANCHES, D_MODEL)),
            _const_spec((1, Q_LORA_RANK)),
            _const_spec((Q_LORA_RANK, MLA_HEADS * QK_PAD_DIM)),
            _const_spec((1, KV_LORA_RANK)),
            _const_spec((KV_LORA_RANK, MLA_HEADS * QK_NOPE_DIM)),
            _const_spec((KV_LORA_RANK, MLA_HEADS * V_HEAD_DIM)),
            _const_spec((len(POOL_WINDOWS), POOL_GROUP_DIM, POOL_GROUP_DIM)),
            _const_spec((1, POOL_DIM)),
        ],
        out_specs=[
            pl.BlockSpec((tm, POOL_DIM), row),
            pl.BlockSpec((tm, XATTN_DIM), row),
            pl.BlockSpec((tm, N_BRANCHES * D_MODEL), row),
            pl.BlockSpec((1, MLA_HEADS, tm, QK_PAD_DIM), head),
            pl.BlockSpec((1, MLA_HEADS, tm, QK_PAD_DIM), head),
            pl.BlockSpec((1, MLA_HEADS, tm, V_HEAD_DIM), head),
        ],
        out_shape=[
            jax.ShapeDtypeStruct((T, POOL_DIM), BF16),
            jax.ShapeDtypeStruct((T, XATTN_DIM), BF16),
            jax.ShapeDtypeStruct((T, N_BRANCHES * D_MODEL), BF16),
            jax.ShapeDtypeStruct((B, MLA_HEADS, S, QK_PAD_DIM), BF16),
            jax.ShapeDtypeStruct((B, MLA_HEADS, S, QK_PAD_DIM), BF16),
            jax.ShapeDtypeStruct((B, MLA_HEADS, S, V_HEAD_DIM), BF16),
        ],
        scratch_shapes=[pltpu.VMEM((tm + POOL_HALO, POOL_DIM), F32)],
        compiler_params=pltpu.CompilerParams(dimension_semantics=("arbitrary",),
                                             vmem_limit_bytes=VMEM_LIMIT_BYTES),
        name="mixer_in",
    )(x2, pos2, invf, mixg, win_p, gate_b, qg, wuq_p, kvg, wuk, wuv, pool_w, pool_s)


def _mem_kv_body(mem_ref, g_ref, w_ref, k_ref, v_ref):
    mb = _rms(mem_ref[...], g_ref[...]).astype(BF16)
    kv = _dot(mb, w_ref[...])
    k_ref[...] = kv[:, 0:XATTN_DIM].astype(BF16)
    v_ref[...] = kv[:, XATTN_DIM:2 * XATTN_DIM].astype(BF16)


def _mem_kv(mem2, g, w):
    rows = mem2.shape[0]
    tr = min(rows, 512)
    return pl.pallas_call(
        _mem_kv_body,
        grid=(rows // tr,),
        in_specs=[pl.BlockSpec((tr, D_MODEL), lambda i: (i, 0)),
                  _const_spec((1, D_MODEL)),
                  _const_spec((D_MODEL, 2 * XATTN_DIM))],
        out_specs=[pl.BlockSpec((tr, XATTN_DIM), lambda i: (i, 0)),
                   pl.BlockSpec((tr, XATTN_DIM), lambda i: (i, 0))],
        out_shape=[jax.ShapeDtypeStruct((rows, XATTN_DIM), BF16),
                   jax.ShapeDtypeStruct((rows, XATTN_DIM), BF16)],
        compiler_params=pltpu.CompilerParams(dimension_semantics=("arbitrary",)),
        name="mem_kv",
    )(mem2, g, w)


def _attn_unrolled_body(q_ref, k_ref, v_ref, o_ref, s_a, s_b, mc_a, mc_b, m_ref, l_ref, acc_ref, *, nq, tq):
    s_bufs = (s_a, s_b)
    mc_bufs = (mc_a, mc_b)
    mxu_row_split = 2
    blocks = [(qi, kb) for qi in range(nq) for kb in range(qi + 1)]

    def scores(i, slot):
        qi, kb = blocks[i]
        s = _dot_nt(q_ref[0, 0, qi * tq:(qi + 1) * tq, :], k_ref[0, 0, kb * tq:(kb + 1) * tq, :])
        if qi == kb:
            ri = lax.broadcasted_iota(jnp.int32, (tq, tq), 0)
            ci = lax.broadcasted_iota(jnp.int32, (tq, tq), 1)
            s = jnp.where(ci <= ri, s, NEG_INF)
        s_bufs[slot][...] = s
        mc_bufs[slot][...] = jnp.broadcast_to(jnp.max(s, axis=1, keepdims=True), (tq, LANES))

    def accumulate(i, slot):
        qi, kb = blocks[i]
        is_first = kb == 0
        is_last = kb == qi
        if is_first:
            m_new = mc_bufs[slot][...]
        else:
            m_prev = m_ref[...]
            m_new = jnp.maximum(m_prev, mc_bufs[slot][...])
            alpha = jnp.exp2(m_prev - m_new)
        p = jnp.exp2(s_bufs[slot][...] - jnp.concatenate([m_new] * (tq // LANES), axis=1))
        psum = p[:, 0:LANES]
        for c in range(1, tq // LANES):
            psum = psum + p[:, c * LANES:(c + 1) * LANES]
        l_new = psum if is_first else alpha * l_ref[...] + psum
        pb = p.astype(BF16)
        v = v_ref[0, 0, kb * tq:(kb + 1) * tq, :]
        if is_last:
            inv = 1.0 / jnp.sum(l_new, axis=1, keepdims=True)
        else:
            l_ref[...] = l_new
            m_ref[...] = m_new
        h = tq // mxu_row_split
        for r in range(mxu_row_split):
            rows = slice(r * h, (r + 1) * h)
            acc = _dot(pb[rows, :], v)
            if not is_first:
                acc = alpha[rows, :] * acc_ref[rows, :] + acc
            if is_last:
                o_ref[0, qi * tq + r * h:qi * tq + (r + 1) * h, :] = (acc * inv[rows, :]).astype(BF16)
            else:
                acc_ref[rows, :] = acc

    scores(0, 0)
    for i in range(len(blocks)):
        if i + 1 < len(blocks):
            scores(i + 1, (i + 1) % 2)
        accumulate(i, i % 2)


def _mla_attn_unrolled(q, k, v, *, tq):
    B, H, S, _ = q.shape
    per_head = lambda b, h: (b, h, 0, 0)
    return pl.pallas_call(
        functools.partial(_attn_unrolled_body, nq=S // tq, tq=tq),
        grid=(B, H),
        in_specs=[pl.BlockSpec((1, 1, S, QK_PAD_DIM), per_head),
                  pl.BlockSpec((1, 1, S, QK_PAD_DIM), per_head),
                  pl.BlockSpec((1, 1, S, V_HEAD_DIM), per_head)],
        out_specs=pl.BlockSpec((1, S, V_HEAD_DIM), lambda b, h: (b, 0, h)),
        out_shape=jax.ShapeDtypeStruct((B, S, H * V_HEAD_DIM), BF16),
        scratch_shapes=[pltpu.VMEM((tq, tq), F32), pltpu.VMEM((tq, tq), F32),
                        pltpu.VMEM((tq, LANES), F32), pltpu.VMEM((tq, LANES), F32),
                        pltpu.VMEM((tq, LANES), F32), pltpu.VMEM((tq, LANES), F32),
                        pltpu.VMEM((tq, V_HEAD_DIM), F32)],
        compiler_params=pltpu.CompilerParams(dimension_semantics=("arbitrary", "arbitrary"),
                                             vmem_limit_bytes=VMEM_LIMIT_BYTES),
        name="mla_attn",
    )(q, k, v)


def _merge_body(x_ref, ypool_ref, ymla_ref, xq_ref, gates_ref, kmem_ref, vmem_ref,
                wbp_ref, wbm_ref, wbx_ref, wout_ref, ffng_ref, wr_ref, br_ref,
                x1_ref, h2_ref, wts_ref, loc_ref, cnt_tab_ref, carry_tab_ref, counts_ref, carry_ref, *, tm):
    @pl.when(pl.program_id(0) == 0)
    def _():
        carry_ref[...] = jnp.zeros((N_EXPERTS, LANES), F32)

    xq = xq_ref[...]
    parts = []
    for h in range(XATTN_HEADS):
        cols = slice(h * XATTN_HEAD_DIM, (h + 1) * XATTN_HEAD_DIM)
        s = _dot_nt(xq[:, cols], kmem_ref[:, cols]) * (XATTN_HEAD_DIM ** -0.5)
        e = jnp.exp(s - jnp.max(s, axis=1, keepdims=True))
        p = e / jnp.sum(e, axis=1, keepdims=True)
        parts.append(_dot(p.astype(BF16), vmem_ref[:, cols]))
    ymem = jnp.concatenate(parts, axis=1).astype(BF16)

    gates = gates_ref[...].astype(F32)
    merged = (gates[:, 0:D_MODEL] * _dot(ypool_ref[...], wbp_ref[...])
              + gates[:, D_MODEL:2 * D_MODEL] * _dot(ymla_ref[...], wbm_ref[...])
              + gates[:, 2 * D_MODEL:3 * D_MODEL] * _dot(ymem, wbx_ref[...]))
    x1 = x_ref[...] + _dot(merged.astype(BF16), wout_ref[...])
    x1_ref[...] = x1
    h2 = _rms(x1, ffng_ref[...]).astype(BF16)
    h2_ref[...] = h2

    lt = _dot_nt(wr_ref[...], h2) + br_ref[...]
    gl = lt[N_EXPERTS:N_EXPERTS + N_GROUPS, :]
    gmax = jnp.max(gl, axis=0, keepdims=True)
    r4 = lax.broadcasted_iota(jnp.int32, (N_GROUPS, tm), 0).astype(F32)
    gidx = jnp.min(jnp.where(gl == gmax, r4, float(N_GROUPS)), axis=0, keepdims=True)
    pg = 1.0 / jnp.sum(jnp.exp(gl - gmax), axis=0, keepdims=True)
    esel = lt[0:EXPERTS_PER_GROUP, :]
    for g in range(1, N_GROUPS):
        esel = jnp.where(gidx == float(g), lt[g * EXPERTS_PER_GROUP:(g + 1) * EXPERTS_PER_GROUP, :], esel)
    r8 = lax.broadcasted_iota(jnp.int32, (EXPERTS_PER_GROUP, tm), 0).astype(F32)
    m1 = jnp.max(esel, axis=0, keepdims=True)
    i1 = jnp.min(jnp.where(esel == m1, r8, float(EXPERTS_PER_GROUP)), axis=0, keepdims=True)
    rest = jnp.where(r8 == i1, -jnp.inf, esel)
    m2 = jnp.max(rest, axis=0, keepdims=True)
    i2 = jnp.min(jnp.where(rest == m2, r8, float(EXPERTS_PER_GROUP)), axis=0, keepdims=True)
    e2 = jnp.exp(m2 - m1)
    den = 1.0 + e2
    wts_ref[0:1, :] = pg / den
    wts_ref[1:2, :] = pg * e2 / den
    ex1 = gidx * float(EXPERTS_PER_GROUP) + i1
    ex2 = gidx * float(EXPERTS_PER_GROUP) + i2

    r32 = lax.broadcasted_iota(jnp.int32, (N_EXPERTS, tm), 0).astype(F32)
    is1 = r32 == ex1
    is2 = r32 == ex2
    member = jnp.where(is1 | is2, 1.0, 0.0)
    upper = jnp.where(lax.broadcasted_iota(jnp.int32, (tm, tm), 0)
                      <= lax.broadcasted_iota(jnp.int32, (tm, tm), 1), 1.0, 0.0).astype(BF16)
    incl = _dot(member.astype(BF16), upper)
    run = jnp.floor((jnp.sum(member, axis=1, keepdims=True) + (RUN_ALIGN - 1)) / RUN_ALIGN) * RUN_ALIGN
    rcol = lax.broadcasted_iota(jnp.int32, (N_EXPERTS, 1), 0)
    run_start = jnp.zeros((N_EXPERTS, 1), F32)
    for e in range(N_EXPERTS - 1):
        run_start = run_start + jnp.where(rcol > e, run[e:e + 1, :], 0.0)
    pos = incl - 1.0 + run_start
    loc_ref[0:1, :] = jnp.sum(jnp.where(is1, pos, 0.0), axis=0, keepdims=True).astype(jnp.int32)
    loc_ref[1:2, :] = jnp.sum(jnp.where(is2, pos, 0.0), axis=0, keepdims=True).astype(jnp.int32)
    carry = carry_ref[...]
    total = carry + run
    cnt_tab_ref[...] = jnp.broadcast_to(run, (N_EXPERTS, LANES)).astype(jnp.int32)
    carry_tab_ref[...] = carry.astype(jnp.int32)
    carry_ref[...] = total
    counts_ref[...] = total.astype(jnp.int32)


def _merge(x2, ypool, ymla, xq, gates, kmem, vmem, wbp, wbm, wbx, wout, ffng, wr, br, *, B, S, tm, mem_len):
    T = B * S
    tps = S // tm
    row = lambda i: (i, 0)
    lane = lambda i: (0, i)
    memb = lambda i: (i // tps, 0)
    return pl.pallas_call(
        functools.partial(_merge_body, tm=tm),
        grid=(T // tm,),
        in_specs=[
            pl.BlockSpec((tm, D_MODEL), row),
            pl.BlockSpec((tm, POOL_DIM), row),
            pl.BlockSpec((tm, MLA_HEADS * V_HEAD_DIM), row),
            pl.BlockSpec((tm, XATTN_DIM), row),
            pl.BlockSpec((tm, N_BRANCHES * D_MODEL), row),
            pl.BlockSpec((mem_len, XATTN_DIM), memb),
            pl.BlockSpec((mem_len, XATTN_DIM), memb),
            _const_spec((POOL_DIM, D_MODEL)),
            _const_spec((MLA_HEADS * V_HEAD_DIM, D_MODEL)),
            _const_spec((XATTN_DIM, D_MODEL)),
            _const_spec((D_MODEL, D_MODEL)),
            _const_spec((1, D_MODEL)),
            _const_spec((ROUTER_ROWS, D_MODEL)),
            _const_spec((ROUTER_ROWS, 1)),
        ],
        out_specs=[
            pl.BlockSpec((tm, D_MODEL), row),
            pl.BlockSpec((tm, D_MODEL), row),
            pl.BlockSpec((2, tm), lane),
            pl.BlockSpec((2, tm), lane),
            pl.BlockSpec((N_EXPERTS, LANES), row),
            pl.BlockSpec((N_EXPERTS, LANES), row),
            pl.BlockSpec((N_EXPERTS, LANES), lambda i: (0, 0)),
        ],
        out_shape=[
            jax.ShapeDtypeStruct((T, D_MODEL), F32),
            jax.ShapeDtypeStruct((T, D_MODEL), BF16),
            jax.ShapeDtypeStruct((2, T), F32),
            jax.ShapeDtypeStruct((2, T), jnp.int32),
            jax.ShapeDtypeStruct((T // tm * N_EXPERTS, LANES), jnp.int32),
            jax.ShapeDtypeStruct((T // tm * N_EXPERTS, LANES), jnp.int32),
            jax.ShapeDtypeStruct((N_EXPERTS, LANES), jnp.int32),
        ],
        scratch_shapes=[pltpu.VMEM((N_EXPERTS, LANES), F32)],
        compiler_params=pltpu.CompilerParams(dimension_semantics=("arbitrary",),
                                             vmem_limit_bytes=VMEM_LIMIT_BYTES),
        name="merge",
    )(x2, ypool, ymla, xq, gates, kmem, vmem, wbp, wbm, wbx, wout, ffng, wr, br)


def _moe_pos_body(counts_ref, meta_ref):
    shift = int(math.log2(MOE_ROWS))
    cnt = counts_ref[...]
    padded = lax.shift_left(lax.shift_right_logical(cnt + (MOE_ROWS - 1), shift), shift)
    r32 = lax.broadcasted_iota(jnp.int32, (N_EXPERTS, LANES), 0)
    pad_start = jnp.zeros((N_EXPERTS, LANES), jnp.int32)
    for e in range(N_EXPERTS - 1):
        pad_start = pad_start + jnp.where(r32 > e, padded[e:e + 1, :], 0)
    pad_end = pad_start + padded

    lane = lax.broadcasted_iota(jnp.int32, (1, META_LANES), 1)
    block_row = lane * MOE_ROWS
    blk_e = jnp.zeros((1, META_LANES), jnp.int32)
    pe_row = jnp.zeros((1, META_LANES), jnp.int32)
    for e in range(N_EXPERTS):
        pe = pad_end[e:e + 1, 0:1]
        blk_e = blk_e + jnp.where(pe <= block_row, 1, 0)
        pe_row = pe_row + jnp.where(lane == META_PAD_END + e, pe, 0)
    blk_e = jnp.minimum(blk_e, N_EXPERTS - 1)
    nact = lax.shift_right_logical(pad_end[N_EXPERTS - 1:N_EXPERTS, 0:1], shift)
    meta = jnp.where(lane < META_PAD_END, blk_e, pe_row)
    meta_ref[...] = jnp.where(lane == META_NACT, nact, meta)


def _moe_pos(counts):
    full = lambda shape: pl.BlockSpec(shape, lambda i: (0,) * len(shape))
    return pl.pallas_call(
        _moe_pos_body,
        grid=(1,),
        in_specs=[full((N_EXPERTS, LANES))],
        out_specs=full((1, META_LANES)),
        out_shape=jax.ShapeDtypeStruct((1, META_LANES), jnp.int32),
        compiler_params=pltpu.CompilerParams(dimension_semantics=("arbitrary",)),
        name="moe_pos",
    )(counts)


def _pack_bf16_pairs(x):
    lo = pltpu.bitcast(x[:, 0:PACKED_DIM], U32)
    hi = pltpu.bitcast(x[:, PACKED_DIM:D_MODEL], U32)
    return hi | lax.shift_right_logical(lo, jnp.uint32(16))


def _unpack_bf16_pairs(w):
    lo = pltpu.bitcast(lax.shift_left(w, jnp.uint32(16)), F32)
    hi = pltpu.bitcast(w & jnp.uint32(0xFFFF0000), F32)
    return jnp.concatenate([lo, hi], axis=1)


def _loc_rows(tm):
    return 2 * tm + N_EXPERTS * RUN_ALIGN


def _run_copies(tile, cnt_ref, carry_ref, meta_ref, make_copy, max_run):
    bits = [b for b in (max_run >> i for i in range(max_run.bit_length())) if b >= RUN_ALIGN]
    rare = [b for b in bits if b >= RARE_RUN]
    common = [b for b in bits if b < RARE_RUN]

    def per_expert(e, local):
        n = cnt_ref[tile * N_EXPERTS + e]
        start = jnp.where(e == 0, 0, meta_ref[META_PAD_END + jnp.maximum(e - 1, 0)])
        glob = start + carry_ref[tile * N_EXPERTS + e]

        def copy_bits(group):
            for bit in group:
                done = n & ~(2 * bit - 1)

                @pl.when(n & bit != 0)
                def _(bit=bit, done=done):
                    make_copy(pl.multiple_of(local + done, RUN_ALIGN),
                              pl.multiple_of(glob + done, RUN_ALIGN), bit).start()

        if rare:
            @pl.when(n >= RARE_RUN)
            def _():
                copy_bits(rare)
        copy_bits(common)
        return local + n

    return pl.multiple_of(lax.fori_loop(0, N_EXPERTS, per_expert, 0), RUN_ALIGN)


def _tile_rows(tile, cnt_ref):
    total = lax.fori_loop(0, N_EXPERTS, lambda e, t: t + cnt_ref[tile * N_EXPERTS + e], 0)
    return pl.multiple_of(total, RUN_ALIGN)


def _dispatch_body(meta_ref, cnt_ref, carry_ref, loc_ref, h2_ref, xs_ref, xloc_ref, zero_ref, sems, zsem,
                   *, tm, n_blocks):
    tile = pl.program_id(0)
    slot = lax.rem(tile, 2)

    def wait_rows(t, s):
        n = _tile_rows(t, cnt_ref)
        pltpu.make_async_copy(xloc_ref.at[s, pl.ds(0, n)], xs_ref.at[pl.ds(0, n)], sems.at[s]).wait()

    def pad_copy(e):
        end = pl.multiple_of(meta_ref[META_PAD_END + e], MOE_ROWS)
        return pltpu.make_async_copy(zero_ref, xs_ref.at[pl.ds(end - MOE_ROWS, MOE_ROWS)], zsem)

    def has_rows(e):
        prev = jnp.where(e == 0, 0, meta_ref[META_PAD_END + jnp.maximum(e - 1, 0)])
        return meta_ref[META_PAD_END + e] > prev

    @pl.when(tile == 0)
    def _():
        zero_ref[...] = jnp.zeros((MOE_ROWS, PACKED_DIM), U32)

        def start(e, c):
            @pl.when(has_rows(e))
            def _():
                pad_copy(e).start()
            return c

        def wait(e, c):
            @pl.when(has_rows(e))
            def _():
                pad_copy(e).wait()
            return c

        def tail_copy(b):
            return pltpu.make_async_copy(
                zero_ref, xs_ref.at[pl.ds(pl.multiple_of(b * MOE_ROWS, MOE_ROWS), MOE_ROWS)], zsem)

        def tail_start(b, c):
            tail_copy(b).start()
            return c

        def tail_wait(b, c):
            tail_copy(b).wait()
            return c

        nact = meta_ref[META_NACT]
        lax.fori_loop(0, N_EXPERTS, start, 0)
        lax.fori_loop(nact, n_blocks, tail_start, 0)
        lax.fori_loop(0, N_EXPERTS, wait, 0)
        lax.fori_loop(nact, n_blocks, tail_wait, 0)

    @pl.when(tile >= 2)
    def _():
        wait_rows(tile - 2, slot)

    r = lax.broadcasted_iota(jnp.int32, (_loc_rows(tm), tm), 0)
    onehot = jnp.where((r == loc_ref[0:1, :]) | (r == loc_ref[1:2, :]), 1.0, 0.0).astype(BF16)
    xloc_ref[slot] = _pack_bf16_pairs(_dot(onehot, h2_ref[...]))

    def make_copy(local, glob, n):
        return pltpu.make_async_copy(xloc_ref.at[slot, pl.ds(local, n)], xs_ref.at[pl.ds(glob, n)],
                                     sems.at[slot])

    _run_copies(tile, cnt_ref, carry_ref, meta_ref, make_copy, tm)

    @pl.when(tile == pl.num_programs(0) - 1)
    def _():
        @pl.when(tile >= 1)
        def _():
            wait_rows(tile - 1, 1 - slot)

        wait_rows(tile, slot)


def _dispatch(meta1, cnt_tab, carry_tab, loc, h2, *, R, tm):
    T = h2.shape[0]
    return pl.pallas_call(
        functools.partial(_dispatch_body, tm=tm, n_blocks=R // MOE_ROWS),
        grid_spec=pltpu.PrefetchScalarGridSpec(
            num_scalar_prefetch=3,
            grid=(T // tm,),
            in_specs=[pl.BlockSpec((2, tm), lambda i, *_: (0, i)),
                      pl.BlockSpec((tm, D_MODEL), lambda i, *_: (i, 0))],
            out_specs=pl.BlockSpec(memory_space=pl.ANY),
            scratch_shapes=[pltpu.VMEM((2, _loc_rows(tm), PACKED_DIM), U32),
                            pltpu.VMEM((MOE_ROWS, PACKED_DIM), U32),
                            pltpu.SemaphoreType.DMA((2,)), pltpu.SemaphoreType.DMA],
        ),
        out_shape=jax.ShapeDtypeStruct((R, PACKED_DIM), U32),
        compiler_params=pltpu.CompilerParams(dimension_semantics=("arbitrary",),
                                             vmem_limit_bytes=VMEM_LIMIT_BYTES),
        name="dispatch",
    )(meta1, cnt_tab, carry_tab, loc, h2)


def _moe_ffn_body(meta_ref, xs_ref, wg_ref, wu_ref, wd_ref, ys_ref):
    active = pl.program_id(0) < meta_ref[META_NACT]

    @pl.when(active)
    def _():
        wg = wg_ref[0].astype(BF16)
        wu = wu_ref[0].astype(BF16)
        wd = wd_ref[0].astype(BF16)
        for c in range(MOE_ROWS // MOE_CHUNK):
            rows = slice(c * MOE_CHUNK, (c + 1) * MOE_CHUNK)
            x = _unpack_bf16_pairs(xs_ref[rows, :]).astype(BF16)
            g = _dot(x, wg)
            a = (g * jax.nn.sigmoid(g)) * _dot(x, wu)
            y = _dot(a.astype(BF16), wd)
            ys_ref[rows, :] = _pack_bf16_pairs(y.astype(BF16).astype(F32))

    @pl.when(jnp.logical_not(active))
    def _():
        ys_ref[...] = jnp.zeros((MOE_ROWS, PACKED_DIM), U32)


def _moe_ffn(meta1, xs, wg, wu, wd):
    R = xs.shape[0]
    blk = lambda b, m: (jnp.minimum(b, m[META_NACT] - 1), 0)
    wsel = lambda b, m: (m[jnp.minimum(b, m[META_NACT] - 1)], 0, 0)
    return pl.pallas_call(
        _moe_ffn_body,
        grid_spec=pltpu.PrefetchScalarGridSpec(
            num_scalar_prefetch=1,
            grid=(R // MOE_ROWS,),
            in_specs=[pl.BlockSpec((MOE_ROWS, PACKED_DIM), blk),
                      pl.BlockSpec((1, D_MODEL, D_EXPERT), wsel),
                      pl.BlockSpec((1, D_MODEL, D_EXPERT), wsel),
                      pl.BlockSpec((1, D_EXPERT, D_MODEL), wsel)],
            out_specs=pl.BlockSpec((MOE_ROWS, PACKED_DIM), lambda b, m: (b, 0)),
        ),
        out_shape=jax.ShapeDtypeStruct((R, PACKED_DIM), U32),
        compiler_params=pltpu.CompilerParams(dimension_semantics=("arbitrary",),
                                             vmem_limit_bytes=VMEM_LIMIT_BYTES),
        name="moe_ffn",
    )(meta1, xs, wg, wu, wd)


def _combine_body(meta_ref, cnt_ref, carry_ref, x1_ref, loc_ref, wts_ref, fg_ref, ys_ref, out_ref,
                  yloc_ref, sems, *, tm):
    tile = pl.program_id(0)
    slot = lax.rem(tile, 2)

    def fetch(t, s):
        def make_copy(local, glob, n):
            return pltpu.make_async_copy(ys_ref.at[pl.ds(glob, n)], yloc_ref.at[s, pl.ds(local, n)],
                                         sems.at[s])
        _run_copies(t, cnt_ref, carry_ref, meta_ref, make_copy, tm)

    @pl.when(tile == 0)
    def _():
        yloc_ref[...] = jnp.zeros(yloc_ref.shape, U32)
        fetch(tile, slot)

    @pl.when(tile + 1 < pl.num_programs(0))
    def _():
        fetch(tile + 1, 1 - slot)

    n = _tile_rows(tile, cnt_ref)
    pltpu.make_async_copy(ys_ref.at[pl.ds(0, n)], yloc_ref.at[slot, pl.ds(0, n)], sems.at[slot]).wait()
    r = lax.broadcasted_iota(jnp.int32, (_loc_rows(tm), tm), 0)
    is0 = r == loc_ref[0:1, :]
    is1 = r == loc_ref[1:2, :]
    row_w = jnp.sum(jnp.where(is0, wts_ref[0:1, :], 0.0) + jnp.where(is1, wts_ref[1:2, :], 0.0),
                    axis=1, keepdims=True)
    yw = (row_w * _unpack_bf16_pairs(yloc_ref[slot])).astype(BF16)
    twohot = jnp.where(is0 | is1, 1.0, 0.0).astype(BF16)
    moe = lax.dot_general(twohot, yw, (((0,), (0,)), ((), ())), preferred_element_type=F32)
    out_ref[...] = _rms(x1_ref[...] + moe, fg_ref[...])


def _combine(meta1, cnt_tab, carry_tab, x1, loc, wts, fg, ys, *, tm):
    T = x1.shape[0]
    row = lambda i, *_: (i, 0)
    lane = lambda i, *_: (0, i)
    return pl.pallas_call(
        functools.partial(_combine_body, tm=tm),
        grid_spec=pltpu.PrefetchScalarGridSpec(
            num_scalar_prefetch=3,
            grid=(T // tm,),
            in_specs=[pl.BlockSpec((tm, D_MODEL), row),
                      pl.BlockSpec((2, tm), lane),
                      pl.BlockSpec((2, tm), lane),
                      pl.BlockSpec((1, D_MODEL), lambda i, *_: (0, 0)),
                      pl.BlockSpec(memory_space=pl.ANY)],
            out_specs=pl.BlockSpec((tm, D_MODEL), row),
            scratch_shapes=[pltpu.VMEM((2, _loc_rows(tm), PACKED_DIM), U32),
                            pltpu.SemaphoreType.DMA((2,))],
        ),
        out_shape=jax.ShapeDtypeStruct((T, D_MODEL), F32),
        compiler_params=pltpu.CompilerParams(dimension_semantics=("arbitrary",),
                                             vmem_limit_bytes=VMEM_LIMIT_BYTES),
        name="combine",
    )(meta1, cnt_tab, carry_tab, x1, loc, wts, fg, ys)


def _tile(n, t):
    t = min(n, t)
    assert n % t == 0, (n, t)
    return t


def kernel(x, mem, positions, mix_norm_g, w_in, gate_b, q_norm_g, w_uq, kv_norm_g, w_uk, w_uv, pool_w, pool_scale, mem_norm_g, w_mem_kv, w_br_pool, w_br_mla, w_br_mem, w_out, ffn_norm_g, w_router_group, b_router_group, w_router_expert, b_router_expert, w_gate_e, w_up_e, w_down_e, final_norm_g):
    B, S, D = x.shape
    assert D == D_MODEL and mix_norm_g.shape[0] == 1
    T = B * S
    mem_len = mem.shape[1]
    tm = _tile(S, 512)
    l = 0

    win_p = _pack_w_in(w_in[l])
    wuq_p = jnp.pad(w_uq[l].reshape(Q_LORA_RANK, MLA_HEADS, QK_NOPE_DIM + QK_ROPE_DIM),
                    ((0, 0), (0, 0), (0, QK_PAD_DIM - QK_NOPE_DIM - QK_ROPE_DIM))
                    ).reshape(Q_LORA_RANK, MLA_HEADS * QK_PAD_DIM).astype(BF16)
    inv_freq = 1.0 / (ROPE_THETA ** (jnp.arange(0, QK_ROPE_DIM, 2, dtype=F32) / QK_ROPE_DIM))
    invf = jnp.concatenate([inv_freq, inv_freq, jnp.zeros((LANES - QK_ROPE_DIM,), F32)])[None, :]
    wr = jnp.concatenate([w_router_expert[l], w_router_group[l],
                          jnp.zeros((D_MODEL, ROUTER_ROWS - N_EXPERTS - N_GROUPS), F32)], axis=1).T.astype(BF16)
    br = jnp.concatenate([b_router_expert[l], b_router_group[l],
                          jnp.zeros((ROUTER_ROWS - N_EXPERTS - N_GROUPS,), F32)])[:, None].astype(F32)
    x2 = x.reshape(T, D_MODEL)
    pos2 = positions.reshape(T, 1)

    ypool, xq, gates, q, k, v = _mixer_in(
        x2, pos2, invf, mix_norm_g[l][None, :], win_p, gate_b[l], q_norm_g[l][None, :], wuq_p,
        kv_norm_g[l][None, :], w_uk[l].astype(BF16), w_uv[l].astype(BF16), pool_w[l].astype(BF16),
        pool_scale[l][None, :], B=B, S=S, tm=tm)
    kmem, vmem = _mem_kv(mem.reshape(B * mem_len, D_MODEL), mem_norm_g[l][None, :], w_mem_kv[l].astype(BF16))
    ymla = _mla_attn_unrolled(q, k, v, tq=tm).reshape(T, MLA_HEADS * V_HEAD_DIM)
    x1, h2, wts, loc, cnt_tab, carry_tab, counts = _merge(
        x2, ypool, ymla, xq, gates, kmem, vmem, w_br_pool[l].astype(BF16), w_br_mla[l].astype(BF16),
        w_br_mem[l].astype(BF16), w_out[l].astype(BF16), ffn_norm_g[l][None, :], wr, br,
        B=B, S=S, tm=tm, mem_len=mem_len)

    R = 2 * T + (T // tm) * N_EXPERTS * RUN_ALIGN + N_EXPERTS * MOE_ROWS
    assert R % MOE_ROWS == 0 and R // MOE_ROWS <= META_PAD_END
    meta1 = _moe_pos(counts).reshape(META_LANES)
    cnt1 = cnt_tab[:, 0]
    carry1 = carry_tab[:, 0]
    xs = _dispatch(meta1, cnt1, carry1, loc, h2, R=R, tm=tm)
    ys = _moe_ffn(meta1, xs, w_gate_e[l], w_up_e[l], w_down_e[l])
    out = _combine(meta1, cnt1, carry1, x1, loc, wts, final_norm_g[None, :], ys, tm=tm)
    return out.reshape(B, S, D_MODEL)
```

```python
import functools
import math

import jax
import jax.numpy as jnp
from jax import lax
from jax.experimental import pallas as pl
from jax.experimental.pallas import tpu as pltpu

D_MODEL = 1024
POOL_WINDOWS = (2, 4, 8, 16)
POOL_GROUP_DIM = 128
POOL_DIM = 512
MLA_HEADS = 8
QK_NOPE_DIM = 128
QK_ROPE_DIM = 64
V_HEAD_DIM = 128
Q_LORA_RANK = 384
KV_LORA_RANK = 256
ROPE_THETA = 10000.0
XATTN_HEADS = 4
XATTN_HEAD_DIM = 128
XATTN_DIM = 512
N_BRANCHES = 3
N_GROUPS = 4
EXPERTS_PER_GROUP = 8
N_EXPERTS = 32
D_EXPERT = 256
RMS_EPS = 1e-6
NEG_INF = -1e30

LANES = 128
QK_PAD_DIM = 2 * LANES
POOL_HALO = 16
MOE_ROWS = 512
MOE_CHUNK = 256
RUN_ALIGN = 8
PACKED_DIM = D_MODEL // 2
ROUTER_ROWS = 40
META_LANES = 256
META_PAD_END = 192
META_NACT = 255
VMEM_LIMIT_BYTES = 56 * 1024 * 1024

IN_POOL, IN_QD, IN_KV, IN_XQ, IN_GATE, IN_KR, IN_END = 0, 512, 896, 1152, 1664, 4736, 4864
W_IN_KR, W_IN_XQ, W_IN_END = 1152, 1216, 4800

F32 = jnp.float32
BF16 = jnp.bfloat16
U32 = jnp.uint32


def _rms(x, g):
    ms = jnp.mean(x * x, axis=-1, keepdims=True)
    return (x * lax.rsqrt(ms + RMS_EPS)) * g


def _dot(a, b):
    return jnp.dot(a, b, preferred_element_type=F32)


def _dot_nt(a, b):
    return lax.dot_general(a, b, (((1,), (1,)), ((), ())), preferred_element_type=F32)


def _const_spec(shape):
    nd = len(shape)
    return pl.BlockSpec(shape, lambda *_: (0,) * nd, pipeline_mode=pl.Buffered(1))


def _pack_w_in_body(w_ref, o_ref):
    w = w_ref[...]
    rows = w.shape[0]
    o_ref[...] = jnp.concatenate(
        [w[:, 0:W_IN_KR], w[:, W_IN_XQ:W_IN_END], w[:, W_IN_KR:W_IN_XQ],
         jnp.zeros((rows, LANES - QK_ROPE_DIM), w.dtype)], axis=1).astype(BF16)


def _pack_w_in(w):
    tr = 128
    return pl.pallas_call(
        _pack_w_in_body,
        grid=(D_MODEL // tr,),
        in_specs=[pl.BlockSpec((tr, W_IN_END), lambda i: (i, 0))],
        out_specs=pl.BlockSpec((tr, IN_END), lambda i: (i, 0)),
        out_shape=jax.ShapeDtypeStruct((D_MODEL, IN_END), BF16),
        compiler_params=pltpu.CompilerParams(dimension_semantics=("arbitrary",)),
        name="pack_w_in",
    )(w)


def _mixer_in_body(x_ref, pos_ref, invf_ref, mixg_ref, win_ref, gateb_ref, qg_ref, wuq_ref,
                   kvg_ref, wuk_ref, wuv_ref, poolw_ref, pools_ref,
                   ypool_ref, xq_ref, gates_ref, q_ref, k_ref, v_ref, ext_ref,
                   *, tm, tiles_per_seq, q_scale):
    si = lax.rem(pl.program_id(0), tiles_per_seq)

    @pl.when(pl.program_id(0) == 0)
    def _():
        ext_ref[0:POOL_HALO, :] = jnp.zeros((POOL_HALO, POOL_DIM), F32)

    hb = _rms(x_ref[...], mixg_ref[...]).astype(BF16)

    u = _dot(hb, win_ref[:, IN_POOL:IN_QD])
    ext_ref[0:POOL_HALO, :] = jnp.where(si == 0, 0.0, ext_ref[0:POOL_HALO, :])
    ext_ref[POOL_HALO:POOL_HALO + tm, :] = u
    t_seq = lax.broadcasted_iota(jnp.int32, (tm, 1), 0) + si * tm
    for g, w in enumerate(POOL_WINDOWS):
        lo = g * POOL_GROUP_DIM
        hi = lo + POOL_GROUP_DIM
        acc = u[:, lo:hi]
        for j in range(1, w):
            acc = acc + ext_ref[POOL_HALO - j:POOL_HALO - j + tm, lo:hi]
        cnt = jnp.minimum(t_seq + 1, w).astype(F32)
        p = acc / cnt - u[:, lo:hi]
        y = _dot(p.astype(BF16), poolw_ref[g]) * pools_ref[:, lo:hi]
        ypool_ref[:, lo:hi] = y.astype(BF16)
    ext_ref[0:POOL_HALO, :] = ext_ref[tm:tm + POOL_HALO, :]

    ang = pos_ref[...].astype(F32) * invf_ref[...]
    cos = jnp.cos(ang)
    sin = jnp.sin(ang)
    first_half = lax.broadcasted_iota(jnp.int32, (tm, LANES), 1) < (QK_ROPE_DIM // 2)
    sin_signed = jnp.where(first_half, -sin, sin)

    def rope(r):
        swapped = jnp.where(first_half, pltpu.roll(r, LANES - QK_ROPE_DIM // 2, 1),
                            pltpu.roll(r, QK_ROPE_DIM // 2, 1))
        return r * cos + swapped * sin_signed

    cq = _rms(_dot(hb, win_ref[:, IN_QD:IN_KV]), qg_ref[...]).astype(BF16)
    for h in range(MLA_HEADS):
        qh = _dot(cq, wuq_ref[:, h * QK_PAD_DIM:(h + 1) * QK_PAD_DIM])
        q_ref[0, h, :, 0:LANES] = (qh[:, 0:LANES] * q_scale).astype(BF16)
        q_ref[0, h, :, LANES:QK_PAD_DIM] = (rope(qh[:, LANES:QK_PAD_DIM]) * q_scale).astype(BF16)

    ckv = _rms(_dot(hb, win_ref[:, IN_KV:IN_XQ]), kvg_ref[...]).astype(BF16)
    kr = rope(_dot(hb, win_ref[:, IN_KR:IN_END])).astype(BF16)
    for hp in range(MLA_HEADS // 2):
        cols = slice(hp * 2 * LANES, (hp + 1) * 2 * LANES)
        kn = _dot(ckv, wuk_ref[:, cols]).astype(BF16)
        vv = _dot(ckv, wuv_ref[:, cols]).astype(BF16)
        for j in range(2):
            h = 2 * hp + j
            k_ref[0, h, :, 0:LANES] = kn[:, j * LANES:(j + 1) * LANES]
            k_ref[0, h, :, LANES:QK_PAD_DIM] = kr
            v_ref[0, h] = vv[:, j * LANES:(j + 1) * LANES]

    xq_ref[...] = _dot(hb, win_ref[:, IN_XQ:IN_GATE]).astype(BF16)
    for c in range(N_BRANCHES):
        gl = _dot(hb, win_ref[:, IN_GATE + c * D_MODEL:IN_GATE + (c + 1) * D_MODEL])
        gates_ref[:, c * D_MODEL:(c + 1) * D_MODEL] = jax.nn.sigmoid(gl + gateb_ref[c:c + 1, :]).astype(BF16)


def _mixer_in(x2, pos2, invf, mixg, win_p, gate_b, qg, wuq_p, kvg, wuk, wuv, pool_w, pool_s, *, B, S, tm):
    T = B * S
    tps = S // tm
    q_scale = (QK_NOPE_DIM + QK_ROPE_DIM) ** -0.5 * math.log2(math.e)
    body = functools.partial(_mixer_in_body, tm=tm, tiles_per_seq=tps, q_scale=q_scale)
    row = lambda i: (i, 0)
    head = lambda i: (i // tps, 0, i % tps, 0)
    return pl.pallas_call(
        body,
        grid=(T // tm,),
        in_specs=[
            pl.BlockSpec((tm, D_MODEL), row),
            pl.BlockSpec((tm, 1), row),
            _const_spec((1, LANES)),
            _const_spec((1, D_MODEL)),
            _const_spec((D_MODEL, IN_END)),
            _const_spec((N_BRANCHES, D_MODEL)),
            _const_spec((1, Q_LORA_RANK)),
            _const_spec((Q_LORA_RANK, MLA_HEADS * QK_PAD_DIM)),
            _const_spec((1, KV_LORA_RANK)),
            _const_spec((KV_LORA_RANK, MLA_HEADS * QK_NOPE_DIM)),
            _const_spec((KV_LORA_RANK, MLA_HEADS * V_HEAD_DIM)),
            _const_spec((len(POOL_WINDOWS), POOL_GROUP_DIM, POOL_GROUP_DIM)),
            _const_spec((1, POOL_DIM)),
        ],
        out_specs=[
            pl.BlockSpec((tm, POOL_DIM), row),
            pl.BlockSpec((tm, XATTN_DIM), row),
            pl.BlockSpec((tm, N_BRANCHES * D_MODEL), row),
            pl.BlockSpec((1, MLA_HEADS, tm, QK_PAD_DIM), head),
            pl.BlockSpec((1, MLA_HEADS, tm, QK_PAD_DIM), head),
            pl.BlockSpec((1, MLA_HEADS, tm, V_HEAD_DIM), head),
        ],
        out_shape=[
            jax.ShapeDtypeStruct((T, POOL_DIM), BF16),
            jax.ShapeDtypeStruct((T, XATTN_DIM), BF16),
            jax.ShapeDtypeStruct((T, N_BRANCHES * D_MODEL), BF16),
            jax.ShapeDtypeStruct((B, MLA_HEADS, S, QK_PAD_DIM), BF16),
            jax.ShapeDtypeStruct((B, MLA_HEADS, S, QK_PAD_DIM), BF16),
            jax.ShapeDtypeStruct((B, MLA_HEADS, S, V_HEAD_DIM), BF16),
        ],
        scratch_shapes=[pltpu.VMEM((tm + POOL_HALO, POOL_DIM), F32)],
        compiler_params=pltpu.CompilerParams(dimension_semantics=("arbitrary",),
                                             vmem_limit_bytes=VMEM_LIMIT_BYTES),
        name="mixer_in",
    )(x2, pos2, invf, mixg, win_p, gate_b, qg, wuq_p, kvg, wuk, wuv, pool_w, pool_s)


def _mem_kv_body(mem_ref, g_ref, w_ref, k_ref, v_ref):
    mb = _rms(mem_ref[...], g_ref[...]).astype(BF16)
    kv = _dot(mb, w_ref[...])
    k_ref[...] = kv[:, 0:XATTN_DIM].astype(BF16)
    v_ref[...] = kv[:, XATTN_DIM:2 * XATTN_DIM].astype(BF16)


def _mem_kv(mem2, g, w):
    rows = mem2.shape[0]
    tr = min(rows, 512)
    return pl.pallas_call(
        _mem_kv_body,
        grid=(rows // tr,),
        in_specs=[pl.BlockSpec((tr, D_MODEL), lambda i: (i, 0)),
                  _const_spec((1, D_MODEL)),
                  _const_spec((D_MODEL, 2 * XATTN_DIM))],
        out_specs=[pl.BlockSpec((tr, XATTN_DIM), lambda i: (i, 0)),
                   pl.BlockSpec((tr, XATTN_DIM), lambda i: (i, 0))],
        out_shape=[jax.ShapeDtypeStruct((rows, XATTN_DIM), BF16),
                   jax.ShapeDtypeStruct((rows, XATTN_DIM), BF16)],
        compiler_params=pltpu.CompilerParams(dimension_semantics=("arbitrary",)),
        name="mem_kv",
    )(mem2, g, w)


def _attn_unrolled_body(q_ref, k_ref, v_ref, o_ref, s_a, s_b, mc_a, mc_b, m_ref, l_ref, acc_ref, *, nq, tq):
    s_bufs = (s_a, s_b)
    mc_bufs = (mc_a, mc_b)
    mxu_row_split = 2
    blocks = [(qi, kb) for qi in range(nq) for kb in range(qi + 1)]

    def scores(i, slot):
        qi, kb = blocks[i]
        s = _dot_nt(q_ref[0, 0, qi * tq:(qi + 1) * tq, :], k_ref[0, 0, kb * tq:(kb + 1) * tq, :])
        if qi == kb:
            ri = lax.broadcasted_iota(jnp.int32, (tq, tq), 0)
            ci = lax.broadcasted_iota(jnp.int32, (tq, tq), 1)
            s = jnp.where(ci <= ri, s, NEG_INF)
        s_bufs[slot][...] = s
        mc_bufs[slot][...] = jnp.broadcast_to(jnp.max(s, axis=1, keepdims=True), (tq, LANES))

    def accumulate(i, slot):
        qi, kb = blocks[i]
        is_first = kb == 0
        is_last = kb == qi
        if is_first:
            m_new = mc_bufs[slot][...]
        else:
            m_prev = m_ref[...]
            m_new = jnp.maximum(m_prev, mc_bufs[slot][...])
            alpha = jnp.exp2(m_prev - m_new)
        p = jnp.exp2(s_bufs[slot][...] - jnp.concatenate([m_new] * (tq // LANES), axis=1))
        psum = p[:, 0:LANES]
        for c in range(1, tq // LANES):
            psum = psum + p[:, c * LANES:(c + 1) * LANES]
        l_new = psum if is_first else alpha * l_ref[...] + psum
        pb = p.astype(BF16)
        v = v_ref[0, 0, kb * tq:(kb + 1) * tq, :]
        if is_last:
            inv = 1.0 / jnp.sum(l_new, axis=1, keepdims=True)
        else:
            l_ref[...] = l_new
            m_ref[...] = m_new
        h = tq // mxu_row_split
        for r in range(mxu_row_split):
            rows = slice(r * h, (r + 1) * h)
            acc = _dot(pb[rows, :], v)
            if not is_first:
                acc = alpha[rows, :] * acc_ref[rows, :] + acc
            if is_last:
                o_ref[0, qi * tq + r * h:qi * tq + (r + 1) * h, :] = (acc * inv[rows, :]).astype(BF16)
            else:
                acc_ref[rows, :] = acc

    scores(0, 0)
    for i in range(len(blocks)):
        if i + 1 < len(blocks):
            scores(i + 1, (i + 1) % 2)
        accumulate(i, i % 2)


def _mla_attn_unrolled(q, k, v, *, tq):
    B, H, S, _ = q.shape
    per_head = lambda b, h: (b, h, 0, 0)
    return pl.pallas_call(
        functools.partial(_attn_unrolled_body, nq=S // tq, tq=tq),
        grid=(B, H),
        in_specs=[pl.BlockSpec((1, 1, S, QK_PAD_DIM), per_head),
                  pl.BlockSpec((1, 1, S, QK_PAD_DIM), per_head),
                  pl.BlockSpec((1, 1, S, V_HEAD_DIM), per_head)],
        out_specs=pl.BlockSpec((1, S, V_HEAD_DIM), lambda b, h: (b, 0, h)),
        out_shape=jax.ShapeDtypeStruct((B, S, H * V_HEAD_DIM), BF16),
        scratch_shapes=[pltpu.VMEM((tq, tq), F32), pltpu.VMEM((tq, tq), F32),
                        pltpu.VMEM((tq, LANES), F32), pltpu.VMEM((tq, LANES), F32),
                        pltpu.VMEM((tq, LANES), F32), pltpu.VMEM((tq, LANES), F32),
                        pltpu.VMEM((tq, V_HEAD_DIM), F32)],
        compiler_params=pltpu.CompilerParams(dimension_semantics=("arbitrary", "arbitrary"),
                                             vmem_limit_bytes=VMEM_LIMIT_BYTES),
        name="mla_attn",
    )(q, k, v)


def _merge_body(x_ref, ypool_ref, ymla_ref, xq_ref, gates_ref, kmem_ref, vmem_ref,
                wbp_ref, wbm_ref, wbx_ref, wout_ref, ffng_ref, wr_ref, br_ref,
                x1_ref, h2_ref, wts_ref, loc_ref, cnt_tab_ref, carry_tab_ref, counts_ref, carry_ref, *, tm):
    @pl.when(pl.program_id(0) == 0)
    def _():
        carry_ref[...] = jnp.zeros((N_EXPERTS, LANES), F32)

    xq = xq_ref[...]
    parts = []
    for h in range(XATTN_HEADS):
        cols = slice(h * XATTN_HEAD_DIM, (h + 1) * XATTN_HEAD_DIM)
        s = _dot_nt(xq[:, cols], kmem_ref[:, cols]) * (XATTN_HEAD_DIM ** -0.5)
        e = jnp.exp(s - jnp.max(s, axis=1, keepdims=True))
        p = e / jnp.sum(e, axis=1, keepdims=True)
        parts.append(_dot(p.astype(BF16), vmem_ref[:, cols]))
    ymem = jnp.concatenate(parts, axis=1).astype(BF16)

    gates = gates_ref[...].astype(F32)
    merged = (gates[:, 0:D_MODEL] * _dot(ypool_ref[...], wbp_ref[...])
              + gates[:, D_MODEL:2 * D_MODEL] * _dot(ymla_ref[...], wbm_ref[...])
              + gates[:, 2 * D_MODEL:3 * D_MODEL] * _dot(ymem, wbx_ref[...]))
    x1 = x_ref[...] + _dot(merged.astype(BF16), wout_ref[...])
    x1_ref[...] = x1
    h2 = _rms(x1, ffng_ref[...]).astype(BF16)
    h2_ref[...] = h2

    lt = _dot_nt(wr_ref[...], h2) + br_ref[...]
    gl = lt[N_EXPERTS:N_EXPERTS + N_GROUPS, :]
    gmax = jnp.max(gl, axis=0, keepdims=True)
    r4 = lax.broadcasted_iota(jnp.int32, (N_GROUPS, tm), 0).astype(F32)
    gidx = jnp.min(jnp.where(gl == gmax, r4, float(N_GROUPS)), axis=0, keepdims=True)
    pg = 1.0 / jnp.sum(jnp.exp(gl - gmax), axis=0, keepdims=True)
    esel = lt[0:EXPERTS_PER_GROUP, :]
    for g in range(1, N_GROUPS):
        esel = jnp.where(gidx == float(g), lt[g * EXPERTS_PER_GROUP:(g + 1) * EXPERTS_PER_GROUP, :], esel)
    r8 = lax.broadcasted_iota(jnp.int32, (EXPERTS_PER_GROUP, tm), 0).astype(F32)
    m1 = jnp.max(esel, axis=0, keepdims=True)
    i1 = jnp.min(jnp.where(esel == m1, r8, float(EXPERTS_PER_GROUP)), axis=0, keepdims=True)
    rest = jnp.where(r8 == i1, -jnp.inf, esel)
    m2 = jnp.max(rest, axis=0, keepdims=True)
    i2 = jnp.min(jnp.where(rest == m2, r8, float(EXPERTS_PER_GROUP)), axis=0, keepdims=True)
    e2 = jnp.exp(m2 - m1)
    den = 1.0 + e2
    wts_ref[0:1, :] = pg / den
    wts_ref[1:2, :] = pg * e2 / den
    ex1 = gidx * float(EXPERTS_PER_GROUP) + i1
    ex2 = gidx * float(EXPERTS_PER_GROUP) + i2

    r32 = lax.broadcasted_iota(jnp.int32, (N_EXPERTS, tm), 0).astype(F32)
    is1 = r32 == ex1
    is2 = r32 == ex2
    member = jnp.where(is1 | is2, 1.0, 0.0)
    upper = jnp.where(lax.broadcasted_iota(jnp.int32, (tm, tm), 0)
                      <= lax.broadcasted_iota(jnp.int32, (tm, tm), 1), 1.0, 0.0).astype(BF16)
    incl = _dot(member.astype(BF16), upper)
    run = jnp.floor((jnp.sum(member, axis=1, keepdims=True) + (RUN_ALIGN - 1)) / RUN_ALIGN) * RUN_ALIGN
    rcol = lax.broadcasted_iota(jnp.int32, (N_EXPERTS, 1), 0)
    run_start = jnp.zeros((N_EXPERTS, 1), F32)
    for e in range(N_EXPERTS - 1):
        run_start = run_start + jnp.where(rcol > e, run[e:e + 1, :], 0.0)
    pos = incl - 1.0 + run_start
    loc_ref[0:1, :] = jnp.sum(jnp.where(is1, pos, 0.0), axis=0, keepdims=True).astype(jnp.int32)
    loc_ref[1:2, :] = jnp.sum(jnp.where(is2, pos, 0.0), axis=0, keepdims=True).astype(jnp.int32)
    carry = carry_ref[...]
    total = carry + run
    cnt_tab_ref[...] = jnp.broadcast_to(run, (N_EXPERTS, LANES)).astype(jnp.int32)
    carry_tab_ref[...] = carry.astype(jnp.int32)
    carry_ref[...] = total
    counts_ref[...] = total.astype(jnp.int32)


def _merge(x2, ypool, ymla, xq, gates, kmem, vmem, wbp, wbm, wbx, wout, ffng, wr, br, *, B, S, tm, mem_len):
    T = B * S
    tps = S // tm
    row = lambda i: (i, 0)
    lane = lambda i: (0, i)
    memb = lambda i: (i // tps, 0)
    return pl.pallas_call(
        functools.partial(_merge_body, tm=tm),
        grid=(T // tm,),
        in_specs=[
            pl.BlockSpec((tm, D_MODEL), row),
            pl.BlockSpec((tm, POOL_DIM), row),
            pl.BlockSpec((tm, MLA_HEADS * V_HEAD_DIM), row),
            pl.BlockSpec((tm, XATTN_DIM), row),
            pl.BlockSpec((tm, N_BRANCHES * D_MODEL), row),
            pl.BlockSpec((mem_len, XATTN_DIM), memb),
            pl.BlockSpec((mem_len, XATTN_DIM), memb),
            _const_spec((POOL_DIM, D_MODEL)),
            _const_spec((MLA_HEADS * V_HEAD_DIM, D_MODEL)),
            _const_spec((XATTN_DIM, D_MODEL)),
            _const_spec((D_MODEL, D_MODEL)),
            _const_spec((1, D_MODEL)),
            _const_spec((ROUTER_ROWS, D_MODEL)),
            _const_spec((ROUTER_ROWS, 1)),
        ],
        out_specs=[
            pl.BlockSpec((tm, D_MODEL), row),
            pl.BlockSpec((tm, D_MODEL), row),
            pl.BlockSpec((2, tm), lane),
            pl.BlockSpec((2, tm), lane),
            pl.BlockSpec((N_EXPERTS, LANES), row),
            pl.BlockSpec((N_EXPERTS, LANES), row),
            pl.BlockSpec((N_EXPERTS, LANES), lambda i: (0, 0)),
        ],
        out_shape=[
            jax.ShapeDtypeStruct((T, D_MODEL), F32),
            jax.ShapeDtypeStruct((T, D_MODEL), BF16),
            jax.ShapeDtypeStruct((2, T), F32),
            jax.ShapeDtypeStruct((2, T), jnp.int32),
            jax.ShapeDtypeStruct((T // tm * N_EXPERTS, LANES), jnp.int32),
            jax.ShapeDtypeStruct((T // tm * N_EXPERTS, LANES), jnp.int32),
            jax.ShapeDtypeStruct((N_EXPERTS, LANES), jnp.int32),
        ],
        scratch_shapes=[pltpu.VMEM((N_EXPERTS, LANES), F32)],
        compiler_params=pltpu.CompilerParams(dimension_semantics=("arbitrary",),
                                             vmem_limit_bytes=VMEM_LIMIT_BYTES),
        name="merge",
    )(x2, ypool, ymla, xq, gates, kmem, vmem, wbp, wbm, wbx, wout, ffng, wr, br)


def _moe_pos_body(counts_ref, meta_ref):
    shift = int(math.log2(MOE_ROWS))
    cnt = counts_ref[...]
    padded = lax.shift_left(lax.shift_right_logical(cnt + (MOE_ROWS - 1), shift), shift)
    r32 = lax.broadcasted_iota(jnp.int32, (N_EXPERTS, LANES), 0)
    pad_start = jnp.zeros((N_EXPERTS, LANES), jnp.int32)
    for e in range(N_EXPERTS - 1):
        pad_start = pad_start + jnp.where(r32 > e, padded[e:e + 1, :], 0)
    pad_end = pad_start + padded

    lane = lax.broadcasted_iota(jnp.int32, (1, META_LANES), 1)
    block_row = lane * MOE_ROWS
    blk_e = jnp.zeros((1, META_LANES), jnp.int32)
    pe_row = jnp.zeros((1, META_LANES), jnp.int32)
    for e in range(N_EXPERTS):
        pe = pad_end[e:e + 1, 0:1]
        blk_e = blk_e + jnp.where(pe <= block_row, 1, 0)
        pe_row = pe_row + jnp.where(lane == META_PAD_END + e, pe, 0)
    blk_e = jnp.minimum(blk_e, N_EXPERTS - 1)
    nact = lax.shift_right_logical(pad_end[N_EXPERTS - 1:N_EXPERTS, 0:1], shift)
    meta = jnp.where(lane < META_PAD_END, blk_e, pe_row)
    meta_ref[...] = jnp.where(lane == META_NACT, nact, meta)


def _moe_pos(counts):
    full = lambda shape: pl.BlockSpec(shape, lambda i: (0,) * len(shape))
    return pl.pallas_call(
        _moe_pos_body,
        grid=(1,),
        in_specs=[full((N_EXPERTS, LANES))],
        out_specs=full((1, META_LANES)),
        out_shape=jax.ShapeDtypeStruct((1, META_LANES), jnp.int32),
        compiler_params=pltpu.CompilerParams(dimension_semantics=("arbitrary",)),
        name="moe_pos",
    )(counts)


def _pack_bf16_pairs(x):
    lo = pltpu.bitcast(x[:, 0:PACKED_DIM], U32)
    hi = pltpu.bitcast(x[:, PACKED_DIM:D_MODEL], U32)
    return hi | lax.shift_right_logical(lo, jnp.uint32(16))


def _unpack_bf16_pairs(w):
    lo = pltpu.bitcast(lax.shift_left(w, jnp.uint32(16)), F32)
    hi = pltpu.bitcast(w & jnp.uint32(0xFFFF0000), F32)
    return jnp.concatenate([lo, hi], axis=1)


def _loc_rows(tm):
    return 2 * tm + N_EXPERTS * RUN_ALIGN


def _run_copies(tile, cnt_ref, carry_ref, meta_ref, make_copy):
    def per_expert(e, local):
        n = pl.multiple_of(cnt_ref[tile * N_EXPERTS + e], RUN_ALIGN)
        start = jnp.where(e == 0, 0, meta_ref[META_PAD_END + jnp.maximum(e - 1, 0)])
        glob = pl.multiple_of(start + carry_ref[tile * N_EXPERTS + e], RUN_ALIGN)

        @pl.when(n > 0)
        def _():
            make_copy(pl.multiple_of(local, RUN_ALIGN), glob, n).start()

        return local + n

    return pl.multiple_of(lax.fori_loop(0, N_EXPERTS, per_expert, 0), RUN_ALIGN)


def _tile_rows(tile, cnt_ref):
    total = lax.fori_loop(0, N_EXPERTS, lambda e, t: t + cnt_ref[tile * N_EXPERTS + e], 0)
    return pl.multiple_of(total, RUN_ALIGN)


def _dispatch_body(meta_ref, cnt_ref, carry_ref, loc_ref, h2_ref, xs_ref, xloc_ref, zero_ref, sems, zsem,
                   *, tm, n_blocks):
    tile = pl.program_id(0)
    slot = lax.rem(tile, 2)

    def wait_rows(t, s):
        n = _tile_rows(t, cnt_ref)
        pltpu.make_async_copy(xloc_ref.at[s, pl.ds(0, n)], xs_ref.at[pl.ds(0, n)], sems.at[s]).wait()

    def pad_copy(e):
        end = pl.multiple_of(meta_ref[META_PAD_END + e], MOE_ROWS)
        return pltpu.make_async_copy(zero_ref, xs_ref.at[pl.ds(end - MOE_ROWS, MOE_ROWS)], zsem)

    def has_rows(e):
        prev = jnp.where(e == 0, 0, meta_ref[META_PAD_END + jnp.maximum(e - 1, 0)])
        return meta_ref[META_PAD_END + e] > prev

    @pl.when(tile == 0)
    def _():
        zero_ref[...] = jnp.zeros((MOE_ROWS, PACKED_DIM), U32)

        def start(e, c):
            @pl.when(has_rows(e))
            def _():
                pad_copy(e).start()
            return c

        def wait(e, c):
            @pl.when(has_rows(e))
            def _():
                pad_copy(e).wait()
            return c

        def tail_copy(b):
            return pltpu.make_async_copy(
                zero_ref, xs_ref.at[pl.ds(pl.multiple_of(b * MOE_ROWS, MOE_ROWS), MOE_ROWS)], zsem)

        def tail_start(b, c):
            tail_copy(b).start()
            return c

        def tail_wait(b, c):
            tail_copy(b).wait()
            return c

        nact = meta_ref[META_NACT]
        lax.fori_loop(0, N_EXPERTS, start, 0)
        lax.fori_loop(nact, n_blocks, tail_start, 0)
        lax.fori_loop(0, N_EXPERTS, wait, 0)
        lax.fori_loop(nact, n_blocks, tail_wait, 0)

    @pl.when(tile >= 2)
    def _():
        wait_rows(tile - 2, slot)

    r = lax.broadcasted_iota(jnp.int32, (_loc_rows(tm), tm), 0)
    onehot = jnp.where((r == loc_ref[0:1, :]) | (r == loc_ref[1:2, :]), 1.0, 0.0).astype(BF16)
    xloc_ref[slot] = _pack_bf16_pairs(_dot(onehot, h2_ref[...]))

    def make_copy(local, glob, n):
        return pltpu.make_async_copy(xloc_ref.at[slot, pl.ds(local, n)], xs_ref.at[pl.ds(glob, n)],
                                     sems.at[slot])

    _run_copies(tile, cnt_ref, carry_ref, meta_ref, make_copy)

    @pl.when(tile == pl.num_programs(0) - 1)
    def _():
        @pl.when(tile >= 1)
        def _():
            wait_rows(tile - 1, 1 - slot)

        wait_rows(tile, slot)


def _dispatch(meta1, cnt_tab, carry_tab, loc, h2, *, R, tm):
    T = h2.shape[0]
    return pl.pallas_call(
        functools.partial(_dispatch_body, tm=tm, n_blocks=R // MOE_ROWS),
        grid_spec=pltpu.PrefetchScalarGridSpec(
            num_scalar_prefetch=3,
            grid=(T // tm,),
            in_specs=[pl.BlockSpec((2, tm), lambda i, *_: (0, i)),
                      pl.BlockSpec((tm, D_MODEL), lambda i, *_: (i, 0))],
            out_specs=pl.BlockSpec(memory_space=pl.ANY),
            scratch_shapes=[pltpu.VMEM((2, _loc_rows(tm), PACKED_DIM), U32),
                            pltpu.VMEM((MOE_ROWS, PACKED_DIM), U32),
                            pltpu.SemaphoreType.DMA((2,)), pltpu.SemaphoreType.DMA],
        ),
        out_shape=jax.ShapeDtypeStruct((R, PACKED_DIM), U32),
        compiler_params=pltpu.CompilerParams(dimension_semantics=("arbitrary",),
                                             vmem_limit_bytes=VMEM_LIMIT_BYTES),
        name="dispatch",
    )(meta1, cnt_tab, carry_tab, loc, h2)


def _moe_ffn_body(meta_ref, xs_ref, wg_hbm, wu_hbm, wd_hbm, ys_ref,
                  xbuf, ybuf, wg_stage, wu_stage, wd_stage, wg_b, wu_b, wd_b, zero_ref,
                  xsem, ysem, wsem, zsem, *, n_blocks):
    nact = meta_ref[META_NACT]
    shift = int(math.log2(MOE_ROWS))

    def rows_of(b):
        return pl.ds(pl.multiple_of(b * MOE_ROWS, MOE_ROWS), MOE_ROWS)

    def x_copy(b, s):
        return pltpu.make_async_copy(xs_ref.at[rows_of(b)], xbuf.at[s], xsem.at[s])

    def y_copy(b, s):
        return pltpu.make_async_copy(ybuf.at[s], ys_ref.at[rows_of(b)], ysem.at[s])

    def w_copies(e, s):
        return (pltpu.make_async_copy(wg_hbm.at[e], wg_stage.at[s], wsem.at[s]),
                pltpu.make_async_copy(wu_hbm.at[e], wu_stage.at[s], wsem.at[s]),
                pltpu.make_async_copy(wd_hbm.at[e], wd_stage.at[s], wsem.at[s]))

    def tail_copy(b):
        return pltpu.make_async_copy(zero_ref, ys_ref.at[rows_of(b)], zsem)

    zero_ref[...] = jnp.zeros((MOE_ROWS, PACKED_DIM), U32)
    lax.fori_loop(nact, n_blocks, lambda b, c: (tail_copy(b).start(), c)[1], 0)

    x_copy(0, 0).start()
    for cp in w_copies(meta_ref[0], 0):
        cp.start()

    def block(b, ws_prev):
        s = lax.rem(b, 2)
        e = meta_ref[b]
        changed = jnp.logical_or(b == 0, e != meta_ref[jnp.maximum(b - 1, 0)])
        ws = jnp.where(changed, 1 - ws_prev, ws_prev)

        @pl.when(changed)
        def _():
            for cp in w_copies(e, ws):
                cp.wait()
            wg_b[...] = wg_stage[ws].astype(BF16)
            wu_b[...] = wu_stage[ws].astype(BF16)
            wd_b[...] = wd_stage[ws].astype(BF16)
            nxt = lax.shift_right_logical(meta_ref[META_PAD_END + e], shift)

            @pl.when(nxt < nact)
            def _():
                for cp in w_copies(meta_ref[nxt], 1 - ws):
                    cp.start()

        x_copy(b, s).wait()

        @pl.when(b + 1 < nact)
        def _():
            x_copy(b + 1, 1 - s).start()

        @pl.when(b >= 2)
        def _():
            y_copy(b - 2, s).wait()

        for c in range(MOE_ROWS // MOE_CHUNK):
            rows = slice(c * MOE_CHUNK, (c + 1) * MOE_CHUNK)
            x = _unpack_bf16_pairs(xbuf[s, rows, :]).astype(BF16)
            g = _dot(x, wg_b[...])
            a = (g * jax.nn.sigmoid(g)) * _dot(x, wu_b[...])
            y = _dot(a.astype(BF16), wd_b[...])
            ybuf[s, rows, :] = _pack_bf16_pairs(y.astype(BF16).astype(F32))
        y_copy(b, s).start()
        return ws

    lax.fori_loop(0, nact, block, 1)

    @pl.when(nact >= 2)
    def _():
        y_copy(nact - 2, lax.rem(nact, 2)).wait()

    y_copy(nact - 1, lax.rem(nact - 1, 2)).wait()
    lax.fori_loop(nact, n_blocks, lambda b, c: (tail_copy(b).wait(), c)[1], 0)


def _moe_ffn(meta1, xs, wg, wu, wd):
    R = xs.shape[0]
    hbm = pl.BlockSpec(memory_space=pl.ANY)
    return pl.pallas_call(
        functools.partial(_moe_ffn_body, n_blocks=R // MOE_ROWS),
        grid_spec=pltpu.PrefetchScalarGridSpec(
            num_scalar_prefetch=1,
            grid=(1,),
            in_specs=[hbm, hbm, hbm, hbm],
            out_specs=hbm,
            scratch_shapes=[pltpu.VMEM((2, MOE_ROWS, PACKED_DIM), U32),
                            pltpu.VMEM((2, MOE_ROWS, PACKED_DIM), U32),
                            pltpu.VMEM((2, D_MODEL, D_EXPERT), F32),
                            pltpu.VMEM((2, D_MODEL, D_EXPERT), F32),
                            pltpu.VMEM((2, D_EXPERT, D_MODEL), F32),
                            pltpu.VMEM((D_MODEL, D_EXPERT), BF16),
                            pltpu.VMEM((D_MODEL, D_EXPERT), BF16),
                            pltpu.VMEM((D_EXPERT, D_MODEL), BF16),
                            pltpu.VMEM((MOE_ROWS, PACKED_DIM), U32),
                            pltpu.SemaphoreType.DMA((2,)), pltpu.SemaphoreType.DMA((2,)),
                            pltpu.SemaphoreType.DMA((2,)), pltpu.SemaphoreType.DMA],
        ),
        out_shape=jax.ShapeDtypeStruct((R, PACKED_DIM), U32),
        compiler_params=pltpu.CompilerParams(dimension_semantics=("arbitrary",),
                                             vmem_limit_bytes=VMEM_LIMIT_BYTES),
        name="moe_ffn",
    )(meta1, xs, wg, wu, wd)


def _combine_body(meta_ref, cnt_ref, carry_ref, x1_ref, loc_ref, wts_ref, fg_ref, ys_ref, out_ref,
                  yloc_ref, sems, *, tm):
    tile = pl.program_id(0)
    slot = lax.rem(tile, 2)

    def fetch(t, s):
        def make_copy(local, glob, n):
            return pltpu.make_async_copy(ys_ref.at[pl.ds(glob, n)], yloc_ref.at[s, pl.ds(local, n)],
                                         sems.at[s])
        _run_copies(t, cnt_ref, carry_ref, meta_ref, make_copy)

    @pl.when(tile == 0)
    def _():
        yloc_ref[...] = jnp.zeros(yloc_ref.shape, U32)
        fetch(tile, slot)

    @pl.when(tile + 1 < pl.num_programs(0))
    def _():
        fetch(tile + 1, 1 - slot)

    n = _tile_rows(tile, cnt_ref)
    pltpu.make_async_copy(ys_ref.at[pl.ds(0, n)], yloc_ref.at[slot, pl.ds(0, n)], sems.at[slot]).wait()
    r = lax.broadcasted_iota(jnp.int32, (_loc_rows(tm), tm), 0)
    is0 = r == loc_ref[0:1, :]
    is1 = r == loc_ref[1:2, :]
    row_w = jnp.sum(jnp.where(is0, wts_ref[0:1, :], 0.0) + jnp.where(is1, wts_ref[1:2, :], 0.0),
                    axis=1, keepdims=True)
    yw = (row_w * _unpack_bf16_pairs(yloc_ref[slot])).astype(BF16)
    twohot = jnp.where(is0 | is1, 1.0, 0.0).astype(BF16)
    moe = lax.dot_general(twohot, yw, (((0,), (0,)), ((), ())), preferred_element_type=F32)
    out_ref[...] = _rms(x1_ref[...] + moe, fg_ref[...])


def _combine(meta1, cnt_tab, carry_tab, x1, loc, wts, fg, ys, *, tm):
    T = x1.shape[0]
    row = lambda i, *_: (i, 0)
    lane = lambda i, *_: (0, i)
    return pl.pallas_call(
        functools.partial(_combine_body, tm=tm),
        grid_spec=pltpu.PrefetchScalarGridSpec(
            num_scalar_prefetch=3,
            grid=(T // tm,),
            in_specs=[pl.BlockSpec((tm, D_MODEL), row),
                      pl.BlockSpec((2, tm), lane),
                      pl.BlockSpec((2, tm), lane),
                      pl.BlockSpec((1, D_MODEL), lambda i, *_: (0, 0)),
                      pl.BlockSpec(memory_space=pl.ANY)],
            out_specs=pl.BlockSpec((tm, D_MODEL), row),
            scratch_shapes=[pltpu.VMEM((2, _loc_rows(tm), PACKED_DIM), U32),
                            pltpu.SemaphoreType.DMA((2,))],
        ),
        out_shape=jax.ShapeDtypeStruct((T, D_MODEL), F32),
        compiler_params=pltpu.CompilerParams(dimension_semantics=("arbitrary",),
                                             vmem_limit_bytes=VMEM_LIMIT_BYTES),
        name="combine",
    )(meta1, cnt_tab, carry_tab, x1, loc, wts, fg, ys)


def _tile(n, t):
    t = min(n, t)
    assert n % t == 0, (n, t)
    return t


def kernel(x, mem, positions, mix_norm_g, w_in, gate_b, q_norm_g, w_uq, kv_norm_g, w_uk, w_uv, pool_w, pool_scale, mem_norm_g, w_mem_kv, w_br_pool, w_br_mla, w_br_mem, w_out, ffn_norm_g, w_router_group, b_router_group, w_router_expert, b_router_expert, w_gate_e, w_up_e, w_down_e, final_norm_g):
    B, S, D = x.shape
    assert D == D_MODEL and mix_norm_g.shape[0] == 1
    T = B * S
    mem_len = mem.shape[1]
    tm = _tile(S, 512)
    l = 0

    win_p = _pack_w_in(w_in[l])
    wuq_p = jnp.pad(w_uq[l].reshape(Q_LORA_RANK, MLA_HEADS, QK_NOPE_DIM + QK_ROPE_DIM),
                    ((0, 0), (0, 0), (0, QK_PAD_DIM - QK_NOPE_DIM - QK_ROPE_DIM))
                    ).reshape(Q_LORA_RANK, MLA_HEADS * QK_PAD_DIM).astype(BF16)
    inv_freq = 1.0 / (ROPE_THETA ** (jnp.arange(0, QK_ROPE_DIM, 2, dtype=F32) / QK_ROPE_DIM))
    invf = jnp.concatenate([inv_freq, inv_freq, jnp.zeros((LANES - QK_ROPE_DIM,), F32)])[None, :]
    wr = jnp.concatenate([w_router_expert[l], w_router_group[l],
                          jnp.zeros((D_MODEL, ROUTER_ROWS - N_EXPERTS - N_GROUPS), F32)], axis=1).T.astype(BF16)
    br = jnp.concatenate([b_router_expert[l], b_router_group[l],
                          jnp.zeros((ROUTER_ROWS - N_EXPERTS - N_GROUPS,), F32)])[:, None].astype(F32)
    x2 = x.reshape(T, D_MODEL)
    pos2 = positions.reshape(T, 1)

    ypool, xq, gates, q, k, v = _mixer_in(
        x2, pos2, invf, mix_norm_g[l][None, :], win_p, gate_b[l], q_norm_g[l][None, :], wuq_p,
        kv_norm_g[l][None, :], w_uk[l].astype(BF16), w_uv[l].astype(BF16), pool_w[l].astype(BF16),
        pool_scale[l][None, :], B=B, S=S, tm=tm)
    kmem, vmem = _mem_kv(mem.reshape(B * mem_len, D_MODEL), mem_norm_g[l][None, :], w_mem_kv[l].astype(BF16))
    ymla = _mla_attn_unrolled(q, k, v, tq=tm).reshape(T, MLA_HEADS * V_HEAD_DIM)
    x1, h2, wts, loc, cnt_tab, carry_tab, counts = _merge(
        x2, ypool, ymla, xq, gates, kmem, vmem, w_br_pool[l].astype(BF16), w_br_mla[l].astype(BF16),
        w_br_mem[l].astype(BF16), w_out[l].astype(BF16), ffn_norm_g[l][None, :], wr, br,
        B=B, S=S, tm=tm, mem_len=mem_len)

    R = 2 * T + (T // tm) * N_EXPERTS * RUN_ALIGN + N_EXPERTS * MOE_ROWS
    assert R % MOE_ROWS == 0 and R // MOE_ROWS <= META_PAD_END
    meta1 = _moe_pos(counts).reshape(META_LANES)
    cnt1 = cnt_tab[:, 0]
    carry1 = carry_tab[:, 0]
    xs = _dispatch(meta1, cnt1, carry1, loc, h2, R=R, tm=tm)
    ys = _moe_ffn(meta1, xs, w_gate_e[l], w_up_e[l], w_down_e[l])
    out = _combine(meta1, cnt1, carry1, x1, loc, wts, final_norm_g[None, :], ys, tm=tm)
    return out.reshape(B, S, D_MODEL)
```

```python
import functools
import math

import jax
import jax.numpy as jnp
from jax import lax
from jax.experimental import pallas as pl
from jax.experimental.pallas import tpu as pltpu

D_MODEL = 1024
POOL_WINDOWS = (2, 4, 8, 16)
POOL_GROUP_DIM = 128
POOL_DIM = 512
MLA_HEADS = 8
QK_NOPE_DIM = 128
QK_ROPE_DIM = 64
V_HEAD_DIM = 128
Q_LORA_RANK = 384
KV_LORA_RANK = 256
ROPE_THETA = 10000.0
XATTN_HEADS = 4
XATTN_HEAD_DIM = 128
XATTN_DIM = 512
N_BRANCHES = 3
N_GROUPS = 4
EXPERTS_PER_GROUP = 8
N_EXPERTS = 32
D_EXPERT = 256
RMS_EPS = 1e-6
NEG_INF = -1e30

LANES = 128
QK_PAD_DIM = 2 * LANES
POOL_HALO = 16
MOE_ROWS = 512
MOE_CHUNK = 256
RUN_ALIGN = 8
PACKED_DIM = D_MODEL // 2
ROUTER_ROWS = 40
META_LANES = 256
META_PAD_END = 192
META_NACT = 255
VMEM_LIMIT_BYTES = 56 * 1024 * 1024

IN_POOL, IN_QD, IN_KV, IN_XQ, IN_GATE, IN_KR, IN_END = 0, 512, 896, 1152, 1664, 4736, 4864
W_IN_KR, W_IN_XQ, W_IN_END = 1152, 1216, 4800

F32 = jnp.float32
BF16 = jnp.bfloat16
U32 = jnp.uint32


def _rms(x, g):
    ms = jnp.mean(x * x, axis=-1, keepdims=True)
    return (x * lax.rsqrt(ms + RMS_EPS)) * g


def _dot(a, b):
    return jnp.dot(a, b, preferred_element_type=F32)


def _dot_nt(a, b):
    return lax.dot_general(a, b, (((1,), (1,)), ((), ())), preferred_element_type=F32)


def _const_spec(shape):
    nd = len(shape)
    return pl.BlockSpec(shape, lambda *_: (0,) * nd, pipeline_mode=pl.Buffered(1))


def _pack_w_in_body(w_ref, o_ref):
    w = w_ref[...]
    rows = w.shape[0]
    o_ref[...] = jnp.concatenate(
        [w[:, 0:W_IN_KR], w[:, W_IN_XQ:W_IN_END], w[:, W_IN_KR:W_IN_XQ],
         jnp.zeros((rows, LANES - QK_ROPE_DIM), w.dtype)], axis=1).astype(BF16)


def _pack_w_in(w):
    tr = 128
    return pl.pallas_call(
        _pack_w_in_body,
        grid=(D_MODEL // tr,),
        in_specs=[pl.BlockSpec((tr, W_IN_END), lambda i: (i, 0))],
        out_specs=pl.BlockSpec((tr, IN_END), lambda i: (i, 0)),
        out_shape=jax.ShapeDtypeStruct((D_MODEL, IN_END), BF16),
        compiler_params=pltpu.CompilerParams(dimension_semantics=("arbitrary",)),
        name="pack_w_in",
    )(w)


def _mixer_in_body(x_ref, pos_ref, invf_ref, mixg_ref, win_ref, gateb_ref, qg_ref, wuq_ref,
                   kvg_ref, wuk_ref, wuv_ref, poolw_ref, pools_ref,
                   ypool_ref, xq_ref, gates_ref, q_ref, k_ref, v_ref, ext_ref,
                   *, tm, tiles_per_seq, q_scale):
    si = lax.rem(pl.program_id(0), tiles_per_seq)

    @pl.when(pl.program_id(0) == 0)
    def _():
        ext_ref[0:POOL_HALO, :] = jnp.zeros((POOL_HALO, POOL_DIM), F32)

    hb = _rms(x_ref[...], mixg_ref[...]).astype(BF16)

    u = _dot(hb, win_ref[:, IN_POOL:IN_QD])
    ext_ref[0:POOL_HALO, :] = jnp.where(si == 0, 0.0, ext_ref[0:POOL_HALO, :])
    ext_ref[POOL_HALO:POOL_HALO + tm, :] = u
    t_seq = lax.broadcasted_iota(jnp.int32, (tm, 1), 0) + si * tm
    for g, w in enumerate(POOL_WINDOWS):
        lo = g * POOL_GROUP_DIM
        hi = lo + POOL_GROUP_DIM
        acc = u[:, lo:hi]
        for j in range(1, w):
            acc = acc + ext_ref[POOL_HALO - j:POOL_HALO - j + tm, lo:hi]
        cnt = jnp.minimum(t_seq + 1, w).astype(F32)
        p = acc / cnt - u[:, lo:hi]
        y = _dot(p.astype(BF16), poolw_ref[g]) * pools_ref[:, lo:hi]
        ypool_ref[:, lo:hi] = y.astype(BF16)
    ext_ref[0:POOL_HALO, :] = ext_ref[tm:tm + POOL_HALO, :]

    ang = pos_ref[...].astype(F32) * invf_ref[...]
    cos = jnp.cos(ang)
    sin = jnp.sin(ang)
    first_half = lax.broadcasted_iota(jnp.int32, (tm, LANES), 1) < (QK_ROPE_DIM // 2)
    sin_signed = jnp.where(first_half, -sin, sin)

    def rope(r):
        swapped = jnp.where(first_half, pltpu.roll(r, LANES - QK_ROPE_DIM // 2, 1),
                            pltpu.roll(r, QK_ROPE_DIM // 2, 1))
        return r * cos + swapped * sin_signed

    cq = _rms(_dot(hb, win_ref[:, IN_QD:IN_KV]), qg_ref[...]).astype(BF16)
    for h in range(MLA_HEADS):
        qh = _dot(cq, wuq_ref[:, h * QK_PAD_DIM:(h + 1) * QK_PAD_DIM])
        q_ref[0, h, :, 0:LANES] = (qh[:, 0:LANES] * q_scale).astype(BF16)
        q_ref[0, h, :, LANES:QK_PAD_DIM] = (rope(qh[:, LANES:QK_PAD_DIM]) * q_scale).astype(BF16)

    ckv = _rms(_dot(hb, win_ref[:, IN_KV:IN_XQ]), kvg_ref[...]).astype(BF16)
    kr = rope(_dot(hb, win_ref[:, IN_KR:IN_END])).astype(BF16)
    for hp in range(MLA_HEADS // 2):
        cols = slice(hp * 2 * LANES, (hp + 1) * 2 * LANES)
        kn = _dot(ckv, wuk_ref[:, cols]).astype(BF16)
        vv = _dot(ckv, wuv_ref[:, cols]).astype(BF16)
        for j in range(2):
            h = 2 * hp + j
            k_ref[0, h, :, 0:LANES] = kn[:, j * LANES:(j + 1) * LANES]
            k_ref[0, h, :, LANES:QK_PAD_DIM] = kr
            v_ref[0, h] = vv[:, j * LANES:(j + 1) * LANES]

    xq_ref[...] = _dot(hb, win_ref[:, IN_XQ:IN_GATE]).astype(BF16)
    for c in range(N_BRANCHES):
        gl = _dot(hb, win_ref[:, IN_GATE + c * D_MODEL:IN_GATE + (c + 1) * D_MODEL])
        gates_ref[:, c * D_MODEL:(c + 1) * D_MODEL] = jax.nn.sigmoid(gl + gateb_ref[c:c + 1, :]).astype(BF16)


def _mixer_in(x2, pos2, invf, mixg, win_p, gate_b, qg, wuq_p, kvg, wuk, wuv, pool_w, pool_s, *, B, S, tm):
    T = B * S
    tps = S // tm
    q_scale = (QK_NOPE_DIM + QK_ROPE_DIM) ** -0.5 * math.log2(math.e)
    body = functools.partial(_mixer_in_body, tm=tm, tiles_per_seq=tps, q_scale=q_scale)
    row = lambda i: (i, 0)
    head = lambda i: (i // tps, 0, i % tps, 0)
    return pl.pallas_call(
        body,
        grid=(T // tm,),
        in_specs=[
            pl.BlockSpec((tm, D_MODEL), row),
            pl.BlockSpec((tm, 1), row),
            _const_spec((1, LANES)),
            _const_spec((1, D_MODEL)),
            _const_spec((D_MODEL, IN_END)),
            _const_spec((N_BRANCHES, D_MODEL)),
            _const_spec((1, Q_LORA_RANK)),
            _const_spec((Q_LORA_RANK, MLA_HEADS * QK_PAD_DIM)),
            _const_spec((1, KV_LORA_RANK)),
            _const_spec((KV_LORA_RANK, MLA_HEADS * QK_NOPE_DIM)),
            _const_spec((KV_LORA_RANK, MLA_HEADS * V_HEAD_DIM)),
            _const_spec((len(POOL_WINDOWS), POOL_GROUP_DIM, POOL_GROUP_DIM)),
            _const_spec((1, POOL_DIM)),
        ],
        out_specs=[
            pl.BlockSpec((tm, POOL_DIM), row),
            pl.BlockSpec((tm, XATTN_DIM), row),
            pl.BlockSpec((tm, N_BRANCHES * D_MODEL), row),
            pl.BlockSpec((1, MLA_HEADS, tm, QK_PAD_DIM), head),
            pl.BlockSpec((1, MLA_HEADS, tm, QK_PAD_DIM), head),
            pl.BlockSpec((1, MLA_HEADS, tm, V_HEAD_DIM), head),
        ],
        out_shape=[
            jax.ShapeDtypeStruct((T, POOL_DIM), BF16),
            jax.ShapeDtypeStruct((T, XATTN_DIM), BF16),
            jax.ShapeDtypeStruct((T, N_BRANCHES * D_MODEL), BF16),
            jax.ShapeDtypeStruct((B, MLA_HEADS, S, QK_PAD_DIM), BF16),
            jax.ShapeDtypeStruct((B, MLA_HEADS, S, QK_PAD_DIM), BF16),
            jax.ShapeDtypeStruct((B, MLA_HEADS, S, V_HEAD_DIM), BF16),
        ],
        scratch_shapes=[pltpu.VMEM((tm + POOL_HALO, POOL_DIM), F32)],
        compiler_params=pltpu.CompilerParams(dimension_semantics=("arbitrary",),
                                             vmem_limit_bytes=VMEM_LIMIT_BYTES),
        name="mixer_in",
    )(x2, pos2, invf, mixg, win_p, gate_b, qg, wuq_p, kvg, wuk, wuv, pool_w, pool_s)


def _mem_kv_body(mem_ref, g_ref, w_ref, k_ref, v_ref):
    mb = _rms(mem_ref[...], g_ref[...]).astype(BF16)
    kv = _dot(mb, w_ref[...])
    k_ref[...] = kv[:, 0:XATTN_DIM].astype(BF16)
    v_ref[...] = kv[:, XATTN_DIM:2 * XATTN_DIM].astype(BF16)


def _mem_kv(mem2, g, w):
    rows = mem2.shape[0]
    tr = min(rows, 512)
    return pl.pallas_call(
        _mem_kv_body,
        grid=(rows // tr,),
        in_specs=[pl.BlockSpec((tr, D_MODEL), lambda i: (i, 0)),
                  _const_spec((1, D_MODEL)),
                  _const_spec((D_MODEL, 2 * XATTN_DIM))],
        out_specs=[pl.BlockSpec((tr, XATTN_DIM), lambda i: (i, 0)),
                   pl.BlockSpec((tr, XATTN_DIM), lambda i: (i, 0))],
        out_shape=[jax.ShapeDtypeStruct((rows, XATTN_DIM), BF16),
                   jax.ShapeDtypeStruct((rows, XATTN_DIM), BF16)],
        compiler_params=pltpu.CompilerParams(dimension_semantics=("arbitrary",)),
        name="mem_kv",
    )(mem2, g, w)


def _attn_unrolled_body(q_ref, k_ref, v_ref, o_ref, s_a, s_b, mc_a, mc_b, m_ref, l_ref, acc_ref, *, nq, tq):
    s_bufs = (s_a, s_b)
    mc_bufs = (mc_a, mc_b)
    mxu_row_split = 2
    blocks = [(qi, kb) for qi in range(nq) for kb in range(qi + 1)]

    def scores(i, slot):
        qi, kb = blocks[i]
        s = _dot_nt(q_ref[0, 0, qi * tq:(qi + 1) * tq, :], k_ref[0, 0, kb * tq:(kb + 1) * tq, :])
        if qi == kb:
            ri = lax.broadcasted_iota(jnp.int32, (tq, tq), 0)
            ci = lax.broadcasted_iota(jnp.int32, (tq, tq), 1)
            s = jnp.where(ci <= ri, s, NEG_INF)
        s_bufs[slot][...] = s
        mc_bufs[slot][...] = jnp.broadcast_to(jnp.max(s, axis=1, keepdims=True), (tq, LANES))

    def accumulate(i, slot):
        qi, kb = blocks[i]
        is_first = kb == 0
        is_last = kb == qi
        if is_first:
            m_new = mc_bufs[slot][...]
        else:
            m_prev = m_ref[...]
            m_new = jnp.maximum(m_prev, mc_bufs[slot][...])
            alpha = jnp.exp2(m_prev - m_new)
        p = jnp.exp2(s_bufs[slot][...] - jnp.concatenate([m_new] * (tq // LANES), axis=1))
        psum = p[:, 0:LANES]
        for c in range(1, tq // LANES):
            psum = psum + p[:, c * LANES:(c + 1) * LANES]
        l_new = psum if is_first else alpha * l_ref[...] + psum
        pb = p.astype(BF16)
        v = v_ref[0, 0, kb * tq:(kb + 1) * tq, :]
        if is_last:
            inv = 1.0 / jnp.sum(l_new, axis=1, keepdims=True)
        else:
            l_ref[...] = l_new
            m_ref[...] = m_new
        h = tq // mxu_row_split
        for r in range(mxu_row_split):
            rows = slice(r * h, (r + 1) * h)
            acc = _dot(pb[rows, :], v)
            if not is_first:
                acc = alpha[rows, :] * acc_ref[rows, :] + acc
            if is_last:
                o_ref[0, qi * tq + r * h:qi * tq + (r + 1) * h, :] = (acc * inv[rows, :]).astype(BF16)
            else:
                acc_ref[rows, :] = acc

    scores(0, 0)
    for i in range(len(blocks)):
        if i + 1 < len(blocks):
            scores(i + 1, (i + 1) % 2)
        accumulate(i, i % 2)


def _mla_attn_unrolled(q, k, v, *, tq):
    B, H, S, _ = q.shape
    per_head = lambda b, h: (b, h, 0, 0)
    return pl.pallas_call(
        functools.partial(_attn_unrolled_body, nq=S // tq, tq=tq),
        grid=(B, H),
        in_specs=[pl.BlockSpec((1, 1, S, QK_PAD_DIM), per_head),
                  pl.BlockSpec((1, 1, S, QK_PAD_DIM), per_head),
                  pl.BlockSpec((1, 1, S, V_HEAD_DIM), per_head)],
        out_specs=pl.BlockSpec((1, S, V_HEAD_DIM), lambda b, h: (b, 0, h)),
        out_shape=jax.ShapeDtypeStruct((B, S, H * V_HEAD_DIM), BF16),
        scratch_shapes=[pltpu.VMEM((tq, tq), F32), pltpu.VMEM((tq, tq), F32),
                        pltpu.VMEM((tq, LANES), F32), pltpu.VMEM((tq, LANES), F32),
                        pltpu.VMEM((tq, LANES), F32), pltpu.VMEM((tq, LANES), F32),
                        pltpu.VMEM((tq, V_HEAD_DIM), F32)],
        compiler_params=pltpu.CompilerParams(dimension_semantics=("arbitrary", "arbitrary"),
                                             vmem_limit_bytes=VMEM_LIMIT_BYTES),
        name="mla_attn",
    )(q, k, v)


def _merge_body(x_ref, ypool_ref, ymla_ref, xq_ref, gates_ref, kmem_ref, vmem_ref,
                wbp_ref, wbm_ref, wbx_ref, wout_ref, ffng_ref, wr_ref, br_ref,
                x1_ref, h2_ref, wts_ref, loc_ref, cnt_tab_ref, carry_tab_ref, counts_ref, carry_ref, *, tm):
    @pl.when(pl.program_id(0) == 0)
    def _():
        carry_ref[...] = jnp.zeros((N_EXPERTS, LANES), F32)

    xq = xq_ref[...]
    parts = []
    for h in range(XATTN_HEADS):
        cols = slice(h * XATTN_HEAD_DIM, (h + 1) * XATTN_HEAD_DIM)
        s = _dot_nt(xq[:, cols], kmem_ref[:, cols]) * (XATTN_HEAD_DIM ** -0.5)
        e = jnp.exp(s - jnp.max(s, axis=1, keepdims=True))
        p = e / jnp.sum(e, axis=1, keepdims=True)
        parts.append(_dot(p.astype(BF16), vmem_ref[:, cols]))
    ymem = jnp.concatenate(parts, axis=1).astype(BF16)

    gates = gates_ref[...].astype(F32)
    merged = (gates[:, 0:D_MODEL] * _dot(ypool_ref[...], wbp_ref[...])
              + gates[:, D_MODEL:2 * D_MODEL] * _dot(ymla_ref[...], wbm_ref[...])
              + gates[:, 2 * D_MODEL:3 * D_MODEL] * _dot(ymem, wbx_ref[...]))
    x1 = x_ref[...] + _dot(merged.astype(BF16), wout_ref[...])
    x1_ref[...] = x1
    h2 = _rms(x1, ffng_ref[...]).astype(BF16)
    h2_ref[...] = h2

    lt = _dot_nt(wr_ref[...], h2) + br_ref[...]
    gl = lt[N_EXPERTS:N_EXPERTS + N_GROUPS, :]
    gmax = jnp.max(gl, axis=0, keepdims=True)
    r4 = lax.broadcasted_iota(jnp.int32, (N_GROUPS, tm), 0).astype(F32)
    gidx = jnp.min(jnp.where(gl == gmax, r4, float(N_GROUPS)), axis=0, keepdims=True)
    pg = 1.0 / jnp.sum(jnp.exp(gl - gmax), axis=0, keepdims=True)
    esel = lt[0:EXPERTS_PER_GROUP, :]
    for g in range(1, N_GROUPS):
        esel = jnp.where(gidx == float(g), lt[g * EXPERTS_PER_GROUP:(g + 1) * EXPERTS_PER_GROUP, :], esel)
    r8 = lax.broadcasted_iota(jnp.int32, (EXPERTS_PER_GROUP, tm), 0).astype(F32)
    m1 = jnp.max(esel, axis=0, keepdims=True)
    i1 = jnp.min(jnp.where(esel == m1, r8, float(EXPERTS_PER_GROUP)), axis=0, keepdims=True)
    rest = jnp.where(r8 == i1, -jnp.inf, esel)
    m2 = jnp.max(rest, axis=0, keepdims=True)
    i2 = jnp.min(jnp.where(rest == m2, r8, float(EXPERTS_PER_GROUP)), axis=0, keepdims=True)
    e2 = jnp.exp(m2 - m1)
    den = 1.0 + e2
    wts_ref[0:1, :] = pg / den
    wts_ref[1:2, :] = pg * e2 / den
    ex1 = gidx * float(EXPERTS_PER_GROUP) + i1
    ex2 = gidx * float(EXPERTS_PER_GROUP) + i2

    r32 = lax.broadcasted_iota(jnp.int32, (N_EXPERTS, tm), 0).astype(F32)
    is1 = r32 == ex1
    is2 = r32 == ex2
    member = jnp.where(is1 | is2, 1.0, 0.0)
    upper = jnp.where(lax.broadcasted_iota(jnp.int32, (tm, tm), 0)
                      <= lax.broadcasted_iota(jnp.int32, (tm, tm), 1), 1.0, 0.0).astype(BF16)
    incl = _dot(member.astype(BF16), upper)
    run = jnp.floor((jnp.sum(member, axis=1, keepdims=True) + (RUN_ALIGN - 1)) / RUN_ALIGN) * RUN_ALIGN
    rcol = lax.broadcasted_iota(jnp.int32, (N_EXPERTS, 1), 0)
    run_start = jnp.zeros((N_EXPERTS, 1), F32)
    for e in range(N_EXPERTS - 1):
        run_start = run_start + jnp.where(rcol > e, run[e:e + 1, :], 0.0)
    pos = incl - 1.0 + run_start
    loc_ref[0:1, :] = jnp.sum(jnp.where(is1, pos, 0.0), axis=0, keepdims=True).astype(jnp.int32)
    loc_ref[1:2, :] = jnp.sum(jnp.where(is2, pos, 0.0), axis=0, keepdims=True).astype(jnp.int32)
    carry = carry_ref[...]
    total = carry + run
    cnt_tab_ref[...] = jnp.broadcast_to(run, (N_EXPERTS, LANES)).astype(jnp.int32)
    carry_tab_ref[...] = carry.astype(jnp.int32)
    carry_ref[...] = total
    counts_ref[...] = total.astype(jnp.int32)


def _merge(x2, ypool, ymla, xq, gates, kmem, vmem, wbp, wbm, wbx, wout, ffng, wr, br, *, B, S, tm, mem_len):
    T = B * S
    tps = S // tm
    row = lambda i: (i, 0)
    lane = lambda i: (0, i)
    memb = lambda i: (i // tps, 0)
    return pl.pallas_call(
        functools.partial(_merge_body, tm=tm),
        grid=(T // tm,),
        in_specs=[
            pl.BlockSpec((tm, D_MODEL), row),
            pl.BlockSpec((tm, POOL_DIM), row),
            pl.BlockSpec((tm, MLA_HEADS * V_HEAD_DIM), row),
            pl.BlockSpec((tm, XATTN_DIM), row),
            pl.BlockSpec((tm, N_BRANCHES * D_MODEL), row),
            pl.BlockSpec((mem_len, XATTN_DIM), memb),
            pl.BlockSpec((mem_len, XATTN_DIM), memb),
            _const_spec((POOL_DIM, D_MODEL)),
            _const_spec((MLA_HEADS * V_HEAD_DIM, D_MODEL)),
            _const_spec((XATTN_DIM, D_MODEL)),
            _const_spec((D_MODEL, D_MODEL)),
            _const_spec((1, D_MODEL)),
            _const_spec((ROUTER_ROWS, D_MODEL)),
            _const_spec((ROUTER_ROWS, 1)),
        ],
        out_specs=[
            pl.BlockSpec((tm, D_MODEL), row),
            pl.BlockSpec((tm, D_MODEL), row),
            pl.BlockSpec((2, tm), lane),
            pl.BlockSpec((2, tm), lane),
            pl.BlockSpec((N_EXPERTS, LANES), row),
            pl.BlockSpec((N_EXPERTS, LANES), row),
            pl.BlockSpec((N_EXPERTS, LANES), lambda i: (0, 0)),
        ],
        out_shape=[
            jax.ShapeDtypeStruct((T, D_MODEL), F32),
            jax.ShapeDtypeStruct((T, D_MODEL), BF16),
            jax.ShapeDtypeStruct((2, T), F32),
            jax.ShapeDtypeStruct((2, T), jnp.int32),
            jax.ShapeDtypeStruct((T // tm * N_EXPERTS, LANES), jnp.int32),
            jax.ShapeDtypeStruct((T // tm * N_EXPERTS, LANES), jnp.int32),
            jax.ShapeDtypeStruct((N_EXPERTS, LANES), jnp.int32),
        ],
        scratch_shapes=[pltpu.VMEM((N_EXPERTS, LANES), F32)],
        compiler_params=pltpu.CompilerParams(dimension_semantics=("arbitrary",),
                                             vmem_limit_bytes=VMEM_LIMIT_BYTES),
        name="merge",
    )(x2, ypool, ymla, xq, gates, kmem, vmem, wbp, wbm, wbx, wout, ffng, wr, br)


def _moe_pos_body(counts_ref, meta_ref):
    shift = int(math.log2(MOE_ROWS))
    cnt = counts_ref[...]
    padded = lax.shift_left(lax.shift_right_logical(cnt + (MOE_ROWS - 1), shift), shift)
    r32 = lax.broadcasted_iota(jnp.int32, (N_EXPERTS, LANES), 0)
    pad_start = jnp.zeros((N_EXPERTS, LANES), jnp.int32)
    for e in range(N_EXPERTS - 1):
        pad_start = pad_start + jnp.where(r32 > e, padded[e:e + 1, :], 0)
    pad_end = pad_start + padded

    lane = lax.broadcasted_iota(jnp.int32, (1, META_LANES), 1)
    block_row = lane * MOE_ROWS
    blk_e = jnp.zeros((1, META_LANES), jnp.int32)
    pe_row = jnp.zeros((1, META_LANES), jnp.int32)
    for e in range(N_EXPERTS):
        pe = pad_end[e:e + 1, 0:1]
        blk_e = blk_e + jnp.where(pe <= block_row, 1, 0)
        pe_row = pe_row + jnp.where(lane == META_PAD_END + e, pe, 0)
    blk_e = jnp.minimum(blk_e, N_EXPERTS - 1)
    nact = lax.shift_right_logical(pad_end[N_EXPERTS - 1:N_EXPERTS, 0:1], shift)
    meta = jnp.where(lane < META_PAD_END, blk_e, pe_row)
    meta_ref[...] = jnp.where(lane == META_NACT, nact, meta)


def _moe_pos(counts):
    full = lambda shape: pl.BlockSpec(shape, lambda i: (0,) * len(shape))
    return pl.pallas_call(
        _moe_pos_body,
        grid=(1,),
        in_specs=[full((N_EXPERTS, LANES))],
        out_specs=full((1, META_LANES)),
        out_shape=jax.ShapeDtypeStruct((1, META_LANES), jnp.int32),
        compiler_params=pltpu.CompilerParams(dimension_semantics=("arbitrary",)),
        name="moe_pos",
    )(counts)


def _pack_bf16_pairs(x):
    lo = pltpu.bitcast(x[:, 0:PACKED_DIM], U32)
    hi = pltpu.bitcast(x[:, PACKED_DIM:D_MODEL], U32)
    return hi | lax.shift_right_logical(lo, jnp.uint32(16))


def _unpack_bf16_pairs(w):
    lo = pltpu.bitcast(lax.shift_left(w, jnp.uint32(16)), F32)
    hi = pltpu.bitcast(w & jnp.uint32(0xFFFF0000), F32)
    return jnp.concatenate([lo, hi], axis=1)


def _loc_rows(tm):
    return 2 * tm + N_EXPERTS * RUN_ALIGN


def _run_copies(tile, cnt_ref, carry_ref, meta_ref, make_copy):
    def per_expert(e, local):
        n = pl.multiple_of(cnt_ref[tile * N_EXPERTS + e], RUN_ALIGN)
        start = jnp.where(e == 0, 0, meta_ref[META_PAD_END + jnp.maximum(e - 1, 0)])
        glob = pl.multiple_of(start + carry_ref[tile * N_EXPERTS + e], RUN_ALIGN)

        @pl.when(n > 0)
        def _():
            make_copy(pl.multiple_of(local, RUN_ALIGN), glob, n).start()

        return local + n

    return pl.multiple_of(lax.fori_loop(0, N_EXPERTS, per_expert, 0), RUN_ALIGN)


def _tile_rows(tile, cnt_ref):
    total = lax.fori_loop(0, N_EXPERTS, lambda e, t: t + cnt_ref[tile * N_EXPERTS + e], 0)
    return pl.multiple_of(total, RUN_ALIGN)


def _dispatch_body(meta_ref, cnt_ref, carry_ref, loc_ref, h2_ref, xs_ref, xloc_ref, zero_ref, sems, zsem,
                   *, tm, n_blocks):
    tile = pl.program_id(0)
    slot = lax.rem(tile, 2)

    def wait_rows(t, s):
        n = _tile_rows(t, cnt_ref)
        pltpu.make_async_copy(xloc_ref.at[s, pl.ds(0, n)], xs_ref.at[pl.ds(0, n)], sems.at[s]).wait()

    def pad_copy(e):
        end = pl.multiple_of(meta_ref[META_PAD_END + e], MOE_ROWS)
        return pltpu.make_async_copy(zero_ref, xs_ref.at[pl.ds(end - MOE_ROWS, MOE_ROWS)], zsem)

    def has_rows(e):
        prev = jnp.where(e == 0, 0, meta_ref[META_PAD_END + jnp.maximum(e - 1, 0)])
        return meta_ref[META_PAD_END + e] > prev

    @pl.when(tile == 0)
    def _():
        zero_ref[...] = jnp.zeros((MOE_ROWS, PACKED_DIM), U32)

        def start(e, c):
            @pl.when(has_rows(e))
            def _():
                pad_copy(e).start()
            return c

        def wait(e, c):
            @pl.when(has_rows(e))
            def _():
                pad_copy(e).wait()
            return c

        def tail_copy(b):
            return pltpu.make_async_copy(
                zero_ref, xs_ref.at[pl.ds(pl.multiple_of(b * MOE_ROWS, MOE_ROWS), MOE_ROWS)], zsem)

        def tail_start(b, c):
            tail_copy(b).start()
            return c

        def tail_wait(b, c):
            tail_copy(b).wait()
            return c

        nact = meta_ref[META_NACT]
        lax.fori_loop(0, N_EXPERTS, start, 0)
        lax.fori_loop(nact, n_blocks, tail_start, 0)
        lax.fori_loop(0, N_EXPERTS, wait, 0)
        lax.fori_loop(nact, n_blocks, tail_wait, 0)

    @pl.when(tile >= 2)
    def _():
        wait_rows(tile - 2, slot)

    r = lax.broadcasted_iota(jnp.int32, (_loc_rows(tm), tm), 0)
    onehot = jnp.where((r == loc_ref[0:1, :]) | (r == loc_ref[1:2, :]), 1.0, 0.0).astype(BF16)
    xloc_ref[slot] = _pack_bf16_pairs(_dot(onehot, h2_ref[...]))

    def make_copy(local, glob, n):
        return pltpu.make_async_copy(xloc_ref.at[slot, pl.ds(local, n)], xs_ref.at[pl.ds(glob, n)],
                                     sems.at[slot])

    _run_copies(tile, cnt_ref, carry_ref, meta_ref, make_copy)

    @pl.when(tile == pl.num_programs(0) - 1)
    def _():
        @pl.when(tile >= 1)
        def _():
            wait_rows(tile - 1, 1 - slot)

        wait_rows(tile, slot)


def _dispatch(meta1, cnt_tab, carry_tab, loc, h2, *, R, tm):
    T = h2.shape[0]
    return pl.pallas_call(
        functools.partial(_dispatch_body, tm=tm, n_blocks=R // MOE_ROWS),
        grid_spec=pltpu.PrefetchScalarGridSpec(
            num_scalar_prefetch=3,
            grid=(T // tm,),
            in_specs=[pl.BlockSpec((2, tm), lambda i, *_: (0, i)),
                      pl.BlockSpec((tm, D_MODEL), lambda i, *_: (i, 0))],
            out_specs=pl.BlockSpec(memory_space=pl.ANY),
            scratch_shapes=[pltpu.VMEM((2, _loc_rows(tm), PACKED_DIM), U32),
                            pltpu.VMEM((MOE_ROWS, PACKED_DIM), U32),
                            pltpu.SemaphoreType.DMA((2,)), pltpu.SemaphoreType.DMA],
        ),
        out_shape=jax.ShapeDtypeStruct((R, PACKED_DIM), U32),
        compiler_params=pltpu.CompilerParams(dimension_semantics=("arbitrary",),
                                             vmem_limit_bytes=VMEM_LIMIT_BYTES),
        name="dispatch",
    )(meta1, cnt_tab, carry_tab, loc, h2)


def _moe_ffn_body(meta_ref, xs_ref, wg_hbm, wu_hbm, wd_hbm, ys_ref,
                  xbuf, ybuf, wg_stage, wu_stage, wd_stage, wg_b, wu_b, wd_b, zero_ref,
                  xsem, ysem, wsem, zsem, *, n_blocks):
    nact = meta_ref[META_NACT]
    shift = int(math.log2(MOE_ROWS))

    def rows_of(b):
        return pl.ds(pl.multiple_of(b * MOE_ROWS, MOE_ROWS), MOE_ROWS)

    def x_copy(b, s):
        return pltpu.make_async_copy(xs_ref.at[rows_of(b)], xbuf.at[s], xsem.at[s])

    def y_copy(b, s):
        return pltpu.make_async_copy(ybuf.at[s], ys_ref.at[rows_of(b)], ysem.at[s])

    def w_copies(e, s):
        return (pltpu.make_async_copy(wg_hbm.at[e], wg_stage.at[s], wsem.at[s]),
                pltpu.make_async_copy(wu_hbm.at[e], wu_stage.at[s], wsem.at[s]),
                pltpu.make_async_copy(wd_hbm.at[e], wd_stage.at[s], wsem.at[s]))

    def tail_copy(b):
        return pltpu.make_async_copy(zero_ref, ys_ref.at[rows_of(b)], zsem)

    zero_ref[...] = jnp.zeros((MOE_ROWS, PACKED_DIM), U32)
    lax.fori_loop(nact, n_blocks, lambda b, c: (tail_copy(b).start(), c)[1], 0)

    x_copy(0, 0).start()
    for cp in w_copies(meta_ref[0], 0):
        cp.start()

    def block(b, s, ws_prev):
        valid = b < nact
        e = meta_ref[jnp.minimum(b, nact - 1)]
        changed = jnp.logical_and(valid, jnp.logical_or(b == 0, e != meta_ref[jnp.maximum(b - 1, 0)]))
        ws = jnp.where(changed, 1 - ws_prev, ws_prev)

        @pl.when(changed)
        def _():
            for cp in w_copies(e, ws):
                cp.wait()
            wg_b[...] = wg_stage[ws].astype(BF16)
            wu_b[...] = wu_stage[ws].astype(BF16)
            wd_b[...] = wd_stage[ws].astype(BF16)
            nxt = lax.shift_right_logical(meta_ref[META_PAD_END + e], shift)

            @pl.when(nxt < nact)
            def _():
                for cp in w_copies(meta_ref[nxt], 1 - ws):
                    cp.start()

        @pl.when(valid)
        def _():
            x_copy(b, s).wait()

            @pl.when(b + 1 < nact)
            def _():
                x_copy(b + 1, 1 - s).start()

            @pl.when(b >= 2)
            def _():
                y_copy(b - 2, s).wait()

            for c in range(MOE_ROWS // MOE_CHUNK):
                rows = slice(c * MOE_CHUNK, (c + 1) * MOE_CHUNK)
                x = _unpack_bf16_pairs(xbuf[s, rows, :]).astype(BF16)
                g = _dot(x, wg_b[...])
                a = (g * jax.nn.sigmoid(g)) * _dot(x, wu_b[...])
                y = _dot(a.astype(BF16), wd_b[...])
                ybuf[s, rows, :] = _pack_bf16_pairs(y.astype(BF16).astype(F32))
            y_copy(b, s).start()

        return ws

    def pair(i, ws):
        return block(2 * i + 1, 1, block(2 * i, 0, ws))

    lax.fori_loop(0, lax.shift_right_logical(nact + 1, 1), pair, 1)

    @pl.when(nact >= 2)
    def _():
        y_copy(nact - 2, lax.rem(nact, 2)).wait()

    y_copy(nact - 1, lax.rem(nact - 1, 2)).wait()
    lax.fori_loop(nact, n_blocks, lambda b, c: (tail_copy(b).wait(), c)[1], 0)


def _moe_ffn(meta1, xs, wg, wu, wd):
    R = xs.shape[0]
    hbm = pl.BlockSpec(memory_space=pl.ANY)
    return pl.pallas_call(
        functools.partial(_moe_ffn_body, n_blocks=R // MOE_ROWS),
        grid_spec=pltpu.PrefetchScalarGridSpec(
            num_scalar_prefetch=1,
            grid=(1,),
            in_specs=[hbm, hbm, hbm, hbm],
            out_specs=hbm,
            scratch_shapes=[pltpu.VMEM((2, MOE_ROWS, PACKED_DIM), U32),
                            pltpu.VMEM((2, MOE_ROWS, PACKED_DIM), U32),
                            pltpu.VMEM((2, D_MODEL, D_EXPERT), F32),
                            pltpu.VMEM((2, D_MODEL, D_EXPERT), F32),
                            pltpu.VMEM((2, D_EXPERT, D_MODEL), F32),
                            pltpu.VMEM((D_MODEL, D_EXPERT), BF16),
                            pltpu.VMEM((D_MODEL, D_EXPERT), BF16),
                            pltpu.VMEM((D_EXPERT, D_MODEL), BF16),
                            pltpu.VMEM((MOE_ROWS, PACKED_DIM), U32),
                            pltpu.SemaphoreType.DMA((2,)), pltpu.SemaphoreType.DMA((2,)),
                            pltpu.SemaphoreType.DMA((2,)), pltpu.SemaphoreType.DMA],
        ),
        out_shape=jax.ShapeDtypeStruct((R, PACKED_DIM), U32),
        compiler_params=pltpu.CompilerParams(dimension_semantics=("arbitrary",),
                                             vmem_limit_bytes=VMEM_LIMIT_BYTES),
        name="moe_ffn",
    )(meta1, xs, wg, wu, wd)


def _combine_body(meta_ref, cnt_ref, carry_ref, x1_ref, loc_ref, wts_ref, fg_ref, ys_ref, out_ref,
                  yloc_ref, sems, *, tm):
    tile = pl.program_id(0)
    slot = lax.rem(tile, 2)

    def fetch(t, s):
        def make_copy(local, glob, n):
            return pltpu.make_async_copy(ys_ref.at[pl.ds(glob, n)], yloc_ref.at[s, pl.ds(local, n)],
                                         sems.at[s])
        _run_copies(t, cnt_ref, carry_ref, meta_ref, make_copy)

    @pl.when(tile == 0)
    def _():
        yloc_ref[...] = jnp.zeros(yloc_ref.shape, U32)
        fetch(tile, slot)

    @pl.when(tile + 1 < pl.num_programs(0))
    def _():
        fetch(tile + 1, 1 - slot)

    n = _tile_rows(tile, cnt_ref)
    pltpu.make_async_copy(ys_ref.at[pl.ds(0, n)], yloc_ref.at[slot, pl.ds(0, n)], sems.at[slot]).wait()
    r = lax.broadcasted_iota(jnp.int32, (_loc_rows(tm), tm), 0)
    is0 = r == loc_ref[0:1, :]
    is1 = r == loc_ref[1:2, :]
    row_w = jnp.sum(jnp.where(is0, wts_ref[0:1, :], 0.0) + jnp.where(is1, wts_ref[1:2, :], 0.0),
                    axis=1, keepdims=True)
    yw = (row_w * _unpack_bf16_pairs(yloc_ref[slot])).astype(BF16)
    twohot = jnp.where(is0 | is1, 1.0, 0.0).astype(BF16)
    moe = lax.dot_general(twohot, yw, (((0,), (0,)), ((), ())), preferred_element_type=F32)
    out_ref[...] = _rms(x1_ref[...] + moe, fg_ref[...])


def _combine(meta1, cnt_tab, carry_tab, x1, loc, wts, fg, ys, *, tm):
    T = x1.shape[0]
    row = lambda i, *_: (i, 0)
    lane = lambda i, *_: (0, i)
    return pl.pallas_call(
        functools.partial(_combine_body, tm=tm),
        grid_spec=pltpu.PrefetchScalarGridSpec(
            num_scalar_prefetch=3,
            grid=(T // tm,),
            in_specs=[pl.BlockSpec((tm, D_MODEL), row),
                      pl.BlockSpec((2, tm), lane),
                      pl.BlockSpec((2, tm), lane),
                      pl.BlockSpec((1, D_MODEL), lambda i, *_: (0, 0)),
                      pl.BlockSpec(memory_space=pl.ANY)],
            out_specs=pl.BlockSpec((tm, D_MODEL), row),
            scratch_shapes=[pltpu.VMEM((2, _loc_rows(tm), PACKED_DIM), U32),
                            pltpu.SemaphoreType.DMA((2,))],
        ),
        out_shape=jax.ShapeDtypeStruct((T, D_MODEL), F32),
        compiler_params=pltpu.CompilerParams(dimension_semantics=("arbitrary",),
                                             vmem_limit_bytes=VMEM_LIMIT_BYTES),
        name="combine",
    )(meta1, cnt_tab, carry_tab, x1, loc, wts, fg, ys)


def _tile(n, t):
    t = min(n, t)
    assert n % t == 0, (n, t)
    return t


def kernel(x, mem, positions, mix_norm_g, w_in, gate_b, q_norm_g, w_uq, kv_norm_g, w_uk, w_uv, pool_w, pool_scale, mem_norm_g, w_mem_kv, w_br_pool, w_br_mla, w_br_mem, w_out, ffn_norm_g, w_router_group, b_router_group, w_router_expert, b_router_expert, w_gate_e, w_up_e, w_down_e, final_norm_g):
    B, S, D = x.shape
    assert D == D_MODEL and mix_norm_g.shape[0] == 1
    T = B * S
    mem_len = mem.shape[1]
    tm = _tile(S, 512)
    l = 0

    win_p = _pack_w_in(w_in[l])
    wuq_p = jnp.pad(w_uq[l].reshape(Q_LORA_RANK, MLA_HEADS, QK_NOPE_DIM + QK_ROPE_DIM),
                    ((0, 0), (0, 0), (0, QK_PAD_DIM - QK_NOPE_DIM - QK_ROPE_DIM))
                    ).reshape(Q_LORA_RANK, MLA_HEADS * QK_PAD_DIM).astype(BF16)
    inv_freq = 1.0 / (ROPE_THETA ** (jnp.arange(0, QK_ROPE_DIM, 2, dtype=F32) / QK_ROPE_DIM))
    invf = jnp.concatenate([inv_freq, inv_freq, jnp.zeros((LANES - QK_ROPE_DIM,), F32)])[None, :]
    wr = jnp.concatenate([w_router_expert[l], w_router_group[l],
                          jnp.zeros((D_MODEL, ROUTER_ROWS - N_EXPERTS - N_GROUPS), F32)], axis=1).T.astype(BF16)
    br = jnp.concatenate([b_router_expert[l], b_router_group[l],
                          jnp.zeros((ROUTER_ROWS - N_EXPERTS - N_GROUPS,), F32)])[:, None].astype(F32)
    x2 = x.reshape(T, D_MODEL)
    pos2 = positions.reshape(T, 1)

    ypool, xq, gates, q, k, v = _mixer_in(
        x2, pos2, invf, mix_norm_g[l][None, :], win_p, gate_b[l], q_norm_g[l][None, :], wuq_p,
        kv_norm_g[l][None, :], w_uk[l].astype(BF16), w_uv[l].astype(BF16), pool_w[l].astype(BF16),
        pool_scale[l][None, :], B=B, S=S, tm=tm)
    kmem, vmem = _mem_kv(mem.reshape(B * mem_len, D_MODEL), mem_norm_g[l][None, :], w_mem_kv[l].astype(BF16))
    ymla = _mla_attn_unrolled(q, k, v, tq=tm).reshape(T, MLA_HEADS * V_HEAD_DIM)
    x1, h2, wts, loc, cnt_tab, carry_tab, counts = _merge(
        x2, ypool, ymla, xq, gates, kmem, vmem, w_br_pool[l].astype(BF16), w_br_mla[l].astype(BF16),
        w_br_mem[l].astype(BF16), w_out[l].astype(BF16), ffn_norm_g[l][None, :], wr, br,
        B=B, S=S, tm=tm, mem_len=mem_len)

    R = 2 * T + (T // tm) * N_EXPERTS * RUN_ALIGN + N_EXPERTS * MOE_ROWS
    assert R % MOE_ROWS == 0 and R // MOE_ROWS <= META_PAD_END
    meta1 = _moe_pos(counts).reshape(META_LANES)
    cnt1 = cnt_tab[:, 0]
    carry1 = carry_tab[:, 0]
    xs = _dispatch(meta1, cnt1, carry1, loc, h2, R=R, tm=tm)
    ys = _moe_ffn(meta1, xs, w_gate_e[l], w_up_e[l], w_down_e[l])
    out = _combine(meta1, cnt1, carry1, x1, loc, wts, final_norm_g[None, :], ys, tm=tm)
    return out.reshape(B, S, D_MODEL)
```

```python
import functools
import math

import jax
import jax.numpy as jnp
from jax import lax
from jax.experimental import pallas as pl
from jax.experimental.pallas import tpu as pltpu

D_MODEL = 1024
POOL_WINDOWS = (2, 4, 8, 16)
POOL_GROUP_DIM = 128
POOL_DIM = 512
MLA_HEADS = 8
QK_NOPE_DIM = 128
QK_ROPE_DIM = 64
V_HEAD_DIM = 128
Q_LORA_RANK = 384
KV_LORA_RANK = 256
ROPE_THETA = 10000.0
XATTN_HEADS = 4
XATTN_HEAD_DIM = 128
XATTN_DIM = 512
N_BRANCHES = 3
N_GROUPS = 4
EXPERTS_PER_GROUP = 8
N_EXPERTS = 32
D_EXPERT = 256
RMS_EPS = 1e-6
NEG_INF = -1e30

LANES = 128
QK_PAD_DIM = 2 * LANES
POOL_HALO = 16
MOE_ROWS = 512
MOE_CHUNK = 256
RUN_ALIGN = 8
PACKED_DIM = D_MODEL // 2
ROUTER_ROWS = 40
META_LANES = 256
META_PAD_END = 192
META_NACT = 255
VMEM_LIMIT_BYTES = 56 * 1024 * 1024

IN_POOL, IN_QD, IN_KV, IN_XQ, IN_GATE, IN_KR, IN_END = 0, 512, 896, 1152, 1664, 4736, 4864
W_IN_KR, W_IN_XQ, W_IN_END = 1152, 1216, 4800

F32 = jnp.float32
BF16 = jnp.bfloat16
U32 = jnp.uint32


def _rms(x, g):
    ms = jnp.mean(x * x, axis=-1, keepdims=True)
    return (x * lax.rsqrt(ms + RMS_EPS)) * g


def _dot(a, b):
    return jnp.dot(a, b, preferred_element_type=F32)


def _dot_nt(a, b):
    return lax.dot_general(a, b, (((1,), (1,)), ((), ())), preferred_element_type=F32)


def _const_spec(shape):
    nd = len(shape)
    return pl.BlockSpec(shape, lambda *_: (0,) * nd, pipeline_mode=pl.Buffered(1))


def _pack_w_in_body(w_ref, o_ref):
    w = w_ref[...]
    rows = w.shape[0]
    o_ref[...] = jnp.concatenate(
        [w[:, 0:W_IN_KR], w[:, W_IN_XQ:W_IN_END], w[:, W_IN_KR:W_IN_XQ],
         jnp.zeros((rows, LANES - QK_ROPE_DIM), w.dtype)], axis=1).astype(BF16)


def _pack_w_in(w):
    tr = 128
    return pl.pallas_call(
        _pack_w_in_body,
        grid=(D_MODEL // tr,),
        in_specs=[pl.BlockSpec((tr, W_IN_END), lambda i: (i, 0))],
        out_specs=pl.BlockSpec((tr, IN_END), lambda i: (i, 0)),
        out_shape=jax.ShapeDtypeStruct((D_MODEL, IN_END), BF16),
        compiler_params=pltpu.CompilerParams(dimension_semantics=("arbitrary",)),
        name="pack_w_in",
    )(w)


def _mixer_in_body(x_ref, pos_ref, invf_ref, mixg_ref, win_ref, gateb_ref, qg_ref, wuq_ref,
                   kvg_ref, wuk_ref, wuv_ref, poolw_ref, pools_ref,
                   ypool_ref, xq_ref, gates_ref, q_ref, k_ref, v_ref, ext_ref,
                   *, tm, tiles_per_seq, q_scale):
    si = lax.rem(pl.program_id(0), tiles_per_seq)

    @pl.when(pl.program_id(0) == 0)
    def _():
        ext_ref[0:POOL_HALO, :] = jnp.zeros((POOL_HALO, POOL_DIM), F32)

    hb = _rms(x_ref[...], mixg_ref[...]).astype(BF16)

    u = _dot(hb, win_ref[:, IN_POOL:IN_QD])
    ext_ref[0:POOL_HALO, :] = jnp.where(si == 0, 0.0, ext_ref[0:POOL_HALO, :])
    ext_ref[POOL_HALO:POOL_HALO + tm, :] = u

    for c in range(N_BRANCHES):
        gl = _dot(hb, win_ref[:, IN_GATE + c * D_MODEL:IN_GATE + (c + 1) * D_MODEL])
        gates_ref[:, c * D_MODEL:(c + 1) * D_MODEL] = jax.nn.sigmoid(gl + gateb_ref[c:c + 1, :]).astype(BF16)

    ang = pos_ref[...].astype(F32) * invf_ref[...]
    cos = jnp.cos(ang)
    sin = jnp.sin(ang)
    first_half = lax.broadcasted_iota(jnp.int32, (tm, LANES), 1) < (QK_ROPE_DIM // 2)
    sin_signed = jnp.where(first_half, -sin, sin)

    def rope(r):
        swapped = jnp.where(first_half, pltpu.roll(r, LANES - QK_ROPE_DIM // 2, 1),
                            pltpu.roll(r, QK_ROPE_DIM // 2, 1))
        return r * cos + swapped * sin_signed

    cq = _rms(_dot(hb, win_ref[:, IN_QD:IN_KV]), qg_ref[...]).astype(BF16)
    for h in range(MLA_HEADS):
        qh = _dot(cq, wuq_ref[:, h * QK_PAD_DIM:(h + 1) * QK_PAD_DIM])
        q_ref[0, h, :, 0:LANES] = (qh[:, 0:LANES] * q_scale).astype(BF16)
        q_ref[0, h, :, LANES:QK_PAD_DIM] = (rope(qh[:, LANES:QK_PAD_DIM]) * q_scale).astype(BF16)

    ckv = _rms(_dot(hb, win_ref[:, IN_KV:IN_XQ]), kvg_ref[...]).astype(BF16)
    kr = rope(_dot(hb, win_ref[:, IN_KR:IN_END])).astype(BF16)
    for hp in range(MLA_HEADS // 2):
        cols = slice(hp * 2 * LANES, (hp + 1) * 2 * LANES)
        kn = _dot(ckv, wuk_ref[:, cols]).astype(BF16)
        vv = _dot(ckv, wuv_ref[:, cols]).astype(BF16)
        for j in range(2):
            h = 2 * hp + j
            k_ref[0, h, :, 0:LANES] = kn[:, j * LANES:(j + 1) * LANES]
            k_ref[0, h, :, LANES:QK_PAD_DIM] = kr
            v_ref[0, h] = vv[:, j * LANES:(j + 1) * LANES]

    t_seq = lax.broadcasted_iota(jnp.int32, (tm, 1), 0) + si * tm
    for g, w in enumerate(POOL_WINDOWS):
        lo = g * POOL_GROUP_DIM
        hi = lo + POOL_GROUP_DIM
        acc = u[:, lo:hi]
        for j in range(1, w):
            acc = acc + ext_ref[POOL_HALO - j:POOL_HALO - j + tm, lo:hi]
        cnt = jnp.minimum(t_seq + 1, w).astype(F32)
        p = acc / cnt - u[:, lo:hi]
        y = _dot(p.astype(BF16), poolw_ref[g]) * pools_ref[:, lo:hi]
        ypool_ref[:, lo:hi] = y.astype(BF16)
    ext_ref[0:POOL_HALO, :] = ext_ref[tm:tm + POOL_HALO, :]

    xq_ref[...] = _dot(hb, win_ref[:, IN_XQ:IN_GATE]).astype(BF16)


def _mixer_in(x2, pos2, invf, mixg, win_p, gate_b, qg, wuq_p, kvg, wuk, wuv, pool_w, pool_s, *, B, S, tm):
    T = B * S
    tps = S // tm
    q_scale = (QK_NOPE_DIM + QK_ROPE_DIM) ** -0.5 * math.log2(math.e)
    body = functools.partial(_mixer_in_body, tm=tm, tiles_per_seq=tps, q_scale=q_scale)
    row = lambda i: (i, 0)
    head = lambda i: (i // tps, 0, i % tps, 0)
    return pl.pallas_call(
        body,
        grid=(T // tm,),
        in_specs=[
            pl.BlockSpec((tm, D_MODEL), row),
            pl.BlockSpec((tm, 1), row),
            _const_spec((1, LANES)),
            _const_spec((1, D_MODEL)),
            _const_spec((D_MODEL, IN_END)),
            _const_spec((N_BRANCHES, D_MODEL)),
            _const_spec((1, Q_LORA_RANK)),
            _const_spec((Q_LORA_RANK, MLA_HEADS * QK_PAD_DIM)),
            _const_spec((1, KV_LORA_RANK)),
            _const_spec((KV_LORA_RANK, MLA_HEADS * QK_NOPE_DIM)),
            _const_spec((KV_LORA_RANK, MLA_HEADS * V_HEAD_DIM)),
            _const_spec((len(POOL_WINDOWS), POOL_GROUP_DIM, POOL_GROUP_DIM)),
            _const_spec((1, POOL_DIM)),
        ],
        out_specs=[
            pl.BlockSpec((tm, POOL_DIM), row),
            pl.BlockSpec((tm, XATTN_DIM), row),
            pl.BlockSpec((tm, N_BRANCHES * D_MODEL), row),
            pl.BlockSpec((1, MLA_HEADS, tm, QK_PAD_DIM), head),
            pl.BlockSpec((1, MLA_HEADS, tm, QK_PAD_DIM), head),
            pl.BlockSpec((1, MLA_HEADS, tm, V_HEAD_DIM), head),
        ],
        out_shape=[
            jax.ShapeDtypeStruct((T, POOL_DIM), BF16),
            jax.ShapeDtypeStruct((T, XATTN_DIM), BF16),
            jax.ShapeDtypeStruct((T, N_BRANCHES * D_MODEL), BF16),
            jax.ShapeDtypeStruct((B, MLA_HEADS, S, QK_PAD_DIM), BF16),
            jax.ShapeDtypeStruct((B, MLA_HEADS, S, QK_PAD_DIM), BF16),
            jax.ShapeDtypeStruct((B, MLA_HEADS, S, V_HEAD_DIM), BF16),
        ],
        scratch_shapes=[pltpu.VMEM((tm + POOL_HALO, POOL_DIM), F32)],
        compiler_params=pltpu.CompilerParams(dimension_semantics=("arbitrary",),
                                             vmem_limit_bytes=VMEM_LIMIT_BYTES),
        name="mixer_in",
    )(x2, pos2, invf, mixg, win_p, gate_b, qg, wuq_p, kvg, wuk, wuv, pool_w, pool_s)


def _mem_kv_body(mem_ref, g_ref, w_ref, k_ref, v_ref):
    mb = _rms(mem_ref[...], g_ref[...]).astype(BF16)
    kv = _dot(mb, w_ref[...])
    k_ref[...] = kv[:, 0:XATTN_DIM].astype(BF16)
    v_ref[...] = kv[:, XATTN_DIM:2 * XATTN_DIM].astype(BF16)


def _mem_kv(mem2, g, w):
    rows = mem2.shape[0]
    tr = min(rows, 512)
    return pl.pallas_call(
        _mem_kv_body,
        grid=(rows // tr,),
        in_specs=[pl.BlockSpec((tr, D_MODEL), lambda i: (i, 0)),
                  _const_spec((1, D_MODEL)),
                  _const_spec((D_MODEL, 2 * XATTN_DIM))],
        out_specs=[pl.BlockSpec((tr, XATTN_DIM), lambda i: (i, 0)),
                   pl.BlockSpec((tr, XATTN_DIM), lambda i: (i, 0))],
        out_shape=[jax.ShapeDtypeStruct((rows, XATTN_DIM), BF16),
                   jax.ShapeDtypeStruct((rows, XATTN_DIM), BF16)],
        compiler_params=pltpu.CompilerParams(dimension_semantics=("arbitrary",)),
        name="mem_kv",
    )(mem2, g, w)


def _attn_unrolled_body(q_ref, k_ref, v_ref, o_ref, s_a, s_b, mc_a, mc_b, m_ref, l_ref, acc_ref, *, nq, tq):
    s_bufs = (s_a, s_b)
    mc_bufs = (mc_a, mc_b)
    mxu_row_split = 2
    blocks = [(qi, kb) for qi in range(nq) for kb in range(qi + 1)]

    def scores(i, slot):
        qi, kb = blocks[i]
        s = _dot_nt(q_ref[0, 0, qi * tq:(qi + 1) * tq, :], k_ref[0, 0, kb * tq:(kb + 1) * tq, :])
        if qi == kb:
            ri = lax.broadcasted_iota(jnp.int32, (tq, tq), 0)
            ci = lax.broadcasted_iota(jnp.int32, (tq, tq), 1)
            s = jnp.where(ci <= ri, s, NEG_INF)
        s_bufs[slot][...] = s
        mc_bufs[slot][...] = jnp.broadcast_to(jnp.max(s, axis=1, keepdims=True), (tq, LANES))

    def accumulate(i, slot):
        qi, kb = blocks[i]
        is_first = kb == 0
        is_last = kb == qi
        if is_first:
            m_new = mc_bufs[slot][...]
        else:
            m_prev = m_ref[...]
            m_new = jnp.maximum(m_prev, mc_bufs[slot][...])
            alpha = jnp.exp2(m_prev - m_new)
        p = jnp.exp2(s_bufs[slot][...] - jnp.concatenate([m_new] * (tq // LANES), axis=1))
        psum = p[:, 0:LANES]
        for c in range(1, tq // LANES):
            psum = psum + p[:, c * LANES:(c + 1) * LANES]
        l_new = psum if is_first else alpha * l_ref[...] + psum
        pb = p.astype(BF16)
        v = v_ref[0, 0, kb * tq:(kb + 1) * tq, :]
        if is_last:
            inv = 1.0 / jnp.sum(l_new, axis=1, keepdims=True)
        else:
            l_ref[...] = l_new
            m_ref[...] = m_new
        h = tq // mxu_row_split
        for r in range(mxu_row_split):
            rows = slice(r * h, (r + 1) * h)
            acc = _dot(pb[rows, :], v)
            if not is_first:
                acc = alpha[rows, :] * acc_ref[rows, :] + acc
            if is_last:
                o_ref[0, qi * tq + r * h:qi * tq + (r + 1) * h, :] = (acc * inv[rows, :]).astype(BF16)
            else:
                acc_ref[rows, :] = acc

    scores(0, 0)
    for i in range(len(blocks)):
        if i + 1 < len(blocks):
            scores(i + 1, (i + 1) % 2)
        accumulate(i, i % 2)


def _mla_attn_unrolled(q, k, v, *, tq):
    B, H, S, _ = q.shape
    per_head = lambda b, h: (b, h, 0, 0)
    return pl.pallas_call(
        functools.partial(_attn_unrolled_body, nq=S // tq, tq=tq),
        grid=(B, H),
        in_specs=[pl.BlockSpec((1, 1, S, QK_PAD_DIM), per_head),
                  pl.BlockSpec((1, 1, S, QK_PAD_DIM), per_head),
                  pl.BlockSpec((1, 1, S, V_HEAD_DIM), per_head)],
        out_specs=pl.BlockSpec((1, S, V_HEAD_DIM), lambda b, h: (b, 0, h)),
        out_shape=jax.ShapeDtypeStruct((B, S, H * V_HEAD_DIM), BF16),
        scratch_shapes=[pltpu.VMEM((tq, tq), F32), pltpu.VMEM((tq, tq), F32),
                        pltpu.VMEM((tq, LANES), F32), pltpu.VMEM((tq, LANES), F32),
                        pltpu.VMEM((tq, LANES), F32), pltpu.VMEM((tq, LANES), F32),
                        pltpu.VMEM((tq, V_HEAD_DIM), F32)],
        compiler_params=pltpu.CompilerParams(dimension_semantics=("arbitrary", "arbitrary"),
                                             vmem_limit_bytes=VMEM_LIMIT_BYTES),
        name="mla_attn",
    )(q, k, v)


def _merge_body(x_ref, ypool_ref, ymla_ref, xq_ref, gates_ref, kmem_ref, vmem_ref,
                wbp_ref, wbm_ref, wbx_ref, wout_ref, ffng_ref, wr_ref, br_ref,
                x1_ref, h2_ref, wts_ref, loc_ref, cnt_tab_ref, carry_tab_ref, counts_ref, carry_ref, *, tm):
    @pl.when(pl.program_id(0) == 0)
    def _():
        carry_ref[...] = jnp.zeros((N_EXPERTS, LANES), F32)

    xq = xq_ref[...]
    parts = []
    for h in range(XATTN_HEADS):
        cols = slice(h * XATTN_HEAD_DIM, (h + 1) * XATTN_HEAD_DIM)
        s = _dot_nt(xq[:, cols], kmem_ref[:, cols]) * (XATTN_HEAD_DIM ** -0.5)
        e = jnp.exp(s - jnp.max(s, axis=1, keepdims=True))
        p = e / jnp.sum(e, axis=1, keepdims=True)
        parts.append(_dot(p.astype(BF16), vmem_ref[:, cols]))
    ymem = jnp.concatenate(parts, axis=1).astype(BF16)

    gates = gates_ref[...].astype(F32)
    merged = (gates[:, 0:D_MODEL] * _dot(ypool_ref[...], wbp_ref[...])
              + gates[:, D_MODEL:2 * D_MODEL] * _dot(ymla_ref[...], wbm_ref[...])
              + gates[:, 2 * D_MODEL:3 * D_MODEL] * _dot(ymem, wbx_ref[...]))
    x1 = x_ref[...] + _dot(merged.astype(BF16), wout_ref[...])
    x1_ref[...] = x1
    h2 = _rms(x1, ffng_ref[...]).astype(BF16)
    h2_ref[...] = h2

    lt = _dot_nt(wr_ref[...], h2) + br_ref[...]
    gl = lt[N_EXPERTS:N_EXPERTS + N_GROUPS, :]
    gmax = jnp.max(gl, axis=0, keepdims=True)
    r4 = lax.broadcasted_iota(jnp.int32, (N_GROUPS, tm), 0).astype(F32)
    gidx = jnp.min(jnp.where(gl == gmax, r4, float(N_GROUPS)), axis=0, keepdims=True)
    pg = 1.0 / jnp.sum(jnp.exp(gl - gmax), axis=0, keepdims=True)
    esel = lt[0:EXPERTS_PER_GROUP, :]
    for g in range(1, N_GROUPS):
        esel = jnp.where(gidx == float(g), lt[g * EXPERTS_PER_GROUP:(g + 1) * EXPERTS_PER_GROUP, :], esel)
    r8 = lax.broadcasted_iota(jnp.int32, (EXPERTS_PER_GROUP, tm), 0).astype(F32)
    m1 = jnp.max(esel, axis=0, keepdims=True)
    i1 = jnp.min(jnp.where(esel == m1, r8, float(EXPERTS_PER_GROUP)), axis=0, keepdims=True)
    rest = jnp.where(r8 == i1, -jnp.inf, esel)
    m2 = jnp.max(rest, axis=0, keepdims=True)
    i2 = jnp.min(jnp.where(rest == m2, r8, float(EXPERTS_PER_GROUP)), axis=0, keepdims=True)
    e2 = jnp.exp(m2 - m1)
    den = 1.0 + e2
    wts_ref[0:1, :] = pg / den
    wts_ref[1:2, :] = pg * e2 / den
    ex1 = gidx * float(EXPERTS_PER_GROUP) + i1
    ex2 = gidx * float(EXPERTS_PER_GROUP) + i2

    r32 = lax.broadcasted_iota(jnp.int32, (N_EXPERTS, tm), 0).astype(F32)
    is1 = r32 == ex1
    is2 = r32 == ex2
    member = jnp.where(is1 | is2, 1.0, 0.0)
    upper = jnp.where(lax.broadcasted_iota(jnp.int32, (tm, tm), 0)
                      <= lax.broadcasted_iota(jnp.int32, (tm, tm), 1), 1.0, 0.0).astype(BF16)
    incl = _dot(member.astype(BF16), upper)
    run = jnp.floor((jnp.sum(member, axis=1, keepdims=True) + (RUN_ALIGN - 1)) / RUN_ALIGN) * RUN_ALIGN
    rcol = lax.broadcasted_iota(jnp.int32, (N_EXPERTS, 1), 0)
    run_start = jnp.zeros((N_EXPERTS, 1), F32)
    for e in range(N_EXPERTS - 1):
        run_start = run_start + jnp.where(rcol > e, run[e:e + 1, :], 0.0)
    pos = incl - 1.0 + run_start
    loc_ref[0:1, :] = jnp.sum(jnp.where(is1, pos, 0.0), axis=0, keepdims=True).astype(jnp.int32)
    loc_ref[1:2, :] = jnp.sum(jnp.where(is2, pos, 0.0), axis=0, keepdims=True).astype(jnp.int32)
    carry = carry_ref[...]
    total = carry + run
    cnt_tab_ref[...] = jnp.broadcast_to(run, (N_EXPERTS, LANES)).astype(jnp.int32)
    carry_tab_ref[...] = carry.astype(jnp.int32)
    carry_ref[...] = total
    counts_ref[...] = total.astype(jnp.int32)


def _merge(x2, ypool, ymla, xq, gates, kmem, vmem, wbp, wbm, wbx, wout, ffng, wr, br, *, B, S, tm, mem_len):
    T = B * S
    tps = S // tm
    row = lambda i: (i, 0)
    lane = lambda i: (0, i)
    memb = lambda i: (i // tps, 0)
    return pl.pallas_call(
        functools.partial(_merge_body, tm=tm),
        grid=(T // tm,),
        in_specs=[
            pl.BlockSpec((tm, D_MODEL), row),
            pl.BlockSpec((tm, POOL_DIM), row),
            pl.BlockSpec((tm, MLA_HEADS * V_HEAD_DIM), row),
            pl.BlockSpec((tm, XATTN_DIM), row),
            pl.BlockSpec((tm, N_BRANCHES * D_MODEL), row),
            pl.BlockSpec((mem_len, XATTN_DIM), memb),
            pl.BlockSpec((mem_len, XATTN_DIM), memb),
            _const_spec((POOL_DIM, D_MODEL)),
            _const_spec((MLA_HEADS * V_HEAD_DIM, D_MODEL)),
            _const_spec((XATTN_DIM, D_MODEL)),
            _const_spec((D_MODEL, D_MODEL)),
            _const_spec((1, D_MODEL)),
            _const_spec((ROUTER_ROWS, D_MODEL)),
            _const_spec((ROUTER_ROWS, 1)),
        ],
        out_specs=[
            pl.BlockSpec((tm, D_MODEL), row),
            pl.BlockSpec((tm, D_MODEL), row),
            pl.BlockSpec((2, tm), lane),
            pl.BlockSpec((2, tm), lane),
            pl.BlockSpec((N_EXPERTS, LANES), row),
            pl.BlockSpec((N_EXPERTS, LANES), row),
            pl.BlockSpec((N_EXPERTS, LANES), lambda i: (0, 0)),
        ],
        out_shape=[
            jax.ShapeDtypeStruct((T, D_MODEL), F32),
            jax.ShapeDtypeStruct((T, D_MODEL), BF16),
            jax.ShapeDtypeStruct((2, T), F32),
            jax.ShapeDtypeStruct((2, T), jnp.int32),
            jax.ShapeDtypeStruct((T // tm * N_EXPERTS, LANES), jnp.int32),
            jax.ShapeDtypeStruct((T // tm * N_EXPERTS, LANES), jnp.int32),
            jax.ShapeDtypeStruct((N_EXPERTS, LANES), jnp.int32),
        ],
        scratch_shapes=[pltpu.VMEM((N_EXPERTS, LANES), F32)],
        compiler_params=pltpu.CompilerParams(dimension_semantics=("arbitrary",),
                                             vmem_limit_bytes=VMEM_LIMIT_BYTES),
        name="merge",
    )(x2, ypool, ymla, xq, gates, kmem, vmem, wbp, wbm, wbx, wout, ffng, wr, br)


def _moe_pos_body(counts_ref, meta_ref):
    shift = int(math.log2(MOE_ROWS))
    cnt = counts_ref[...]
    padded = lax.shift_left(lax.shift_right_logical(cnt + (MOE_ROWS - 1), shift), shift)
    r32 = lax.broadcasted_iota(jnp.int32, (N_EXPERTS, LANES), 0)
    pad_start = jnp.zeros((N_EXPERTS, LANES), jnp.int32)
    for e in range(N_EXPERTS - 1):
        pad_start = pad_start + jnp.where(r32 > e, padded[e:e + 1, :], 0)
    pad_end = pad_start + padded

    lane = lax.broadcasted_iota(jnp.int32, (1, META_LANES), 1)
    block_row = lane * MOE_ROWS
    blk_e = jnp.zeros((1, META_LANES), jnp.int32)
    pe_row = jnp.zeros((1, META_LANES), jnp.int32)
    for e in range(N_EXPERTS):
        pe = pad_end[e:e + 1, 0:1]
        blk_e = blk_e + jnp.where(pe <= block_row, 1, 0)
        pe_row = pe_row + jnp.where(lane == META_PAD_END + e, pe, 0)
    blk_e = jnp.minimum(blk_e, N_EXPERTS - 1)
    nact = lax.shift_right_logical(pad_end[N_EXPERTS - 1:N_EXPERTS, 0:1], shift)
    meta = jnp.where(lane < META_PAD_END, blk_e, pe_row)
    meta_ref[...] = jnp.where(lane == META_NACT, nact, meta)


def _moe_pos(counts):
    full = lambda shape: pl.BlockSpec(shape, lambda i: (0,) * len(shape))
    return pl.pallas_call(
        _moe_pos_body,
        grid=(1,),
        in_specs=[full((N_EXPERTS, LANES))],
        out_specs=full((1, META_LANES)),
        out_shape=jax.ShapeDtypeStruct((1, META_LANES), jnp.int32),
        compiler_params=pltpu.CompilerParams(dimension_semantics=("arbitrary",)),
        name="moe_pos",
    )(counts)


def _pack_bf16_pairs(x):
    lo = pltpu.bitcast(x[:, 0:PACKED_DIM], U32)
    hi = pltpu.bitcast(x[:, PACKED_DIM:D_MODEL], U32)
    return hi | lax.shift_right_logical(lo, jnp.uint32(16))


def _unpack_bf16_pairs(w):
    lo = pltpu.bitcast(lax.shift_left(w, jnp.uint32(16)), F32)
    hi = pltpu.bitcast(w & jnp.uint32(0xFFFF0000), F32)
    return jnp.concatenate([lo, hi], axis=1)


def _loc_rows(tm):
    return 2 * tm + N_EXPERTS * RUN_ALIGN


def _run_copies(tile, cnt_ref, carry_ref, meta_ref, make_copy):
    def per_expert(e, local):
        n = pl.multiple_of(cnt_ref[tile * N_EXPERTS + e], RUN_ALIGN)
        start = jnp.where(e == 0, 0, meta_ref[META_PAD_END + jnp.maximum(e - 1, 0)])
        glob = pl.multiple_of(start + carry_ref[tile * N_EXPERTS + e], RUN_ALIGN)

        @pl.when(n > 0)
        def _():
            make_copy(pl.multiple_of(local, RUN_ALIGN), glob, n).start()

        return local + n

    return pl.multiple_of(lax.fori_loop(0, N_EXPERTS, per_expert, 0), RUN_ALIGN)


def _tile_rows(tile, cnt_ref):
    total = lax.fori_loop(0, N_EXPERTS, lambda e, t: t + cnt_ref[tile * N_EXPERTS + e], 0)
    return pl.multiple_of(total, RUN_ALIGN)


def _dispatch_body(meta_ref, cnt_ref, carry_ref, loc_ref, h2_ref, xs_ref, xloc_ref, zero_ref, sems, zsem,
                   *, tm, n_blocks):
    tile = pl.program_id(0)
    slot = lax.rem(tile, 2)

    def wait_rows(t, s):
        n = _tile_rows(t, cnt_ref)
        pltpu.make_async_copy(xloc_ref.at[s, pl.ds(0, n)], xs_ref.at[pl.ds(0, n)], sems.at[s]).wait()

    def pad_copy(e):
        end = pl.multiple_of(meta_ref[META_PAD_END + e], MOE_ROWS)
        return pltpu.make_async_copy(zero_ref, xs_ref.at[pl.ds(end - MOE_ROWS, MOE_ROWS)], zsem)

    def has_rows(e):
        prev = jnp.where(e == 0, 0, meta_ref[META_PAD_END + jnp.maximum(e - 1, 0)])
        return meta_ref[META_PAD_END + e] > prev

    @pl.when(tile == 0)
    def _():
        zero_ref[...] = jnp.zeros((MOE_ROWS, PACKED_DIM), U32)

        def start(e, c):
            @pl.when(has_rows(e))
            def _():
                pad_copy(e).start()
            return c

        def wait(e, c):
            @pl.when(has_rows(e))
            def _():
                pad_copy(e).wait()
            return c

        def tail_copy(b):
            return pltpu.make_async_copy(
                zero_ref, xs_ref.at[pl.ds(pl.multiple_of(b * MOE_ROWS, MOE_ROWS), MOE_ROWS)], zsem)

        def tail_start(b, c):
            tail_copy(b).start()
            return c

        def tail_wait(b, c):
            tail_copy(b).wait()
            return c

        nact = meta_ref[META_NACT]
        lax.fori_loop(0, N_EXPERTS, start, 0)
        lax.fori_loop(nact, n_blocks, tail_start, 0)
        lax.fori_loop(0, N_EXPERTS, wait, 0)
        lax.fori_loop(nact, n_blocks, tail_wait, 0)

    @pl.when(tile >= 2)
    def _():
        wait_rows(tile - 2, slot)

    r = lax.broadcasted_iota(jnp.int32, (_loc_rows(tm), tm), 0)
    onehot = jnp.where((r == loc_ref[0:1, :]) | (r == loc_ref[1:2, :]), 1.0, 0.0).astype(BF16)
    xloc_ref[slot] = _pack_bf16_pairs(_dot(onehot, h2_ref[...]))

    def make_copy(local, glob, n):
        return pltpu.make_async_copy(xloc_ref.at[slot, pl.ds(local, n)], xs_ref.at[pl.ds(glob, n)],
                                     sems.at[slot])

    _run_copies(tile, cnt_ref, carry_ref, meta_ref, make_copy)

    @pl.when(tile == pl.num_programs(0) - 1)
    def _():
        @pl.when(tile >= 1)
        def _():
            wait_rows(tile - 1, 1 - slot)

        wait_rows(tile, slot)


def _dispatch(meta1, cnt_tab, carry_tab, loc, h2, *, R, tm):
    T = h2.shape[0]
    return pl.pallas_call(
        functools.partial(_dispatch_body, tm=tm, n_blocks=R // MOE_ROWS),
        grid_spec=pltpu.PrefetchScalarGridSpec(
            num_scalar_prefetch=3,
            grid=(T // tm,),
            in_specs=[pl.BlockSpec((2, tm), lambda i, *_: (0, i)),
                      pl.BlockSpec((tm, D_MODEL), lambda i, *_: (i, 0))],
            out_specs=pl.BlockSpec(memory_space=pl.ANY),
            scratch_shapes=[pltpu.VMEM((2, _loc_rows(tm), PACKED_DIM), U32),
                            pltpu.VMEM((MOE_ROWS, PACKED_DIM), U32),
                            pltpu.SemaphoreType.DMA((2,)), pltpu.SemaphoreType.DMA],
        ),
        out_shape=jax.ShapeDtypeStruct((R, PACKED_DIM), U32),
        compiler_params=pltpu.CompilerParams(dimension_semantics=("arbitrary",),
                                             vmem_limit_bytes=VMEM_LIMIT_BYTES),
        name="dispatch",
    )(meta1, cnt_tab, carry_tab, loc, h2)


def _moe_ffn_body(meta_ref, xs_ref, wg_hbm, wu_hbm, wd_hbm, ys_ref,
                  xbuf, ybuf, wg_stage, wu_stage, wd_stage, wg_b, wu_b, wd_b, zero_ref,
                  xsem, ysem, wsem, zsem, *, n_blocks):
    nact = meta_ref[META_NACT]
    shift = int(math.log2(MOE_ROWS))

    def rows_of(b):
        return pl.ds(pl.multiple_of(b * MOE_ROWS, MOE_ROWS), MOE_ROWS)

    def x_copy(b, s):
        return pltpu.make_async_copy(xs_ref.at[rows_of(b)], xbuf.at[s], xsem.at[s])

    def y_copy(b, s):
        return pltpu.make_async_copy(ybuf.at[s], ys_ref.at[rows_of(b)], ysem.at[s])

    def w_copies(e, s):
        return (pltpu.make_async_copy(wg_hbm.at[e], wg_stage.at[s], wsem.at[s]),
                pltpu.make_async_copy(wu_hbm.at[e], wu_stage.at[s], wsem.at[s]),
                pltpu.make_async_copy(wd_hbm.at[e], wd_stage.at[s], wsem.at[s]))

    def tail_copy(b):
        return pltpu.make_async_copy(zero_ref, ys_ref.at[rows_of(b)], zsem)

    zero_ref[...] = jnp.zeros((MOE_ROWS, PACKED_DIM), U32)
    lax.fori_loop(nact, n_blocks, lambda b, c: (tail_copy(b).start(), c)[1], 0)

    x_copy(0, 0).start()
    for cp in w_copies(meta_ref[0], 0):
        cp.start()

    def block(b, s, ws_prev):
        valid = b < nact
        e = meta_ref[jnp.minimum(b, nact - 1)]
        changed = jnp.logical_and(valid, jnp.logical_or(b == 0, e != meta_ref[jnp.maximum(b - 1, 0)]))
        ws = jnp.where(changed, 1 - ws_prev, ws_prev)

        @pl.when(changed)
        def _():
            for cp in w_copies(e, ws):
                cp.wait()
            wg_b[...] = wg_stage[ws].astype(BF16)
            wu_b[...] = wu_stage[ws].astype(BF16)
            wd_b[...] = wd_stage[ws].astype(BF16)
            nxt = lax.shift_right_logical(meta_ref[META_PAD_END + e], shift)

            @pl.when(nxt < nact)
            def _():
                for cp in w_copies(meta_ref[nxt], 1 - ws):
                    cp.start()

        @pl.when(valid)
        def _():
            x_copy(b, s).wait()

            @pl.when(b + 1 < nact)
            def _():
                x_copy(b + 1, 1 - s).start()

            @pl.when(b >= 2)
            def _():
                y_copy(b - 2, s).wait()

            for c in range(MOE_ROWS // MOE_CHUNK):
                rows = slice(c * MOE_CHUNK, (c + 1) * MOE_CHUNK)
                x = _unpack_bf16_pairs(xbuf[s, rows, :]).astype(BF16)
                g = _dot(x, wg_b[...])
                a = (g * jax.nn.sigmoid(g)) * _dot(x, wu_b[...])
                y = _dot(a.astype(BF16), wd_b[...])
                ybuf[s, rows, :] = _pack_bf16_pairs(y.astype(BF16).astype(F32))
            y_copy(b, s).start()

        return ws

    def pair(i, ws):
        return block(2 * i + 1, 1, block(2 * i, 0, ws))

    lax.fori_loop(0, lax.shift_right_logical(nact + 1, 1), pair, 1)

    @pl.when(nact >= 2)
    def _():
        y_copy(nact - 2, lax.rem(nact, 2)).wait()

    y_copy(nact - 1, lax.rem(nact - 1, 2)).wait()
    lax.fori_loop(nact, n_blocks, lambda b, c: (tail_copy(b).wait(), c)[1], 0)


def _moe_ffn(meta1, xs, wg, wu, wd):
    R = xs.shape[0]
    hbm = pl.BlockSpec(memory_space=pl.ANY)
    return pl.pallas_call(
        functools.partial(_moe_ffn_body, n_blocks=R // MOE_ROWS),
        grid_spec=pltpu.PrefetchScalarGridSpec(
            num_scalar_prefetch=1,
            grid=(1,),
            in_specs=[hbm, hbm, hbm, hbm],
            out_specs=hbm,
            scratch_shapes=[pltpu.VMEM((2, MOE_ROWS, PACKED_DIM), U32),
                            pltpu.VMEM((2, MOE_ROWS, PACKED_DIM), U32),
                            pltpu.VMEM((2, D_MODEL, D_EXPERT), F32),
                            pltpu.VMEM((2, D_MODEL, D_EXPERT), F32),
                            pltpu.VMEM((2, D_EXPERT, D_MODEL), F32),
                            pltpu.VMEM((D_MODEL, D_EXPERT), BF16),
                            pltpu.VMEM((D_MODEL, D_EXPERT), BF16),
                            pltpu.VMEM((D_EXPERT, D_MODEL), BF16),
                            pltpu.VMEM((MOE_ROWS, PACKED_DIM), U32),
                            pltpu.SemaphoreType.DMA((2,)), pltpu.SemaphoreType.DMA((2,)),
                            pltpu.SemaphoreType.DMA((2,)), pltpu.SemaphoreType.DMA],
        ),
        out_shape=jax.ShapeDtypeStruct((R, PACKED_DIM), U32),
        compiler_params=pltpu.CompilerParams(dimension_semantics=("arbitrary",),
                                             vmem_limit_bytes=VMEM_LIMIT_BYTES),
        name="moe_ffn",
    )(meta1, xs, wg, wu, wd)


def _combine_body(meta_ref, cnt_ref, carry_ref, x1_ref, loc_ref, wts_ref, fg_ref, ys_ref, out_ref,
                  yloc_ref, sems, *, tm):
    tile = pl.program_id(0)
    slot = lax.rem(tile, 2)

    def fetch(t, s):
        def make_copy(local, glob, n):
            return pltpu.make_async_copy(ys_ref.at[pl.ds(glob, n)], yloc_ref.at[s, pl.ds(local, n)],
                                         sems.at[s])
        _run_copies(t, cnt_ref, carry_ref, meta_ref, make_copy)

    @pl.when(tile == 0)
    def _():
        yloc_ref[...] = jnp.zeros(yloc_ref.shape, U32)
        fetch(tile, slot)

    @pl.when(tile + 1 < pl.num_programs(0))
    def _():
        fetch(tile + 1, 1 - slot)

    n = _tile_rows(tile, cnt_ref)
    pltpu.make_async_copy(ys_ref.at[pl.ds(0, n)], yloc_ref.at[slot, pl.ds(0, n)], sems.at[slot]).wait()
    r = lax.broadcasted_iota(jnp.int32, (_loc_rows(tm), tm), 0)
    is0 = r == loc_ref[0:1, :]
    is1 = r == loc_ref[1:2, :]
    row_w = jnp.sum(jnp.where(is0, wts_ref[0:1, :], 0.0) + jnp.where(is1, wts_ref[1:2, :], 0.0),
                    axis=1, keepdims=True)
    yw = (row_w * _unpack_bf16_pairs(yloc_ref[slot])).astype(BF16)
    twohot = jnp.where(is0 | is1, 1.0, 0.0).astype(BF16)
    moe = lax.dot_general(twohot, yw, (((0,), (0,)), ((), ())), preferred_element_type=F32)
    out_ref[...] = _rms(x1_ref[...] + moe, fg_ref[...])


def _combine(meta1, cnt_tab, carry_tab, x1, loc, wts, fg, ys, *, tm):
    T = x1.shape[0]
    row = lambda i, *_: (i, 0)
    lane = lambda i, *_: (0, i)
    return pl.pallas_call(
        functools.partial(_combine_body, tm=tm),
        grid_spec=pltpu.PrefetchScalarGridSpec(
            num_scalar_prefetch=3,
            grid=(T // tm,),
            in_specs=[pl.BlockSpec((tm, D_MODEL), row),
                      pl.BlockSpec((2, tm), lane),
                      pl.BlockSpec((2, tm), lane),
                      pl.BlockSpec((1, D_MODEL), lambda i, *_: (0, 0)),
                      pl.BlockSpec(memory_space=pl.ANY)],
            out_specs=pl.BlockSpec((tm, D_MODEL), row),
            scratch_shapes=[pltpu.VMEM((2, _loc_rows(tm), PACKED_DIM), U32),
                            pltpu.SemaphoreType.DMA((2,))],
        ),
        out_shape=jax.ShapeDtypeStruct((T, D_MODEL), F32),
        compiler_params=pltpu.CompilerParams(dimension_semantics=("arbitrary",),
                                             vmem_limit_bytes=VMEM_LIMIT_BYTES),
        name="combine",
    )(meta1, cnt_tab, carry_tab, x1, loc, wts, fg, ys)


def _tile(n, t):
    t = min(n, t)
    assert n % t == 0, (n, t)
    return t


def kernel(x, mem, positions, mix_norm_g, w_in, gate_b, q_norm_g, w_uq, kv_norm_g, w_uk, w_uv, pool_w, pool_scale, mem_norm_g, w_mem_kv, w_br_pool, w_br_mla, w_br_mem, w_out, ffn_norm_g, w_router_group, b_router_group, w_router_expert, b_router_expert, w_gate_e, w_up_e, w_down_e, final_norm_g):
    B, S, D = x.shape
    assert D == D_MODEL and mix_norm_g.shape[0] == 1
    T = B * S
    mem_len = mem.shape[1]
    tm = _tile(S, 512)
    l = 0

    win_p = _pack_w_in(w_in[l])
    wuq_p = jnp.pad(w_uq[l].reshape(Q_LORA_RANK, MLA_HEADS, QK_NOPE_DIM + QK_ROPE_DIM),
                    ((0, 0), (0, 0), (0, QK_PAD_DIM - QK_NOPE_DIM - QK_ROPE_DIM))
                    ).reshape(Q_LORA_RANK, MLA_HEADS * QK_PAD_DIM).astype(BF16)
    inv_freq = 1.0 / (ROPE_THETA ** (jnp.arange(0, QK_ROPE_DIM, 2, dtype=F32) / QK_ROPE_DIM))
    invf = jnp.concatenate([inv_freq, inv_freq, jnp.zeros((LANES - QK_ROPE_DIM,), F32)])[None, :]
    wr = jnp.concatenate([w_router_expert[l], w_router_group[l],
                          jnp.zeros((D_MODEL, ROUTER_ROWS - N_EXPERTS - N_GROUPS), F32)], axis=1).T.astype(BF16)
    br = jnp.concatenate([b_router_expert[l], b_router_group[l],
                          jnp.zeros((ROUTER_ROWS - N_EXPERTS - N_GROUPS,), F32)])[:, None].astype(F32)
    x2 = x.reshape(T, D_MODEL)
    pos2 = positions.reshape(T, 1)

    ypool, xq, gates, q, k, v = _mixer_in(
        x2, pos2, invf, mix_norm_g[l][None, :], win_p, gate_b[l], q_norm_g[l][None, :], wuq_p,
        kv_norm_g[l][None, :], w_uk[l].astype(BF16), w_uv[l].astype(BF16), pool_w[l].astype(BF16),
        pool_scale[l][None, :], B=B, S=S, tm=tm)
    kmem, vmem = _mem_kv(mem.reshape(B * mem_len, D_MODEL), mem_norm_g[l][None, :], w_mem_kv[l].astype(BF16))
    ymla = _mla_attn_unrolled(q, k, v, tq=tm).reshape(T, MLA_HEADS * V_HEAD_DIM)
    x1, h2, wts, loc, cnt_tab, carry_tab, counts = _merge(
        x2, ypool, ymla, xq, gates, kmem, vmem, w_br_pool[l].astype(BF16), w_br_mla[l].astype(BF16),
        w_br_mem[l].astype(BF16), w_out[l].astype(BF16), ffn_norm_g[l][None, :], wr, br,
        B=B, S=S, tm=tm, mem_len=mem_len)

    R = 2 * T + (T // tm) * N_EXPERTS * RUN_ALIGN + N_EXPERTS * MOE_ROWS
    assert R % MOE_ROWS == 0 and R // MOE_ROWS <= META_PAD_END
    meta1 = _moe_pos(counts).reshape(META_LANES)
    cnt1 = cnt_tab[:, 0]
    carry1 = carry_tab[:, 0]
    xs = _dispatch(meta1, cnt1, carry1, loc, h2, R=R, tm=tm)
    ys = _moe_ffn(meta1, xs, w_gate_e[l], w_up_e[l], w_down_e[l])
    out = _combine(meta1, cnt1, carry1, x1, loc, wts, final_norm_g[None, :], ys, tm=tm)
    return out.reshape(B, S, D_MODEL)
```

```python
import functools
import math

import jax
import jax.numpy as jnp
from jax import lax
from jax.experimental import pallas as pl
from jax.experimental.pallas import tpu as pltpu

D_MODEL = 1024
POOL_WINDOWS = (2, 4, 8, 16)
POOL_GROUP_DIM = 128
POOL_DIM = 512
MLA_HEADS = 8
QK_NOPE_DIM = 128
QK_ROPE_DIM = 64
V_HEAD_DIM = 128
Q_LORA_RANK = 384
KV_LORA_RANK = 256
ROPE_THETA = 10000.0
XATTN_HEADS = 4
XATTN_HEAD_DIM = 128
XATTN_DIM = 512
N_BRANCHES = 3
N_GROUPS = 4
EXPERTS_PER_GROUP = 8
N_EXPERTS = 32
D_EXPERT = 256
RMS_EPS = 1e-6
NEG_INF = -1e30

LANES = 128
QK_PAD_DIM = 2 * LANES
POOL_HALO = 16
MOE_ROWS = 512
MOE_CHUNK = 256
X_SLOTS = 4
W_STAGES = 3
RUN_ALIGN = 8
PACKED_DIM = D_MODEL // 2
ROUTER_ROWS = 40
META_LANES = 256
META_PAD_END = 192
META_NACT = 255
VMEM_LIMIT_BYTES = 56 * 1024 * 1024

IN_POOL, IN_QD, IN_KV, IN_XQ, IN_GATE, IN_KR, IN_END = 0, 512, 896, 1152, 1664, 4736, 4864
W_IN_KR, W_IN_XQ, W_IN_END = 1152, 1216, 4800

F32 = jnp.float32
BF16 = jnp.bfloat16
U32 = jnp.uint32


def _rms(x, g):
    ms = jnp.mean(x * x, axis=-1, keepdims=True)
    return (x * lax.rsqrt(ms + RMS_EPS)) * g


def _dot(a, b):
    return jnp.dot(a, b, preferred_element_type=F32)


def _dot_nt(a, b):
    return lax.dot_general(a, b, (((1,), (1,)), ((), ())), preferred_element_type=F32)


def _const_spec(shape):
    nd = len(shape)
    return pl.BlockSpec(shape, lambda *_: (0,) * nd, pipeline_mode=pl.Buffered(1))


def _pack_w_in_body(w_ref, o_ref):
    w = w_ref[...]
    rows = w.shape[0]
    o_ref[...] = jnp.concatenate(
        [w[:, 0:W_IN_KR], w[:, W_IN_XQ:W_IN_END], w[:, W_IN_KR:W_IN_XQ],
         jnp.zeros((rows, LANES - QK_ROPE_DIM), w.dtype)], axis=1).astype(BF16)


def _pack_w_in(w):
    tr = 128
    return pl.pallas_call(
        _pack_w_in_body,
        grid=(D_MODEL // tr,),
        in_specs=[pl.BlockSpec((tr, W_IN_END), lambda i: (i, 0))],
        out_specs=pl.BlockSpec((tr, IN_END), lambda i: (i, 0)),
        out_shape=jax.ShapeDtypeStruct((D_MODEL, IN_END), BF16),
        compiler_params=pltpu.CompilerParams(dimension_semantics=("arbitrary",)),
        name="pack_w_in",
    )(w)


def _mixer_in_body(x_ref, pos_ref, invf_ref, mixg_ref, win_ref, gateb_ref, qg_ref, wuq_ref,
                   kvg_ref, wuk_ref, wuv_ref, poolw_ref, pools_ref,
                   ypool_ref, xq_ref, gates_ref, q_ref, k_ref, v_ref, ext_ref,
                   *, tm, tiles_per_seq, q_scale):
    si = lax.rem(pl.program_id(0), tiles_per_seq)

    @pl.when(pl.program_id(0) == 0)
    def _():
        ext_ref[0:POOL_HALO, :] = jnp.zeros((POOL_HALO, POOL_DIM), F32)

    hb = _rms(x_ref[...], mixg_ref[...]).astype(BF16)

    u = _dot(hb, win_ref[:, IN_POOL:IN_QD])
    ext_ref[0:POOL_HALO, :] = jnp.where(si == 0, 0.0, ext_ref[0:POOL_HALO, :])
    ext_ref[POOL_HALO:POOL_HALO + tm, :] = u

    for c in range(N_BRANCHES):
        gl = _dot(hb, win_ref[:, IN_GATE + c * D_MODEL:IN_GATE + (c + 1) * D_MODEL])
        gates_ref[:, c * D_MODEL:(c + 1) * D_MODEL] = jax.nn.sigmoid(gl + gateb_ref[c:c + 1, :]).astype(BF16)

    ang = pos_ref[...].astype(F32) * invf_ref[...]
    cos = jnp.cos(ang)
    sin = jnp.sin(ang)
    first_half = lax.broadcasted_iota(jnp.int32, (tm, LANES), 1) < (QK_ROPE_DIM // 2)
    sin_signed = jnp.where(first_half, -sin, sin)

    def rope(r):
        swapped = jnp.where(first_half, pltpu.roll(r, LANES - QK_ROPE_DIM // 2, 1),
                            pltpu.roll(r, QK_ROPE_DIM // 2, 1))
        return r * cos + swapped * sin_signed

    cq = _rms(_dot(hb, win_ref[:, IN_QD:IN_KV]), qg_ref[...]).astype(BF16)
    for h in range(MLA_HEADS):
        qh = _dot(cq, wuq_ref[:, h * QK_PAD_DIM:(h + 1) * QK_PAD_DIM])
        q_ref[0, h, :, 0:LANES] = (qh[:, 0:LANES] * q_scale).astype(BF16)
        q_ref[0, h, :, LANES:QK_PAD_DIM] = (rope(qh[:, LANES:QK_PAD_DIM]) * q_scale).astype(BF16)

    ckv = _rms(_dot(hb, win_ref[:, IN_KV:IN_XQ]), kvg_ref[...]).astype(BF16)
    kr = rope(_dot(hb, win_ref[:, IN_KR:IN_END])).astype(BF16)
    for hp in range(MLA_HEADS // 2):
        cols = slice(hp * 2 * LANES, (hp + 1) * 2 * LANES)
        kn = _dot(ckv, wuk_ref[:, cols]).astype(BF16)
        vv = _dot(ckv, wuv_ref[:, cols]).astype(BF16)
        for j in range(2):
            h = 2 * hp + j
            k_ref[0, h, :, 0:LANES] = kn[:, j * LANES:(j + 1) * LANES]
            k_ref[0, h, :, LANES:QK_PAD_DIM] = kr
            v_ref[0, h] = vv[:, j * LANES:(j + 1) * LANES]

    t_seq = lax.broadcasted_iota(jnp.int32, (tm, 1), 0) + si * tm
    for g, w in enumerate(POOL_WINDOWS):
        lo = g * POOL_GROUP_DIM
        hi = lo + POOL_GROUP_DIM
        acc = u[:, lo:hi]
        for j in range(1, w):
            acc = acc + ext_ref[POOL_HALO - j:POOL_HALO - j + tm, lo:hi]
        cnt = jnp.minimum(t_seq + 1, w).astype(F32)
        p = acc / cnt - u[:, lo:hi]
        y = _dot(p.astype(BF16), poolw_ref[g]) * pools_ref[:, lo:hi]
        ypool_ref[:, lo:hi] = y.astype(BF16)
    ext_ref[0:POOL_HALO, :] = ext_ref[tm:tm + POOL_HALO, :]

    xq_ref[...] = _dot(hb, win_ref[:, IN_XQ:IN_GATE]).astype(BF16)


def _mixer_in(x2, pos2, invf, mixg, win_p, gate_b, qg, wuq_p, kvg, wuk, wuv, pool_w, pool_s, *, B, S, tm):
    T = B * S
    tps = S // tm
    q_scale = (QK_NOPE_DIM + QK_ROPE_DIM) ** -0.5 * math.log2(math.e)
    body = functools.partial(_mixer_in_body, tm=tm, tiles_per_seq=tps, q_scale=q_scale)
    row = lambda i: (i, 0)
    head = lambda i: (i // tps, 0, i % tps, 0)
    return pl.pallas_call(
        body,
        grid=(T // tm,),
        in_specs=[
            pl.BlockSpec((tm, D_MODEL), row),
            pl.BlockSpec((tm, 1), row),
            _const_spec((1, LANES)),
            _const_spec((1, D_MODEL)),
            _const_spec((D_MODEL, IN_END)),
            _const_spec((N_BRANCHES, D_MODEL)),
            _const_spec((1, Q_LORA_RANK)),
            _const_spec((Q_LORA_RANK, MLA_HEADS * QK_PAD_DIM)),
            _const_spec((1, KV_LORA_RANK)),
            _const_spec((KV_LORA_RANK, MLA_HEADS * QK_NOPE_DIM)),
            _const_spec((KV_LORA_RANK, MLA_HEADS * V_HEAD_DIM)),
            _const_spec((len(POOL_WINDOWS), POOL_GROUP_DIM, POOL_GROUP_DIM)),
            _const_spec((1, POOL_DIM)),
        ],
        out_specs=[
            pl.BlockSpec((tm, POOL_DIM), row),
            pl.BlockSpec((tm, XATTN_DIM), row),
            pl.BlockSpec((tm, N_BRANCHES * D_MODEL), row),
            pl.BlockSpec((1, MLA_HEADS, tm, QK_PAD_DIM), head),
            pl.BlockSpec((1, MLA_HEADS, tm, QK_PAD_DIM), head),
            pl.BlockSpec((1, MLA_HEADS, tm, V_HEAD_DIM), head),
        ],
        out_shape=[
            jax.ShapeDtypeStruct((T, POOL_DIM), BF16),
            jax.ShapeDtypeStruct((T, XATTN_DIM), BF16),
            jax.ShapeDtypeStruct((T, N_BRANCHES * D_MODEL), BF16),
            jax.ShapeDtypeStruct((B, MLA_HEADS, S, QK_PAD_DIM), BF16),
            jax.ShapeDtypeStruct((B, MLA_HEADS, S, QK_PAD_DIM), BF16),
            jax.ShapeDtypeStruct((B, MLA_HEADS, S, V_HEAD_DIM), BF16),
        ],
        scratch_shapes=[pltpu.VMEM((tm + POOL_HALO, POOL_DIM), F32)],
        compiler_params=pltpu.CompilerParams(dimension_semantics=("arbitrary",),
                                             vmem_limit_bytes=VMEM_LIMIT_BYTES),
        name="mixer_in",
    )(x2, pos2, invf, mixg, win_p, gate_b, qg, wuq_p, kvg, wuk, wuv, pool_w, pool_s)


def _mem_kv_body(mem_ref, g_ref, w_ref, k_ref, v_ref):
    mb = _rms(mem_ref[...], g_ref[...]).astype(BF16)
    kv = _dot(mb, w_ref[...])
    k_ref[...] = kv[:, 0:XATTN_DIM].astype(BF16)
    v_ref[...] = kv[:, XATTN_DIM:2 * XATTN_DIM].astype(BF16)


def _mem_kv(mem2, g, w):
    rows = mem2.shape[0]
    tr = min(rows, 512)
    return pl.pallas_call(
        _mem_kv_body,
        grid=(rows // tr,),
        in_specs=[pl.BlockSpec((tr, D_MODEL), lambda i: (i, 0)),
                  _const_spec((1, D_MODEL)),
                  _const_spec((D_MODEL, 2 * XATTN_DIM))],
        out_specs=[pl.BlockSpec((tr, XATTN_DIM), lambda i: (i, 0)),
                   pl.BlockSpec((tr, XATTN_DIM), lambda i: (i, 0))],
        out_shape=[jax.ShapeDtypeStruct((rows, XATTN_DIM), BF16),
                   jax.ShapeDtypeStruct((rows, XATTN_DIM), BF16)],
        compiler_params=pltpu.CompilerParams(dimension_semantics=("arbitrary",)),
        name="mem_kv",
    )(mem2, g, w)


def _attn_unrolled_body(q_ref, k_ref, v_ref, o_ref, s_a, s_b, mc_a, mc_b, m_ref, l_ref, acc_ref, *, nq, tq):
    s_bufs = (s_a, s_b)
    mc_bufs = (mc_a, mc_b)
    mxu_row_split = 2
    blocks = [(qi, kb) for qi in range(nq) for kb in range(qi + 1)]

    def scores(i, slot):
        qi, kb = blocks[i]
        s = _dot_nt(q_ref[0, 0, qi * tq:(qi + 1) * tq, :], k_ref[0, 0, kb * tq:(kb + 1) * tq, :])
        if qi == kb:
            ri = lax.broadcasted_iota(jnp.int32, (tq, tq), 0)
            ci = lax.broadcasted_iota(jnp.int32, (tq, tq), 1)
            s = jnp.where(ci <= ri, s, NEG_INF)
        s_bufs[slot][...] = s
        mc_bufs[slot][...] = jnp.broadcast_to(jnp.max(s, axis=1, keepdims=True), (tq, LANES))

    def accumulate(i, slot):
        qi, kb = blocks[i]
        is_first = kb == 0
        is_last = kb == qi
        if is_first:
            m_new = mc_bufs[slot][...]
        else:
            m_prev = m_ref[...]
            m_new = jnp.maximum(m_prev, mc_bufs[slot][...])
            alpha = jnp.exp2(m_prev - m_new)
        p = jnp.exp2(s_bufs[slot][...] - jnp.concatenate([m_new] * (tq // LANES), axis=1))
        psum = p[:, 0:LANES]
        for c in range(1, tq // LANES):
            psum = psum + p[:, c * LANES:(c + 1) * LANES]
        l_new = psum if is_first else alpha * l_ref[...] + psum
        pb = p.astype(BF16)
        v = v_ref[0, 0, kb * tq:(kb + 1) * tq, :]
        if is_last:
            inv = 1.0 / jnp.sum(l_new, axis=1, keepdims=True)
        else:
            l_ref[...] = l_new
            m_ref[...] = m_new
        h = tq // mxu_row_split
        for r in range(mxu_row_split):
            rows = slice(r * h, (r + 1) * h)
            acc = _dot(pb[rows, :], v)
            if not is_first:
                acc = alpha[rows, :] * acc_ref[rows, :] + acc
            if is_last:
                o_ref[0, qi * tq + r * h:qi * tq + (r + 1) * h, :] = (acc * inv[rows, :]).astype(BF16)
            else:
                acc_ref[rows, :] = acc

    scores(0, 0)
    for i in range(len(blocks)):
        if i + 1 < len(blocks):
            scores(i + 1, (i + 1) % 2)
        accumulate(i, i % 2)


def _mla_attn_unrolled(q, k, v, *, tq):
    B, H, S, _ = q.shape
    per_head = lambda b, h: (b, h, 0, 0)
    return pl.pallas_call(
        functools.partial(_attn_unrolled_body, nq=S // tq, tq=tq),
        grid=(B, H),
        in_specs=[pl.BlockSpec((1, 1, S, QK_PAD_DIM), per_head),
                  pl.BlockSpec((1, 1, S, QK_PAD_DIM), per_head),
                  pl.BlockSpec((1, 1, S, V_HEAD_DIM), per_head)],
        out_specs=pl.BlockSpec((1, S, V_HEAD_DIM), lambda b, h: (b, 0, h)),
        out_shape=jax.ShapeDtypeStruct((B, S, H * V_HEAD_DIM), BF16),
        scratch_shapes=[pltpu.VMEM((tq, tq), F32), pltpu.VMEM((tq, tq), F32),
                        pltpu.VMEM((tq, LANES), F32), pltpu.VMEM((tq, LANES), F32),
                        pltpu.VMEM((tq, LANES), F32), pltpu.VMEM((tq, LANES), F32),
                        pltpu.VMEM((tq, V_HEAD_DIM), F32)],
        compiler_params=pltpu.CompilerParams(dimension_semantics=("arbitrary", "arbitrary"),
                                             vmem_limit_bytes=VMEM_LIMIT_BYTES),
        name="mla_attn",
    )(q, k, v)


def _merge_body(x_ref, ypool_ref, ymla_ref, xq_ref, gates_ref, kmem_ref, vmem_ref,
                wbp_ref, wbm_ref, wbx_ref, wout_ref, ffng_ref, wr_ref, br_ref,
                x1_ref, h2_ref, wts_ref, loc_ref, cnt_tab_ref, carry_tab_ref, counts_ref, carry_ref, *, tm):
    @pl.when(pl.program_id(0) == 0)
    def _():
        carry_ref[...] = jnp.zeros((N_EXPERTS, LANES), F32)

    xq = xq_ref[...]
    parts = []
    for h in range(XATTN_HEADS):
        cols = slice(h * XATTN_HEAD_DIM, (h + 1) * XATTN_HEAD_DIM)
        s = _dot_nt(xq[:, cols], kmem_ref[:, cols]) * (XATTN_HEAD_DIM ** -0.5)
        e = jnp.exp(s - jnp.max(s, axis=1, keepdims=True))
        p = e / jnp.sum(e, axis=1, keepdims=True)
        parts.append(_dot(p.astype(BF16), vmem_ref[:, cols]))
    ymem = jnp.concatenate(parts, axis=1).astype(BF16)

    gates = gates_ref[...].astype(F32)
    merged = (gates[:, 0:D_MODEL] * _dot(ypool_ref[...], wbp_ref[...])
              + gates[:, D_MODEL:2 * D_MODEL] * _dot(ymla_ref[...], wbm_ref[...])
              + gates[:, 2 * D_MODEL:3 * D_MODEL] * _dot(ymem, wbx_ref[...]))
    x1 = x_ref[...] + _dot(merged.astype(BF16), wout_ref[...])
    x1_ref[...] = x1
    h2 = _rms(x1, ffng_ref[...]).astype(BF16)
    h2_ref[...] = h2

    lt = _dot_nt(wr_ref[...], h2) + br_ref[...]
    gl = lt[N_EXPERTS:N_EXPERTS + N_GROUPS, :]
    gmax = jnp.max(gl, axis=0, keepdims=True)
    r4 = lax.broadcasted_iota(jnp.int32, (N_GROUPS, tm), 0).astype(F32)
    gidx = jnp.min(jnp.where(gl == gmax, r4, float(N_GROUPS)), axis=0, keepdims=True)
    pg = 1.0 / jnp.sum(jnp.exp(gl - gmax), axis=0, keepdims=True)
    esel = lt[0:EXPERTS_PER_GROUP, :]
    for g in range(1, N_GROUPS):
        esel = jnp.where(gidx == float(g), lt[g * EXPERTS_PER_GROUP:(g + 1) * EXPERTS_PER_GROUP, :], esel)
    r8 = lax.broadcasted_iota(jnp.int32, (EXPERTS_PER_GROUP, tm), 0).astype(F32)
    m1 = jnp.max(esel, axis=0, keepdims=True)
    i1 = jnp.min(jnp.where(esel == m1, r8, float(EXPERTS_PER_GROUP)), axis=0, keepdims=True)
    rest = jnp.where(r8 == i1, -jnp.inf, esel)
    m2 = jnp.max(rest, axis=0, keepdims=True)
    i2 = jnp.min(jnp.where(rest == m2, r8, float(EXPERTS_PER_GROUP)), axis=0, keepdims=True)
    e2 = jnp.exp(m2 - m1)
    den = 1.0 + e2
    wts_ref[0:1, :] = pg / den
    wts_ref[1:2, :] = pg * e2 / den
    ex1 = gidx * float(EXPERTS_PER_GROUP) + i1
    ex2 = gidx * float(EXPERTS_PER_GROUP) + i2

    r32 = lax.broadcasted_iota(jnp.int32, (N_EXPERTS, tm), 0).astype(F32)
    is1 = r32 == ex1
    is2 = r32 == ex2
    member = jnp.where(is1 | is2, 1.0, 0.0)
    upper = jnp.where(lax.broadcasted_iota(jnp.int32, (tm, tm), 0)
                      <= lax.broadcasted_iota(jnp.int32, (tm, tm), 1), 1.0, 0.0).astype(BF16)
    incl = _dot(member.astype(BF16), upper)
    run = jnp.floor((jnp.sum(member, axis=1, keepdims=True) + (RUN_ALIGN - 1)) / RUN_ALIGN) * RUN_ALIGN
    rcol = lax.broadcasted_iota(jnp.int32, (N_EXPERTS, 1), 0)
    run_start = jnp.zeros((N_EXPERTS, 1), F32)
    for e in range(N_EXPERTS - 1):
        run_start = run_start + jnp.where(rcol > e, run[e:e + 1, :], 0.0)
    pos = incl - 1.0 + run_start
    loc_ref[0:1, :] = jnp.sum(jnp.where(is1, pos, 0.0), axis=0, keepdims=True).astype(jnp.int32)
    loc_ref[1:2, :] = jnp.sum(jnp.where(is2, pos, 0.0), axis=0, keepdims=True).astype(jnp.int32)
    carry = carry_ref[...]
    total = carry + run
    cnt_tab_ref[...] = jnp.broadcast_to(run, (N_EXPERTS, LANES)).astype(jnp.int32)
    carry_tab_ref[...] = carry.astype(jnp.int32)
    carry_ref[...] = total
    counts_ref[...] = total.astype(jnp.int32)


def _merge(x2, ypool, ymla, xq, gates, kmem, vmem, wbp, wbm, wbx, wout, ffng, wr, br, *, B, S, tm, mem_len):
    T = B * S
    tps = S // tm
    row = lambda i: (i, 0)
    lane = lambda i: (0, i)
    memb = lambda i: (i // tps, 0)
    return pl.pallas_call(
        functools.partial(_merge_body, tm=tm),
        grid=(T // tm,),
        in_specs=[
            pl.BlockSpec((tm, D_MODEL), row),
            pl.BlockSpec((tm, POOL_DIM), row),
            pl.BlockSpec((tm, MLA_HEADS * V_HEAD_DIM), row),
            pl.BlockSpec((tm, XATTN_DIM), row),
            pl.BlockSpec((tm, N_BRANCHES * D_MODEL), row),
            pl.BlockSpec((mem_len, XATTN_DIM), memb),
            pl.BlockSpec((mem_len, XATTN_DIM), memb),
            _const_spec((POOL_DIM, D_MODEL)),
            _const_spec((MLA_HEADS * V_HEAD_DIM, D_MODEL)),
            _const_spec((XATTN_DIM, D_MODEL)),
            _const_spec((D_MODEL, D_MODEL)),
            _const_spec((1, D_MODEL)),
            _const_spec((ROUTER_ROWS, D_MODEL)),
            _const_spec((ROUTER_ROWS, 1)),
        ],
        out_specs=[
            pl.BlockSpec((tm, D_MODEL), row),
            pl.BlockSpec((tm, D_MODEL), row),
            pl.BlockSpec((2, tm), lane),
            pl.BlockSpec((2, tm), lane),
            pl.BlockSpec((N_EXPERTS, LANES), row),
            pl.BlockSpec((N_EXPERTS, LANES), row),
            pl.BlockSpec((N_EXPERTS, LANES), lambda i: (0, 0)),
        ],
        out_shape=[
            jax.ShapeDtypeStruct((T, D_MODEL), F32),
            jax.ShapeDtypeStruct((T, D_MODEL), BF16),
            jax.ShapeDtypeStruct((2, T), F32),
            jax.ShapeDtypeStruct((2, T), jnp.int32),
            jax.ShapeDtypeStruct((T // tm * N_EXPERTS, LANES), jnp.int32),
            jax.ShapeDtypeStruct((T // tm * N_EXPERTS, LANES), jnp.int32),
            jax.ShapeDtypeStruct((N_EXPERTS, LANES), jnp.int32),
        ],
        scratch_shapes=[pltpu.VMEM((N_EXPERTS, LANES), F32)],
        compiler_params=pltpu.CompilerParams(dimension_semantics=("arbitrary",),
                                             vmem_limit_bytes=VMEM_LIMIT_BYTES),
        name="merge",
    )(x2, ypool, ymla, xq, gates, kmem, vmem, wbp, wbm, wbx, wout, ffng, wr, br)


def _moe_pos_body(counts_ref, meta_ref):
    shift = int(math.log2(MOE_ROWS))
    cnt = counts_ref[...]
    padded = lax.shift_left(lax.shift_right_logical(cnt + (MOE_ROWS - 1), shift), shift)
    r32 = lax.broadcasted_iota(jnp.int32, (N_EXPERTS, LANES), 0)
    pad_start = jnp.zeros((N_EXPERTS, LANES), jnp.int32)
    for e in range(N_EXPERTS - 1):
        pad_start = pad_start + jnp.where(r32 > e, padded[e:e + 1, :], 0)
    pad_end = pad_start + padded

    lane = lax.broadcasted_iota(jnp.int32, (1, META_LANES), 1)
    block_row = lane * MOE_ROWS
    blk_e = jnp.zeros((1, META_LANES), jnp.int32)
    pe_row = jnp.zeros((1, META_LANES), jnp.int32)
    for e in range(N_EXPERTS):
        pe = pad_end[e:e + 1, 0:1]
        blk_e = blk_e + jnp.where(pe <= block_row, 1, 0)
        pe_row = pe_row + jnp.where(lane == META_PAD_END + e, pe, 0)
    blk_e = jnp.minimum(blk_e, N_EXPERTS - 1)
    nact = lax.shift_right_logical(pad_end[N_EXPERTS - 1:N_EXPERTS, 0:1], shift)
    meta = jnp.where(lane < META_PAD_END, blk_e, pe_row)
    meta_ref[...] = jnp.where(lane == META_NACT, nact, meta)


def _moe_pos(counts):
    full = lambda shape: pl.BlockSpec(shape, lambda i: (0,) * len(shape))
    return pl.pallas_call(
        _moe_pos_body,
        grid=(1,),
        in_specs=[full((N_EXPERTS, LANES))],
        out_specs=full((1, META_LANES)),
        out_shape=jax.ShapeDtypeStruct((1, META_LANES), jnp.int32),
        compiler_params=pltpu.CompilerParams(dimension_semantics=("arbitrary",)),
        name="moe_pos",
    )(counts)


def _pack_bf16_pairs(x):
    lo = pltpu.bitcast(x[:, 0:PACKED_DIM], U32)
    hi = pltpu.bitcast(x[:, PACKED_DIM:D_MODEL], U32)
    return hi | lax.shift_right_logical(lo, jnp.uint32(16))


def _unpack_bf16_pairs(w):
    lo = pltpu.bitcast(lax.shift_left(w, jnp.uint32(16)), F32)
    hi = pltpu.bitcast(w & jnp.uint32(0xFFFF0000), F32)
    return jnp.concatenate([lo, hi], axis=1)


def _loc_rows(tm):
    return 2 * tm + N_EXPERTS * RUN_ALIGN


def _run_copies(tile, cnt_ref, carry_ref, meta_ref, make_copy):
    def per_expert(e, local):
        n = pl.multiple_of(cnt_ref[tile * N_EXPERTS + e], RUN_ALIGN)
        start = jnp.where(e == 0, 0, meta_ref[META_PAD_END + jnp.maximum(e - 1, 0)])
        glob = pl.multiple_of(start + carry_ref[tile * N_EXPERTS + e], RUN_ALIGN)

        @pl.when(n > 0)
        def _():
            make_copy(pl.multiple_of(local, RUN_ALIGN), glob, n).start()

        return local + n

    return pl.multiple_of(lax.fori_loop(0, N_EXPERTS, per_expert, 0), RUN_ALIGN)


def _tile_rows(tile, cnt_ref):
    total = lax.fori_loop(0, N_EXPERTS, lambda e, t: t + cnt_ref[tile * N_EXPERTS + e], 0)
    return pl.multiple_of(total, RUN_ALIGN)


def _dispatch_body(meta_ref, cnt_ref, carry_ref, loc_ref, h2_ref, xs_ref, xloc_ref, zero_ref, sems, zsem,
                   *, tm, n_blocks):
    tile = pl.program_id(0)
    slot = lax.rem(tile, 2)

    def wait_rows(t, s):
        n = _tile_rows(t, cnt_ref)
        pltpu.make_async_copy(xloc_ref.at[s, pl.ds(0, n)], xs_ref.at[pl.ds(0, n)], sems.at[s]).wait()

    def pad_copy(e):
        end = pl.multiple_of(meta_ref[META_PAD_END + e], MOE_ROWS)
        return pltpu.make_async_copy(zero_ref, xs_ref.at[pl.ds(end - MOE_ROWS, MOE_ROWS)], zsem)

    def has_rows(e):
        prev = jnp.where(e == 0, 0, meta_ref[META_PAD_END + jnp.maximum(e - 1, 0)])
        return meta_ref[META_PAD_END + e] > prev

    @pl.when(tile == 0)
    def _():
        zero_ref[...] = jnp.zeros((MOE_ROWS, PACKED_DIM), U32)

        def start(e, c):
            @pl.when(has_rows(e))
            def _():
                pad_copy(e).start()
            return c

        def wait(e, c):
            @pl.when(has_rows(e))
            def _():
                pad_copy(e).wait()
            return c

        def tail_copy(b):
            return pltpu.make_async_copy(
                zero_ref, xs_ref.at[pl.ds(pl.multiple_of(b * MOE_ROWS, MOE_ROWS), MOE_ROWS)], zsem)

        def tail_start(b, c):
            tail_copy(b).start()
            return c

        def tail_wait(b, c):
            tail_copy(b).wait()
            return c

        nact = meta_ref[META_NACT]
        lax.fori_loop(0, N_EXPERTS, start, 0)
        lax.fori_loop(nact, n_blocks, tail_start, 0)
        lax.fori_loop(0, N_EXPERTS, wait, 0)
        lax.fori_loop(nact, n_blocks, tail_wait, 0)

    @pl.when(tile >= 2)
    def _():
        wait_rows(tile - 2, slot)

    r = lax.broadcasted_iota(jnp.int32, (_loc_rows(tm), tm), 0)
    onehot = jnp.where((r == loc_ref[0:1, :]) | (r == loc_ref[1:2, :]), 1.0, 0.0).astype(BF16)
    xloc_ref[slot] = _pack_bf16_pairs(_dot(onehot, h2_ref[...]))

    def make_copy(local, glob, n):
        return pltpu.make_async_copy(xloc_ref.at[slot, pl.ds(local, n)], xs_ref.at[pl.ds(glob, n)],
                                     sems.at[slot])

    _run_copies(tile, cnt_ref, carry_ref, meta_ref, make_copy)

    @pl.when(tile == pl.num_programs(0) - 1)
    def _():
        @pl.when(tile >= 1)
        def _():
            wait_rows(tile - 1, 1 - slot)

        wait_rows(tile, slot)


def _dispatch(meta1, cnt_tab, carry_tab, loc, h2, *, R, tm):
    T = h2.shape[0]
    return pl.pallas_call(
        functools.partial(_dispatch_body, tm=tm, n_blocks=R // MOE_ROWS),
        grid_spec=pltpu.PrefetchScalarGridSpec(
            num_scalar_prefetch=3,
            grid=(T // tm,),
            in_specs=[pl.BlockSpec((2, tm), lambda i, *_: (0, i)),
                      pl.BlockSpec((tm, D_MODEL), lambda i, *_: (i, 0))],
            out_specs=pl.BlockSpec(memory_space=pl.ANY),
            scratch_shapes=[pltpu.VMEM((2, _loc_rows(tm), PACKED_DIM), U32),
                            pltpu.VMEM((MOE_ROWS, PACKED_DIM), U32),
                            pltpu.SemaphoreType.DMA((2,)), pltpu.SemaphoreType.DMA],
        ),
        out_shape=jax.ShapeDtypeStruct((R, PACKED_DIM), U32),
        compiler_params=pltpu.CompilerParams(dimension_semantics=("arbitrary",),
                                             vmem_limit_bytes=VMEM_LIMIT_BYTES),
        name="dispatch",
    )(meta1, cnt_tab, carry_tab, loc, h2)


def _moe_ffn_body(meta_ref, xs_ref, wg_hbm, wu_hbm, wd_hbm, ys_ref,
                  xbuf, ybuf, wg_stage, wu_stage, wd_stage, wg_b, wu_b, wd_b, zero_ref,
                  xsem, ysem, wsem, zsem, *, n_blocks):
    nact = meta_ref[META_NACT]
    shift = int(math.log2(MOE_ROWS))

    def rows_of(b):
        return pl.ds(pl.multiple_of(b * MOE_ROWS, MOE_ROWS), MOE_ROWS)

    def x_copy(b, s):
        return pltpu.make_async_copy(xs_ref.at[rows_of(b)], xbuf.at[s], xsem.at[s])

    def y_copy(b, s):
        return pltpu.make_async_copy(ybuf.at[s], ys_ref.at[rows_of(b)], ysem.at[s])

    def w_copies(e, s):
        return (pltpu.make_async_copy(wg_hbm.at[e], wg_stage.at[s], wsem.at[s]),
                pltpu.make_async_copy(wu_hbm.at[e], wu_stage.at[s], wsem.at[s]),
                pltpu.make_async_copy(wd_hbm.at[e], wd_stage.at[s], wsem.at[s]))

    def tail_copy(b):
        return pltpu.make_async_copy(zero_ref, ys_ref.at[rows_of(b)], zsem)

    zero_ref[...] = jnp.zeros((MOE_ROWS, PACKED_DIM), U32)
    lax.fori_loop(nact, n_blocks, lambda b, c: (tail_copy(b).start(), c)[1], 0)

    def next_expert_block(e):
        return lax.shift_right_logical(meta_ref[META_PAD_END + e], shift)

    def start_weights(b, s):
        @pl.when(b < nact)
        def _():
            for cp in w_copies(meta_ref[jnp.minimum(b, n_blocks - 1)], s):
                cp.start()

    x_copy(0, 0).start()

    @pl.when(nact > 1)
    def _():
        x_copy(1, 1).start()

    e_first = meta_ref[0]
    start_weights(0, 0)
    start_weights(next_expert_block(e_first), 1)

    def block(b, xs, ys, k_prev):
        valid = b < nact
        e = meta_ref[jnp.minimum(b, nact - 1)]
        changed = jnp.logical_and(valid, jnp.logical_or(b == 0, e != meta_ref[jnp.maximum(b - 1, 0)]))
        k = jnp.where(changed, k_prev + 1, k_prev)

        @pl.when(changed)
        def _():
            ws = lax.rem(k, W_STAGES)
            for cp in w_copies(e, ws):
                cp.wait()
            wg_b[...] = wg_stage[ws].astype(BF16)
            wu_b[...] = wu_stage[ws].astype(BF16)
            wd_b[...] = wd_stage[ws].astype(BF16)
            n1 = next_expert_block(e)
            e1 = meta_ref[jnp.minimum(n1, n_blocks - 1)]
            n2 = jnp.where(n1 < nact, next_expert_block(e1), n_blocks)
            start_weights(n2, lax.rem(k + 2, W_STAGES))

        @pl.when(valid)
        def _():
            x_copy(b, xs).wait()

            @pl.when(b + 2 < nact)
            def _():
                x_copy(b + 2, (xs + 2) % X_SLOTS).start()

            @pl.when(b >= 2)
            def _():
                y_copy(b - 2, ys).wait()

            for c in range(MOE_ROWS // MOE_CHUNK):
                rows = slice(c * MOE_CHUNK, (c + 1) * MOE_CHUNK)
                x = _unpack_bf16_pairs(xbuf[xs, rows, :]).astype(BF16)
                g = _dot(x, wg_b[...])
                a = (g * jax.nn.sigmoid(g)) * _dot(x, wu_b[...])
                y = _dot(a.astype(BF16), wd_b[...])
                ybuf[ys, rows, :] = _pack_bf16_pairs(y.astype(BF16).astype(F32))
            y_copy(b, ys).start()

        return k

    def quad(i, k):
        for j in range(X_SLOTS):
            k = block(X_SLOTS * i + j, j, j % 2, k)
        return k

    lax.fori_loop(0, lax.div(nact + (X_SLOTS - 1), X_SLOTS), quad, -1)

    @pl.when(nact >= 2)
    def _():
        y_copy(nact - 2, lax.rem(nact, 2)).wait()

    y_copy(nact - 1, lax.rem(nact - 1, 2)).wait()
    lax.fori_loop(nact, n_blocks, lambda b, c: (tail_copy(b).wait(), c)[1], 0)


def _moe_ffn(meta1, xs, wg, wu, wd):
    R = xs.shape[0]
    hbm = pl.BlockSpec(memory_space=pl.ANY)
    return pl.pallas_call(
        functools.partial(_moe_ffn_body, n_blocks=R // MOE_ROWS),
        grid_spec=pltpu.PrefetchScalarGridSpec(
            num_scalar_prefetch=1,
            grid=(1,),
            in_specs=[hbm, hbm, hbm, hbm],
            out_specs=hbm,
            scratch_shapes=[pltpu.VMEM((X_SLOTS, MOE_ROWS, PACKED_DIM), U32),
                            pltpu.VMEM((2, MOE_ROWS, PACKED_DIM), U32),
                            pltpu.VMEM((W_STAGES, D_MODEL, D_EXPERT), F32),
                            pltpu.VMEM((W_STAGES, D_MODEL, D_EXPERT), F32),
                            pltpu.VMEM((W_STAGES, D_EXPERT, D_MODEL), F32),
                            pltpu.VMEM((D_MODEL, D_EXPERT), BF16),
                            pltpu.VMEM((D_MODEL, D_EXPERT), BF16),
                            pltpu.VMEM((D_EXPERT, D_MODEL), BF16),
                            pltpu.VMEM((MOE_ROWS, PACKED_DIM), U32),
                            pltpu.SemaphoreType.DMA((X_SLOTS,)), pltpu.SemaphoreType.DMA((2,)),
                            pltpu.SemaphoreType.DMA((W_STAGES,)), pltpu.SemaphoreType.DMA],
        ),
        out_shape=jax.ShapeDtypeStruct((R, PACKED_DIM), U32),
        compiler_params=pltpu.CompilerParams(dimension_semantics=("arbitrary",),
                                             vmem_limit_bytes=VMEM_LIMIT_BYTES),
        name="moe_ffn",
    )(meta1, xs, wg, wu, wd)


def _combine_body(meta_ref, cnt_ref, carry_ref, x1_ref, loc_ref, wts_ref, fg_ref, ys_ref, out_ref,
                  yloc_ref, sems, *, tm):
    tile = pl.program_id(0)
    slot = lax.rem(tile, 2)

    def fetch(t, s):
        def make_copy(local, glob, n):
            return pltpu.make_async_copy(ys_ref.at[pl.ds(glob, n)], yloc_ref.at[s, pl.ds(local, n)],
                                         sems.at[s])
        _run_copies(t, cnt_ref, carry_ref, meta_ref, make_copy)

    @pl.when(tile == 0)
    def _():
        yloc_ref[...] = jnp.zeros(yloc_ref.shape, U32)
        fetch(tile, slot)

    @pl.when(tile + 1 < pl.num_programs(0))
    def _():
        fetch(tile + 1, 1 - slot)

    n = _tile_rows(tile, cnt_ref)
    pltpu.make_async_copy(ys_ref.at[pl.ds(0, n)], yloc_ref.at[slot, pl.ds(0, n)], sems.at[slot]).wait()
    r = lax.broadcasted_iota(jnp.int32, (_loc_rows(tm), tm), 0)
    is0 = r == loc_ref[0:1, :]
    is1 = r == loc_ref[1:2, :]
    row_w = jnp.sum(jnp.where(is0, wts_ref[0:1, :], 0.0) + jnp.where(is1, wts_ref[1:2, :], 0.0),
                    axis=1, keepdims=True)
    yw = (row_w * _unpack_bf16_pairs(yloc_ref[slot])).astype(BF16)
    twohot = jnp.where(is0 | is1, 1.0, 0.0).astype(BF16)
    moe = lax.dot_general(twohot, yw, (((0,), (0,)), ((), ())), preferred_element_type=F32)
    out_ref[...] = _rms(x1_ref[...] + moe, fg_ref[...])


def _combine(meta1, cnt_tab, carry_tab, x1, loc, wts, fg, ys, *, tm):
    T = x1.shape[0]
    row = lambda i, *_: (i, 0)
    lane = lambda i, *_: (0, i)
    return pl.pallas_call(
        functools.partial(_combine_body, tm=tm),
        grid_spec=pltpu.PrefetchScalarGridSpec(
            num_scalar_prefetch=3,
            grid=(T // tm,),
            in_specs=[pl.BlockSpec((tm, D_MODEL), row),
                      pl.BlockSpec((2, tm), lane),
                      pl.BlockSpec((2, tm), lane),
                      pl.BlockSpec((1, D_MODEL), lambda i, *_: (0, 0)),
                      pl.BlockSpec(memory_space=pl.ANY)],
            out_specs=pl.BlockSpec((tm, D_MODEL), row),
            scratch_shapes=[pltpu.VMEM((2, _loc_rows(tm), PACKED_DIM), U32),
                            pltpu.SemaphoreType.DMA((2,))],
        ),
        out_shape=jax.ShapeDtypeStruct((T, D_MODEL), F32),
        compiler_params=pltpu.CompilerParams(dimension_semantics=("arbitrary",),
                                             vmem_limit_bytes=VMEM_LIMIT_BYTES),
        name="combine",
    )(meta1, cnt_tab, carry_tab, x1, loc, wts, fg, ys)


def _tile(n, t):
    t = min(n, t)
    assert n % t == 0, (n, t)
    return t


def kernel(x, mem, positions, mix_norm_g, w_in, gate_b, q_norm_g, w_uq, kv_norm_g, w_uk, w_uv, pool_w, pool_scale, mem_norm_g, w_mem_kv, w_br_pool, w_br_mla, w_br_mem, w_out, ffn_norm_g, w_router_group, b_router_group, w_router_expert, b_router_expert, w_gate_e, w_up_e, w_down_e, final_norm_g):
    B, S, D = x.shape
    assert D == D_MODEL and mix_norm_g.shape[0] == 1
    T = B * S
    mem_len = mem.shape[1]
    tm = _tile(S, 512)
    l = 0

    win_p = _pack_w_in(w_in[l])
    wuq_p = jnp.pad(w_uq[l].reshape(Q_LORA_RANK, MLA_HEADS, QK_NOPE_DIM + QK_ROPE_DIM),
                    ((0, 0), (0, 0), (0, QK_PAD_DIM - QK_NOPE_DIM - QK_ROPE_DIM))
                    ).reshape(Q_LORA_RANK, MLA_HEADS * QK_PAD_DIM).astype(BF16)
    inv_freq = 1.0 / (ROPE_THETA ** (jnp.arange(0, QK_ROPE_DIM, 2, dtype=F32) / QK_ROPE_DIM))
    invf = jnp.concatenate([inv_freq, inv_freq, jnp.zeros((LANES - QK_ROPE_DIM,), F32)])[None, :]
    wr = jnp.concatenate([w_router_expert[l], w_router_group[l],
                          jnp.zeros((D_MODEL, ROUTER_ROWS - N_EXPERTS - N_GROUPS), F32)], axis=1).T.astype(BF16)
    br = jnp.concatenate([b_router_expert[l], b_router_group[l],
                          jnp.zeros((ROUTER_ROWS - N_EXPERTS - N_GROUPS,), F32)])[:, None].astype(F32)
    x2 = x.reshape(T, D_MODEL)
    pos2 = positions.reshape(T, 1)

    ypool, xq, gates, q, k, v = _mixer_in(
        x2, pos2, invf, mix_norm_g[l][None, :], win_p, gate_b[l], q_norm_g[l][None, :], wuq_p,
        kv_norm_g[l][None, :], w_uk[l].astype(BF16), w_uv[l].astype(BF16), pool_w[l].astype(BF16),
        pool_scale[l][None, :], B=B, S=S, tm=tm)
    kmem, vmem = _mem_kv(mem.reshape(B * mem_len, D_MODEL), mem_norm_g[l][None, :], w_mem_kv[l].astype(BF16))
    ymla = _mla_attn_unrolled(q, k, v, tq=tm).reshape(T, MLA_HEADS * V_HEAD_DIM)
    x1, h2, wts, loc, cnt_tab, carry_tab, counts = _merge(
        x2, ypool, ymla, xq, gates, kmem, vmem, w_br_pool[l].astype(BF16), w_br_mla[l].astype(BF16),
        w_br_mem[l].astype(BF16), w_out[l].astype(BF16), ffn_norm_g[l][None, :], wr, br,
        B=B, S=S, tm=tm, mem_len=mem_len)

    R = 2 * T + (T // tm) * N_EXPERTS * RUN_ALIGN + N_EXPERTS * MOE_ROWS
    assert R % MOE_ROWS == 0 and R // MOE_ROWS <= META_PAD_END
    meta1 = _moe_pos(counts).reshape(META_LANES)
    cnt1 = cnt_tab[:, 0]
    carry1 = carry_tab[:, 0]
    xs = _dispatch(meta1, cnt1, carry1, loc, h2, R=R, tm=tm)
    ys = _moe_ffn(meta1, xs, w_gate_e[l], w_up_e[l], w_down_e[l])
    out = _combine(meta1, cnt1, carry1, x1, loc, wts, final_norm_g[None, :], ys, tm=tm)
    return out.reshape(B, S, D_MODEL)
```

```python
import functools
import math

import jax
import jax.numpy as jnp
from jax import lax
from jax.experimental import pallas as pl
from jax.experimental.pallas import tpu as pltpu

D_MODEL = 1024
POOL_WINDOWS = (2, 4, 8, 16)
POOL_GROUP_DIM = 128
POOL_DIM = 512
MLA_HEADS = 8
QK_NOPE_DIM = 128
QK_ROPE_DIM = 64
V_HEAD_DIM = 128
Q_LORA_RANK = 384
KV_LORA_RANK = 256
ROPE_THETA = 10000.0
XATTN_HEADS = 4
XATTN_HEAD_DIM = 128
XATTN_DIM = 512
N_BRANCHES = 3
N_GROUPS = 4
EXPERTS_PER_GROUP = 8
N_EXPERTS = 32
D_EXPERT = 256
RMS_EPS = 1e-6
NEG_INF = -1e30

LANES = 128
QK_PAD_DIM = 2 * LANES
POOL_HALO = 16
MOE_ROWS = 512
MOE_CHUNK = 256
X_SLOTS = 4
W_STAGES = 3
RUN_ALIGN = 8
PACKED_DIM = D_MODEL // 2
ROUTER_ROWS = 40
META_LANES = 256
META_PAD_END = 192
META_NACT = 255
VMEM_LIMIT_BYTES = 56 * 1024 * 1024

IN_POOL, IN_QD, IN_KV, IN_XQ, IN_GATE, IN_KR, IN_END = 0, 512, 896, 1152, 1664, 4736, 4864
W_IN_KR, W_IN_XQ, W_IN_END = 1152, 1216, 4800

F32 = jnp.float32
BF16 = jnp.bfloat16
U32 = jnp.uint32


def _rms(x, g):
    ms = jnp.mean(x * x, axis=-1, keepdims=True)
    return (x * lax.rsqrt(ms + RMS_EPS)) * g


def _dot(a, b):
    return jnp.dot(a, b, preferred_element_type=F32)


def _dot_nt(a, b):
    return lax.dot_general(a, b, (((1,), (1,)), ((), ())), preferred_element_type=F32)


def _const_spec(shape):
    nd = len(shape)
    return pl.BlockSpec(shape, lambda *_: (0,) * nd, pipeline_mode=pl.Buffered(1))


def _pack_w_in_body(wt_ref, o_ref):
    chunk = 256

    def copy_rows(dst, src, n):
        for r in range(0, n, chunk):
            m = min(chunk, n - r)
            o_ref[dst + r:dst + r + m, :] = wt_ref[src + r:src + r + m, :].astype(BF16)

    copy_rows(0, 0, W_IN_KR)
    copy_rows(IN_XQ, W_IN_XQ, W_IN_END - W_IN_XQ)
    copy_rows(IN_KR, W_IN_KR, W_IN_XQ - W_IN_KR)
    o_ref[IN_KR + QK_ROPE_DIM:IN_END, :] = jnp.zeros((LANES - QK_ROPE_DIM, D_MODEL), BF16)


def _pack_w_in(w_t):
    whole = pl.BlockSpec(memory_space=pltpu.VMEM)
    return pl.pallas_call(
        _pack_w_in_body,
        in_specs=[whole],
        out_specs=whole,
        out_shape=jax.ShapeDtypeStruct((IN_END, D_MODEL), BF16),
        compiler_params=pltpu.CompilerParams(vmem_limit_bytes=VMEM_LIMIT_BYTES),
        name="pack_w_in",
    )(w_t)


def _mixer_in_body(x_ref, pos_ref, invf_ref, mixg_ref, win_ref, gateb_ref, qg_ref, wuq_ref,
                   kvg_ref, wuk_ref, wuv_ref, poolw_ref, pools_ref,
                   ypool_ref, xq_ref, gates_ref, q_ref, k_ref, v_ref, ext_ref,
                   *, tm, tiles_per_seq, q_scale):
    si = lax.rem(pl.program_id(0), tiles_per_seq)

    @pl.when(pl.program_id(0) == 0)
    def _():
        ext_ref[0:POOL_HALO, :] = jnp.zeros((POOL_HALO, POOL_DIM), F32)

    hb = _rms(x_ref[...], mixg_ref[...]).astype(BF16)

    u = _dot_nt(hb, win_ref[IN_POOL:IN_QD, :])
    ext_ref[0:POOL_HALO, :] = jnp.where(si == 0, 0.0, ext_ref[0:POOL_HALO, :])
    ext_ref[POOL_HALO:POOL_HALO + tm, :] = u

    for c in range(N_BRANCHES):
        gl = _dot_nt(hb, win_ref[IN_GATE + c * D_MODEL:IN_GATE + (c + 1) * D_MODEL, :])
        gates_ref[:, c * D_MODEL:(c + 1) * D_MODEL] = jax.nn.sigmoid(gl + gateb_ref[c:c + 1, :]).astype(BF16)

    ang = pos_ref[...].astype(F32) * invf_ref[...]
    cos = jnp.cos(ang)
    sin = jnp.sin(ang)
    first_half = lax.broadcasted_iota(jnp.int32, (tm, LANES), 1) < (QK_ROPE_DIM // 2)
    sin_signed = jnp.where(first_half, -sin, sin)

    def rope(r):
        swapped = jnp.where(first_half, pltpu.roll(r, LANES - QK_ROPE_DIM // 2, 1),
                            pltpu.roll(r, QK_ROPE_DIM // 2, 1))
        return r * cos + swapped * sin_signed

    cq = _rms(_dot_nt(hb, win_ref[IN_QD:IN_KV, :]), qg_ref[...]).astype(BF16)
    for h in range(MLA_HEADS):
        qh = _dot(cq, wuq_ref[:, h * QK_PAD_DIM:(h + 1) * QK_PAD_DIM])
        q_ref[0, h, :, 0:LANES] = (qh[:, 0:LANES] * q_scale).astype(BF16)
        q_ref[0, h, :, LANES:QK_PAD_DIM] = (rope(qh[:, LANES:QK_PAD_DIM]) * q_scale).astype(BF16)

    ckv = _rms(_dot_nt(hb, win_ref[IN_KV:IN_XQ, :]), kvg_ref[...]).astype(BF16)
    kr = rope(_dot_nt(hb, win_ref[IN_KR:IN_END, :])).astype(BF16)
    for hp in range(MLA_HEADS // 2):
        cols = slice(hp * 2 * LANES, (hp + 1) * 2 * LANES)
        kn = _dot(ckv, wuk_ref[:, cols]).astype(BF16)
        vv = _dot(ckv, wuv_ref[:, cols]).astype(BF16)
        for j in range(2):
            h = 2 * hp + j
            k_ref[0, h, :, 0:LANES] = kn[:, j * LANES:(j + 1) * LANES]
            k_ref[0, h, :, LANES:QK_PAD_DIM] = kr
            v_ref[0, h] = vv[:, j * LANES:(j + 1) * LANES]

    t_seq = lax.broadcasted_iota(jnp.int32, (tm, 1), 0) + si * tm
    for g, w in enumerate(POOL_WINDOWS):
        lo = g * POOL_GROUP_DIM
        hi = lo + POOL_GROUP_DIM
        acc = u[:, lo:hi]
        for j in range(1, w):
            acc = acc + ext_ref[POOL_HALO - j:POOL_HALO - j + tm, lo:hi]
        cnt = jnp.minimum(t_seq + 1, w).astype(F32)
        p = acc / cnt - u[:, lo:hi]
        y = _dot(p.astype(BF16), poolw_ref[g]) * pools_ref[:, lo:hi]
        ypool_ref[:, lo:hi] = y.astype(BF16)
    ext_ref[0:POOL_HALO, :] = ext_ref[tm:tm + POOL_HALO, :]

    xq_ref[...] = _dot_nt(hb, win_ref[IN_XQ:IN_GATE, :]).astype(BF16)


def _mixer_in(x2, pos2, invf, mixg, win_p, gate_b, qg, wuq_p, kvg, wuk, wuv, pool_w, pool_s, *, B, S, tm):
    T = B * S
    tps = S // tm
    q_scale = (QK_NOPE_DIM + QK_ROPE_DIM) ** -0.5 * math.log2(math.e)
    body = functools.partial(_mixer_in_body, tm=tm, tiles_per_seq=tps, q_scale=q_scale)
    row = lambda i: (i, 0)
    head = lambda i: (i // tps, 0, i % tps, 0)
    return pl.pallas_call(
        body,
        grid=(T // tm,),
        in_specs=[
            pl.BlockSpec((tm, D_MODEL), row),
            pl.BlockSpec((tm, 1), row),
            _const_spec((1, LANES)),
            _const_spec((1, D_MODEL)),
            _const_spec((IN_END, D_MODEL)),
            _const_spec((N_BRANCHES, D_MODEL)),
            _const_spec((1, Q_LORA_RANK)),
            _const_spec((Q_LORA_RANK, MLA_HEADS * QK_PAD_DIM)),
            _const_spec((1, KV_LORA_RANK)),
            _const_spec((KV_LORA_RANK, MLA_HEADS * QK_NOPE_DIM)),
            _const_spec((KV_LORA_RANK, MLA_HEADS * V_HEAD_DIM)),
            _const_spec((len(POOL_WINDOWS), POOL_GROUP_DIM, POOL_GROUP_DIM)),
            _const_spec((1, POOL_DIM)),
        ],
        out_specs=[
            pl.BlockSpec((tm, POOL_DIM), row),
            pl.BlockSpec((tm, XATTN_DIM), row),
            pl.BlockSpec((tm, N_BRANCHES * D_MODEL), row),
            pl.BlockSpec((1, MLA_HEADS, tm, QK_PAD_DIM), head),
            pl.BlockSpec((1, MLA_HEADS, tm, QK_PAD_DIM), head),
            pl.BlockSpec((1, MLA_HEADS, tm, V_HEAD_DIM), head),
        ],
        out_shape=[
            jax.ShapeDtypeStruct((T, POOL_DIM), BF16),
            jax.ShapeDtypeStruct((T, XATTN_DIM), BF16),
            jax.ShapeDtypeStruct((T, N_BRANCHES * D_MODEL), BF16),
            jax.ShapeDtypeStruct((B, MLA_HEADS, S, QK_PAD_DIM), BF16),
            jax.ShapeDtypeStruct((B, MLA_HEADS, S, QK_PAD_DIM), BF16),
            jax.ShapeDtypeStruct((B, MLA_HEADS, S, V_HEAD_DIM), BF16),
        ],
        scratch_shapes=[pltpu.VMEM((tm + POOL_HALO, POOL_DIM), F32)],
        compiler_params=pltpu.CompilerParams(dimension_semantics=("arbitrary",),
                                             vmem_limit_bytes=VMEM_LIMIT_BYTES),
        name="mixer_in",
    )(x2, pos2, invf, mixg, win_p, gate_b, qg, wuq_p, kvg, wuk, wuv, pool_w, pool_s)


def _mem_kv_body(mem_ref, g_ref, w_ref, k_ref, v_ref):
    mb = _rms(mem_ref[...], g_ref[...]).astype(BF16)
    kv = _dot(mb, w_ref[...])
    k_ref[...] = kv[:, 0:XATTN_DIM].astype(BF16)
    v_ref[...] = kv[:, XATTN_DIM:2 * XATTN_DIM].astype(BF16)


def _mem_kv(mem2, g, w):
    rows = mem2.shape[0]
    tr = min(rows, 512)
    return pl.pallas_call(
        _mem_kv_body,
        grid=(rows // tr,),
        in_specs=[pl.BlockSpec((tr, D_MODEL), lambda i: (i, 0)),
                  _const_spec((1, D_MODEL)),
                  _const_spec((D_MODEL, 2 * XATTN_DIM))],
        out_specs=[pl.BlockSpec((tr, XATTN_DIM), lambda i: (i, 0)),
                   pl.BlockSpec((tr, XATTN_DIM), lambda i: (i, 0))],
        out_shape=[jax.ShapeDtypeStruct((rows, XATTN_DIM), BF16),
                   jax.ShapeDtypeStruct((rows, XATTN_DIM), BF16)],
        compiler_params=pltpu.CompilerParams(dimension_semantics=("arbitrary",)),
        name="mem_kv",
    )(mem2, g, w)


def _attn_unrolled_body(q_ref, k_ref, v_ref, o_ref, s_a, s_b, mc_a, mc_b, m_ref, l_ref, acc_ref, *, nq, tq):
    s_bufs = (s_a, s_b)
    mc_bufs = (mc_a, mc_b)
    mxu_row_split = 2
    blocks = [(qi, kb) for qi in range(nq) for kb in range(qi + 1)]

    def scores(i, slot):
        qi, kb = blocks[i]
        s = _dot_nt(q_ref[0, 0, qi * tq:(qi + 1) * tq, :], k_ref[0, 0, kb * tq:(kb + 1) * tq, :])
        if qi == kb:
            ri = lax.broadcasted_iota(jnp.int32, (tq, tq), 0)
            ci = lax.broadcasted_iota(jnp.int32, (tq, tq), 1)
            s = jnp.where(ci <= ri, s, NEG_INF)
        s_bufs[slot][...] = s
        mc_bufs[slot][...] = jnp.broadcast_to(jnp.max(s, axis=1, keepdims=True), (tq, LANES))

    def accumulate(i, slot):
        qi, kb = blocks[i]
        is_first = kb == 0
        is_last = kb == qi
        if is_first:
            m_new = mc_bufs[slot][...]
        else:
            m_prev = m_ref[...]
            m_new = jnp.maximum(m_prev, mc_bufs[slot][...])
            alpha = jnp.exp2(m_prev - m_new)
        p = jnp.exp2(s_bufs[slot][...] - jnp.concatenate([m_new] * (tq // LANES), axis=1))
        psum = p[:, 0:LANES]
        for c in range(1, tq // LANES):
            psum = psum + p[:, c * LANES:(c + 1) * LANES]
        l_new = psum if is_first else alpha * l_ref[...] + psum
        pb = p.astype(BF16)
        v = v_ref[0, 0, kb * tq:(kb + 1) * tq, :]
        if is_last:
            inv = 1.0 / jnp.sum(l_new, axis=1, keepdims=True)
        else:
            l_ref[...] = l_new
            m_ref[...] = m_new
        h = tq // mxu_row_split
        for r in range(mxu_row_split):
            rows = slice(r * h, (r + 1) * h)
            acc = _dot(pb[rows, :], v)
            if not is_first:
                acc = alpha[rows, :] * acc_ref[rows, :] + acc
            if is_last:
                o_ref[0, qi * tq + r * h:qi * tq + (r + 1) * h, :] = (acc * inv[rows, :]).astype(BF16)
            else:
                acc_ref[rows, :] = acc

    scores(0, 0)
    for i in range(len(blocks)):
        if i + 1 < len(blocks):
            scores(i + 1, (i + 1) % 2)
        accumulate(i, i % 2)


def _mla_attn_unrolled(q, k, v, *, tq):
    B, H, S, _ = q.shape
    per_head = lambda b, h: (b, h, 0, 0)
    return pl.pallas_call(
        functools.partial(_attn_unrolled_body, nq=S // tq, tq=tq),
        grid=(B, H),
        in_specs=[pl.BlockSpec((1, 1, S, QK_PAD_DIM), per_head),
                  pl.BlockSpec((1, 1, S, QK_PAD_DIM), per_head),
                  pl.BlockSpec((1, 1, S, V_HEAD_DIM), per_head)],
        out_specs=pl.BlockSpec((1, S, V_HEAD_DIM), lambda b, h: (b, 0, h)),
        out_shape=jax.ShapeDtypeStruct((B, S, H * V_HEAD_DIM), BF16),
        scratch_shapes=[pltpu.VMEM((tq, tq), F32), pltpu.VMEM((tq, tq), F32),
                        pltpu.VMEM((tq, LANES), F32), pltpu.VMEM((tq, LANES), F32),
                        pltpu.VMEM((tq, LANES), F32), pltpu.VMEM((tq, LANES), F32),
                        pltpu.VMEM((tq, V_HEAD_DIM), F32)],
        compiler_params=pltpu.CompilerParams(dimension_semantics=("arbitrary", "arbitrary"),
                                             vmem_limit_bytes=VMEM_LIMIT_BYTES),
        name="mla_attn",
    )(q, k, v)


def _merge_body(x_ref, ypool_ref, ymla_ref, xq_ref, gates_ref, kmem_ref, vmem_ref,
                wbp_ref, wbm_ref, wbx_ref, wout_ref, ffng_ref, wr_ref, br_ref,
                x1_ref, h2_ref, wts_ref, loc_ref, cnt_tab_ref, carry_tab_ref, counts_ref, carry_ref, *, tm):
    @pl.when(pl.program_id(0) == 0)
    def _():
        carry_ref[...] = jnp.zeros((N_EXPERTS, LANES), F32)

    xq = xq_ref[...]
    parts = []
    for h in range(XATTN_HEADS):
        cols = slice(h * XATTN_HEAD_DIM, (h + 1) * XATTN_HEAD_DIM)
        s = _dot_nt(xq[:, cols], kmem_ref[:, cols]) * (XATTN_HEAD_DIM ** -0.5)
        e = jnp.exp(s - jnp.max(s, axis=1, keepdims=True))
        p = e / jnp.sum(e, axis=1, keepdims=True)
        parts.append(_dot(p.astype(BF16), vmem_ref[:, cols]))
    ymem = jnp.concatenate(parts, axis=1).astype(BF16)

    gates = gates_ref[...].astype(F32)
    merged = (gates[:, 0:D_MODEL] * _dot(ypool_ref[...], wbp_ref[...])
              + gates[:, D_MODEL:2 * D_MODEL] * _dot(ymla_ref[...], wbm_ref[...])
              + gates[:, 2 * D_MODEL:3 * D_MODEL] * _dot(ymem, wbx_ref[...]))
    x1 = x_ref[...] + _dot(merged.astype(BF16), wout_ref[...])
    x1_ref[...] = x1
    h2 = _rms(x1, ffng_ref[...]).astype(BF16)
    h2_ref[...] = h2

    lt = _dot_nt(wr_ref[...], h2) + br_ref[...]
    gl = lt[N_EXPERTS:N_EXPERTS + N_GROUPS, :]
    gmax = jnp.max(gl, axis=0, keepdims=True)
    r4 = lax.broadcasted_iota(jnp.int32, (N_GROUPS, tm), 0).astype(F32)
    gidx = jnp.min(jnp.where(gl == gmax, r4, float(N_GROUPS)), axis=0, keepdims=True)
    pg = 1.0 / jnp.sum(jnp.exp(gl - gmax), axis=0, keepdims=True)
    esel = lt[0:EXPERTS_PER_GROUP, :]
    for g in range(1, N_GROUPS):
        esel = jnp.where(gidx == float(g), lt[g * EXPERTS_PER_GROUP:(g + 1) * EXPERTS_PER_GROUP, :], esel)
    r8 = lax.broadcasted_iota(jnp.int32, (EXPERTS_PER_GROUP, tm), 0).astype(F32)
    m1 = jnp.max(esel, axis=0, keepdims=True)
    i1 = jnp.min(jnp.where(esel == m1, r8, float(EXPERTS_PER_GROUP)), axis=0, keepdims=True)
    rest = jnp.where(r8 == i1, -jnp.inf, esel)
    m2 = jnp.max(rest, axis=0, keepdims=True)
    i2 = jnp.min(jnp.where(rest == m2, r8, float(EXPERTS_PER_GROUP)), axis=0, keepdims=True)
    e2 = jnp.exp(m2 - m1)
    den = 1.0 + e2
    wts_ref[0:1, :] = pg / den
    wts_ref[1:2, :] = pg * e2 / den
    ex1 = gidx * float(EXPERTS_PER_GROUP) + i1
    ex2 = gidx * float(EXPERTS_PER_GROUP) + i2

    r32 = lax.broadcasted_iota(jnp.int32, (N_EXPERTS, tm), 0).astype(F32)
    is1 = r32 == ex1
    is2 = r32 == ex2
    member = jnp.where(is1 | is2, 1.0, 0.0)
    upper = jnp.where(lax.broadcasted_iota(jnp.int32, (tm, tm), 0)
                      <= lax.broadcasted_iota(jnp.int32, (tm, tm), 1), 1.0, 0.0).astype(BF16)
    incl = _dot(member.astype(BF16), upper)
    run = jnp.floor((jnp.sum(member, axis=1, keepdims=True) + (RUN_ALIGN - 1)) / RUN_ALIGN) * RUN_ALIGN
    rcol = lax.broadcasted_iota(jnp.int32, (N_EXPERTS, 1), 0)
    run_start = jnp.zeros((N_EXPERTS, 1), F32)
    for e in range(N_EXPERTS - 1):
        run_start = run_start + jnp.where(rcol > e, run[e:e + 1, :], 0.0)
    pos = incl - 1.0 + run_start
    loc_ref[0:1, :] = jnp.sum(jnp.where(is1, pos, 0.0), axis=0, keepdims=True).astype(jnp.int32)
    loc_ref[1:2, :] = jnp.sum(jnp.where(is2, pos, 0.0), axis=0, keepdims=True).astype(jnp.int32)
    carry = carry_ref[...]
    total = carry + run
    cnt_tab_ref[...] = jnp.broadcast_to(run, (N_EXPERTS, LANES)).astype(jnp.int32)
    carry_tab_ref[...] = carry.astype(jnp.int32)
    carry_ref[...] = total
    counts_ref[...] = total.astype(jnp.int32)


def _merge(x2, ypool, ymla, xq, gates, kmem, vmem, wbp, wbm, wbx, wout, ffng, wr, br, *, B, S, tm, mem_len):
    T = B * S
    tps = S // tm
    row = lambda i: (i, 0)
    lane = lambda i: (0, i)
    memb = lambda i: (i // tps, 0)
    return pl.pallas_call(
        functools.partial(_merge_body, tm=tm),
        grid=(T // tm,),
        in_specs=[
            pl.BlockSpec((tm, D_MODEL), row),
            pl.BlockSpec((tm, POOL_DIM), row),
            pl.BlockSpec((tm, MLA_HEADS * V_HEAD_DIM), row),
            pl.BlockSpec((tm, XATTN_DIM), row),
            pl.BlockSpec((tm, N_BRANCHES * D_MODEL), row),
            pl.BlockSpec((mem_len, XATTN_DIM), memb),
            pl.BlockSpec((mem_len, XATTN_DIM), memb),
            _const_spec((POOL_DIM, D_MODEL)),
            _const_spec((MLA_HEADS * V_HEAD_DIM, D_MODEL)),
            _const_spec((XATTN_DIM, D_MODEL)),
            _const_spec((D_MODEL, D_MODEL)),
            _const_spec((1, D_MODEL)),
            _const_spec((ROUTER_ROWS, D_MODEL)),
            _const_spec((ROUTER_ROWS, 1)),
        ],
        out_specs=[
            pl.BlockSpec((tm, D_MODEL), row),
            pl.BlockSpec((tm, D_MODEL), row),
            pl.BlockSpec((2, tm), lane),
            pl.BlockSpec((2, tm), lane),
            pl.BlockSpec((N_EXPERTS, LANES), row),
            pl.BlockSpec((N_EXPERTS, LANES), row),
            pl.BlockSpec((N_EXPERTS, LANES), lambda i: (0, 0)),
        ],
        out_shape=[
            jax.ShapeDtypeStruct((T, D_MODEL), F32),
            jax.ShapeDtypeStruct((T, D_MODEL), BF16),
            jax.ShapeDtypeStruct((2, T), F32),
            jax.ShapeDtypeStruct((2, T), jnp.int32),
            jax.ShapeDtypeStruct((T // tm * N_EXPERTS, LANES), jnp.int32),
            jax.ShapeDtypeStruct((T // tm * N_EXPERTS, LANES), jnp.int32),
            jax.ShapeDtypeStruct((N_EXPERTS, LANES), jnp.int32),
        ],
        scratch_shapes=[pltpu.VMEM((N_EXPERTS, LANES), F32)],
        compiler_params=pltpu.CompilerParams(dimension_semantics=("arbitrary",),
                                             vmem_limit_bytes=VMEM_LIMIT_BYTES),
        name="merge",
    )(x2, ypool, ymla, xq, gates, kmem, vmem, wbp, wbm, wbx, wout, ffng, wr, br)


def _moe_pos_body(counts_ref, meta_ref):
    shift = int(math.log2(MOE_ROWS))
    cnt = counts_ref[...]
    padded = lax.shift_left(lax.shift_right_logical(cnt + (MOE_ROWS - 1), shift), shift)
    r32 = lax.broadcasted_iota(jnp.int32, (N_EXPERTS, LANES), 0)
    pad_start = jnp.zeros((N_EXPERTS, LANES), jnp.int32)
    for e in range(N_EXPERTS - 1):
        pad_start = pad_start + jnp.where(r32 > e, padded[e:e + 1, :], 0)
    pad_end = pad_start + padded

    lane = lax.broadcasted_iota(jnp.int32, (1, META_LANES), 1)
    block_row = lane * MOE_ROWS
    blk_e = jnp.zeros((1, META_LANES), jnp.int32)
    pe_row = jnp.zeros((1, META_LANES), jnp.int32)
    for e in range(N_EXPERTS):
        pe = pad_end[e:e + 1, 0:1]
        blk_e = blk_e + jnp.where(pe <= block_row, 1, 0)
        pe_row = pe_row + jnp.where(lane == META_PAD_END + e, pe, 0)
    blk_e = jnp.minimum(blk_e, N_EXPERTS - 1)
    nact = lax.shift_right_logical(pad_end[N_EXPERTS - 1:N_EXPERTS, 0:1], shift)
    meta = jnp.where(lane < META_PAD_END, blk_e, pe_row)
    meta_ref[...] = jnp.where(lane == META_NACT, nact, meta)


def _moe_pos(counts):
    full = lambda shape: pl.BlockSpec(shape, lambda i: (0,) * len(shape))
    return pl.pallas_call(
        _moe_pos_body,
        grid=(1,),
        in_specs=[full((N_EXPERTS, LANES))],
        out_specs=full((1, META_LANES)),
        out_shape=jax.ShapeDtypeStruct((1, META_LANES), jnp.int32),
        compiler_params=pltpu.CompilerParams(dimension_semantics=("arbitrary",)),
        name="moe_pos",
    )(counts)


def _pack_bf16_pairs(x):
    lo = pltpu.bitcast(x[:, 0:PACKED_DIM], U32)
    hi = pltpu.bitcast(x[:, PACKED_DIM:D_MODEL], U32)
    return hi | lax.shift_right_logical(lo, jnp.uint32(16))


def _unpack_bf16_pairs(w):
    lo = pltpu.bitcast(lax.shift_left(w, jnp.uint32(16)), F32)
    hi = pltpu.bitcast(w & jnp.uint32(0xFFFF0000), F32)
    return jnp.concatenate([lo, hi], axis=1)


def _loc_rows(tm):
    return 2 * tm + N_EXPERTS * RUN_ALIGN


def _run_copies(tile, cnt_ref, carry_ref, meta_ref, make_copy):
    def per_expert(e, local):
        n = pl.multiple_of(cnt_ref[tile * N_EXPERTS + e], RUN_ALIGN)
        start = jnp.where(e == 0, 0, meta_ref[META_PAD_END + jnp.maximum(e - 1, 0)])
        glob = pl.multiple_of(start + carry_ref[tile * N_EXPERTS + e], RUN_ALIGN)

        @pl.when(n > 0)
        def _():
            make_copy(pl.multiple_of(local, RUN_ALIGN), glob, n).start()

        return local + n

    return pl.multiple_of(lax.fori_loop(0, N_EXPERTS, per_expert, 0), RUN_ALIGN)


def _tile_rows(tile, cnt_ref):
    total = lax.fori_loop(0, N_EXPERTS, lambda e, t: t + cnt_ref[tile * N_EXPERTS + e], 0)
    return pl.multiple_of(total, RUN_ALIGN)


def _dispatch_body(meta_ref, cnt_ref, carry_ref, loc_ref, h2_ref, xs_ref, xloc_ref, zero_ref, sems, zsem,
                   *, tm, n_blocks):
    tile = pl.program_id(0)
    slot = lax.rem(tile, 2)

    def wait_rows(t, s):
        n = _tile_rows(t, cnt_ref)
        pltpu.make_async_copy(xloc_ref.at[s, pl.ds(0, n)], xs_ref.at[pl.ds(0, n)], sems.at[s]).wait()

    def pad_copy(e):
        end = pl.multiple_of(meta_ref[META_PAD_END + e], MOE_ROWS)
        return pltpu.make_async_copy(zero_ref, xs_ref.at[pl.ds(end - MOE_ROWS, MOE_ROWS)], zsem)

    def has_rows(e):
        prev = jnp.where(e == 0, 0, meta_ref[META_PAD_END + jnp.maximum(e - 1, 0)])
        return meta_ref[META_PAD_END + e] > prev

    @pl.when(tile == 0)
    def _():
        zero_ref[...] = jnp.zeros((MOE_ROWS, PACKED_DIM), U32)

        def start(e, c):
            @pl.when(has_rows(e))
            def _():
                pad_copy(e).start()
            return c

        def wait(e, c):
            @pl.when(has_rows(e))
            def _():
                pad_copy(e).wait()
            return c

        def tail_copy(b):
            return pltpu.make_async_copy(
                zero_ref, xs_ref.at[pl.ds(pl.multiple_of(b * MOE_ROWS, MOE_ROWS), MOE_ROWS)], zsem)

        def tail_start(b, c):
            tail_copy(b).start()
            return c

        def tail_wait(b, c):
            tail_copy(b).wait()
            return c

        nact = meta_ref[META_NACT]
        lax.fori_loop(0, N_EXPERTS, start, 0)
        lax.fori_loop(nact, n_blocks, tail_start, 0)
        lax.fori_loop(0, N_EXPERTS, wait, 0)
        lax.fori_loop(nact, n_blocks, tail_wait, 0)

    @pl.when(tile >= 2)
    def _():
        wait_rows(tile - 2, slot)

    r = lax.broadcasted_iota(jnp.int32, (_loc_rows(tm), tm), 0)
    onehot = jnp.where((r == loc_ref[0:1, :]) | (r == loc_ref[1:2, :]), 1.0, 0.0).astype(BF16)
    xloc_ref[slot] = _pack_bf16_pairs(_dot(onehot, h2_ref[...]))

    def make_copy(local, glob, n):
        return pltpu.make_async_copy(xloc_ref.at[slot, pl.ds(local, n)], xs_ref.at[pl.ds(glob, n)],
                                     sems.at[slot])

    _run_copies(tile, cnt_ref, carry_ref, meta_ref, make_copy)

    @pl.when(tile == pl.num_programs(0) - 1)
    def _():
        @pl.when(tile >= 1)
        def _():
            wait_rows(tile - 1, 1 - slot)

        wait_rows(tile, slot)


def _dispatch(meta1, cnt_tab, carry_tab, loc, h2, *, R, tm):
    T = h2.shape[0]
    return pl.pallas_call(
        functools.partial(_dispatch_body, tm=tm, n_blocks=R // MOE_ROWS),
        grid_spec=pltpu.PrefetchScalarGridSpec(
            num_scalar_prefetch=3,
            grid=(T // tm,),
            in_specs=[pl.BlockSpec((2, tm), lambda i, *_: (0, i)),
                      pl.BlockSpec((tm, D_MODEL), lambda i, *_: (i, 0))],
            out_specs=pl.BlockSpec(memory_space=pl.ANY),
            scratch_shapes=[pltpu.VMEM((2, _loc_rows(tm), PACKED_DIM), U32),
                            pltpu.VMEM((MOE_ROWS, PACKED_DIM), U32),
                            pltpu.SemaphoreType.DMA((2,)), pltpu.SemaphoreType.DMA],
        ),
        out_shape=jax.ShapeDtypeStruct((R, PACKED_DIM), U32),
        compiler_params=pltpu.CompilerParams(dimension_semantics=("arbitrary",),
                                             vmem_limit_bytes=VMEM_LIMIT_BYTES),
        name="dispatch",
    )(meta1, cnt_tab, carry_tab, loc, h2)


def _moe_ffn_body(meta_ref, xs_ref, wg_hbm, wu_hbm, wd_hbm, ys_ref,
                  xbuf, ybuf, wg_stage, wu_stage, wd_stage, wg_b, wu_b, wd_b, zero_ref,
                  xsem, ysem, wsem, zsem, *, n_blocks):
    nact = meta_ref[META_NACT]
    shift = int(math.log2(MOE_ROWS))

    def rows_of(b):
        return pl.ds(pl.multiple_of(b * MOE_ROWS, MOE_ROWS), MOE_ROWS)

    def x_copy(b, s):
        return pltpu.make_async_copy(xs_ref.at[rows_of(b)], xbuf.at[s], xsem.at[s])

    def y_copy(b, s):
        return pltpu.make_async_copy(ybuf.at[s], ys_ref.at[rows_of(b)], ysem.at[s])

    def w_copies(e, s):
        return (pltpu.make_async_copy(wg_hbm.at[e], wg_stage.at[s], wsem.at[s]),
                pltpu.make_async_copy(wu_hbm.at[e], wu_stage.at[s], wsem.at[s]),
                pltpu.make_async_copy(wd_hbm.at[e], wd_stage.at[s], wsem.at[s]))

    def tail_copy(b):
        return pltpu.make_async_copy(zero_ref, ys_ref.at[rows_of(b)], zsem)

    zero_ref[...] = jnp.zeros((MOE_ROWS, PACKED_DIM), U32)
    lax.fori_loop(nact, n_blocks, lambda b, c: (tail_copy(b).start(), c)[1], 0)

    def next_expert_block(e):
        return lax.shift_right_logical(meta_ref[META_PAD_END + e], shift)

    def start_weights(b, s):
        @pl.when(b < nact)
        def _():
            for cp in w_copies(meta_ref[jnp.minimum(b, n_blocks - 1)], s):
                cp.start()

    x_copy(0, 0).start()

    @pl.when(nact > 1)
    def _():
        x_copy(1, 1).start()

    e_first = meta_ref[0]
    start_weights(0, 0)
    start_weights(next_expert_block(e_first), 1)

    def block(b, xs, ys, k_prev):
        valid = b < nact
        e = meta_ref[jnp.minimum(b, nact - 1)]
        changed = jnp.logical_and(valid, jnp.logical_or(b == 0, e != meta_ref[jnp.maximum(b - 1, 0)]))
        k = jnp.where(changed, k_prev + 1, k_prev)

        @pl.when(changed)
        def _():
            ws = lax.rem(k, W_STAGES)
            for cp in w_copies(e, ws):
                cp.wait()
            wg_b[...] = wg_stage[ws].astype(BF16)
            wu_b[...] = wu_stage[ws].astype(BF16)
            wd_b[...] = wd_stage[ws].astype(BF16)
            n1 = next_expert_block(e)
            e1 = meta_ref[jnp.minimum(n1, n_blocks - 1)]
            n2 = jnp.where(n1 < nact, next_expert_block(e1), n_blocks)
            start_weights(n2, lax.rem(k + 2, W_STAGES))

        @pl.when(valid)
        def _():
            x_copy(b, xs).wait()

            @pl.when(b + 2 < nact)
            def _():
                x_copy(b + 2, (xs + 2) % X_SLOTS).start()

            @pl.when(b >= 2)
            def _():
                y_copy(b - 2, ys).wait()

            for c in range(MOE_ROWS // MOE_CHUNK):
                rows = slice(c * MOE_CHUNK, (c + 1) * MOE_CHUNK)
                x = _unpack_bf16_pairs(xbuf[xs, rows, :]).astype(BF16)
                g = _dot(x, wg_b[...])
                a = (g * jax.nn.sigmoid(g)) * _dot(x, wu_b[...])
                y = _dot(a.astype(BF16), wd_b[...])
                ybuf[ys, rows, :] = _pack_bf16_pairs(y.astype(BF16).astype(F32))
            y_copy(b, ys).start()

        return k

    def quad(i, k):
        for j in range(X_SLOTS):
            k = block(X_SLOTS * i + j, j, j % 2, k)
        return k

    lax.fori_loop(0, lax.div(nact + (X_SLOTS - 1), X_SLOTS), quad, -1)

    @pl.when(nact >= 2)
    def _():
        y_copy(nact - 2, lax.rem(nact, 2)).wait()

    y_copy(nact - 1, lax.rem(nact - 1, 2)).wait()
    lax.fori_loop(nact, n_blocks, lambda b, c: (tail_copy(b).wait(), c)[1], 0)


def _moe_ffn(meta1, xs, wg, wu, wd):
    R = xs.shape[0]
    hbm = pl.BlockSpec(memory_space=pl.ANY)
    return pl.pallas_call(
        functools.partial(_moe_ffn_body, n_blocks=R // MOE_ROWS),
        grid_spec=pltpu.PrefetchScalarGridSpec(
            num_scalar_prefetch=1,
            grid=(1,),
            in_specs=[hbm, hbm, hbm, hbm],
            out_specs=hbm,
            scratch_shapes=[pltpu.VMEM((X_SLOTS, MOE_ROWS, PACKED_DIM), U32),
                            pltpu.VMEM((2, MOE_ROWS, PACKED_DIM), U32),
                            pltpu.VMEM((W_STAGES, D_MODEL, D_EXPERT), F32),
                            pltpu.VMEM((W_STAGES, D_MODEL, D_EXPERT), F32),
                            pltpu.VMEM((W_STAGES, D_EXPERT, D_MODEL), F32),
                            pltpu.VMEM((D_MODEL, D_EXPERT), BF16),
                            pltpu.VMEM((D_MODEL, D_EXPERT), BF16),
                            pltpu.VMEM((D_EXPERT, D_MODEL), BF16),
                            pltpu.VMEM((MOE_ROWS, PACKED_DIM), U32),
                            pltpu.SemaphoreType.DMA((X_SLOTS,)), pltpu.SemaphoreType.DMA((2,)),
                            pltpu.SemaphoreType.DMA((W_STAGES,)), pltpu.SemaphoreType.DMA],
        ),
        out_shape=jax.ShapeDtypeStruct((R, PACKED_DIM), U32),
        compiler_params=pltpu.CompilerParams(dimension_semantics=("arbitrary",),
                                             vmem_limit_bytes=VMEM_LIMIT_BYTES),
        name="moe_ffn",
    )(meta1, xs, wg, wu, wd)


def _combine_body(meta_ref, cnt_ref, carry_ref, x1_ref, loc_ref, wts_ref, fg_ref, ys_ref, out_ref,
                  yloc_ref, sems, *, tm):
    tile = pl.program_id(0)
    slot = lax.rem(tile, 2)

    def fetch(t, s):
        def make_copy(local, glob, n):
            return pltpu.make_async_copy(ys_ref.at[pl.ds(glob, n)], yloc_ref.at[s, pl.ds(local, n)],
                                         sems.at[s])
        _run_copies(t, cnt_ref, carry_ref, meta_ref, make_copy)

    @pl.when(tile == 0)
    def _():
        yloc_ref[...] = jnp.zeros(yloc_ref.shape, U32)
        fetch(tile, slot)

    @pl.when(tile + 1 < pl.num_programs(0))
    def _():
        fetch(tile + 1, 1 - slot)

    n = _tile_rows(tile, cnt_ref)
    pltpu.make_async_copy(ys_ref.at[pl.ds(0, n)], yloc_ref.at[slot, pl.ds(0, n)], sems.at[slot]).wait()
    r = lax.broadcasted_iota(jnp.int32, (_loc_rows(tm), tm), 0)
    is0 = r == loc_ref[0:1, :]
    is1 = r == loc_ref[1:2, :]
    row_w = jnp.sum(jnp.where(is0, wts_ref[0:1, :], 0.0) + jnp.where(is1, wts_ref[1:2, :], 0.0),
                    axis=1, keepdims=True)
    yw = (row_w * _unpack_bf16_pairs(yloc_ref[slot])).astype(BF16)
    twohot = jnp.where(is0 | is1, 1.0, 0.0).astype(BF16)
    moe = lax.dot_general(twohot, yw, (((0,), (0,)), ((), ())), preferred_element_type=F32)
    out_ref[...] = _rms(x1_ref[...] + moe, fg_ref[...])


def _combine(meta1, cnt_tab, carry_tab, x1, loc, wts, fg, ys, *, tm):
    T = x1.shape[0]
    row = lambda i, *_: (i, 0)
    lane = lambda i, *_: (0, i)
    return pl.pallas_call(
        functools.partial(_combine_body, tm=tm),
        grid_spec=pltpu.PrefetchScalarGridSpec(
            num_scalar_prefetch=3,
            grid=(T // tm,),
            in_specs=[pl.BlockSpec((tm, D_MODEL), row),
                      pl.BlockSpec((2, tm), lane),
                      pl.BlockSpec((2, tm), lane),
                      pl.BlockSpec((1, D_MODEL), lambda i, *_: (0, 0)),
                      pl.BlockSpec(memory_space=pl.ANY)],
            out_specs=pl.BlockSpec((tm, D_MODEL), row),
            scratch_shapes=[pltpu.VMEM((2, _loc_rows(tm), PACKED_DIM), U32),
                            pltpu.SemaphoreType.DMA((2,))],
        ),
        out_shape=jax.ShapeDtypeStruct((T, D_MODEL), F32),
        compiler_params=pltpu.CompilerParams(dimension_semantics=("arbitrary",),
                                             vmem_limit_bytes=VMEM_LIMIT_BYTES),
        name="combine",
    )(meta1, cnt_tab, carry_tab, x1, loc, wts, fg, ys)


def _tile(n, t):
    t = min(n, t)
    assert n % t == 0, (n, t)
    return t


def kernel(x, mem, positions, mix_norm_g, w_in, gate_b, q_norm_g, w_uq, kv_norm_g, w_uk, w_uv, pool_w, pool_scale, mem_norm_g, w_mem_kv, w_br_pool, w_br_mla, w_br_mem, w_out, ffn_norm_g, w_router_group, b_router_group, w_router_expert, b_router_expert, w_gate_e, w_up_e, w_down_e, final_norm_g):
    B, S, D = x.shape
    assert D == D_MODEL and mix_norm_g.shape[0] == 1
    T = B * S
    mem_len = mem.shape[1]
    tm = _tile(S, 512)
    l = 0

    win_p = _pack_w_in(jnp.swapaxes(w_in, 1, 2).reshape(W_IN_END, D_MODEL))
    wuq_p = jnp.pad(w_uq[l].reshape(Q_LORA_RANK, MLA_HEADS, QK_NOPE_DIM + QK_ROPE_DIM),
                    ((0, 0), (0, 0), (0, QK_PAD_DIM - QK_NOPE_DIM - QK_ROPE_DIM))
                    ).reshape(Q_LORA_RANK, MLA_HEADS * QK_PAD_DIM).astype(BF16)
    inv_freq = 1.0 / (ROPE_THETA ** (jnp.arange(0, QK_ROPE_DIM, 2, dtype=F32) / QK_ROPE_DIM))
    invf = jnp.concatenate([inv_freq, inv_freq, jnp.zeros((LANES - QK_ROPE_DIM,), F32)])[None, :]
    wr = jnp.concatenate([w_router_expert[l], w_router_group[l],
                          jnp.zeros((D_MODEL, ROUTER_ROWS - N_EXPERTS - N_GROUPS), F32)], axis=1).T.astype(BF16)
    br = jnp.concatenate([b_router_expert[l], b_router_group[l],
                          jnp.zeros((ROUTER_ROWS - N_EXPERTS - N_GROUPS,), F32)])[:, None].astype(F32)
    x2 = x.reshape(T, D_MODEL)
    pos2 = positions.reshape(T, 1)

    ypool, xq, gates, q, k, v = _mixer_in(
        x2, pos2, invf, mix_norm_g[l][None, :], win_p, gate_b[l], q_norm_g[l][None, :], wuq_p,
        kv_norm_g[l][None, :], w_uk[l].astype(BF16), w_uv[l].astype(BF16), pool_w[l].astype(BF16),
        pool_scale[l][None, :], B=B, S=S, tm=tm)
    kmem, vmem = _mem_kv(mem.reshape(B * mem_len, D_MODEL), mem_norm_g[l][None, :], w_mem_kv[l].astype(BF16))
    ymla = _mla_attn_unrolled(q, k, v, tq=tm).reshape(T, MLA_HEADS * V_HEAD_DIM)
    x1, h2, wts, loc, cnt_tab, carry_tab, counts = _merge(
        x2, ypool, ymla, xq, gates, kmem, vmem, w_br_pool[l].astype(BF16), w_br_mla[l].astype(BF16),
        w_br_mem[l].astype(BF16), w_out[l].astype(BF16), ffn_norm_g[l][None, :], wr, br,
        B=B, S=S, tm=tm, mem_len=mem_len)

    R = 2 * T + (T // tm) * N_EXPERTS * RUN_ALIGN + N_EXPERTS * MOE_ROWS
    assert R % MOE_ROWS == 0 and R // MOE_ROWS <= META_PAD_END
    meta1 = _moe_pos(counts).reshape(META_LANES)
    cnt1 = cnt_tab[:, 0]
    carry1 = carry_tab[:, 0]
    xs = _dispatch(meta1, cnt1, carry1, loc, h2, R=R, tm=tm)
    ys = _moe_ffn(meta1, xs, w_gate_e[l], w_up_e[l], w_down_e[l])
    out = _combine(meta1, cnt1, carry1, x1, loc, wts, final_norm_g[None, :], ys, tm=tm)
    return out.reshape(B, S, D_MODEL)
```

```python
import functools
import math

import jax
import jax.numpy as jnp
from jax import lax
from jax.experimental import pallas as pl
from jax.experimental.pallas import tpu as pltpu

D_MODEL = 1024
POOL_WINDOWS = (2, 4, 8, 16)
POOL_GROUP_DIM = 128
POOL_DIM = 512
MLA_HEADS = 8
QK_NOPE_DIM = 128
QK_ROPE_DIM = 64
V_HEAD_DIM = 128
Q_LORA_RANK = 384
KV_LORA_RANK = 256
ROPE_THETA = 10000.0
XATTN_HEADS = 4
XATTN_HEAD_DIM = 128
XATTN_DIM = 512
N_BRANCHES = 3
N_GROUPS = 4
EXPERTS_PER_GROUP = 8
N_EXPERTS = 32
D_EXPERT = 256
RMS_EPS = 1e-6
NEG_INF = -1e30

LANES = 128
QK_PAD_DIM = 2 * LANES
POOL_HALO = 16
MOE_ROWS = 512
MOE_CHUNK = 256
X_SLOTS = 4
W_STAGES = 3
RUN_ALIGN = 8
PACKED_DIM = D_MODEL // 2
ROUTER_ROWS = 40
META_LANES = 256
META_PAD_END = 192
META_NACT = 255
VMEM_LIMIT_BYTES = 56 * 1024 * 1024

IN_POOL, IN_QD, IN_KV, IN_XQ, IN_GATE, IN_KR, IN_END = 0, 512, 896, 1152, 1664, 4736, 4864
W_IN_KR, W_IN_XQ, W_IN_END = 1152, 1216, 4800

F32 = jnp.float32
BF16 = jnp.bfloat16
U32 = jnp.uint32


def _rms(x, g):
    ms = jnp.mean(x * x, axis=-1, keepdims=True)
    return (x * lax.rsqrt(ms + RMS_EPS)) * g


def _dot(a, b):
    return jnp.dot(a, b, preferred_element_type=F32)


def _dot_nt(a, b):
    return lax.dot_general(a, b, (((1,), (1,)), ((), ())), preferred_element_type=F32)


def _const_spec(shape):
    nd = len(shape)
    return pl.BlockSpec(shape, lambda *_: (0,) * nd, pipeline_mode=pl.Buffered(1))


def _pack_w_in_body(wt_ref, o_ref):
    chunk = 256

    def copy_rows(dst, src, n):
        for r in range(0, n, chunk):
            m = min(chunk, n - r)
            o_ref[dst + r:dst + r + m, :] = wt_ref[src + r:src + r + m, :].astype(BF16)

    copy_rows(0, 0, W_IN_KR)
    copy_rows(IN_XQ, W_IN_XQ, W_IN_END - W_IN_XQ)
    copy_rows(IN_KR, W_IN_KR, W_IN_XQ - W_IN_KR)
    o_ref[IN_KR + QK_ROPE_DIM:IN_END, :] = jnp.zeros((LANES - QK_ROPE_DIM, D_MODEL), BF16)


def _pack_w_in(w_t):
    whole = pl.BlockSpec(memory_space=pltpu.VMEM)
    return pl.pallas_call(
        _pack_w_in_body,
        in_specs=[whole],
        out_specs=whole,
        out_shape=jax.ShapeDtypeStruct((IN_END, D_MODEL), BF16),
        compiler_params=pltpu.CompilerParams(vmem_limit_bytes=VMEM_LIMIT_BYTES),
        name="pack_w_in",
    )(w_t)


def _mixer_in_body(x_ref, pos_ref, invf_ref, mixg_ref, win_ref, gateb_ref, qg_ref, wuq_ref,
                   kvg_ref, wuk_ref, wuv_ref, poolw_ref, pools_ref,
                   ypool_ref, xq_ref, gates_ref, q_ref, k_ref, v_ref, ext_ref,
                   *, tm, tiles_per_seq, q_scale):
    si = lax.rem(pl.program_id(0), tiles_per_seq)

    @pl.when(pl.program_id(0) == 0)
    def _():
        ext_ref[0:POOL_HALO, :] = jnp.zeros((POOL_HALO, POOL_DIM), F32)

    hb = _rms(x_ref[...], mixg_ref[...]).astype(BF16)

    u = _dot_nt(hb, win_ref[IN_POOL:IN_QD, :])
    ext_ref[0:POOL_HALO, :] = jnp.where(si == 0, 0.0, ext_ref[0:POOL_HALO, :])
    ext_ref[POOL_HALO:POOL_HALO + tm, :] = u

    for c in range(N_BRANCHES):
        gl = _dot_nt(hb, win_ref[IN_GATE + c * D_MODEL:IN_GATE + (c + 1) * D_MODEL, :])
        gates_ref[:, c * D_MODEL:(c + 1) * D_MODEL] = jax.nn.sigmoid(gl + gateb_ref[c:c + 1, :]).astype(BF16)

    ang = pos_ref[...].astype(F32) * invf_ref[...]
    cos = jnp.cos(ang)
    sin = jnp.sin(ang)
    first_half = lax.broadcasted_iota(jnp.int32, (tm, LANES), 1) < (QK_ROPE_DIM // 2)
    sin_signed = jnp.where(first_half, -sin, sin)

    def rope(r):
        swapped = jnp.where(first_half, pltpu.roll(r, LANES - QK_ROPE_DIM // 2, 1),
                            pltpu.roll(r, QK_ROPE_DIM // 2, 1))
        return r * cos + swapped * sin_signed

    cq = _rms(_dot_nt(hb, win_ref[IN_QD:IN_KV, :]), qg_ref[...]).astype(BF16)
    for h in range(MLA_HEADS):
        qh = _dot(cq, wuq_ref[:, h * QK_PAD_DIM:(h + 1) * QK_PAD_DIM])
        q_ref[0, h, :, 0:LANES] = (qh[:, 0:LANES] * q_scale).astype(BF16)
        q_ref[0, h, :, LANES:QK_PAD_DIM] = (rope(qh[:, LANES:QK_PAD_DIM]) * q_scale).astype(BF16)

    ckv = _rms(_dot_nt(hb, win_ref[IN_KV:IN_XQ, :]), kvg_ref[...]).astype(BF16)
    kr = rope(_dot_nt(hb, win_ref[IN_KR:IN_END, :])).astype(BF16)
    for hp in range(MLA_HEADS // 2):
        cols = slice(hp * 2 * LANES, (hp + 1) * 2 * LANES)
        kn = _dot(ckv, wuk_ref[:, cols]).astype(BF16)
        vv = _dot(ckv, wuv_ref[:, cols]).astype(BF16)
        for j in range(2):
            h = 2 * hp + j
            k_ref[0, h, :, 0:LANES] = kn[:, j * LANES:(j + 1) * LANES]
            k_ref[0, h, :, LANES:QK_PAD_DIM] = kr
            v_ref[0, h] = vv[:, j * LANES:(j + 1) * LANES]

    t_seq = lax.broadcasted_iota(jnp.int32, (tm, 1), 0) + si * tm
    for g, w in enumerate(POOL_WINDOWS):
        lo = g * POOL_GROUP_DIM
        hi = lo + POOL_GROUP_DIM
        acc = u[:, lo:hi]
        for j in range(1, w):
            acc = acc + ext_ref[POOL_HALO - j:POOL_HALO - j + tm, lo:hi]
        cnt = jnp.minimum(t_seq + 1, w).astype(F32)
        p = acc / cnt - u[:, lo:hi]
        y = _dot(p.astype(BF16), poolw_ref[g]) * pools_ref[:, lo:hi]
        ypool_ref[:, lo:hi] = y.astype(BF16)
    ext_ref[0:POOL_HALO, :] = ext_ref[tm:tm + POOL_HALO, :]

    xq_ref[...] = _dot_nt(hb, win_ref[IN_XQ:IN_GATE, :]).astype(BF16)


def _mixer_in(x2, pos2, invf, mixg, win_p, gate_b, qg, wuq_p, kvg, wuk, wuv, pool_w, pool_s, *, B, S, tm):
    T = B * S
    tps = S // tm
    q_scale = (QK_NOPE_DIM + QK_ROPE_DIM) ** -0.5 * math.log2(math.e)
    body = functools.partial(_mixer_in_body, tm=tm, tiles_per_seq=tps, q_scale=q_scale)
    row = lambda i: (i, 0)
    head = lambda i: (i // tps, 0, i % tps, 0)
    return pl.pallas_call(
        body,
        grid=(T // tm,),
        in_specs=[
            pl.BlockSpec((tm, D_MODEL), row),
            pl.BlockSpec((tm, 1), row),
            _const_spec((1, LANES)),
            _const_spec((1, D_MODEL)),
            _const_spec((IN_END, D_MODEL)),
            _const_spec((N_BRANCHES, D_MODEL)),
            _const_spec((1, Q_LORA_RANK)),
            _const_spec((Q_LORA_RANK, MLA_HEADS * QK_PAD_DIM)),
            _const_spec((1, KV_LORA_RANK)),
            _const_spec((KV_LORA_RANK, MLA_HEADS * QK_NOPE_DIM)),
            _const_spec((KV_LORA_RANK, MLA_HEADS * V_HEAD_DIM)),
            _const_spec((len(POOL_WINDOWS), POOL_GROUP_DIM, POOL_GROUP_DIM)),
            _const_spec((1, POOL_DIM)),
        ],
        out_specs=[
            pl.BlockSpec((tm, POOL_DIM), row),
            pl.BlockSpec((tm, XATTN_DIM), row),
            pl.BlockSpec((tm, N_BRANCHES * D_MODEL), row),
            pl.BlockSpec((1, MLA_HEADS, tm, QK_PAD_DIM), head),
            pl.BlockSpec((1, MLA_HEADS, tm, QK_PAD_DIM), head),
            pl.BlockSpec((1, MLA_HEADS, tm, V_HEAD_DIM), head),
        ],
        out_shape=[
            jax.ShapeDtypeStruct((T, POOL_DIM), BF16),
            jax.ShapeDtypeStruct((T, XATTN_DIM), BF16),
            jax.ShapeDtypeStruct((T, N_BRANCHES * D_MODEL), BF16),
            jax.ShapeDtypeStruct((B, MLA_HEADS, S, QK_PAD_DIM), BF16),
            jax.ShapeDtypeStruct((B, MLA_HEADS, S, QK_PAD_DIM), BF16),
            jax.ShapeDtypeStruct((B, MLA_HEADS, S, V_HEAD_DIM), BF16),
        ],
        scratch_shapes=[pltpu.VMEM((tm + POOL_HALO, POOL_DIM), F32)],
        compiler_params=pltpu.CompilerParams(dimension_semantics=("arbitrary",),
                                             vmem_limit_bytes=VMEM_LIMIT_BYTES),
        name="mixer_in",
    )(x2, pos2, invf, mixg, win_p, gate_b, qg, wuq_p, kvg, wuk, wuv, pool_w, pool_s)


def _mem_kv_body(mem_ref, g_ref, w_ref, k_ref, v_ref):
    mb = _rms(mem_ref[...], g_ref[...]).astype(BF16)
    kv = _dot(mb, w_ref[...])
    k_ref[...] = kv[:, 0:XATTN_DIM].astype(BF16)
    v_ref[...] = kv[:, XATTN_DIM:2 * XATTN_DIM].astype(BF16)


def _mem_kv(mem2, g, w):
    rows = mem2.shape[0]
    tr = min(rows, 512)
    return pl.pallas_call(
        _mem_kv_body,
        grid=(rows // tr,),
        in_specs=[pl.BlockSpec((tr, D_MODEL), lambda i: (i, 0)),
                  _const_spec((1, D_MODEL)),
                  _const_spec((D_MODEL, 2 * XATTN_DIM))],
        out_specs=[pl.BlockSpec((tr, XATTN_DIM), lambda i: (i, 0)),
                   pl.BlockSpec((tr, XATTN_DIM), lambda i: (i, 0))],
        out_shape=[jax.ShapeDtypeStruct((rows, XATTN_DIM), BF16),
                   jax.ShapeDtypeStruct((rows, XATTN_DIM), BF16)],
        compiler_params=pltpu.CompilerParams(dimension_semantics=("arbitrary",)),
        name="mem_kv",
    )(mem2, g, w)


def _attn_unrolled_body(q_ref, k_ref, v_ref, o_ref, s_a, s_b, mc_a, mc_b, m_ref, l_ref, acc_ref, *, nq, tq):
    s_bufs = (s_a, s_b)
    mc_bufs = (mc_a, mc_b)
    mxu_row_split = 2
    blocks = [(qi, kb) for qi in range(nq) for kb in range(qi + 1)]

    def scores(i, slot):
        qi, kb = blocks[i]
        s = _dot_nt(q_ref[0, 0, qi * tq:(qi + 1) * tq, :], k_ref[0, 0, kb * tq:(kb + 1) * tq, :])
        if qi == kb:
            ri = lax.broadcasted_iota(jnp.int32, (tq, tq), 0)
            ci = lax.broadcasted_iota(jnp.int32, (tq, tq), 1)
            s = jnp.where(ci <= ri, s, NEG_INF)
        s_bufs[slot][...] = s
        mc_bufs[slot][...] = jnp.broadcast_to(jnp.max(s, axis=1, keepdims=True), (tq, LANES))

    def accumulate(i, slot):
        qi, kb = blocks[i]
        is_first = kb == 0
        is_last = kb == qi
        if is_first:
            m_new = mc_bufs[slot][...]
        else:
            m_prev = m_ref[...]
            m_new = jnp.maximum(m_prev, mc_bufs[slot][...])
            alpha = jnp.exp2(m_prev - m_new)
        p = jnp.exp2(s_bufs[slot][...] - jnp.concatenate([m_new] * (tq // LANES), axis=1))
        psum = p[:, 0:LANES]
        for c in range(1, tq // LANES):
            psum = psum + p[:, c * LANES:(c + 1) * LANES]
        l_new = psum if is_first else alpha * l_ref[...] + psum
        pb = p.astype(BF16)
        v = v_ref[0, 0, kb * tq:(kb + 1) * tq, :]
        if is_last:
            inv = 1.0 / jnp.sum(l_new, axis=1, keepdims=True)
        else:
            l_ref[...] = l_new
            m_ref[...] = m_new
        h = tq // mxu_row_split
        for r in range(mxu_row_split):
            rows = slice(r * h, (r + 1) * h)
            acc = _dot(pb[rows, :], v)
            if not is_first:
                acc = alpha[rows, :] * acc_ref[rows, :] + acc
            if is_last:
                o_ref[0, qi * tq + r * h:qi * tq + (r + 1) * h, :] = (acc * inv[rows, :]).astype(BF16)
            else:
                acc_ref[rows, :] = acc

    scores(0, 0)
    for i in range(len(blocks)):
        if i + 1 < len(blocks):
            scores(i + 1, (i + 1) % 2)
        accumulate(i, i % 2)


def _mla_attn_unrolled(q, k, v, *, tq):
    B, H, S, _ = q.shape
    per_head = lambda b, h: (b, h, 0, 0)
    return pl.pallas_call(
        functools.partial(_attn_unrolled_body, nq=S // tq, tq=tq),
        grid=(B, H),
        in_specs=[pl.BlockSpec((1, 1, S, QK_PAD_DIM), per_head),
                  pl.BlockSpec((1, 1, S, QK_PAD_DIM), per_head),
                  pl.BlockSpec((1, 1, S, V_HEAD_DIM), per_head)],
        out_specs=pl.BlockSpec((1, S, V_HEAD_DIM), lambda b, h: (b, 0, h)),
        out_shape=jax.ShapeDtypeStruct((B, S, H * V_HEAD_DIM), BF16),
        scratch_shapes=[pltpu.VMEM((tq, tq), F32), pltpu.VMEM((tq, tq), F32),
                        pltpu.VMEM((tq, LANES), F32), pltpu.VMEM((tq, LANES), F32),
                        pltpu.VMEM((tq, LANES), F32), pltpu.VMEM((tq, LANES), F32),
                        pltpu.VMEM((tq, V_HEAD_DIM), F32)],
        compiler_params=pltpu.CompilerParams(dimension_semantics=("arbitrary", "arbitrary"),
                                             vmem_limit_bytes=VMEM_LIMIT_BYTES),
        name="mla_attn",
    )(q, k, v)


def _merge_body(x_ref, ypool_ref, ymla_ref, xq_ref, gates_ref, kmem_ref, vmem_ref,
                wbp_ref, wbm_ref, wbx_ref, wout_ref, ffng_ref, wr_ref, br_ref,
                x1_ref, h2_ref, wts_ref, loc_ref, cnt_tab_ref, carry_tab_ref, counts_ref, carry_ref, *, tm):
    @pl.when(pl.program_id(0) == 0)
    def _():
        carry_ref[...] = jnp.zeros((N_EXPERTS, LANES), F32)

    xq = xq_ref[...]
    parts = []
    for h in range(XATTN_HEADS):
        cols = slice(h * XATTN_HEAD_DIM, (h + 1) * XATTN_HEAD_DIM)
        s = _dot_nt(xq[:, cols], kmem_ref[:, cols]) * (XATTN_HEAD_DIM ** -0.5)
        e = jnp.exp(s - jnp.max(s, axis=1, keepdims=True))
        p = e / jnp.sum(e, axis=1, keepdims=True)
        parts.append(_dot(p.astype(BF16), vmem_ref[:, cols]))
    ymem = jnp.concatenate(parts, axis=1).astype(BF16)

    gates = gates_ref[...].astype(F32)
    merged = (gates[:, 0:D_MODEL] * _dot(ypool_ref[...], wbp_ref[...])
              + gates[:, D_MODEL:2 * D_MODEL] * _dot(ymla_ref[...], wbm_ref[...])
              + gates[:, 2 * D_MODEL:3 * D_MODEL] * _dot(ymem, wbx_ref[...]))
    x1 = x_ref[...] + _dot(merged.astype(BF16), wout_ref[...])
    x1_ref[...] = x1
    h2 = _rms(x1, ffng_ref[...]).astype(BF16)
    h2_ref[...] = h2

    lt = _dot_nt(wr_ref[...], h2) + br_ref[...]
    gl = lt[N_EXPERTS:N_EXPERTS + N_GROUPS, :]
    gmax = jnp.max(gl, axis=0, keepdims=True)
    r4 = lax.broadcasted_iota(jnp.int32, (N_GROUPS, tm), 0).astype(F32)
    gidx = jnp.min(jnp.where(gl == gmax, r4, float(N_GROUPS)), axis=0, keepdims=True)
    pg = 1.0 / jnp.sum(jnp.exp(gl - gmax), axis=0, keepdims=True)
    esel = lt[0:EXPERTS_PER_GROUP, :]
    for g in range(1, N_GROUPS):
        esel = jnp.where(gidx == float(g), lt[g * EXPERTS_PER_GROUP:(g + 1) * EXPERTS_PER_GROUP, :], esel)
    r8 = lax.broadcasted_iota(jnp.int32, (EXPERTS_PER_GROUP, tm), 0).astype(F32)
    m1 = jnp.max(esel, axis=0, keepdims=True)
    i1 = jnp.min(jnp.where(esel == m1, r8, float(EXPERTS_PER_GROUP)), axis=0, keepdims=True)
    rest = jnp.where(r8 == i1, -jnp.inf, esel)
    m2 = jnp.max(rest, axis=0, keepdims=True)
    i2 = jnp.min(jnp.where(rest == m2, r8, float(EXPERTS_PER_GROUP)), axis=0, keepdims=True)
    e2 = jnp.exp(m2 - m1)
    den = 1.0 + e2
    wts_ref[0:1, :] = pg / den
    wts_ref[1:2, :] = pg * e2 / den
    ex1 = gidx * float(EXPERTS_PER_GROUP) + i1
    ex2 = gidx * float(EXPERTS_PER_GROUP) + i2

    r32 = lax.broadcasted_iota(jnp.int32, (N_EXPERTS, tm), 0).astype(F32)
    is1 = r32 == ex1
    is2 = r32 == ex2
    member = jnp.where(is1 | is2, 1.0, 0.0)
    upper = jnp.where(lax.broadcasted_iota(jnp.int32, (tm, tm), 0)
                      <= lax.broadcasted_iota(jnp.int32, (tm, tm), 1), 1.0, 0.0).astype(BF16)
    incl = _dot(member.astype(BF16), upper)
    run = jnp.floor((jnp.sum(member, axis=1, keepdims=True) + (RUN_ALIGN - 1)) / RUN_ALIGN) * RUN_ALIGN
    rcol = lax.broadcasted_iota(jnp.int32, (N_EXPERTS, 1), 0)
    run_start = jnp.zeros((N_EXPERTS, 1), F32)
    for e in range(N_EXPERTS - 1):
        run_start = run_start + jnp.where(rcol > e, run[e:e + 1, :], 0.0)
    pos = incl - 1.0 + run_start
    loc_ref[0:1, :] = jnp.sum(jnp.where(is1, pos, 0.0), axis=0, keepdims=True).astype(jnp.int32)
    loc_ref[1:2, :] = jnp.sum(jnp.where(is2, pos, 0.0), axis=0, keepdims=True).astype(jnp.int32)
    carry = carry_ref[...]
    total = carry + run
    cnt_tab_ref[...] = jnp.broadcast_to(run, (N_EXPERTS, LANES)).astype(jnp.int32)
    carry_tab_ref[...] = carry.astype(jnp.int32)
    carry_ref[...] = total
    counts_ref[...] = total.astype(jnp.int32)


def _merge(x2, ypool, ymla, xq, gates, kmem, vmem, wbp, wbm, wbx, wout, ffng, wr, br, *, B, S, tm, mem_len):
    T = B * S
    tps = S // tm
    row = lambda i: (i, 0)
    lane = lambda i: (0, i)
    memb = lambda i: (i // tps, 0)
    return pl.pallas_call(
        functools.partial(_merge_body, tm=tm),
        grid=(T // tm,),
        in_specs=[
            pl.BlockSpec((tm, D_MODEL), row),
            pl.BlockSpec((tm, POOL_DIM), row),
            pl.BlockSpec((tm, MLA_HEADS * V_HEAD_DIM), row),
            pl.BlockSpec((tm, XATTN_DIM), row),
            pl.BlockSpec((tm, N_BRANCHES * D_MODEL), row),
            pl.BlockSpec((mem_len, XATTN_DIM), memb),
            pl.BlockSpec((mem_len, XATTN_DIM), memb),
            _const_spec((POOL_DIM, D_MODEL)),
            _const_spec((MLA_HEADS * V_HEAD_DIM, D_MODEL)),
            _const_spec((XATTN_DIM, D_MODEL)),
            _const_spec((D_MODEL, D_MODEL)),
            _const_spec((1, D_MODEL)),
            _const_spec((ROUTER_ROWS, D_MODEL)),
            _const_spec((ROUTER_ROWS, 1)),
        ],
        out_specs=[
            pl.BlockSpec((tm, D_MODEL), row),
            pl.BlockSpec((tm, D_MODEL), row),
            pl.BlockSpec((2, tm), lane),
            pl.BlockSpec((2, tm), lane),
            pl.BlockSpec((N_EXPERTS, LANES), row),
            pl.BlockSpec((N_EXPERTS, LANES), row),
            pl.BlockSpec((N_EXPERTS, LANES), lambda i: (0, 0)),
        ],
        out_shape=[
            jax.ShapeDtypeStruct((T, D_MODEL), F32),
            jax.ShapeDtypeStruct((T, D_MODEL), BF16),
            jax.ShapeDtypeStruct((2, T), F32),
            jax.ShapeDtypeStruct((2, T), jnp.int32),
            jax.ShapeDtypeStruct((T // tm * N_EXPERTS, LANES), jnp.int32),
            jax.ShapeDtypeStruct((T // tm * N_EXPERTS, LANES), jnp.int32),
            jax.ShapeDtypeStruct((N_EXPERTS, LANES), jnp.int32),
        ],
        scratch_shapes=[pltpu.VMEM((N_EXPERTS, LANES), F32)],
        compiler_params=pltpu.CompilerParams(dimension_semantics=("arbitrary",),
                                             vmem_limit_bytes=VMEM_LIMIT_BYTES),
        name="merge",
    )(x2, ypool, ymla, xq, gates, kmem, vmem, wbp, wbm, wbx, wout, ffng, wr, br)


def _moe_pos_body(counts_ref, meta_ref):
    shift = int(math.log2(MOE_ROWS))
    cnt = counts_ref[...]
    padded = lax.shift_left(lax.shift_right_logical(cnt + (MOE_ROWS - 1), shift), shift)
    r32 = lax.broadcasted_iota(jnp.int32, (N_EXPERTS, LANES), 0)
    pad_start = jnp.zeros((N_EXPERTS, LANES), jnp.int32)
    for e in range(N_EXPERTS - 1):
        pad_start = pad_start + jnp.where(r32 > e, padded[e:e + 1, :], 0)
    pad_end = pad_start + padded

    lane = lax.broadcasted_iota(jnp.int32, (1, META_LANES), 1)
    block_row = lane * MOE_ROWS
    blk_e = jnp.zeros((1, META_LANES), jnp.int32)
    pe_row = jnp.zeros((1, META_LANES), jnp.int32)
    for e in range(N_EXPERTS):
        pe = pad_end[e:e + 1, 0:1]
        blk_e = blk_e + jnp.where(pe <= block_row, 1, 0)
        pe_row = pe_row + jnp.where(lane == META_PAD_END + e, pe, 0)
    blk_e = jnp.minimum(blk_e, N_EXPERTS - 1)
    nact = lax.shift_right_logical(pad_end[N_EXPERTS - 1:N_EXPERTS, 0:1], shift)
    meta = jnp.where(lane < META_PAD_END, blk_e, pe_row)
    meta_ref[...] = jnp.where(lane == META_NACT, nact, meta)


def _moe_pos(counts):
    full = lambda shape: pl.BlockSpec(shape, lambda i: (0,) * len(shape))
    return pl.pallas_call(
        _moe_pos_body,
        grid=(1,),
        in_specs=[full((N_EXPERTS, LANES))],
        out_specs=full((1, META_LANES)),
        out_shape=jax.ShapeDtypeStruct((1, META_LANES), jnp.int32),
        compiler_params=pltpu.CompilerParams(dimension_semantics=("arbitrary",)),
        name="moe_pos",
    )(counts)


def _pack_bf16_pairs(x):
    lo = pltpu.bitcast(x[:, 0:PACKED_DIM], U32)
    hi = pltpu.bitcast(x[:, PACKED_DIM:D_MODEL], U32)
    return hi | lax.shift_right_logical(lo, jnp.uint32(16))


def _unpack_bf16_pairs(w):
    lo = pltpu.bitcast(lax.shift_left(w, jnp.uint32(16)), F32)
    hi = pltpu.bitcast(w & jnp.uint32(0xFFFF0000), F32)
    return jnp.concatenate([lo, hi], axis=1)


def _loc_rows(tm):
    return 2 * tm + N_EXPERTS * RUN_ALIGN


def _run_copies(tile, cnt_ref, carry_ref, meta_ref, make_copy):
    def per_expert(e, local):
        n = pl.multiple_of(cnt_ref[tile * N_EXPERTS + e], RUN_ALIGN)
        start = jnp.where(e == 0, 0, meta_ref[META_PAD_END + jnp.maximum(e - 1, 0)])
        glob = pl.multiple_of(start + carry_ref[tile * N_EXPERTS + e], RUN_ALIGN)

        @pl.when(n > 0)
        def _():
            make_copy(pl.multiple_of(local, RUN_ALIGN), glob, n).start()

        return local + n

    return pl.multiple_of(lax.fori_loop(0, N_EXPERTS, per_expert, 0), RUN_ALIGN)


def _tile_rows(tile, cnt_ref):
    total = lax.fori_loop(0, N_EXPERTS, lambda e, t: t + cnt_ref[tile * N_EXPERTS + e], 0)
    return pl.multiple_of(total, RUN_ALIGN)


def _dispatch_body(meta_ref, cnt_ref, carry_ref, loc_ref, h2_ref, xs_ref, xloc_ref, zero_ref, sems, zsem,
                   *, tm, n_blocks):
    tile = pl.program_id(0)
    last_tile = pl.num_programs(0) - 1
    slot = lax.rem(tile, 2)
    nact = meta_ref[META_NACT]

    def wait_rows(t, s):
        n = _tile_rows(t, cnt_ref)
        pltpu.make_async_copy(xloc_ref.at[s, pl.ds(0, n)], xs_ref.at[pl.ds(0, n)], sems.at[s]).wait()

    def pad_copy(e):
        end = pl.multiple_of(meta_ref[META_PAD_END + e], MOE_ROWS)
        start = jnp.where(e == 0, 0, meta_ref[META_PAD_END + jnp.maximum(e - 1, 0)])
        used = carry_ref[last_tile * N_EXPERTS + e] + cnt_ref[last_tile * N_EXPERTS + e]
        first = pl.multiple_of(start + used, RUN_ALIGN)
        n = pl.multiple_of(end - first, RUN_ALIGN)
        return n, pltpu.make_async_copy(zero_ref.at[pl.ds(0, n)], xs_ref.at[pl.ds(first, n)], zsem)

    def tail_copy(b):
        return pltpu.make_async_copy(
            zero_ref, xs_ref.at[pl.ds(pl.multiple_of(b * MOE_ROWS, MOE_ROWS), MOE_ROWS)], zsem)

    def fill(op):
        def pad(e, c):
            n, cp = pad_copy(e)

            @pl.when(n > 0)
            def _():
                op(cp)
            return c

        def tail(b, c):
            op(tail_copy(b))
            return c

        lax.fori_loop(0, N_EXPERTS, pad, 0)
        lax.fori_loop(nact, n_blocks, tail, 0)

    @pl.when(tile == 0)
    def _():
        zero_ref[...] = jnp.zeros((MOE_ROWS, PACKED_DIM), U32)
        fill(lambda cp: cp.start())

    @pl.when(tile >= 2)
    def _():
        wait_rows(tile - 2, slot)

    r = lax.broadcasted_iota(jnp.int32, (_loc_rows(tm), tm), 0)
    onehot = jnp.where((r == loc_ref[0:1, :]) | (r == loc_ref[1:2, :]), 1.0, 0.0).astype(BF16)
    xloc_ref[slot] = _pack_bf16_pairs(_dot(onehot, h2_ref[...]))

    def make_copy(local, glob, n):
        return pltpu.make_async_copy(xloc_ref.at[slot, pl.ds(local, n)], xs_ref.at[pl.ds(glob, n)],
                                     sems.at[slot])

    _run_copies(tile, cnt_ref, carry_ref, meta_ref, make_copy)

    @pl.when(tile == last_tile)
    def _():
        @pl.when(tile >= 1)
        def _():
            wait_rows(tile - 1, 1 - slot)

        wait_rows(tile, slot)
        fill(lambda cp: cp.wait())


def _dispatch(meta1, cnt_tab, carry_tab, loc, h2, *, R, tm):
    T = h2.shape[0]
    return pl.pallas_call(
        functools.partial(_dispatch_body, tm=tm, n_blocks=R // MOE_ROWS),
        grid_spec=pltpu.PrefetchScalarGridSpec(
            num_scalar_prefetch=3,
            grid=(T // tm,),
            in_specs=[pl.BlockSpec((2, tm), lambda i, *_: (0, i)),
                      pl.BlockSpec((tm, D_MODEL), lambda i, *_: (i, 0))],
            out_specs=pl.BlockSpec(memory_space=pl.ANY),
            scratch_shapes=[pltpu.VMEM((2, _loc_rows(tm), PACKED_DIM), U32),
                            pltpu.VMEM((MOE_ROWS, PACKED_DIM), U32),
                            pltpu.SemaphoreType.DMA((2,)), pltpu.SemaphoreType.DMA],
        ),
        out_shape=jax.ShapeDtypeStruct((R, PACKED_DIM), U32),
        compiler_params=pltpu.CompilerParams(dimension_semantics=("arbitrary",),
                                             vmem_limit_bytes=VMEM_LIMIT_BYTES),
        name="dispatch",
    )(meta1, cnt_tab, carry_tab, loc, h2)


def _moe_ffn_body(meta_ref, xs_ref, wg_hbm, wu_hbm, wd_hbm, ys_ref,
                  xbuf, ybuf, wg_stage, wu_stage, wd_stage, wg_b, wu_b, wd_b, zero_ref,
                  xsem, ysem, wsem, zsem, *, n_blocks):
    nact = meta_ref[META_NACT]
    shift = int(math.log2(MOE_ROWS))

    def rows_of(b):
        return pl.ds(pl.multiple_of(b * MOE_ROWS, MOE_ROWS), MOE_ROWS)

    def x_copy(b, s):
        return pltpu.make_async_copy(xs_ref.at[rows_of(b)], xbuf.at[s], xsem.at[s])

    def y_copy(b, s):
        return pltpu.make_async_copy(ybuf.at[s], ys_ref.at[rows_of(b)], ysem.at[s])

    def w_copies(e, s):
        return (pltpu.make_async_copy(wg_hbm.at[e], wg_stage.at[s], wsem.at[s]),
                pltpu.make_async_copy(wu_hbm.at[e], wu_stage.at[s], wsem.at[s]),
                pltpu.make_async_copy(wd_hbm.at[e], wd_stage.at[s], wsem.at[s]))

    def tail_copy(b):
        return pltpu.make_async_copy(zero_ref, ys_ref.at[rows_of(b)], zsem)

    zero_ref[...] = jnp.zeros((MOE_ROWS, PACKED_DIM), U32)
    lax.fori_loop(nact, n_blocks, lambda b, c: (tail_copy(b).start(), c)[1], 0)

    def next_expert_block(e):
        return lax.shift_right_logical(meta_ref[META_PAD_END + e], shift)

    def start_weights(b, s):
        @pl.when(b < nact)
        def _():
            for cp in w_copies(meta_ref[jnp.minimum(b, n_blocks - 1)], s):
                cp.start()

    x_copy(0, 0).start()

    @pl.when(nact > 1)
    def _():
        x_copy(1, 1).start()

    e_first = meta_ref[0]
    start_weights(0, 0)
    start_weights(next_expert_block(e_first), 1)

    def block(b, xs, ys, k_prev):
        valid = b < nact
        e = meta_ref[jnp.minimum(b, nact - 1)]
        changed = jnp.logical_and(valid, jnp.logical_or(b == 0, e != meta_ref[jnp.maximum(b - 1, 0)]))
        k = jnp.where(changed, k_prev + 1, k_prev)

        @pl.when(changed)
        def _():
            ws = lax.rem(k, W_STAGES)
            for cp in w_copies(e, ws):
                cp.wait()
            wg_b[...] = wg_stage[ws].astype(BF16)
            wu_b[...] = wu_stage[ws].astype(BF16)
            wd_b[...] = wd_stage[ws].astype(BF16)
            n1 = next_expert_block(e)
            e1 = meta_ref[jnp.minimum(n1, n_blocks - 1)]
            n2 = jnp.where(n1 < nact, next_expert_block(e1), n_blocks)
            start_weights(n2, lax.rem(k + 2, W_STAGES))

        @pl.when(valid)
        def _():
            x_copy(b, xs).wait()

            @pl.when(b + 2 < nact)
            def _():
                x_copy(b + 2, (xs + 2) % X_SLOTS).start()

            @pl.when(b >= 2)
            def _():
                y_copy(b - 2, ys).wait()

            for c in range(MOE_ROWS // MOE_CHUNK):
                rows = slice(c * MOE_CHUNK, (c + 1) * MOE_CHUNK)
                x = _unpack_bf16_pairs(xbuf[xs, rows, :]).astype(BF16)
                g = _dot(x, wg_b[...])
                a = (g * jax.nn.sigmoid(g)) * _dot(x, wu_b[...])
                y = _dot(a.astype(BF16), wd_b[...])
                ybuf[ys, rows, :] = _pack_bf16_pairs(y.astype(BF16).astype(F32))
            y_copy(b, ys).start()

        return k

    def quad(i, k):
        for j in range(X_SLOTS):
            k = block(X_SLOTS * i + j, j, j % 2, k)
        return k

    lax.fori_loop(0, lax.div(nact + (X_SLOTS - 1), X_SLOTS), quad, -1)

    @pl.when(nact >= 2)
    def _():
        y_copy(nact - 2, lax.rem(nact, 2)).wait()

    y_copy(nact - 1, lax.rem(nact - 1, 2)).wait()
    lax.fori_loop(nact, n_blocks, lambda b, c: (tail_copy(b).wait(), c)[1], 0)


def _moe_ffn(meta1, xs, wg, wu, wd):
    R = xs.shape[0]
    hbm = pl.BlockSpec(memory_space=pl.ANY)
    return pl.pallas_call(
        functools.partial(_moe_ffn_body, n_blocks=R // MOE_ROWS),
        grid_spec=pltpu.PrefetchScalarGridSpec(
            num_scalar_prefetch=1,
            grid=(1,),
            in_specs=[hbm, hbm, hbm, hbm],
            out_specs=hbm,
            scratch_shapes=[pltpu.VMEM((X_SLOTS, MOE_ROWS, PACKED_DIM), U32),
                            pltpu.VMEM((2, MOE_ROWS, PACKED_DIM), U32),
                            pltpu.VMEM((W_STAGES, D_MODEL, D_EXPERT), F32),
                            pltpu.VMEM((W_STAGES, D_MODEL, D_EXPERT), F32),
                            pltpu.VMEM((W_STAGES, D_EXPERT, D_MODEL), F32),
                            pltpu.VMEM((D_MODEL, D_EXPERT), BF16),
                            pltpu.VMEM((D_MODEL, D_EXPERT), BF16),
                            pltpu.VMEM((D_EXPERT, D_MODEL), BF16),
                            pltpu.VMEM((MOE_ROWS, PACKED_DIM), U32),
                            pltpu.SemaphoreType.DMA((X_SLOTS,)), pltpu.SemaphoreType.DMA((2,)),
                            pltpu.SemaphoreType.DMA((W_STAGES,)), pltpu.SemaphoreType.DMA],
        ),
        out_shape=jax.ShapeDtypeStruct((R, PACKED_DIM), U32),
        compiler_params=pltpu.CompilerParams(dimension_semantics=("arbitrary",),
                                             vmem_limit_bytes=VMEM_LIMIT_BYTES),
        name="moe_ffn",
    )(meta1, xs, wg, wu, wd)


def _combine_body(meta_ref, cnt_ref, carry_ref, x1_ref, loc_ref, wts_ref, fg_ref, ys_ref, out_ref,
                  yloc_ref, sems, *, tm):
    tile = pl.program_id(0)
    slot = lax.rem(tile, 2)

    def fetch(t, s):
        def make_copy(local, glob, n):
            return pltpu.make_async_copy(ys_ref.at[pl.ds(glob, n)], yloc_ref.at[s, pl.ds(local, n)],
                                         sems.at[s])
        _run_copies(t, cnt_ref, carry_ref, meta_ref, make_copy)

    @pl.when(tile == 0)
    def _():
        yloc_ref[...] = jnp.zeros(yloc_ref.shape, U32)
        fetch(tile, slot)

    @pl.when(tile + 1 < pl.num_programs(0))
    def _():
        fetch(tile + 1, 1 - slot)

    n = _tile_rows(tile, cnt_ref)
    pltpu.make_async_copy(ys_ref.at[pl.ds(0, n)], yloc_ref.at[slot, pl.ds(0, n)], sems.at[slot]).wait()
    r = lax.broadcasted_iota(jnp.int32, (_loc_rows(tm), tm), 0)
    is0 = r == loc_ref[0:1, :]
    is1 = r == loc_ref[1:2, :]
    row_w = jnp.sum(jnp.where(is0, wts_ref[0:1, :], 0.0) + jnp.where(is1, wts_ref[1:2, :], 0.0),
                    axis=1, keepdims=True)
    yw = (row_w * _unpack_bf16_pairs(yloc_ref[slot])).astype(BF16)
    twohot = jnp.where(is0 | is1, 1.0, 0.0).astype(BF16)
    moe = lax.dot_general(twohot, yw, (((0,), (0,)), ((), ())), preferred_element_type=F32)
    out_ref[...] = _rms(x1_ref[...] + moe, fg_ref[...])


def _combine(meta1, cnt_tab, carry_tab, x1, loc, wts, fg, ys, *, tm):
    T = x1.shape[0]
    row = lambda i, *_: (i, 0)
    lane = lambda i, *_: (0, i)
    return pl.pallas_call(
        functools.partial(_combine_body, tm=tm),
        grid_spec=pltpu.PrefetchScalarGridSpec(
            num_scalar_prefetch=3,
            grid=(T // tm,),
            in_specs=[pl.BlockSpec((tm, D_MODEL), row),
                      pl.BlockSpec((2, tm), lane),
                      pl.BlockSpec((2, tm), lane),
                      pl.BlockSpec((1, D_MODEL), lambda i, *_: (0, 0)),
                      pl.BlockSpec(memory_space=pl.ANY)],
            out_specs=pl.BlockSpec((tm, D_MODEL), row),
            scratch_shapes=[pltpu.VMEM((2, _loc_rows(tm), PACKED_DIM), U32),
                            pltpu.SemaphoreType.DMA((2,))],
        ),
        out_shape=jax.ShapeDtypeStruct((T, D_MODEL), F32),
        compiler_params=pltpu.CompilerParams(dimension_semantics=("arbitrary",),
                                             vmem_limit_bytes=VMEM_LIMIT_BYTES),
        name="combine",
    )(meta1, cnt_tab, carry_tab, x1, loc, wts, fg, ys)


def _tile(n, t):
    t = min(n, t)
    assert n % t == 0, (n, t)
    return t


def kernel(x, mem, positions, mix_norm_g, w_in, gate_b, q_norm_g, w_uq, kv_norm_g, w_uk, w_uv, pool_w, pool_scale, mem_norm_g, w_mem_kv, w_br_pool, w_br_mla, w_br_mem, w_out, ffn_norm_g, w_router_group, b_router_group, w_router_expert, b_router_expert, w_gate_e, w_up_e, w_down_e, final_norm_g):
    B, S, D = x.shape
    assert D == D_MODEL and mix_norm_g.shape[0] == 1
    T = B * S
    mem_len = mem.shape[1]
    tm = _tile(S, 512)
    l = 0

    win_p = _pack_w_in(jnp.swapaxes(w_in, 1, 2).reshape(W_IN_END, D_MODEL))
    wuq_p = jnp.pad(w_uq[l].reshape(Q_LORA_RANK, MLA_HEADS, QK_NOPE_DIM + QK_ROPE_DIM),
                    ((0, 0), (0, 0), (0, QK_PAD_DIM - QK_NOPE_DIM - QK_ROPE_DIM))
                    ).reshape(Q_LORA_RANK, MLA_HEADS * QK_PAD_DIM).astype(BF16)
    inv_freq = 1.0 / (ROPE_THETA ** (jnp.arange(0, QK_ROPE_DIM, 2, dtype=F32) / QK_ROPE_DIM))
    invf = jnp.concatenate([inv_freq, inv_freq, jnp.zeros((LANES - QK_ROPE_DIM,), F32)])[None, :]
    wr = jnp.concatenate([w_router_expert[l], w_router_group[l],
                          jnp.zeros((D_MODEL, ROUTER_ROWS - N_EXPERTS - N_GROUPS), F32)], axis=1).T.astype(BF16)
    br = jnp.concatenate([b_router_expert[l], b_router_group[l],
                          jnp.zeros((ROUTER_ROWS - N_EXPERTS - N_GROUPS,), F32)])[:, None].astype(F32)
    x2 = x.reshape(T, D_MODEL)
    pos2 = positions.reshape(T, 1)

    ypool, xq, gates, q, k, v = _mixer_in(
        x2, pos2, invf, mix_norm_g[l][None, :], win_p, gate_b[l], q_norm_g[l][None, :], wuq_p,
        kv_norm_g[l][None, :], w_uk[l].astype(BF16), w_uv[l].astype(BF16), pool_w[l].astype(BF16),
        pool_scale[l][None, :], B=B, S=S, tm=tm)
    kmem, vmem = _mem_kv(mem.reshape(B * mem_len, D_MODEL), mem_norm_g[l][None, :], w_mem_kv[l].astype(BF16))
    ymla = _mla_attn_unrolled(q, k, v, tq=tm).reshape(T, MLA_HEADS * V_HEAD_DIM)
    x1, h2, wts, loc, cnt_tab, carry_tab, counts = _merge(
        x2, ypool, ymla, xq, gates, kmem, vmem, w_br_pool[l].astype(BF16), w_br_mla[l].astype(BF16),
        w_br_mem[l].astype(BF16), w_out[l].astype(BF16), ffn_norm_g[l][None, :], wr, br,
        B=B, S=S, tm=tm, mem_len=mem_len)

    R = 2 * T + (T // tm) * N_EXPERTS * RUN_ALIGN + N_EXPERTS * MOE_ROWS
    assert R % MOE_ROWS == 0 and R // MOE_ROWS <= META_PAD_END
    meta1 = _moe_pos(counts).reshape(META_LANES)
    cnt1 = cnt_tab[:, 0]
    carry1 = carry_tab[:, 0]
    xs = _dispatch(meta1, cnt1, carry1, loc, h2, R=R, tm=tm)
    ys = _moe_ffn(meta1, xs, w_gate_e[l], w_up_e[l], w_down_e[l])
    out = _combine(meta1, cnt1, carry1, x1, loc, wts, final_norm_g[None, :], ys, tm=tm)
    return out.reshape(B, S, D_MODEL)
```

```python
import functools
import math

import jax
import jax.numpy as jnp
from jax import lax
from jax.experimental import pallas as pl
from jax.experimental.pallas import tpu as pltpu

D_MODEL = 1024
POOL_WINDOWS = (2, 4, 8, 16)
POOL_GROUP_DIM = 128
POOL_DIM = 512
MLA_HEADS = 8
QK_NOPE_DIM = 128
QK_ROPE_DIM = 64
V_HEAD_DIM = 128
Q_LORA_RANK = 384
KV_LORA_RANK = 256
ROPE_THETA = 10000.0
XATTN_HEADS = 4
XATTN_HEAD_DIM = 128
XATTN_DIM = 512
N_BRANCHES = 3
N_GROUPS = 4
EXPERTS_PER_GROUP = 8
N_EXPERTS = 32
D_EXPERT = 256
RMS_EPS = 1e-6
NEG_INF = -1e30

LANES = 128
QK_PAD_DIM = 2 * LANES
POOL_HALO = 16
MOE_ROWS = 512
MOE_CHUNK = 256
X_SLOTS = 4
W_STAGES = 3
RUN_ALIGN = 8
PACKED_DIM = D_MODEL // 2
ROUTER_ROWS = 40
META_LANES = 256
META_PAD_END = 192
META_NACT = 255
VMEM_LIMIT_BYTES = 56 * 1024 * 1024

IN_POOL, IN_QD, IN_KV, IN_XQ, IN_GATE, IN_KR, IN_END = 0, 512, 896, 1152, 1664, 4736, 4864
W_IN_KR, W_IN_XQ, W_IN_END = 1152, 1216, 4800

F32 = jnp.float32
BF16 = jnp.bfloat16
U32 = jnp.uint32


def _rms(x, g):
    ms = jnp.mean(x * x, axis=-1, keepdims=True)
    return (x * lax.rsqrt(ms + RMS_EPS)) * g


def _dot(a, b):
    return jnp.dot(a, b, preferred_element_type=F32)


def _dot_nt(a, b):
    return lax.dot_general(a, b, (((1,), (1,)), ((), ())), preferred_element_type=F32)


def _const_spec(shape):
    nd = len(shape)
    return pl.BlockSpec(shape, lambda *_: (0,) * nd, pipeline_mode=pl.Buffered(1))


def _pack_w_in_body(wt_ref, o_ref):
    chunk = 256

    def copy_rows(dst, src, n):
        for r in range(0, n, chunk):
            m = min(chunk, n - r)
            o_ref[dst + r:dst + r + m, :] = wt_ref[src + r:src + r + m, :].astype(BF16)

    copy_rows(0, 0, W_IN_KR)
    copy_rows(IN_XQ, W_IN_XQ, W_IN_END - W_IN_XQ)
    copy_rows(IN_KR, W_IN_KR, W_IN_XQ - W_IN_KR)
    o_ref[IN_KR + QK_ROPE_DIM:IN_END, :] = jnp.zeros((LANES - QK_ROPE_DIM, D_MODEL), BF16)


def _pack_w_in(w_t):
    whole = pl.BlockSpec(memory_space=pltpu.VMEM)
    return pl.pallas_call(
        _pack_w_in_body,
        in_specs=[whole],
        out_specs=whole,
        out_shape=jax.ShapeDtypeStruct((IN_END, D_MODEL), BF16),
        compiler_params=pltpu.CompilerParams(vmem_limit_bytes=VMEM_LIMIT_BYTES),
        name="pack_w_in",
    )(w_t)


def _mixer_in_body(x_ref, pos_ref, invf_ref, mixg_ref, win_ref, gateb_ref, qg_ref, wuq_ref,
                   kvg_ref, wuk_ref, wuv_ref, poolw_ref, pools_ref,
                   ypool_ref, xq_ref, gates_ref, q_ref, k_ref, v_ref, ext_ref,
                   *, tm, tiles_per_seq, q_scale):
    si = lax.rem(pl.program_id(0), tiles_per_seq)

    @pl.when(pl.program_id(0) == 0)
    def _():
        ext_ref[0:POOL_HALO, :] = jnp.zeros((POOL_HALO, POOL_DIM), F32)

    hb = _rms(x_ref[...], mixg_ref[...]).astype(BF16)

    u = _dot_nt(hb, win_ref[IN_POOL:IN_QD, :])
    ext_ref[0:POOL_HALO, :] = jnp.where(si == 0, 0.0, ext_ref[0:POOL_HALO, :])
    ext_ref[POOL_HALO:POOL_HALO + tm, :] = u

    for c in range(N_BRANCHES):
        gl = _dot_nt(hb, win_ref[IN_GATE + c * D_MODEL:IN_GATE + (c + 1) * D_MODEL, :])
        gates_ref[:, c * D_MODEL:(c + 1) * D_MODEL] = jax.nn.sigmoid(gl + gateb_ref[c:c + 1, :]).astype(BF16)

    ang = pos_ref[...].astype(F32) * invf_ref[...]
    cos = jnp.cos(ang)
    sin = jnp.sin(ang)
    lane = lax.broadcasted_iota(jnp.int32, (tm, LANES), 1)
    first_half = (lane & (QK_ROPE_DIM - 1)) < (QK_ROPE_DIM // 2)
    low_group = lane < QK_ROPE_DIM
    sin_signed = jnp.where(first_half, -sin, sin)

    def rope(r):
        swapped = jnp.where(first_half, pltpu.roll(r, LANES - QK_ROPE_DIM // 2, 1),
                            pltpu.roll(r, QK_ROPE_DIM // 2, 1))
        return r * cos + swapped * sin_signed

    cq = _rms(_dot_nt(hb, win_ref[IN_QD:IN_KV, :]), qg_ref[...]).astype(BF16)
    nope_cols = MLA_HEADS * QK_NOPE_DIM
    for hp in range(MLA_HEADS // 2):
        qn = _dot(cq, wuq_ref[:, hp * 2 * LANES:(hp + 1) * 2 * LANES])
        q_ref[0, 2 * hp, :, 0:LANES] = (qn[:, 0:LANES] * q_scale).astype(BF16)
        q_ref[0, 2 * hp + 1, :, 0:LANES] = (qn[:, LANES:2 * LANES] * q_scale).astype(BF16)
    for hq in range(MLA_HEADS // 4):
        qr = _dot(cq, wuq_ref[:, nope_cols + hq * 2 * LANES:nope_cols + (hq + 1) * 2 * LANES])
        for j in range(2):
            pair = rope(qr[:, j * LANES:(j + 1) * LANES]) * q_scale
            h = 4 * hq + 2 * j
            q_ref[0, h, :, LANES:QK_PAD_DIM] = jnp.where(low_group, pair, 0.0).astype(BF16)
            q_ref[0, h + 1, :, LANES:QK_PAD_DIM] = jnp.where(
                low_group, pltpu.roll(pair, QK_ROPE_DIM, 1), 0.0).astype(BF16)

    ckv = _rms(_dot_nt(hb, win_ref[IN_KV:IN_XQ, :]), kvg_ref[...]).astype(BF16)
    kr = rope(_dot_nt(hb, win_ref[IN_KR:IN_END, :])).astype(BF16)
    for hp in range(MLA_HEADS // 2):
        cols = slice(hp * 2 * LANES, (hp + 1) * 2 * LANES)
        kn = _dot(ckv, wuk_ref[:, cols]).astype(BF16)
        vv = _dot(ckv, wuv_ref[:, cols]).astype(BF16)
        for j in range(2):
            h = 2 * hp + j
            k_ref[0, h, :, 0:LANES] = kn[:, j * LANES:(j + 1) * LANES]
            k_ref[0, h, :, LANES:QK_PAD_DIM] = kr
            v_ref[0, h] = vv[:, j * LANES:(j + 1) * LANES]

    t_seq = lax.broadcasted_iota(jnp.int32, (tm, 1), 0) + si * tm
    for g, w in enumerate(POOL_WINDOWS):
        lo = g * POOL_GROUP_DIM
        hi = lo + POOL_GROUP_DIM
        acc = u[:, lo:hi]
        for j in range(1, w):
            acc = acc + ext_ref[POOL_HALO - j:POOL_HALO - j + tm, lo:hi]
        cnt = jnp.minimum(t_seq + 1, w).astype(F32)
        p = acc / cnt - u[:, lo:hi]
        y = _dot(p.astype(BF16), poolw_ref[g]) * pools_ref[:, lo:hi]
        ypool_ref[:, lo:hi] = y.astype(BF16)
    ext_ref[0:POOL_HALO, :] = ext_ref[tm:tm + POOL_HALO, :]

    xq_ref[...] = _dot_nt(hb, win_ref[IN_XQ:IN_GATE, :]).astype(BF16)


def _mixer_in(x2, pos2, invf, mixg, win_p, gate_b, qg, wuq_p, kvg, wuk, wuv, pool_w, pool_s, *, B, S, tm):
    T = B * S
    tps = S // tm
    q_scale = (QK_NOPE_DIM + QK_ROPE_DIM) ** -0.5 * math.log2(math.e)
    body = functools.partial(_mixer_in_body, tm=tm, tiles_per_seq=tps, q_scale=q_scale)
    row = lambda i: (i, 0)
    head = lambda i: (i // tps, 0, i % tps, 0)
    return pl.pallas_call(
        body,
        grid=(T // tm,),
        in_specs=[
            pl.BlockSpec((tm, D_MODEL), row),
            pl.BlockSpec((tm, 1), row),
            _const_spec((1, LANES)),
            _const_spec((1, D_MODEL)),
            _const_spec((IN_END, D_MODEL)),
            _const_spec((N_BRANCHES, D_MODEL)),
            _const_spec((1, Q_LORA_RANK)),
            _const_spec((Q_LORA_RANK, MLA_HEADS * (QK_NOPE_DIM + QK_ROPE_DIM))),
            _const_spec((1, KV_LORA_RANK)),
            _const_spec((KV_LORA_RANK, MLA_HEADS * QK_NOPE_DIM)),
            _const_spec((KV_LORA_RANK, MLA_HEADS * V_HEAD_DIM)),
            _const_spec((len(POOL_WINDOWS), POOL_GROUP_DIM, POOL_GROUP_DIM)),
            _const_spec((1, POOL_DIM)),
        ],
        out_specs=[
            pl.BlockSpec((tm, POOL_DIM), row),
            pl.BlockSpec((tm, XATTN_DIM), row),
            pl.BlockSpec((tm, N_BRANCHES * D_MODEL), row),
            pl.BlockSpec((1, MLA_HEADS, tm, QK_PAD_DIM), head),
            pl.BlockSpec((1, MLA_HEADS, tm, QK_PAD_DIM), head),
            pl.BlockSpec((1, MLA_HEADS, tm, V_HEAD_DIM), head),
        ],
        out_shape=[
            jax.ShapeDtypeStruct((T, POOL_DIM), BF16),
            jax.ShapeDtypeStruct((T, XATTN_DIM), BF16),
            jax.ShapeDtypeStruct((T, N_BRANCHES * D_MODEL), BF16),
            jax.ShapeDtypeStruct((B, MLA_HEADS, S, QK_PAD_DIM), BF16),
            jax.ShapeDtypeStruct((B, MLA_HEADS, S, QK_PAD_DIM), BF16),
            jax.ShapeDtypeStruct((B, MLA_HEADS, S, V_HEAD_DIM), BF16),
        ],
        scratch_shapes=[pltpu.VMEM((tm + POOL_HALO, POOL_DIM), F32)],
        compiler_params=pltpu.CompilerParams(dimension_semantics=("arbitrary",),
                                             vmem_limit_bytes=VMEM_LIMIT_BYTES),
        name="mixer_in",
    )(x2, pos2, invf, mixg, win_p, gate_b, qg, wuq_p, kvg, wuk, wuv, pool_w, pool_s)


def _mem_kv_body(mem_ref, g_ref, w_ref, k_ref, v_ref):
    mb = _rms(mem_ref[...], g_ref[...]).astype(BF16)
    kv = _dot(mb, w_ref[...])
    k_ref[...] = kv[:, 0:XATTN_DIM].astype(BF16)
    v_ref[...] = kv[:, XATTN_DIM:2 * XATTN_DIM].astype(BF16)


def _mem_kv(mem2, g, w):
    rows = mem2.shape[0]
    tr = min(rows, 512)
    return pl.pallas_call(
        _mem_kv_body,
        grid=(rows // tr,),
        in_specs=[pl.BlockSpec((tr, D_MODEL), lambda i: (i, 0)),
                  _const_spec((1, D_MODEL)),
                  _const_spec((D_MODEL, 2 * XATTN_DIM))],
        out_specs=[pl.BlockSpec((tr, XATTN_DIM), lambda i: (i, 0)),
                   pl.BlockSpec((tr, XATTN_DIM), lambda i: (i, 0))],
        out_shape=[jax.ShapeDtypeStruct((rows, XATTN_DIM), BF16),
                   jax.ShapeDtypeStruct((rows, XATTN_DIM), BF16)],
        compiler_params=pltpu.CompilerParams(dimension_semantics=("arbitrary",)),
        name="mem_kv",
    )(mem2, g, w)


def _attn_unrolled_body(q_ref, k_ref, v_ref, o_ref, s_a, s_b, mc_a, mc_b, m_ref, l_ref, acc_ref, *, nq, tq):
    s_bufs = (s_a, s_b)
    mc_bufs = (mc_a, mc_b)
    mxu_row_split = 2
    blocks = [(qi, kb) for qi in range(nq) for kb in range(qi + 1)]

    def scores(i, slot):
        qi, kb = blocks[i]
        s = _dot_nt(q_ref[0, 0, qi * tq:(qi + 1) * tq, :], k_ref[0, 0, kb * tq:(kb + 1) * tq, :])
        if qi == kb:
            ri = lax.broadcasted_iota(jnp.int32, (tq, tq), 0)
            ci = lax.broadcasted_iota(jnp.int32, (tq, tq), 1)
            s = jnp.where(ci <= ri, s, NEG_INF)
        s_bufs[slot][...] = s
        mc_bufs[slot][...] = jnp.broadcast_to(jnp.max(s, axis=1, keepdims=True), (tq, LANES))

    def accumulate(i, slot):
        qi, kb = blocks[i]
        is_first = kb == 0
        is_last = kb == qi
        if is_first:
            m_new = mc_bufs[slot][...]
        else:
            m_prev = m_ref[...]
            m_new = jnp.maximum(m_prev, mc_bufs[slot][...])
            alpha = jnp.exp2(m_prev - m_new)
        p = jnp.exp2(s_bufs[slot][...] - jnp.concatenate([m_new] * (tq // LANES), axis=1))
        psum = p[:, 0:LANES]
        for c in range(1, tq // LANES):
            psum = psum + p[:, c * LANES:(c + 1) * LANES]
        l_new = psum if is_first else alpha * l_ref[...] + psum
        pb = p.astype(BF16)
        v = v_ref[0, 0, kb * tq:(kb + 1) * tq, :]
        if is_last:
            inv = 1.0 / jnp.sum(l_new, axis=1, keepdims=True)
        else:
            l_ref[...] = l_new
            m_ref[...] = m_new
        h = tq // mxu_row_split
        for r in range(mxu_row_split):
            rows = slice(r * h, (r + 1) * h)
            acc = _dot(pb[rows, :], v)
            if not is_first:
                acc = alpha[rows, :] * acc_ref[rows, :] + acc
            if is_last:
                o_ref[0, qi * tq + r * h:qi * tq + (r + 1) * h, :] = (acc * inv[rows, :]).astype(BF16)
            else:
                acc_ref[rows, :] = acc

    scores(0, 0)
    for i in range(len(blocks)):
        if i + 1 < len(blocks):
            scores(i + 1, (i + 1) % 2)
        accumulate(i, i % 2)


def _mla_attn_unrolled(q, k, v, *, tq):
    B, H, S, _ = q.shape
    per_head = lambda b, h: (b, h, 0, 0)
    return pl.pallas_call(
        functools.partial(_attn_unrolled_body, nq=S // tq, tq=tq),
        grid=(B, H),
        in_specs=[pl.BlockSpec((1, 1, S, QK_PAD_DIM), per_head),
                  pl.BlockSpec((1, 1, S, QK_PAD_DIM), per_head),
                  pl.BlockSpec((1, 1, S, V_HEAD_DIM), per_head)],
        out_specs=pl.BlockSpec((1, S, V_HEAD_DIM), lambda b, h: (b, 0, h)),
        out_shape=jax.ShapeDtypeStruct((B, S, H * V_HEAD_DIM), BF16),
        scratch_shapes=[pltpu.VMEM((tq, tq), F32), pltpu.VMEM((tq, tq), F32),
                        pltpu.VMEM((tq, LANES), F32), pltpu.VMEM((tq, LANES), F32),
                        pltpu.VMEM((tq, LANES), F32), pltpu.VMEM((tq, LANES), F32),
                        pltpu.VMEM((tq, V_HEAD_DIM), F32)],
        compiler_params=pltpu.CompilerParams(dimension_semantics=("arbitrary", "arbitrary"),
                                             vmem_limit_bytes=VMEM_LIMIT_BYTES),
        name="mla_attn",
    )(q, k, v)


def _merge_body(x_ref, ypool_ref, ymla_ref, xq_ref, gates_ref, kmem_ref, vmem_ref,
                wbp_ref, wbm_ref, wbx_ref, wout_ref, ffng_ref, wr_ref, br_ref,
                x1_ref, h2_ref, wts_ref, loc_ref, cnt_tab_ref, carry_tab_ref, counts_ref, carry_ref, *, tm):
    @pl.when(pl.program_id(0) == 0)
    def _():
        carry_ref[...] = jnp.zeros((N_EXPERTS, LANES), F32)

    xq = xq_ref[...]
    parts = []
    for h in range(XATTN_HEADS):
        cols = slice(h * XATTN_HEAD_DIM, (h + 1) * XATTN_HEAD_DIM)
        s = _dot_nt(xq[:, cols], kmem_ref[:, cols]) * (XATTN_HEAD_DIM ** -0.5)
        e = jnp.exp(s - jnp.max(s, axis=1, keepdims=True))
        p = e / jnp.sum(e, axis=1, keepdims=True)
        parts.append(_dot(p.astype(BF16), vmem_ref[:, cols]))
    ymem = jnp.concatenate(parts, axis=1).astype(BF16)

    gates = gates_ref[...].astype(F32)
    merged = (gates[:, 0:D_MODEL] * _dot(ypool_ref[...], wbp_ref[...])
              + gates[:, D_MODEL:2 * D_MODEL] * _dot(ymla_ref[...], wbm_ref[...])
              + gates[:, 2 * D_MODEL:3 * D_MODEL] * _dot(ymem, wbx_ref[...]))
    x1 = x_ref[...] + _dot(merged.astype(BF16), wout_ref[...])
    x1_ref[...] = x1
    h2 = _rms(x1, ffng_ref[...]).astype(BF16)
    h2_ref[...] = h2

    lt = _dot_nt(wr_ref[...], h2) + br_ref[...]
    gl = lt[N_EXPERTS:N_EXPERTS + N_GROUPS, :]
    gmax = jnp.max(gl, axis=0, keepdims=True)
    r4 = lax.broadcasted_iota(jnp.int32, (N_GROUPS, tm), 0).astype(F32)
    gidx = jnp.min(jnp.where(gl == gmax, r4, float(N_GROUPS)), axis=0, keepdims=True)
    pg = 1.0 / jnp.sum(jnp.exp(gl - gmax), axis=0, keepdims=True)
    esel = lt[0:EXPERTS_PER_GROUP, :]
    for g in range(1, N_GROUPS):
        esel = jnp.where(gidx == float(g), lt[g * EXPERTS_PER_GROUP:(g + 1) * EXPERTS_PER_GROUP, :], esel)
    r8 = lax.broadcasted_iota(jnp.int32, (EXPERTS_PER_GROUP, tm), 0).astype(F32)
    m1 = jnp.max(esel, axis=0, keepdims=True)
    i1 = jnp.min(jnp.where(esel == m1, r8, float(EXPERTS_PER_GROUP)), axis=0, keepdims=True)
    rest = jnp.where(r8 == i1, -jnp.inf, esel)
    m2 = jnp.max(rest, axis=0, keepdims=True)
    i2 = jnp.min(jnp.where(rest == m2, r8, float(EXPERTS_PER_GROUP)), axis=0, keepdims=True)
    e2 = jnp.exp(m2 - m1)
    den = 1.0 + e2
    wts_ref[0:1, :] = pg / den
    wts_ref[1:2, :] = pg * e2 / den
    ex1 = gidx * float(EXPERTS_PER_GROUP) + i1
    ex2 = gidx * float(EXPERTS_PER_GROUP) + i2

    r32 = lax.broadcasted_iota(jnp.int32, (N_EXPERTS, tm), 0).astype(F32)
    is1 = r32 == ex1
    is2 = r32 == ex2
    member = jnp.where(is1 | is2, 1.0, 0.0)
    upper = jnp.where(lax.broadcasted_iota(jnp.int32, (tm, tm), 0)
                      <= lax.broadcasted_iota(jnp.int32, (tm, tm), 1), 1.0, 0.0).astype(BF16)
    incl = _dot(member.astype(BF16), upper)
    run = jnp.floor((jnp.sum(member, axis=1, keepdims=True) + (RUN_ALIGN - 1)) / RUN_ALIGN) * RUN_ALIGN
    rcol = lax.broadcasted_iota(jnp.int32, (N_EXPERTS, 1), 0)
    run_start = jnp.zeros((N_EXPERTS, 1), F32)
    for e in range(N_EXPERTS - 1):
        run_start = run_start + jnp.where(rcol > e, run[e:e + 1, :], 0.0)
    pos = incl - 1.0 + run_start
    loc_ref[0:1, :] = jnp.sum(jnp.where(is1, pos, 0.0), axis=0, keepdims=True).astype(jnp.int32)
    loc_ref[1:2, :] = jnp.sum(jnp.where(is2, pos, 0.0), axis=0, keepdims=True).astype(jnp.int32)
    carry = carry_ref[...]
    total = carry + run
    cnt_tab_ref[...] = jnp.broadcast_to(run, (N_EXPERTS, LANES)).astype(jnp.int32)
    carry_tab_ref[...] = carry.astype(jnp.int32)
    carry_ref[...] = total
    counts_ref[...] = total.astype(jnp.int32)


def _merge(x2, ypool, ymla, xq, gates, kmem, vmem, wbp, wbm, wbx, wout, ffng, wr, br, *, B, S, tm, mem_len):
    T = B * S
    tps = S // tm
    row = lambda i: (i, 0)
    lane = lambda i: (0, i)
    memb = lambda i: (i // tps, 0)
    return pl.pallas_call(
        functools.partial(_merge_body, tm=tm),
        grid=(T // tm,),
        in_specs=[
            pl.BlockSpec((tm, D_MODEL), row),
            pl.BlockSpec((tm, POOL_DIM), row),
            pl.BlockSpec((tm, MLA_HEADS * V_HEAD_DIM), row),
            pl.BlockSpec((tm, XATTN_DIM), row),
            pl.BlockSpec((tm, N_BRANCHES * D_MODEL), row),
            pl.BlockSpec((mem_len, XATTN_DIM), memb),
            pl.BlockSpec((mem_len, XATTN_DIM), memb),
            _const_spec((POOL_DIM, D_MODEL)),
            _const_spec((MLA_HEADS * V_HEAD_DIM, D_MODEL)),
            _const_spec((XATTN_DIM, D_MODEL)),
            _const_spec((D_MODEL, D_MODEL)),
            _const_spec((1, D_MODEL)),
            _const_spec((ROUTER_ROWS, D_MODEL)),
            _const_spec((ROUTER_ROWS, 1)),
        ],
        out_specs=[
            pl.BlockSpec((tm, D_MODEL), row),
            pl.BlockSpec((tm, D_MODEL), row),
            pl.BlockSpec((2, tm), lane),
            pl.BlockSpec((2, tm), lane),
            pl.BlockSpec((N_EXPERTS, LANES), row),
            pl.BlockSpec((N_EXPERTS, LANES), row),
            pl.BlockSpec((N_EXPERTS, LANES), lambda i: (0, 0)),
        ],
        out_shape=[
            jax.ShapeDtypeStruct((T, D_MODEL), F32),
            jax.ShapeDtypeStruct((T, D_MODEL), BF16),
            jax.ShapeDtypeStruct((2, T), F32),
            jax.ShapeDtypeStruct((2, T), jnp.int32),
            jax.ShapeDtypeStruct((T // tm * N_EXPERTS, LANES), jnp.int32),
            jax.ShapeDtypeStruct((T // tm * N_EXPERTS, LANES), jnp.int32),
            jax.ShapeDtypeStruct((N_EXPERTS, LANES), jnp.int32),
        ],
        scratch_shapes=[pltpu.VMEM((N_EXPERTS, LANES), F32)],
        compiler_params=pltpu.CompilerParams(dimension_semantics=("arbitrary",),
                                             vmem_limit_bytes=VMEM_LIMIT_BYTES),
        name="merge",
    )(x2, ypool, ymla, xq, gates, kmem, vmem, wbp, wbm, wbx, wout, ffng, wr, br)


def _moe_pos_body(counts_ref, meta_ref):
    shift = int(math.log2(MOE_ROWS))
    cnt = counts_ref[...]
    padded = lax.shift_left(lax.shift_right_logical(cnt + (MOE_ROWS - 1), shift), shift)
    r32 = lax.broadcasted_iota(jnp.int32, (N_EXPERTS, LANES), 0)
    pad_start = jnp.zeros((N_EXPERTS, LANES), jnp.int32)
    for e in range(N_EXPERTS - 1):
        pad_start = pad_start + jnp.where(r32 > e, padded[e:e + 1, :], 0)
    pad_end = pad_start + padded

    lane = lax.broadcasted_iota(jnp.int32, (1, META_LANES), 1)
    block_row = lane * MOE_ROWS
    blk_e = jnp.zeros((1, META_LANES), jnp.int32)
    pe_row = jnp.zeros((1, META_LANES), jnp.int32)
    for e in range(N_EXPERTS):
        pe = pad_end[e:e + 1, 0:1]
        blk_e = blk_e + jnp.where(pe <= block_row, 1, 0)
        pe_row = pe_row + jnp.where(lane == META_PAD_END + e, pe, 0)
    blk_e = jnp.minimum(blk_e, N_EXPERTS - 1)
    nact = lax.shift_right_logical(pad_end[N_EXPERTS - 1:N_EXPERTS, 0:1], shift)
    meta = jnp.where(lane < META_PAD_END, blk_e, pe_row)
    meta_ref[...] = jnp.where(lane == META_NACT, nact, meta)


def _moe_pos(counts):
    full = lambda shape: pl.BlockSpec(shape, lambda i: (0,) * len(shape))
    return pl.pallas_call(
        _moe_pos_body,
        grid=(1,),
        in_specs=[full((N_EXPERTS, LANES))],
        out_specs=full((1, META_LANES)),
        out_shape=jax.ShapeDtypeStruct((1, META_LANES), jnp.int32),
        compiler_params=pltpu.CompilerParams(dimension_semantics=("arbitrary",)),
        name="moe_pos",
    )(counts)


def _pack_bf16_pairs(x):
    lo = pltpu.bitcast(x[:, 0:PACKED_DIM], U32)
    hi = pltpu.bitcast(x[:, PACKED_DIM:D_MODEL], U32)
    return hi | lax.shift_right_logical(lo, jnp.uint32(16))


def _unpack_bf16_pairs(w):
    lo = pltpu.bitcast(lax.shift_left(w, jnp.uint32(16)), F32)
    hi = pltpu.bitcast(w & jnp.uint32(0xFFFF0000), F32)
    return jnp.concatenate([lo, hi], axis=1)


def _loc_rows(tm):
    bf16_rows = 2 * RUN_ALIGN
    return pl.cdiv(2 * tm + N_EXPERTS * (RUN_ALIGN - 1), bf16_rows) * bf16_rows


def _run_copies(tile, cnt_ref, carry_ref, meta_ref, make_copy):
    def per_expert(e, local):
        n = pl.multiple_of(cnt_ref[tile * N_EXPERTS + e], RUN_ALIGN)
        start = jnp.where(e == 0, 0, meta_ref[META_PAD_END + jnp.maximum(e - 1, 0)])
        glob = pl.multiple_of(start + carry_ref[tile * N_EXPERTS + e], RUN_ALIGN)

        @pl.when(n > 0)
        def _():
            make_copy(pl.multiple_of(local, RUN_ALIGN), glob, n).start()

        return local + n

    return pl.multiple_of(lax.fori_loop(0, N_EXPERTS, per_expert, 0), RUN_ALIGN)


def _tile_rows(tile, cnt_ref):
    total = lax.fori_loop(0, N_EXPERTS, lambda e, t: t + cnt_ref[tile * N_EXPERTS + e], 0)
    return pl.multiple_of(total, RUN_ALIGN)


def _dispatch_body(meta_ref, cnt_ref, carry_ref, loc_ref, h2_ref, xs_ref, xloc_ref, zero_ref, sems, zsem,
                   *, tm, n_blocks):
    tile = pl.program_id(0)
    last_tile = pl.num_programs(0) - 1
    slot = lax.rem(tile, 2)
    nact = meta_ref[META_NACT]

    def wait_rows(t, s):
        n = _tile_rows(t, cnt_ref)
        pltpu.make_async_copy(xloc_ref.at[s, pl.ds(0, n)], xs_ref.at[pl.ds(0, n)], sems.at[s]).wait()

    def pad_copy(e):
        end = pl.multiple_of(meta_ref[META_PAD_END + e], MOE_ROWS)
        start = jnp.where(e == 0, 0, meta_ref[META_PAD_END + jnp.maximum(e - 1, 0)])
        used = carry_ref[last_tile * N_EXPERTS + e] + cnt_ref[last_tile * N_EXPERTS + e]
        first = pl.multiple_of(start + used, RUN_ALIGN)
        n = pl.multiple_of(end - first, RUN_ALIGN)
        return n, pltpu.make_async_copy(zero_ref.at[pl.ds(0, n)], xs_ref.at[pl.ds(first, n)], zsem)

    def tail_copy(b):
        return pltpu.make_async_copy(
            zero_ref, xs_ref.at[pl.ds(pl.multiple_of(b * MOE_ROWS, MOE_ROWS), MOE_ROWS)], zsem)

    def fill(op):
        def pad(e, c):
            n, cp = pad_copy(e)

            @pl.when(n > 0)
            def _():
                op(cp)
            return c

        def tail(b, c):
            op(tail_copy(b))
            return c

        lax.fori_loop(0, N_EXPERTS, pad, 0)
        lax.fori_loop(nact, n_blocks, tail, 0)

    @pl.when(tile == 0)
    def _():
        zero_ref[...] = jnp.zeros((MOE_ROWS, PACKED_DIM), U32)
        fill(lambda cp: cp.start())

    @pl.when(tile >= 2)
    def _():
        wait_rows(tile - 2, slot)

    r = lax.broadcasted_iota(jnp.int32, (_loc_rows(tm), tm), 0)
    onehot = jnp.where((r == loc_ref[0:1, :]) | (r == loc_ref[1:2, :]), 1.0, 0.0).astype(BF16)
    xloc_ref[slot] = _pack_bf16_pairs(_dot(onehot, h2_ref[...]))

    def make_copy(local, glob, n):
        return pltpu.make_async_copy(xloc_ref.at[slot, pl.ds(local, n)], xs_ref.at[pl.ds(glob, n)],
                                     sems.at[slot])

    _run_copies(tile, cnt_ref, carry_ref, meta_ref, make_copy)

    @pl.when(tile == last_tile)
    def _():
        @pl.when(tile >= 1)
        def _():
            wait_rows(tile - 1, 1 - slot)

        wait_rows(tile, slot)
        fill(lambda cp: cp.wait())


def _dispatch(meta1, cnt_tab, carry_tab, loc, h2, *, R, tm):
    T = h2.shape[0]
    return pl.pallas_call(
        functools.partial(_dispatch_body, tm=tm, n_blocks=R // MOE_ROWS),
        grid_spec=pltpu.PrefetchScalarGridSpec(
            num_scalar_prefetch=3,
            grid=(T // tm,),
            in_specs=[pl.BlockSpec((2, tm), lambda i, *_: (0, i)),
                      pl.BlockSpec((tm, D_MODEL), lambda i, *_: (i, 0))],
            out_specs=pl.BlockSpec(memory_space=pl.ANY),
            scratch_shapes=[pltpu.VMEM((2, _loc_rows(tm), PACKED_DIM), U32),
                            pltpu.VMEM((MOE_ROWS, PACKED_DIM), U32),
                            pltpu.SemaphoreType.DMA((2,)), pltpu.SemaphoreType.DMA],
        ),
        out_shape=jax.ShapeDtypeStruct((R, PACKED_DIM), U32),
        compiler_params=pltpu.CompilerParams(dimension_semantics=("arbitrary",),
                                             vmem_limit_bytes=VMEM_LIMIT_BYTES),
        name="dispatch",
    )(meta1, cnt_tab, carry_tab, loc, h2)


def _moe_ffn_body(meta_ref, xs_ref, wg_hbm, wu_hbm, wd_hbm, ys_ref,
                  xbuf, ybuf, wg_stage, wu_stage, wd_stage, wg_b, wu_b, wd_b, zero_ref,
                  xsem, ysem, wsem, zsem, *, n_blocks):
    nact = meta_ref[META_NACT]
    shift = int(math.log2(MOE_ROWS))

    def rows_of(b):
        return pl.ds(pl.multiple_of(b * MOE_ROWS, MOE_ROWS), MOE_ROWS)

    def x_copy(b, s):
        return pltpu.make_async_copy(xs_ref.at[rows_of(b)], xbuf.at[s], xsem.at[s])

    def y_copy(b, s):
        return pltpu.make_async_copy(ybuf.at[s], ys_ref.at[rows_of(b)], ysem.at[s])

    def w_copies(e, s):
        return (pltpu.make_async_copy(wg_hbm.at[e], wg_stage.at[s], wsem.at[s]),
                pltpu.make_async_copy(wu_hbm.at[e], wu_stage.at[s], wsem.at[s]),
                pltpu.make_async_copy(wd_hbm.at[e], wd_stage.at[s], wsem.at[s]))

    def tail_copy(b):
        return pltpu.make_async_copy(zero_ref, ys_ref.at[rows_of(b)], zsem)

    zero_ref[...] = jnp.zeros((MOE_ROWS, PACKED_DIM), U32)
    lax.fori_loop(nact, n_blocks, lambda b, c: (tail_copy(b).start(), c)[1], 0)

    def next_expert_block(e):
        return lax.shift_right_logical(meta_ref[META_PAD_END + e], shift)

    def start_weights(b, s):
        @pl.when(b < nact)
        def _():
            for cp in w_copies(meta_ref[jnp.minimum(b, n_blocks - 1)], s):
                cp.start()

    x_copy(0, 0).start()

    @pl.when(nact > 1)
    def _():
        x_copy(1, 1).start()

    e_first = meta_ref[0]
    start_weights(0, 0)
    start_weights(next_expert_block(e_first), 1)

    def block(b, xs, ys, k_prev):
        valid = b < nact
        e = meta_ref[jnp.minimum(b, nact - 1)]
        changed = jnp.logical_and(valid, jnp.logical_or(b == 0, e != meta_ref[jnp.maximum(b - 1, 0)]))
        k = jnp.where(changed, k_prev + 1, k_prev)

        @pl.when(changed)
        def _():
            ws = lax.rem(k, W_STAGES)
            for cp in w_copies(e, ws):
                cp.wait()
            wg_b[...] = wg_stage[ws].astype(BF16)
            wu_b[...] = wu_stage[ws].astype(BF16)
            wd_b[...] = wd_stage[ws].astype(BF16)
            n1 = next_expert_block(e)
            e1 = meta_ref[jnp.minimum(n1, n_blocks - 1)]
            n2 = jnp.where(n1 < nact, next_expert_block(e1), n_blocks)
            start_weights(n2, lax.rem(k + 2, W_STAGES))

        @pl.when(valid)
        def _():
            x_copy(b, xs).wait()

            @pl.when(b + 2 < nact)
            def _():
                x_copy(b + 2, (xs + 2) % X_SLOTS).start()

            @pl.when(b >= 2)
            def _():
                y_copy(b - 2, ys).wait()

            for c in range(MOE_ROWS // MOE_CHUNK):
                rows = slice(c * MOE_CHUNK, (c + 1) * MOE_CHUNK)
                x = _unpack_bf16_pairs(xbuf[xs, rows, :]).astype(BF16)
                g = _dot(x, wg_b[...])
                a = (g * jax.nn.sigmoid(g)) * _dot(x, wu_b[...])
                y = _dot(a.astype(BF16), wd_b[...])
                ybuf[ys, rows, :] = _pack_bf16_pairs(y.astype(BF16).astype(F32))
            y_copy(b, ys).start()

        return k

    def quad(i, k):
        for j in range(X_SLOTS):
            k = block(X_SLOTS * i + j, j, j % 2, k)
        return k

    lax.fori_loop(0, lax.div(nact + (X_SLOTS - 1), X_SLOTS), quad, -1)

    @pl.when(nact >= 2)
    def _():
        y_copy(nact - 2, lax.rem(nact, 2)).wait()

    y_copy(nact - 1, lax.rem(nact - 1, 2)).wait()
    lax.fori_loop(nact, n_blocks, lambda b, c: (tail_copy(b).wait(), c)[1], 0)


def _moe_ffn(meta1, xs, wg, wu, wd):
    R = xs.shape[0]
    hbm = pl.BlockSpec(memory_space=pl.ANY)
    return pl.pallas_call(
        functools.partial(_moe_ffn_body, n_blocks=R // MOE_ROWS),
        grid_spec=pltpu.PrefetchScalarGridSpec(
            num_scalar_prefetch=1,
            grid=(1,),
            in_specs=[hbm, hbm, hbm, hbm],
            out_specs=hbm,
            scratch_shapes=[pltpu.VMEM((X_SLOTS, MOE_ROWS, PACKED_DIM), U32),
                            pltpu.VMEM((2, MOE_ROWS, PACKED_DIM), U32),
                            pltpu.VMEM((W_STAGES, D_MODEL, D_EXPERT), F32),
                            pltpu.VMEM((W_STAGES, D_MODEL, D_EXPERT), F32),
                            pltpu.VMEM((W_STAGES, D_EXPERT, D_MODEL), F32),
                            pltpu.VMEM((D_MODEL, D_EXPERT), BF16),
                            pltpu.VMEM((D_MODEL, D_EXPERT), BF16),
                            pltpu.VMEM((D_EXPERT, D_MODEL), BF16),
                            pltpu.VMEM((MOE_ROWS, PACKED_DIM), U32),
                            pltpu.SemaphoreType.DMA((X_SLOTS,)), pltpu.SemaphoreType.DMA((2,)),
                            pltpu.SemaphoreType.DMA((W_STAGES,)), pltpu.SemaphoreType.DMA],
        ),
        out_shape=jax.ShapeDtypeStruct((R, PACKED_DIM), U32),
        compiler_params=pltpu.CompilerParams(dimension_semantics=("arbitrary",),
                                             vmem_limit_bytes=VMEM_LIMIT_BYTES),
        name="moe_ffn",
    )(meta1, xs, wg, wu, wd)


def _combine_body(meta_ref, cnt_ref, carry_ref, x1_ref, loc_ref, wts_ref, fg_ref, ys_ref, out_ref,
                  yloc_ref, sems, *, tm):
    tile = pl.program_id(0)
    slot = lax.rem(tile, 2)

    def fetch(t, s):
        def make_copy(local, glob, n):
            return pltpu.make_async_copy(ys_ref.at[pl.ds(glob, n)], yloc_ref.at[s, pl.ds(local, n)],
                                         sems.at[s])
        _run_copies(t, cnt_ref, carry_ref, meta_ref, make_copy)

    @pl.when(tile == 0)
    def _():
        yloc_ref[...] = jnp.zeros(yloc_ref.shape, U32)
        fetch(tile, slot)

    @pl.when(tile + 1 < pl.num_programs(0))
    def _():
        fetch(tile + 1, 1 - slot)

    n = _tile_rows(tile, cnt_ref)
    pltpu.make_async_copy(ys_ref.at[pl.ds(0, n)], yloc_ref.at[slot, pl.ds(0, n)], sems.at[slot]).wait()
    r = lax.broadcasted_iota(jnp.int32, (_loc_rows(tm), tm), 0)
    is0 = r == loc_ref[0:1, :]
    is1 = r == loc_ref[1:2, :]
    row_w = jnp.sum(jnp.where(is0, wts_ref[0:1, :], 0.0) + jnp.where(is1, wts_ref[1:2, :], 0.0),
                    axis=1, keepdims=True)
    yw = (row_w * _unpack_bf16_pairs(yloc_ref[slot])).astype(BF16)
    twohot = jnp.where(is0 | is1, 1.0, 0.0).astype(BF16)
    moe = lax.dot_general(twohot, yw, (((0,), (0,)), ((), ())), preferred_element_type=F32)
    out_ref[...] = _rms(x1_ref[...] + moe, fg_ref[...])


def _combine(meta1, cnt_tab, carry_tab, x1, loc, wts, fg, ys, *, tm):
    T = x1.shape[0]
    row = lambda i, *_: (i, 0)
    lane = lambda i, *_: (0, i)
    return pl.pallas_call(
        functools.partial(_combine_body, tm=tm),
        grid_spec=pltpu.PrefetchScalarGridSpec(
            num_scalar_prefetch=3,
            grid=(T // tm,),
            in_specs=[pl.BlockSpec((tm, D_MODEL), row),
                      pl.BlockSpec((2, tm), lane),
                      pl.BlockSpec((2, tm), lane),
                      pl.BlockSpec((1, D_MODEL), lambda i, *_: (0, 0)),
                      pl.BlockSpec(memory_space=pl.ANY)],
            out_specs=pl.BlockSpec((tm, D_MODEL), row),
            scratch_shapes=[pltpu.VMEM((2, _loc_rows(tm), PACKED_DIM), U32),
                            pltpu.SemaphoreType.DMA((2,))],
        ),
        out_shape=jax.ShapeDtypeStruct((T, D_MODEL), F32),
        compiler_params=pltpu.CompilerParams(dimension_semantics=("arbitrary",),
                                             vmem_limit_bytes=VMEM_LIMIT_BYTES),
        name="combine",
    )(meta1, cnt_tab, carry_tab, x1, loc, wts, fg, ys)


def _tile(n, t):
    t = min(n, t)
    assert n % t == 0, (n, t)
    return t


def kernel(x, mem, positions, mix_norm_g, w_in, gate_b, q_norm_g, w_uq, kv_norm_g, w_uk, w_uv, pool_w, pool_scale, mem_norm_g, w_mem_kv, w_br_pool, w_br_mla, w_br_mem, w_out, ffn_norm_g, w_router_group, b_router_group, w_router_expert, b_router_expert, w_gate_e, w_up_e, w_down_e, final_norm_g):
    B, S, D = x.shape
    assert D == D_MODEL and mix_norm_g.shape[0] == 1
    T = B * S
    mem_len = mem.shape[1]
    tm = _tile(S, 512)
    l = 0

    win_p = _pack_w_in(jnp.swapaxes(w_in, 1, 2).reshape(W_IN_END, D_MODEL))
    wuq_h = w_uq[l].reshape(Q_LORA_RANK, MLA_HEADS, QK_NOPE_DIM + QK_ROPE_DIM)
    wuq_p = jnp.concatenate([wuq_h[:, :, :QK_NOPE_DIM].reshape(Q_LORA_RANK, MLA_HEADS * QK_NOPE_DIM),
                             wuq_h[:, :, QK_NOPE_DIM:].reshape(Q_LORA_RANK, MLA_HEADS * QK_ROPE_DIM)],
                            axis=1).astype(BF16)
    inv_freq = 1.0 / (ROPE_THETA ** (jnp.arange(0, QK_ROPE_DIM, 2, dtype=F32) / QK_ROPE_DIM))
    invf = jnp.tile(inv_freq, LANES // (QK_ROPE_DIM // 2))[None, :]
    wr = jnp.concatenate([w_router_expert[l], w_router_group[l],
                          jnp.zeros((D_MODEL, ROUTER_ROWS - N_EXPERTS - N_GROUPS), F32)], axis=1).T.astype(BF16)
    br = jnp.concatenate([b_router_expert[l], b_router_group[l],
                          jnp.zeros((ROUTER_ROWS - N_EXPERTS - N_GROUPS,), F32)])[:, None].astype(F32)
    x2 = x.reshape(T, D_MODEL)
    pos2 = positions.reshape(T, 1)

    ypool, xq, gates, q, k, v = _mixer_in(
        x2, pos2, invf, mix_norm_g[l][None, :], win_p, gate_b[l], q_norm_g[l][None, :], wuq_p,
        kv_norm_g[l][None, :], w_uk[l].astype(BF16), w_uv[l].astype(BF16), pool_w[l].astype(BF16),
        pool_scale[l][None, :], B=B, S=S, tm=tm)
    kmem, vmem = _mem_kv(mem.reshape(B * mem_len, D_MODEL), mem_norm_g[l][None, :], w_mem_kv[l].astype(BF16))
    ymla = _mla_attn_unrolled(q, k, v, tq=tm).reshape(T, MLA_HEADS * V_HEAD_DIM)
    x1, h2, wts, loc, cnt_tab, carry_tab, counts = _merge(
        x2, ypool, ymla, xq, gates, kmem, vmem, w_br_pool[l].astype(BF16), w_br_mla[l].astype(BF16),
        w_br_mem[l].astype(BF16), w_out[l].astype(BF16), ffn_norm_g[l][None, :], wr, br,
        B=B, S=S, tm=tm, mem_len=mem_len)

    R = 2 * T + (T // tm) * N_EXPERTS * RUN_ALIGN + N_EXPERTS * MOE_ROWS
    assert R % MOE_ROWS == 0 and R // MOE_ROWS <= META_PAD_END
    meta1 = _moe_pos(counts).reshape(META_LANES)
    cnt1 = cnt_tab[:, 0]
    carry1 = carry_tab[:, 0]
    xs = _dispatch(meta1, cnt1, carry1, loc, h2, R=R, tm=tm)
    ys = _moe_ffn(meta1, xs, w_gate_e[l], w_up_e[l], w_down_e[l])
    out = _combine(meta1, cnt1, carry1, x1, loc, wts, final_norm_g[None, :], ys, tm=tm)
    return out.reshape(B, S, D_MODEL)
```

```python
import functools
import math

import jax
import jax.numpy as jnp
from jax import lax
from jax.experimental import pallas as pl
from jax.experimental.pallas import tpu as pltpu

D_MODEL = 1024
POOL_WINDOWS = (2, 4, 8, 16)
POOL_GROUP_DIM = 128
POOL_DIM = 512
MLA_HEADS = 8
QK_NOPE_DIM = 128
QK_ROPE_DIM = 64
V_HEAD_DIM = 128
Q_LORA_RANK = 384
KV_LORA_RANK = 256
ROPE_THETA = 10000.0
XATTN_HEADS = 4
XATTN_HEAD_DIM = 128
XATTN_DIM = 512
N_BRANCHES = 3
N_GROUPS = 4
EXPERTS_PER_GROUP = 8
N_EXPERTS = 32
D_EXPERT = 256
RMS_EPS = 1e-6
NEG_INF = -1e30

LANES = 128
QK_PAD_DIM = 2 * LANES
POOL_HALO = 16
MOE_ROWS = 512
MOE_CHUNK = 256
X_SLOTS = 4
W_STAGES = 3
RUN_ALIGN = 8
PACKED_DIM = D_MODEL // 2
ROUTER_ROWS = 40
META_LANES = 256
META_PAD_END = 192
META_NACT = 255
VMEM_LIMIT_BYTES = 56 * 1024 * 1024

IN_POOL, IN_QD, IN_KV, IN_XQ, IN_GATE, IN_KR, IN_END = 0, 512, 896, 1152, 1664, 4736, 4864
W_IN_KR, W_IN_XQ, W_IN_END = 1152, 1216, 4800

F32 = jnp.float32
BF16 = jnp.bfloat16
U32 = jnp.uint32


def _rms(x, g):
    ms = jnp.mean(x * x, axis=-1, keepdims=True)
    return (x * lax.rsqrt(ms + RMS_EPS)) * g


def _dot(a, b):
    return jnp.dot(a, b, preferred_element_type=F32)


def _dot_nt(a, b):
    return lax.dot_general(a, b, (((1,), (1,)), ((), ())), preferred_element_type=F32)


def _const_spec(shape):
    nd = len(shape)
    return pl.BlockSpec(shape, lambda *_: (0,) * nd, pipeline_mode=pl.Buffered(1))


def _pack_w_in_body(wt_ref, o_ref):
    chunk = LANES

    def copy_cols(dst, src, n):
        for r in range(0, n, chunk):
            m = min(chunk, n - r)
            o_ref[:, dst + r:dst + r + m] = wt_ref[src + r:src + r + m, :].T.astype(BF16)

    copy_cols(0, 0, W_IN_KR)
    copy_cols(IN_XQ, W_IN_XQ, W_IN_END - W_IN_XQ)
    copy_cols(IN_KR, W_IN_KR, W_IN_XQ - W_IN_KR)
    o_ref[:, IN_KR + QK_ROPE_DIM:IN_END] = jnp.zeros((D_MODEL, LANES - QK_ROPE_DIM), BF16)


def _pack_w_in(w_t):
    whole = pl.BlockSpec(memory_space=pltpu.VMEM)
    return pl.pallas_call(
        _pack_w_in_body,
        in_specs=[whole],
        out_specs=whole,
        out_shape=jax.ShapeDtypeStruct((D_MODEL, IN_END), BF16),
        compiler_params=pltpu.CompilerParams(vmem_limit_bytes=VMEM_LIMIT_BYTES),
        name="pack_w_in",
    )(w_t)


def _mixer_in_body(x_ref, pos_ref, invf_ref, mixg_ref, win_ref, gateb_ref, qg_ref, wuq_ref,
                   kvg_ref, wuk_ref, wuv_ref, poolw_ref, pools_ref,
                   ypool_ref, xq_ref, gates_ref, q_ref, k_ref, v_ref, ext_ref,
                   *, tm, tiles_per_seq, q_scale):
    si = lax.rem(pl.program_id(0), tiles_per_seq)

    @pl.when(pl.program_id(0) == 0)
    def _():
        ext_ref[0:POOL_HALO, :] = jnp.zeros((POOL_HALO, POOL_DIM), F32)

    hb = _rms(x_ref[...], mixg_ref[...]).astype(BF16)

    u = _dot(hb, win_ref[:, IN_POOL:IN_QD])
    ext_ref[0:POOL_HALO, :] = jnp.where(si == 0, 0.0, ext_ref[0:POOL_HALO, :])
    ext_ref[POOL_HALO:POOL_HALO + tm, :] = u

    for c in range(N_BRANCHES):
        gl = _dot(hb, win_ref[:, IN_GATE + c * D_MODEL:IN_GATE + (c + 1) * D_MODEL])
        gates_ref[:, c * D_MODEL:(c + 1) * D_MODEL] = jax.nn.sigmoid(gl + gateb_ref[c:c + 1, :]).astype(BF16)

    ang = pos_ref[...].astype(F32) * invf_ref[...]
    cos = jnp.cos(ang)
    sin = jnp.sin(ang)
    first_half = lax.broadcasted_iota(jnp.int32, (tm, LANES), 1) < (QK_ROPE_DIM // 2)
    sin_signed = jnp.where(first_half, -sin, sin)

    def rope(r):
        swapped = jnp.where(first_half, pltpu.roll(r, LANES - QK_ROPE_DIM // 2, 1),
                            pltpu.roll(r, QK_ROPE_DIM // 2, 1))
        return r * cos + swapped * sin_signed

    cq = _rms(_dot(hb, win_ref[:, IN_QD:IN_KV]), qg_ref[...]).astype(BF16)
    for h in range(MLA_HEADS):
        qh = _dot(cq, wuq_ref[:, h * QK_PAD_DIM:(h + 1) * QK_PAD_DIM])
        q_ref[0, h, :, 0:LANES] = (qh[:, 0:LANES] * q_scale).astype(BF16)
        q_ref[0, h, :, LANES:QK_PAD_DIM] = (rope(qh[:, LANES:QK_PAD_DIM]) * q_scale).astype(BF16)

    ckv = _rms(_dot(hb, win_ref[:, IN_KV:IN_XQ]), kvg_ref[...]).astype(BF16)
    kr = rope(_dot(hb, win_ref[:, IN_KR:IN_END])).astype(BF16)
    for hp in range(MLA_HEADS // 2):
        cols = slice(hp * 2 * LANES, (hp + 1) * 2 * LANES)
        kn = _dot(ckv, wuk_ref[:, cols]).astype(BF16)
        vv = _dot(ckv, wuv_ref[:, cols]).astype(BF16)
        for j in range(2):
            h = 2 * hp + j
            k_ref[0, h, :, 0:LANES] = kn[:, j * LANES:(j + 1) * LANES]
            k_ref[0, h, :, LANES:QK_PAD_DIM] = kr
            v_ref[0, h] = vv[:, j * LANES:(j + 1) * LANES]

    t_seq = lax.broadcasted_iota(jnp.int32, (tm, 1), 0) + si * tm
    for g, w in enumerate(POOL_WINDOWS):
        lo = g * POOL_GROUP_DIM
        hi = lo + POOL_GROUP_DIM
        acc = u[:, lo:hi]
        for j in range(1, w):
            acc = acc + ext_ref[POOL_HALO - j:POOL_HALO - j + tm, lo:hi]
        cnt = jnp.minimum(t_seq + 1, w).astype(F32)
        p = acc / cnt - u[:, lo:hi]
        y = _dot(p.astype(BF16), poolw_ref[g]) * pools_ref[:, lo:hi]
        ypool_ref[:, lo:hi] = y.astype(BF16)
    ext_ref[0:POOL_HALO, :] = ext_ref[tm:tm + POOL_HALO, :]

    xq_ref[...] = _dot(hb, win_ref[:, IN_XQ:IN_GATE]).astype(BF16)


def _mixer_in(x2, pos2, invf, mixg, win_p, gate_b, qg, wuq_p, kvg, wuk, wuv, pool_w, pool_s, *, B, S, tm):
    T = B * S
    tps = S // tm
    q_scale = (QK_NOPE_DIM + QK_ROPE_DIM) ** -0.5 * math.log2(math.e)
    body = functools.partial(_mixer_in_body, tm=tm, tiles_per_seq=tps, q_scale=q_scale)
    row = lambda i: (i, 0)
    head = lambda i: (i // tps, 0, i % tps, 0)
    return pl.pallas_call(
        body,
        grid=(T // tm,),
        in_specs=[
            pl.BlockSpec((tm, D_MODEL), row),
            pl.BlockSpec((tm, 1), row),
            _const_spec((1, LANES)),
            _const_spec((1, D_MODEL)),
            _const_spec((D_MODEL, IN_END)),
            _const_spec((N_BRANCHES, D_MODEL)),
            _const_spec((1, Q_LORA_RANK)),
            _const_spec((Q_LORA_RANK, MLA_HEADS * QK_PAD_DIM)),
            _const_spec((1, KV_LORA_RANK)),
            _const_spec((KV_LORA_RANK, MLA_HEADS * QK_NOPE_DIM)),
            _const_spec((KV_LORA_RANK, MLA_HEADS * V_HEAD_DIM)),
            _const_spec((len(POOL_WINDOWS), POOL_GROUP_DIM, POOL_GROUP_DIM)),
            _const_spec((1, POOL_DIM)),
        ],
        out_specs=[
            pl.BlockSpec((tm, POOL_DIM), row),
            pl.BlockSpec((tm, XATTN_DIM), row),
            pl.BlockSpec((tm, N_BRANCHES * D_MODEL), row),
            pl.BlockSpec((1, MLA_HEADS, tm, QK_PAD_DIM), head),
            pl.BlockSpec((1, MLA_HEADS, tm, QK_PAD_DIM), head),
            pl.BlockSpec((1, MLA_HEADS, tm, V_HEAD_DIM), head),
        ],
        out_shape=[
            jax.ShapeDtypeStruct((T, POOL_DIM), BF16),
            jax.ShapeDtypeStruct((T, XATTN_DIM), BF16),
            jax.ShapeDtypeStruct((T, N_BRANCHES * D_MODEL), BF16),
            jax.ShapeDtypeStruct((B, MLA_HEADS, S, QK_PAD_DIM), BF16),
            jax.ShapeDtypeStruct((B, MLA_HEADS, S, QK_PAD_DIM), BF16),
            jax.ShapeDtypeStruct((B, MLA_HEADS, S, V_HEAD_DIM), BF16),
        ],
        scratch_shapes=[pltpu.VMEM((tm + POOL_HALO, POOL_DIM), F32)],
        compiler_params=pltpu.CompilerParams(dimension_semantics=("arbitrary",),
                                             vmem_limit_bytes=VMEM_LIMIT_BYTES),
        name="mixer_in",
    )(x2, pos2, invf, mixg, win_p, gate_b, qg, wuq_p, kvg, wuk, wuv, pool_w, pool_s)


def _mem_kv_body(mem_ref, g_ref, w_ref, k_ref, v_ref):
    mb = _rms(mem_ref[...], g_ref[...]).astype(BF16)
    kv = _dot(mb, w_ref[...])
    k_ref[...] = kv[:, 0:XATTN_DIM].astype(BF16)
    v_ref[...] = kv[:, XATTN_DIM:2 * XATTN_DIM].astype(BF16)


def _mem_kv(mem2, g, w):
    rows = mem2.shape[0]
    tr = min(rows, 512)
    return pl.pallas_call(
        _mem_kv_body,
        grid=(rows // tr,),
        in_specs=[pl.BlockSpec((tr, D_MODEL), lambda i: (i, 0)),
                  _const_spec((1, D_MODEL)),
                  _const_spec((D_MODEL, 2 * XATTN_DIM))],
        out_specs=[pl.BlockSpec((tr, XATTN_DIM), lambda i: (i, 0)),
                   pl.BlockSpec((tr, XATTN_DIM), lambda i: (i, 0))],
        out_shape=[jax.ShapeDtypeStruct((rows, XATTN_DIM), BF16),
                   jax.ShapeDtypeStruct((rows, XATTN_DIM), BF16)],
        compiler_params=pltpu.CompilerParams(dimension_semantics=("arbitrary",)),
        name="mem_kv",
    )(mem2, g, w)


def _attn_unrolled_body(q_ref, k_ref, v_ref, o_ref, s_a, s_b, mc_a, mc_b, m_ref, l_ref, acc_ref, *, nq, tq):
    s_bufs = (s_a, s_b)
    mc_bufs = (mc_a, mc_b)
    mxu_row_split = 2
    blocks = [(qi, kb) for qi in range(nq) for kb in range(qi + 1)]

    def scores(i, slot):
        qi, kb = blocks[i]
        s = _dot_nt(q_ref[0, 0, qi * tq:(qi + 1) * tq, :], k_ref[0, 0, kb * tq:(kb + 1) * tq, :])
        if qi == kb:
            ri = lax.broadcasted_iota(jnp.int32, (tq, tq), 0)
            ci = lax.broadcasted_iota(jnp.int32, (tq, tq), 1)
            s = jnp.where(ci <= ri, s, NEG_INF)
        s_bufs[slot][...] = s
        mc_bufs[slot][...] = jnp.broadcast_to(jnp.max(s, axis=1, keepdims=True), (tq, LANES))

    def accumulate(i, slot):
        qi, kb = blocks[i]
        is_first = kb == 0
        is_last = kb == qi
        if is_first:
            m_new = mc_bufs[slot][...]
        else:
            m_prev = m_ref[...]
            m_new = jnp.maximum(m_prev, mc_bufs[slot][...])
            alpha = jnp.exp2(m_prev - m_new)
        p = jnp.exp2(s_bufs[slot][...] - jnp.concatenate([m_new] * (tq // LANES), axis=1))
        psum = p[:, 0:LANES]
        for c in range(1, tq // LANES):
            psum = psum + p[:, c * LANES:(c + 1) * LANES]
        l_new = psum if is_first else alpha * l_ref[...] + psum
        pb = p.astype(BF16)
        v = v_ref[0, 0, kb * tq:(kb + 1) * tq, :]
        if is_last:
            inv = 1.0 / jnp.sum(l_new, axis=1, keepdims=True)
        else:
            l_ref[...] = l_new
            m_ref[...] = m_new
        h = tq // mxu_row_split
        for r in range(mxu_row_split):
            rows = slice(r * h, (r + 1) * h)
            acc = _dot(pb[rows, :], v)
            if not is_first:
                acc = alpha[rows, :] * acc_ref[rows, :] + acc
            if is_last:
                o_ref[0, qi * tq + r * h:qi * tq + (r + 1) * h, :] = (acc * inv[rows, :]).astype(BF16)
            else:
                acc_ref[rows, :] = acc

    scores(0, 0)
    for i in range(len(blocks)):
        if i + 1 < len(blocks):
            scores(i + 1, (i + 1) % 2)
        accumulate(i, i % 2)


def _mla_attn_unrolled(q, k, v, *, tq):
    B, H, S, _ = q.shape
    per_head = lambda b, h: (b, h, 0, 0)
    return pl.pallas_call(
        functools.partial(_attn_unrolled_body, nq=S // tq, tq=tq),
        grid=(B, H),
        in_specs=[pl.BlockSpec((1, 1, S, QK_PAD_DIM), per_head),
                  pl.BlockSpec((1, 1, S, QK_PAD_DIM), per_head),
                  pl.BlockSpec((1, 1, S, V_HEAD_DIM), per_head)],
        out_specs=pl.BlockSpec((1, S, V_HEAD_DIM), lambda b, h: (b, 0, h)),
        out_shape=jax.ShapeDtypeStruct((B, S, H * V_HEAD_DIM), BF16),
        scratch_shapes=[pltpu.VMEM((tq, tq), F32), pltpu.VMEM((tq, tq), F32),
                        pltpu.VMEM((tq, LANES), F32), pltpu.VMEM((tq, LANES), F32),
                        pltpu.VMEM((tq, LANES), F32), pltpu.VMEM((tq, LANES), F32),
                        pltpu.VMEM((tq, V_HEAD_DIM), F32)],
        compiler_params=pltpu.CompilerParams(dimension_semantics=("arbitrary", "arbitrary"),
                                             vmem_limit_bytes=VMEM_LIMIT_BYTES),
        name="mla_attn",
    )(q, k, v)


def _merge_body(x_ref, ypool_ref, ymla_ref, xq_ref, gates_ref, kmem_ref, vmem_ref,
                wbp_ref, wbm_ref, wbx_ref, wout_ref, ffng_ref, wr_ref, br_ref,
                x1_ref, h2_ref, wts_ref, loc_ref, cnt_tab_ref, carry_tab_ref, counts_ref, carry_ref, *, tm):
    @pl.when(pl.program_id(0) == 0)
    def _():
        carry_ref[...] = jnp.zeros((N_EXPERTS, LANES), F32)

    xq = xq_ref[...]
    parts = []
    for h in range(XATTN_HEADS):
        cols = slice(h * XATTN_HEAD_DIM, (h + 1) * XATTN_HEAD_DIM)
        s = _dot_nt(xq[:, cols], kmem_ref[:, cols]) * (XATTN_HEAD_DIM ** -0.5)
        e = jnp.exp(s - jnp.max(s, axis=1, keepdims=True))
        p = e / jnp.sum(e, axis=1, keepdims=True)
        parts.append(_dot(p.astype(BF16), vmem_ref[:, cols]))
    ymem = jnp.concatenate(parts, axis=1).astype(BF16)

    gates = gates_ref[...].astype(F32)
    merged = (gates[:, 0:D_MODEL] * _dot(ypool_ref[...], wbp_ref[...])
              + gates[:, D_MODEL:2 * D_MODEL] * _dot(ymla_ref[...], wbm_ref[...])
              + gates[:, 2 * D_MODEL:3 * D_MODEL] * _dot(ymem, wbx_ref[...]))
    x1 = x_ref[...] + _dot(merged.astype(BF16), wout_ref[...])
    x1_ref[...] = x1
    h2 = _rms(x1, ffng_ref[...]).astype(BF16)
    h2_ref[...] = h2

    lt = _dot_nt(wr_ref[...], h2) + br_ref[...]
    gl = lt[N_EXPERTS:N_EXPERTS + N_GROUPS, :]
    gmax = jnp.max(gl, axis=0, keepdims=True)
    r4 = lax.broadcasted_iota(jnp.int32, (N_GROUPS, tm), 0).astype(F32)
    gidx = jnp.min(jnp.where(gl == gmax, r4, float(N_GROUPS)), axis=0, keepdims=True)
    pg = 1.0 / jnp.sum(jnp.exp(gl - gmax), axis=0, keepdims=True)
    esel = lt[0:EXPERTS_PER_GROUP, :]
    for g in range(1, N_GROUPS):
        esel = jnp.where(gidx == float(g), lt[g * EXPERTS_PER_GROUP:(g + 1) * EXPERTS_PER_GROUP, :], esel)
    r8 = lax.broadcasted_iota(jnp.int32, (EXPERTS_PER_GROUP, tm), 0).astype(F32)
    m1 = jnp.max(esel, axis=0, keepdims=True)
    i1 = jnp.min(jnp.where(esel == m1, r8, float(EXPERTS_PER_GROUP)), axis=0, keepdims=True)
    rest = jnp.where(r8 == i1, -jnp.inf, esel)
    m2 = jnp.max(rest, axis=0, keepdims=True)
    i2 = jnp.min(jnp.where(rest == m2, r8, float(EXPERTS_PER_GROUP)), axis=0, keepdims=True)
    e2 = jnp.exp(m2 - m1)
    den = 1.0 + e2
    wts_ref[0:1, :] = pg / den
    wts_ref[1:2, :] = pg * e2 / den
    ex1 = gidx * float(EXPERTS_PER_GROUP) + i1
    ex2 = gidx * float(EXPERTS_PER_GROUP) + i2

    r32 = lax.broadcasted_iota(jnp.int32, (N_EXPERTS, tm), 0).astype(F32)
    is1 = r32 == ex1
    is2 = r32 == ex2
    member = jnp.where(is1 | is2, 1.0, 0.0)
    upper = jnp.where(lax.broadcasted_iota(jnp.int32, (tm, tm), 0)
                      <= lax.broadcasted_iota(jnp.int32, (tm, tm), 1), 1.0, 0.0).astype(BF16)
    incl = _dot(member.astype(BF16), upper)
    run = jnp.floor((jnp.sum(member, axis=1, keepdims=True) + (RUN_ALIGN - 1)) / RUN_ALIGN) * RUN_ALIGN
    rcol = lax.broadcasted_iota(jnp.int32, (N_EXPERTS, 1), 0)
    run_start = jnp.zeros((N_EXPERTS, 1), F32)
    for e in range(N_EXPERTS - 1):
        run_start = run_start + jnp.where(rcol > e, run[e:e + 1, :], 0.0)
    pos = incl - 1.0 + run_start
    loc_ref[0:1, :] = jnp.sum(jnp.where(is1, pos, 0.0), axis=0, keepdims=True).astype(jnp.int32)
    loc_ref[1:2, :] = jnp.sum(jnp.where(is2, pos, 0.0), axis=0, keepdims=True).astype(jnp.int32)
    carry = carry_ref[...]
    total = carry + run
    cnt_tab_ref[...] = jnp.broadcast_to(run, (N_EXPERTS, LANES)).astype(jnp.int32)
    carry_tab_ref[...] = carry.astype(jnp.int32)
    carry_ref[...] = total
    counts_ref[...] = total.astype(jnp.int32)


def _merge(x2, ypool, ymla, xq, gates, kmem, vmem, wbp, wbm, wbx, wout, ffng, wr, br, *, B, S, tm, mem_len):
    T = B * S
    tps = S // tm
    row = lambda i: (i, 0)
    lane = lambda i: (0, i)
    memb = lambda i: (i // tps, 0)
    return pl.pallas_call(
        functools.partial(_merge_body, tm=tm),
        grid=(T // tm,),
        in_specs=[
            pl.BlockSpec((tm, D_MODEL), row),
            pl.BlockSpec((tm, POOL_DIM), row),
            pl.BlockSpec((tm, MLA_HEADS * V_HEAD_DIM), row),
            pl.BlockSpec((tm, XATTN_DIM), row),
            pl.BlockSpec((tm, N_BRANCHES * D_MODEL), row),
            pl.BlockSpec((mem_len, XATTN_DIM), memb),
            pl.BlockSpec((mem_len, XATTN_DIM), memb),
            _const_spec((POOL_DIM, D_MODEL)),
            _const_spec((MLA_HEADS * V_HEAD_DIM, D_MODEL)),
            _const_spec((XATTN_DIM, D_MODEL)),
            _const_spec((D_MODEL, D_MODEL)),
            _const_spec((1, D_MODEL)),
            _const_spec((ROUTER_ROWS, D_MODEL)),
            _const_spec((ROUTER_ROWS, 1)),
        ],
        out_specs=[
            pl.BlockSpec((tm, D_MODEL), row),
            pl.BlockSpec((tm, D_MODEL), row),
            pl.BlockSpec((2, tm), lane),
            pl.BlockSpec((2, tm), lane),
            pl.BlockSpec((N_EXPERTS, LANES), row),
            pl.BlockSpec((N_EXPERTS, LANES), row),
            pl.BlockSpec((N_EXPERTS, LANES), lambda i: (0, 0)),
        ],
        out_shape=[
            jax.ShapeDtypeStruct((T, D_MODEL), F32),
            jax.ShapeDtypeStruct((T, D_MODEL), BF16),
            jax.ShapeDtypeStruct((2, T), F32),
            jax.ShapeDtypeStruct((2, T), jnp.int32),
            jax.ShapeDtypeStruct((T // tm * N_EXPERTS, LANES), jnp.int32),
            jax.ShapeDtypeStruct((T // tm * N_EXPERTS, LANES), jnp.int32),
            jax.ShapeDtypeStruct((N_EXPERTS, LANES), jnp.int32),
        ],
        scratch_shapes=[pltpu.VMEM((N_EXPERTS, LANES), F32)],
        compiler_params=pltpu.CompilerParams(dimension_semantics=("arbitrary",),
                                             vmem_limit_bytes=VMEM_LIMIT_BYTES),
        name="merge",
    )(x2, ypool, ymla, xq, gates, kmem, vmem, wbp, wbm, wbx, wout, ffng, wr, br)


def _moe_pos_body(counts_ref, meta_ref):
    shift = int(math.log2(MOE_ROWS))
    cnt = counts_ref[...]
    padded = lax.shift_left(lax.shift_right_logical(cnt + (MOE_ROWS - 1), shift), shift)
    r32 = lax.broadcasted_iota(jnp.int32, (N_EXPERTS, LANES), 0)
    pad_start = jnp.zeros((N_EXPERTS, LANES), jnp.int32)
    for e in range(N_EXPERTS - 1):
        pad_start = pad_start + jnp.where(r32 > e, padded[e:e + 1, :], 0)
    pad_end = pad_start + padded

    lane = lax.broadcasted_iota(jnp.int32, (1, META_LANES), 1)
    block_row = lane * MOE_ROWS
    blk_e = jnp.zeros((1, META_LANES), jnp.int32)
    pe_row = jnp.zeros((1, META_LANES), jnp.int32)
    for e in range(N_EXPERTS):
        pe = pad_end[e:e + 1, 0:1]
        blk_e = blk_e + jnp.where(pe <= block_row, 1, 0)
        pe_row = pe_row + jnp.where(lane == META_PAD_END + e, pe, 0)
    blk_e = jnp.minimum(blk_e, N_EXPERTS - 1)
    nact = lax.shift_right_logical(pad_end[N_EXPERTS - 1:N_EXPERTS, 0:1], shift)
    meta = jnp.where(lane < META_PAD_END, blk_e, pe_row)
    meta_ref[...] = jnp.where(lane == META_NACT, nact, meta)


def _moe_pos(counts):
    full = lambda shape: pl.BlockSpec(shape, lambda i: (0,) * len(shape))
    return pl.pallas_call(
        _moe_pos_body,
        grid=(1,),
        in_specs=[full((N_EXPERTS, LANES))],
        out_specs=full((1, META_LANES)),
        out_shape=jax.ShapeDtypeStruct((1, META_LANES), jnp.int32),
        compiler_params=pltpu.CompilerParams(dimension_semantics=("arbitrary",)),
        name="moe_pos",
    )(counts)


def _pack_bf16_pairs(x):
    lo = pltpu.bitcast(x[:, 0:PACKED_DIM], U32)
    hi = pltpu.bitcast(x[:, PACKED_DIM:D_MODEL], U32)
    return hi | lax.shift_right_logical(lo, jnp.uint32(16))


def _unpack_bf16_pairs(w):
    lo = pltpu.bitcast(lax.shift_left(w, jnp.uint32(16)), F32)
    hi = pltpu.bitcast(w & jnp.uint32(0xFFFF0000), F32)
    return jnp.concatenate([lo, hi], axis=1)


def _loc_rows(tm):
    bf16_rows = 2 * RUN_ALIGN
    return pl.cdiv(2 * tm + N_EXPERTS * (RUN_ALIGN - 1), bf16_rows) * bf16_rows


def _run_copies(tile, cnt_ref, carry_ref, meta_ref, make_copy):
    def per_expert(e, local):
        n = pl.multiple_of(cnt_ref[tile * N_EXPERTS + e], RUN_ALIGN)
        start = jnp.where(e == 0, 0, meta_ref[META_PAD_END + jnp.maximum(e - 1, 0)])
        glob = pl.multiple_of(start + carry_ref[tile * N_EXPERTS + e], RUN_ALIGN)

        @pl.when(n > 0)
        def _():
            make_copy(pl.multiple_of(local, RUN_ALIGN), glob, n).start()

        return local + n

    return pl.multiple_of(lax.fori_loop(0, N_EXPERTS, per_expert, 0), RUN_ALIGN)


def _tile_rows(tile, cnt_ref):
    total = lax.fori_loop(0, N_EXPERTS, lambda e, t: t + cnt_ref[tile * N_EXPERTS + e], 0)
    return pl.multiple_of(total, RUN_ALIGN)


def _dispatch_body(meta_ref, cnt_ref, carry_ref, loc_ref, h2_ref, xs_ref, xloc_ref, zero_ref, sems, zsem,
                   *, tm, n_blocks):
    tile = pl.program_id(0)
    last_tile = pl.num_programs(0) - 1
    slot = lax.rem(tile, 2)
    nact = meta_ref[META_NACT]

    def wait_rows(t, s):
        n = _tile_rows(t, cnt_ref)
        pltpu.make_async_copy(xloc_ref.at[s, pl.ds(0, n)], xs_ref.at[pl.ds(0, n)], sems.at[s]).wait()

    def pad_copy(e):
        end = pl.multiple_of(meta_ref[META_PAD_END + e], MOE_ROWS)
        start = jnp.where(e == 0, 0, meta_ref[META_PAD_END + jnp.maximum(e - 1, 0)])
        used = carry_ref[last_tile * N_EXPERTS + e] + cnt_ref[last_tile * N_EXPERTS + e]
        first = pl.multiple_of(start + used, RUN_ALIGN)
        n = pl.multiple_of(end - first, RUN_ALIGN)
        return n, pltpu.make_async_copy(zero_ref.at[pl.ds(0, n)], xs_ref.at[pl.ds(first, n)], zsem)

    def tail_copy(b):
        return pltpu.make_async_copy(
            zero_ref, xs_ref.at[pl.ds(pl.multiple_of(b * MOE_ROWS, MOE_ROWS), MOE_ROWS)], zsem)

    def fill(op):
        def pad(e, c):
            n, cp = pad_copy(e)

            @pl.when(n > 0)
            def _():
                op(cp)
            return c

        def tail(b, c):
            op(tail_copy(b))
            return c

        lax.fori_loop(0, N_EXPERTS, pad, 0)
        lax.fori_loop(nact, n_blocks, tail, 0)

    @pl.when(tile == 0)
    def _():
        zero_ref[...] = jnp.zeros((MOE_ROWS, PACKED_DIM), U32)
        fill(lambda cp: cp.start())

    @pl.when(tile >= 2)
    def _():
        wait_rows(tile - 2, slot)

    r = lax.broadcasted_iota(jnp.int32, (_loc_rows(tm), tm), 0)
    onehot = jnp.where((r == loc_ref[0:1, :]) | (r == loc_ref[1:2, :]), 1.0, 0.0).astype(BF16)
    xloc_ref[slot] = _pack_bf16_pairs(_dot(onehot, h2_ref[...]))

    def make_copy(local, glob, n):
        return pltpu.make_async_copy(xloc_ref.at[slot, pl.ds(local, n)], xs_ref.at[pl.ds(glob, n)],
                                     sems.at[slot])

    _run_copies(tile, cnt_ref, carry_ref, meta_ref, make_copy)

    @pl.when(tile == last_tile)
    def _():
        @pl.when(tile >= 1)
        def _():
            wait_rows(tile - 1, 1 - slot)

        wait_rows(tile, slot)
        fill(lambda cp: cp.wait())


def _dispatch(meta1, cnt_tab, carry_tab, loc, h2, *, R, tm):
    T = h2.shape[0]
    return pl.pallas_call(
        functools.partial(_dispatch_body, tm=tm, n_blocks=R // MOE_ROWS),
        grid_spec=pltpu.PrefetchScalarGridSpec(
            num_scalar_prefetch=3,
            grid=(T // tm,),
            in_specs=[pl.BlockSpec((2, tm), lambda i, *_: (0, i)),
                      pl.BlockSpec((tm, D_MODEL), lambda i, *_: (i, 0))],
            out_specs=pl.BlockSpec(memory_space=pl.ANY),
            scratch_shapes=[pltpu.VMEM((2, _loc_rows(tm), PACKED_DIM), U32),
                            pltpu.VMEM((MOE_ROWS, PACKED_DIM), U32),
                            pltpu.SemaphoreType.DMA((2,)), pltpu.SemaphoreType.DMA],
        ),
        out_shape=jax.ShapeDtypeStruct((R, PACKED_DIM), U32),
        compiler_params=pltpu.CompilerParams(dimension_semantics=("arbitrary",),
                                             vmem_limit_bytes=VMEM_LIMIT_BYTES),
        name="dispatch",
    )(meta1, cnt_tab, carry_tab, loc, h2)


def _moe_ffn_body(meta_ref, xs_ref, wg_hbm, wu_hbm, wd_hbm, ys_ref,
                  xbuf, ybuf, wg_stage, wu_stage, wd_stage, wg_b, wu_b, wd_b, zero_ref,
                  xsem, ysem, wsem, zsem, *, n_blocks):
    nact = meta_ref[META_NACT]
    shift = int(math.log2(MOE_ROWS))

    def rows_of(b):
        return pl.ds(pl.multiple_of(b * MOE_ROWS, MOE_ROWS), MOE_ROWS)

    def x_copy(b, s):
        return pltpu.make_async_copy(xs_ref.at[rows_of(b)], xbuf.at[s], xsem.at[s])

    def y_copy(b, s):
        return pltpu.make_async_copy(ybuf.at[s], ys_ref.at[rows_of(b)], ysem.at[s])

    def w_copies(e, s):
        return (pltpu.make_async_copy(wg_hbm.at[e], wg_stage.at[s], wsem.at[s]),
                pltpu.make_async_copy(wu_hbm.at[e], wu_stage.at[s], wsem.at[s]),
                pltpu.make_async_copy(wd_hbm.at[e], wd_stage.at[s], wsem.at[s]))

    def tail_copy(b):
        return pltpu.make_async_copy(zero_ref, ys_ref.at[rows_of(b)], zsem)

    zero_ref[...] = jnp.zeros((MOE_ROWS, PACKED_DIM), U32)
    lax.fori_loop(nact, n_blocks, lambda b, c: (tail_copy(b).start(), c)[1], 0)

    def next_expert_block(e):
        return lax.shift_right_logical(meta_ref[META_PAD_END + e], shift)

    def start_weights(b, s):
        @pl.when(b < nact)
        def _():
            for cp in w_copies(meta_ref[jnp.minimum(b, n_blocks - 1)], s):
                cp.start()

    x_copy(0, 0).start()

    @pl.when(nact > 1)
    def _():
        x_copy(1, 1).start()

    e_first = meta_ref[0]
    start_weights(0, 0)
    start_weights(next_expert_block(e_first), 1)

    def block(b, xs, ys, k_prev):
        valid = b < nact
        e = meta_ref[jnp.minimum(b, nact - 1)]
        changed = jnp.logical_and(valid, jnp.logical_or(b == 0, e != meta_ref[jnp.maximum(b - 1, 0)]))
        k = jnp.where(changed, k_prev + 1, k_prev)

        @pl.when(changed)
        def _():
            ws = lax.rem(k, W_STAGES)
            for cp in w_copies(e, ws):
                cp.wait()
            wg_b[...] = wg_stage[ws].astype(BF16)
            wu_b[...] = wu_stage[ws].astype(BF16)
            wd_b[...] = wd_stage[ws].astype(BF16)
            n1 = next_expert_block(e)
            e1 = meta_ref[jnp.minimum(n1, n_blocks - 1)]
            n2 = jnp.where(n1 < nact, next_expert_block(e1), n_blocks)
            start_weights(n2, lax.rem(k + 2, W_STAGES))

        @pl.when(valid)
        def _():
            x_copy(b, xs).wait()

            @pl.when(b + 2 < nact)
            def _():
                x_copy(b + 2, (xs + 2) % X_SLOTS).start()

            @pl.when(b >= 2)
            def _():
                y_copy(b - 2, ys).wait()

            for c in range(MOE_ROWS // MOE_CHUNK):
                rows = slice(c * MOE_CHUNK, (c + 1) * MOE_CHUNK)
                x = _unpack_bf16_pairs(xbuf[xs, rows, :]).astype(BF16)
                g = _dot(x, wg_b[...])
                a = (g * jax.nn.sigmoid(g)) * _dot(x, wu_b[...])
                y = _dot(a.astype(BF16), wd_b[...])
                ybuf[ys, rows, :] = _pack_bf16_pairs(y.astype(BF16).astype(F32))
            y_copy(b, ys).start()

        return k

    def quad(i, k):
        for j in range(X_SLOTS):
            k = block(X_SLOTS * i + j, j, j % 2, k)
        return k

    lax.fori_loop(0, lax.div(nact + (X_SLOTS - 1), X_SLOTS), quad, -1)

    @pl.when(nact >= 2)
    def _():
        y_copy(nact - 2, lax.rem(nact, 2)).wait()

    y_copy(nact - 1, lax.rem(nact - 1, 2)).wait()
    lax.fori_loop(nact, n_blocks, lambda b, c: (tail_copy(b).wait(), c)[1], 0)


def _moe_ffn(meta1, xs, wg, wu, wd):
    R = xs.shape[0]
    hbm = pl.BlockSpec(memory_space=pl.ANY)
    return pl.pallas_call(
        functools.partial(_moe_ffn_body, n_blocks=R // MOE_ROWS),
        grid_spec=pltpu.PrefetchScalarGridSpec(
            num_scalar_prefetch=1,
            grid=(1,),
            in_specs=[hbm, hbm, hbm, hbm],
            out_specs=hbm,
            scratch_shapes=[pltpu.VMEM((X_SLOTS, MOE_ROWS, PACKED_DIM), U32),
                            pltpu.VMEM((2, MOE_ROWS, PACKED_DIM), U32),
                            pltpu.VMEM((W_STAGES, D_MODEL, D_EXPERT), F32),
                            pltpu.VMEM((W_STAGES, D_MODEL, D_EXPERT), F32),
                            pltpu.VMEM((W_STAGES, D_EXPERT, D_MODEL), F32),
                            pltpu.VMEM((D_MODEL, D_EXPERT), BF16),
                            pltpu.VMEM((D_MODEL, D_EXPERT), BF16),
                            pltpu.VMEM((D_EXPERT, D_MODEL), BF16),
                            pltpu.VMEM((MOE_ROWS, PACKED_DIM), U32),
                            pltpu.SemaphoreType.DMA((X_SLOTS,)), pltpu.SemaphoreType.DMA((2,)),
                            pltpu.SemaphoreType.DMA((W_STAGES,)), pltpu.SemaphoreType.DMA],
        ),
        out_shape=jax.ShapeDtypeStruct((R, PACKED_DIM), U32),
        compiler_params=pltpu.CompilerParams(dimension_semantics=("arbitrary",),
                                             vmem_limit_bytes=VMEM_LIMIT_BYTES),
        name="moe_ffn",
    )(meta1, xs, wg, wu, wd)


def _combine_body(meta_ref, cnt_ref, carry_ref, x1_ref, loc_ref, wts_ref, fg_ref, ys_ref, out_ref,
                  yloc_ref, sems, *, tm):
    tile = pl.program_id(0)
    slot = lax.rem(tile, 2)

    def fetch(t, s):
        def make_copy(local, glob, n):
            return pltpu.make_async_copy(ys_ref.at[pl.ds(glob, n)], yloc_ref.at[s, pl.ds(local, n)],
                                         sems.at[s])
        _run_copies(t, cnt_ref, carry_ref, meta_ref, make_copy)

    @pl.when(tile == 0)
    def _():
        yloc_ref[...] = jnp.zeros(yloc_ref.shape, U32)
        fetch(tile, slot)

    @pl.when(tile + 1 < pl.num_programs(0))
    def _():
        fetch(tile + 1, 1 - slot)

    n = _tile_rows(tile, cnt_ref)
    pltpu.make_async_copy(ys_ref.at[pl.ds(0, n)], yloc_ref.at[slot, pl.ds(0, n)], sems.at[slot]).wait()
    r = lax.broadcasted_iota(jnp.int32, (_loc_rows(tm), tm), 0)
    is0 = r == loc_ref[0:1, :]
    is1 = r == loc_ref[1:2, :]
    row_w = jnp.sum(jnp.where(is0, wts_ref[0:1, :], 0.0) + jnp.where(is1, wts_ref[1:2, :], 0.0),
                    axis=1, keepdims=True)
    yw = (row_w * _unpack_bf16_pairs(yloc_ref[slot])).astype(BF16)
    twohot = jnp.where(is0 | is1, 1.0, 0.0).astype(BF16)
    moe = lax.dot_general(twohot, yw, (((0,), (0,)), ((), ())), preferred_element_type=F32)
    out_ref[...] = _rms(x1_ref[...] + moe, fg_ref[...])


def _combine(meta1, cnt_tab, carry_tab, x1, loc, wts, fg, ys, *, tm):
    T = x1.shape[0]
    row = lambda i, *_: (i, 0)
    lane = lambda i, *_: (0, i)
    return pl.pallas_call(
        functools.partial(_combine_body, tm=tm),
        grid_spec=pltpu.PrefetchScalarGridSpec(
            num_scalar_prefetch=3,
            grid=(T // tm,),
            in_specs=[pl.BlockSpec((tm, D_MODEL), row),
                      pl.BlockSpec((2, tm), lane),
                      pl.BlockSpec((2, tm), lane),
                      pl.BlockSpec((1, D_MODEL), lambda i, *_: (0, 0)),
                      pl.BlockSpec(memory_space=pl.ANY)],
            out_specs=pl.BlockSpec((tm, D_MODEL), row),
            scratch_shapes=[pltpu.VMEM((2, _loc_rows(tm), PACKED_DIM), U32),
                            pltpu.SemaphoreType.DMA((2,))],
        ),
        out_shape=jax.ShapeDtypeStruct((T, D_MODEL), F32),
        compiler_params=pltpu.CompilerParams(dimension_semantics=("arbitrary",),
                                             vmem_limit_bytes=VMEM_LIMIT_BYTES),
        name="combine",
    )(meta1, cnt_tab, carry_tab, x1, loc, wts, fg, ys)


def _tile(n, t):
    t = min(n, t)
    assert n % t == 0, (n, t)
    return t


def kernel(x, mem, positions, mix_norm_g, w_in, gate_b, q_norm_g, w_uq, kv_norm_g, w_uk, w_uv, pool_w, pool_scale, mem_norm_g, w_mem_kv, w_br_pool, w_br_mla, w_br_mem, w_out, ffn_norm_g, w_router_group, b_router_group, w_router_expert, b_router_expert, w_gate_e, w_up_e, w_down_e, final_norm_g):
    B, S, D = x.shape
    assert D == D_MODEL and mix_norm_g.shape[0] == 1
    T = B * S
    mem_len = mem.shape[1]
    tm = _tile(S, 512)
    l = 0

    win_p = _pack_w_in(jnp.swapaxes(w_in, 1, 2).reshape(W_IN_END, D_MODEL))
    wuq_p = jnp.pad(w_uq[l].reshape(Q_LORA_RANK, MLA_HEADS, QK_NOPE_DIM + QK_ROPE_DIM),
                    ((0, 0), (0, 0), (0, QK_PAD_DIM - QK_NOPE_DIM - QK_ROPE_DIM))
                    ).reshape(Q_LORA_RANK, MLA_HEADS * QK_PAD_DIM).astype(BF16)
    inv_freq = 1.0 / (ROPE_THETA ** (jnp.arange(0, QK_ROPE_DIM, 2, dtype=F32) / QK_ROPE_DIM))
    invf = jnp.concatenate([inv_freq, inv_freq, jnp.zeros((LANES - QK_ROPE_DIM,), F32)])[None, :]
    wr = jnp.concatenate([w_router_expert[l], w_router_group[l],
                          jnp.zeros((D_MODEL, ROUTER_ROWS - N_EXPERTS - N_GROUPS), F32)], axis=1).T.astype(BF16)
    br = jnp.concatenate([b_router_expert[l], b_router_group[l],
                          jnp.zeros((ROUTER_ROWS - N_EXPERTS - N_GROUPS,), F32)])[:, None].astype(F32)
    x2 = x.reshape(T, D_MODEL)
    pos2 = positions.reshape(T, 1)

    ypool, xq, gates, q, k, v = _mixer_in(
        x2, pos2, invf, mix_norm_g[l][None, :], win_p, gate_b[l], q_norm_g[l][None, :], wuq_p,
        kv_norm_g[l][None, :], w_uk[l].astype(BF16), w_uv[l].astype(BF16), pool_w[l].astype(BF16),
        pool_scale[l][None, :], B=B, S=S, tm=tm)
    kmem, vmem = _mem_kv(mem.reshape(B * mem_len, D_MODEL), mem_norm_g[l][None, :], w_mem_kv[l].astype(BF16))
    ymla = _mla_attn_unrolled(q, k, v, tq=tm).reshape(T, MLA_HEADS * V_HEAD_DIM)
    x1, h2, wts, loc, cnt_tab, carry_tab, counts = _merge(
        x2, ypool, ymla, xq, gates, kmem, vmem, w_br_pool[l].astype(BF16), w_br_mla[l].astype(BF16),
        w_br_mem[l].astype(BF16), w_out[l].astype(BF16), ffn_norm_g[l][None, :], wr, br,
        B=B, S=S, tm=tm, mem_len=mem_len)

    R = 2 * T + (T // tm) * N_EXPERTS * RUN_ALIGN + N_EXPERTS * MOE_ROWS
    assert R % MOE_ROWS == 0 and R // MOE_ROWS <= META_PAD_END
    meta1 = _moe_pos(counts).reshape(META_LANES)
    cnt1 = cnt_tab[:, 0]
    carry1 = carry_tab[:, 0]
    xs = _dispatch(meta1, cnt1, carry1, loc, h2, R=R, tm=tm)
    ys = _moe_ffn(meta1, xs, w_gate_e[l], w_up_e[l], w_down_e[l])
    out = _combine(meta1, cnt1, carry1, x1, loc, wts, final_norm_g[None, :], ys, tm=tm)
    return out.reshape(B, S, D_MODEL)
```

```python
import functools
import math

import jax
import jax.numpy as jnp
from jax import lax
from jax.experimental import pallas as pl
from jax.experimental.pallas import tpu as pltpu

D_MODEL = 1024
POOL_WINDOWS = (2, 4, 8, 16)
POOL_GROUP_DIM = 128
POOL_DIM = 512
MLA_HEADS = 8
QK_NOPE_DIM = 128
QK_ROPE_DIM = 64
V_HEAD_DIM = 128
Q_LORA_RANK = 384
KV_LORA_RANK = 256
ROPE_THETA = 10000.0
XATTN_HEADS = 4
XATTN_HEAD_DIM = 128
XATTN_DIM = 512
N_BRANCHES = 3
N_GROUPS = 4
EXPERTS_PER_GROUP = 8
N_EXPERTS = 32
D_EXPERT = 256
RMS_EPS = 1e-6
NEG_INF = -1e30

LANES = 128
QK_PAD_DIM = 2 * LANES
POOL_HALO = 16
MOE_ROWS = 512
X_SLOTS = 4
W_STAGES = 3
RUN_ALIGN = 8
PACKED_DIM = D_MODEL // 2
ROUTER_ROWS = 40
META_LANES = 256
META_PAD_END = 192
META_NACT = 255
VMEM_LIMIT_BYTES = 56 * 1024 * 1024

IN_POOL, IN_QD, IN_KV, IN_XQ, IN_GATE, IN_KR, IN_END = 0, 512, 896, 1152, 1664, 4736, 4864
W_IN_KR, W_IN_XQ, W_IN_END = 1152, 1216, 4800

F32 = jnp.float32
BF16 = jnp.bfloat16
U32 = jnp.uint32


def _rms(x, g):
    ms = jnp.mean(x * x, axis=-1, keepdims=True)
    return (x * lax.rsqrt(ms + RMS_EPS)) * g


def _dot(a, b):
    return jnp.dot(a, b, preferred_element_type=F32)


def _dot_nt(a, b):
    return lax.dot_general(a, b, (((1,), (1,)), ((), ())), preferred_element_type=F32)


def _const_spec(shape):
    nd = len(shape)
    return pl.BlockSpec(shape, lambda *_: (0,) * nd, pipeline_mode=pl.Buffered(1))


def _pack_w_in_body(wt_ref, o_ref):
    chunk = LANES

    def copy_cols(dst, src, n):
        for r in range(0, n, chunk):
            m = min(chunk, n - r)
            o_ref[:, dst + r:dst + r + m] = wt_ref[src + r:src + r + m, :].T.astype(BF16)

    copy_cols(0, 0, W_IN_KR)
    copy_cols(IN_XQ, W_IN_XQ, W_IN_END - W_IN_XQ)
    copy_cols(IN_KR, W_IN_KR, W_IN_XQ - W_IN_KR)
    o_ref[:, IN_KR + QK_ROPE_DIM:IN_END] = jnp.zeros((D_MODEL, LANES - QK_ROPE_DIM), BF16)


def _pack_w_in(w_t):
    whole = pl.BlockSpec(memory_space=pltpu.VMEM)
    return pl.pallas_call(
        _pack_w_in_body,
        in_specs=[whole],
        out_specs=whole,
        out_shape=jax.ShapeDtypeStruct((D_MODEL, IN_END), BF16),
        compiler_params=pltpu.CompilerParams(vmem_limit_bytes=VMEM_LIMIT_BYTES),
        name="pack_w_in",
    )(w_t)


def _mixer_in_body(x_ref, pos_ref, invf_ref, mixg_ref, win_ref, gateb_ref, qg_ref, wuq_ref,
                   kvg_ref, wuk_ref, wuv_ref, poolw_ref, pools_ref,
                   ypool_ref, xq_ref, gates_ref, q_ref, k_ref, v_ref, ext_ref,
                   *, tm, tiles_per_seq, q_scale):
    si = lax.rem(pl.program_id(0), tiles_per_seq)

    @pl.when(pl.program_id(0) == 0)
    def _():
        ext_ref[0:POOL_HALO, :] = jnp.zeros((POOL_HALO, POOL_DIM), F32)

    hb = _rms(x_ref[...], mixg_ref[...]).astype(BF16)

    u = _dot(hb, win_ref[:, IN_POOL:IN_QD])
    ext_ref[0:POOL_HALO, :] = jnp.where(si == 0, 0.0, ext_ref[0:POOL_HALO, :])
    ext_ref[POOL_HALO:POOL_HALO + tm, :] = u

    for c in range(N_BRANCHES):
        gl = _dot(hb, win_ref[:, IN_GATE + c * D_MODEL:IN_GATE + (c + 1) * D_MODEL])
        gates_ref[:, c * D_MODEL:(c + 1) * D_MODEL] = jax.nn.sigmoid(gl + gateb_ref[c:c + 1, :]).astype(BF16)

    ang = pos_ref[...].astype(F32) * invf_ref[...]
    cos = jnp.cos(ang)
    sin = jnp.sin(ang)
    first_half = lax.broadcasted_iota(jnp.int32, (tm, LANES), 1) < (QK_ROPE_DIM // 2)
    sin_signed = jnp.where(first_half, -sin, sin)

    def rope(r):
        swapped = jnp.where(first_half, pltpu.roll(r, LANES - QK_ROPE_DIM // 2, 1),
                            pltpu.roll(r, QK_ROPE_DIM // 2, 1))
        return r * cos + swapped * sin_signed

    cq = _rms(_dot(hb, win_ref[:, IN_QD:IN_KV]), qg_ref[...]).astype(BF16)
    for h in range(MLA_HEADS):
        qh = _dot(cq, wuq_ref[:, h * QK_PAD_DIM:(h + 1) * QK_PAD_DIM])
        q_ref[0, h, :, 0:LANES] = (qh[:, 0:LANES] * q_scale).astype(BF16)
        q_ref[0, h, :, LANES:QK_PAD_DIM] = (rope(qh[:, LANES:QK_PAD_DIM]) * q_scale).astype(BF16)

    ckv = _rms(_dot(hb, win_ref[:, IN_KV:IN_XQ]), kvg_ref[...]).astype(BF16)
    kr = rope(_dot(hb, win_ref[:, IN_KR:IN_END])).astype(BF16)
    for hp in range(MLA_HEADS // 2):
        cols = slice(hp * 2 * LANES, (hp + 1) * 2 * LANES)
        kn = _dot(ckv, wuk_ref[:, cols]).astype(BF16)
        vv = _dot(ckv, wuv_ref[:, cols]).astype(BF16)
        for j in range(2):
            h = 2 * hp + j
            k_ref[0, h, :, 0:LANES] = kn[:, j * LANES:(j + 1) * LANES]
            k_ref[0, h, :, LANES:QK_PAD_DIM] = kr
            v_ref[0, h] = vv[:, j * LANES:(j + 1) * LANES]

    t_seq = lax.broadcasted_iota(jnp.int32, (tm, 1), 0) + si * tm
    for g, w in enumerate(POOL_WINDOWS):
        lo = g * POOL_GROUP_DIM
        hi = lo + POOL_GROUP_DIM
        acc = u[:, lo:hi]
        for j in range(1, w):
            acc = acc + ext_ref[POOL_HALO - j:POOL_HALO - j + tm, lo:hi]
        cnt = jnp.minimum(t_seq + 1, w).astype(F32)
        p = acc / cnt - u[:, lo:hi]
        y = _dot(p.astype(BF16), poolw_ref[g]) * pools_ref[:, lo:hi]
        ypool_ref[:, lo:hi] = y.astype(BF16)
    ext_ref[0:POOL_HALO, :] = ext_ref[tm:tm + POOL_HALO, :]

    xq_ref[...] = _dot(hb, win_ref[:, IN_XQ:IN_GATE]).astype(BF16)


def _mixer_in(x2, pos2, invf, mixg, win_p, gate_b, qg, wuq_p, kvg, wuk, wuv, pool_w, pool_s, *, B, S, tm):
    T = B * S
    tps = S // tm
    q_scale = (QK_NOPE_DIM + QK_ROPE_DIM) ** -0.5 * math.log2(math.e)
    body = functools.partial(_mixer_in_body, tm=tm, tiles_per_seq=tps, q_scale=q_scale)
    row = lambda i: (i, 0)
    head = lambda i: (i // tps, 0, i % tps, 0)
    return pl.pallas_call(
        body,
        grid=(T // tm,),
        in_specs=[
            pl.BlockSpec((tm, D_MODEL), row),
            pl.BlockSpec((tm, 1), row),
            _const_spec((1, LANES)),
            _const_spec((1, D_MODEL)),
            _const_spec((D_MODEL, IN_END)),
            _const_spec((N_BRANCHES, D_MODEL)),
            _const_spec((1, Q_LORA_RANK)),
            _const_spec((Q_LORA_RANK, MLA_HEADS * QK_PAD_DIM)),
            _const_spec((1, KV_LORA_RANK)),
            _const_spec((KV_LORA_RANK, MLA_HEADS * QK_NOPE_DIM)),
            _const_spec((KV_LORA_RANK, MLA_HEADS * V_HEAD_DIM)),
            _const_spec((len(POOL_WINDOWS), POOL_GROUP_DIM, POOL_GROUP_DIM)),
            _const_spec((1, POOL_DIM)),
        ],
        out_specs=[
            pl.BlockSpec((tm, POOL_DIM), row),
            pl.BlockSpec((tm, XATTN_DIM), row),
            pl.BlockSpec((tm, N_BRANCHES * D_MODEL), row),
            pl.BlockSpec((1, MLA_HEADS, tm, QK_PAD_DIM), head),
            pl.BlockSpec((1, MLA_HEADS, tm, QK_PAD_DIM), head),
            pl.BlockSpec((1, MLA_HEADS, tm, V_HEAD_DIM), head),
        ],
        out_shape=[
            jax.ShapeDtypeStruct((T, POOL_DIM), BF16),
            jax.ShapeDtypeStruct((T, XATTN_DIM), BF16),
            jax.ShapeDtypeStruct((T, N_BRANCHES * D_MODEL), BF16),
            jax.ShapeDtypeStruct((B, MLA_HEADS, S, QK_PAD_DIM), BF16),
            jax.ShapeDtypeStruct((B, MLA_HEADS, S, QK_PAD_DIM), BF16),
            jax.ShapeDtypeStruct((B, MLA_HEADS, S, V_HEAD_DIM), BF16),
        ],
        scratch_shapes=[pltpu.VMEM((tm + POOL_HALO, POOL_DIM), F32)],
        compiler_params=pltpu.CompilerParams(dimension_semantics=("arbitrary",),
                                             vmem_limit_bytes=VMEM_LIMIT_BYTES),
        name="mixer_in",
    )(x2, pos2, invf, mixg, win_p, gate_b, qg, wuq_p, kvg, wuk, wuv, pool_w, pool_s)


def _mem_kv_body(mem_ref, g_ref, w_ref, k_ref, v_ref):
    mb = _rms(mem_ref[...], g_ref[...]).astype(BF16)
    kv = _dot(mb, w_ref[...])
    k_ref[...] = kv[:, 0:XATTN_DIM].astype(BF16)
    v_ref[...] = kv[:, XATTN_DIM:2 * XATTN_DIM].astype(BF16)


def _mem_kv(mem2, g, w):
    rows = mem2.shape[0]
    tr = min(rows, 512)
    return pl.pallas_call(
        _mem_kv_body,
        grid=(rows // tr,),
        in_specs=[pl.BlockSpec((tr, D_MODEL), lambda i: (i, 0)),
                  _const_spec((1, D_MODEL)),
                  _const_spec((D_MODEL, 2 * XATTN_DIM))],
        out_specs=[pl.BlockSpec((tr, XATTN_DIM), lambda i: (i, 0)),
                   pl.BlockSpec((tr, XATTN_DIM), lambda i: (i, 0))],
        out_shape=[jax.ShapeDtypeStruct((rows, XATTN_DIM), BF16),
                   jax.ShapeDtypeStruct((rows, XATTN_DIM), BF16)],
        compiler_params=pltpu.CompilerParams(dimension_semantics=("arbitrary",)),
        name="mem_kv",
    )(mem2, g, w)


def _attn_unrolled_body(q_ref, k_ref, v_ref, o_ref, s_a, s_b, mc_a, mc_b, m_ref, l_ref, acc_ref, *, nq, tq):
    s_bufs = (s_a, s_b)
    mc_bufs = (mc_a, mc_b)
    mxu_row_split = 2
    blocks = [(qi, kb) for qi in range(nq) for kb in range(qi + 1)]

    def scores(i, slot):
        qi, kb = blocks[i]
        s = _dot_nt(q_ref[0, 0, qi * tq:(qi + 1) * tq, :], k_ref[0, 0, kb * tq:(kb + 1) * tq, :])
        if qi == kb:
            ri = lax.broadcasted_iota(jnp.int32, (tq, tq), 0)
            ci = lax.broadcasted_iota(jnp.int32, (tq, tq), 1)
            s = jnp.where(ci <= ri, s, NEG_INF)
        s_bufs[slot][...] = s
        mc_bufs[slot][...] = jnp.broadcast_to(jnp.max(s, axis=1, keepdims=True), (tq, LANES))

    def accumulate(i, slot):
        qi, kb = blocks[i]
        is_first = kb == 0
        is_last = kb == qi
        if is_first:
            m_new = mc_bufs[slot][...]
        else:
            m_prev = m_ref[...]
            m_new = jnp.maximum(m_prev, mc_bufs[slot][...])
            alpha = jnp.exp2(m_prev - m_new)
        p = jnp.exp2(s_bufs[slot][...] - jnp.concatenate([m_new] * (tq // LANES), axis=1))
        psum = p[:, 0:LANES]
        for c in range(1, tq // LANES):
            psum = psum + p[:, c * LANES:(c + 1) * LANES]
        l_new = psum if is_first else alpha * l_ref[...] + psum
        pb = p.astype(BF16)
        v = v_ref[0, 0, kb * tq:(kb + 1) * tq, :]
        if is_last:
            inv = 1.0 / jnp.sum(l_new, axis=1, keepdims=True)
        else:
            l_ref[...] = l_new
            m_ref[...] = m_new
        h = tq // mxu_row_split
        for r in range(mxu_row_split):
            rows = slice(r * h, (r + 1) * h)
            acc = _dot(pb[rows, :], v)
            if not is_first:
                acc = alpha[rows, :] * acc_ref[rows, :] + acc
            if is_last:
                o_ref[0, qi * tq + r * h:qi * tq + (r + 1) * h, :] = (acc * inv[rows, :]).astype(BF16)
            else:
                acc_ref[rows, :] = acc

    scores(0, 0)
    for i in range(len(blocks)):
        if i + 1 < len(blocks):
            scores(i + 1, (i + 1) % 2)
        accumulate(i, i % 2)


def _mla_attn_unrolled(q, k, v, *, tq):
    B, H, S, _ = q.shape
    per_head = lambda b, h: (b, h, 0, 0)
    return pl.pallas_call(
        functools.partial(_attn_unrolled_body, nq=S // tq, tq=tq),
        grid=(B, H),
        in_specs=[pl.BlockSpec((1, 1, S, QK_PAD_DIM), per_head),
                  pl.BlockSpec((1, 1, S, QK_PAD_DIM), per_head),
                  pl.BlockSpec((1, 1, S, V_HEAD_DIM), per_head)],
        out_specs=pl.BlockSpec((1, S, V_HEAD_DIM), lambda b, h: (b, 0, h)),
        out_shape=jax.ShapeDtypeStruct((B, S, H * V_HEAD_DIM), BF16),
        scratch_shapes=[pltpu.VMEM((tq, tq), F32), pltpu.VMEM((tq, tq), F32),
                        pltpu.VMEM((tq, LANES), F32), pltpu.VMEM((tq, LANES), F32),
                        pltpu.VMEM((tq, LANES), F32), pltpu.VMEM((tq, LANES), F32),
                        pltpu.VMEM((tq, V_HEAD_DIM), F32)],
        compiler_params=pltpu.CompilerParams(dimension_semantics=("arbitrary", "arbitrary"),
                                             vmem_limit_bytes=VMEM_LIMIT_BYTES),
        name="mla_attn",
    )(q, k, v)


def _merge_body(x_ref, ypool_ref, ymla_ref, xq_ref, gates_ref, kmem_ref, vmem_ref,
                wbp_ref, wbm_ref, wbx_ref, wout_ref, ffng_ref, wr_ref, br_ref,
                x1_ref, h2_ref, wts_ref, loc_ref, cnt_tab_ref, carry_tab_ref, counts_ref, carry_ref, *, tm):
    @pl.when(pl.program_id(0) == 0)
    def _():
        carry_ref[...] = jnp.zeros((N_EXPERTS, LANES), F32)

    xq = xq_ref[...]
    parts = []
    for h in range(XATTN_HEADS):
        cols = slice(h * XATTN_HEAD_DIM, (h + 1) * XATTN_HEAD_DIM)
        s = _dot_nt(xq[:, cols], kmem_ref[:, cols]) * (XATTN_HEAD_DIM ** -0.5)
        e = jnp.exp(s - jnp.max(s, axis=1, keepdims=True))
        p = e / jnp.sum(e, axis=1, keepdims=True)
        parts.append(_dot(p.astype(BF16), vmem_ref[:, cols]))
    ymem = jnp.concatenate(parts, axis=1).astype(BF16)

    gates = gates_ref[...].astype(F32)
    merged = (gates[:, 0:D_MODEL] * _dot(ypool_ref[...], wbp_ref[...])
              + gates[:, D_MODEL:2 * D_MODEL] * _dot(ymla_ref[...], wbm_ref[...])
              + gates[:, 2 * D_MODEL:3 * D_MODEL] * _dot(ymem, wbx_ref[...]))
    x1 = x_ref[...] + _dot(merged.astype(BF16), wout_ref[...])
    x1_ref[...] = x1
    h2 = _rms(x1, ffng_ref[...]).astype(BF16)
    h2_ref[...] = h2

    lt = _dot_nt(wr_ref[...], h2) + br_ref[...]
    gl = lt[N_EXPERTS:N_EXPERTS + N_GROUPS, :]
    gmax = jnp.max(gl, axis=0, keepdims=True)
    r4 = lax.broadcasted_iota(jnp.int32, (N_GROUPS, tm), 0).astype(F32)
    gidx = jnp.min(jnp.where(gl == gmax, r4, float(N_GROUPS)), axis=0, keepdims=True)
    pg = 1.0 / jnp.sum(jnp.exp(gl - gmax), axis=0, keepdims=True)
    esel = lt[0:EXPERTS_PER_GROUP, :]
    for g in range(1, N_GROUPS):
        esel = jnp.where(gidx == float(g), lt[g * EXPERTS_PER_GROUP:(g + 1) * EXPERTS_PER_GROUP, :], esel)
    r8 = lax.broadcasted_iota(jnp.int32, (EXPERTS_PER_GROUP, tm), 0).astype(F32)
    m1 = jnp.max(esel, axis=0, keepdims=True)
    i1 = jnp.min(jnp.where(esel == m1, r8, float(EXPERTS_PER_GROUP)), axis=0, keepdims=True)
    rest = jnp.where(r8 == i1, -jnp.inf, esel)
    m2 = jnp.max(rest, axis=0, keepdims=True)
    i2 = jnp.min(jnp.where(rest == m2, r8, float(EXPERTS_PER_GROUP)), axis=0, keepdims=True)
    e2 = jnp.exp(m2 - m1)
    den = 1.0 + e2
    wts_ref[0:1, :] = pg / den
    wts_ref[1:2, :] = pg * e2 / den
    ex1 = gidx * float(EXPERTS_PER_GROUP) + i1
    ex2 = gidx * float(EXPERTS_PER_GROUP) + i2

    r32 = lax.broadcasted_iota(jnp.int32, (N_EXPERTS, tm), 0).astype(F32)
    is1 = r32 == ex1
    is2 = r32 == ex2
    member = jnp.where(is1 | is2, 1.0, 0.0)
    upper = jnp.where(lax.broadcasted_iota(jnp.int32, (tm, tm), 0)
                      <= lax.broadcasted_iota(jnp.int32, (tm, tm), 1), 1.0, 0.0).astype(BF16)
    incl = _dot(member.astype(BF16), upper)
    run = jnp.floor((jnp.sum(member, axis=1, keepdims=True) + (RUN_ALIGN - 1)) / RUN_ALIGN) * RUN_ALIGN
    rcol = lax.broadcasted_iota(jnp.int32, (N_EXPERTS, 1), 0)
    run_start = jnp.zeros((N_EXPERTS, 1), F32)
    for e in range(N_EXPERTS - 1):
        run_start = run_start + jnp.where(rcol > e, run[e:e + 1, :], 0.0)
    pos = incl - 1.0 + run_start
    loc_ref[0:1, :] = jnp.sum(jnp.where(is1, pos, 0.0), axis=0, keepdims=True).astype(jnp.int32)
    loc_ref[1:2, :] = jnp.sum(jnp.where(is2, pos, 0.0), axis=0, keepdims=True).astype(jnp.int32)
    carry = carry_ref[...]
    total = carry + run
    cnt_tab_ref[...] = jnp.broadcast_to(run, (N_EXPERTS, LANES)).astype(jnp.int32)
    carry_tab_ref[...] = carry.astype(jnp.int32)
    carry_ref[...] = total
    counts_ref[...] = total.astype(jnp.int32)


def _merge(x2, ypool, ymla, xq, gates, kmem, vmem, wbp, wbm, wbx, wout, ffng, wr, br, *, B, S, tm, mem_len):
    T = B * S
    tps = S // tm
    row = lambda i: (i, 0)
    lane = lambda i: (0, i)
    memb = lambda i: (i // tps, 0)
    return pl.pallas_call(
        functools.partial(_merge_body, tm=tm),
        grid=(T // tm,),
        in_specs=[
            pl.BlockSpec((tm, D_MODEL), row),
            pl.BlockSpec((tm, POOL_DIM), row),
            pl.BlockSpec((tm, MLA_HEADS * V_HEAD_DIM), row),
            pl.BlockSpec((tm, XATTN_DIM), row),
            pl.BlockSpec((tm, N_BRANCHES * D_MODEL), row),
            pl.BlockSpec((mem_len, XATTN_DIM), memb),
            pl.BlockSpec((mem_len, XATTN_DIM), memb),
            _const_spec((POOL_DIM, D_MODEL)),
            _const_spec((MLA_HEADS * V_HEAD_DIM, D_MODEL)),
            _const_spec((XATTN_DIM, D_MODEL)),
            _const_spec((D_MODEL, D_MODEL)),
            _const_spec((1, D_MODEL)),
            _const_spec((ROUTER_ROWS, D_MODEL)),
            _const_spec((ROUTER_ROWS, 1)),
        ],
        out_specs=[
            pl.BlockSpec((tm, D_MODEL), row),
            pl.BlockSpec((tm, D_MODEL), row),
            pl.BlockSpec((2, tm), lane),
            pl.BlockSpec((2, tm), lane),
            pl.BlockSpec((N_EXPERTS, LANES), row),
            pl.BlockSpec((N_EXPERTS, LANES), row),
            pl.BlockSpec((N_EXPERTS, LANES), lambda i: (0, 0)),
        ],
        out_shape=[
            jax.ShapeDtypeStruct((T, D_MODEL), F32),
            jax.ShapeDtypeStruct((T, D_MODEL), BF16),
            jax.ShapeDtypeStruct((2, T), F32),
            jax.ShapeDtypeStruct((2, T), jnp.int32),
            jax.ShapeDtypeStruct((T // tm * N_EXPERTS, LANES), jnp.int32),
            jax.ShapeDtypeStruct((T // tm * N_EXPERTS, LANES), jnp.int32),
            jax.ShapeDtypeStruct((N_EXPERTS, LANES), jnp.int32),
        ],
        scratch_shapes=[pltpu.VMEM((N_EXPERTS, LANES), F32)],
        compiler_params=pltpu.CompilerParams(dimension_semantics=("arbitrary",),
                                             vmem_limit_bytes=VMEM_LIMIT_BYTES),
        name="merge",
    )(x2, ypool, ymla, xq, gates, kmem, vmem, wbp, wbm, wbx, wout, ffng, wr, br)


def _moe_pos_body(counts_ref, meta_ref):
    shift = int(math.log2(MOE_ROWS))
    cnt = counts_ref[...]
    padded = lax.shift_left(lax.shift_right_logical(cnt + (MOE_ROWS - 1), shift), shift)
    r32 = lax.broadcasted_iota(jnp.int32, (N_EXPERTS, LANES), 0)
    pad_start = jnp.zeros((N_EXPERTS, LANES), jnp.int32)
    for e in range(N_EXPERTS - 1):
        pad_start = pad_start + jnp.where(r32 > e, padded[e:e + 1, :], 0)
    pad_end = pad_start + padded

    lane = lax.broadcasted_iota(jnp.int32, (1, META_LANES), 1)
    block_row = lane * MOE_ROWS
    blk_e = jnp.zeros((1, META_LANES), jnp.int32)
    pe_row = jnp.zeros((1, META_LANES), jnp.int32)
    for e in range(N_EXPERTS):
        pe = pad_end[e:e + 1, 0:1]
        blk_e = blk_e + jnp.where(pe <= block_row, 1, 0)
        pe_row = pe_row + jnp.where(lane == META_PAD_END + e, pe, 0)
    blk_e = jnp.minimum(blk_e, N_EXPERTS - 1)
    nact = lax.shift_right_logical(pad_end[N_EXPERTS - 1:N_EXPERTS, 0:1], shift)
    meta = jnp.where(lane < META_PAD_END, blk_e, pe_row)
    meta_ref[...] = jnp.where(lane == META_NACT, nact, meta)


def _moe_pos(counts):
    full = lambda shape: pl.BlockSpec(shape, lambda i: (0,) * len(shape))
    return pl.pallas_call(
        _moe_pos_body,
        grid=(1,),
        in_specs=[full((N_EXPERTS, LANES))],
        out_specs=full((1, META_LANES)),
        out_shape=jax.ShapeDtypeStruct((1, META_LANES), jnp.int32),
        compiler_params=pltpu.CompilerParams(dimension_semantics=("arbitrary",)),
        name="moe_pos",
    )(counts)


def _pack_bf16_pairs(x):
    lo = pltpu.bitcast(x[:, 0:PACKED_DIM], U32)
    hi = pltpu.bitcast(x[:, PACKED_DIM:D_MODEL], U32)
    return hi | lax.shift_right_logical(lo, jnp.uint32(16))


def _unpack_bf16_pairs(w):
    lo = pltpu.bitcast(lax.shift_left(w, jnp.uint32(16)), F32)
    hi = pltpu.bitcast(w & jnp.uint32(0xFFFF0000), F32)
    return jnp.concatenate([lo, hi], axis=1)


def _loc_rows(tm):
    bf16_rows = 2 * RUN_ALIGN
    return pl.cdiv(2 * tm + N_EXPERTS * (RUN_ALIGN - 1), bf16_rows) * bf16_rows


def _run_copies(tile, cnt_ref, carry_ref, meta_ref, make_copy):
    def per_expert(e, local):
        n = pl.multiple_of(cnt_ref[tile * N_EXPERTS + e], RUN_ALIGN)
        start = jnp.where(e == 0, 0, meta_ref[META_PAD_END + jnp.maximum(e - 1, 0)])
        glob = pl.multiple_of(start + carry_ref[tile * N_EXPERTS + e], RUN_ALIGN)

        @pl.when(n > 0)
        def _():
            make_copy(pl.multiple_of(local, RUN_ALIGN), glob, n).start()

        return local + n

    return pl.multiple_of(lax.fori_loop(0, N_EXPERTS, per_expert, 0), RUN_ALIGN)


def _tile_rows(tile, cnt_ref):
    total = lax.fori_loop(0, N_EXPERTS, lambda e, t: t + cnt_ref[tile * N_EXPERTS + e], 0)
    return pl.multiple_of(total, RUN_ALIGN)


def _dispatch_body(meta_ref, cnt_ref, carry_ref, loc_ref, h2_ref, xs_ref, xloc_ref, zero_ref, sems, zsem,
                   *, tm, n_blocks):
    tile = pl.program_id(0)
    last_tile = pl.num_programs(0) - 1
    slot = lax.rem(tile, 2)
    nact = meta_ref[META_NACT]

    def wait_rows(t, s):
        n = _tile_rows(t, cnt_ref)
        pltpu.make_async_copy(xloc_ref.at[s, pl.ds(0, n)], xs_ref.at[pl.ds(0, n)], sems.at[s]).wait()

    def pad_copy(e):
        end = pl.multiple_of(meta_ref[META_PAD_END + e], MOE_ROWS)
        start = jnp.where(e == 0, 0, meta_ref[META_PAD_END + jnp.maximum(e - 1, 0)])
        used = carry_ref[last_tile * N_EXPERTS + e] + cnt_ref[last_tile * N_EXPERTS + e]
        first = pl.multiple_of(start + used, RUN_ALIGN)
        n = pl.multiple_of(end - first, RUN_ALIGN)
        return n, pltpu.make_async_copy(zero_ref.at[pl.ds(0, n)], xs_ref.at[pl.ds(first, n)], zsem)

    def tail_copy(b):
        return pltpu.make_async_copy(
            zero_ref, xs_ref.at[pl.ds(pl.multiple_of(b * MOE_ROWS, MOE_ROWS), MOE_ROWS)], zsem)

    def fill(op):
        def pad(e, c):
            n, cp = pad_copy(e)

            @pl.when(n > 0)
            def _():
                op(cp)
            return c

        def tail(b, c):
            op(tail_copy(b))
            return c

        lax.fori_loop(0, N_EXPERTS, pad, 0)
        lax.fori_loop(nact, n_blocks, tail, 0)

    @pl.when(tile == 0)
    def _():
        zero_ref[...] = jnp.zeros((MOE_ROWS, PACKED_DIM), U32)
        fill(lambda cp: cp.start())

    @pl.when(tile >= 2)
    def _():
        wait_rows(tile - 2, slot)

    r = lax.broadcasted_iota(jnp.int32, (_loc_rows(tm), tm), 0)
    onehot = jnp.where((r == loc_ref[0:1, :]) | (r == loc_ref[1:2, :]), 1.0, 0.0).astype(BF16)
    xloc_ref[slot] = _pack_bf16_pairs(_dot(onehot, h2_ref[...]))

    def make_copy(local, glob, n):
        return pltpu.make_async_copy(xloc_ref.at[slot, pl.ds(local, n)], xs_ref.at[pl.ds(glob, n)],
                                     sems.at[slot])

    _run_copies(tile, cnt_ref, carry_ref, meta_ref, make_copy)

    @pl.when(tile == last_tile)
    def _():
        @pl.when(tile >= 1)
        def _():
            wait_rows(tile - 1, 1 - slot)

        wait_rows(tile, slot)
        fill(lambda cp: cp.wait())


def _dispatch(meta1, cnt_tab, carry_tab, loc, h2, *, R, tm):
    T = h2.shape[0]
    return pl.pallas_call(
        functools.partial(_dispatch_body, tm=tm, n_blocks=R // MOE_ROWS),
        grid_spec=pltpu.PrefetchScalarGridSpec(
            num_scalar_prefetch=3,
            grid=(T // tm,),
            in_specs=[pl.BlockSpec((2, tm), lambda i, *_: (0, i)),
                      pl.BlockSpec((tm, D_MODEL), lambda i, *_: (i, 0))],
            out_specs=pl.BlockSpec(memory_space=pl.ANY),
            scratch_shapes=[pltpu.VMEM((2, _loc_rows(tm), PACKED_DIM), U32),
                            pltpu.VMEM((MOE_ROWS, PACKED_DIM), U32),
                            pltpu.SemaphoreType.DMA((2,)), pltpu.SemaphoreType.DMA],
        ),
        out_shape=jax.ShapeDtypeStruct((R, PACKED_DIM), U32),
        compiler_params=pltpu.CompilerParams(dimension_semantics=("arbitrary",),
                                             vmem_limit_bytes=VMEM_LIMIT_BYTES),
        name="dispatch",
    )(meta1, cnt_tab, carry_tab, loc, h2)


def _moe_ffn_body(meta_ref, xs_ref, wg_hbm, wu_hbm, wd_hbm, ys_ref,
                  xbuf, ybuf, wg_stage, wu_stage, wd_stage, wg_b, wu_b, wd_b, zero_ref,
                  xsem, ysem, wsem, zsem, *, n_blocks):
    nact = meta_ref[META_NACT]
    shift = int(math.log2(MOE_ROWS))

    def rows_of(b):
        return pl.ds(pl.multiple_of(b * MOE_ROWS, MOE_ROWS), MOE_ROWS)

    def x_copy(b, s):
        return pltpu.make_async_copy(xs_ref.at[rows_of(b)], xbuf.at[s], xsem.at[s])

    def y_copy(b, s):
        return pltpu.make_async_copy(ybuf.at[s], ys_ref.at[rows_of(b)], ysem.at[s])

    def w_copies(e, s):
        return (pltpu.make_async_copy(wg_hbm.at[e], wg_stage.at[s], wsem.at[s]),
                pltpu.make_async_copy(wu_hbm.at[e], wu_stage.at[s], wsem.at[s]),
                pltpu.make_async_copy(wd_hbm.at[e], wd_stage.at[s], wsem.at[s]))

    def tail_copy(b):
        return pltpu.make_async_copy(zero_ref, ys_ref.at[rows_of(b)], zsem)

    zero_ref[...] = jnp.zeros((MOE_ROWS, PACKED_DIM), U32)
    lax.fori_loop(nact, n_blocks, lambda b, c: (tail_copy(b).start(), c)[1], 0)

    def next_expert_block(e):
        return lax.shift_right_logical(meta_ref[META_PAD_END + e], shift)

    def start_weights(b, s):
        @pl.when(b < nact)
        def _():
            for cp in w_copies(meta_ref[jnp.minimum(b, n_blocks - 1)], s):
                cp.start()

    x_copy(0, 0).start()

    @pl.when(nact > 1)
    def _():
        x_copy(1, 1).start()

    e_first = meta_ref[0]
    start_weights(0, 0)
    start_weights(next_expert_block(e_first), 1)

    def block(b, xs, ys, k_prev):
        valid = b < nact
        e = meta_ref[jnp.minimum(b, nact - 1)]
        changed = jnp.logical_and(valid, jnp.logical_or(b == 0, e != meta_ref[jnp.maximum(b - 1, 0)]))
        k = jnp.where(changed, k_prev + 1, k_prev)

        @pl.when(changed)
        def _():
            ws = lax.rem(k, W_STAGES)
            for cp in w_copies(e, ws):
                cp.wait()
            wg_b[...] = wg_stage[ws].astype(BF16)
            wu_b[...] = wu_stage[ws].astype(BF16)
            wd_b[...] = wd_stage[ws].astype(BF16)
            n1 = next_expert_block(e)
            e1 = meta_ref[jnp.minimum(n1, n_blocks - 1)]
            n2 = jnp.where(n1 < nact, next_expert_block(e1), n_blocks)
            start_weights(n2, lax.rem(k + 2, W_STAGES))

        @pl.when(valid)
        def _():
            x_copy(b, xs).wait()

            @pl.when(b + 2 < nact)
            def _():
                x_copy(b + 2, (xs + 2) % X_SLOTS).start()

            @pl.when(b >= 2)
            def _():
                y_copy(b - 2, ys).wait()

            x = _unpack_bf16_pairs(xbuf[xs]).astype(BF16)
            g = _dot(x, wg_b[...])
            a = (g * jax.nn.sigmoid(g)) * _dot(x, wu_b[...])
            y = _dot(a.astype(BF16), wd_b[...])
            ybuf[ys] = _pack_bf16_pairs(y.astype(BF16).astype(F32))
            y_copy(b, ys).start()

        return k

    def quad(i, k):
        for j in range(X_SLOTS):
            k = block(X_SLOTS * i + j, j, j % 2, k)
        return k

    lax.fori_loop(0, lax.div(nact + (X_SLOTS - 1), X_SLOTS), quad, -1)

    @pl.when(nact >= 2)
    def _():
        y_copy(nact - 2, lax.rem(nact, 2)).wait()

    y_copy(nact - 1, lax.rem(nact - 1, 2)).wait()
    lax.fori_loop(nact, n_blocks, lambda b, c: (tail_copy(b).wait(), c)[1], 0)


def _moe_ffn(meta1, xs, wg, wu, wd):
    R = xs.shape[0]
    hbm = pl.BlockSpec(memory_space=pl.ANY)
    return pl.pallas_call(
        functools.partial(_moe_ffn_body, n_blocks=R // MOE_ROWS),
        grid_spec=pltpu.PrefetchScalarGridSpec(
            num_scalar_prefetch=1,
            grid=(1,),
            in_specs=[hbm, hbm, hbm, hbm],
            out_specs=hbm,
            scratch_shapes=[pltpu.VMEM((X_SLOTS, MOE_ROWS, PACKED_DIM), U32),
                            pltpu.VMEM((2, MOE_ROWS, PACKED_DIM), U32),
                            pltpu.VMEM((W_STAGES, D_MODEL, D_EXPERT), F32),
                            pltpu.VMEM((W_STAGES, D_MODEL, D_EXPERT), F32),
                            pltpu.VMEM((W_STAGES, D_EXPERT, D_MODEL), F32),
                            pltpu.VMEM((D_MODEL, D_EXPERT), BF16),
                            pltpu.VMEM((D_MODEL, D_EXPERT), BF16),
                            pltpu.VMEM((D_EXPERT, D_MODEL), BF16),
                            pltpu.VMEM((MOE_ROWS, PACKED_DIM), U32),
                            pltpu.SemaphoreType.DMA((X_SLOTS,)), pltpu.SemaphoreType.DMA((2,)),
                            pltpu.SemaphoreType.DMA((W_STAGES,)), pltpu.SemaphoreType.DMA],
        ),
        out_shape=jax.ShapeDtypeStruct((R, PACKED_DIM), U32),
        compiler_params=pltpu.CompilerParams(dimension_semantics=("arbitrary",),
                                             vmem_limit_bytes=VMEM_LIMIT_BYTES),
        name="moe_ffn",
    )(meta1, xs, wg, wu, wd)


def _combine_body(meta_ref, cnt_ref, carry_ref, x1_ref, loc_ref, wts_ref, fg_ref, ys_ref, out_ref,
                  yloc_ref, sems, *, tm):
    tile = pl.program_id(0)
    slot = lax.rem(tile, 2)

    def fetch(t, s):
        def make_copy(local, glob, n):
            return pltpu.make_async_copy(ys_ref.at[pl.ds(glob, n)], yloc_ref.at[s, pl.ds(local, n)],
                                         sems.at[s])
        _run_copies(t, cnt_ref, carry_ref, meta_ref, make_copy)

    @pl.when(tile == 0)
    def _():
        yloc_ref[...] = jnp.zeros(yloc_ref.shape, U32)
        fetch(tile, slot)

    @pl.when(tile + 1 < pl.num_programs(0))
    def _():
        fetch(tile + 1, 1 - slot)

    n = _tile_rows(tile, cnt_ref)
    pltpu.make_async_copy(ys_ref.at[pl.ds(0, n)], yloc_ref.at[slot, pl.ds(0, n)], sems.at[slot]).wait()
    r = lax.broadcasted_iota(jnp.int32, (_loc_rows(tm), tm), 0)
    is0 = r == loc_ref[0:1, :]
    is1 = r == loc_ref[1:2, :]
    row_w = jnp.sum(jnp.where(is0, wts_ref[0:1, :], 0.0) + jnp.where(is1, wts_ref[1:2, :], 0.0),
                    axis=1, keepdims=True)
    yw = (row_w * _unpack_bf16_pairs(yloc_ref[slot])).astype(BF16)
    twohot = jnp.where(is0 | is1, 1.0, 0.0).astype(BF16)
    moe = lax.dot_general(twohot, yw, (((0,), (0,)), ((), ())), preferred_element_type=F32)
    out_ref[...] = _rms(x1_ref[...] + moe, fg_ref[...])


def _combine(meta1, cnt_tab, carry_tab, x1, loc, wts, fg, ys, *, tm):
    T = x1.shape[0]
    row = lambda i, *_: (i, 0)
    lane = lambda i, *_: (0, i)
    return pl.pallas_call(
        functools.partial(_combine_body, tm=tm),
        grid_spec=pltpu.PrefetchScalarGridSpec(
            num_scalar_prefetch=3,
            grid=(T // tm,),
            in_specs=[pl.BlockSpec((tm, D_MODEL), row),
                      pl.BlockSpec((2, tm), lane),
                      pl.BlockSpec((2, tm), lane),
                      pl.BlockSpec((1, D_MODEL), lambda i, *_: (0, 0)),
                      pl.BlockSpec(memory_space=pl.ANY)],
            out_specs=pl.BlockSpec((tm, D_MODEL), row),
            scratch_shapes=[pltpu.VMEM((2, _loc_rows(tm), PACKED_DIM), U32),
                            pltpu.SemaphoreType.DMA((2,))],
        ),
        out_shape=jax.ShapeDtypeStruct((T, D_MODEL), F32),
        compiler_params=pltpu.CompilerParams(dimension_semantics=("arbitrary",),
                                             vmem_limit_bytes=VMEM_LIMIT_BYTES),
        name="combine",
    )(meta1, cnt_tab, carry_tab, x1, loc, wts, fg, ys)


def _tile(n, t):
    t = min(n, t)
    assert n % t == 0, (n, t)
    return t


def kernel(x, mem, positions, mix_norm_g, w_in, gate_b, q_norm_g, w_uq, kv_norm_g, w_uk, w_uv, pool_w, pool_scale, mem_norm_g, w_mem_kv, w_br_pool, w_br_mla, w_br_mem, w_out, ffn_norm_g, w_router_group, b_router_group, w_router_expert, b_router_expert, w_gate_e, w_up_e, w_down_e, final_norm_g):
    B, S, D = x.shape
    assert D == D_MODEL and mix_norm_g.shape[0] == 1
    T = B * S
    mem_len = mem.shape[1]
    tm = _tile(S, 512)
    l = 0

    win_p = _pack_w_in(jnp.swapaxes(w_in, 1, 2).reshape(W_IN_END, D_MODEL))
    wuq_p = jnp.pad(w_uq[l].reshape(Q_LORA_RANK, MLA_HEADS, QK_NOPE_DIM + QK_ROPE_DIM),
                    ((0, 0), (0, 0), (0, QK_PAD_DIM - QK_NOPE_DIM - QK_ROPE_DIM))
                    ).reshape(Q_LORA_RANK, MLA_HEADS * QK_PAD_DIM).astype(BF16)
    inv_freq = 1.0 / (ROPE_THETA ** (jnp.arange(0, QK_ROPE_DIM, 2, dtype=F32) / QK_ROPE_DIM))
    invf = jnp.concatenate([inv_freq, inv_freq, jnp.zeros((LANES - QK_ROPE_DIM,), F32)])[None, :]
    wr = jnp.concatenate([w_router_expert[l], w_router_group[l],
                          jnp.zeros((D_MODEL, ROUTER_ROWS - N_EXPERTS - N_GROUPS), F32)], axis=1).T.astype(BF16)
    br = jnp.concatenate([b_router_expert[l], b_router_group[l],
                          jnp.zeros((ROUTER_ROWS - N_EXPERTS - N_GROUPS,), F32)])[:, None].astype(F32)
    x2 = x.reshape(T, D_MODEL)
    pos2 = positions.reshape(T, 1)

    ypool, xq, gates, q, k, v = _mixer_in(
        x2, pos2, invf, mix_norm_g[l][None, :], win_p, gate_b[l], q_norm_g[l][None, :], wuq_p,
        kv_norm_g[l][None, :], w_uk[l].astype(BF16), w_uv[l].astype(BF16), pool_w[l].astype(BF16),
        pool_scale[l][None, :], B=B, S=S, tm=tm)
    kmem, vmem = _mem_kv(mem.reshape(B * mem_len, D_MODEL), mem_norm_g[l][None, :], w_mem_kv[l].astype(BF16))
    ymla = _mla_attn_unrolled(q, k, v, tq=tm).reshape(T, MLA_HEADS * V_HEAD_DIM)
    x1, h2, wts, loc, cnt_tab, carry_tab, counts = _merge(
        x2, ypool, ymla, xq, gates, kmem, vmem, w_br_pool[l].astype(BF16), w_br_mla[l].astype(BF16),
        w_br_mem[l].astype(BF16), w_out[l].astype(BF16), ffn_norm_g[l][None, :], wr, br,
        B=B, S=S, tm=tm, mem_len=mem_len)

    R = 2 * T + (T // tm) * N_EXPERTS * RUN_ALIGN + N_EXPERTS * MOE_ROWS
    assert R % MOE_ROWS == 0 and R // MOE_ROWS <= META_PAD_END
    meta1 = _moe_pos(counts).reshape(META_LANES)
    cnt1 = cnt_tab[:, 0]
    carry1 = carry_tab[:, 0]
    xs = _dispatch(meta1, cnt1, carry1, loc, h2, R=R, tm=tm)
    ys = _moe_ffn(meta1, xs, w_gate_e[l], w_up_e[l], w_down_e[l])
    out = _combine(meta1, cnt1, carry1, x1, loc, wts, final_norm_g[None, :], ys, tm=tm)
    return out.reshape(B, S, D_MODEL)
```

```python
import functools
import math

import jax
import jax.numpy as jnp
from jax import lax
from jax.experimental import pallas as pl
from jax.experimental.pallas import tpu as pltpu

D_MODEL = 1024
POOL_WINDOWS = (2, 4, 8, 16)
POOL_GROUP_DIM = 128
POOL_DIM = 512
MLA_HEADS = 8
QK_NOPE_DIM = 128
QK_ROPE_DIM = 64
V_HEAD_DIM = 128
Q_LORA_RANK = 384
KV_LORA_RANK = 256
ROPE_THETA = 10000.0
XATTN_HEADS = 4
XATTN_HEAD_DIM = 128
XATTN_DIM = 512
N_BRANCHES = 3
N_GROUPS = 4
EXPERTS_PER_GROUP = 8
N_EXPERTS = 32
D_EXPERT = 256
RMS_EPS = 1e-6
NEG_INF = -1e30

LANES = 128
QK_PAD_DIM = 2 * LANES
POOL_HALO = 16
MOE_ROWS = 512
X_SLOTS = 4
W_STAGES = 3
RUN_ALIGN = 8
TILES_PER_STEP = 2
PACKED_DIM = D_MODEL // 2
ROUTER_ROWS = 40
META_LANES = 256
META_PAD_END = 192
META_NACT = 255
VMEM_LIMIT_BYTES = 56 * 1024 * 1024

IN_POOL, IN_QD, IN_KV, IN_XQ, IN_GATE, IN_KR, IN_END = 0, 512, 896, 1152, 1664, 4736, 4864
W_IN_KR, W_IN_XQ, W_IN_END = 1152, 1216, 4800

F32 = jnp.float32
BF16 = jnp.bfloat16
U32 = jnp.uint32


def _rms(x, g):
    ms = jnp.mean(x * x, axis=-1, keepdims=True)
    return (x * lax.rsqrt(ms + RMS_EPS)) * g


def _dot(a, b):
    return jnp.dot(a, b, preferred_element_type=F32)


def _dot_nt(a, b):
    return lax.dot_general(a, b, (((1,), (1,)), ((), ())), preferred_element_type=F32)


def _const_spec(shape):
    nd = len(shape)
    return pl.BlockSpec(shape, lambda *_: (0,) * nd, pipeline_mode=pl.Buffered(1))


def _pack_w_in_body(wt_ref, o_ref):
    chunk = LANES

    def copy_cols(dst, src, n):
        for r in range(0, n, chunk):
            m = min(chunk, n - r)
            o_ref[:, dst + r:dst + r + m] = wt_ref[src + r:src + r + m, :].T.astype(BF16)

    copy_cols(0, 0, W_IN_KR)
    copy_cols(IN_XQ, W_IN_XQ, W_IN_END - W_IN_XQ)
    copy_cols(IN_KR, W_IN_KR, W_IN_XQ - W_IN_KR)
    o_ref[:, IN_KR + QK_ROPE_DIM:IN_END] = jnp.zeros((D_MODEL, LANES - QK_ROPE_DIM), BF16)


def _pack_w_in(w_t):
    whole = pl.BlockSpec(memory_space=pltpu.VMEM)
    return pl.pallas_call(
        _pack_w_in_body,
        in_specs=[whole],
        out_specs=whole,
        out_shape=jax.ShapeDtypeStruct((D_MODEL, IN_END), BF16),
        compiler_params=pltpu.CompilerParams(vmem_limit_bytes=VMEM_LIMIT_BYTES),
        name="pack_w_in",
    )(w_t)


def _mixer_in_body(x_ref, pos_ref, invf_ref, mixg_ref, win_ref, gateb_ref, qg_ref, wuq_ref,
                   kvg_ref, wuk_ref, wuv_ref, poolw_ref, pools_ref,
                   ypool_ref, xq_ref, gates_ref, q_ref, k_ref, v_ref, ext_ref,
                   *, tm, tiles_per_seq, q_scale):
    si = lax.rem(pl.program_id(0), tiles_per_seq)

    @pl.when(pl.program_id(0) == 0)
    def _():
        ext_ref[0:POOL_HALO, :] = jnp.zeros((POOL_HALO, POOL_DIM), F32)

    hb = _rms(x_ref[...], mixg_ref[...]).astype(BF16)

    u = _dot(hb, win_ref[:, IN_POOL:IN_QD])
    ext_ref[0:POOL_HALO, :] = jnp.where(si == 0, 0.0, ext_ref[0:POOL_HALO, :])
    ext_ref[POOL_HALO:POOL_HALO + tm, :] = u

    for c in range(N_BRANCHES):
        gl = _dot(hb, win_ref[:, IN_GATE + c * D_MODEL:IN_GATE + (c + 1) * D_MODEL])
        gates_ref[:, c * D_MODEL:(c + 1) * D_MODEL] = jax.nn.sigmoid(gl + gateb_ref[c:c + 1, :]).astype(BF16)

    ang = pos_ref[...].astype(F32) * invf_ref[...]
    cos = jnp.cos(ang)
    sin = jnp.sin(ang)
    first_half = lax.broadcasted_iota(jnp.int32, (tm, LANES), 1) < (QK_ROPE_DIM // 2)
    sin_signed = jnp.where(first_half, -sin, sin)

    def rope(r):
        swapped = jnp.where(first_half, pltpu.roll(r, LANES - QK_ROPE_DIM // 2, 1),
                            pltpu.roll(r, QK_ROPE_DIM // 2, 1))
        return r * cos + swapped * sin_signed

    cq = _rms(_dot(hb, win_ref[:, IN_QD:IN_KV]), qg_ref[...]).astype(BF16)
    for h in range(MLA_HEADS):
        qh = _dot(cq, wuq_ref[:, h * QK_PAD_DIM:(h + 1) * QK_PAD_DIM])
        q_ref[0, h, :, 0:LANES] = (qh[:, 0:LANES] * q_scale).astype(BF16)
        q_ref[0, h, :, LANES:QK_PAD_DIM] = (rope(qh[:, LANES:QK_PAD_DIM]) * q_scale).astype(BF16)

    ckv = _rms(_dot(hb, win_ref[:, IN_KV:IN_XQ]), kvg_ref[...]).astype(BF16)
    kr = rope(_dot(hb, win_ref[:, IN_KR:IN_END])).astype(BF16)
    for hp in range(MLA_HEADS // 2):
        cols = slice(hp * 2 * LANES, (hp + 1) * 2 * LANES)
        kn = _dot(ckv, wuk_ref[:, cols]).astype(BF16)
        vv = _dot(ckv, wuv_ref[:, cols]).astype(BF16)
        for j in range(2):
            h = 2 * hp + j
            k_ref[0, h, :, 0:LANES] = kn[:, j * LANES:(j + 1) * LANES]
            k_ref[0, h, :, LANES:QK_PAD_DIM] = kr
            v_ref[0, h] = vv[:, j * LANES:(j + 1) * LANES]

    t_seq = lax.broadcasted_iota(jnp.int32, (tm, 1), 0) + si * tm
    for g, w in enumerate(POOL_WINDOWS):
        lo = g * POOL_GROUP_DIM
        hi = lo + POOL_GROUP_DIM
        acc = u[:, lo:hi]
        for j in range(1, w):
            acc = acc + ext_ref[POOL_HALO - j:POOL_HALO - j + tm, lo:hi]
        cnt = jnp.minimum(t_seq + 1, w).astype(F32)
        p = acc / cnt - u[:, lo:hi]
        y = _dot(p.astype(BF16), poolw_ref[g]) * pools_ref[:, lo:hi]
        ypool_ref[:, lo:hi] = y.astype(BF16)
    ext_ref[0:POOL_HALO, :] = ext_ref[tm:tm + POOL_HALO, :]

    xq_ref[...] = _dot(hb, win_ref[:, IN_XQ:IN_GATE]).astype(BF16)


def _mixer_in(x2, pos2, invf, mixg, win_p, gate_b, qg, wuq_p, kvg, wuk, wuv, pool_w, pool_s, *, B, S, tm):
    T = B * S
    tps = S // tm
    q_scale = (QK_NOPE_DIM + QK_ROPE_DIM) ** -0.5 * math.log2(math.e)
    body = functools.partial(_mixer_in_body, tm=tm, tiles_per_seq=tps, q_scale=q_scale)
    row = lambda i: (i, 0)
    head = lambda i: (i // tps, 0, i % tps, 0)
    return pl.pallas_call(
        body,
        grid=(T // tm,),
        in_specs=[
            pl.BlockSpec((tm, D_MODEL), row),
            pl.BlockSpec((tm, 1), row),
            _const_spec((1, LANES)),
            _const_spec((1, D_MODEL)),
            _const_spec((D_MODEL, IN_END)),
            _const_spec((N_BRANCHES, D_MODEL)),
            _const_spec((1, Q_LORA_RANK)),
            _const_spec((Q_LORA_RANK, MLA_HEADS * QK_PAD_DIM)),
            _const_spec((1, KV_LORA_RANK)),
            _const_spec((KV_LORA_RANK, MLA_HEADS * QK_NOPE_DIM)),
            _const_spec((KV_LORA_RANK, MLA_HEADS * V_HEAD_DIM)),
            _const_spec((len(POOL_WINDOWS), POOL_GROUP_DIM, POOL_GROUP_DIM)),
            _const_spec((1, POOL_DIM)),
        ],
        out_specs=[
            pl.BlockSpec((tm, POOL_DIM), row),
            pl.BlockSpec((tm, XATTN_DIM), row),
            pl.BlockSpec((tm, N_BRANCHES * D_MODEL), row),
            pl.BlockSpec((1, MLA_HEADS, tm, QK_PAD_DIM), head),
            pl.BlockSpec((1, MLA_HEADS, tm, QK_PAD_DIM), head),
            pl.BlockSpec((1, MLA_HEADS, tm, V_HEAD_DIM), head),
        ],
        out_shape=[
            jax.ShapeDtypeStruct((T, POOL_DIM), BF16),
            jax.ShapeDtypeStruct((T, XATTN_DIM), BF16),
            jax.ShapeDtypeStruct((T, N_BRANCHES * D_MODEL), BF16),
            jax.ShapeDtypeStruct((B, MLA_HEADS, S, QK_PAD_DIM), BF16),
            jax.ShapeDtypeStruct((B, MLA_HEADS, S, QK_PAD_DIM), BF16),
            jax.ShapeDtypeStruct((B, MLA_HEADS, S, V_HEAD_DIM), BF16),
        ],
        scratch_shapes=[pltpu.VMEM((tm + POOL_HALO, POOL_DIM), F32)],
        compiler_params=pltpu.CompilerParams(dimension_semantics=("arbitrary",),
                                             vmem_limit_bytes=VMEM_LIMIT_BYTES),
        name="mixer_in",
    )(x2, pos2, invf, mixg, win_p, gate_b, qg, wuq_p, kvg, wuk, wuv, pool_w, pool_s)


def _mem_kv_body(mem_ref, g_ref, w_ref, k_ref, v_ref):
    mb = _rms(mem_ref[...], g_ref[...]).astype(BF16)
    kv = _dot(mb, w_ref[...])
    k_ref[...] = kv[:, 0:XATTN_DIM].astype(BF16)
    v_ref[...] = kv[:, XATTN_DIM:2 * XATTN_DIM].astype(BF16)


def _mem_kv(mem2, g, w):
    rows = mem2.shape[0]
    tr = min(rows, 512)
    return pl.pallas_call(
        _mem_kv_body,
        grid=(rows // tr,),
        in_specs=[pl.BlockSpec((tr, D_MODEL), lambda i: (i, 0)),
                  _const_spec((1, D_MODEL)),
                  _const_spec((D_MODEL, 2 * XATTN_DIM))],
        out_specs=[pl.BlockSpec((tr, XATTN_DIM), lambda i: (i, 0)),
                   pl.BlockSpec((tr, XATTN_DIM), lambda i: (i, 0))],
        out_shape=[jax.ShapeDtypeStruct((rows, XATTN_DIM), BF16),
                   jax.ShapeDtypeStruct((rows, XATTN_DIM), BF16)],
        compiler_params=pltpu.CompilerParams(dimension_semantics=("arbitrary",)),
        name="mem_kv",
    )(mem2, g, w)


def _attn_unrolled_body(q_ref, k_ref, v_ref, o_ref, s_a, s_b, mc_a, mc_b, m_ref, l_ref, acc_ref, *, nq, tq):
    s_bufs = (s_a, s_b)
    mc_bufs = (mc_a, mc_b)
    mxu_row_split = 2
    blocks = [(qi, kb) for qi in range(nq) for kb in range(qi + 1)]

    def scores(i, slot):
        qi, kb = blocks[i]
        s = _dot_nt(q_ref[0, 0, qi * tq:(qi + 1) * tq, :], k_ref[0, 0, kb * tq:(kb + 1) * tq, :])
        if qi == kb:
            ri = lax.broadcasted_iota(jnp.int32, (tq, tq), 0)
            ci = lax.broadcasted_iota(jnp.int32, (tq, tq), 1)
            s = jnp.where(ci <= ri, s, NEG_INF)
        s_bufs[slot][...] = s
        mc_bufs[slot][...] = jnp.broadcast_to(jnp.max(s, axis=1, keepdims=True), (tq, LANES))

    def accumulate(i, slot):
        qi, kb = blocks[i]
        is_first = kb == 0
        is_last = kb == qi
        if is_first:
            m_new = mc_bufs[slot][...]
        else:
            m_prev = m_ref[...]
            m_new = jnp.maximum(m_prev, mc_bufs[slot][...])
            alpha = jnp.exp2(m_prev - m_new)
        p = jnp.exp2(s_bufs[slot][...] - jnp.concatenate([m_new] * (tq // LANES), axis=1))
        psum = p[:, 0:LANES]
        for c in range(1, tq // LANES):
            psum = psum + p[:, c * LANES:(c + 1) * LANES]
        l_new = psum if is_first else alpha * l_ref[...] + psum
        pb = p.astype(BF16)
        v = v_ref[0, 0, kb * tq:(kb + 1) * tq, :]
        if is_last:
            inv = 1.0 / jnp.sum(l_new, axis=1, keepdims=True)
        else:
            l_ref[...] = l_new
            m_ref[...] = m_new
        h = tq // mxu_row_split
        for r in range(mxu_row_split):
            rows = slice(r * h, (r + 1) * h)
            acc = _dot(pb[rows, :], v)
            if not is_first:
                acc = alpha[rows, :] * acc_ref[rows, :] + acc
            if is_last:
                o_ref[0, qi * tq + r * h:qi * tq + (r + 1) * h, :] = (acc * inv[rows, :]).astype(BF16)
            else:
                acc_ref[rows, :] = acc

    scores(0, 0)
    for i in range(len(blocks)):
        if i + 1 < len(blocks):
            scores(i + 1, (i + 1) % 2)
        accumulate(i, i % 2)


def _mla_attn_unrolled(q, k, v, *, tq):
    B, H, S, _ = q.shape
    per_head = lambda b, h: (b, h, 0, 0)
    return pl.pallas_call(
        functools.partial(_attn_unrolled_body, nq=S // tq, tq=tq),
        grid=(B, H),
        in_specs=[pl.BlockSpec((1, 1, S, QK_PAD_DIM), per_head),
                  pl.BlockSpec((1, 1, S, QK_PAD_DIM), per_head),
                  pl.BlockSpec((1, 1, S, V_HEAD_DIM), per_head)],
        out_specs=pl.BlockSpec((1, S, V_HEAD_DIM), lambda b, h: (b, 0, h)),
        out_shape=jax.ShapeDtypeStruct((B, S, H * V_HEAD_DIM), BF16),
        scratch_shapes=[pltpu.VMEM((tq, tq), F32), pltpu.VMEM((tq, tq), F32),
                        pltpu.VMEM((tq, LANES), F32), pltpu.VMEM((tq, LANES), F32),
                        pltpu.VMEM((tq, LANES), F32), pltpu.VMEM((tq, LANES), F32),
                        pltpu.VMEM((tq, V_HEAD_DIM), F32)],
        compiler_params=pltpu.CompilerParams(dimension_semantics=("arbitrary", "arbitrary"),
                                             vmem_limit_bytes=VMEM_LIMIT_BYTES),
        name="mla_attn",
    )(q, k, v)


def _merge_body(x_ref, ypool_ref, ymla_ref, xq_ref, gates_ref, kmem_ref, vmem_ref,
                wbp_ref, wbm_ref, wbx_ref, wout_ref, ffng_ref, wr_ref, br_ref,
                x1_ref, h2_ref, wts_ref, loc_ref, cnt_tab_ref, carry_tab_ref, counts_ref, carry_ref, *, tm):
    @pl.when(pl.program_id(0) == 0)
    def _():
        carry_ref[...] = jnp.zeros((N_EXPERTS, LANES), F32)

    xq = xq_ref[...]
    parts = []
    for h in range(XATTN_HEADS):
        cols = slice(h * XATTN_HEAD_DIM, (h + 1) * XATTN_HEAD_DIM)
        s = _dot_nt(xq[:, cols], kmem_ref[:, cols]) * (XATTN_HEAD_DIM ** -0.5)
        e = jnp.exp(s - jnp.max(s, axis=1, keepdims=True))
        p = e / jnp.sum(e, axis=1, keepdims=True)
        parts.append(_dot(p.astype(BF16), vmem_ref[:, cols]))
    ymem = jnp.concatenate(parts, axis=1).astype(BF16)

    gates = gates_ref[...].astype(F32)
    merged = (gates[:, 0:D_MODEL] * _dot(ypool_ref[...], wbp_ref[...])
              + gates[:, D_MODEL:2 * D_MODEL] * _dot(ymla_ref[...], wbm_ref[...])
              + gates[:, 2 * D_MODEL:3 * D_MODEL] * _dot(ymem, wbx_ref[...]))
    x1 = x_ref[...] + _dot(merged.astype(BF16), wout_ref[...])
    x1_ref[...] = x1
    h2 = _rms(x1, ffng_ref[...]).astype(BF16)
    h2_ref[...] = h2

    lt = _dot_nt(wr_ref[...], h2) + br_ref[...]
    gl = lt[N_EXPERTS:N_EXPERTS + N_GROUPS, :]
    gmax = jnp.max(gl, axis=0, keepdims=True)
    r4 = lax.broadcasted_iota(jnp.int32, (N_GROUPS, tm), 0).astype(F32)
    gidx = jnp.min(jnp.where(gl == gmax, r4, float(N_GROUPS)), axis=0, keepdims=True)
    pg = 1.0 / jnp.sum(jnp.exp(gl - gmax), axis=0, keepdims=True)
    esel = lt[0:EXPERTS_PER_GROUP, :]
    for g in range(1, N_GROUPS):
        esel = jnp.where(gidx == float(g), lt[g * EXPERTS_PER_GROUP:(g + 1) * EXPERTS_PER_GROUP, :], esel)
    r8 = lax.broadcasted_iota(jnp.int32, (EXPERTS_PER_GROUP, tm), 0).astype(F32)
    m1 = jnp.max(esel, axis=0, keepdims=True)
    i1 = jnp.min(jnp.where(esel == m1, r8, float(EXPERTS_PER_GROUP)), axis=0, keepdims=True)
    rest = jnp.where(r8 == i1, -jnp.inf, esel)
    m2 = jnp.max(rest, axis=0, keepdims=True)
    i2 = jnp.min(jnp.where(rest == m2, r8, float(EXPERTS_PER_GROUP)), axis=0, keepdims=True)
    e2 = jnp.exp(m2 - m1)
    den = 1.0 + e2
    wts_ref[0:1, :] = pg / den
    wts_ref[1:2, :] = pg * e2 / den
    ex1 = gidx * float(EXPERTS_PER_GROUP) + i1
    ex2 = gidx * float(EXPERTS_PER_GROUP) + i2

    r32 = lax.broadcasted_iota(jnp.int32, (N_EXPERTS, tm), 0).astype(F32)
    is1 = r32 == ex1
    is2 = r32 == ex2
    member = jnp.where(is1 | is2, 1.0, 0.0)
    upper = jnp.where(lax.broadcasted_iota(jnp.int32, (tm, tm), 0)
                      <= lax.broadcasted_iota(jnp.int32, (tm, tm), 1), 1.0, 0.0).astype(BF16)
    incl = _dot(member.astype(BF16), upper)
    run = jnp.floor((jnp.sum(member, axis=1, keepdims=True) + (RUN_ALIGN - 1)) / RUN_ALIGN) * RUN_ALIGN
    rcol = lax.broadcasted_iota(jnp.int32, (N_EXPERTS, 1), 0)
    run_start = jnp.zeros((N_EXPERTS, 1), F32)
    for e in range(N_EXPERTS - 1):
        run_start = run_start + jnp.where(rcol > e, run[e:e + 1, :], 0.0)
    pos = incl - 1.0 + run_start
    loc_ref[0:1, :] = jnp.sum(jnp.where(is1, pos, 0.0), axis=0, keepdims=True).astype(jnp.int32)
    loc_ref[1:2, :] = jnp.sum(jnp.where(is2, pos, 0.0), axis=0, keepdims=True).astype(jnp.int32)
    carry = carry_ref[...]
    total = carry + run
    cnt_tab_ref[...] = jnp.broadcast_to(run, (N_EXPERTS, LANES)).astype(jnp.int32)
    carry_tab_ref[...] = carry.astype(jnp.int32)
    carry_ref[...] = total
    counts_ref[...] = total.astype(jnp.int32)


def _merge(x2, ypool, ymla, xq, gates, kmem, vmem, wbp, wbm, wbx, wout, ffng, wr, br, *, B, S, tm, mem_len):
    T = B * S
    tps = S // tm
    row = lambda i: (i, 0)
    lane = lambda i: (0, i)
    memb = lambda i: (i // tps, 0)
    return pl.pallas_call(
        functools.partial(_merge_body, tm=tm),
        grid=(T // tm,),
        in_specs=[
            pl.BlockSpec((tm, D_MODEL), row),
            pl.BlockSpec((tm, POOL_DIM), row),
            pl.BlockSpec((tm, MLA_HEADS * V_HEAD_DIM), row),
            pl.BlockSpec((tm, XATTN_DIM), row),
            pl.BlockSpec((tm, N_BRANCHES * D_MODEL), row),
            pl.BlockSpec((mem_len, XATTN_DIM), memb),
            pl.BlockSpec((mem_len, XATTN_DIM), memb),
            _const_spec((POOL_DIM, D_MODEL)),
            _const_spec((MLA_HEADS * V_HEAD_DIM, D_MODEL)),
            _const_spec((XATTN_DIM, D_MODEL)),
            _const_spec((D_MODEL, D_MODEL)),
            _const_spec((1, D_MODEL)),
            _const_spec((ROUTER_ROWS, D_MODEL)),
            _const_spec((ROUTER_ROWS, 1)),
        ],
        out_specs=[
            pl.BlockSpec((tm, D_MODEL), row),
            pl.BlockSpec((tm, D_MODEL), row),
            pl.BlockSpec((2, tm), lane),
            pl.BlockSpec((2, tm), lane),
            pl.BlockSpec((N_EXPERTS, LANES), row),
            pl.BlockSpec((N_EXPERTS, LANES), row),
            pl.BlockSpec((N_EXPERTS, LANES), lambda i: (0, 0)),
        ],
        out_shape=[
            jax.ShapeDtypeStruct((T, D_MODEL), F32),
            jax.ShapeDtypeStruct((T, D_MODEL), BF16),
            jax.ShapeDtypeStruct((2, T), F32),
            jax.ShapeDtypeStruct((2, T), jnp.int32),
            jax.ShapeDtypeStruct((T // tm * N_EXPERTS, LANES), jnp.int32),
            jax.ShapeDtypeStruct((T // tm * N_EXPERTS, LANES), jnp.int32),
            jax.ShapeDtypeStruct((N_EXPERTS, LANES), jnp.int32),
        ],
        scratch_shapes=[pltpu.VMEM((N_EXPERTS, LANES), F32)],
        compiler_params=pltpu.CompilerParams(dimension_semantics=("arbitrary",),
                                             vmem_limit_bytes=VMEM_LIMIT_BYTES),
        name="merge",
    )(x2, ypool, ymla, xq, gates, kmem, vmem, wbp, wbm, wbx, wout, ffng, wr, br)


def _moe_pos_body(counts_ref, meta_ref):
    shift = int(math.log2(MOE_ROWS))
    cnt = counts_ref[...]
    padded = lax.shift_left(lax.shift_right_logical(cnt + (MOE_ROWS - 1), shift), shift)
    r32 = lax.broadcasted_iota(jnp.int32, (N_EXPERTS, LANES), 0)
    pad_start = jnp.zeros((N_EXPERTS, LANES), jnp.int32)
    for e in range(N_EXPERTS - 1):
        pad_start = pad_start + jnp.where(r32 > e, padded[e:e + 1, :], 0)
    pad_end = pad_start + padded

    lane = lax.broadcasted_iota(jnp.int32, (1, META_LANES), 1)
    block_row = lane * MOE_ROWS
    blk_e = jnp.zeros((1, META_LANES), jnp.int32)
    pe_row = jnp.zeros((1, META_LANES), jnp.int32)
    for e in range(N_EXPERTS):
        pe = pad_end[e:e + 1, 0:1]
        blk_e = blk_e + jnp.where(pe <= block_row, 1, 0)
        pe_row = pe_row + jnp.where(lane == META_PAD_END + e, pe, 0)
    blk_e = jnp.minimum(blk_e, N_EXPERTS - 1)
    nact = lax.shift_right_logical(pad_end[N_EXPERTS - 1:N_EXPERTS, 0:1], shift)
    meta = jnp.where(lane < META_PAD_END, blk_e, pe_row)
    meta_ref[...] = jnp.where(lane == META_NACT, nact, meta)


def _moe_pos(counts):
    full = lambda shape: pl.BlockSpec(shape, lambda i: (0,) * len(shape))
    return pl.pallas_call(
        _moe_pos_body,
        grid=(1,),
        in_specs=[full((N_EXPERTS, LANES))],
        out_specs=full((1, META_LANES)),
        out_shape=jax.ShapeDtypeStruct((1, META_LANES), jnp.int32),
        compiler_params=pltpu.CompilerParams(dimension_semantics=("arbitrary",)),
        name="moe_pos",
    )(counts)


def _pack_bf16_pairs(x):
    lo = pltpu.bitcast(x[:, 0:PACKED_DIM], U32)
    hi = pltpu.bitcast(x[:, PACKED_DIM:D_MODEL], U32)
    return hi | lax.shift_right_logical(lo, jnp.uint32(16))


def _unpack_bf16_pairs(w):
    lo = pltpu.bitcast(lax.shift_left(w, jnp.uint32(16)), F32)
    hi = pltpu.bitcast(w & jnp.uint32(0xFFFF0000), F32)
    return jnp.concatenate([lo, hi], axis=1)


def _loc_rows(tm):
    bf16_rows = 2 * RUN_ALIGN
    return pl.cdiv(2 * tm + N_EXPERTS * (RUN_ALIGN - 1), bf16_rows) * bf16_rows


def _run_copies(tile, cnt_ref, carry_ref, meta_ref, make_copy):
    def per_expert(e, local):
        n = pl.multiple_of(cnt_ref[tile * N_EXPERTS + e], RUN_ALIGN)
        start = jnp.where(e == 0, 0, meta_ref[META_PAD_END + jnp.maximum(e - 1, 0)])
        glob = pl.multiple_of(start + carry_ref[tile * N_EXPERTS + e], RUN_ALIGN)

        @pl.when(n > 0)
        def _():
            make_copy(pl.multiple_of(local, RUN_ALIGN), glob, n).start()

        return local + n

    return pl.multiple_of(lax.fori_loop(0, N_EXPERTS, per_expert, 0), RUN_ALIGN)


def _tile_rows(tile, cnt_ref):
    total = lax.fori_loop(0, N_EXPERTS, lambda e, t: t + cnt_ref[tile * N_EXPERTS + e], 0)
    return pl.multiple_of(total, RUN_ALIGN)


def _dispatch_body(meta_ref, cnt_ref, carry_ref, loc_ref, h2_ref, xs_ref, xloc_ref, zero_ref, sems, zsem,
                   *, tm, n_blocks):
    step = pl.program_id(0)
    last_step = pl.num_programs(0) - 1
    last_tile = TILES_PER_STEP * pl.num_programs(0) - 1
    nact = meta_ref[META_NACT]

    def wait_rows(t, s):
        n = _tile_rows(t, cnt_ref)
        pltpu.make_async_copy(xloc_ref.at[s, pl.ds(0, n)], xs_ref.at[pl.ds(0, n)], sems.at[s]).wait()

    def pad_copy(e):
        end = pl.multiple_of(meta_ref[META_PAD_END + e], MOE_ROWS)
        start = jnp.where(e == 0, 0, meta_ref[META_PAD_END + jnp.maximum(e - 1, 0)])
        used = carry_ref[last_tile * N_EXPERTS + e] + cnt_ref[last_tile * N_EXPERTS + e]
        first = pl.multiple_of(start + used, RUN_ALIGN)
        n = pl.multiple_of(end - first, RUN_ALIGN)
        return n, pltpu.make_async_copy(zero_ref.at[pl.ds(0, n)], xs_ref.at[pl.ds(first, n)], zsem)

    def tail_copy(b):
        return pltpu.make_async_copy(
            zero_ref, xs_ref.at[pl.ds(pl.multiple_of(b * MOE_ROWS, MOE_ROWS), MOE_ROWS)], zsem)

    def fill(op):
        def pad(e, c):
            n, cp = pad_copy(e)

            @pl.when(n > 0)
            def _():
                op(cp)
            return c

        def tail(b, c):
            op(tail_copy(b))
            return c

        lax.fori_loop(0, N_EXPERTS, pad, 0)
        lax.fori_loop(nact, n_blocks, tail, 0)

    @pl.when(step == 0)
    def _():
        zero_ref[...] = jnp.zeros((MOE_ROWS, PACKED_DIM), U32)
        fill(lambda cp: cp.start())

    r = lax.broadcasted_iota(jnp.int32, (_loc_rows(tm), tm), 0)
    for sub in range(TILES_PER_STEP):
        tile = step * TILES_PER_STEP + sub
        cols = slice(sub * tm, (sub + 1) * tm)

        @pl.when(step >= 1)
        def _(tile=tile, sub=sub):
            wait_rows(tile - TILES_PER_STEP, sub)

        onehot = jnp.where((r == loc_ref[0:1, cols]) | (r == loc_ref[1:2, cols]), 1.0, 0.0).astype(BF16)
        xloc_ref[sub] = _pack_bf16_pairs(_dot(onehot, h2_ref[cols, :]))

        def make_copy(local, glob, n, sub=sub):
            return pltpu.make_async_copy(xloc_ref.at[sub, pl.ds(local, n)], xs_ref.at[pl.ds(glob, n)],
                                         sems.at[sub])

        _run_copies(tile, cnt_ref, carry_ref, meta_ref, make_copy)

    @pl.when(step == last_step)
    def _():
        for sub in range(TILES_PER_STEP):
            wait_rows(step * TILES_PER_STEP + sub, sub)
        fill(lambda cp: cp.wait())


def _dispatch(meta1, cnt_tab, carry_tab, loc, h2, *, R, tm):
    T = h2.shape[0]
    ts = TILES_PER_STEP * tm
    assert T % ts == 0
    return pl.pallas_call(
        functools.partial(_dispatch_body, tm=tm, n_blocks=R // MOE_ROWS),
        grid_spec=pltpu.PrefetchScalarGridSpec(
            num_scalar_prefetch=3,
            grid=(T // ts,),
            in_specs=[pl.BlockSpec((2, ts), lambda i, *_: (0, i)),
                      pl.BlockSpec((ts, D_MODEL), lambda i, *_: (i, 0))],
            out_specs=pl.BlockSpec(memory_space=pl.ANY),
            scratch_shapes=[pltpu.VMEM((TILES_PER_STEP, _loc_rows(tm), PACKED_DIM), U32),
                            pltpu.VMEM((MOE_ROWS, PACKED_DIM), U32),
                            pltpu.SemaphoreType.DMA((2,)), pltpu.SemaphoreType.DMA],
        ),
        out_shape=jax.ShapeDtypeStruct((R, PACKED_DIM), U32),
        compiler_params=pltpu.CompilerParams(dimension_semantics=("arbitrary",),
                                             vmem_limit_bytes=VMEM_LIMIT_BYTES),
        name="dispatch",
    )(meta1, cnt_tab, carry_tab, loc, h2)


def _moe_ffn_body(meta_ref, xs_ref, wg_hbm, wu_hbm, wd_hbm, ys_ref,
                  xbuf, ybuf, wg_stage, wu_stage, wd_stage, wg_b, wu_b, wd_b, zero_ref,
                  xsem, ysem, wsem, zsem, *, n_blocks):
    nact = meta_ref[META_NACT]
    shift = int(math.log2(MOE_ROWS))

    def rows_of(b):
        return pl.ds(pl.multiple_of(b * MOE_ROWS, MOE_ROWS), MOE_ROWS)

    def x_copy(b, s):
        return pltpu.make_async_copy(xs_ref.at[rows_of(b)], xbuf.at[s], xsem.at[s])

    def y_copy(b, s):
        return pltpu.make_async_copy(ybuf.at[s], ys_ref.at[rows_of(b)], ysem.at[s])

    def w_copies(e, s):
        return (pltpu.make_async_copy(wg_hbm.at[e], wg_stage.at[s], wsem.at[s]),
                pltpu.make_async_copy(wu_hbm.at[e], wu_stage.at[s], wsem.at[s]),
                pltpu.make_async_copy(wd_hbm.at[e], wd_stage.at[s], wsem.at[s]))

    def tail_copy(b):
        return pltpu.make_async_copy(zero_ref, ys_ref.at[rows_of(b)], zsem)

    zero_ref[...] = jnp.zeros((MOE_ROWS, PACKED_DIM), U32)
    lax.fori_loop(nact, n_blocks, lambda b, c: (tail_copy(b).start(), c)[1], 0)

    def next_expert_block(e):
        return lax.shift_right_logical(meta_ref[META_PAD_END + e], shift)

    def start_weights(b, s):
        @pl.when(b < nact)
        def _():
            for cp in w_copies(meta_ref[jnp.minimum(b, n_blocks - 1)], s):
                cp.start()

    x_copy(0, 0).start()

    @pl.when(nact > 1)
    def _():
        x_copy(1, 1).start()

    e_first = meta_ref[0]
    start_weights(0, 0)
    start_weights(next_expert_block(e_first), 1)

    def block(b, xs, ys, k_prev):
        valid = b < nact
        e = meta_ref[jnp.minimum(b, nact - 1)]
        changed = jnp.logical_and(valid, jnp.logical_or(b == 0, e != meta_ref[jnp.maximum(b - 1, 0)]))
        k = jnp.where(changed, k_prev + 1, k_prev)

        @pl.when(changed)
        def _():
            ws = lax.rem(k, W_STAGES)
            for cp in w_copies(e, ws):
                cp.wait()
            wg_b[...] = wg_stage[ws].astype(BF16)
            wu_b[...] = wu_stage[ws].astype(BF16)
            wd_b[...] = wd_stage[ws].astype(BF16)
            n1 = next_expert_block(e)
            e1 = meta_ref[jnp.minimum(n1, n_blocks - 1)]
            n2 = jnp.where(n1 < nact, next_expert_block(e1), n_blocks)
            start_weights(n2, lax.rem(k + 2, W_STAGES))

        @pl.when(valid)
        def _():
            x_copy(b, xs).wait()

            @pl.when(b + 2 < nact)
            def _():
                x_copy(b + 2, (xs + 2) % X_SLOTS).start()

            @pl.when(b >= 2)
            def _():
                y_copy(b - 2, ys).wait()

            x = _unpack_bf16_pairs(xbuf[xs]).astype(BF16)
            g = _dot(x, wg_b[...])
            a = (g * jax.nn.sigmoid(g)) * _dot(x, wu_b[...])
            y = _dot(a.astype(BF16), wd_b[...])
            ybuf[ys] = _pack_bf16_pairs(y.astype(BF16).astype(F32))
            y_copy(b, ys).start()

        return k

    def quad(i, k):
        for j in range(X_SLOTS):
            k = block(X_SLOTS * i + j, j, j % 2, k)
        return k

    lax.fori_loop(0, lax.div(nact + (X_SLOTS - 1), X_SLOTS), quad, -1)

    @pl.when(nact >= 2)
    def _():
        y_copy(nact - 2, lax.rem(nact, 2)).wait()

    y_copy(nact - 1, lax.rem(nact - 1, 2)).wait()
    lax.fori_loop(nact, n_blocks, lambda b, c: (tail_copy(b).wait(), c)[1], 0)


def _moe_ffn(meta1, xs, wg, wu, wd):
    R = xs.shape[0]
    hbm = pl.BlockSpec(memory_space=pl.ANY)
    return pl.pallas_call(
        functools.partial(_moe_ffn_body, n_blocks=R // MOE_ROWS),
        grid_spec=pltpu.PrefetchScalarGridSpec(
            num_scalar_prefetch=1,
            grid=(1,),
            in_specs=[hbm, hbm, hbm, hbm],
            out_specs=hbm,
            scratch_shapes=[pltpu.VMEM((X_SLOTS, MOE_ROWS, PACKED_DIM), U32),
                            pltpu.VMEM((2, MOE_ROWS, PACKED_DIM), U32),
                            pltpu.VMEM((W_STAGES, D_MODEL, D_EXPERT), F32),
                            pltpu.VMEM((W_STAGES, D_MODEL, D_EXPERT), F32),
                            pltpu.VMEM((W_STAGES, D_EXPERT, D_MODEL), F32),
                            pltpu.VMEM((D_MODEL, D_EXPERT), BF16),
                            pltpu.VMEM((D_MODEL, D_EXPERT), BF16),
                            pltpu.VMEM((D_EXPERT, D_MODEL), BF16),
                            pltpu.VMEM((MOE_ROWS, PACKED_DIM), U32),
                            pltpu.SemaphoreType.DMA((X_SLOTS,)), pltpu.SemaphoreType.DMA((2,)),
                            pltpu.SemaphoreType.DMA((W_STAGES,)), pltpu.SemaphoreType.DMA],
        ),
        out_shape=jax.ShapeDtypeStruct((R, PACKED_DIM), U32),
        compiler_params=pltpu.CompilerParams(dimension_semantics=("arbitrary",),
                                             vmem_limit_bytes=VMEM_LIMIT_BYTES),
        name="moe_ffn",
    )(meta1, xs, wg, wu, wd)


def _combine_body(meta_ref, cnt_ref, carry_ref, x1_ref, loc_ref, wts_ref, fg_ref, ys_ref, out_ref,
                  yloc_ref, sems, *, tm):
    step = pl.program_id(0)

    def fetch(t, s):
        def make_copy(local, glob, n):
            return pltpu.make_async_copy(ys_ref.at[pl.ds(glob, n)], yloc_ref.at[s, pl.ds(local, n)],
                                         sems.at[s])
        _run_copies(t, cnt_ref, carry_ref, meta_ref, make_copy)

    @pl.when(step == 0)
    def _():
        yloc_ref[...] = jnp.zeros(yloc_ref.shape, U32)
        for sub in range(TILES_PER_STEP):
            fetch(sub, sub)

    r = lax.broadcasted_iota(jnp.int32, (_loc_rows(tm), tm), 0)
    for sub in range(TILES_PER_STEP):
        tile = step * TILES_PER_STEP + sub
        cols = slice(sub * tm, (sub + 1) * tm)
        n = _tile_rows(tile, cnt_ref)
        pltpu.make_async_copy(ys_ref.at[pl.ds(0, n)], yloc_ref.at[sub, pl.ds(0, n)], sems.at[sub]).wait()
        is0 = r == loc_ref[0:1, cols]
        is1 = r == loc_ref[1:2, cols]
        row_w = jnp.sum(jnp.where(is0, wts_ref[0:1, cols], 0.0) + jnp.where(is1, wts_ref[1:2, cols], 0.0),
                        axis=1, keepdims=True)
        yw = (row_w * _unpack_bf16_pairs(yloc_ref[sub])).astype(BF16)

        @pl.when(step + 1 < pl.num_programs(0))
        def _(tile=tile, sub=sub):
            fetch(tile + TILES_PER_STEP, sub)

        twohot = jnp.where(is0 | is1, 1.0, 0.0).astype(BF16)
        moe = lax.dot_general(twohot, yw, (((0,), (0,)), ((), ())), preferred_element_type=F32)
        out_ref[cols, :] = _rms(x1_ref[cols, :] + moe, fg_ref[...])


def _combine(meta1, cnt_tab, carry_tab, x1, loc, wts, fg, ys, *, tm):
    T = x1.shape[0]
    ts = TILES_PER_STEP * tm
    assert T % ts == 0
    row = lambda i, *_: (i, 0)
    lane = lambda i, *_: (0, i)
    return pl.pallas_call(
        functools.partial(_combine_body, tm=tm),
        grid_spec=pltpu.PrefetchScalarGridSpec(
            num_scalar_prefetch=3,
            grid=(T // ts,),
            in_specs=[pl.BlockSpec((ts, D_MODEL), row),
                      pl.BlockSpec((2, ts), lane),
                      pl.BlockSpec((2, ts), lane),
                      pl.BlockSpec((1, D_MODEL), lambda i, *_: (0, 0)),
                      pl.BlockSpec(memory_space=pl.ANY)],
            out_specs=pl.BlockSpec((ts, D_MODEL), row),
            scratch_shapes=[pltpu.VMEM((TILES_PER_STEP, _loc_rows(tm), PACKED_DIM), U32),
                            pltpu.SemaphoreType.DMA((TILES_PER_STEP,))],
        ),
        out_shape=jax.ShapeDtypeStruct((T, D_MODEL), F32),
        compiler_params=pltpu.CompilerParams(dimension_semantics=("arbitrary",),
                                             vmem_limit_bytes=VMEM_LIMIT_BYTES),
        name="combine",
    )(meta1, cnt_tab, carry_tab, x1, loc, wts, fg, ys)


def _tile(n, t):
    t = min(n, t)
    assert n % t == 0, (n, t)
    return t


def kernel(x, mem, positions, mix_norm_g, w_in, gate_b, q_norm_g, w_uq, kv_norm_g, w_uk, w_uv, pool_w, pool_scale, mem_norm_g, w_mem_kv, w_br_pool, w_br_mla, w_br_mem, w_out, ffn_norm_g, w_router_group, b_router_group, w_router_expert, b_router_expert, w_gate_e, w_up_e, w_down_e, final_norm_g):
    B, S, D = x.shape
    assert D == D_MODEL and mix_norm_g.shape[0] == 1
    T = B * S
    mem_len = mem.shape[1]
    tm = _tile(S, 512)
    l = 0

    win_p = _pack_w_in(jnp.swapaxes(w_in, 1, 2).reshape(W_IN_END, D_MODEL))
    wuq_p = jnp.pad(w_uq[l].reshape(Q_LORA_RANK, MLA_HEADS, QK_NOPE_DIM + QK_ROPE_DIM),
                    ((0, 0), (0, 0), (0, QK_PAD_DIM - QK_NOPE_DIM - QK_ROPE_DIM))
                    ).reshape(Q_LORA_RANK, MLA_HEADS * QK_PAD_DIM).astype(BF16)
    inv_freq = 1.0 / (ROPE_THETA ** (jnp.arange(0, QK_ROPE_DIM, 2, dtype=F32) / QK_ROPE_DIM))
    invf = jnp.concatenate([inv_freq, inv_freq, jnp.zeros((LANES - QK_ROPE_DIM,), F32)])[None, :]
    wr = jnp.concatenate([w_router_expert[l], w_router_group[l],
                          jnp.zeros((D_MODEL, ROUTER_ROWS - N_EXPERTS - N_GROUPS), F32)], axis=1).T.astype(BF16)
    br = jnp.concatenate([b_router_expert[l], b_router_group[l],
                          jnp.zeros((ROUTER_ROWS - N_EXPERTS - N_GROUPS,), F32)])[:, None].astype(F32)
    x2 = x.reshape(T, D_MODEL)
    pos2 = positions.reshape(T, 1)

    ypool, xq, gates, q, k, v = _mixer_in(
        x2, pos2, invf, mix_norm_g[l][None, :], win_p, gate_b[l], q_norm_g[l][None, :], wuq_p,
        kv_norm_g[l][None, :], w_uk[l].astype(BF16), w_uv[l].astype(BF16), pool_w[l].astype(BF16),
        pool_scale[l][None, :], B=B, S=S, tm=tm)
    kmem, vmem = _mem_kv(mem.reshape(B * mem_len, D_MODEL), mem_norm_g[l][None, :], w_mem_kv[l].astype(BF16))
    ymla = _mla_attn_unrolled(q, k, v, tq=tm).reshape(T, MLA_HEADS * V_HEAD_DIM)
    x1, h2, wts, loc, cnt_tab, carry_tab, counts = _merge(
        x2, ypool, ymla, xq, gates, kmem, vmem, w_br_pool[l].astype(BF16), w_br_mla[l].astype(BF16),
        w_br_mem[l].astype(BF16), w_out[l].astype(BF16), ffn_norm_g[l][None, :], wr, br,
        B=B, S=S, tm=tm, mem_len=mem_len)

    R = 2 * T + (T // tm) * N_EXPERTS * RUN_ALIGN + N_EXPERTS * MOE_ROWS
    assert R % MOE_ROWS == 0 and R // MOE_ROWS <= META_PAD_END
    meta1 = _moe_pos(counts).reshape(META_LANES)
    cnt1 = cnt_tab[:, 0]
    carry1 = carry_tab[:, 0]
    xs = _dispatch(meta1, cnt1, carry1, loc, h2, R=R, tm=tm)
    ys = _moe_ffn(meta1, xs, w_gate_e[l], w_up_e[l], w_down_e[l])
    out = _combine(meta1, cnt1, carry1, x1, loc, wts, final_norm_g[None, :], ys, tm=tm)
    return out.reshape(B, S, D_MODEL)
```

```python
import functools
import math

import jax
import jax.numpy as jnp
from jax import lax
from jax.experimental import pallas as pl
from jax.experimental.pallas import tpu as pltpu

D_MODEL = 1024
POOL_WINDOWS = (2, 4, 8, 16)
POOL_GROUP_DIM = 128
POOL_DIM = 512
MLA_HEADS = 8
QK_NOPE_DIM = 128
QK_ROPE_DIM = 64
V_HEAD_DIM = 128
Q_LORA_RANK = 384
KV_LORA_RANK = 256
ROPE_THETA = 10000.0
XATTN_HEADS = 4
XATTN_HEAD_DIM = 128
XATTN_DIM = 512
N_BRANCHES = 3
N_GROUPS = 4
EXPERTS_PER_GROUP = 8
N_EXPERTS = 32
D_EXPERT = 256
RMS_EPS = 1e-6
NEG_INF = -1e30

LANES = 128
QK_PAD_DIM = 2 * LANES
POOL_HALO = 16
MOE_ROWS = 512
X_SLOTS = 4
W_STAGES = 4
RUN_ALIGN = 8
PACKED_DIM = D_MODEL // 2
ROUTER_ROWS = 40
META_LANES = 256
META_PAD_END = 192
META_NACT = 255
VMEM_LIMIT_BYTES = 56 * 1024 * 1024

IN_POOL, IN_QD, IN_KV, IN_XQ, IN_GATE, IN_KR, IN_END = 0, 512, 896, 1152, 1664, 4736, 4864
W_IN_KR, W_IN_XQ, W_IN_END = 1152, 1216, 4800

F32 = jnp.float32
BF16 = jnp.bfloat16
U32 = jnp.uint32


def _rms(x, g):
    ms = jnp.mean(x * x, axis=-1, keepdims=True)
    return (x * lax.rsqrt(ms + RMS_EPS)) * g


def _dot(a, b):
    return jnp.dot(a, b, preferred_element_type=F32)


def _dot_nt(a, b):
    return lax.dot_general(a, b, (((1,), (1,)), ((), ())), preferred_element_type=F32)


def _const_spec(shape):
    nd = len(shape)
    return pl.BlockSpec(shape, lambda *_: (0,) * nd, pipeline_mode=pl.Buffered(1))


def _pack_w_in_body(wt_ref, o_ref):
    chunk = LANES

    def copy_cols(dst, src, n):
        for r in range(0, n, chunk):
            m = min(chunk, n - r)
            o_ref[:, dst + r:dst + r + m] = wt_ref[src + r:src + r + m, :].T.astype(BF16)

    copy_cols(0, 0, W_IN_KR)
    copy_cols(IN_XQ, W_IN_XQ, W_IN_END - W_IN_XQ)
    copy_cols(IN_KR, W_IN_KR, W_IN_XQ - W_IN_KR)
    o_ref[:, IN_KR + QK_ROPE_DIM:IN_END] = jnp.zeros((D_MODEL, LANES - QK_ROPE_DIM), BF16)


def _pack_w_in(w_t):
    whole = pl.BlockSpec(memory_space=pltpu.VMEM)
    return pl.pallas_call(
        _pack_w_in_body,
        in_specs=[whole],
        out_specs=whole,
        out_shape=jax.ShapeDtypeStruct((D_MODEL, IN_END), BF16),
        compiler_params=pltpu.CompilerParams(vmem_limit_bytes=VMEM_LIMIT_BYTES),
        name="pack_w_in",
    )(w_t)


def _mixer_in_body(x_ref, pos_ref, invf_ref, mixg_ref, win_ref, gateb_ref, qg_ref, wuq_ref,
                   kvg_ref, wuk_ref, wuv_ref, poolw_ref, pools_ref,
                   ypool_ref, xq_ref, gates_ref, q_ref, k_ref, v_ref, ext_ref,
                   *, tm, tiles_per_seq, q_scale):
    si = lax.rem(pl.program_id(0), tiles_per_seq)

    @pl.when(pl.program_id(0) == 0)
    def _():
        ext_ref[0:POOL_HALO, :] = jnp.zeros((POOL_HALO, POOL_DIM), F32)

    hb = _rms(x_ref[...], mixg_ref[...]).astype(BF16)

    u = _dot(hb, win_ref[:, IN_POOL:IN_QD])
    ext_ref[0:POOL_HALO, :] = jnp.where(si == 0, 0.0, ext_ref[0:POOL_HALO, :])
    ext_ref[POOL_HALO:POOL_HALO + tm, :] = u

    for c in range(N_BRANCHES):
        gl = _dot(hb, win_ref[:, IN_GATE + c * D_MODEL:IN_GATE + (c + 1) * D_MODEL])
        gates_ref[:, c * D_MODEL:(c + 1) * D_MODEL] = jax.nn.sigmoid(gl + gateb_ref[c:c + 1, :]).astype(BF16)

    ang = pos_ref[...].astype(F32) * invf_ref[...]
    cos = jnp.cos(ang)
    sin = jnp.sin(ang)
    first_half = lax.broadcasted_iota(jnp.int32, (tm, LANES), 1) < (QK_ROPE_DIM // 2)
    sin_signed = jnp.where(first_half, -sin, sin)

    def rope(r):
        swapped = jnp.where(first_half, pltpu.roll(r, LANES - QK_ROPE_DIM // 2, 1),
                            pltpu.roll(r, QK_ROPE_DIM // 2, 1))
        return r * cos + swapped * sin_signed

    cq = _rms(_dot(hb, win_ref[:, IN_QD:IN_KV]), qg_ref[...]).astype(BF16)
    for h in range(MLA_HEADS):
        qh = _dot(cq, wuq_ref[:, h * QK_PAD_DIM:(h + 1) * QK_PAD_DIM])
        q_ref[0, h, :, 0:LANES] = (qh[:, 0:LANES] * q_scale).astype(BF16)
        q_ref[0, h, :, LANES:QK_PAD_DIM] = (rope(qh[:, LANES:QK_PAD_DIM]) * q_scale).astype(BF16)

    ckv = _rms(_dot(hb, win_ref[:, IN_KV:IN_XQ]), kvg_ref[...]).astype(BF16)
    kr = rope(_dot(hb, win_ref[:, IN_KR:IN_END])).astype(BF16)
    for hp in range(MLA_HEADS // 2):
        cols = slice(hp * 2 * LANES, (hp + 1) * 2 * LANES)
        kn = _dot(ckv, wuk_ref[:, cols]).astype(BF16)
        vv = _dot(ckv, wuv_ref[:, cols]).astype(BF16)
        for j in range(2):
            h = 2 * hp + j
            k_ref[0, h, :, 0:LANES] = kn[:, j * LANES:(j + 1) * LANES]
            k_ref[0, h, :, LANES:QK_PAD_DIM] = kr
            v_ref[0, h] = vv[:, j * LANES:(j + 1) * LANES]

    t_seq = lax.broadcasted_iota(jnp.int32, (tm, 1), 0) + si * tm
    for g, w in enumerate(POOL_WINDOWS):
        lo = g * POOL_GROUP_DIM
        hi = lo + POOL_GROUP_DIM
        acc = u[:, lo:hi]
        for j in range(1, w):
            acc = acc + ext_ref[POOL_HALO - j:POOL_HALO - j + tm, lo:hi]
        cnt = jnp.minimum(t_seq + 1, w).astype(F32)
        p = acc / cnt - u[:, lo:hi]
        y = _dot(p.astype(BF16), poolw_ref[g]) * pools_ref[:, lo:hi]
        ypool_ref[:, lo:hi] = y.astype(BF16)
    ext_ref[0:POOL_HALO, :] = ext_ref[tm:tm + POOL_HALO, :]

    xq_ref[...] = _dot(hb, win_ref[:, IN_XQ:IN_GATE]).astype(BF16)


def _mixer_in(x2, pos2, invf, mixg, win_p, gate_b, qg, wuq_p, kvg, wuk, wuv, pool_w, pool_s, *, B, S, tm):
    T = B * S
    tps = S // tm
    q_scale = (QK_NOPE_DIM + QK_ROPE_DIM) ** -0.5 * math.log2(math.e)
    body = functools.partial(_mixer_in_body, tm=tm, tiles_per_seq=tps, q_scale=q_scale)
    row = lambda i: (i, 0)
    head = lambda i: (i // tps, 0, i % tps, 0)
    return pl.pallas_call(
        body,
        grid=(T // tm,),
        in_specs=[
            pl.BlockSpec((tm, D_MODEL), row),
            pl.BlockSpec((tm, 1), row),
            _const_spec((1, LANES)),
            _const_spec((1, D_MODEL)),
            _const_spec((D_MODEL, IN_END)),
            _const_spec((N_BRANCHES, D_MODEL)),
            _const_spec((1, Q_LORA_RANK)),
            _const_spec((Q_LORA_RANK, MLA_HEADS * QK_PAD_DIM)),
            _const_spec((1, KV_LORA_RANK)),
            _const_spec((KV_LORA_RANK, MLA_HEADS * QK_NOPE_DIM)),
            _const_spec((KV_LORA_RANK, MLA_HEADS * V_HEAD_DIM)),
            _const_spec((len(POOL_WINDOWS), POOL_GROUP_DIM, POOL_GROUP_DIM)),
            _const_spec((1, POOL_DIM)),
        ],
        out_specs=[
            pl.BlockSpec((tm, POOL_DIM), row),
            pl.BlockSpec((tm, XATTN_DIM), row),
            pl.BlockSpec((tm, N_BRANCHES * D_MODEL), row),
            pl.BlockSpec((1, MLA_HEADS, tm, QK_PAD_DIM), head),
            pl.BlockSpec((1, MLA_HEADS, tm, QK_PAD_DIM), head),
            pl.BlockSpec((1, MLA_HEADS, tm, V_HEAD_DIM), head),
        ],
        out_shape=[
            jax.ShapeDtypeStruct((T, POOL_DIM), BF16),
            jax.ShapeDtypeStruct((T, XATTN_DIM), BF16),
            jax.ShapeDtypeStruct((T, N_BRANCHES * D_MODEL), BF16),
            jax.ShapeDtypeStruct((B, MLA_HEADS, S, QK_PAD_DIM), BF16),
            jax.ShapeDtypeStruct((B, MLA_HEADS, S, QK_PAD_DIM), BF16),
            jax.ShapeDtypeStruct((B, MLA_HEADS, S, V_HEAD_DIM), BF16),
        ],
        scratch_shapes=[pltpu.VMEM((tm + POOL_HALO, POOL_DIM), F32)],
        compiler_params=pltpu.CompilerParams(dimension_semantics=("arbitrary",),
                                             vmem_limit_bytes=VMEM_LIMIT_BYTES),
        name="mixer_in",
    )(x2, pos2, invf, mixg, win_p, gate_b, qg, wuq_p, kvg, wuk, wuv, pool_w, pool_s)


def _mem_kv_body(mem_ref, g_ref, w_ref, k_ref, v_ref):
    mb = _rms(mem_ref[...], g_ref[...]).astype(BF16)
    kv = _dot(mb, w_ref[...])
    k_ref[...] = kv[:, 0:XATTN_DIM].astype(BF16)
    v_ref[...] = kv[:, XATTN_DIM:2 * XATTN_DIM].astype(BF16)


def _mem_kv(mem2, g, w):
    rows = mem2.shape[0]
    tr = min(rows, 512)
    return pl.pallas_call(
        _mem_kv_body,
        grid=(rows // tr,),
        in_specs=[pl.BlockSpec((tr, D_MODEL), lambda i: (i, 0)),
                  _const_spec((1, D_MODEL)),
                  _const_spec((D_MODEL, 2 * XATTN_DIM))],
        out_specs=[pl.BlockSpec((tr, XATTN_DIM), lambda i: (i, 0)),
                   pl.BlockSpec((tr, XATTN_DIM), lambda i: (i, 0))],
        out_shape=[jax.ShapeDtypeStruct((rows, XATTN_DIM), BF16),
                   jax.ShapeDtypeStruct((rows, XATTN_DIM), BF16)],
        compiler_params=pltpu.CompilerParams(dimension_semantics=("arbitrary",)),
        name="mem_kv",
    )(mem2, g, w)


def _attn_unrolled_body(q_ref, k_ref, v_ref, o_ref, s_a, s_b, mc_a, mc_b, m_ref, l_ref, acc_ref, *, nq, tq):
    s_bufs = (s_a, s_b)
    mc_bufs = (mc_a, mc_b)
    mxu_row_split = 2
    blocks = [(qi, kb) for qi in range(nq) for kb in range(qi + 1)]

    def scores(i, slot):
        qi, kb = blocks[i]
        s = _dot_nt(q_ref[0, 0, qi * tq:(qi + 1) * tq, :], k_ref[0, 0, kb * tq:(kb + 1) * tq, :])
        if qi == kb:
            ri = lax.broadcasted_iota(jnp.int32, (tq, tq), 0)
            ci = lax.broadcasted_iota(jnp.int32, (tq, tq), 1)
            s = jnp.where(ci <= ri, s, NEG_INF)
        s_bufs[slot][...] = s
        mc_bufs[slot][...] = jnp.broadcast_to(jnp.max(s, axis=1, keepdims=True), (tq, LANES))

    def accumulate(i, slot):
        qi, kb = blocks[i]
        is_first = kb == 0
        is_last = kb == qi
        if is_first:
            m_new = mc_bufs[slot][...]
        else:
            m_prev = m_ref[...]
            m_new = jnp.maximum(m_prev, mc_bufs[slot][...])
            alpha = jnp.exp2(m_prev - m_new)
        p = jnp.exp2(s_bufs[slot][...] - jnp.concatenate([m_new] * (tq // LANES), axis=1))
        psum = p[:, 0:LANES]
        for c in range(1, tq // LANES):
            psum = psum + p[:, c * LANES:(c + 1) * LANES]
        l_new = psum if is_first else alpha * l_ref[...] + psum
        pb = p.astype(BF16)
        v = v_ref[0, 0, kb * tq:(kb + 1) * tq, :]
        if is_last:
            inv = 1.0 / jnp.sum(l_new, axis=1, keepdims=True)
        else:
            l_ref[...] = l_new
            m_ref[...] = m_new
        h = tq // mxu_row_split
        for r in range(mxu_row_split):
            rows = slice(r * h, (r + 1) * h)
            acc = _dot(pb[rows, :], v)
            if not is_first:
                acc = alpha[rows, :] * acc_ref[rows, :] + acc
            if is_last:
                o_ref[0, qi * tq + r * h:qi * tq + (r + 1) * h, :] = (acc * inv[rows, :]).astype(BF16)
            else:
                acc_ref[rows, :] = acc

    scores(0, 0)
    for i in range(len(blocks)):
        if i + 1 < len(blocks):
            scores(i + 1, (i + 1) % 2)
        accumulate(i, i % 2)


def _mla_attn_unrolled(q, k, v, *, tq):
    B, H, S, _ = q.shape
    per_head = lambda b, h: (b, h, 0, 0)
    return pl.pallas_call(
        functools.partial(_attn_unrolled_body, nq=S // tq, tq=tq),
        grid=(B, H),
        in_specs=[pl.BlockSpec((1, 1, S, QK_PAD_DIM), per_head),
                  pl.BlockSpec((1, 1, S, QK_PAD_DIM), per_head),
                  pl.BlockSpec((1, 1, S, V_HEAD_DIM), per_head)],
        out_specs=pl.BlockSpec((1, S, V_HEAD_DIM), lambda b, h: (b, 0, h)),
        out_shape=jax.ShapeDtypeStruct((B, S, H * V_HEAD_DIM), BF16),
        scratch_shapes=[pltpu.VMEM((tq, tq), F32), pltpu.VMEM((tq, tq), F32),
                        pltpu.VMEM((tq, LANES), F32), pltpu.VMEM((tq, LANES), F32),
                        pltpu.VMEM((tq, LANES), F32), pltpu.VMEM((tq, LANES), F32),
                        pltpu.VMEM((tq, V_HEAD_DIM), F32)],
        compiler_params=pltpu.CompilerParams(dimension_semantics=("arbitrary", "arbitrary"),
                                             vmem_limit_bytes=VMEM_LIMIT_BYTES),
        name="mla_attn",
    )(q, k, v)


def _merge_body(x_ref, ypool_ref, ymla_ref, xq_ref, gates_ref, kmem_ref, vmem_ref,
                wbp_ref, wbm_ref, wbx_ref, wout_ref, ffng_ref, wr_ref, br_ref,
                x1_ref, h2_ref, wts_ref, loc_ref, cnt_tab_ref, carry_tab_ref, counts_ref, carry_ref, *, tm):
    @pl.when(pl.program_id(0) == 0)
    def _():
        carry_ref[...] = jnp.zeros((N_EXPERTS, LANES), F32)

    xq = xq_ref[...]
    parts = []
    for h in range(XATTN_HEADS):
        cols = slice(h * XATTN_HEAD_DIM, (h + 1) * XATTN_HEAD_DIM)
        s = _dot_nt(xq[:, cols], kmem_ref[:, cols]) * (XATTN_HEAD_DIM ** -0.5)
        e = jnp.exp(s - jnp.max(s, axis=1, keepdims=True))
        p = e / jnp.sum(e, axis=1, keepdims=True)
        parts.append(_dot(p.astype(BF16), vmem_ref[:, cols]))
    ymem = jnp.concatenate(parts, axis=1).astype(BF16)

    gates = gates_ref[...].astype(F32)
    merged = (gates[:, 0:D_MODEL] * _dot(ypool_ref[...], wbp_ref[...])
              + gates[:, D_MODEL:2 * D_MODEL] * _dot(ymla_ref[...], wbm_ref[...])
              + gates[:, 2 * D_MODEL:3 * D_MODEL] * _dot(ymem, wbx_ref[...]))
    x1 = x_ref[...] + _dot(merged.astype(BF16), wout_ref[...])
    x1_ref[...] = x1
    h2 = _rms(x1, ffng_ref[...]).astype(BF16)
    h2_ref[...] = h2

    lt = _dot_nt(wr_ref[...], h2) + br_ref[...]
    gl = lt[N_EXPERTS:N_EXPERTS + N_GROUPS, :]
    gmax = jnp.max(gl, axis=0, keepdims=True)
    r4 = lax.broadcasted_iota(jnp.int32, (N_GROUPS, tm), 0).astype(F32)
    gidx = jnp.min(jnp.where(gl == gmax, r4, float(N_GROUPS)), axis=0, keepdims=True)
    pg = 1.0 / jnp.sum(jnp.exp(gl - gmax), axis=0, keepdims=True)
    esel = lt[0:EXPERTS_PER_GROUP, :]
    for g in range(1, N_GROUPS):
        esel = jnp.where(gidx == float(g), lt[g * EXPERTS_PER_GROUP:(g + 1) * EXPERTS_PER_GROUP, :], esel)
    r8 = lax.broadcasted_iota(jnp.int32, (EXPERTS_PER_GROUP, tm), 0).astype(F32)
    m1 = jnp.max(esel, axis=0, keepdims=True)
    i1 = jnp.min(jnp.where(esel == m1, r8, float(EXPERTS_PER_GROUP)), axis=0, keepdims=True)
    rest = jnp.where(r8 == i1, -jnp.inf, esel)
    m2 = jnp.max(rest, axis=0, keepdims=True)
    i2 = jnp.min(jnp.where(rest == m2, r8, float(EXPERTS_PER_GROUP)), axis=0, keepdims=True)
    e2 = jnp.exp(m2 - m1)
    den = 1.0 + e2
    wts_ref[0:1, :] = pg / den
    wts_ref[1:2, :] = pg * e2 / den
    ex1 = gidx * float(EXPERTS_PER_GROUP) + i1
    ex2 = gidx * float(EXPERTS_PER_GROUP) + i2

    r32 = lax.broadcasted_iota(jnp.int32, (N_EXPERTS, tm), 0).astype(F32)
    is1 = r32 == ex1
    is2 = r32 == ex2
    member = jnp.where(is1 | is2, 1.0, 0.0)
    upper = jnp.where(lax.broadcasted_iota(jnp.int32, (tm, tm), 0)
                      <= lax.broadcasted_iota(jnp.int32, (tm, tm), 1), 1.0, 0.0).astype(BF16)
    incl = _dot(member.astype(BF16), upper)
    run = jnp.floor((jnp.sum(member, axis=1, keepdims=True) + (RUN_ALIGN - 1)) / RUN_ALIGN) * RUN_ALIGN
    rcol = lax.broadcasted_iota(jnp.int32, (N_EXPERTS, 1), 0)
    run_start = jnp.zeros((N_EXPERTS, 1), F32)
    for e in range(N_EXPERTS - 1):
        run_start = run_start + jnp.where(rcol > e, run[e:e + 1, :], 0.0)
    pos = incl - 1.0 + run_start
    loc_ref[0:1, :] = jnp.sum(jnp.where(is1, pos, 0.0), axis=0, keepdims=True).astype(jnp.int32)
    loc_ref[1:2, :] = jnp.sum(jnp.where(is2, pos, 0.0), axis=0, keepdims=True).astype(jnp.int32)
    carry = carry_ref[...]
    total = carry + run
    cnt_tab_ref[...] = jnp.broadcast_to(run, (N_EXPERTS, LANES)).astype(jnp.int32)
    carry_tab_ref[...] = carry.astype(jnp.int32)
    carry_ref[...] = total
    counts_ref[...] = total.astype(jnp.int32)


def _merge(x2, ypool, ymla, xq, gates, kmem, vmem, wbp, wbm, wbx, wout, ffng, wr, br, *, B, S, tm, mem_len):
    T = B * S
    tps = S // tm
    row = lambda i: (i, 0)
    lane = lambda i: (0, i)
    memb = lambda i: (i // tps, 0)
    return pl.pallas_call(
        functools.partial(_merge_body, tm=tm),
        grid=(T // tm,),
        in_specs=[
            pl.BlockSpec((tm, D_MODEL), row),
            pl.BlockSpec((tm, POOL_DIM), row),
            pl.BlockSpec((tm, MLA_HEADS * V_HEAD_DIM), row),
            pl.BlockSpec((tm, XATTN_DIM), row),
            pl.BlockSpec((tm, N_BRANCHES * D_MODEL), row),
            pl.BlockSpec((mem_len, XATTN_DIM), memb),
            pl.BlockSpec((mem_len, XATTN_DIM), memb),
            _const_spec((POOL_DIM, D_MODEL)),
            _const_spec((MLA_HEADS * V_HEAD_DIM, D_MODEL)),
            _const_spec((XATTN_DIM, D_MODEL)),
            _const_spec((D_MODEL, D_MODEL)),
            _const_spec((1, D_MODEL)),
            _const_spec((ROUTER_ROWS, D_MODEL)),
            _const_spec((ROUTER_ROWS, 1)),
        ],
        out_specs=[
            pl.BlockSpec((tm, D_MODEL), row),
            pl.BlockSpec((tm, D_MODEL), row),
            pl.BlockSpec((2, tm), lane),
            pl.BlockSpec((2, tm), lane),
            pl.BlockSpec((N_EXPERTS, LANES), row),
            pl.BlockSpec((N_EXPERTS, LANES), row),
            pl.BlockSpec((N_EXPERTS, LANES), lambda i: (0, 0)),
        ],
        out_shape=[
            jax.ShapeDtypeStruct((T, D_MODEL), F32),
            jax.ShapeDtypeStruct((T, D_MODEL), BF16),
            jax.ShapeDtypeStruct((2, T), F32),
            jax.ShapeDtypeStruct((2, T), jnp.int32),
            jax.ShapeDtypeStruct((T // tm * N_EXPERTS, LANES), jnp.int32),
            jax.ShapeDtypeStruct((T // tm * N_EXPERTS, LANES), jnp.int32),
            jax.ShapeDtypeStruct((N_EXPERTS, LANES), jnp.int32),
        ],
        scratch_shapes=[pltpu.VMEM((N_EXPERTS, LANES), F32)],
        compiler_params=pltpu.CompilerParams(dimension_semantics=("arbitrary",),
                                             vmem_limit_bytes=VMEM_LIMIT_BYTES),
        name="merge",
    )(x2, ypool, ymla, xq, gates, kmem, vmem, wbp, wbm, wbx, wout, ffng, wr, br)


def _moe_pos_body(counts_ref, meta_ref):
    shift = int(math.log2(MOE_ROWS))
    cnt = counts_ref[...]
    padded = lax.shift_left(lax.shift_right_logical(cnt + (MOE_ROWS - 1), shift), shift)
    r32 = lax.broadcasted_iota(jnp.int32, (N_EXPERTS, LANES), 0)
    pad_start = jnp.zeros((N_EXPERTS, LANES), jnp.int32)
    for e in range(N_EXPERTS - 1):
        pad_start = pad_start + jnp.where(r32 > e, padded[e:e + 1, :], 0)
    pad_end = pad_start + padded

    lane = lax.broadcasted_iota(jnp.int32, (1, META_LANES), 1)
    block_row = lane * MOE_ROWS
    blk_e = jnp.zeros((1, META_LANES), jnp.int32)
    pe_row = jnp.zeros((1, META_LANES), jnp.int32)
    for e in range(N_EXPERTS):
        pe = pad_end[e:e + 1, 0:1]
        blk_e = blk_e + jnp.where(pe <= block_row, 1, 0)
        pe_row = pe_row + jnp.where(lane == META_PAD_END + e, pe, 0)
    blk_e = jnp.minimum(blk_e, N_EXPERTS - 1)
    nact = lax.shift_right_logical(pad_end[N_EXPERTS - 1:N_EXPERTS, 0:1], shift)
    meta = jnp.where(lane < META_PAD_END, blk_e, pe_row)
    meta_ref[...] = jnp.where(lane == META_NACT, nact, meta)


def _moe_pos(counts):
    full = lambda shape: pl.BlockSpec(shape, lambda i: (0,) * len(shape))
    return pl.pallas_call(
        _moe_pos_body,
        grid=(1,),
        in_specs=[full((N_EXPERTS, LANES))],
        out_specs=full((1, META_LANES)),
        out_shape=jax.ShapeDtypeStruct((1, META_LANES), jnp.int32),
        compiler_params=pltpu.CompilerParams(dimension_semantics=("arbitrary",)),
        name="moe_pos",
    )(counts)


def _pack_bf16_pairs(x):
    lo = pltpu.bitcast(x[:, 0:PACKED_DIM], U32)
    hi = pltpu.bitcast(x[:, PACKED_DIM:D_MODEL], U32)
    return hi | lax.shift_right_logical(lo, jnp.uint32(16))


def _unpack_bf16_pairs(w):
    lo = pltpu.bitcast(lax.shift_left(w, jnp.uint32(16)), F32)
    hi = pltpu.bitcast(w & jnp.uint32(0xFFFF0000), F32)
    return jnp.concatenate([lo, hi], axis=1)


def _loc_rows(tm):
    bf16_rows = 2 * RUN_ALIGN
    return pl.cdiv(2 * tm + N_EXPERTS * (RUN_ALIGN - 1), bf16_rows) * bf16_rows


def _run_copies(tile, cnt_ref, carry_ref, meta_ref, make_copy):
    def per_expert(e, local):
        n = pl.multiple_of(cnt_ref[tile * N_EXPERTS + e], RUN_ALIGN)
        start = jnp.where(e == 0, 0, meta_ref[META_PAD_END + jnp.maximum(e - 1, 0)])
        glob = pl.multiple_of(start + carry_ref[tile * N_EXPERTS + e], RUN_ALIGN)

        @pl.when(n > 0)
        def _():
            make_copy(pl.multiple_of(local, RUN_ALIGN), glob, n).start()

        return local + n

    return pl.multiple_of(lax.fori_loop(0, N_EXPERTS, per_expert, 0), RUN_ALIGN)


def _tile_rows(tile, cnt_ref):
    total = lax.fori_loop(0, N_EXPERTS, lambda e, t: t + cnt_ref[tile * N_EXPERTS + e], 0)
    return pl.multiple_of(total, RUN_ALIGN)


def _dispatch_body(meta_ref, cnt_ref, carry_ref, loc_ref, h2_ref, xs_ref, xloc_ref, zero_ref, sems, zsem,
                   *, tm, n_blocks):
    tile = pl.program_id(0)
    last_tile = pl.num_programs(0) - 1
    slot = lax.rem(tile, 2)
    nact = meta_ref[META_NACT]

    def wait_rows(t, s):
        n = _tile_rows(t, cnt_ref)
        pltpu.make_async_copy(xloc_ref.at[s, pl.ds(0, n)], xs_ref.at[pl.ds(0, n)], sems.at[s]).wait()

    def pad_copy(e):
        end = pl.multiple_of(meta_ref[META_PAD_END + e], MOE_ROWS)
        start = jnp.where(e == 0, 0, meta_ref[META_PAD_END + jnp.maximum(e - 1, 0)])
        used = carry_ref[last_tile * N_EXPERTS + e] + cnt_ref[last_tile * N_EXPERTS + e]
        first = pl.multiple_of(start + used, RUN_ALIGN)
        n = pl.multiple_of(end - first, RUN_ALIGN)
        return n, pltpu.make_async_copy(zero_ref.at[pl.ds(0, n)], xs_ref.at[pl.ds(first, n)], zsem)

    def tail_copy(b):
        return pltpu.make_async_copy(
            zero_ref, xs_ref.at[pl.ds(pl.multiple_of(b * MOE_ROWS, MOE_ROWS), MOE_ROWS)], zsem)

    def fill(op):
        def pad(e, c):
            n, cp = pad_copy(e)

            @pl.when(n > 0)
            def _():
                op(cp)
            return c

        def tail(b, c):
            op(tail_copy(b))
            return c

        lax.fori_loop(0, N_EXPERTS, pad, 0)
        lax.fori_loop(nact, n_blocks, tail, 0)

    @pl.when(tile == 0)
    def _():
        zero_ref[...] = jnp.zeros((MOE_ROWS, PACKED_DIM), U32)
        fill(lambda cp: cp.start())

    @pl.when(tile >= 2)
    def _():
        wait_rows(tile - 2, slot)

    r = lax.broadcasted_iota(jnp.int32, (_loc_rows(tm), tm), 0)
    onehot = jnp.where((r == loc_ref[0:1, :]) | (r == loc_ref[1:2, :]), 1.0, 0.0).astype(BF16)
    xloc_ref[slot] = _pack_bf16_pairs(_dot(onehot, h2_ref[...]))

    def make_copy(local, glob, n):
        return pltpu.make_async_copy(xloc_ref.at[slot, pl.ds(local, n)], xs_ref.at[pl.ds(glob, n)],
                                     sems.at[slot])

    _run_copies(tile, cnt_ref, carry_ref, meta_ref, make_copy)

    @pl.when(tile == last_tile)
    def _():
        @pl.when(tile >= 1)
        def _():
            wait_rows(tile - 1, 1 - slot)

        wait_rows(tile, slot)
        fill(lambda cp: cp.wait())


def _dispatch(meta1, cnt_tab, carry_tab, loc, h2, *, R, tm):
    T = h2.shape[0]
    return pl.pallas_call(
        functools.partial(_dispatch_body, tm=tm, n_blocks=R // MOE_ROWS),
        grid_spec=pltpu.PrefetchScalarGridSpec(
            num_scalar_prefetch=3,
            grid=(T // tm,),
            in_specs=[pl.BlockSpec((2, tm), lambda i, *_: (0, i)),
                      pl.BlockSpec((tm, D_MODEL), lambda i, *_: (i, 0))],
            out_specs=pl.BlockSpec(memory_space=pl.ANY),
            scratch_shapes=[pltpu.VMEM((2, _loc_rows(tm), PACKED_DIM), U32),
                            pltpu.VMEM((MOE_ROWS, PACKED_DIM), U32),
                            pltpu.SemaphoreType.DMA((2,)), pltpu.SemaphoreType.DMA],
        ),
        out_shape=jax.ShapeDtypeStruct((R, PACKED_DIM), U32),
        compiler_params=pltpu.CompilerParams(dimension_semantics=("arbitrary",),
                                             vmem_limit_bytes=VMEM_LIMIT_BYTES),
        name="dispatch",
    )(meta1, cnt_tab, carry_tab, loc, h2)


def _moe_ffn_body(meta_ref, xs_ref, wg_hbm, wu_hbm, wd_hbm, ys_ref,
                  xbuf, ybuf, wg_stage, wu_stage, wd_stage, wg_b, wu_b, wd_b, zero_ref,
                  xsem, ysem, wsem, zsem, *, n_blocks):
    nact = meta_ref[META_NACT]
    shift = int(math.log2(MOE_ROWS))

    def rows_of(b):
        return pl.ds(pl.multiple_of(b * MOE_ROWS, MOE_ROWS), MOE_ROWS)

    def x_copy(b, s):
        return pltpu.make_async_copy(xs_ref.at[rows_of(b)], xbuf.at[s], xsem.at[s])

    def y_copy(b, s):
        return pltpu.make_async_copy(ybuf.at[s], ys_ref.at[rows_of(b)], ysem.at[s])

    def w_copies(e, s):
        return (pltpu.make_async_copy(wg_hbm.at[e], wg_stage.at[s], wsem.at[s]),
                pltpu.make_async_copy(wu_hbm.at[e], wu_stage.at[s], wsem.at[s]),
                pltpu.make_async_copy(wd_hbm.at[e], wd_stage.at[s], wsem.at[s]))

    def tail_copy(b):
        return pltpu.make_async_copy(zero_ref, ys_ref.at[rows_of(b)], zsem)

    zero_ref[...] = jnp.zeros((MOE_ROWS, PACKED_DIM), U32)
    lax.fori_loop(nact, n_blocks, lambda b, c: (tail_copy(b).start(), c)[1], 0)

    def next_expert_block(e):
        return lax.shift_right_logical(meta_ref[META_PAD_END + e], shift)

    def start_weights(b, s):
        @pl.when(b < nact)
        def _():
            for cp in w_copies(meta_ref[jnp.minimum(b, n_blocks - 1)], s):
                cp.start()

    def expert_block_after(b, hops):
        for _ in range(hops):
            e_b = meta_ref[jnp.minimum(b, n_blocks - 1)]
            b = jnp.where(b < nact, next_expert_block(e_b), n_blocks)
        return b

    x_copy(0, 0).start()

    @pl.when(nact > 1)
    def _():
        x_copy(1, 1).start()

    for j in range(W_STAGES - 1):
        start_weights(expert_block_after(0, j), j)

    def block(b, xs, ys, k_prev):
        valid = b < nact
        e = meta_ref[jnp.minimum(b, nact - 1)]
        changed = jnp.logical_and(valid, jnp.logical_or(b == 0, e != meta_ref[jnp.maximum(b - 1, 0)]))
        k = jnp.where(changed, k_prev + 1, k_prev)

        @pl.when(changed)
        def _():
            ws = lax.rem(k, W_STAGES)
            for cp in w_copies(e, ws):
                cp.wait()
            wg_b[...] = wg_stage[ws].astype(BF16)
            wu_b[...] = wu_stage[ws].astype(BF16)
            wd_b[...] = wd_stage[ws].astype(BF16)
            ahead = expert_block_after(b, W_STAGES - 1)
            start_weights(ahead, lax.rem(k + W_STAGES - 1, W_STAGES))

        @pl.when(valid)
        def _():
            x_copy(b, xs).wait()

            @pl.when(b + 2 < nact)
            def _():
                x_copy(b + 2, (xs + 2) % X_SLOTS).start()

            @pl.when(b >= 2)
            def _():
                y_copy(b - 2, ys).wait()

            x = _unpack_bf16_pairs(xbuf[xs]).astype(BF16)
            g = _dot(x, wg_b[...])
            a = (g * jax.nn.sigmoid(g)) * _dot(x, wu_b[...])
            y = _dot(a.astype(BF16), wd_b[...])
            ybuf[ys] = _pack_bf16_pairs(y.astype(BF16).astype(F32))
            y_copy(b, ys).start()

        return k

    def quad(i, k):
        for j in range(X_SLOTS):
            k = block(X_SLOTS * i + j, j, j % 2, k)
        return k

    lax.fori_loop(0, lax.div(nact + (X_SLOTS - 1), X_SLOTS), quad, -1)

    @pl.when(nact >= 2)
    def _():
        y_copy(nact - 2, lax.rem(nact, 2)).wait()

    y_copy(nact - 1, lax.rem(nact - 1, 2)).wait()
    lax.fori_loop(nact, n_blocks, lambda b, c: (tail_copy(b).wait(), c)[1], 0)


def _moe_ffn(meta1, xs, wg, wu, wd):
    R = xs.shape[0]
    hbm = pl.BlockSpec(memory_space=pl.ANY)
    return pl.pallas_call(
        functools.partial(_moe_ffn_body, n_blocks=R // MOE_ROWS),
        grid_spec=pltpu.PrefetchScalarGridSpec(
            num_scalar_prefetch=1,
            grid=(1,),
            in_specs=[hbm, hbm, hbm, hbm],
            out_specs=hbm,
            scratch_shapes=[pltpu.VMEM((X_SLOTS, MOE_ROWS, PACKED_DIM), U32),
                            pltpu.VMEM((2, MOE_ROWS, PACKED_DIM), U32),
                            pltpu.VMEM((W_STAGES, D_MODEL, D_EXPERT), F32),
                            pltpu.VMEM((W_STAGES, D_MODEL, D_EXPERT), F32),
                            pltpu.VMEM((W_STAGES, D_EXPERT, D_MODEL), F32),
                            pltpu.VMEM((D_MODEL, D_EXPERT), BF16),
                            pltpu.VMEM((D_MODEL, D_EXPERT), BF16),
                            pltpu.VMEM((D_EXPERT, D_MODEL), BF16),
                            pltpu.VMEM((MOE_ROWS, PACKED_DIM), U32),
                            pltpu.SemaphoreType.DMA((X_SLOTS,)), pltpu.SemaphoreType.DMA((2,)),
                            pltpu.SemaphoreType.DMA((W_STAGES,)), pltpu.SemaphoreType.DMA],
        ),
        out_shape=jax.ShapeDtypeStruct((R, PACKED_DIM), U32),
        compiler_params=pltpu.CompilerParams(dimension_semantics=("arbitrary",),
                                             vmem_limit_bytes=VMEM_LIMIT_BYTES),
        name="moe_ffn",
    )(meta1, xs, wg, wu, wd)


def _combine_body(meta_ref, cnt_ref, carry_ref, x1_ref, loc_ref, wts_ref, fg_ref, ys_ref, out_ref,
                  yloc_ref, sems, *, tm):
    tile = pl.program_id(0)
    slot = lax.rem(tile, 2)

    def fetch(t, s):
        def make_copy(local, glob, n):
            return pltpu.make_async_copy(ys_ref.at[pl.ds(glob, n)], yloc_ref.at[s, pl.ds(local, n)],
                                         sems.at[s])
        _run_copies(t, cnt_ref, carry_ref, meta_ref, make_copy)

    @pl.when(tile == 0)
    def _():
        yloc_ref[...] = jnp.zeros(yloc_ref.shape, U32)
        fetch(tile, slot)

    @pl.when(tile + 1 < pl.num_programs(0))
    def _():
        fetch(tile + 1, 1 - slot)

    n = _tile_rows(tile, cnt_ref)
    pltpu.make_async_copy(ys_ref.at[pl.ds(0, n)], yloc_ref.at[slot, pl.ds(0, n)], sems.at[slot]).wait()
    r = lax.broadcasted_iota(jnp.int32, (_loc_rows(tm), tm), 0)
    is0 = r == loc_ref[0:1, :]
    is1 = r == loc_ref[1:2, :]
    row_w = jnp.sum(jnp.where(is0, wts_ref[0:1, :], 0.0) + jnp.where(is1, wts_ref[1:2, :], 0.0),
                    axis=1, keepdims=True)
    yw = (row_w * _unpack_bf16_pairs(yloc_ref[slot])).astype(BF16)
    twohot = jnp.where(is0 | is1, 1.0, 0.0).astype(BF16)
    moe = lax.dot_general(twohot, yw, (((0,), (0,)), ((), ())), preferred_element_type=F32)
    out_ref[...] = _rms(x1_ref[...] + moe, fg_ref[...])


def _combine(meta1, cnt_tab, carry_tab, x1, loc, wts, fg, ys, *, tm):
    T = x1.shape[0]
    row = lambda i, *_: (i, 0)
    lane = lambda i, *_: (0, i)
    return pl.pallas_call(
        functools.partial(_combine_body, tm=tm),
        grid_spec=pltpu.PrefetchScalarGridSpec(
            num_scalar_prefetch=3,
            grid=(T // tm,),
            in_specs=[pl.BlockSpec((tm, D_MODEL), row),
                      pl.BlockSpec((2, tm), lane),
                      pl.BlockSpec((2, tm), lane),
                      pl.BlockSpec((1, D_MODEL), lambda i, *_: (0, 0)),
                      pl.BlockSpec(memory_space=pl.ANY)],
            out_specs=pl.BlockSpec((tm, D_MODEL), row),
            scratch_shapes=[pltpu.VMEM((2, _loc_rows(tm), PACKED_DIM), U32),
                            pltpu.SemaphoreType.DMA((2,))],
        ),
        out_shape=jax.ShapeDtypeStruct((T, D_MODEL), F32),
        compiler_params=pltpu.CompilerParams(dimension_semantics=("arbitrary",),
                                             vmem_limit_bytes=VMEM_LIMIT_BYTES),
        name="combine",
    )(meta1, cnt_tab, carry_tab, x1, loc, wts, fg, ys)


def _tile(n, t):
    t = min(n, t)
    assert n % t == 0, (n, t)
    return t


def kernel(x, mem, positions, mix_norm_g, w_in, gate_b, q_norm_g, w_uq, kv_norm_g, w_uk, w_uv, pool_w, pool_scale, mem_norm_g, w_mem_kv, w_br_pool, w_br_mla, w_br_mem, w_out, ffn_norm_g, w_router_group, b_router_group, w_router_expert, b_router_expert, w_gate_e, w_up_e, w_down_e, final_norm_g):
    B, S, D = x.shape
    assert D == D_MODEL and mix_norm_g.shape[0] == 1
    T = B * S
    mem_len = mem.shape[1]
    tm = _tile(S, 512)
    l = 0

    win_p = _pack_w_in(jnp.swapaxes(w_in, 1, 2).reshape(W_IN_END, D_MODEL))
    wuq_p = jnp.pad(w_uq[l].reshape(Q_LORA_RANK, MLA_HEADS, QK_NOPE_DIM + QK_ROPE_DIM),
                    ((0, 0), (0, 0), (0, QK_PAD_DIM - QK_NOPE_DIM - QK_ROPE_DIM))
                    ).reshape(Q_LORA_RANK, MLA_HEADS * QK_PAD_DIM).astype(BF16)
    inv_freq = 1.0 / (ROPE_THETA ** (jnp.arange(0, QK_ROPE_DIM, 2, dtype=F32) / QK_ROPE_DIM))
    invf = jnp.concatenate([inv_freq, inv_freq, jnp.zeros((LANES - QK_ROPE_DIM,), F32)])[None, :]
    wr = jnp.concatenate([w_router_expert[l], w_router_group[l],
                          jnp.zeros((D_MODEL, ROUTER_ROWS - N_EXPERTS - N_GROUPS), F32)], axis=1).T.astype(BF16)
    br = jnp.concatenate([b_router_expert[l], b_router_group[l],
                          jnp.zeros((ROUTER_ROWS - N_EXPERTS - N_GROUPS,), F32)])[:, None].astype(F32)
    x2 = x.reshape(T, D_MODEL)
    pos2 = positions.reshape(T, 1)

    ypool, xq, gates, q, k, v = _mixer_in(
        x2, pos2, invf, mix_norm_g[l][None, :], win_p, gate_b[l], q_norm_g[l][None, :], wuq_p,
        kv_norm_g[l][None, :], w_uk[l].astype(BF16), w_uv[l].astype(BF16), pool_w[l].astype(BF16),
        pool_scale[l][None, :], B=B, S=S, tm=tm)
    kmem, vmem = _mem_kv(mem.reshape(B * mem_len, D_MODEL), mem_norm_g[l][None, :], w_mem_kv[l].astype(BF16))
    ymla = _mla_attn_unrolled(q, k, v, tq=tm).reshape(T, MLA_HEADS * V_HEAD_DIM)
    x1, h2, wts, loc, cnt_tab, carry_tab, counts = _merge(
        x2, ypool, ymla, xq, gates, kmem, vmem, w_br_pool[l].astype(BF16), w_br_mla[l].astype(BF16),
        w_br_mem[l].astype(BF16), w_out[l].astype(BF16), ffn_norm_g[l][None, :], wr, br,
        B=B, S=S, tm=tm, mem_len=mem_len)

    R = 2 * T + (T // tm) * N_EXPERTS * RUN_ALIGN + N_EXPERTS * MOE_ROWS
    assert R % MOE_ROWS == 0 and R // MOE_ROWS <= META_PAD_END
    meta1 = _moe_pos(counts).reshape(META_LANES)
    cnt1 = cnt_tab[:, 0]
    carry1 = carry_tab[:, 0]
    xs = _dispatch(meta1, cnt1, carry1, loc, h2, R=R, tm=tm)
    ys = _moe_ffn(meta1, xs, w_gate_e[l], w_up_e[l], w_down_e[l])
    out = _combine(meta1, cnt1, carry1, x1, loc, wts, final_norm_g[None, :], ys, tm=tm)
    return out.reshape(B, S, D_MODEL)
```

```python
import functools
import math

import jax
import jax.numpy as jnp
from jax import lax
from jax.experimental import pallas as pl
from jax.experimental.pallas import tpu as pltpu

D_MODEL = 1024
POOL_WINDOWS = (2, 4, 8, 16)
POOL_GROUP_DIM = 128
POOL_DIM = 512
MLA_HEADS = 8
QK_NOPE_DIM = 128
QK_ROPE_DIM = 64
V_HEAD_DIM = 128
Q_LORA_RANK = 384
KV_LORA_RANK = 256
ROPE_THETA = 10000.0
XATTN_HEADS = 4
XATTN_HEAD_DIM = 128
XATTN_DIM = 512
N_BRANCHES = 3
N_GROUPS = 4
EXPERTS_PER_GROUP = 8
N_EXPERTS = 32
D_EXPERT = 256
RMS_EPS = 1e-6
NEG_INF = -1e30

LANES = 128
QK_PAD_DIM = 2 * LANES
POOL_HALO = 16
MOE_ROWS = 512
X_SLOTS = 4
W_STAGES = 3
RUN_ALIGN = 8
PACKED_DIM = D_MODEL // 2
ROUTER_ROWS = 40
META_LANES = 256
META_PAD_END = 192
META_NACT = 255
VMEM_LIMIT_BYTES = 56 * 1024 * 1024

IN_POOL, IN_QD, IN_KV, IN_XQ, IN_GATE, IN_KR, IN_END = 0, 512, 896, 1152, 1664, 4736, 4864
W_IN_KR, W_IN_XQ, W_IN_END = 1152, 1216, 4800

F32 = jnp.float32
BF16 = jnp.bfloat16
U32 = jnp.uint32


def _rms(x, g):
    ms = jnp.mean(x * x, axis=-1, keepdims=True)
    return (x * lax.rsqrt(ms + RMS_EPS)) * g


def _dot(a, b):
    return jnp.dot(a, b, preferred_element_type=F32)


def _dot_nt(a, b):
    return lax.dot_general(a, b, (((1,), (1,)), ((), ())), preferred_element_type=F32)


def _const_spec(shape):
    nd = len(shape)
    return pl.BlockSpec(shape, lambda *_: (0,) * nd, pipeline_mode=pl.Buffered(1))


def _pack_w_in_body(wt_ref, o_ref):
    chunk = LANES

    def copy_cols(dst, src, n):
        for r in range(0, n, chunk):
            m = min(chunk, n - r)
            o_ref[:, dst + r:dst + r + m] = wt_ref[src + r:src + r + m, :].T.astype(BF16)

    copy_cols(0, 0, W_IN_KR)
    copy_cols(IN_XQ, W_IN_XQ, W_IN_END - W_IN_XQ)
    copy_cols(IN_KR, W_IN_KR, W_IN_XQ - W_IN_KR)
    o_ref[:, IN_KR + QK_ROPE_DIM:IN_END] = jnp.zeros((D_MODEL, LANES - QK_ROPE_DIM), BF16)


def _pack_w_in(w_t):
    whole = pl.BlockSpec(memory_space=pltpu.VMEM)
    return pl.pallas_call(
        _pack_w_in_body,
        in_specs=[whole],
        out_specs=whole,
        out_shape=jax.ShapeDtypeStruct((D_MODEL, IN_END), BF16),
        compiler_params=pltpu.CompilerParams(vmem_limit_bytes=VMEM_LIMIT_BYTES),
        name="pack_w_in",
    )(w_t)


def _mixer_in_body(x_ref, pos_ref, invf_ref, mixg_ref, win_ref, gateb_ref, qg_ref, wuq_ref,
                   kvg_ref, wuk_ref, wuv_ref, poolw_ref, pools_ref,
                   ypool_ref, xq_ref, gates_ref, q_ref, k_ref, v_ref, ext_ref,
                   *, tm, tiles_per_seq, q_scale):
    si = lax.rem(pl.program_id(0), tiles_per_seq)

    @pl.when(pl.program_id(0) == 0)
    def _():
        ext_ref[0:POOL_HALO, :] = jnp.zeros((POOL_HALO, POOL_DIM), F32)

    hb = _rms(x_ref[...], mixg_ref[...]).astype(BF16)

    u = _dot(hb, win_ref[:, IN_POOL:IN_QD])
    ext_ref[0:POOL_HALO, :] = jnp.where(si == 0, 0.0, ext_ref[0:POOL_HALO, :])
    ext_ref[POOL_HALO:POOL_HALO + tm, :] = u

    for c in range(N_BRANCHES):
        gl = _dot(hb, win_ref[:, IN_GATE + c * D_MODEL:IN_GATE + (c + 1) * D_MODEL])
        gates_ref[:, c * D_MODEL:(c + 1) * D_MODEL] = jax.nn.sigmoid(gl + gateb_ref[c:c + 1, :]).astype(BF16)

    ang = pos_ref[...].astype(F32) * invf_ref[...]
    cos = jnp.cos(ang)
    sin = jnp.sin(ang)
    first_half = lax.broadcasted_iota(jnp.int32, (tm, LANES), 1) < (QK_ROPE_DIM // 2)
    sin_signed = jnp.where(first_half, -sin, sin)

    def rope(r):
        swapped = jnp.where(first_half, pltpu.roll(r, LANES - QK_ROPE_DIM // 2, 1),
                            pltpu.roll(r, QK_ROPE_DIM // 2, 1))
        return r * cos + swapped * sin_signed

    cq = _rms(_dot(hb, win_ref[:, IN_QD:IN_KV]), qg_ref[...]).astype(BF16)
    for h in range(MLA_HEADS):
        qh = _dot(cq, wuq_ref[:, h * QK_PAD_DIM:(h + 1) * QK_PAD_DIM])
        q_ref[0, h, :, 0:LANES] = (qh[:, 0:LANES] * q_scale).astype(BF16)
        q_ref[0, h, :, LANES:QK_PAD_DIM] = (rope(qh[:, LANES:QK_PAD_DIM]) * q_scale).astype(BF16)

    ckv = _rms(_dot(hb, win_ref[:, IN_KV:IN_XQ]), kvg_ref[...]).astype(BF16)
    kr = rope(_dot(hb, win_ref[:, IN_KR:IN_END])).astype(BF16)
    for hp in range(MLA_HEADS // 2):
        cols = slice(hp * 2 * LANES, (hp + 1) * 2 * LANES)
        kn = _dot(ckv, wuk_ref[:, cols]).astype(BF16)
        vv = _dot(ckv, wuv_ref[:, cols]).astype(BF16)
        for j in range(2):
            h = 2 * hp + j
            k_ref[0, h, :, 0:LANES] = kn[:, j * LANES:(j + 1) * LANES]
            k_ref[0, h, :, LANES:QK_PAD_DIM] = kr
            v_ref[0, h] = vv[:, j * LANES:(j + 1) * LANES]

    t_seq = lax.broadcasted_iota(jnp.int32, (tm, 1), 0) + si * tm
    for g, w in enumerate(POOL_WINDOWS):
        lo = g * POOL_GROUP_DIM
        hi = lo + POOL_GROUP_DIM
        acc = u[:, lo:hi]
        for j in range(1, w):
            acc = acc + ext_ref[POOL_HALO - j:POOL_HALO - j + tm, lo:hi]
        cnt = jnp.minimum(t_seq + 1, w).astype(F32)
        p = acc / cnt - u[:, lo:hi]
        y = _dot(p.astype(BF16), poolw_ref[g]) * pools_ref[:, lo:hi]
        ypool_ref[:, lo:hi] = y.astype(BF16)
    ext_ref[0:POOL_HALO, :] = ext_ref[tm:tm + POOL_HALO, :]

    xq_ref[...] = _dot(hb, win_ref[:, IN_XQ:IN_GATE]).astype(BF16)


def _mixer_in(x2, pos2, invf, mixg, win_p, gate_b, qg, wuq_p, kvg, wuk, wuv, pool_w, pool_s, *, B, S, tm):
    T = B * S
    tps = S // tm
    q_scale = (QK_NOPE_DIM + QK_ROPE_DIM) ** -0.5 * math.log2(math.e)
    body = functools.partial(_mixer_in_body, tm=tm, tiles_per_seq=tps, q_scale=q_scale)
    row = lambda i: (i, 0)
    head = lambda i: (i // tps, 0, i % tps, 0)
    return pl.pallas_call(
        body,
        grid=(T // tm,),
        in_specs=[
            pl.BlockSpec((tm, D_MODEL), row),
            pl.BlockSpec((tm, 1), row),
            _const_spec((1, LANES)),
            _const_spec((1, D_MODEL)),
            _const_spec((D_MODEL, IN_END)),
            _const_spec((N_BRANCHES, D_MODEL)),
            _const_spec((1, Q_LORA_RANK)),
            _const_spec((Q_LORA_RANK, MLA_HEADS * QK_PAD_DIM)),
            _const_spec((1, KV_LORA_RANK)),
            _const_spec((KV_LORA_RANK, MLA_HEADS * QK_NOPE_DIM)),
            _const_spec((KV_LORA_RANK, MLA_HEADS * V_HEAD_DIM)),
            _const_spec((len(POOL_WINDOWS), POOL_GROUP_DIM, POOL_GROUP_DIM)),
            _const_spec((1, POOL_DIM)),
        ],
        out_specs=[
            pl.BlockSpec((tm, POOL_DIM), row),
            pl.BlockSpec((tm, XATTN_DIM), row),
            pl.BlockSpec((tm, N_BRANCHES * D_MODEL), row),
            pl.BlockSpec((1, MLA_HEADS, tm, QK_PAD_DIM), head),
            pl.BlockSpec((1, MLA_HEADS, tm, QK_PAD_DIM), head),
            pl.BlockSpec((1, MLA_HEADS, tm, V_HEAD_DIM), head),
        ],
        out_shape=[
            jax.ShapeDtypeStruct((T, POOL_DIM), BF16),
            jax.ShapeDtypeStruct((T, XATTN_DIM), BF16),
            jax.ShapeDtypeStruct((T, N_BRANCHES * D_MODEL), BF16),
            jax.ShapeDtypeStruct((B, MLA_HEADS, S, QK_PAD_DIM), BF16),
            jax.ShapeDtypeStruct((B, MLA_HEADS, S, QK_PAD_DIM), BF16),
            jax.ShapeDtypeStruct((B, MLA_HEADS, S, V_HEAD_DIM), BF16),
        ],
        scratch_shapes=[pltpu.VMEM((tm + POOL_HALO, POOL_DIM), F32)],
        compiler_params=pltpu.CompilerParams(dimension_semantics=("arbitrary",),
                                             vmem_limit_bytes=VMEM_LIMIT_BYTES),
        name="mixer_in",
    )(x2, pos2, invf, mixg, win_p, gate_b, qg, wuq_p, kvg, wuk, wuv, pool_w, pool_s)


def _mem_kv_body(mem_ref, g_ref, w_ref, k_ref, v_ref):
    mb = _rms(mem_ref[...], g_ref[...]).astype(BF16)
    kv = _dot(mb, w_ref[...])
    k_ref[...] = kv[:, 0:XATTN_DIM].astype(BF16)
    v_ref[...] = kv[:, XATTN_DIM:2 * XATTN_DIM].astype(BF16)


def _mem_kv(mem2, g, w):
    rows = mem2.shape[0]
    tr = min(rows, 512)
    return pl.pallas_call(
        _mem_kv_body,
        grid=(rows // tr,),
        in_specs=[pl.BlockSpec((tr, D_MODEL), lambda i: (i, 0)),
                  _const_spec((1, D_MODEL)),
                  _const_spec((D_MODEL, 2 * XATTN_DIM))],
        out_specs=[pl.BlockSpec((tr, XATTN_DIM), lambda i: (i, 0)),
                   pl.BlockSpec((tr, XATTN_DIM), lambda i: (i, 0))],
        out_shape=[jax.ShapeDtypeStruct((rows, XATTN_DIM), BF16),
                   jax.ShapeDtypeStruct((rows, XATTN_DIM), BF16)],
        compiler_params=pltpu.CompilerParams(dimension_semantics=("arbitrary",)),
        name="mem_kv",
    )(mem2, g, w)


def _attn_unrolled_body(q_ref, k_ref, v_ref, o_ref, s_a, s_b, mc_a, mc_b, m_ref, l_ref, acc_ref, *, nq, tq):
    s_bufs = (s_a, s_b)
    mc_bufs = (mc_a, mc_b)
    half = tq // 2
    blocks = [(qi, kb) for qi in range(nq) for kb in range(qi + 1)]

    def key_width(qi, kb, r):
        return half if (qi == kb and r == 0) else tq

    def scores(i, slot):
        qi, kb = blocks[i]
        for r in range(2):
            w = key_width(qi, kb, r)
            rows = slice(r * half, (r + 1) * half)
            s = _dot_nt(q_ref[0, 0, qi * tq + r * half:qi * tq + (r + 1) * half, :],
                        k_ref[0, 0, kb * tq:kb * tq + w, :])
            if qi == kb:
                ri = lax.broadcasted_iota(jnp.int32, (half, w), 0) + r * half
                ci = lax.broadcasted_iota(jnp.int32, (half, w), 1)
                s = jnp.where(ci <= ri, s, NEG_INF)
            s_bufs[slot][rows, 0:w] = s
            mc_bufs[slot][rows, :] = jnp.broadcast_to(jnp.max(s, axis=1, keepdims=True), (half, LANES))

    def accumulate(i, slot):
        qi, kb = blocks[i]
        is_first = kb == 0
        is_last = kb == qi
        for r in range(2):
            w = key_width(qi, kb, r)
            rows = slice(r * half, (r + 1) * half)
            if is_first:
                m_new = mc_bufs[slot][rows, :]
            else:
                m_prev = m_ref[rows, :]
                m_new = jnp.maximum(m_prev, mc_bufs[slot][rows, :])
                alpha = jnp.exp2(m_prev - m_new)
            p = jnp.exp2(s_bufs[slot][rows, 0:w] - jnp.concatenate([m_new] * (w // LANES), axis=1))
            psum = p[:, 0:LANES]
            for c in range(1, w // LANES):
                psum = psum + p[:, c * LANES:(c + 1) * LANES]
            l_new = psum if is_first else alpha * l_ref[rows, :] + psum
            acc = _dot(p.astype(BF16), v_ref[0, 0, kb * tq:kb * tq + w, :])
            if not is_first:
                acc = alpha * acc_ref[rows, :] + acc
            if is_last:
                inv = 1.0 / jnp.sum(l_new, axis=1, keepdims=True)
                o_ref[0, qi * tq + r * half:qi * tq + (r + 1) * half, :] = (acc * inv).astype(BF16)
            else:
                l_ref[rows, :] = l_new
                m_ref[rows, :] = m_new
                acc_ref[rows, :] = acc

    scores(0, 0)
    for i in range(len(blocks)):
        if i + 1 < len(blocks):
            scores(i + 1, (i + 1) % 2)
        accumulate(i, i % 2)


def _mla_attn_unrolled(q, k, v, *, tq):
    B, H, S, _ = q.shape
    per_head = lambda b, h: (b, h, 0, 0)
    return pl.pallas_call(
        functools.partial(_attn_unrolled_body, nq=S // tq, tq=tq),
        grid=(B, H),
        in_specs=[pl.BlockSpec((1, 1, S, QK_PAD_DIM), per_head),
                  pl.BlockSpec((1, 1, S, QK_PAD_DIM), per_head),
                  pl.BlockSpec((1, 1, S, V_HEAD_DIM), per_head)],
        out_specs=pl.BlockSpec((1, S, V_HEAD_DIM), lambda b, h: (b, 0, h)),
        out_shape=jax.ShapeDtypeStruct((B, S, H * V_HEAD_DIM), BF16),
        scratch_shapes=[pltpu.VMEM((tq, tq), F32), pltpu.VMEM((tq, tq), F32),
                        pltpu.VMEM((tq, LANES), F32), pltpu.VMEM((tq, LANES), F32),
                        pltpu.VMEM((tq, LANES), F32), pltpu.VMEM((tq, LANES), F32),
                        pltpu.VMEM((tq, V_HEAD_DIM), F32)],
        compiler_params=pltpu.CompilerParams(dimension_semantics=("arbitrary", "arbitrary"),
                                             vmem_limit_bytes=VMEM_LIMIT_BYTES),
        name="mla_attn",
    )(q, k, v)


def _merge_body(x_ref, ypool_ref, ymla_ref, xq_ref, gates_ref, kmem_ref, vmem_ref,
                wbp_ref, wbm_ref, wbx_ref, wout_ref, ffng_ref, wr_ref, br_ref,
                x1_ref, h2_ref, wts_ref, loc_ref, cnt_tab_ref, carry_tab_ref, counts_ref, carry_ref, *, tm):
    @pl.when(pl.program_id(0) == 0)
    def _():
        carry_ref[...] = jnp.zeros((N_EXPERTS, LANES), F32)

    xq = xq_ref[...]
    parts = []
    for h in range(XATTN_HEADS):
        cols = slice(h * XATTN_HEAD_DIM, (h + 1) * XATTN_HEAD_DIM)
        s = _dot_nt(xq[:, cols], kmem_ref[:, cols]) * (XATTN_HEAD_DIM ** -0.5)
        e = jnp.exp(s - jnp.max(s, axis=1, keepdims=True))
        p = e / jnp.sum(e, axis=1, keepdims=True)
        parts.append(_dot(p.astype(BF16), vmem_ref[:, cols]))
    ymem = jnp.concatenate(parts, axis=1).astype(BF16)

    gates = gates_ref[...].astype(F32)
    merged = (gates[:, 0:D_MODEL] * _dot(ypool_ref[...], wbp_ref[...])
              + gates[:, D_MODEL:2 * D_MODEL] * _dot(ymla_ref[...], wbm_ref[...])
              + gates[:, 2 * D_MODEL:3 * D_MODEL] * _dot(ymem, wbx_ref[...]))
    x1 = x_ref[...] + _dot(merged.astype(BF16), wout_ref[...])
    x1_ref[...] = x1
    h2 = _rms(x1, ffng_ref[...]).astype(BF16)
    h2_ref[...] = h2

    lt = _dot_nt(wr_ref[...], h2) + br_ref[...]
    gl = lt[N_EXPERTS:N_EXPERTS + N_GROUPS, :]
    gmax = jnp.max(gl, axis=0, keepdims=True)
    r4 = lax.broadcasted_iota(jnp.int32, (N_GROUPS, tm), 0).astype(F32)
    gidx = jnp.min(jnp.where(gl == gmax, r4, float(N_GROUPS)), axis=0, keepdims=True)
    pg = 1.0 / jnp.sum(jnp.exp(gl - gmax), axis=0, keepdims=True)
    esel = lt[0:EXPERTS_PER_GROUP, :]
    for g in range(1, N_GROUPS):
        esel = jnp.where(gidx == float(g), lt[g * EXPERTS_PER_GROUP:(g + 1) * EXPERTS_PER_GROUP, :], esel)
    r8 = lax.broadcasted_iota(jnp.int32, (EXPERTS_PER_GROUP, tm), 0).astype(F32)
    m1 = jnp.max(esel, axis=0, keepdims=True)
    i1 = jnp.min(jnp.where(esel == m1, r8, float(EXPERTS_PER_GROUP)), axis=0, keepdims=True)
    rest = jnp.where(r8 == i1, -jnp.inf, esel)
    m2 = jnp.max(rest, axis=0, keepdims=True)
    i2 = jnp.min(jnp.where(rest == m2, r8, float(EXPERTS_PER_GROUP)), axis=0, keepdims=True)
    e2 = jnp.exp(m2 - m1)
    den = 1.0 + e2
    wts_ref[0:1, :] = pg / den
    wts_ref[1:2, :] = pg * e2 / den
    ex1 = gidx * float(EXPERTS_PER_GROUP) + i1
    ex2 = gidx * float(EXPERTS_PER_GROUP) + i2

    r32 = lax.broadcasted_iota(jnp.int32, (N_EXPERTS, tm), 0).astype(F32)
    is1 = r32 == ex1
    is2 = r32 == ex2
    member = jnp.where(is1 | is2, 1.0, 0.0)
    upper = jnp.where(lax.broadcasted_iota(jnp.int32, (tm, tm), 0)
                      <= lax.broadcasted_iota(jnp.int32, (tm, tm), 1), 1.0, 0.0).astype(BF16)
    incl = _dot(member.astype(BF16), upper)
    run = jnp.floor((jnp.sum(member, axis=1, keepdims=True) + (RUN_ALIGN - 1)) / RUN_ALIGN) * RUN_ALIGN
    rcol = lax.broadcasted_iota(jnp.int32, (N_EXPERTS, 1), 0)
    run_start = jnp.zeros((N_EXPERTS, 1), F32)
    for e in range(N_EXPERTS - 1):
        run_start = run_start + jnp.where(rcol > e, run[e:e + 1, :], 0.0)
    pos = incl - 1.0 + run_start
    loc_ref[0:1, :] = jnp.sum(jnp.where(is1, pos, 0.0), axis=0, keepdims=True).astype(jnp.int32)
    loc_ref[1:2, :] = jnp.sum(jnp.where(is2, pos, 0.0), axis=0, keepdims=True).astype(jnp.int32)
    carry = carry_ref[...]
    total = carry + run
    cnt_tab_ref[...] = jnp.broadcast_to(run, (N_EXPERTS, LANES)).astype(jnp.int32)
    carry_tab_ref[...] = carry.astype(jnp.int32)
    carry_ref[...] = total
    counts_ref[...] = total.astype(jnp.int32)


def _merge(x2, ypool, ymla, xq, gates, kmem, vmem, wbp, wbm, wbx, wout, ffng, wr, br, *, B, S, tm, mem_len):
    T = B * S
    tps = S // tm
    row = lambda i: (i, 0)
    lane = lambda i: (0, i)
    memb = lambda i: (i // tps, 0)
    return pl.pallas_call(
        functools.partial(_merge_body, tm=tm),
        grid=(T // tm,),
        in_specs=[
            pl.BlockSpec((tm, D_MODEL), row),
            pl.BlockSpec((tm, POOL_DIM), row),
            pl.BlockSpec((tm, MLA_HEADS * V_HEAD_DIM), row),
            pl.BlockSpec((tm, XATTN_DIM), row),
            pl.BlockSpec((tm, N_BRANCHES * D_MODEL), row),
            pl.BlockSpec((mem_len, XATTN_DIM), memb),
            pl.BlockSpec((mem_len, XATTN_DIM), memb),
            _const_spec((POOL_DIM, D_MODEL)),
            _const_spec((MLA_HEADS * V_HEAD_DIM, D_MODEL)),
            _const_spec((XATTN_DIM, D_MODEL)),
            _const_spec((D_MODEL, D_MODEL)),
            _const_spec((1, D_MODEL)),
            _const_spec((ROUTER_ROWS, D_MODEL)),
            _const_spec((ROUTER_ROWS, 1)),
        ],
        out_specs=[
            pl.BlockSpec((tm, D_MODEL), row),
            pl.BlockSpec((tm, D_MODEL), row),
            pl.BlockSpec((2, tm), lane),
            pl.BlockSpec((2, tm), lane),
            pl.BlockSpec((N_EXPERTS, LANES), row),
            pl.BlockSpec((N_EXPERTS, LANES), row),
            pl.BlockSpec((N_EXPERTS, LANES), lambda i: (0, 0)),
        ],
        out_shape=[
            jax.ShapeDtypeStruct((T, D_MODEL), F32),
            jax.ShapeDtypeStruct((T, D_MODEL), BF16),
            jax.ShapeDtypeStruct((2, T), F32),
            jax.ShapeDtypeStruct((2, T), jnp.int32),
            jax.ShapeDtypeStruct((T // tm * N_EXPERTS, LANES), jnp.int32),
            jax.ShapeDtypeStruct((T // tm * N_EXPERTS, LANES), jnp.int32),
            jax.ShapeDtypeStruct((N_EXPERTS, LANES), jnp.int32),
        ],
        scratch_shapes=[pltpu.VMEM((N_EXPERTS, LANES), F32)],
        compiler_params=pltpu.CompilerParams(dimension_semantics=("arbitrary",),
                                             vmem_limit_bytes=VMEM_LIMIT_BYTES),
        name="merge",
    )(x2, ypool, ymla, xq, gates, kmem, vmem, wbp, wbm, wbx, wout, ffng, wr, br)


def _moe_pos_body(counts_ref, meta_ref):
    shift = int(math.log2(MOE_ROWS))
    cnt = counts_ref[...]
    padded = lax.shift_left(lax.shift_right_logical(cnt + (MOE_ROWS - 1), shift), shift)
    r32 = lax.broadcasted_iota(jnp.int32, (N_EXPERTS, LANES), 0)
    pad_start = jnp.zeros((N_EXPERTS, LANES), jnp.int32)
    for e in range(N_EXPERTS - 1):
        pad_start = pad_start + jnp.where(r32 > e, padded[e:e + 1, :], 0)
    pad_end = pad_start + padded

    lane = lax.broadcasted_iota(jnp.int32, (1, META_LANES), 1)
    block_row = lane * MOE_ROWS
    blk_e = jnp.zeros((1, META_LANES), jnp.int32)
    pe_row = jnp.zeros((1, META_LANES), jnp.int32)
    for e in range(N_EXPERTS):
        pe = pad_end[e:e + 1, 0:1]
        blk_e = blk_e + jnp.where(pe <= block_row, 1, 0)
        pe_row = pe_row + jnp.where(lane == META_PAD_END + e, pe, 0)
    blk_e = jnp.minimum(blk_e, N_EXPERTS - 1)
    nact = lax.shift_right_logical(pad_end[N_EXPERTS - 1:N_EXPERTS, 0:1], shift)
    meta = jnp.where(lane < META_PAD_END, blk_e, pe_row)
    meta_ref[...] = jnp.where(lane == META_NACT, nact, meta)


def _moe_pos(counts):
    full = lambda shape: pl.BlockSpec(shape, lambda i: (0,) * len(shape))
    return pl.pallas_call(
        _moe_pos_body,
        grid=(1,),
        in_specs=[full((N_EXPERTS, LANES))],
        out_specs=full((1, META_LANES)),
        out_shape=jax.ShapeDtypeStruct((1, META_LANES), jnp.int32),
        compiler_params=pltpu.CompilerParams(dimension_semantics=("arbitrary",)),
        name="moe_pos",
    )(counts)


def _pack_bf16_pairs(x):
    lo = pltpu.bitcast(x[:, 0:PACKED_DIM], U32)
    hi = pltpu.bitcast(x[:, PACKED_DIM:D_MODEL], U32)
    return hi | lax.shift_right_logical(lo, jnp.uint32(16))


def _unpack_bf16_pairs(w):
    lo = pltpu.bitcast(lax.shift_left(w, jnp.uint32(16)), F32)
    hi = pltpu.bitcast(w & jnp.uint32(0xFFFF0000), F32)
    return jnp.concatenate([lo, hi], axis=1)


def _loc_rows(tm):
    bf16_rows = 2 * RUN_ALIGN
    return pl.cdiv(2 * tm + N_EXPERTS * (RUN_ALIGN - 1), bf16_rows) * bf16_rows


def _run_copies(tile, cnt_ref, carry_ref, meta_ref, make_copy):
    def per_expert(e, local):
        n = pl.multiple_of(cnt_ref[tile * N_EXPERTS + e], RUN_ALIGN)
        start = jnp.where(e == 0, 0, meta_ref[META_PAD_END + jnp.maximum(e - 1, 0)])
        glob = pl.multiple_of(start + carry_ref[tile * N_EXPERTS + e], RUN_ALIGN)

        @pl.when(n > 0)
        def _():
            make_copy(pl.multiple_of(local, RUN_ALIGN), glob, n).start()

        return local + n

    return pl.multiple_of(lax.fori_loop(0, N_EXPERTS, per_expert, 0), RUN_ALIGN)


def _tile_rows(tile, cnt_ref):
    total = lax.fori_loop(0, N_EXPERTS, lambda e, t: t + cnt_ref[tile * N_EXPERTS + e], 0)
    return pl.multiple_of(total, RUN_ALIGN)


def _dispatch_body(meta_ref, cnt_ref, carry_ref, loc_ref, h2_ref, xs_ref, xloc_ref, zero_ref, sems, zsem,
                   *, tm, n_blocks):
    tile = pl.program_id(0)
    last_tile = pl.num_programs(0) - 1
    slot = lax.rem(tile, 2)
    nact = meta_ref[META_NACT]

    def wait_rows(t, s):
        n = _tile_rows(t, cnt_ref)
        pltpu.make_async_copy(xloc_ref.at[s, pl.ds(0, n)], xs_ref.at[pl.ds(0, n)], sems.at[s]).wait()

    def pad_copy(e):
        end = pl.multiple_of(meta_ref[META_PAD_END + e], MOE_ROWS)
        start = jnp.where(e == 0, 0, meta_ref[META_PAD_END + jnp.maximum(e - 1, 0)])
        used = carry_ref[last_tile * N_EXPERTS + e] + cnt_ref[last_tile * N_EXPERTS + e]
        first = pl.multiple_of(start + used, RUN_ALIGN)
        n = pl.multiple_of(end - first, RUN_ALIGN)
        return n, pltpu.make_async_copy(zero_ref.at[pl.ds(0, n)], xs_ref.at[pl.ds(first, n)], zsem)

    def tail_copy(b):
        return pltpu.make_async_copy(
            zero_ref, xs_ref.at[pl.ds(pl.multiple_of(b * MOE_ROWS, MOE_ROWS), MOE_ROWS)], zsem)

    def fill(op):
        def pad(e, c):
            n, cp = pad_copy(e)

            @pl.when(n > 0)
            def _():
                op(cp)
            return c

        def tail(b, c):
            op(tail_copy(b))
            return c

        lax.fori_loop(0, N_EXPERTS, pad, 0)
        lax.fori_loop(nact, n_blocks, tail, 0)

    @pl.when(tile == 0)
    def _():
        zero_ref[...] = jnp.zeros((MOE_ROWS, PACKED_DIM), U32)
        fill(lambda cp: cp.start())

    @pl.when(tile >= 2)
    def _():
        wait_rows(tile - 2, slot)

    r = lax.broadcasted_iota(jnp.int32, (_loc_rows(tm), tm), 0)
    onehot = jnp.where((r == loc_ref[0:1, :]) | (r == loc_ref[1:2, :]), 1.0, 0.0).astype(BF16)
    xloc_ref[slot] = _pack_bf16_pairs(_dot(onehot, h2_ref[...]))

    def make_copy(local, glob, n):
        return pltpu.make_async_copy(xloc_ref.at[slot, pl.ds(local, n)], xs_ref.at[pl.ds(glob, n)],
                                     sems.at[slot])

    _run_copies(tile, cnt_ref, carry_ref, meta_ref, make_copy)

    @pl.when(tile == last_tile)
    def _():
        @pl.when(tile >= 1)
        def _():
            wait_rows(tile - 1, 1 - slot)

        wait_rows(tile, slot)
        fill(lambda cp: cp.wait())


def _dispatch(meta1, cnt_tab, carry_tab, loc, h2, *, R, tm):
    T = h2.shape[0]
    return pl.pallas_call(
        functools.partial(_dispatch_body, tm=tm, n_blocks=R // MOE_ROWS),
        grid_spec=pltpu.PrefetchScalarGridSpec(
            num_scalar_prefetch=3,
            grid=(T // tm,),
            in_specs=[pl.BlockSpec((2, tm), lambda i, *_: (0, i)),
                      pl.BlockSpec((tm, D_MODEL), lambda i, *_: (i, 0))],
            out_specs=pl.BlockSpec(memory_space=pl.ANY),
            scratch_shapes=[pltpu.VMEM((2, _loc_rows(tm), PACKED_DIM), U32),
                            pltpu.VMEM((MOE_ROWS, PACKED_DIM), U32),
                            pltpu.SemaphoreType.DMA((2,)), pltpu.SemaphoreType.DMA],
        ),
        out_shape=jax.ShapeDtypeStruct((R, PACKED_DIM), U32),
        compiler_params=pltpu.CompilerParams(dimension_semantics=("arbitrary",),
                                             vmem_limit_bytes=VMEM_LIMIT_BYTES),
        name="dispatch",
    )(meta1, cnt_tab, carry_tab, loc, h2)


def _moe_ffn_body(meta_ref, xs_ref, wg_hbm, wu_hbm, wd_hbm, ys_ref,
                  xbuf, ybuf, wg_stage, wu_stage, wd_stage, wg_b, wu_b, wd_b, zero_ref,
                  xsem, ysem, wsem, zsem, *, n_blocks):
    nact = meta_ref[META_NACT]
    shift = int(math.log2(MOE_ROWS))

    def rows_of(b):
        return pl.ds(pl.multiple_of(b * MOE_ROWS, MOE_ROWS), MOE_ROWS)

    def x_copy(b, s):
        return pltpu.make_async_copy(xs_ref.at[rows_of(b)], xbuf.at[s], xsem.at[s])

    def y_copy(b, s):
        return pltpu.make_async_copy(ybuf.at[s], ys_ref.at[rows_of(b)], ysem.at[s])

    def w_copies(e, s):
        return (pltpu.make_async_copy(wg_hbm.at[e], wg_stage.at[s], wsem.at[s]),
                pltpu.make_async_copy(wu_hbm.at[e], wu_stage.at[s], wsem.at[s]),
                pltpu.make_async_copy(wd_hbm.at[e], wd_stage.at[s], wsem.at[s]))

    def tail_copy(b):
        return pltpu.make_async_copy(zero_ref, ys_ref.at[rows_of(b)], zsem)

    zero_ref[...] = jnp.zeros((MOE_ROWS, PACKED_DIM), U32)
    lax.fori_loop(nact, n_blocks, lambda b, c: (tail_copy(b).start(), c)[1], 0)

    def next_expert_block(e):
        return lax.shift_right_logical(meta_ref[META_PAD_END + e], shift)

    def start_weights(b, s):
        @pl.when(b < nact)
        def _():
            for cp in w_copies(meta_ref[jnp.minimum(b, n_blocks - 1)], s):
                cp.start()

    x_copy(0, 0).start()

    @pl.when(nact > 1)
    def _():
        x_copy(1, 1).start()

    e_first = meta_ref[0]
    start_weights(0, 0)
    start_weights(next_expert_block(e_first), 1)

    def block(b, xs, ys, k_prev):
        valid = b < nact
        e = meta_ref[jnp.minimum(b, nact - 1)]
        changed = jnp.logical_and(valid, jnp.logical_or(b == 0, e != meta_ref[jnp.maximum(b - 1, 0)]))
        k = jnp.where(changed, k_prev + 1, k_prev)

        @pl.when(changed)
        def _():
            ws = lax.rem(k, W_STAGES)
            for cp in w_copies(e, ws):
                cp.wait()
            wg_b[...] = wg_stage[ws].astype(BF16)
            wu_b[...] = wu_stage[ws].astype(BF16)
            wd_b[...] = wd_stage[ws].astype(BF16)
            n1 = next_expert_block(e)
            e1 = meta_ref[jnp.minimum(n1, n_blocks - 1)]
            n2 = jnp.where(n1 < nact, next_expert_block(e1), n_blocks)
            start_weights(n2, lax.rem(k + 2, W_STAGES))

        @pl.when(valid)
        def _():
            x_copy(b, xs).wait()

            @pl.when(b + 2 < nact)
            def _():
                x_copy(b + 2, (xs + 2) % X_SLOTS).start()

            @pl.when(b >= 2)
            def _():
                y_copy(b - 2, ys).wait()

            x = _unpack_bf16_pairs(xbuf[xs]).astype(BF16)
            g = _dot(x, wg_b[...])
            a = (g * jax.nn.sigmoid(g)) * _dot(x, wu_b[...])
            y = _dot(a.astype(BF16), wd_b[...])
            ybuf[ys] = _pack_bf16_pairs(y.astype(BF16).astype(F32))
            y_copy(b, ys).start()

        return k

    def quad(i, k):
        for j in range(X_SLOTS):
            k = block(X_SLOTS * i + j, j, j % 2, k)
        return k

    lax.fori_loop(0, lax.div(nact + (X_SLOTS - 1), X_SLOTS), quad, -1)

    @pl.when(nact >= 2)
    def _():
        y_copy(nact - 2, lax.rem(nact, 2)).wait()

    y_copy(nact - 1, lax.rem(nact - 1, 2)).wait()
    lax.fori_loop(nact, n_blocks, lambda b, c: (tail_copy(b).wait(), c)[1], 0)


def _moe_ffn(meta1, xs, wg, wu, wd):
    R = xs.shape[0]
    hbm = pl.BlockSpec(memory_space=pl.ANY)
    return pl.pallas_call(
        functools.partial(_moe_ffn_body, n_blocks=R // MOE_ROWS),
        grid_spec=pltpu.PrefetchScalarGridSpec(
            num_scalar_prefetch=1,
            grid=(1,),
            in_specs=[hbm, hbm, hbm, hbm],
            out_specs=hbm,
            scratch_shapes=[pltpu.VMEM((X_SLOTS, MOE_ROWS, PACKED_DIM), U32),
                            pltpu.VMEM((2, MOE_ROWS, PACKED_DIM), U32),
                            pltpu.VMEM((W_STAGES, D_MODEL, D_EXPERT), F32),
                            pltpu.VMEM((W_STAGES, D_MODEL, D_EXPERT), F32),
                            pltpu.VMEM((W_STAGES, D_EXPERT, D_MODEL), F32),
                            pltpu.VMEM((D_MODEL, D_EXPERT), BF16),
                            pltpu.VMEM((D_MODEL, D_EXPERT), BF16),
                            pltpu.VMEM((D_EXPERT, D_MODEL), BF16),
                            pltpu.VMEM((MOE_ROWS, PACKED_DIM), U32),
                            pltpu.SemaphoreType.DMA((X_SLOTS,)), pltpu.SemaphoreType.DMA((2,)),
                            pltpu.SemaphoreType.DMA((W_STAGES,)), pltpu.SemaphoreType.DMA],
        ),
        out_shape=jax.ShapeDtypeStruct((R, PACKED_DIM), U32),
        compiler_params=pltpu.CompilerParams(dimension_semantics=("arbitrary",),
                                             vmem_limit_bytes=VMEM_LIMIT_BYTES),
        name="moe_ffn",
    )(meta1, xs, wg, wu, wd)


def _combine_body(meta_ref, cnt_ref, carry_ref, x1_ref, loc_ref, wts_ref, fg_ref, ys_ref, out_ref,
                  yloc_ref, sems, *, tm):
    tile = pl.program_id(0)
    slot = lax.rem(tile, 2)

    def fetch(t, s):
        def make_copy(local, glob, n):
            return pltpu.make_async_copy(ys_ref.at[pl.ds(glob, n)], yloc_ref.at[s, pl.ds(local, n)],
                                         sems.at[s])
        _run_copies(t, cnt_ref, carry_ref, meta_ref, make_copy)

    @pl.when(tile == 0)
    def _():
        yloc_ref[...] = jnp.zeros(yloc_ref.shape, U32)
        fetch(tile, slot)

    @pl.when(tile + 1 < pl.num_programs(0))
    def _():
        fetch(tile + 1, 1 - slot)

    n = _tile_rows(tile, cnt_ref)
    pltpu.make_async_copy(ys_ref.at[pl.ds(0, n)], yloc_ref.at[slot, pl.ds(0, n)], sems.at[slot]).wait()
    r = lax.broadcasted_iota(jnp.int32, (_loc_rows(tm), tm), 0)
    is0 = r == loc_ref[0:1, :]
    is1 = r == loc_ref[1:2, :]
    row_w = jnp.sum(jnp.where(is0, wts_ref[0:1, :], 0.0) + jnp.where(is1, wts_ref[1:2, :], 0.0),
                    axis=1, keepdims=True)
    yw = (row_w * _unpack_bf16_pairs(yloc_ref[slot])).astype(BF16)
    twohot = jnp.where(is0 | is1, 1.0, 0.0).astype(BF16)
    moe = lax.dot_general(twohot, yw, (((0,), (0,)), ((), ())), preferred_element_type=F32)
    out_ref[...] = _rms(x1_ref[...] + moe, fg_ref[...])


def _combine(meta1, cnt_tab, carry_tab, x1, loc, wts, fg, ys, *, tm):
    T = x1.shape[0]
    row = lambda i, *_: (i, 0)
    lane = lambda i, *_: (0, i)
    return pl.pallas_call(
        functools.partial(_combine_body, tm=tm),
        grid_spec=pltpu.PrefetchScalarGridSpec(
            num_scalar_prefetch=3,
            grid=(T // tm,),
            in_specs=[pl.BlockSpec((tm, D_MODEL), row),
                      pl.BlockSpec((2, tm), lane),
                      pl.BlockSpec((2, tm), lane),
                      pl.BlockSpec((1, D_MODEL), lambda i, *_: (0, 0)),
                      pl.BlockSpec(memory_space=pl.ANY)],
            out_specs=pl.BlockSpec((tm, D_MODEL), row),
            scratch_shapes=[pltpu.VMEM((2, _loc_rows(tm), PACKED_DIM), U32),
                            pltpu.SemaphoreType.DMA((2,))],
        ),
        out_shape=jax.ShapeDtypeStruct((T, D_MODEL), F32),
        compiler_params=pltpu.CompilerParams(dimension_semantics=("arbitrary",),
                                             vmem_limit_bytes=VMEM_LIMIT_BYTES),
        name="combine",
    )(meta1, cnt_tab, carry_tab, x1, loc, wts, fg, ys)


def _tile(n, t):
    t = min(n, t)
    assert n % t == 0, (n, t)
    return t


def kernel(x, mem, positions, mix_norm_g, w_in, gate_b, q_norm_g, w_uq, kv_norm_g, w_uk, w_uv, pool_w, pool_scale, mem_norm_g, w_mem_kv, w_br_pool, w_br_mla, w_br_mem, w_out, ffn_norm_g, w_router_group, b_router_group, w_router_expert, b_router_expert, w_gate_e, w_up_e, w_down_e, final_norm_g):
    B, S, D = x.shape
    assert D == D_MODEL and mix_norm_g.shape[0] == 1
    T = B * S
    mem_len = mem.shape[1]
    tm = _tile(S, 512)
    l = 0

    win_p = _pack_w_in(jnp.swapaxes(w_in, 1, 2).reshape(W_IN_END, D_MODEL))
    wuq_p = jnp.pad(w_uq[l].reshape(Q_LORA_RANK, MLA_HEADS, QK_NOPE_DIM + QK_ROPE_DIM),
                    ((0, 0), (0, 0), (0, QK_PAD_DIM - QK_NOPE_DIM - QK_ROPE_DIM))
                    ).reshape(Q_LORA_RANK, MLA_HEADS * QK_PAD_DIM).astype(BF16)
    inv_freq = 1.0 / (ROPE_THETA ** (jnp.arange(0, QK_ROPE_DIM, 2, dtype=F32) / QK_ROPE_DIM))
    invf = jnp.concatenate([inv_freq, inv_freq, jnp.zeros((LANES - QK_ROPE_DIM,), F32)])[None, :]
    wr = jnp.concatenate([w_router_expert[l], w_router_group[l],
                          jnp.zeros((D_MODEL, ROUTER_ROWS - N_EXPERTS - N_GROUPS), F32)], axis=1).T.astype(BF16)
    br = jnp.concatenate([b_router_expert[l], b_router_group[l],
                          jnp.zeros((ROUTER_ROWS - N_EXPERTS - N_GROUPS,), F32)])[:, None].astype(F32)
    x2 = x.reshape(T, D_MODEL)
    pos2 = positions.reshape(T, 1)

    ypool, xq, gates, q, k, v = _mixer_in(
        x2, pos2, invf, mix_norm_g[l][None, :], win_p, gate_b[l], q_norm_g[l][None, :], wuq_p,
        kv_norm_g[l][None, :], w_uk[l].astype(BF16), w_uv[l].astype(BF16), pool_w[l].astype(BF16),
        pool_scale[l][None, :], B=B, S=S, tm=tm)
    kmem, vmem = _mem_kv(mem.reshape(B * mem_len, D_MODEL), mem_norm_g[l][None, :], w_mem_kv[l].astype(BF16))
    ymla = _mla_attn_unrolled(q, k, v, tq=tm).reshape(T, MLA_HEADS * V_HEAD_DIM)
    x1, h2, wts, loc, cnt_tab, carry_tab, counts = _merge(
        x2, ypool, ymla, xq, gates, kmem, vmem, w_br_pool[l].astype(BF16), w_br_mla[l].astype(BF16),
        w_br_mem[l].astype(BF16), w_out[l].astype(BF16), ffn_norm_g[l][None, :], wr, br,
        B=B, S=S, tm=tm, mem_len=mem_len)

    R = 2 * T + (T // tm) * N_EXPERTS * RUN_ALIGN + N_EXPERTS * MOE_ROWS
    assert R % MOE_ROWS == 0 and R // MOE_ROWS <= META_PAD_END
    meta1 = _moe_pos(counts).reshape(META_LANES)
    cnt1 = cnt_tab[:, 0]
    carry1 = carry_tab[:, 0]
    xs = _dispatch(meta1, cnt1, carry1, loc, h2, R=R, tm=tm)
    ys = _moe_ffn(meta1, xs, w_gate_e[l], w_up_e[l], w_down_e[l])
    out = _combine(meta1, cnt1, carry1, x1, loc, wts, final_norm_g[None, :], ys, tm=tm)
    return out.reshape(B, S, D_MODEL)
```

```python
import functools
import math

import jax
import jax.numpy as jnp
from jax import lax
from jax.experimental import pallas as pl
from jax.experimental.pallas import tpu as pltpu

D_MODEL = 1024
POOL_WINDOWS = (2, 4, 8, 16)
POOL_GROUP_DIM = 128
POOL_DIM = 512
MLA_HEADS = 8
QK_NOPE_DIM = 128
QK_ROPE_DIM = 64
V_HEAD_DIM = 128
Q_LORA_RANK = 384
KV_LORA_RANK = 256
ROPE_THETA = 10000.0
XATTN_HEADS = 4
XATTN_HEAD_DIM = 128
XATTN_DIM = 512
N_BRANCHES = 3
N_GROUPS = 4
EXPERTS_PER_GROUP = 8
N_EXPERTS = 32
D_EXPERT = 256
RMS_EPS = 1e-6
NEG_INF = -1e30

LANES = 128
QK_PAD_DIM = 2 * LANES
POOL_HALO = 16
MOE_ROWS = 512
X_SLOTS = 4
W_STAGES = 3
RUN_ALIGN = 8
PACKED_DIM = D_MODEL // 2
ROUTER_ROWS = 40
META_LANES = 256
META_PAD_END = 192
META_NACT = 255
VMEM_LIMIT_BYTES = 56 * 1024 * 1024

IN_POOL, IN_QD, IN_KV, IN_XQ, IN_GATE, IN_KR, IN_END = 0, 512, 896, 1152, 1664, 4736, 4864
W_IN_KR, W_IN_XQ, W_IN_END = 1152, 1216, 4800

F32 = jnp.float32
BF16 = jnp.bfloat16
U32 = jnp.uint32


def _rms(x, g):
    ms = jnp.mean(x * x, axis=-1, keepdims=True)
    return (x * lax.rsqrt(ms + RMS_EPS)) * g


def _dot(a, b):
    return jnp.dot(a, b, preferred_element_type=F32)


def _dot_nt(a, b):
    return lax.dot_general(a, b, (((1,), (1,)), ((), ())), preferred_element_type=F32)


def _const_spec(shape):
    nd = len(shape)
    return pl.BlockSpec(shape, lambda *_: (0,) * nd, pipeline_mode=pl.Buffered(1))


def _pack_w_in_body(wt_ref, o_ref):
    chunk = LANES

    def copy_cols(dst, src, n):
        for r in range(0, n, chunk):
            m = min(chunk, n - r)
            o_ref[:, dst + r:dst + r + m] = wt_ref[src + r:src + r + m, :].T.astype(BF16)

    copy_cols(0, 0, W_IN_KR)
    copy_cols(IN_XQ, W_IN_XQ, W_IN_END - W_IN_XQ)
    copy_cols(IN_KR, W_IN_KR, W_IN_XQ - W_IN_KR)
    o_ref[:, IN_KR + QK_ROPE_DIM:IN_END] = jnp.zeros((D_MODEL, LANES - QK_ROPE_DIM), BF16)


def _pack_w_in(w_t):
    whole = pl.BlockSpec(memory_space=pltpu.VMEM)
    return pl.pallas_call(
        _pack_w_in_body,
        in_specs=[whole],
        out_specs=whole,
        out_shape=jax.ShapeDtypeStruct((D_MODEL, IN_END), BF16),
        compiler_params=pltpu.CompilerParams(vmem_limit_bytes=VMEM_LIMIT_BYTES),
        name="pack_w_in",
    )(w_t)


def _mixer_in_body(x_ref, pos_ref, invf_ref, mixg_ref, win_ref, gateb_ref, qg_ref, wuq_ref,
                   kvg_ref, wuk_ref, wuv_ref, poolw_ref, pools_ref,
                   ypool_ref, xq_ref, gates_ref, q_ref, k_ref, v_ref, ext_ref,
                   *, tm, tiles_per_seq, q_scale):
    si = lax.rem(pl.program_id(0), tiles_per_seq)

    @pl.when(pl.program_id(0) == 0)
    def _():
        ext_ref[0:POOL_HALO, :] = jnp.zeros((POOL_HALO, POOL_DIM), F32)

    hb = _rms(x_ref[...], mixg_ref[...]).astype(BF16)

    u = _dot(hb, win_ref[:, IN_POOL:IN_QD])
    ext_ref[0:POOL_HALO, :] = jnp.where(si == 0, 0.0, ext_ref[0:POOL_HALO, :])
    ext_ref[POOL_HALO:POOL_HALO + tm, :] = u

    for c in range(N_BRANCHES):
        gl = _dot(hb, win_ref[:, IN_GATE + c * D_MODEL:IN_GATE + (c + 1) * D_MODEL])
        gates_ref[:, c * D_MODEL:(c + 1) * D_MODEL] = jax.nn.sigmoid(gl + gateb_ref[c:c + 1, :]).astype(BF16)

    ang = pos_ref[...].astype(F32) * invf_ref[...]
    cos = jnp.cos(ang)
    sin = jnp.sin(ang)
    first_half = lax.broadcasted_iota(jnp.int32, (tm, LANES), 1) < (QK_ROPE_DIM // 2)
    sin_signed = jnp.where(first_half, -sin, sin)

    def rope(r):
        swapped = jnp.where(first_half, pltpu.roll(r, LANES - QK_ROPE_DIM // 2, 1),
                            pltpu.roll(r, QK_ROPE_DIM // 2, 1))
        return r * cos + swapped * sin_signed

    cq = _rms(_dot(hb, win_ref[:, IN_QD:IN_KV]), qg_ref[...]).astype(BF16)
    for h in range(MLA_HEADS):
        qh = _dot(cq, wuq_ref[:, h * QK_PAD_DIM:(h + 1) * QK_PAD_DIM])
        q_ref[0, h, :, 0:LANES] = (qh[:, 0:LANES] * q_scale).astype(BF16)
        q_ref[0, h, :, LANES:QK_PAD_DIM] = (rope(qh[:, LANES:QK_PAD_DIM]) * q_scale).astype(BF16)

    ckv = _rms(_dot(hb, win_ref[:, IN_KV:IN_XQ]), kvg_ref[...]).astype(BF16)
    kr = rope(_dot(hb, win_ref[:, IN_KR:IN_END])).astype(BF16)
    for hp in range(MLA_HEADS // 2):
        cols = slice(hp * 2 * LANES, (hp + 1) * 2 * LANES)
        kn = _dot(ckv, wuk_ref[:, cols]).astype(BF16)
        vv = _dot(ckv, wuv_ref[:, cols]).astype(BF16)
        for j in range(2):
            h = 2 * hp + j
            k_ref[0, h, :, 0:LANES] = kn[:, j * LANES:(j + 1) * LANES]
            k_ref[0, h, :, LANES:QK_PAD_DIM] = kr
            v_ref[0, h] = vv[:, j * LANES:(j + 1) * LANES]

    t_seq = lax.broadcasted_iota(jnp.int32, (tm, 1), 0) + si * tm
    for g, w in enumerate(POOL_WINDOWS):
        lo = g * POOL_GROUP_DIM
        hi = lo + POOL_GROUP_DIM
        acc = u[:, lo:hi]
        for j in range(1, w):
            acc = acc + ext_ref[POOL_HALO - j:POOL_HALO - j + tm, lo:hi]
        cnt = jnp.minimum(t_seq + 1, w).astype(F32)
        p = acc / cnt - u[:, lo:hi]
        y = _dot(p.astype(BF16), poolw_ref[g]) * pools_ref[:, lo:hi]
        ypool_ref[:, lo:hi] = y.astype(BF16)
    ext_ref[0:POOL_HALO, :] = ext_ref[tm:tm + POOL_HALO, :]

    xq_ref[...] = _dot(hb, win_ref[:, IN_XQ:IN_GATE]).astype(BF16)


def _mixer_in(x2, pos2, invf, mixg, win_p, gate_b, qg, wuq_p, kvg, wuk, wuv, pool_w, pool_s, *, B, S, tm):
    T = B * S
    tps = S // tm
    q_scale = (QK_NOPE_DIM + QK_ROPE_DIM) ** -0.5 * math.log2(math.e)
    body = functools.partial(_mixer_in_body, tm=tm, tiles_per_seq=tps, q_scale=q_scale)
    row = lambda i: (i, 0)
    head = lambda i: (i // tps, 0, i % tps, 0)
    return pl.pallas_call(
        body,
        grid=(T // tm,),
        in_specs=[
            pl.BlockSpec((tm, D_MODEL), row),
            pl.BlockSpec((tm, 1), row),
            _const_spec((1, LANES)),
            _const_spec((1, D_MODEL)),
            _const_spec((D_MODEL, IN_END)),
            _const_spec((N_BRANCHES, D_MODEL)),
            _const_spec((1, Q_LORA_RANK)),
            _const_spec((Q_LORA_RANK, MLA_HEADS * QK_PAD_DIM)),
            _const_spec((1, KV_LORA_RANK)),
            _const_spec((KV_LORA_RANK, MLA_HEADS * QK_NOPE_DIM)),
            _const_spec((KV_LORA_RANK, MLA_HEADS * V_HEAD_DIM)),
            _const_spec((len(POOL_WINDOWS), POOL_GROUP_DIM, POOL_GROUP_DIM)),
            _const_spec((1, POOL_DIM)),
        ],
        out_specs=[
            pl.BlockSpec((tm, POOL_DIM), row),
            pl.BlockSpec((tm, XATTN_DIM), row),
            pl.BlockSpec((tm, N_BRANCHES * D_MODEL), row),
            pl.BlockSpec((1, MLA_HEADS, tm, QK_PAD_DIM), head),
            pl.BlockSpec((1, MLA_HEADS, tm, QK_PAD_DIM), head),
            pl.BlockSpec((1, MLA_HEADS, tm, V_HEAD_DIM), head),
        ],
        out_shape=[
            jax.ShapeDtypeStruct((T, POOL_DIM), BF16),
            jax.ShapeDtypeStruct((T, XATTN_DIM), BF16),
            jax.ShapeDtypeStruct((T, N_BRANCHES * D_MODEL), BF16),
            jax.ShapeDtypeStruct((B, MLA_HEADS, S, QK_PAD_DIM), BF16),
            jax.ShapeDtypeStruct((B, MLA_HEADS, S, QK_PAD_DIM), BF16),
            jax.ShapeDtypeStruct((B, MLA_HEADS, S, V_HEAD_DIM), BF16),
        ],
        scratch_shapes=[pltpu.VMEM((tm + POOL_HALO, POOL_DIM), F32)],
        compiler_params=pltpu.CompilerParams(dimension_semantics=("arbitrary",),
                                             vmem_limit_bytes=VMEM_LIMIT_BYTES),
        name="mixer_in",
    )(x2, pos2, invf, mixg, win_p, gate_b, qg, wuq_p, kvg, wuk, wuv, pool_w, pool_s)


def _mem_kv_body(mem_ref, g_ref, w_ref, k_ref, v_ref):
    mb = _rms(mem_ref[...], g_ref[...]).astype(BF16)
    kv = _dot(mb, w_ref[...])
    k_ref[...] = kv[:, 0:XATTN_DIM].astype(BF16)
    v_ref[...] = kv[:, XATTN_DIM:2 * XATTN_DIM].astype(BF16)


def _mem_kv(mem2, g, w):
    rows = mem2.shape[0]
    tr = min(rows, 512)
    return pl.pallas_call(
        _mem_kv_body,
        grid=(rows // tr,),
        in_specs=[pl.BlockSpec((tr, D_MODEL), lambda i: (i, 0)),
                  _const_spec((1, D_MODEL)),
                  _const_spec((D_MODEL, 2 * XATTN_DIM))],
        out_specs=[pl.BlockSpec((tr, XATTN_DIM), lambda i: (i, 0)),
                   pl.BlockSpec((tr, XATTN_DIM), lambda i: (i, 0))],
        out_shape=[jax.ShapeDtypeStruct((rows, XATTN_DIM), BF16),
                   jax.ShapeDtypeStruct((rows, XATTN_DIM), BF16)],
        compiler_params=pltpu.CompilerParams(dimension_semantics=("arbitrary",)),
        name="mem_kv",
    )(mem2, g, w)


def _attn_unrolled_body(q_ref, k_ref, v_ref, o_ref, s_a, s_b, mc_a, mc_b, m_ref, l_ref, acc_ref, *, nq, tq):
    s_bufs = (s_a, s_b)
    mc_bufs = (mc_a, mc_b)
    half = tq // 2
    blocks = [(qi, kb) for qi in range(nq) for kb in range(qi + 1)]

    def key_width(qi, kb, r):
        return half if (qi == kb and r == 0) else tq

    def scores(i, slot):
        qi, kb = blocks[i]
        for r in range(2):
            w = key_width(qi, kb, r)
            rows = slice(r * half, (r + 1) * half)
            s = _dot_nt(q_ref[0, 0, qi * tq + r * half:qi * tq + (r + 1) * half, :],
                        k_ref[0, 0, kb * tq:kb * tq + w, :])
            if qi == kb:
                ri = lax.broadcasted_iota(jnp.int32, (half, w), 0) + r * half
                ci = lax.broadcasted_iota(jnp.int32, (half, w), 1)
                s = jnp.where(ci <= ri, s, NEG_INF)
            s_bufs[slot][rows, 0:w] = s
            mc_bufs[slot][rows, :] = jnp.broadcast_to(jnp.max(s, axis=1, keepdims=True), (half, LANES))

    def accumulate(i, slot):
        qi, kb = blocks[i]
        is_first = kb == 0
        is_last = kb == qi
        for r in range(2):
            w = key_width(qi, kb, r)
            rows = slice(r * half, (r + 1) * half)
            if is_first:
                m_new = mc_bufs[slot][rows, :]
            else:
                m_prev = m_ref[rows, :]
                m_new = jnp.maximum(m_prev, mc_bufs[slot][rows, :])
                alpha = jnp.exp2(m_prev - m_new)
            p = jnp.exp2(s_bufs[slot][rows, 0:w] - jnp.concatenate([m_new] * (w // LANES), axis=1))
            psum = p[:, 0:LANES]
            for c in range(1, w // LANES):
                psum = psum + p[:, c * LANES:(c + 1) * LANES]
            l_new = psum if is_first else alpha * l_ref[rows, :] + psum
            acc = _dot(p.astype(BF16), v_ref[0, 0, kb * tq:kb * tq + w, :])
            if not is_first:
                acc = alpha * acc_ref[rows, :] + acc
            if is_last:
                inv = 1.0 / jnp.sum(l_new, axis=1, keepdims=True)
                o_ref[0, qi * tq + r * half:qi * tq + (r + 1) * half, :] = (acc * inv).astype(BF16)
            else:
                l_ref[rows, :] = l_new
                m_ref[rows, :] = m_new
                acc_ref[rows, :] = acc

    scores(0, 0)
    for i in range(len(blocks)):
        if i + 1 < len(blocks):
            scores(i + 1, (i + 1) % 2)
        accumulate(i, i % 2)


def _mla_attn_unrolled(q, k, v, *, tq):
    B, H, S, _ = q.shape
    per_head = lambda b, h: (b, h, 0, 0)
    return pl.pallas_call(
        functools.partial(_attn_unrolled_body, nq=S // tq, tq=tq),
        grid=(B, H),
        in_specs=[pl.BlockSpec((1, 1, S, QK_PAD_DIM), per_head),
                  pl.BlockSpec((1, 1, S, QK_PAD_DIM), per_head),
                  pl.BlockSpec((1, 1, S, V_HEAD_DIM), per_head)],
        out_specs=pl.BlockSpec((1, S, V_HEAD_DIM), lambda b, h: (b, 0, h)),
        out_shape=jax.ShapeDtypeStruct((B, S, H * V_HEAD_DIM), BF16),
        scratch_shapes=[pltpu.VMEM((tq, tq), F32), pltpu.VMEM((tq, tq), F32),
                        pltpu.VMEM((tq, LANES), F32), pltpu.VMEM((tq, LANES), F32),
                        pltpu.VMEM((tq, LANES), F32), pltpu.VMEM((tq, LANES), F32),
                        pltpu.VMEM((tq, V_HEAD_DIM), F32)],
        compiler_params=pltpu.CompilerParams(dimension_semantics=("arbitrary", "arbitrary"),
                                             vmem_limit_bytes=VMEM_LIMIT_BYTES),
        name="mla_attn",
    )(q, k, v)


def _merge_body(x_ref, ypool_ref, ymla_ref, xq_ref, gates_ref, kmem_ref, vmem_ref,
                wbp_ref, wbm_ref, wbx_ref, wout_ref, ffng_ref, wr_ref, br_ref,
                x1_ref, h2_ref, wts_ref, loc_ref, cnt_tab_ref, carry_tab_ref, counts_ref, carry_ref, *, tm):
    @pl.when(pl.program_id(0) == 0)
    def _():
        carry_ref[...] = jnp.zeros((N_EXPERTS, LANES), F32)

    xq = xq_ref[...]
    parts = []
    for h in range(XATTN_HEADS):
        cols = slice(h * XATTN_HEAD_DIM, (h + 1) * XATTN_HEAD_DIM)
        s = _dot_nt(xq[:, cols], kmem_ref[:, cols]) * (XATTN_HEAD_DIM ** -0.5)
        e = jnp.exp(s - jnp.max(s, axis=1, keepdims=True))
        p = e / jnp.sum(e, axis=1, keepdims=True)
        parts.append(_dot(p.astype(BF16), vmem_ref[:, cols]))
    ymem = jnp.concatenate(parts, axis=1).astype(BF16)

    gates = gates_ref[...].astype(F32)
    merged = (gates[:, 0:D_MODEL] * _dot(ypool_ref[...], wbp_ref[...])
              + gates[:, D_MODEL:2 * D_MODEL] * _dot(ymla_ref[...], wbm_ref[...])
              + gates[:, 2 * D_MODEL:3 * D_MODEL] * _dot(ymem, wbx_ref[...]))
    x1 = x_ref[...] + _dot(merged.astype(BF16), wout_ref[...])
    x1_ref[...] = x1
    h2 = _rms(x1, ffng_ref[...]).astype(BF16)
    h2_ref[...] = h2

    lt = _dot_nt(wr_ref[...], h2) + br_ref[...]
    gl = lt[N_EXPERTS:N_EXPERTS + N_GROUPS, :]
    gmax = jnp.max(gl, axis=0, keepdims=True)
    r4 = lax.broadcasted_iota(jnp.int32, (N_GROUPS, tm), 0).astype(F32)
    gidx = jnp.min(jnp.where(gl == gmax, r4, float(N_GROUPS)), axis=0, keepdims=True)
    pg = 1.0 / jnp.sum(jnp.exp(gl - gmax), axis=0, keepdims=True)
    esel = lt[0:EXPERTS_PER_GROUP, :]
    for g in range(1, N_GROUPS):
        esel = jnp.where(gidx == float(g), lt[g * EXPERTS_PER_GROUP:(g + 1) * EXPERTS_PER_GROUP, :], esel)
    r8 = lax.broadcasted_iota(jnp.int32, (EXPERTS_PER_GROUP, tm), 0).astype(F32)
    m1 = jnp.max(esel, axis=0, keepdims=True)
    i1 = jnp.min(jnp.where(esel == m1, r8, float(EXPERTS_PER_GROUP)), axis=0, keepdims=True)
    rest = jnp.where(r8 == i1, -jnp.inf, esel)
    m2 = jnp.max(rest, axis=0, keepdims=True)
    i2 = jnp.min(jnp.where(rest == m2, r8, float(EXPERTS_PER_GROUP)), axis=0, keepdims=True)
    e2 = jnp.exp(m2 - m1)
    den = 1.0 + e2
    wts_ref[0:1, :] = pg / den
    wts_ref[1:2, :] = pg * e2 / den
    ex1 = gidx * float(EXPERTS_PER_GROUP) + i1
    ex2 = gidx * float(EXPERTS_PER_GROUP) + i2

    r32 = lax.broadcasted_iota(jnp.int32, (N_EXPERTS, tm), 0).astype(F32)
    is1 = r32 == ex1
    is2 = r32 == ex2
    member = jnp.where(is1 | is2, 1.0, 0.0)
    upper = jnp.where(lax.broadcasted_iota(jnp.int32, (tm, tm), 0)
                      <= lax.broadcasted_iota(jnp.int32, (tm, tm), 1), 1.0, 0.0).astype(BF16)
    incl = _dot(member.astype(BF16), upper)
    run = jnp.floor((jnp.sum(member, axis=1, keepdims=True) + (RUN_ALIGN - 1)) / RUN_ALIGN) * RUN_ALIGN
    rcol = lax.broadcasted_iota(jnp.int32, (N_EXPERTS, 1), 0)
    run_start = jnp.zeros((N_EXPERTS, 1), F32)
    for e in range(N_EXPERTS - 1):
        run_start = run_start + jnp.where(rcol > e, run[e:e + 1, :], 0.0)
    pos = incl - 1.0 + run_start
    loc_ref[0:1, :] = jnp.sum(jnp.where(is1, pos, 0.0), axis=0, keepdims=True).astype(jnp.int32)
    loc_ref[1:2, :] = jnp.sum(jnp.where(is2, pos, 0.0), axis=0, keepdims=True).astype(jnp.int32)
    carry = carry_ref[...]
    total = carry + run
    cnt_tab_ref[...] = jnp.broadcast_to(run, (N_EXPERTS, LANES)).astype(jnp.int32)
    carry_tab_ref[...] = carry.astype(jnp.int32)
    carry_ref[...] = total
    counts_ref[...] = total.astype(jnp.int32)


def _merge(x2, ypool, ymla, xq, gates, kmem, vmem, wbp, wbm, wbx, wout, ffng, wr, br, *, B, S, tm, mem_len):
    T = B * S
    tps = S // tm
    row = lambda i: (i, 0)
    lane = lambda i: (0, i)
    memb = lambda i: (i // tps, 0)
    return pl.pallas_call(
        functools.partial(_merge_body, tm=tm),
        grid=(T // tm,),
        in_specs=[
            pl.BlockSpec((tm, D_MODEL), row),
            pl.BlockSpec((tm, POOL_DIM), row),
            pl.BlockSpec((tm, MLA_HEADS * V_HEAD_DIM), row),
            pl.BlockSpec((tm, XATTN_DIM), row),
            pl.BlockSpec((tm, N_BRANCHES * D_MODEL), row),
            pl.BlockSpec((mem_len, XATTN_DIM), memb),
            pl.BlockSpec((mem_len, XATTN_DIM), memb),
            _const_spec((POOL_DIM, D_MODEL)),
            _const_spec((MLA_HEADS * V_HEAD_DIM, D_MODEL)),
            _const_spec((XATTN_DIM, D_MODEL)),
            _const_spec((D_MODEL, D_MODEL)),
            _const_spec((1, D_MODEL)),
            _const_spec((ROUTER_ROWS, D_MODEL)),
            _const_spec((ROUTER_ROWS, 1)),
        ],
        out_specs=[
            pl.BlockSpec((tm, D_MODEL), row),
            pl.BlockSpec((tm, D_MODEL), row),
            pl.BlockSpec((2, tm), lane),
            pl.BlockSpec((2, tm), lane),
            pl.BlockSpec((N_EXPERTS, LANES), row),
            pl.BlockSpec((N_EXPERTS, LANES), row),
            pl.BlockSpec((N_EXPERTS, LANES), lambda i: (0, 0)),
        ],
        out_shape=[
            jax.ShapeDtypeStruct((T, D_MODEL), F32),
            jax.ShapeDtypeStruct((T, D_MODEL), BF16),
            jax.ShapeDtypeStruct((2, T), F32),
            jax.ShapeDtypeStruct((2, T), jnp.int32),
            jax.ShapeDtypeStruct((T // tm * N_EXPERTS, LANES), jnp.int32),
            jax.ShapeDtypeStruct((T // tm * N_EXPERTS, LANES), jnp.int32),
            jax.ShapeDtypeStruct((N_EXPERTS, LANES), jnp.int32),
        ],
        scratch_shapes=[pltpu.VMEM((N_EXPERTS, LANES), F32)],
        compiler_params=pltpu.CompilerParams(dimension_semantics=("arbitrary",),
                                             vmem_limit_bytes=VMEM_LIMIT_BYTES),
        name="merge",
    )(x2, ypool, ymla, xq, gates, kmem, vmem, wbp, wbm, wbx, wout, ffng, wr, br)


def _moe_pos_body(counts_ref, meta_ref):
    shift = int(math.log2(MOE_ROWS))
    cnt = counts_ref[...]
    padded = lax.shift_left(lax.shift_right_logical(cnt + (MOE_ROWS - 1), shift), shift)
    r32 = lax.broadcasted_iota(jnp.int32, (N_EXPERTS, LANES), 0)
    pad_start = jnp.zeros((N_EXPERTS, LANES), jnp.int32)
    for e in range(N_EXPERTS - 1):
        pad_start = pad_start + jnp.where(r32 > e, padded[e:e + 1, :], 0)
    pad_end = pad_start + padded

    lane = lax.broadcasted_iota(jnp.int32, (1, META_LANES), 1)
    block_row = lane * MOE_ROWS
    blk_e = jnp.zeros((1, META_LANES), jnp.int32)
    pe_row = jnp.zeros((1, META_LANES), jnp.int32)
    for e in range(N_EXPERTS):
        pe = pad_end[e:e + 1, 0:1]
        blk_e = blk_e + jnp.where(pe <= block_row, 1, 0)
        pe_row = pe_row + jnp.where(lane == META_PAD_END + e, pe, 0)
    blk_e = jnp.minimum(blk_e, N_EXPERTS - 1)
    nact = lax.shift_right_logical(pad_end[N_EXPERTS - 1:N_EXPERTS, 0:1], shift)
    meta = jnp.where(lane < META_PAD_END, blk_e, pe_row)
    meta_ref[...] = jnp.where(lane == META_NACT, nact, meta)


def _moe_pos(counts):
    full = lambda shape: pl.BlockSpec(shape, lambda i: (0,) * len(shape))
    return pl.pallas_call(
        _moe_pos_body,
        grid=(1,),
        in_specs=[full((N_EXPERTS, LANES))],
        out_specs=full((1, META_LANES)),
        out_shape=jax.ShapeDtypeStruct((1, META_LANES), jnp.int32),
        compiler_params=pltpu.CompilerParams(dimension_semantics=("arbitrary",)),
        name="moe_pos",
    )(counts)


def _pack_bf16_pairs(x):
    lo = pltpu.bitcast(x[:, 0:PACKED_DIM], U32)
    hi = pltpu.bitcast(x[:, PACKED_DIM:D_MODEL], U32)
    return hi | lax.shift_right_logical(lo, jnp.uint32(16))


def _unpack_bf16_pairs(w):
    lo = pltpu.bitcast(lax.shift_left(w, jnp.uint32(16)), F32)
    hi = pltpu.bitcast(w & jnp.uint32(0xFFFF0000), F32)
    return jnp.concatenate([lo, hi], axis=1)


def _loc_rows(tm):
    bf16_rows = 2 * RUN_ALIGN
    return pl.cdiv(2 * tm + N_EXPERTS * (RUN_ALIGN - 1), bf16_rows) * bf16_rows


def _run_copies(tile, cnt_ref, carry_ref, meta_ref, make_copy):
    def per_expert(e, local, priority):
        n = pl.multiple_of(cnt_ref[tile * N_EXPERTS + e], RUN_ALIGN)
        start = jnp.where(e == 0, 0, meta_ref[META_PAD_END + jnp.maximum(e - 1, 0)])
        glob = pl.multiple_of(start + carry_ref[tile * N_EXPERTS + e], RUN_ALIGN)

        @pl.when(n > 0)
        def _():
            make_copy(pl.multiple_of(local, RUN_ALIGN), glob, n).start(priority=priority)

        return local + n

    def expert_pair(i, local):
        return per_expert(2 * i + 1, per_expert(2 * i, local, 0), 1)

    return pl.multiple_of(lax.fori_loop(0, N_EXPERTS // 2, expert_pair, 0), RUN_ALIGN)


def _tile_rows(tile, cnt_ref):
    total = lax.fori_loop(0, N_EXPERTS, lambda e, t: t + cnt_ref[tile * N_EXPERTS + e], 0)
    return pl.multiple_of(total, RUN_ALIGN)


def _dispatch_body(meta_ref, cnt_ref, carry_ref, loc_ref, h2_ref, xs_ref, xloc_ref, zero_ref, sems, zsem,
                   *, tm, n_blocks):
    tile = pl.program_id(0)
    last_tile = pl.num_programs(0) - 1
    slot = lax.rem(tile, 2)
    nact = meta_ref[META_NACT]

    def wait_rows(t, s):
        n = _tile_rows(t, cnt_ref)
        pltpu.make_async_copy(xloc_ref.at[s, pl.ds(0, n)], xs_ref.at[pl.ds(0, n)], sems.at[s]).wait()

    def pad_copy(e):
        end = pl.multiple_of(meta_ref[META_PAD_END + e], MOE_ROWS)
        start = jnp.where(e == 0, 0, meta_ref[META_PAD_END + jnp.maximum(e - 1, 0)])
        used = carry_ref[last_tile * N_EXPERTS + e] + cnt_ref[last_tile * N_EXPERTS + e]
        first = pl.multiple_of(start + used, RUN_ALIGN)
        n = pl.multiple_of(end - first, RUN_ALIGN)
        return n, pltpu.make_async_copy(zero_ref.at[pl.ds(0, n)], xs_ref.at[pl.ds(first, n)], zsem)

    def tail_copy(b):
        return pltpu.make_async_copy(
            zero_ref, xs_ref.at[pl.ds(pl.multiple_of(b * MOE_ROWS, MOE_ROWS), MOE_ROWS)], zsem)

    def fill(op):
        def pad(e, c):
            n, cp = pad_copy(e)

            @pl.when(n > 0)
            def _():
                op(cp)
            return c

        def tail(b, c):
            op(tail_copy(b))
            return c

        lax.fori_loop(0, N_EXPERTS, pad, 0)
        lax.fori_loop(nact, n_blocks, tail, 0)

    @pl.when(tile == 0)
    def _():
        zero_ref[...] = jnp.zeros((MOE_ROWS, PACKED_DIM), U32)
        fill(lambda cp: cp.start())

    @pl.when(tile >= 2)
    def _():
        wait_rows(tile - 2, slot)

    r = lax.broadcasted_iota(jnp.int32, (_loc_rows(tm), tm), 0)
    onehot = jnp.where((r == loc_ref[0:1, :]) | (r == loc_ref[1:2, :]), 1.0, 0.0).astype(BF16)
    xloc_ref[slot] = _pack_bf16_pairs(_dot(onehot, h2_ref[...]))

    def make_copy(local, glob, n):
        return pltpu.make_async_copy(xloc_ref.at[slot, pl.ds(local, n)], xs_ref.at[pl.ds(glob, n)],
                                     sems.at[slot])

    _run_copies(tile, cnt_ref, carry_ref, meta_ref, make_copy)

    @pl.when(tile == last_tile)
    def _():
        @pl.when(tile >= 1)
        def _():
            wait_rows(tile - 1, 1 - slot)

        wait_rows(tile, slot)
        fill(lambda cp: cp.wait())


def _dispatch(meta1, cnt_tab, carry_tab, loc, h2, *, R, tm):
    T = h2.shape[0]
    return pl.pallas_call(
        functools.partial(_dispatch_body, tm=tm, n_blocks=R // MOE_ROWS),
        grid_spec=pltpu.PrefetchScalarGridSpec(
            num_scalar_prefetch=3,
            grid=(T // tm,),
            in_specs=[pl.BlockSpec((2, tm), lambda i, *_: (0, i)),
                      pl.BlockSpec((tm, D_MODEL), lambda i, *_: (i, 0))],
            out_specs=pl.BlockSpec(memory_space=pl.ANY),
            scratch_shapes=[pltpu.VMEM((2, _loc_rows(tm), PACKED_DIM), U32),
                            pltpu.VMEM((MOE_ROWS, PACKED_DIM), U32),
                            pltpu.SemaphoreType.DMA((2,)), pltpu.SemaphoreType.DMA],
        ),
        out_shape=jax.ShapeDtypeStruct((R, PACKED_DIM), U32),
        compiler_params=pltpu.CompilerParams(dimension_semantics=("arbitrary",),
                                             vmem_limit_bytes=VMEM_LIMIT_BYTES),
        name="dispatch",
    )(meta1, cnt_tab, carry_tab, loc, h2)


def _moe_ffn_body(meta_ref, xs_ref, wg_hbm, wu_hbm, wd_hbm, ys_ref,
                  xbuf, ybuf, wg_stage, wu_stage, wd_stage, wg_b, wu_b, wd_b, zero_ref,
                  xsem, ysem, wsem, zsem, *, n_blocks):
    nact = meta_ref[META_NACT]
    shift = int(math.log2(MOE_ROWS))

    def rows_of(b):
        return pl.ds(pl.multiple_of(b * MOE_ROWS, MOE_ROWS), MOE_ROWS)

    def x_copy(b, s):
        return pltpu.make_async_copy(xs_ref.at[rows_of(b)], xbuf.at[s], xsem.at[s])

    def y_copy(b, s):
        return pltpu.make_async_copy(ybuf.at[s], ys_ref.at[rows_of(b)], ysem.at[s])

    def w_copies(e, s):
        return (pltpu.make_async_copy(wg_hbm.at[e], wg_stage.at[s], wsem.at[s]),
                pltpu.make_async_copy(wu_hbm.at[e], wu_stage.at[s], wsem.at[s]),
                pltpu.make_async_copy(wd_hbm.at[e], wd_stage.at[s], wsem.at[s]))

    def tail_copy(b):
        return pltpu.make_async_copy(zero_ref, ys_ref.at[rows_of(b)], zsem)

    zero_ref[...] = jnp.zeros((MOE_ROWS, PACKED_DIM), U32)
    lax.fori_loop(nact, n_blocks, lambda b, c: (tail_copy(b).start(), c)[1], 0)

    def next_expert_block(e):
        return lax.shift_right_logical(meta_ref[META_PAD_END + e], shift)

    def start_weights(b, s):
        @pl.when(b < nact)
        def _():
            for cp in w_copies(meta_ref[jnp.minimum(b, n_blocks - 1)], s):
                cp.start()

    x_copy(0, 0).start()

    @pl.when(nact > 1)
    def _():
        x_copy(1, 1).start()

    e_first = meta_ref[0]
    start_weights(0, 0)
    start_weights(next_expert_block(e_first), 1)

    def block(b, xs, ys, k_prev):
        valid = b < nact
        e = meta_ref[jnp.minimum(b, nact - 1)]
        changed = jnp.logical_and(valid, jnp.logical_or(b == 0, e != meta_ref[jnp.maximum(b - 1, 0)]))
        k = jnp.where(changed, k_prev + 1, k_prev)

        @pl.when(changed)
        def _():
            ws = lax.rem(k, W_STAGES)
            for cp in w_copies(e, ws):
                cp.wait()
            wg_b[...] = wg_stage[ws].astype(BF16)
            wu_b[...] = wu_stage[ws].astype(BF16)
            wd_b[...] = wd_stage[ws].astype(BF16)
            n1 = next_expert_block(e)
            e1 = meta_ref[jnp.minimum(n1, n_blocks - 1)]
            n2 = jnp.where(n1 < nact, next_expert_block(e1), n_blocks)
            start_weights(n2, lax.rem(k + 2, W_STAGES))

        @pl.when(valid)
        def _():
            x_copy(b, xs).wait()

            @pl.when(b + 2 < nact)
            def _():
                x_copy(b + 2, (xs + 2) % X_SLOTS).start()

            @pl.when(b >= 2)
            def _():
                y_copy(b - 2, ys).wait()

            x = _unpack_bf16_pairs(xbuf[xs]).astype(BF16)
            g = _dot(x, wg_b[...])
            a = (g * jax.nn.sigmoid(g)) * _dot(x, wu_b[...])
            y = _dot(a.astype(BF16), wd_b[...])
            ybuf[ys] = _pack_bf16_pairs(y.astype(BF16).astype(F32))
            y_copy(b, ys).start()

        return k

    def quad(i, k):
        for j in range(X_SLOTS):
            k = block(X_SLOTS * i + j, j, j % 2, k)
        return k

    lax.fori_loop(0, lax.div(nact + (X_SLOTS - 1), X_SLOTS), quad, -1)

    @pl.when(nact >= 2)
    def _():
        y_copy(nact - 2, lax.rem(nact, 2)).wait()

    y_copy(nact - 1, lax.rem(nact - 1, 2)).wait()
    lax.fori_loop(nact, n_blocks, lambda b, c: (tail_copy(b).wait(), c)[1], 0)


def _moe_ffn(meta1, xs, wg, wu, wd):
    R = xs.shape[0]
    hbm = pl.BlockSpec(memory_space=pl.ANY)
    return pl.pallas_call(
        functools.partial(_moe_ffn_body, n_blocks=R // MOE_ROWS),
        grid_spec=pltpu.PrefetchScalarGridSpec(
            num_scalar_prefetch=1,
            grid=(1,),
            in_specs=[hbm, hbm, hbm, hbm],
            out_specs=hbm,
            scratch_shapes=[pltpu.VMEM((X_SLOTS, MOE_ROWS, PACKED_DIM), U32),
                            pltpu.VMEM((2, MOE_ROWS, PACKED_DIM), U32),
                            pltpu.VMEM((W_STAGES, D_MODEL, D_EXPERT), F32),
                            pltpu.VMEM((W_STAGES, D_MODEL, D_EXPERT), F32),
                            pltpu.VMEM((W_STAGES, D_EXPERT, D_MODEL), F32),
                            pltpu.VMEM((D_MODEL, D_EXPERT), BF16),
                            pltpu.VMEM((D_MODEL, D_EXPERT), BF16),
                            pltpu.VMEM((D_EXPERT, D_MODEL), BF16),
                            pltpu.VMEM((MOE_ROWS, PACKED_DIM), U32),
                            pltpu.SemaphoreType.DMA((X_SLOTS,)), pltpu.SemaphoreType.DMA((2,)),
                            pltpu.SemaphoreType.DMA((W_STAGES,)), pltpu.SemaphoreType.DMA],
        ),
        out_shape=jax.ShapeDtypeStruct((R, PACKED_DIM), U32),
        compiler_params=pltpu.CompilerParams(dimension_semantics=("arbitrary",),
                                             vmem_limit_bytes=VMEM_LIMIT_BYTES),
        name="moe_ffn",
    )(meta1, xs, wg, wu, wd)


def _combine_body(meta_ref, cnt_ref, carry_ref, x1_ref, loc_ref, wts_ref, fg_ref, ys_ref, out_ref,
                  yloc_ref, sems, *, tm):
    tile = pl.program_id(0)
    slot = lax.rem(tile, 2)

    def fetch(t, s):
        def make_copy(local, glob, n):
            return pltpu.make_async_copy(ys_ref.at[pl.ds(glob, n)], yloc_ref.at[s, pl.ds(local, n)],
                                         sems.at[s])
        _run_copies(t, cnt_ref, carry_ref, meta_ref, make_copy)

    @pl.when(tile == 0)
    def _():
        yloc_ref[...] = jnp.zeros(yloc_ref.shape, U32)
        fetch(tile, slot)

    @pl.when(tile + 1 < pl.num_programs(0))
    def _():
        fetch(tile + 1, 1 - slot)

    n = _tile_rows(tile, cnt_ref)
    pltpu.make_async_copy(ys_ref.at[pl.ds(0, n)], yloc_ref.at[slot, pl.ds(0, n)], sems.at[slot]).wait()
    r = lax.broadcasted_iota(jnp.int32, (_loc_rows(tm), tm), 0)
    is0 = r == loc_ref[0:1, :]
    is1 = r == loc_ref[1:2, :]
    row_w = jnp.sum(jnp.where(is0, wts_ref[0:1, :], 0.0) + jnp.where(is1, wts_ref[1:2, :], 0.0),
                    axis=1, keepdims=True)
    yw = (row_w * _unpack_bf16_pairs(yloc_ref[slot])).astype(BF16)
    twohot = jnp.where(is0 | is1, 1.0, 0.0).astype(BF16)
    moe = lax.dot_general(twohot, yw, (((0,), (0,)), ((), ())), preferred_element_type=F32)
    out_ref[...] = _rms(x1_ref[...] + moe, fg_ref[...])


def _combine(meta1, cnt_tab, carry_tab, x1, loc, wts, fg, ys, *, tm):
    T = x1.shape[0]
    row = lambda i, *_: (i, 0)
    lane = lambda i, *_: (0, i)
    return pl.pallas_call(
        functools.partial(_combine_body, tm=tm),
        grid_spec=pltpu.PrefetchScalarGridSpec(
            num_scalar_prefetch=3,
            grid=(T // tm,),
            in_specs=[pl.BlockSpec((tm, D_MODEL), row),
                      pl.BlockSpec((2, tm), lane),
                      pl.BlockSpec((2, tm), lane),
                      pl.BlockSpec((1, D_MODEL), lambda i, *_: (0, 0)),
                      pl.BlockSpec(memory_space=pl.ANY)],
            out_specs=pl.BlockSpec((tm, D_MODEL), row),
            scratch_shapes=[pltpu.VMEM((2, _loc_rows(tm), PACKED_DIM), U32),
                            pltpu.SemaphoreType.DMA((2,))],
        ),
        out_shape=jax.ShapeDtypeStruct((T, D_MODEL), F32),
        compiler_params=pltpu.CompilerParams(dimension_semantics=("arbitrary",),
                                             vmem_limit_bytes=VMEM_LIMIT_BYTES),
        name="combine",
    )(meta1, cnt_tab, carry_tab, x1, loc, wts, fg, ys)


def _tile(n, t):
    t = min(n, t)
    assert n % t == 0, (n, t)
    return t


def kernel(x, mem, positions, mix_norm_g, w_in, gate_b, q_norm_g, w_uq, kv_norm_g, w_uk, w_uv, pool_w, pool_scale, mem_norm_g, w_mem_kv, w_br_pool, w_br_mla, w_br_mem, w_out, ffn_norm_g, w_router_group, b_router_group, w_router_expert, b_router_expert, w_gate_e, w_up_e, w_down_e, final_norm_g):
    B, S, D = x.shape
    assert D == D_MODEL and mix_norm_g.shape[0] == 1
    T = B * S
    mem_len = mem.shape[1]
    tm = _tile(S, 512)
    l = 0

    win_p = _pack_w_in(jnp.swapaxes(w_in, 1, 2).reshape(W_IN_END, D_MODEL))
    wuq_p = jnp.pad(w_uq[l].reshape(Q_LORA_RANK, MLA_HEADS, QK_NOPE_DIM + QK_ROPE_DIM),
                    ((0, 0), (0, 0), (0, QK_PAD_DIM - QK_NOPE_DIM - QK_ROPE_DIM))
                    ).reshape(Q_LORA_RANK, MLA_HEADS * QK_PAD_DIM).astype(BF16)
    inv_freq = 1.0 / (ROPE_THETA ** (jnp.arange(0, QK_ROPE_DIM, 2, dtype=F32) / QK_ROPE_DIM))
    invf = jnp.concatenate([inv_freq, inv_freq, jnp.zeros((LANES - QK_ROPE_DIM,), F32)])[None, :]
    wr = jnp.concatenate([w_router_expert[l], w_router_group[l],
                          jnp.zeros((D_MODEL, ROUTER_ROWS - N_EXPERTS - N_GROUPS), F32)], axis=1).T.astype(BF16)
    br = jnp.concatenate([b_router_expert[l], b_router_group[l],
                          jnp.zeros((ROUTER_ROWS - N_EXPERTS - N_GROUPS,), F32)])[:, None].astype(F32)
    x2 = x.reshape(T, D_MODEL)
    pos2 = positions.reshape(T, 1)

    ypool, xq, gates, q, k, v = _mixer_in(
        x2, pos2, invf, mix_norm_g[l][None, :], win_p, gate_b[l], q_norm_g[l][None, :], wuq_p,
        kv_norm_g[l][None, :], w_uk[l].astype(BF16), w_uv[l].astype(BF16), pool_w[l].astype(BF16),
        pool_scale[l][None, :], B=B, S=S, tm=tm)
    kmem, vmem = _mem_kv(mem.reshape(B * mem_len, D_MODEL), mem_norm_g[l][None, :], w_mem_kv[l].astype(BF16))
    ymla = _mla_attn_unrolled(q, k, v, tq=tm).reshape(T, MLA_HEADS * V_HEAD_DIM)
    x1, h2, wts, loc, cnt_tab, carry_tab, counts = _merge(
        x2, ypool, ymla, xq, gates, kmem, vmem, w_br_pool[l].astype(BF16), w_br_mla[l].astype(BF16),
        w_br_mem[l].astype(BF16), w_out[l].astype(BF16), ffn_norm_g[l][None, :], wr, br,
        B=B, S=S, tm=tm, mem_len=mem_len)

    R = 2 * T + (T // tm) * N_EXPERTS * RUN_ALIGN + N_EXPERTS * MOE_ROWS
    assert R % MOE_ROWS == 0 and R // MOE_ROWS <= META_PAD_END
    meta1 = _moe_pos(counts).reshape(META_LANES)
    cnt1 = cnt_tab[:, 0]
    carry1 = carry_tab[:, 0]
    xs = _dispatch(meta1, cnt1, carry1, loc, h2, R=R, tm=tm)
    ys = _moe_ffn(meta1, xs, w_gate_e[l], w_up_e[l], w_down_e[l])
    out = _combine(meta1, cnt1, carry1, x1, loc, wts, final_norm_g[None, :], ys, tm=tm)
    return out.reshape(B, S, D_MODEL)
```

```python
import functools
import math

import jax
import jax.numpy as jnp
from jax import lax
from jax.experimental import pallas as pl
from jax.experimental.pallas import tpu as pltpu

D_MODEL = 1024
POOL_WINDOWS = (2, 4, 8, 16)
POOL_GROUP_DIM = 128
POOL_DIM = 512
MLA_HEADS = 8
QK_NOPE_DIM = 128
QK_ROPE_DIM = 64
V_HEAD_DIM = 128
Q_LORA_RANK = 384
KV_LORA_RANK = 256
ROPE_THETA = 10000.0
XATTN_HEADS = 4
XATTN_HEAD_DIM = 128
XATTN_DIM = 512
N_BRANCHES = 3
N_GROUPS = 4
EXPERTS_PER_GROUP = 8
N_EXPERTS = 32
D_EXPERT = 256
RMS_EPS = 1e-6
NEG_INF = -1e30

LANES = 128
QK_PAD_DIM = 2 * LANES
POOL_HALO = 16
MOE_ROWS = 512
X_SLOTS = 4
W_STAGES = 3
RUN_ALIGN = 8
PACKED_DIM = D_MODEL // 2
ROUTER_ROWS = 40
META_LANES = 256
META_PAD_END = 192
META_NACT = 255
VMEM_LIMIT_BYTES = 56 * 1024 * 1024

IN_POOL, IN_QD, IN_KV, IN_XQ, IN_GATE, IN_KR, IN_END = 0, 512, 896, 1152, 1664, 4736, 4864
W_IN_KR, W_IN_XQ, W_IN_END = 1152, 1216, 4800

F32 = jnp.float32
BF16 = jnp.bfloat16
U32 = jnp.uint32


def _rms(x, g):
    ms = jnp.mean(x * x, axis=-1, keepdims=True)
    return (x * lax.rsqrt(ms + RMS_EPS)) * g


def _dot(a, b):
    return jnp.dot(a, b, preferred_element_type=F32)


def _dot_nt(a, b):
    return lax.dot_general(a, b, (((1,), (1,)), ((), ())), preferred_element_type=F32)


def _const_spec(shape):
    nd = len(shape)
    return pl.BlockSpec(shape, lambda *_: (0,) * nd, pipeline_mode=pl.Buffered(1))


def _pack_w_in_body(wt_ref, o_ref):
    chunk = LANES

    def copy_cols(dst, src, n):
        for r in range(0, n, chunk):
            m = min(chunk, n - r)
            o_ref[:, dst + r:dst + r + m] = wt_ref[src + r:src + r + m, :].T.astype(BF16)

    copy_cols(0, 0, W_IN_KR)
    copy_cols(IN_XQ, W_IN_XQ, W_IN_END - W_IN_XQ)
    copy_cols(IN_KR, W_IN_KR, W_IN_XQ - W_IN_KR)
    o_ref[:, IN_KR + QK_ROPE_DIM:IN_END] = jnp.zeros((D_MODEL, LANES - QK_ROPE_DIM), BF16)


def _pack_w_in(w_t):
    whole = pl.BlockSpec(memory_space=pltpu.VMEM)
    return pl.pallas_call(
        _pack_w_in_body,
        in_specs=[whole],
        out_specs=whole,
        out_shape=jax.ShapeDtypeStruct((D_MODEL, IN_END), BF16),
        compiler_params=pltpu.CompilerParams(vmem_limit_bytes=VMEM_LIMIT_BYTES),
        name="pack_w_in",
    )(w_t)


def _mixer_in_body(x_ref, pos_ref, invf_ref, mixg_ref, win_ref, gateb_ref, qg_ref, wuq_ref,
                   kvg_ref, wuk_ref, wuv_ref, poolw_ref, pools_ref,
                   ypool_ref, xq_ref, gates_ref, q_ref, k_ref, v_ref, ext_ref,
                   *, tm, tiles_per_seq, q_scale):
    si = lax.rem(pl.program_id(0), tiles_per_seq)

    @pl.when(pl.program_id(0) == 0)
    def _():
        ext_ref[0:POOL_HALO, :] = jnp.zeros((POOL_HALO, POOL_DIM), F32)

    hb = _rms(x_ref[...], mixg_ref[...]).astype(BF16)

    u = _dot(hb, win_ref[:, IN_POOL:IN_QD])
    ext_ref[0:POOL_HALO, :] = jnp.where(si == 0, 0.0, ext_ref[0:POOL_HALO, :])
    ext_ref[POOL_HALO:POOL_HALO + tm, :] = u

    for c in range(N_BRANCHES):
        gl = _dot(hb, win_ref[:, IN_GATE + c * D_MODEL:IN_GATE + (c + 1) * D_MODEL])
        gates_ref[:, c * D_MODEL:(c + 1) * D_MODEL] = jax.nn.sigmoid(gl + gateb_ref[c:c + 1, :]).astype(BF16)

    ang = pos_ref[...].astype(F32) * invf_ref[...]
    cos = jnp.cos(ang)
    sin = jnp.sin(ang)
    first_half = lax.broadcasted_iota(jnp.int32, (tm, LANES), 1) < (QK_ROPE_DIM // 2)
    sin_signed = jnp.where(first_half, -sin, sin)

    def rope(r):
        swapped = jnp.where(first_half, pltpu.roll(r, LANES - QK_ROPE_DIM // 2, 1),
                            pltpu.roll(r, QK_ROPE_DIM // 2, 1))
        return r * cos + swapped * sin_signed

    cq = _rms(_dot(hb, win_ref[:, IN_QD:IN_KV]), qg_ref[...]).astype(BF16)
    for h in range(MLA_HEADS):
        qh = _dot(cq, wuq_ref[:, h * QK_PAD_DIM:(h + 1) * QK_PAD_DIM])
        q_ref[0, h, :, 0:LANES] = (qh[:, 0:LANES] * q_scale).astype(BF16)
        q_ref[0, h, :, LANES:QK_PAD_DIM] = (rope(qh[:, LANES:QK_PAD_DIM]) * q_scale).astype(BF16)

    ckv = _rms(_dot(hb, win_ref[:, IN_KV:IN_XQ]), kvg_ref[...]).astype(BF16)
    kr = rope(_dot(hb, win_ref[:, IN_KR:IN_END])).astype(BF16)
    for hp in range(MLA_HEADS // 2):
        cols = slice(hp * 2 * LANES, (hp + 1) * 2 * LANES)
        kn = _dot(ckv, wuk_ref[:, cols]).astype(BF16)
        vv = _dot(ckv, wuv_ref[:, cols]).astype(BF16)
        for j in range(2):
            h = 2 * hp + j
            k_ref[0, h, :, 0:LANES] = kn[:, j * LANES:(j + 1) * LANES]
            k_ref[0, h, :, LANES:QK_PAD_DIM] = kr
            v_ref[0, h] = vv[:, j * LANES:(j + 1) * LANES]

    t_seq = lax.broadcasted_iota(jnp.int32, (tm, 1), 0) + si * tm
    for g, w in enumerate(POOL_WINDOWS):
        lo = g * POOL_GROUP_DIM
        hi = lo + POOL_GROUP_DIM
        acc = u[:, lo:hi]
        for j in range(1, w):
            acc = acc + ext_ref[POOL_HALO - j:POOL_HALO - j + tm, lo:hi]
        cnt = jnp.minimum(t_seq + 1, w).astype(F32)
        p = acc / cnt - u[:, lo:hi]
        y = _dot(p.astype(BF16), poolw_ref[g]) * pools_ref[:, lo:hi]
        ypool_ref[:, lo:hi] = y.astype(BF16)
    ext_ref[0:POOL_HALO, :] = ext_ref[tm:tm + POOL_HALO, :]

    xq_ref[...] = _dot(hb, win_ref[:, IN_XQ:IN_GATE]).astype(BF16)


def _mixer_in(x2, pos2, invf, mixg, win_p, gate_b, qg, wuq_p, kvg, wuk, wuv, pool_w, pool_s, *, B, S, tm):
    T = B * S
    tps = S // tm
    q_scale = (QK_NOPE_DIM + QK_ROPE_DIM) ** -0.5 * math.log2(math.e)
    body = functools.partial(_mixer_in_body, tm=tm, tiles_per_seq=tps, q_scale=q_scale)
    row = lambda i: (i, 0)
    head = lambda i: (i // tps, 0, i % tps, 0)
    return pl.pallas_call(
        body,
        grid=(T // tm,),
        in_specs=[
            pl.BlockSpec((tm, D_MODEL), row),
            pl.BlockSpec((tm, 1), row),
            _const_spec((1, LANES)),
            _const_spec((1, D_MODEL)),
            _const_spec((D_MODEL, IN_END)),
            _const_spec((N_BRANCHES, D_MODEL)),
            _const_spec((1, Q_LORA_RANK)),
            _const_spec((Q_LORA_RANK, MLA_HEADS * QK_PAD_DIM)),
            _const_spec((1, KV_LORA_RANK)),
            _const_spec((KV_LORA_RANK, MLA_HEADS * QK_NOPE_DIM)),
            _const_spec((KV_LORA_RANK, MLA_HEADS * V_HEAD_DIM)),
            _const_spec((len(POOL_WINDOWS), POOL_GROUP_DIM, POOL_GROUP_DIM)),
            _const_spec((1, POOL_DIM)),
        ],
        out_specs=[
            pl.BlockSpec((tm, POOL_DIM), row),
            pl.BlockSpec((tm, XATTN_DIM), row),
            pl.BlockSpec((tm, N_BRANCHES * D_MODEL), row),
            pl.BlockSpec((1, MLA_HEADS, tm, QK_PAD_DIM), head),
            pl.BlockSpec((1, MLA_HEADS, tm, QK_PAD_DIM), head),
            pl.BlockSpec((1, MLA_HEADS, tm, V_HEAD_DIM), head),
        ],
        out_shape=[
            jax.ShapeDtypeStruct((T, POOL_DIM), BF16),
            jax.ShapeDtypeStruct((T, XATTN_DIM), BF16),
            jax.ShapeDtypeStruct((T, N_BRANCHES * D_MODEL), BF16),
            jax.ShapeDtypeStruct((B, MLA_HEADS, S, QK_PAD_DIM), BF16),
            jax.ShapeDtypeStruct((B, MLA_HEADS, S, QK_PAD_DIM), BF16),
            jax.ShapeDtypeStruct((B, MLA_HEADS, S, V_HEAD_DIM), BF16),
        ],
        scratch_shapes=[pltpu.VMEM((tm + POOL_HALO, POOL_DIM), F32)],
        compiler_params=pltpu.CompilerParams(dimension_semantics=("arbitrary",),
                                             vmem_limit_bytes=VMEM_LIMIT_BYTES),
        name="mixer_in",
    )(x2, pos2, invf, mixg, win_p, gate_b, qg, wuq_p, kvg, wuk, wuv, pool_w, pool_s)


def _mem_kv_body(mem_ref, g_ref, w_ref, k_ref, v_ref):
    mb = _rms(mem_ref[...], g_ref[...]).astype(BF16)
    kv = _dot(mb, w_ref[...])
    k_ref[...] = kv[:, 0:XATTN_DIM].astype(BF16)
    v_ref[...] = kv[:, XATTN_DIM:2 * XATTN_DIM].astype(BF16)


def _mem_kv(mem2, g, w):
    rows = mem2.shape[0]
    tr = min(rows, 512)
    return pl.pallas_call(
        _mem_kv_body,
        grid=(rows // tr,),
        in_specs=[pl.BlockSpec((tr, D_MODEL), lambda i: (i, 0)),
                  _const_spec((1, D_MODEL)),
                  _const_spec((D_MODEL, 2 * XATTN_DIM))],
        out_specs=[pl.BlockSpec((tr, XATTN_DIM), lambda i: (i, 0)),
                   pl.BlockSpec((tr, XATTN_DIM), lambda i: (i, 0))],
        out_shape=[jax.ShapeDtypeStruct((rows, XATTN_DIM), BF16),
                   jax.ShapeDtypeStruct((rows, XATTN_DIM), BF16)],
        compiler_params=pltpu.CompilerParams(dimension_semantics=("arbitrary",)),
        name="mem_kv",
    )(mem2, g, w)


def _attn_unrolled_body(q_ref, k_ref, v_ref, o_ref, s_a, s_b, mc_a, mc_b, m_ref, l_ref, acc_ref, *, nq, tq):
    s_bufs = (s_a, s_b)
    mc_bufs = (mc_a, mc_b)
    half = tq // 2
    blocks = [(qi, kb) for qi in range(nq) for kb in range(qi + 1)]

    def key_width(qi, kb, r):
        return half if (qi == kb and r == 0) else tq

    def scores(i, slot):
        qi, kb = blocks[i]
        for r in range(2):
            w = key_width(qi, kb, r)
            rows = slice(r * half, (r + 1) * half)
            s = _dot_nt(q_ref[0, 0, qi * tq + r * half:qi * tq + (r + 1) * half, :],
                        k_ref[0, 0, kb * tq:kb * tq + w, :])
            if qi == kb:
                ri = lax.broadcasted_iota(jnp.int32, (half, w), 0) + r * half
                ci = lax.broadcasted_iota(jnp.int32, (half, w), 1)
                s = jnp.where(ci <= ri, s, NEG_INF)
            s_bufs[slot][rows, 0:w] = s
            mc_bufs[slot][rows, :] = jnp.broadcast_to(jnp.max(s, axis=1, keepdims=True), (half, LANES))

    def accumulate(i, slot):
        qi, kb = blocks[i]
        is_first = kb == 0
        is_last = kb == qi
        for r in range(2):
            w = key_width(qi, kb, r)
            rows = slice(r * half, (r + 1) * half)
            if is_first:
                m_new = mc_bufs[slot][rows, :]
            else:
                m_prev = m_ref[rows, :]
                m_new = jnp.maximum(m_prev, mc_bufs[slot][rows, :])
                alpha = jnp.exp2(m_prev - m_new)
            p = jnp.exp2(s_bufs[slot][rows, 0:w] - jnp.concatenate([m_new] * (w // LANES), axis=1))
            psum = p[:, 0:LANES]
            for c in range(1, w // LANES):
                psum = psum + p[:, c * LANES:(c + 1) * LANES]
            l_new = psum if is_first else alpha * l_ref[rows, :] + psum
            acc = _dot(p.astype(BF16), v_ref[0, 0, kb * tq:kb * tq + w, :])
            if not is_first:
                acc = alpha * acc_ref[rows, :] + acc
            if is_last:
                inv = 1.0 / jnp.sum(l_new, axis=1, keepdims=True)
                o_ref[0, qi * tq + r * half:qi * tq + (r + 1) * half, :] = (acc * inv).astype(BF16)
            else:
                l_ref[rows, :] = l_new
                m_ref[rows, :] = m_new
                acc_ref[rows, :] = acc

    scores(0, 0)
    for i in range(len(blocks)):
        if i + 1 < len(blocks):
            scores(i + 1, (i + 1) % 2)
        accumulate(i, i % 2)


def _mla_attn_unrolled(q, k, v, *, tq):
    B, H, S, _ = q.shape
    per_head = lambda b, h: (b, h, 0, 0)
    return pl.pallas_call(
        functools.partial(_attn_unrolled_body, nq=S // tq, tq=tq),
        grid=(B, H),
        in_specs=[pl.BlockSpec((1, 1, S, QK_PAD_DIM), per_head),
                  pl.BlockSpec((1, 1, S, QK_PAD_DIM), per_head),
                  pl.BlockSpec((1, 1, S, V_HEAD_DIM), per_head)],
        out_specs=pl.BlockSpec((1, S, V_HEAD_DIM), lambda b, h: (b, 0, h)),
        out_shape=jax.ShapeDtypeStruct((B, S, H * V_HEAD_DIM), BF16),
        scratch_shapes=[pltpu.VMEM((tq, tq), F32), pltpu.VMEM((tq, tq), F32),
                        pltpu.VMEM((tq, LANES), F32), pltpu.VMEM((tq, LANES), F32),
                        pltpu.VMEM((tq, LANES), F32), pltpu.VMEM((tq, LANES), F32),
                        pltpu.VMEM((tq, V_HEAD_DIM), F32)],
        compiler_params=pltpu.CompilerParams(dimension_semantics=("arbitrary", "arbitrary"),
                                             vmem_limit_bytes=VMEM_LIMIT_BYTES),
        name="mla_attn",
    )(q, k, v)


def _merge_body(x_ref, ypool_ref, ymla_ref, xq_ref, gates_ref, kmem_ref, vmem_ref,
                wbp_ref, wbm_ref, wbx_ref, wout_ref, ffng_ref, wr_ref, br_ref,
                x1_ref, h2_ref, wts_ref, loc_ref, cnt_tab_ref, carry_tab_ref, counts_ref, carry_ref, *, tm):
    @pl.when(pl.program_id(0) == 0)
    def _():
        carry_ref[...] = jnp.zeros((N_EXPERTS, LANES), F32)

    xq = xq_ref[...]
    parts = []
    for h in range(XATTN_HEADS):
        cols = slice(h * XATTN_HEAD_DIM, (h + 1) * XATTN_HEAD_DIM)
        s = _dot_nt(xq[:, cols], kmem_ref[:, cols]) * (XATTN_HEAD_DIM ** -0.5)
        e = jnp.exp(s - jnp.max(s, axis=1, keepdims=True))
        p = e / jnp.sum(e, axis=1, keepdims=True)
        parts.append(_dot(p.astype(BF16), vmem_ref[:, cols]))
    ymem = jnp.concatenate(parts, axis=1).astype(BF16)

    gates = gates_ref[...].astype(F32)
    merged = (gates[:, 0:D_MODEL] * _dot(ypool_ref[...], wbp_ref[...])
              + gates[:, D_MODEL:2 * D_MODEL] * _dot(ymla_ref[...], wbm_ref[...])
              + gates[:, 2 * D_MODEL:3 * D_MODEL] * _dot(ymem, wbx_ref[...]))
    x1 = x_ref[...] + _dot(merged.astype(BF16), wout_ref[...])
    x1_ref[...] = x1
    h2 = _rms(x1, ffng_ref[...]).astype(BF16)
    h2_ref[...] = h2

    lt = _dot_nt(wr_ref[...], h2) + br_ref[...]
    gl = lt[N_EXPERTS:N_EXPERTS + N_GROUPS, :]
    gmax = jnp.max(gl, axis=0, keepdims=True)
    r4 = lax.broadcasted_iota(jnp.int32, (N_GROUPS, tm), 0).astype(F32)
    gidx = jnp.min(jnp.where(gl == gmax, r4, float(N_GROUPS)), axis=0, keepdims=True)
    pg = 1.0 / jnp.sum(jnp.exp(gl - gmax), axis=0, keepdims=True)
    esel = lt[0:EXPERTS_PER_GROUP, :]
    for g in range(1, N_GROUPS):
        esel = jnp.where(gidx == float(g), lt[g * EXPERTS_PER_GROUP:(g + 1) * EXPERTS_PER_GROUP, :], esel)
    r8 = lax.broadcasted_iota(jnp.int32, (EXPERTS_PER_GROUP, tm), 0).astype(F32)
    m1 = jnp.max(esel, axis=0, keepdims=True)
    i1 = jnp.min(jnp.where(esel == m1, r8, float(EXPERTS_PER_GROUP)), axis=0, keepdims=True)
    rest = jnp.where(r8 == i1, -jnp.inf, esel)
    m2 = jnp.max(rest, axis=0, keepdims=True)
    i2 = jnp.min(jnp.where(rest == m2, r8, float(EXPERTS_PER_GROUP)), axis=0, keepdims=True)
    e2 = jnp.exp(m2 - m1)
    den = 1.0 + e2
    wts_ref[0:1, :] = pg / den
    wts_ref[1:2, :] = pg * e2 / den
    ex1 = gidx * float(EXPERTS_PER_GROUP) + i1
    ex2 = gidx * float(EXPERTS_PER_GROUP) + i2

    r32 = lax.broadcasted_iota(jnp.int32, (N_EXPERTS, tm), 0).astype(F32)
    is1 = r32 == ex1
    is2 = r32 == ex2
    member = jnp.where(is1 | is2, 1.0, 0.0)
    upper = jnp.where(lax.broadcasted_iota(jnp.int32, (tm, tm), 0)
                      <= lax.broadcasted_iota(jnp.int32, (tm, tm), 1), 1.0, 0.0).astype(BF16)
    incl = _dot(member.astype(BF16), upper)
    run = jnp.floor((jnp.sum(member, axis=1, keepdims=True) + (RUN_ALIGN - 1)) / RUN_ALIGN) * RUN_ALIGN
    rcol = lax.broadcasted_iota(jnp.int32, (N_EXPERTS, 1), 0)
    run_start = jnp.zeros((N_EXPERTS, 1), F32)
    for e in range(N_EXPERTS - 1):
        run_start = run_start + jnp.where(rcol > e, run[e:e + 1, :], 0.0)
    pos = incl - 1.0 + run_start
    loc_ref[0:1, :] = jnp.sum(jnp.where(is1, pos, 0.0), axis=0, keepdims=True).astype(jnp.int32)
    loc_ref[1:2, :] = jnp.sum(jnp.where(is2, pos, 0.0), axis=0, keepdims=True).astype(jnp.int32)
    carry = carry_ref[...]
    total = carry + run
    cnt_tab_ref[...] = jnp.broadcast_to(run, (N_EXPERTS, LANES)).astype(jnp.int32)
    carry_tab_ref[...] = carry.astype(jnp.int32)
    carry_ref[...] = total
    counts_ref[...] = total.astype(jnp.int32)


def _merge(x2, ypool, ymla, xq, gates, kmem, vmem, wbp, wbm, wbx, wout, ffng, wr, br, *, B, S, tm, mem_len):
    T = B * S
    tps = S // tm
    row = lambda i: (i, 0)
    lane = lambda i: (0, i)
    memb = lambda i: (i // tps, 0)
    return pl.pallas_call(
        functools.partial(_merge_body, tm=tm),
        grid=(T // tm,),
        in_specs=[
            pl.BlockSpec((tm, D_MODEL), row),
            pl.BlockSpec((tm, POOL_DIM), row),
            pl.BlockSpec((tm, MLA_HEADS * V_HEAD_DIM), row),
            pl.BlockSpec((tm, XATTN_DIM), row),
            pl.BlockSpec((tm, N_BRANCHES * D_MODEL), row),
            pl.BlockSpec((mem_len, XATTN_DIM), memb),
            pl.BlockSpec((mem_len, XATTN_DIM), memb),
            _const_spec((POOL_DIM, D_MODEL)),
            _const_spec((MLA_HEADS * V_HEAD_DIM, D_MODEL)),
            _const_spec((XATTN_DIM, D_MODEL)),
            _const_spec((D_MODEL, D_MODEL)),
            _const_spec((1, D_MODEL)),
            _const_spec((ROUTER_ROWS, D_MODEL)),
            _const_spec((ROUTER_ROWS, 1)),
        ],
        out_specs=[
            pl.BlockSpec((tm, D_MODEL), row),
            pl.BlockSpec((tm, D_MODEL), row),
            pl.BlockSpec((2, tm), lane),
            pl.BlockSpec((2, tm), lane),
            pl.BlockSpec((N_EXPERTS, LANES), row),
            pl.BlockSpec((N_EXPERTS, LANES), row),
            pl.BlockSpec((N_EXPERTS, LANES), lambda i: (0, 0)),
        ],
        out_shape=[
            jax.ShapeDtypeStruct((T, D_MODEL), F32),
            jax.ShapeDtypeStruct((T, D_MODEL), BF16),
            jax.ShapeDtypeStruct((2, T), F32),
            jax.ShapeDtypeStruct((2, T), jnp.int32),
            jax.ShapeDtypeStruct((T // tm * N_EXPERTS, LANES), jnp.int32),
            jax.ShapeDtypeStruct((T // tm * N_EXPERTS, LANES), jnp.int32),
            jax.ShapeDtypeStruct((N_EXPERTS, LANES), jnp.int32),
        ],
        scratch_shapes=[pltpu.VMEM((N_EXPERTS, LANES), F32)],
        compiler_params=pltpu.CompilerParams(dimension_semantics=("arbitrary",),
                                             vmem_limit_bytes=VMEM_LIMIT_BYTES),
        name="merge",
    )(x2, ypool, ymla, xq, gates, kmem, vmem, wbp, wbm, wbx, wout, ffng, wr, br)


def _moe_pos_body(counts_ref, meta_ref):
    shift = int(math.log2(MOE_ROWS))
    cnt = counts_ref[...]
    padded = lax.shift_left(lax.shift_right_logical(cnt + (MOE_ROWS - 1), shift), shift)
    r32 = lax.broadcasted_iota(jnp.int32, (N_EXPERTS, LANES), 0)
    pad_start = jnp.zeros((N_EXPERTS, LANES), jnp.int32)
    for e in range(N_EXPERTS - 1):
        pad_start = pad_start + jnp.where(r32 > e, padded[e:e + 1, :], 0)
    pad_end = pad_start + padded

    lane = lax.broadcasted_iota(jnp.int32, (1, META_LANES), 1)
    block_row = lane * MOE_ROWS
    blk_e = jnp.zeros((1, META_LANES), jnp.int32)
    pe_row = jnp.zeros((1, META_LANES), jnp.int32)
    for e in range(N_EXPERTS):
        pe = pad_end[e:e + 1, 0:1]
        blk_e = blk_e + jnp.where(pe <= block_row, 1, 0)
        pe_row = pe_row + jnp.where(lane == META_PAD_END + e, pe, 0)
    blk_e = jnp.minimum(blk_e, N_EXPERTS - 1)
    nact = lax.shift_right_logical(pad_end[N_EXPERTS - 1:N_EXPERTS, 0:1], shift)
    meta = jnp.where(lane < META_PAD_END, blk_e, pe_row)
    meta_ref[...] = jnp.where(lane == META_NACT, nact, meta)


def _moe_pos(counts):
    full = lambda shape: pl.BlockSpec(shape, lambda i: (0,) * len(shape))
    return pl.pallas_call(
        _moe_pos_body,
        grid=(1,),
        in_specs=[full((N_EXPERTS, LANES))],
        out_specs=full((1, META_LANES)),
        out_shape=jax.ShapeDtypeStruct((1, META_LANES), jnp.int32),
        compiler_params=pltpu.CompilerParams(dimension_semantics=("arbitrary",)),
        name="moe_pos",
    )(counts)


def _pack_bf16_pairs(x):
    lo = pltpu.bitcast(x[:, 0:PACKED_DIM], U32)
    hi = pltpu.bitcast(x[:, PACKED_DIM:D_MODEL], U32)
    return hi | lax.shift_right_logical(lo, jnp.uint32(16))


def _unpack_bf16_pairs(w):
    lo = pltpu.bitcast(lax.shift_left(w, jnp.uint32(16)), F32)
    hi = pltpu.bitcast(w & jnp.uint32(0xFFFF0000), F32)
    return jnp.concatenate([lo, hi], axis=1)


def _loc_rows(tm):
    bf16_rows = 2 * RUN_ALIGN
    return pl.cdiv(2 * tm + N_EXPERTS * (RUN_ALIGN - 1), bf16_rows) * bf16_rows


def _run_copies(tile, cnt_ref, carry_ref, meta_ref, make_copy):
    def per_expert(e, local, priority):
        n = pl.multiple_of(cnt_ref[tile * N_EXPERTS + e], RUN_ALIGN)
        start = jnp.where(e == 0, 0, meta_ref[META_PAD_END + jnp.maximum(e - 1, 0)])
        glob = pl.multiple_of(start + carry_ref[tile * N_EXPERTS + e], RUN_ALIGN)

        @pl.when(n > 0)
        def _():
            make_copy(pl.multiple_of(local, RUN_ALIGN), glob, n).start(priority=priority)

        return local + n

    def expert_pair(i, local):
        return per_expert(2 * i + 1, per_expert(2 * i, local, 0), 1)

    return pl.multiple_of(lax.fori_loop(0, N_EXPERTS // 2, expert_pair, 0), RUN_ALIGN)


def _tile_rows(tile, cnt_ref):
    total = lax.fori_loop(0, N_EXPERTS, lambda e, t: t + cnt_ref[tile * N_EXPERTS + e], 0)
    return pl.multiple_of(total, RUN_ALIGN)


def _dispatch_body(meta_ref, cnt_ref, carry_ref, loc_ref, h2_ref, xs_ref, xloc_ref, zero_ref, sems, zsem,
                   *, tm, n_blocks):
    tile = pl.program_id(0)
    last_tile = pl.num_programs(0) - 1
    slot = lax.rem(tile, 2)
    nact = meta_ref[META_NACT]

    def wait_rows(t, s):
        n = _tile_rows(t, cnt_ref)
        pltpu.make_async_copy(xloc_ref.at[s, pl.ds(0, n)], xs_ref.at[pl.ds(0, n)], sems.at[s]).wait()

    def pad_copy(e):
        end = pl.multiple_of(meta_ref[META_PAD_END + e], MOE_ROWS)
        start = jnp.where(e == 0, 0, meta_ref[META_PAD_END + jnp.maximum(e - 1, 0)])
        used = carry_ref[last_tile * N_EXPERTS + e] + cnt_ref[last_tile * N_EXPERTS + e]
        first = pl.multiple_of(start + used, RUN_ALIGN)
        n = pl.multiple_of(end - first, RUN_ALIGN)
        return n, pltpu.make_async_copy(zero_ref.at[pl.ds(0, n)], xs_ref.at[pl.ds(first, n)], zsem)

    def tail_copy(b):
        return pltpu.make_async_copy(
            zero_ref, xs_ref.at[pl.ds(pl.multiple_of(b * MOE_ROWS, MOE_ROWS), MOE_ROWS)], zsem)

    def fill(op):
        def pad(e, c):
            n, cp = pad_copy(e)

            @pl.when(n > 0)
            def _():
                op(cp)
            return c

        def tail(b, c):
            op(tail_copy(b))
            return c

        lax.fori_loop(0, N_EXPERTS, pad, 0)
        lax.fori_loop(nact, n_blocks, tail, 0)

    @pl.when(tile == 0)
    def _():
        zero_ref[...] = jnp.zeros((MOE_ROWS, PACKED_DIM), U32)
        fill(lambda cp: cp.start())

    @pl.when(tile >= 2)
    def _():
        wait_rows(tile - 2, slot)

    r = lax.broadcasted_iota(jnp.int32, (_loc_rows(tm), tm), 0)
    onehot = jnp.where((r == loc_ref[0:1, :]) | (r == loc_ref[1:2, :]), 1.0, 0.0).astype(BF16)
    xloc_ref[slot] = _pack_bf16_pairs(_dot(onehot, h2_ref[...]))

    def make_copy(local, glob, n):
        return pltpu.make_async_copy(xloc_ref.at[slot, pl.ds(local, n)], xs_ref.at[pl.ds(glob, n)],
                                     sems.at[slot])

    _run_copies(tile, cnt_ref, carry_ref, meta_ref, make_copy)

    @pl.when(tile == last_tile)
    def _():
        @pl.when(tile >= 1)
        def _():
            wait_rows(tile - 1, 1 - slot)

        wait_rows(tile, slot)
        fill(lambda cp: cp.wait())


def _dispatch(meta1, cnt_tab, carry_tab, loc, h2, *, R, tm):
    T = h2.shape[0]
    return pl.pallas_call(
        functools.partial(_dispatch_body, tm=tm, n_blocks=R // MOE_ROWS),
        grid_spec=pltpu.PrefetchScalarGridSpec(
            num_scalar_prefetch=3,
            grid=(T // tm,),
            in_specs=[pl.BlockSpec((2, tm), lambda i, *_: (0, i)),
                      pl.BlockSpec((tm, D_MODEL), lambda i, *_: (i, 0))],
            out_specs=pl.BlockSpec(memory_space=pl.ANY),
            scratch_shapes=[pltpu.VMEM((2, _loc_rows(tm), PACKED_DIM), U32),
                            pltpu.VMEM((MOE_ROWS, PACKED_DIM), U32),
                            pltpu.SemaphoreType.DMA((2,)), pltpu.SemaphoreType.DMA],
        ),
        out_shape=jax.ShapeDtypeStruct((R, PACKED_DIM), U32),
        compiler_params=pltpu.CompilerParams(dimension_semantics=("arbitrary",),
                                             vmem_limit_bytes=VMEM_LIMIT_BYTES),
        name="dispatch",
    )(meta1, cnt_tab, carry_tab, loc, h2)


def _moe_ffn_body(meta_ref, xs_ref, wg_hbm, wu_hbm, wd_hbm, ys_ref,
                  xbuf, ybuf, wg_stage, wu_stage, wd_stage, wg_b, wu_b, wd_b, zero_ref,
                  xsem, ysem, wsem, zsem, *, n_blocks):
    nact = meta_ref[META_NACT]
    shift = int(math.log2(MOE_ROWS))

    def rows_of(b):
        return pl.ds(pl.multiple_of(b * MOE_ROWS, MOE_ROWS), MOE_ROWS)

    def x_copy(b, s):
        return pltpu.make_async_copy(xs_ref.at[rows_of(b)], xbuf.at[s], xsem.at[s])

    def y_copy(b, s):
        return pltpu.make_async_copy(ybuf.at[s], ys_ref.at[rows_of(b)], ysem.at[s])

    def w_copies(e, s):
        return (pltpu.make_async_copy(wg_hbm.at[e], wg_stage.at[s], wsem.at[s]),
                pltpu.make_async_copy(wu_hbm.at[e], wu_stage.at[s], wsem.at[s]),
                pltpu.make_async_copy(wd_hbm.at[e], wd_stage.at[s], wsem.at[s]))

    def tail_copy(b):
        return pltpu.make_async_copy(zero_ref, ys_ref.at[rows_of(b)], zsem)

    zero_ref[...] = jnp.zeros((MOE_ROWS, PACKED_DIM), U32)
    lax.fori_loop(nact, n_blocks, lambda b, c: (tail_copy(b).start(), c)[1], 0)

    def next_expert_block(e):
        return lax.shift_right_logical(meta_ref[META_PAD_END + e], shift)

    def start_weights(b, s):
        @pl.when(b < nact)
        def _():
            for cp in w_copies(meta_ref[jnp.minimum(b, n_blocks - 1)], s):
                cp.start(priority=1)

    x_copy(0, 0).start()

    @pl.when(nact > 1)
    def _():
        x_copy(1, 1).start()

    e_first = meta_ref[0]
    start_weights(0, 0)
    start_weights(next_expert_block(e_first), 1)

    def block(b, xs, ys, k_prev):
        valid = b < nact
        e = meta_ref[jnp.minimum(b, nact - 1)]
        changed = jnp.logical_and(valid, jnp.logical_or(b == 0, e != meta_ref[jnp.maximum(b - 1, 0)]))
        k = jnp.where(changed, k_prev + 1, k_prev)

        @pl.when(changed)
        def _():
            ws = lax.rem(k, W_STAGES)
            for cp in w_copies(e, ws):
                cp.wait()
            wg_b[...] = wg_stage[ws].astype(BF16)
            wu_b[...] = wu_stage[ws].astype(BF16)
            wd_b[...] = wd_stage[ws].astype(BF16)
            n1 = next_expert_block(e)
            e1 = meta_ref[jnp.minimum(n1, n_blocks - 1)]
            n2 = jnp.where(n1 < nact, next_expert_block(e1), n_blocks)
            start_weights(n2, lax.rem(k + 2, W_STAGES))

        @pl.when(valid)
        def _():
            x_copy(b, xs).wait()

            @pl.when(b + 2 < nact)
            def _():
                x_copy(b + 2, (xs + 2) % X_SLOTS).start()

            @pl.when(b >= 2)
            def _():
                y_copy(b - 2, ys).wait()

            x = _unpack_bf16_pairs(xbuf[xs]).astype(BF16)
            g = _dot(x, wg_b[...])
            a = (g * jax.nn.sigmoid(g)) * _dot(x, wu_b[...])
            y = _dot(a.astype(BF16), wd_b[...])
            ybuf[ys] = _pack_bf16_pairs(y.astype(BF16).astype(F32))
            y_copy(b, ys).start()

        return k

    def quad(i, k):
        for j in range(X_SLOTS):
            k = block(X_SLOTS * i + j, j, j % 2, k)
        return k

    lax.fori_loop(0, lax.div(nact + (X_SLOTS - 1), X_SLOTS), quad, -1)

    @pl.when(nact >= 2)
    def _():
        y_copy(nact - 2, lax.rem(nact, 2)).wait()

    y_copy(nact - 1, lax.rem(nact - 1, 2)).wait()
    lax.fori_loop(nact, n_blocks, lambda b, c: (tail_copy(b).wait(), c)[1], 0)


def _moe_ffn(meta1, xs, wg, wu, wd):
    R = xs.shape[0]
    hbm = pl.BlockSpec(memory_space=pl.ANY)
    return pl.pallas_call(
        functools.partial(_moe_ffn_body, n_blocks=R // MOE_ROWS),
        grid_spec=pltpu.PrefetchScalarGridSpec(
            num_scalar_prefetch=1,
            grid=(1,),
            in_specs=[hbm, hbm, hbm, hbm],
            out_specs=hbm,
            scratch_shapes=[pltpu.VMEM((X_SLOTS, MOE_ROWS, PACKED_DIM), U32),
                            pltpu.VMEM((2, MOE_ROWS, PACKED_DIM), U32),
                            pltpu.VMEM((W_STAGES, D_MODEL, D_EXPERT), F32),
                            pltpu.VMEM((W_STAGES, D_MODEL, D_EXPERT), F32),
                            pltpu.VMEM((W_STAGES, D_EXPERT, D_MODEL), F32),
                            pltpu.VMEM((D_MODEL, D_EXPERT), BF16),
                            pltpu.VMEM((D_MODEL, D_EXPERT), BF16),
                            pltpu.VMEM((D_EXPERT, D_MODEL), BF16),
                            pltpu.VMEM((MOE_ROWS, PACKED_DIM), U32),
                            pltpu.SemaphoreType.DMA((X_SLOTS,)), pltpu.SemaphoreType.DMA((2,)),
                            pltpu.SemaphoreType.DMA((W_STAGES,)), pltpu.SemaphoreType.DMA],
        ),
        out_shape=jax.ShapeDtypeStruct((R, PACKED_DIM), U32),
        compiler_params=pltpu.CompilerParams(dimension_semantics=("arbitrary",),
                                             vmem_limit_bytes=VMEM_LIMIT_BYTES),
        name="moe_ffn",
    )(meta1, xs, wg, wu, wd)


def _combine_body(meta_ref, cnt_ref, carry_ref, x1_ref, loc_ref, wts_ref, fg_ref, ys_ref, out_ref,
                  yloc_ref, sems, *, tm):
    tile = pl.program_id(0)
    slot = lax.rem(tile, 2)

    def fetch(t, s):
        def make_copy(local, glob, n):
            return pltpu.make_async_copy(ys_ref.at[pl.ds(glob, n)], yloc_ref.at[s, pl.ds(local, n)],
                                         sems.at[s])
        _run_copies(t, cnt_ref, carry_ref, meta_ref, make_copy)

    @pl.when(tile == 0)
    def _():
        yloc_ref[...] = jnp.zeros(yloc_ref.shape, U32)
        fetch(tile, slot)

    @pl.when(tile + 1 < pl.num_programs(0))
    def _():
        fetch(tile + 1, 1 - slot)

    n = _tile_rows(tile, cnt_ref)
    pltpu.make_async_copy(ys_ref.at[pl.ds(0, n)], yloc_ref.at[slot, pl.ds(0, n)], sems.at[slot]).wait()
    r = lax.broadcasted_iota(jnp.int32, (_loc_rows(tm), tm), 0)
    is0 = r == loc_ref[0:1, :]
    is1 = r == loc_ref[1:2, :]
    row_w = jnp.sum(jnp.where(is0, wts_ref[0:1, :], 0.0) + jnp.where(is1, wts_ref[1:2, :], 0.0),
                    axis=1, keepdims=True)
    yw = (row_w * _unpack_bf16_pairs(yloc_ref[slot])).astype(BF16)
    twohot = jnp.where(is0 | is1, 1.0, 0.0).astype(BF16)
    moe = lax.dot_general(twohot, yw, (((0,), (0,)), ((), ())), preferred_element_type=F32)
    out_ref[...] = _rms(x1_ref[...] + moe, fg_ref[...])


def _combine(meta1, cnt_tab, carry_tab, x1, loc, wts, fg, ys, *, tm):
    T = x1.shape[0]
    row = lambda i, *_: (i, 0)
    lane = lambda i, *_: (0, i)
    return pl.pallas_call(
        functools.partial(_combine_body, tm=tm),
        grid_spec=pltpu.PrefetchScalarGridSpec(
            num_scalar_prefetch=3,
            grid=(T // tm,),
            in_specs=[pl.BlockSpec((tm, D_MODEL), row),
                      pl.BlockSpec((2, tm), lane),
                      pl.BlockSpec((2, tm), lane),
                      pl.BlockSpec((1, D_MODEL), lambda i, *_: (0, 0)),
                      pl.BlockSpec(memory_space=pl.ANY)],
            out_specs=pl.BlockSpec((tm, D_MODEL), row),
            scratch_shapes=[pltpu.VMEM((2, _loc_rows(tm), PACKED_DIM), U32),
                            pltpu.SemaphoreType.DMA((2,))],
        ),
        out_shape=jax.ShapeDtypeStruct((T, D_MODEL), F32),
        compiler_params=pltpu.CompilerParams(dimension_semantics=("arbitrary",),
                                             vmem_limit_bytes=VMEM_LIMIT_BYTES),
        name="combine",
    )(meta1, cnt_tab, carry_tab, x1, loc, wts, fg, ys)


def _tile(n, t):
    t = min(n, t)
    assert n % t == 0, (n, t)
    return t


def kernel(x, mem, positions, mix_norm_g, w_in, gate_b, q_norm_g, w_uq, kv_norm_g, w_uk, w_uv, pool_w, pool_scale, mem_norm_g, w_mem_kv, w_br_pool, w_br_mla, w_br_mem, w_out, ffn_norm_g, w_router_group, b_router_group, w_router_expert, b_router_expert, w_gate_e, w_up_e, w_down_e, final_norm_g):
    B, S, D = x.shape
    assert D == D_MODEL and mix_norm_g.shape[0] == 1
    T = B * S
    mem_len = mem.shape[1]
    tm = _tile(S, 512)
    l = 0

    win_p = _pack_w_in(jnp.swapaxes(w_in, 1, 2).reshape(W_IN_END, D_MODEL))
    wuq_p = jnp.pad(w_uq[l].reshape(Q_LORA_RANK, MLA_HEADS, QK_NOPE_DIM + QK_ROPE_DIM),
                    ((0, 0), (0, 0), (0, QK_PAD_DIM - QK_NOPE_DIM - QK_ROPE_DIM))
                    ).reshape(Q_LORA_RANK, MLA_HEADS * QK_PAD_DIM).astype(BF16)
    inv_freq = 1.0 / (ROPE_THETA ** (jnp.arange(0, QK_ROPE_DIM, 2, dtype=F32) / QK_ROPE_DIM))
    invf = jnp.concatenate([inv_freq, inv_freq, jnp.zeros((LANES - QK_ROPE_DIM,), F32)])[None, :]
    wr = jnp.concatenate([w_router_expert[l], w_router_group[l],
                          jnp.zeros((D_MODEL, ROUTER_ROWS - N_EXPERTS - N_GROUPS), F32)], axis=1).T.astype(BF16)
    br = jnp.concatenate([b_router_expert[l], b_router_group[l],
                          jnp.zeros((ROUTER_ROWS - N_EXPERTS - N_GROUPS,), F32)])[:, None].astype(F32)
    x2 = x.reshape(T, D_MODEL)
    pos2 = positions.reshape(T, 1)

    ypool, xq, gates, q, k, v = _mixer_in(
        x2, pos2, invf, mix_norm_g[l][None, :], win_p, gate_b[l], q_norm_g[l][None, :], wuq_p,
        kv_norm_g[l][None, :], w_uk[l].astype(BF16), w_uv[l].astype(BF16), pool_w[l].astype(BF16),
        pool_scale[l][None, :], B=B, S=S, tm=tm)
    kmem, vmem = _mem_kv(mem.reshape(B * mem_len, D_MODEL), mem_norm_g[l][None, :], w_mem_kv[l].astype(BF16))
    ymla = _mla_attn_unrolled(q, k, v, tq=tm).reshape(T, MLA_HEADS * V_HEAD_DIM)
    x1, h2, wts, loc, cnt_tab, carry_tab, counts = _merge(
        x2, ypool, ymla, xq, gates, kmem, vmem, w_br_pool[l].astype(BF16), w_br_mla[l].astype(BF16),
        w_br_mem[l].astype(BF16), w_out[l].astype(BF16), ffn_norm_g[l][None, :], wr, br,
        B=B, S=S, tm=tm, mem_len=mem_len)

    R = 2 * T + (T // tm) * N_EXPERTS * RUN_ALIGN + N_EXPERTS * MOE_ROWS
    assert R % MOE_ROWS == 0 and R // MOE_ROWS <= META_PAD_END
    meta1 = _moe_pos(counts).reshape(META_LANES)
    cnt1 = cnt_tab[:, 0]
    carry1 = carry_tab[:, 0]
    xs = _dispatch(meta1, cnt1, carry1, loc, h2, R=R, tm=tm)
    ys = _moe_ffn(meta1, xs, w_gate_e[l], w_up_e[l], w_down_e[l])
    out = _combine(meta1, cnt1, carry1, x1, loc, wts, final_norm_g[None, :], ys, tm=tm)
    return out.reshape(B, S, D_MODEL)
```
